```python
import math
import jax, jax.numpy as jnp
from jax import lax
import numpy as np

D_MODEL = 1024
BATCH = 8
SEQ = 16384
DEPTH = 1

HEAD_DIM = 64
ATT_GROUPS = ((128, 1), (512, 4), (2048, 16))
HEADS_PER_GROUP = 4
N_ATT_HEADS = HEADS_PER_GROUP * len(ATT_GROUPS)
ATT_WIDTH = N_ATT_HEADS * HEAD_DIM
ATT_MERGED = HEADS_PER_GROUP * HEAD_DIM
BLK = 128
POOL_WINDOWS = (2, 4, 8, 16)
POOL_GROUP_WIDTH = 3 * D_MODEL // 16
POOL_WIDTH = POOL_GROUP_WIDTH * len(POOL_WINDOWS)
D_FF = 4 * D_MODEL
N_IN = 3 * ATT_WIDTH + POOL_WIDTH + 2 * D_MODEL
NORM_EPS = 1e-6
ALIBI_MAX_BIAS = 8.0

kernel_name = "hybrid_dilated_attn_pool_gated_block"


def _rmsnorm(x, g):
    xf = x.astype(jnp.float32)
    y = xf * lax.rsqrt(jnp.mean(xf * xf, axis=-1, keepdims=True) + NORM_EPS)
    return (y * g.astype(jnp.float32)).astype(x.dtype)


def _dilated_window_attention(q, k, v, dilation, n_steps, slopes):
    B, S, H, Dh = q.shape
    L = S // dilation
    nb = -(-L // BLK)
    Lp = nb * BLK
    Z = B * dilation

    def to_sub(t):
        t = t.reshape(B, L, dilation, H, Dh).transpose(0, 2, 1, 3, 4).reshape(Z, L, H, Dh)
        return jnp.pad(t, ((0, 0), (0, Lp - L), (0, 0), (0, 0)))

    def band(t):
        prev = jnp.pad(t, ((0, 0), (BLK, 0), (0, 0), (0, 0)))[:, :Lp]
        return jnp.concatenate([prev.reshape(Z, nb, BLK, H, Dh),
                                t.reshape(Z, nb, BLK, H, Dh)], axis=2)

    qb = to_sub(q).reshape(Z, nb, BLK, H, Dh).astype(jnp.float32)
    kb = band(to_sub(k)).astype(jnp.float32)
    vb = band(to_sub(v)).astype(jnp.float32)

    s = jnp.einsum('znqhd,znkhd->znhqk', qb, kb) * (Dh ** -0.5)
    steps = BLK + jnp.arange(BLK)[:, None] - jnp.arange(2 * BLK)[None, :]
    key_idx = (jnp.arange(nb)[:, None, None] * BLK
               + jnp.arange(2 * BLK)[None, None, :] - BLK)
    valid = (steps >= 0) & (steps <= n_steps) & (key_idx >= 0)
    bias = -(slopes[:, None, None] * (steps * dilation).astype(jnp.float32))
    s = jnp.where(valid[None, :, None], s + bias[None, None], -jnp.inf)
    lse = jax.nn.logsumexp(s, axis=-1)
    p = jnp.exp(s - lse[..., None])
    o = jnp.einsum('znhqk,znkhd->znqhd', p, vb).reshape(Z, Lp, H, Dh)[:, :L]
    lse = lse.transpose(0, 1, 3, 2).reshape(Z, Lp, H)[:, :L]
    o = o.reshape(B, dilation, L, H, Dh).transpose(0, 2, 1, 3, 4).reshape(B, S, H, Dh)
    lse = lse.reshape(B, dilation, L, H).transpose(0, 2, 1, 3).reshape(B, S, H)
    return o, lse


def _attention_branch(q, k, v):
    B, S, _ = q.shape
    q = q.reshape(B, S, N_ATT_HEADS, HEAD_DIM)
    k = k.reshape(B, S, N_ATT_HEADS, HEAD_DIM)
    v = v.reshape(B, S, N_ATT_HEADS, HEAD_DIM)
    slopes = 2.0 ** (-ALIBI_MAX_BIAS * (jnp.arange(N_ATT_HEADS, dtype=jnp.float32) + 1.0)
                     / N_ATT_HEADS)
    outs, lses = [], []
    for g, (window, dilation) in enumerate(ATT_GROUPS):
        hs = slice(g * HEADS_PER_GROUP, (g + 1) * HEADS_PER_GROUP)
        o, l = _dilated_window_attention(q[:, :, hs], k[:, :, hs], v[:, :, hs],
                                         dilation, window // dilation, slopes[hs])
        outs.append(o)
        lses.append(l)
    outs = jnp.stack(outs, axis=0)
    wts = jax.nn.softmax(jnp.stack(lses, axis=0), axis=0)
    merged = jnp.sum(wts[..., None] * outs, axis=0)
    return merged.reshape(B, S, ATT_MERGED)


def _pool_branch(pz, w_grp, scale):
    B, S, _ = pz.shape
    pf = pz.astype(jnp.float32).reshape(B, S, len(POOL_WINDOWS), POOL_GROUP_WIDTH)
    c0 = jnp.pad(jnp.cumsum(pf, axis=1), ((0, 0), (1, 0), (0, 0), (0, 0)))
    t = jnp.arange(S)
    pooled = []
    for g, w in enumerate(POOL_WINDOWS):
        lower = jnp.take(c0[:, :, g], jnp.maximum(t + 1 - w, 0), axis=1)
        count = jnp.minimum(t + 1, w).astype(jnp.float32)[None, :, None]
        pooled.append((c0[:, 1:, g] - lower) / count)
    pooled = jnp.stack(pooled, axis=2) - pf
    mixed = jnp.einsum('bsgc,gcd->bsgd', pooled, w_grp.astype(jnp.float32))
    return mixed.reshape(B, S, POOL_WIDTH) * scale.astype(jnp.float32)


def _fwd_setup_inputs(seed: int = 0) -> dict:
    key = jax.random.key(seed)
    ks = jax.random.split(key, 12)
    f32 = jnp.float32

    def nrm(k, shape, fan_in):
        return jax.random.normal(k, shape, f32) * (fan_in ** -0.5)

    def gain(k, shape):
        return 1.0 + 0.02 * jax.random.normal(k, shape, f32)

    return {
        "x": jax.random.normal(ks[0], (BATCH, SEQ, D_MODEL), f32),
        "norm_mix_g": gain(ks[1], (DEPTH, D_MODEL)),
        "w_in": nrm(ks[2], (DEPTH, D_MODEL, N_IN), D_MODEL),
        "w_att_out": nrm(ks[3], (DEPTH, ATT_MERGED, D_MODEL), ATT_MERGED),
        "w_pool_grp": nrm(ks[4], (DEPTH, len(POOL_WINDOWS), POOL_GROUP_WIDTH, POOL_GROUP_WIDTH),
                          POOL_GROUP_WIDTH),
        "pool_scale": 1.0 + 0.1 * jax.random.normal(ks[5], (DEPTH, POOL_WIDTH), f32),
        "w_pool_out": nrm(ks[6], (DEPTH, POOL_WIDTH, D_MODEL), POOL_WIDTH),
        "w_out": nrm(ks[7], (DEPTH, D_MODEL, D_MODEL), D_MODEL),
        "norm_mlp_g": gain(ks[8], (DEPTH, D_MODEL)),
        "w_mlp_in": nrm(ks[9], (DEPTH, D_MODEL, D_FF), D_MODEL),
        "w_mlp_out": nrm(ks[10], (DEPTH, D_FF, D_MODEL), D_FF),
        "norm_final_g": gain(ks[11], (D_MODEL,)),
    }


def _fwd_reference(x, norm_mix_g, w_in, w_att_out, w_pool_grp, pool_scale, w_pool_out, w_out,
              norm_mlp_g, w_mlp_in, w_mlp_out, norm_final_g):
    dt = x.dtype
    offs = np.cumsum([ATT_WIDTH, ATT_WIDTH, ATT_WIDTH, POOL_WIDTH, D_MODEL]).tolist()
    h = x
    for l in range(DEPTH):
        u = _rmsnorm(h, norm_mix_g[l])
        z = jnp.einsum('bsd,dn->bsn', u, w_in[l])
        q, k, v, pz, ga, gp = jnp.split(z, offs, axis=-1)
        a = _attention_branch(q, k, v).astype(dt)
        p = _pool_branch(pz, w_pool_grp[l], pool_scale[l]).astype(dt)
        merged = (jax.nn.sigmoid(ga) * jnp.einsum('bsc,cd->bsd', a, w_att_out[l])
                  + jax.nn.sigmoid(gp) * jnp.einsum('bsc,cd->bsd', p, w_pool_out[l]))
        h = h + jnp.einsum('bsd,de->bse', merged, w_out[l])
        m = _rmsnorm(h, norm_mlp_g[l])
        hid = jnp.square(jax.nn.relu(jnp.einsum('bsd,df->bsf', m, w_mlp_in[l])))
        h = h + jnp.einsum('bsf,fd->bsd', hid, w_mlp_out[l])
    return _rmsnorm(h, norm_final_g)


import jax as _jax
import jax.numpy as _jnp

TWIN_FORMAT = 'train_step'
FWD_PARAMS = ['x', 'norm_mix_g', 'w_in', 'w_att_out', 'w_pool_grp', 'pool_scale', 'w_pool_out', 'w_out', 'norm_mlp_g', 'w_mlp_in', 'w_mlp_out', 'norm_final_g']
TWIN_WEIGHTS = ['norm_mix_g', 'w_in', 'w_att_out', 'w_pool_grp', 'pool_scale', 'w_pool_out', 'w_out', 'norm_mlp_g', 'w_mlp_in', 'w_mlp_out', 'norm_final_g']
TWIN_DIFF_INPUT = 'x'
TWIN_INPUTS = ['x', 'norm_mix_g', 'w_in', 'w_att_out', 'w_pool_grp', 'pool_scale', 'w_pool_out', 'w_out', 'norm_mlp_g', 'w_mlp_in', 'w_mlp_out', 'norm_final_g', 'loss_target', 'm_norm_mix_g', 'm_w_in', 'm_w_att_out', 'm_w_pool_grp', 'm_pool_scale', 'm_w_pool_out', 'm_w_out', 'm_norm_mlp_g', 'm_w_mlp_in', 'm_w_mlp_out', 'm_norm_final_g', 'v_norm_mix_g', 'v_w_in', 'v_w_att_out', 'v_w_pool_grp', 'v_pool_scale', 'v_w_pool_out', 'v_w_out', 'v_norm_mlp_g', 'v_w_mlp_in', 'v_w_mlp_out', 'v_norm_final_g']
TWIN_OUTPUTS = ['loss', 'grad_x', 'grad_norm_mix_g', 'grad_w_in', 'grad_w_att_out', 'grad_w_pool_grp', 'grad_pool_scale', 'grad_w_pool_out', 'grad_w_out', 'grad_norm_mlp_g', 'grad_w_mlp_in', 'grad_w_mlp_out', 'grad_norm_final_g', 'delta_norm_mix_g', 'delta_w_in', 'delta_w_att_out', 'delta_w_pool_grp', 'delta_pool_scale', 'delta_w_pool_out', 'delta_w_out', 'delta_norm_mlp_g', 'delta_w_mlp_in', 'delta_w_mlp_out', 'delta_norm_final_g', 'new_m_norm_mix_g', 'new_m_w_in', 'new_m_w_att_out', 'new_m_w_pool_grp', 'new_m_pool_scale', 'new_m_w_pool_out', 'new_m_w_out', 'new_m_norm_mlp_g', 'new_m_w_mlp_in', 'new_m_w_mlp_out', 'new_m_norm_final_g', 'new_v_norm_mix_g', 'new_v_w_in', 'new_v_w_att_out', 'new_v_w_pool_grp', 'new_v_pool_scale', 'new_v_w_pool_out', 'new_v_w_out', 'new_v_norm_mlp_g', 'new_v_w_mlp_in', 'new_v_w_mlp_out', 'new_v_norm_final_g']
TWIN_LEAF_KINDS = {'loss': 'loss', 'grad_x': 'grad_x', 'grad_norm_mix_g': 'grad_w', 'grad_w_in': 'grad_w', 'grad_w_att_out': 'grad_w', 'grad_w_pool_grp': 'grad_w', 'grad_pool_scale': 'grad_w', 'grad_w_pool_out': 'grad_w', 'grad_w_out': 'grad_w', 'grad_norm_mlp_g': 'grad_w', 'grad_w_mlp_in': 'grad_w', 'grad_w_mlp_out': 'grad_w', 'grad_norm_final_g': 'grad_w', 'delta_norm_mix_g': 'delta_w', 'delta_w_in': 'delta_w', 'delta_w_att_out': 'delta_w', 'delta_w_pool_grp': 'delta_w', 'delta_pool_scale': 'delta_w', 'delta_w_pool_out': 'delta_w', 'delta_w_out': 'delta_w', 'delta_norm_mlp_g': 'delta_w', 'delta_w_mlp_in': 'delta_w', 'delta_w_mlp_out': 'delta_w', 'delta_norm_final_g': 'delta_w', 'new_m_norm_mix_g': 'new_m', 'new_m_w_in': 'new_m', 'new_m_w_att_out': 'new_m', 'new_m_w_pool_grp': 'new_m', 'new_m_pool_scale': 'new_m', 'new_m_w_pool_out': 'new_m', 'new_m_w_out': 'new_m', 'new_m_norm_mlp_g': 'new_m', 'new_m_w_mlp_in': 'new_m', 'new_m_w_mlp_out': 'new_m', 'new_m_norm_final_g': 'new_m', 'new_v_norm_mix_g': 'new_v', 'new_v_w_in': 'new_v', 'new_v_w_att_out': 'new_v', 'new_v_w_pool_grp': 'new_v', 'new_v_pool_scale': 'new_v', 'new_v_w_pool_out': 'new_v', 'new_v_w_out': 'new_v', 'new_v_norm_mlp_g': 'new_v', 'new_v_w_mlp_in': 'new_v', 'new_v_w_mlp_out': 'new_v', 'new_v_norm_final_g': 'new_v'}


def _forward(args):
    return _fwd_reference(*[args[k] for k in FWD_PARAMS])


def _output_shape():
    def fwd():
        inp = _fwd_setup_inputs(0)
        return _fwd_reference(*[inp[k] for k in FWD_PARAMS])
    out = _jax.eval_shape(fwd)
    return out.shape, out.dtype

N_MICROBATCH = 1
ADAM_LR = 0.001
ADAM_B1 = 0.9
ADAM_B2 = 0.999
ADAM_EPS = 1e-08
ADAM_WD = 0.01
ADAM_STEP = 10
PER_EXAMPLE_BATCH_AXIS = {'x': 0, 'loss_target': 0}
SHARED_INPUTS = []
_WEIGHT_DTYPES = {'norm_mix_g': _jnp.float32, 'w_in': _jnp.float32, 'w_att_out': _jnp.float32, 'w_pool_grp': _jnp.float32, 'pool_scale': _jnp.float32, 'w_pool_out': _jnp.float32, 'w_out': _jnp.float32, 'norm_mlp_g': _jnp.float32, 'w_mlp_in': _jnp.float32, 'w_mlp_out': _jnp.float32, 'norm_final_g': _jnp.float32}
MOMENT_SCALE = {'norm_mix_g': 2.034695e-01, 'w_in': 9.003699e-02, 'w_att_out': 6.155901e-02, 'w_pool_grp': 1.959830e-01, 'pool_scale': 1.825879e-01, 'w_pool_out': 1.693228e-01, 'w_out': 1.745585e-01, 'norm_mlp_g': 3.028960e-01, 'w_mlp_in': 1.521899e-01, 'w_mlp_out': 3.411039e-01, 'norm_final_g': 1.291053e+02}


def _to_microbatches(a, axis):
    t = _jnp.moveaxis(a, axis, 0)
    t = t.reshape((N_MICROBATCH, t.shape[0] // N_MICROBATCH) + t.shape[1:])
    return _jnp.moveaxis(t, 1, axis + 1)


def setup_inputs(seed: int = 0) -> dict:
    inp = _fwd_setup_inputs(seed)
    key = _jax.random.fold_in(_jax.random.key(seed), 7919)
    shape, _ = _output_shape()
    out = dict(inp)
    out["loss_target"] = _jax.random.normal(_jax.random.fold_in(key, 0), shape, _jnp.float32)
    for i, name in enumerate(TWIN_WEIGHTS):
        w = inp[name].astype(_jnp.float32)
        if MOMENT_SCALE is None:
            s = _jnp.sqrt(_jnp.mean(_jnp.square(w)) + 1e-30)
        else:
            s = MOMENT_SCALE[name]
        km, kv = _jax.random.split(_jax.random.fold_in(key, i + 1))
        out[name] = w
        out["m_" + name] = s * _jax.random.normal(km, w.shape, _jnp.float32)
        out["v_" + name] = (s * s) * _jax.random.uniform(kv, w.shape, _jnp.float32, 0.5, 1.5)
    if N_MICROBATCH > 1:
        for name, axis in PER_EXAMPLE_BATCH_AXIS.items():
            out[name] = _to_microbatches(out[name], axis)
    return {'x': out['x'], 'norm_mix_g': out['norm_mix_g'], 'w_in': out['w_in'], 'w_att_out': out['w_att_out'], 'w_pool_grp': out['w_pool_grp'], 'pool_scale': out['pool_scale'], 'w_pool_out': out['w_pool_out'], 'w_out': out['w_out'], 'norm_mlp_g': out['norm_mlp_g'], 'w_mlp_in': out['w_mlp_in'], 'w_mlp_out': out['w_mlp_out'], 'norm_final_g': out['norm_final_g'], 'loss_target': out['loss_target'], 'm_norm_mix_g': out['m_norm_mix_g'], 'm_w_in': out['m_w_in'], 'm_w_att_out': out['m_w_att_out'], 'm_w_pool_grp': out['m_w_pool_grp'], 'm_pool_scale': out['m_pool_scale'], 'm_w_pool_out': out['m_w_pool_out'], 'm_w_out': out['m_w_out'], 'm_norm_mlp_g': out['m_norm_mlp_g'], 'm_w_mlp_in': out['m_w_mlp_in'], 'm_w_mlp_out': out['m_w_mlp_out'], 'm_norm_final_g': out['m_norm_final_g'], 'v_norm_mix_g': out['v_norm_mix_g'], 'v_w_in': out['v_w_in'], 'v_w_att_out': out['v_w_att_out'], 'v_w_pool_grp': out['v_w_pool_grp'], 'v_pool_scale': out['v_pool_scale'], 'v_w_pool_out': out['v_w_pool_out'], 'v_w_out': out['v_w_out'], 'v_norm_mlp_g': out['v_norm_mlp_g'], 'v_w_mlp_in': out['v_w_mlp_in'], 'v_w_mlp_out': out['v_w_mlp_out'], 'v_norm_final_g': out['v_norm_final_g']}


def _loss(weights, diff, rest, loss_target):
    with _jax.named_scope("forward"):
        args = {**rest, TWIN_DIFF_INPUT: diff, **{k: w.astype(_WEIGHT_DTYPES[k]) for k, w in weights.items()}}
        y = _forward(args)
    with _jax.named_scope("loss_head"):
        err = _jnp.square(y.astype(_jnp.float32) - loss_target)
        return 0.5 * _jnp.sum(_jnp.mean(err, axis=-1)) if err.ndim else 0.5 * err


def _adamw(w, g, m, v):
    m = ADAM_B1 * m + (1.0 - ADAM_B1) * g
    v = ADAM_B2 * v + (1.0 - ADAM_B2) * _jnp.square(g)
    m_hat = m / (1.0 - ADAM_B1 ** ADAM_STEP)
    v_hat = v / (1.0 - ADAM_B2 ** ADAM_STEP)
    delta = -ADAM_LR * (m_hat / (_jnp.sqrt(v_hat) + ADAM_EPS) + ADAM_WD * w)
    return delta, m, v


def reference(x, norm_mix_g, w_in, w_att_out, w_pool_grp, pool_scale, w_pool_out, w_out, norm_mlp_g, w_mlp_in, w_mlp_out, norm_final_g, loss_target, m_norm_mix_g, m_w_in, m_w_att_out, m_w_pool_grp, m_pool_scale, m_w_pool_out, m_w_out, m_norm_mlp_g, m_w_mlp_in, m_w_mlp_out, m_norm_final_g, v_norm_mix_g, v_w_in, v_w_att_out, v_w_pool_grp, v_pool_scale, v_w_pool_out, v_w_out, v_norm_mlp_g, v_w_mlp_in, v_w_mlp_out, v_norm_final_g):
    given = dict(x=x, norm_mix_g=norm_mix_g, w_in=w_in, w_att_out=w_att_out, w_pool_grp=w_pool_grp, pool_scale=pool_scale, w_pool_out=w_pool_out, w_out=w_out, norm_mlp_g=norm_mlp_g, w_mlp_in=w_mlp_in, w_mlp_out=w_mlp_out, norm_final_g=norm_final_g, loss_target=loss_target, m_norm_mix_g=m_norm_mix_g, m_w_in=m_w_in, m_w_att_out=m_w_att_out, m_w_pool_grp=m_w_pool_grp, m_pool_scale=m_pool_scale, m_w_pool_out=m_w_pool_out, m_w_out=m_w_out, m_norm_mlp_g=m_norm_mlp_g, m_w_mlp_in=m_w_mlp_in, m_w_mlp_out=m_w_mlp_out, m_norm_final_g=m_norm_final_g, v_norm_mix_g=v_norm_mix_g, v_w_in=v_w_in, v_w_att_out=v_w_att_out, v_w_pool_grp=v_w_pool_grp, v_pool_scale=v_pool_scale, v_w_pool_out=v_w_pool_out, v_w_out=v_w_out, v_norm_mlp_g=v_norm_mlp_g, v_w_mlp_in=v_w_mlp_in, v_w_mlp_out=v_w_mlp_out, v_norm_final_g=v_norm_final_g)
    weights = {n: given[n] for n in TWIN_WEIGHTS}
    shared = {n: given[n] for n in SHARED_INPUTS}
    per_example = {n: given[n] for n in ['x']}
    grad_fn = _jax.value_and_grad(_loss, argnums=(0, 1))

    def one_microbatch(ex, loss_target):
        ex = dict(ex)
        diff = ex.pop(TWIN_DIFF_INPUT)
        return grad_fn(weights, diff, {**shared, **ex}, loss_target)

    if N_MICROBATCH == 1:
        loss, (grad_w, grad_x) = one_microbatch(per_example, given["loss_target"])
    else:
        def body(carry, xs):
            loss_sum, grad_sum = carry
            l_k, (gw_k, gx_k) = one_microbatch(xs[0], xs[1])
            with _jax.named_scope("update"):
                return (loss_sum + l_k, _jax.tree.map(_jnp.add, grad_sum, gw_k)), gx_k

        init = (_jnp.zeros((), _jnp.float32), _jax.tree.map(_jnp.zeros_like, weights))
        (loss, grad_w), grad_x = _jax.lax.scan(body, init, (per_example, given["loss_target"]))
    with _jax.named_scope("update"):
        delta_w, new_m, new_v = {}, {}, {}
        for n in TWIN_WEIGHTS:
            delta_w[n], new_m[n], new_v[n] = _adamw(weights[n], grad_w[n], given["m_" + n], given["v_" + n])
    return (loss, grad_x, *[grad_w[n] for n in TWIN_WEIGHTS], *[delta_w[n] for n in TWIN_WEIGHTS],
            *[new_m[n] for n in TWIN_WEIGHTS], *[new_v[n] for n in TWIN_WEIGHTS])
```

```python
import functools
import math

import jax
import jax.numpy as jnp
from jax import lax
from jax.experimental import pallas as pl
from jax.experimental.pallas import tpu as pltpu

F32 = jnp.float32
BF16 = jnp.bfloat16

D_MODEL = 1024
HEAD_DIM = 64
GROUP_W = 256
ATT_GROUPS = ((128, 1), (512, 4), (2048, 16))
N_GROUPS = 3
BLK = 128
ATT_W = 768
POOL_W = 768
POOL_GW = 192
D_FF = 4096
N_IN = 5120
REST_W = N_IN - 3 * ATT_W
NORM_EPS = 1e-6
ALIBI_MAX_BIAS = 8.0
N_DEV = 8
HALO = 32

ADAM_LR = 0.001
ADAM_B1 = 0.9
ADAM_B2 = 0.999
ADAM_EPS = 1e-08
ADAM_WD = 0.01
ADAM_STEP = 10

VMEM_LIMIT = 56 * 1024 * 1024
ANY = pl.BlockSpec(memory_space=pl.ANY)


def _params(**kw):
    return pltpu.CompilerParams(vmem_limit_bytes=VMEM_LIMIT, **kw)


def _dot(a, b):
    return jnp.dot(a, b, preferred_element_type=F32)


def _dot_nt(a, b):
    return lax.dot_general(a, b, (((1,), (1,)), ((), ())), preferred_element_type=F32)


def _dot_tn(a, b):
    return lax.dot_general(a, b, (((0,), (0,)), ((), ())), preferred_element_type=F32)


def _slope(head):
    return 2.0 ** (-ALIBI_MAX_BIAS * (head + 1.0) / 12.0)


def _load_resident(step, pairs, sem):
    @pl.when(step == 0)
    def _():
        copies = [pltpu.make_async_copy(src, dst, sem.at[n]) for n, (src, dst) in enumerate(pairs)]
        for cp in copies:
            cp.start()
        for cp in copies:
            cp.wait()


LANES = 128


def _to_class_order(value, dil, buf, ref, col0):
    ts, w = value.shape
    if dil == 1:
        ref[:, col0:col0 + w] = value.astype(ref.dtype)
        return
    for c in range(w // LANES):
        buf[c] = value[:, c * LANES:(c + 1) * LANES]
        for r in range(dil):
            ref[r, :, col0 + c * LANES:col0 + (c + 1) * LANES] = (
                buf[c, pl.ds(r, ts // dil, stride=dil), :].astype(ref.dtype))


def _to_token_order(ref, dil, buf, ts):
    if dil == 1:
        return ref[...].astype(F32)
    w = ref.shape[-1]
    for c in range(w // LANES):
        for r in range(dil):
            buf[c, pl.ds(r, ts // dil, stride=dil), :] = ref[r, :, c * LANES:(c + 1) * LANES].astype(F32)
    return jnp.concatenate([buf[c] for c in range(w // LANES)], axis=1)


def _rms(x):
    return lax.rsqrt(jnp.mean(x * x, axis=-1, keepdims=True) + NORM_EPS)


def _rms_bwd(dn, n, r):
    return r * (dn - n * jnp.mean(dn * n, axis=-1, keepdims=True))


def _in_proj_fwd(x, g1, w_in, ts):
    s = x.shape[0]
    dils = [d for _, d in ATT_GROUPS]

    def body(x_ref, g_ref, w_hbm, ut_ref, q0_ref, q1_ref, q2_ref, zr_ref, w_ref, zbuf, sem):
        i = pl.program_id(0)
        _load_resident(i, [(w_hbm, w_ref)], sem)
        x = x_ref[...]
        u = x * _rms(x) * g_ref[...]
        ut_ref[...] = u.T.astype(BF16)
        ub = u.astype(BF16)
        outs = (q0_ref, q1_ref, q2_ref)
        for sec in range(3):
            for g in range(N_GROUPS):
                c0 = sec * ATT_W + g * GROUP_W
                zc = _dot(ub, w_ref[:, c0:c0 + GROUP_W])
                _to_class_order(zc, dils[g], zbuf, outs[g], sec * GROUP_W)
        for c0 in range(0, REST_W, 256):
            zr_ref[:, c0:c0 + 256] = _dot(ub, w_ref[:, 3 * ATT_W + c0:3 * ATT_W + c0 + 256]).astype(BF16)

    n = s // ts
    return pl.pallas_call(
        body, name="in_proj_fwd", grid=(n,),
        in_specs=[pl.BlockSpec((ts, D_MODEL), lambda i: (i, 0)),
                  pl.BlockSpec((1, D_MODEL), lambda i: (0, 0)), ANY],
        out_specs=[pl.BlockSpec((D_MODEL, ts), lambda i: (0, i)),
                   pl.BlockSpec((ts, ATT_W), lambda i: (i, 0)),
                   pl.BlockSpec((4, ts // 4, ATT_W), lambda i: (0, i, 0)),
                   pl.BlockSpec((16, ts // 16, ATT_W), lambda i: (0, i, 0)),
                   pl.BlockSpec((ts, REST_W), lambda i: (i, 0))],
        out_shape=[jax.ShapeDtypeStruct((D_MODEL, s), BF16),
                   jax.ShapeDtypeStruct((s, ATT_W), BF16),
                   jax.ShapeDtypeStruct((4, s // 4, ATT_W), BF16),
                   jax.ShapeDtypeStruct((16, s // 16, ATT_W), BF16),
                   jax.ShapeDtypeStruct((s, REST_W), BF16)],
        scratch_shapes=[pltpu.VMEM((D_MODEL, N_IN), BF16), pltpu.VMEM((GROUP_W // LANES, ts, LANES), F32),
                        pltpu.SemaphoreType.DMA((1,))],
        compiler_params=_params(dimension_semantics=("arbitrary",)),
    )(x, g1, w_in)


ATT_TILE = 4 * BLK


def _band_consts():
    qi = lax.broadcasted_iota(jnp.int32, (BLK, 2 * BLK), 0)
    kj = lax.broadcasted_iota(jnp.int32, (BLK, 2 * BLK), 1)
    steps = BLK + qi - kj
    lane = lax.broadcasted_iota(jnp.int32, (BLK, GROUP_W), 1)
    return steps, kj, lane // HEAD_DIM


def _attn_fwd(qkv, group):
    s = qkv.shape[0]
    dil = ATT_GROUPS[group][1]
    nbc = s // (BLK * dil)
    nb = ATT_TILE // BLK

    def body(q_ref, kc_ref, kp_ref, vc_ref, vp_ref, o_ref, l_ref, kbuf, vbuf):
        i = pl.program_id(0)
        kbuf[0:BLK] = kp_ref[...]
        kbuf[BLK:BLK + ATT_TILE] = kc_ref[...]
        vbuf[0:BLK] = vp_ref[...]
        vbuf[BLK:BLK + ATT_TILE] = vc_ref[...]
        steps, kj, head_of = _band_consts()
        in_band = (steps >= 0) & (steps <= BLK)
        dist = steps.astype(F32) * float(dil)
        for b in range(nb):
            first_key = jnp.where(((i * nb + b) % nbc) != 0, 0, BLK)
            valid = in_band & (kj >= first_key)
            qb = q_ref[b * BLK:(b + 1) * BLK, :]
            kb = kbuf[b * BLK:b * BLK + 2 * BLK, :]
            vb = vbuf[b * BLK:b * BLK + 2 * BLK, :]
            o_acc = jnp.zeros((BLK, GROUP_W), F32)
            l_acc = jnp.zeros((BLK, GROUP_W), F32)
            for h in range(4):
                hm = head_of == h
                qm = jnp.where(hm, qb, jnp.zeros_like(qb))
                sc = _dot_nt(qm, kb) * (HEAD_DIM ** -0.5)
                sc = jnp.where(valid, sc - _slope(4 * group + h) * dist, -jnp.inf)
                mx = jnp.max(sc, axis=1, keepdims=True)
                e = jnp.exp(sc - mx)
                den = jnp.sum(e, axis=1, keepdims=True)
                p = e * (1.0 / den)
                oh = _dot(p.astype(BF16), vb)
                o_acc = jnp.where(hm, oh, o_acc)
                l_acc = jnp.where(hm, mx + jnp.log(den), l_acc)
            o_ref[b * BLK:(b + 1) * BLK, :] = o_acc
            l_ref[b * BLK:(b + 1) * BLK, :] = l_acc

    n = s // ATT_TILE
    cur = lambda c: pl.BlockSpec((ATT_TILE, GROUP_W), lambda i: (i, c))
    prev = lambda c: pl.BlockSpec((BLK, GROUP_W), lambda i: (jnp.maximum(i * nb - 1, 0), c))
    return pl.pallas_call(
        body, name=f"attn_fwd_g{group}", grid=(n,),
        in_specs=[cur(0), cur(1), prev(1), cur(2), prev(2)],
        out_specs=[cur(0), cur(0)],
        out_shape=[jax.ShapeDtypeStruct((s, GROUP_W), F32), jax.ShapeDtypeStruct((s, GROUP_W), F32)],
        scratch_shapes=[pltpu.VMEM((BLK + ATT_TILE, GROUP_W), BF16), pltpu.VMEM((BLK + ATT_TILE, GROUP_W), BF16)],
        compiler_params=_params(dimension_semantics=("arbitrary",)),
    )(qkv, qkv, qkv, qkv, qkv)


def _attn_bwd(qkv, do, lse, corr, group):
    s = qkv.shape[0]
    dil = ATT_GROUPS[group][1]
    nbc = s // (BLK * dil)
    nb = ATT_TILE // BLK
    n = s // ATT_TILE

    def body(q_ref, kc_ref, kp_ref, vc_ref, vp_ref, do_ref, l_ref, c_ref, dq_ref, dk_ref, dv_ref,
             kbuf, vbuf, dkbuf, dvbuf, dkpend, dvpend):
        i = pl.program_id(0)

        @pl.when(i == 0)
        def _():
            dkpend[...] = jnp.zeros_like(dkpend)
            dvpend[...] = jnp.zeros_like(dvpend)

        dkbuf[...] = jnp.zeros_like(dkbuf)
        dvbuf[...] = jnp.zeros_like(dvbuf)

        @pl.when(i < n)
        def _():
            kbuf[0:BLK] = kp_ref[...]
            kbuf[BLK:BLK + ATT_TILE] = kc_ref[...]
            vbuf[0:BLK] = vp_ref[...]
            vbuf[BLK:BLK + ATT_TILE] = vc_ref[...]
            steps, kj, head_of = _band_consts()
            in_band = (steps >= 0) & (steps <= BLK)
            dist = steps.astype(F32) * float(dil)
            for b in range(nb):
                rows = slice(b * BLK, (b + 1) * BLK)
                band = slice(b * BLK, b * BLK + 2 * BLK)
                first_key = jnp.where(((i * nb + b) % nbc) != 0, 0, BLK)
                valid = in_band & (kj >= first_key)
                qb = q_ref[rows, :]
                dob = do_ref[rows, :]
                lb = l_ref[rows, :]
                cb = c_ref[rows, :]
                kb = kbuf[band, :]
                vb = vbuf[band, :]
                dq_acc = jnp.zeros((BLK, GROUP_W), F32)
                dk_acc = jnp.zeros((2 * BLK, GROUP_W), F32)
                dv_acc = jnp.zeros((2 * BLK, GROUP_W), F32)
                for h in range(4):
                    hm = head_of == h
                    qm = jnp.where(hm, qb, jnp.zeros_like(qb))
                    dom = jnp.where(hm, dob, jnp.zeros_like(dob))
                    lse_h = jnp.max(jnp.where(hm, lb, -jnp.inf), axis=1, keepdims=True)
                    cor_h = jnp.max(jnp.where(hm, cb, -jnp.inf), axis=1, keepdims=True)
                    sc = _dot_nt(qm, kb) * (HEAD_DIM ** -0.5)
                    sc = jnp.where(valid, sc - _slope(4 * group + h) * dist, -jnp.inf)
                    p = jnp.exp(sc - lse_h)
                    dp = _dot_nt(dom, vb)
                    ds = (p * (dp + cor_h) * (HEAD_DIM ** -0.5)).astype(BF16)
                    dq_acc = jnp.where(hm, _dot(ds, kb), dq_acc)
                    dk_acc = dk_acc + _dot_tn(ds, qm)
                    dv_acc = dv_acc + _dot_tn(p.astype(BF16), dom)
                dq_ref[rows, :] = dq_acc.astype(BF16)
                dkbuf[band, :] += dk_acc
                dvbuf[band, :] += dv_acc

        tail = slice(ATT_TILE - BLK, ATT_TILE)
        dkpend[tail, :] += dkbuf[0:BLK, :]
        dvpend[tail, :] += dvbuf[0:BLK, :]
        dk_ref[...] = dkpend[...].astype(BF16)
        dv_ref[...] = dvpend[...].astype(BF16)
        dkpend[...] = dkbuf[BLK:BLK + ATT_TILE, :]
        dvpend[...] = dvbuf[BLK:BLK + ATT_TILE, :]

    last = n - 1
    cur = lambda c: pl.BlockSpec((ATT_TILE, GROUP_W), lambda i: (jnp.minimum(i, last), c))
    prev = lambda c: pl.BlockSpec(
        (BLK, GROUP_W), lambda i: (jnp.maximum(jnp.minimum(i, last) * nb - 1, 0), c))
    late = lambda c: pl.BlockSpec((ATT_TILE, GROUP_W), lambda i: (jnp.maximum(i - 1, 0), c))
    dq, dk, dv = pl.pallas_call(
        body, name=f"attn_bwd_g{group}", grid=(n + 1,),
        in_specs=[cur(0), cur(1), prev(1), cur(2), prev(2), cur(0), cur(0), cur(0)],
        out_specs=[cur(0), late(0), late(0)],
        out_shape=[jax.ShapeDtypeStruct((s, GROUP_W), BF16)] * 3,
        scratch_shapes=[pltpu.VMEM((BLK + ATT_TILE, GROUP_W), BF16), pltpu.VMEM((BLK + ATT_TILE, GROUP_W), BF16),
                        pltpu.VMEM((BLK + ATT_TILE, GROUP_W), F32), pltpu.VMEM((BLK + ATT_TILE, GROUP_W), F32),
                        pltpu.VMEM((ATT_TILE, GROUP_W), F32), pltpu.VMEM((ATT_TILE, GROUP_W), F32)],
        compiler_params=_params(dimension_semantics=("arbitrary",)),
    )(qkv, qkv, qkv, qkv, qkv, do, lse, corr)
    return dq, dk, dv


def _dil_specs(ts, width, idx):
    return [pl.BlockSpec((ts, width), lambda i: (idx(i), 0)),
            pl.BlockSpec((4, ts // 4, width), lambda i: (0, idx(i), 0)),
            pl.BlockSpec((16, ts // 16, width), lambda i: (0, idx(i), 0))]


def _gather_rows(refs, buf, ts):
    return [_to_token_order(refs[g], ATT_GROUPS[g][1], buf, ts) for g in range(N_GROUPS)]


def _pool_fwd(ebuf, s2, s4, s8, t0, ts):
    n = ts + HALO
    s2[8:n] = ebuf[8:n] + ebuf[7:n - 1]
    s4[16:n] = s2[16:n] + s2[14:n - 2]
    s8[24:n] = s4[24:n] + s4[20:n - 4]
    s16 = s8[32:n] + s8[24:n - 8]
    col = lax.broadcasted_iota(jnp.int32, (ts, POOL_W), 1)
    psum = jnp.where(col < POOL_GW, s2[32:n],
                     jnp.where(col < 2 * POOL_GW, s4[32:n], jnp.where(col < 3 * POOL_GW, s8[32:n], s16)))
    win = jnp.where(col < POOL_GW, 2, jnp.where(col < 2 * POOL_GW, 4, jnp.where(col < 3 * POOL_GW, 8, 16)))
    t = t0 + lax.broadcasted_iota(jnp.int32, (ts, POOL_W), 0)
    count = jnp.minimum(t + 1, win).astype(F32)
    return psum / count - ebuf[32:n], count


def _mix_core(zr, pooled, outs, lses, wbd, scale, wao, wpo):
    mixed = _dot(pooled.astype(BF16), wbd)
    p = mixed * scale
    l0, l1, l2 = lses
    mx = jnp.maximum(jnp.maximum(l0, l1), l2)
    e0, e1, e2 = jnp.exp(l0 - mx), jnp.exp(l1 - mx), jnp.exp(l2 - mx)
    inv = 1.0 / (e0 + e1 + e2)
    wts = (e0 * inv, e1 * inv, e2 * inv)
    a = wts[0] * outs[0] + wts[1] * outs[1] + wts[2] * outs[2]
    att = _dot(a.astype(BF16), wao)
    pol = _dot(p.astype(BF16), wpo)
    sga = jax.nn.sigmoid(zr[:, POOL_W:POOL_W + D_MODEL].astype(F32))
    sgp = jax.nn.sigmoid(zr[:, POOL_W + D_MODEL:].astype(F32))
    mg = sga * att + sgp * pol
    return dict(mixed=mixed, p=p, wts=wts, a=a, att=att, pol=pol, sga=sga, sgp=sgp, mg=mg)


def _fill_pool_input(ebuf, zr_ref, halo_ref, t0):
    ts = zr_ref.shape[0]
    halo = halo_ref[...].astype(F32)
    t = t0 - HALO + lax.broadcasted_iota(jnp.int32, (HALO, POOL_W), 0)
    ebuf[0:HALO] = jnp.where(t >= 0, halo, 0.0)
    ebuf[HALO:HALO + ts] = zr_ref[:, 0:POOL_W].astype(F32)


def _mix_fwd(x, zr, o_dil, l_dil, wbd, scale, wao, wpo, wout, ts):
    s = x.shape[0]
    n = s // ts

    def body(x_ref, zr_ref, halo_ref, o0, o1, o2, l0, l1, l2, wbd_ref, sc_ref, wao_ref, wpo_ref, wout_ref,
             h1_ref, ebuf, s2, s4, s8, rbuf):
        i = pl.program_id(0)
        _fill_pool_input(ebuf, zr_ref, halo_ref, i * ts)
        pooled, _ = _pool_fwd(ebuf, s2, s4, s8, i * ts, ts)
        outs = _gather_rows((o0, o1, o2), rbuf, ts)
        lses = _gather_rows((l0, l1, l2), rbuf, ts)
        f = _mix_core(zr_ref[...], pooled, outs, lses, wbd_ref[...], sc_ref[...], wao_ref[...], wpo_ref[...])
        h1_ref[...] = x_ref[...] + _dot(f["mg"].astype(BF16), wout_ref[...])

    whole = lambda a: pl.BlockSpec(a.shape, lambda i: (0,) * a.ndim)
    idx = lambda i: i
    return pl.pallas_call(
        body, name="mix_fwd", grid=(n,),
        in_specs=[pl.BlockSpec((ts, D_MODEL), lambda i: (i, 0)),
                  pl.BlockSpec((ts, REST_W), lambda i: (i, 0)),
                  pl.BlockSpec((HALO, POOL_W), lambda i: (jnp.maximum(i * (ts // HALO) - 1, 0), 0))]
                 + _dil_specs(ts, GROUP_W, idx) + _dil_specs(ts, GROUP_W, idx)
                 + [whole(wbd), whole(scale), whole(wao), whole(wpo), whole(wout)],
        out_specs=pl.BlockSpec((ts, D_MODEL), lambda i: (i, 0)),
        out_shape=jax.ShapeDtypeStruct((s, D_MODEL), F32),
        scratch_shapes=[pltpu.VMEM((ts + HALO, POOL_W), F32)] * 4
                       + [pltpu.VMEM((GROUP_W // LANES, ts, LANES), F32)],
        compiler_params=_params(dimension_semantics=("arbitrary",)),
    )(x, zr, zr, *o_dil, *l_dil, wbd, scale, wao, wpo, wout)


def _mix_bwd(dh1, zr, o_dil, l_dil, wbd, scale, wao, wpo, wout, ts):
    s = dh1.shape[0]
    n = s // ts

    def body(dh_ref, zr_ref, halo_ref, o0, o1, o2, l0, l1, l2, sc_ref, wbd_hbm, wao_hbm, wpo_hbm, wout_hbm,
             dzr_ref, do0, do1, do2, c0, c1, c2, gsc_ref, gwout_hbm, gwao_hbm, gwpo_hbm, gwbd_hbm,
             ebuf, s2, s4, s8, gbuf, t2, t4, t8, rbuf,
             wbd_ref, wao_ref, wpo_ref, wout_ref, gwout_ref, gwao_ref, gwpo_ref, gwbd_ref, sem):
        j = pl.program_id(0)
        i = n - 1 - j
        _load_resident(j, [(wbd_hbm, wbd_ref), (wao_hbm, wao_ref), (wpo_hbm, wpo_ref), (wout_hbm, wout_ref)], sem)

        @pl.when(j == 0)
        def _():
            gwout_ref[...] = jnp.zeros_like(gwout_ref)
            gwao_ref[...] = jnp.zeros_like(gwao_ref)
            gwpo_ref[...] = jnp.zeros_like(gwpo_ref)
            gwbd_ref[...] = jnp.zeros_like(gwbd_ref)
            gsc_ref[...] = jnp.zeros_like(gsc_ref)
            gbuf[ts:ts + HALO] = jnp.zeros((HALO, POOL_W), F32)

        _fill_pool_input(ebuf, zr_ref, halo_ref, i * ts)
        pooled, count = _pool_fwd(ebuf, s2, s4, s8, i * ts, ts)
        outs = _gather_rows((o0, o1, o2), rbuf, ts)
        lses = _gather_rows((l0, l1, l2), rbuf, ts)
        zr = zr_ref[...]
        wbd, wao, wpo, wout = wbd_ref[...], wao_ref[...], wpo_ref[...], wout_ref[...]
        scale = sc_ref[...]
        f = _mix_core(zr, pooled, outs, lses, wbd, scale, wao, wpo)

        dhb = dh_ref[...].astype(BF16)
        gwout_ref[...] += _dot(f["mg"].T.astype(BF16), dhb)
        dmg = _dot_nt(dhb, wout)
        sga, sgp, att, pol = f["sga"], f["sgp"], f["att"], f["pol"]
        datt = dmg * sga
        dpol = dmg * sgp
        dzr_ref[:, POOL_W:POOL_W + D_MODEL] = (dmg * att * sga * (1.0 - sga)).astype(BF16)
        dzr_ref[:, POOL_W + D_MODEL:] = (dmg * pol * sgp * (1.0 - sgp)).astype(BF16)
        dattb = datt.astype(BF16)
        dpolb = dpol.astype(BF16)
        gwao_ref[...] += _dot(f["a"].T.astype(BF16), dattb)
        gwpo_ref[...] += _dot(f["p"].T.astype(BF16), dpolb)
        da = _dot_nt(dattb, wao)
        dp = _dot_nt(dpolb, wpo)

        gsc_ref[...] += jnp.sum(f["mixed"] * dp, axis=0, keepdims=True)
        dmixed = (dp * scale).astype(BF16)
        gwbd_ref[...] += _dot(pooled.T.astype(BF16), dmixed)
        dpooled = _dot_nt(dmixed, wbd)
        gbuf[0:ts] = dpooled / count
        m = ts + HALO
        t2[0:m - 8] = gbuf[0:m - 8] + gbuf[1:m - 7]
        t4[0:m - 16] = t2[0:m - 16] + t2[2:m - 14]
        t8[0:m - 24] = t4[0:m - 24] + t4[4:m - 20]
        t16 = t8[0:ts] + t8[8:ts + 8]
        col = lax.broadcasted_iota(jnp.int32, (ts, POOL_W), 1)
        back = jnp.where(col < POOL_GW, t2[0:ts],
                         jnp.where(col < 2 * POOL_GW, t4[0:ts], jnp.where(col < 3 * POOL_GW, t8[0:ts], t16)))
        dzr_ref[:, 0:POOL_W] = (back - dpooled).astype(BF16)
        gbuf[ts:ts + HALO] = gbuf[0:HALO]

        head_of = lax.broadcasted_iota(jnp.int32, (ts, GROUP_W), 1) // HEAD_DIM
        prod = da * f["a"]
        inner = jnp.zeros((ts, GROUP_W), F32)
        for h in range(4):
            hm = head_of == h
            tot = jnp.sum(jnp.where(hm, prod, 0.0), axis=1, keepdims=True)
            inner = jnp.where(hm, tot, inner)
        for g, (do_ref, c_ref) in enumerate(((do0, c0), (do1, c1), (do2, c2))):
            dil = ATT_GROUPS[g][1]
            _to_class_order(f["wts"][g] * da, dil, rbuf, do_ref, 0)
            _to_class_order(-f["wts"][g] * inner, dil, rbuf, c_ref, 0)

        @pl.when(j == n - 1)
        def _():
            pairs = ((gwout_ref, gwout_hbm), (gwao_ref, gwao_hbm), (gwpo_ref, gwpo_hbm), (gwbd_ref, gwbd_hbm))
            copies = [pltpu.make_async_copy(src, dst, sem.at[k]) for k, (src, dst) in enumerate(pairs)]
            for cp in copies:
                cp.start()
            for cp in copies:
                cp.wait()

    idx = lambda j: n - 1 - j
    do_shapes = [jax.ShapeDtypeStruct((s, GROUP_W), BF16), jax.ShapeDtypeStruct((4, s // 4, GROUP_W), BF16),
                 jax.ShapeDtypeStruct((16, s // 16, GROUP_W), BF16)]
    c_shapes = [jax.ShapeDtypeStruct(a.shape, F32) for a in do_shapes]
    weights = (wbd, wao, wpo, wout)
    grad_shapes = [(D_MODEL, D_MODEL), (GROUP_W, D_MODEL), (POOL_W, D_MODEL), (POOL_W, POOL_W)]
    tile_buf = pltpu.VMEM((ts + HALO, POOL_W), F32)
    outs = pl.pallas_call(
        body, name="mix_bwd", grid=(n,),
        in_specs=[pl.BlockSpec((ts, D_MODEL), lambda j: (idx(j), 0)),
                  pl.BlockSpec((ts, REST_W), lambda j: (idx(j), 0)),
                  pl.BlockSpec((HALO, POOL_W), lambda j: (jnp.maximum(idx(j) * (ts // HALO) - 1, 0), 0))]
                 + _dil_specs(ts, GROUP_W, idx) + _dil_specs(ts, GROUP_W, idx)
                 + [pl.BlockSpec((1, POOL_W), lambda j: (0, 0))] + [ANY] * 4,
        out_specs=[pl.BlockSpec((ts, REST_W), lambda j: (idx(j), 0))]
                  + _dil_specs(ts, GROUP_W, idx) + _dil_specs(ts, GROUP_W, idx)
                  + [pl.BlockSpec((1, POOL_W), lambda j: (0, 0))] + [ANY] * 4,
        out_shape=[jax.ShapeDtypeStruct((s, REST_W), BF16)] + do_shapes + c_shapes
                  + [jax.ShapeDtypeStruct((1, POOL_W), F32)]
                  + [jax.ShapeDtypeStruct(shape, F32) for shape in grad_shapes],
        scratch_shapes=[tile_buf] * 8 + [pltpu.VMEM((GROUP_W // LANES, ts, LANES), F32)]
                       + [pltpu.VMEM(w.shape, BF16) for w in weights]
                       + [pltpu.VMEM(shape, F32) for shape in grad_shapes]
                       + [pltpu.SemaphoreType.DMA((4,))],
        compiler_params=_params(dimension_semantics=("arbitrary",)),
    )(dh1, zr, zr, *o_dil, *l_dil, scale, wbd, wao, wpo, wout)
    dzr, do_dil, c_dil, g_scale = outs[0], outs[1:4], outs[4:7], outs[7]
    g_out, g_ao, g_po, g_bd = outs[8:]
    return dzr, do_dil, c_dil, (g_out, g_ao, g_po, g_bd, g_scale)


FF_CHUNK = 1024


def _mlp_fwd_bwd(h1, tgt, g2, g3, wmi, wmo, ts):
    s = h1.shape[0]
    n = s // ts
    nchunk = D_FF // FF_CHUNK

    def body(h1_ref, t_ref, g2_ref, g3_ref, wmi_hbm, wmo_hbm,
             dh1_ref, mt_ref, dh2t_ref, hid_ref, df_ref, loss_ref, dg2_ref, dg3_ref,
             wmi, wmo, relu_buf, sem):
        i = pl.program_id(0)
        _load_resident(i, [(wmi_hbm, wmi), (wmo_hbm, wmo)], sem)

        @pl.when(i == 0)
        def _():
            loss_ref[...] = jnp.zeros_like(loss_ref)
            dg2_ref[...] = jnp.zeros_like(dg2_ref)
            dg3_ref[...] = jnp.zeros_like(dg3_ref)

        h1 = h1_ref[...]
        g2 = g2_ref[...]
        g3 = g3_ref[...]
        r2 = _rms(h1)
        n2 = h1 * r2
        m = n2 * g2
        mb = m.astype(BF16)
        mt_ref[...] = m.T.astype(BF16)
        h2 = h1
        for c in range(nchunk):
            cols = slice(c * FF_CHUNK, (c + 1) * FF_CHUNK)
            rl = jnp.maximum(_dot(mb, wmi[:, cols]), 0.0)
            relu_buf[:, cols] = rl
            hb = (rl * rl).astype(BF16)
            hid_ref[:, cols] = hb
            h2 = h2 + _dot(hb, wmo[cols, :])
        r3 = _rms(h2)
        n3 = h2 * r3
        diff = n3 * g3 - t_ref[...]
        loss_ref[...] += jnp.sum(0.5 * jnp.sum(diff * diff, axis=1, keepdims=True) / D_MODEL,
                                 axis=0, keepdims=True)
        dy = diff * (1.0 / D_MODEL)
        dg3_ref[...] += jnp.sum(dy * n3, axis=0, keepdims=True)
        dh2 = _rms_bwd(dy * g3, n3, r3)
        dh2b = dh2.astype(BF16)
        dh2t_ref[...] = dh2.T.astype(BF16)
        dm = jnp.zeros((ts, D_MODEL), F32)
        for c in range(nchunk):
            cols = slice(c * FF_CHUNK, (c + 1) * FF_CHUNK)
            dfb = (_dot_nt(dh2b, wmo[cols, :]) * (2.0 * relu_buf[:, cols])).astype(BF16)
            df_ref[:, cols] = dfb
            dm = dm + _dot_nt(dfb, wmi[:, cols])
        dg2_ref[...] += jnp.sum(dm * n2, axis=0, keepdims=True)
        dh1_ref[...] = dh2 + _rms_bwd(dm * g2, n2, r2)

    row = lambda w: pl.BlockSpec((ts, w), lambda i: (i, 0))
    colb = pl.BlockSpec((D_MODEL, ts), lambda i: (0, i))
    vec = pl.BlockSpec((1, D_MODEL), lambda i: (0, 0))
    return pl.pallas_call(
        body, name="mlp_fwd_bwd", grid=(n,),
        in_specs=[row(D_MODEL), row(D_MODEL), vec, vec, ANY, ANY],
        out_specs=[row(D_MODEL), colb, colb, row(D_FF), row(D_FF),
                   pl.BlockSpec((1, 1), lambda i: (0, 0)), vec, vec],
        out_shape=[jax.ShapeDtypeStruct((s, D_MODEL), F32),
                   jax.ShapeDtypeStruct((D_MODEL, s), BF16), jax.ShapeDtypeStruct((D_MODEL, s), BF16),
                   jax.ShapeDtypeStruct((s, D_FF), BF16), jax.ShapeDtypeStruct((s, D_FF), BF16),
                   jax.ShapeDtypeStruct((1, 1), F32),
                   jax.ShapeDtypeStruct((1, D_MODEL), F32), jax.ShapeDtypeStruct((1, D_MODEL), F32)],
        scratch_shapes=[pltpu.VMEM((D_MODEL, D_FF), BF16), pltpu.VMEM((D_FF, D_MODEL), BF16),
                        pltpu.VMEM((ts, D_FF), F32), pltpu.SemaphoreType.DMA((2,))],
        compiler_params=_params(dimension_semantics=("arbitrary",)),
    )(h1, tgt, g2, g3, wmi, wmo)


def _in_proj_bwd(x, dh1, dzr, dqkv_dil, g1, w_in, ts):
    s = x.shape[0]
    n = s // ts

    def body(x_ref, dh_ref, dzr_ref, q0, q1, q2, g_ref, w_hbm, dx_ref, dz_ref, dg_ref, w_ref, qbuf, sem):
        i = pl.program_id(0)
        _load_resident(i, [(w_hbm, w_ref)], sem)

        @pl.when(i == 0)
        def _():
            dg_ref[...] = jnp.zeros_like(dg_ref)

        for g, dqkv in enumerate(_gather_rows((q0, q1, q2), qbuf, ts)):
            for sec in range(3):
                c0 = sec * ATT_W + g * GROUP_W
                dz_ref[:, c0:c0 + GROUP_W] = dqkv[:, sec * GROUP_W:(sec + 1) * GROUP_W].astype(BF16)
        dz_ref[:, 3 * ATT_W:] = dzr_ref[...]
        du = _dot_nt(dz_ref[...], w_ref[...])
        x = x_ref[...]
        r1 = _rms(x)
        n1 = x * r1
        g1 = g_ref[...]
        dg_ref[...] += jnp.sum(du * n1, axis=0, keepdims=True)
        dx_ref[...] = dh_ref[...] + _rms_bwd(du * g1, n1, r1)

    row = lambda w: pl.BlockSpec((ts, w), lambda i: (i, 0))
    vec = pl.BlockSpec((1, D_MODEL), lambda i: (0, 0))
    return pl.pallas_call(
        body, name="in_proj_bwd", grid=(n,),
        in_specs=[row(D_MODEL), row(D_MODEL), row(REST_W)] + _dil_specs(ts, ATT_W, lambda i: i) + [vec, ANY],
        out_specs=[row(D_MODEL), row(N_IN), vec],
        out_shape=[jax.ShapeDtypeStruct((s, D_MODEL), F32), jax.ShapeDtypeStruct((s, N_IN), BF16),
                   jax.ShapeDtypeStruct((1, D_MODEL), F32)],
        scratch_shapes=[pltpu.VMEM((D_MODEL, N_IN), BF16), pltpu.VMEM((ATT_W // LANES, ts, LANES), F32),
                        pltpu.SemaphoreType.DMA((1,))],
        compiler_params=_params(dimension_semantics=("arbitrary",)),
    )(x, dh1, dzr, *dqkv_dil, g1, w_in)


def _weight_grad(at, b, name, transpose_out, tk=2048):
    m, s = at.shape
    nn = b.shape[1]
    tn = nn // N_DEV
    tk = min(tk, s)
    nk = s // tk

    def body(at_ref, b_ref, out_ref, acc):
        k = pl.program_id(1)

        @pl.when(k == 0)
        def _():
            acc[...] = jnp.zeros_like(acc)

        acc[...] += _dot(at_ref[...], b_ref[...])

        @pl.when(k == nk - 1)
        def _():
            res = acc[...]
            out_ref[...] = (res.T if transpose_out else res).astype(BF16)

    oshape = (tn, m) if transpose_out else (m, tn)
    return pl.pallas_call(
        body, name=name, grid=(N_DEV, nk),
        in_specs=[pl.BlockSpec((m, tk), lambda j, k: (0, k)), pl.BlockSpec((tk, tn), lambda j, k: (k, j))],
        out_specs=pl.BlockSpec((None,) + oshape, lambda j, k: (j, 0, 0)),
        out_shape=jax.ShapeDtypeStruct((N_DEV,) + oshape, BF16),
        scratch_shapes=[pltpu.VMEM((m, tn), F32)],
        compiler_params=_params(dimension_semantics=("arbitrary", "arbitrary")),
    )(at, b)


def _my_place():
    x, y, c = lax.axis_index("x"), lax.axis_index("y"), lax.axis_index("c")
    return x, y, c


def _peer(place, k):
    x, y, c = place
    return (1 - x if k & 4 else x, 1 - y if k & 2 else y, 1 - c if k & 1 else c)


def _linear(place):
    x, y, c = place
    return 4 * x + 2 * y + c


def _exchange(arrays, gather, name):
    nw = len(arrays)

    def body(*refs):
        ins, outs = refs[:nw], refs[nw:2 * nw]
        send_sems, recv_sems, local_sems = refs[2 * nw:]
        me = _my_place()
        mine = _linear(me)
        local = []
        for w in range(nw):
            src = ins[w] if gather[w] else ins[w].at[mine]
            cp = pltpu.make_async_copy(src, outs[w].at[mine], local_sems.at[w])
            cp.start()
            local.append(cp)
        remote = []
        for k in range(1, N_DEV):
            peer = _peer(me, k)
            for w in range(nw):
                src = ins[w] if gather[w] else ins[w].at[_linear(peer)]
                cp = pltpu.make_async_remote_copy(
                    src_ref=src, dst_ref=outs[w].at[mine],
                    send_sem=send_sems.at[w, k - 1], recv_sem=recv_sems.at[w, k - 1],
                    device_id=peer, device_id_type=pl.DeviceIdType.MESH)
                cp.start()
                remote.append(cp)
        for cp in remote:
            cp.wait_recv()
        for cp in remote:
            cp.wait_send()
        for cp in local:
            cp.wait()

    out_shape = []
    for a, g in zip(arrays, gather):
        block = a.shape if g else a.shape[1:]
        out_shape.append(jax.ShapeDtypeStruct((N_DEV,) + tuple(block), a.dtype))
    return pl.pallas_call(
        body, name=name,
        in_specs=[ANY] * nw, out_specs=[ANY] * nw, out_shape=out_shape,
        scratch_shapes=[pltpu.SemaphoreType.DMA((nw, N_DEV - 1)), pltpu.SemaphoreType.DMA((nw, N_DEV - 1)),
                        pltpu.SemaphoreType.DMA((nw,))],
    )(*arrays)


def _adamw(parts, w, m, v, name, tr):
    rows, cols = w.shape
    tr = min(tr, rows)

    def body(p_ref, w_ref, m_ref, v_ref, g_ref, d_ref, nm_ref, nv_ref):
        g = p_ref[0].astype(F32)
        for j in range(1, N_DEV):
            g = g + p_ref[j].astype(F32)
        nm = ADAM_B1 * m_ref[...] + (1.0 - ADAM_B1) * g
        nv = ADAM_B2 * v_ref[...] + (1.0 - ADAM_B2) * (g * g)
        m_hat = nm / (1.0 - ADAM_B1 ** ADAM_STEP)
        v_hat = nv / (1.0 - ADAM_B2 ** ADAM_STEP)
        g_ref[...] = g
        d_ref[...] = -ADAM_LR * (m_hat / (jnp.sqrt(v_hat) + ADAM_EPS) + ADAM_WD * w_ref[...])
        nm_ref[...] = nm
        nv_ref[...] = nv

    blk = pl.BlockSpec((tr, cols), lambda i: (i, 0))
    return pl.pallas_call(
        body, name=name, grid=(rows // tr,),
        in_specs=[pl.BlockSpec((N_DEV, tr, cols), lambda i: (0, i, 0)), blk, blk, blk],
        out_specs=[blk] * 4,
        out_shape=[jax.ShapeDtypeStruct((rows, cols), F32)] * 4,
        compiler_params=_params(dimension_semantics=("arbitrary",)),
    )(parts, w, m, v)


SMALL_PARTS = (("loss", 1), ("norm_mix_g", D_MODEL), ("pool_scale", POOL_W), ("norm_mlp_g", D_MODEL),
               ("norm_final_g", D_MODEL), ("w_pool_grp", 4 * POOL_GW * POOL_GW))


def _pack_small(values):
    rows = []
    for (name, size), v in zip(SMALL_PARTS, values):
        padded = -(-size // 1024) * 1024
        flat = jnp.pad(v.reshape(-1).astype(F32), (0, padded - size))
        rows.append(flat.reshape(padded // 128, 128))
    return jnp.concatenate(rows, axis=0)


def _unpack_small(packed, shapes):
    out, r = [], 0
    for (name, size), shape in zip(SMALL_PARTS, shapes):
        nrow = -(-size // 1024) * 8
        out.append(packed[r:r + nrow].reshape(-1)[:size].reshape(shape))
        r += nrow
    return out


def _block_diag(w_grp):
    z = jnp.zeros((POOL_GW, POOL_GW), w_grp.dtype)
    return jnp.concatenate(
        [jnp.concatenate([w_grp[g] if c == g else z for c in range(4)], axis=1) for g in range(4)], axis=0)


def _local_step(x, tgt, g1, g2, g3, w_in, w_ao, wbd, scale, w_po, w_out, w_mi, w_mo):
    ut, qkv0, qkv1, qkv2, zr = _in_proj_fwd(x, g1, w_in, ts=512)
    s = x.shape[0]
    qkv_dil = (qkv0, qkv1, qkv2)
    flat = lambda a: a.reshape(s, a.shape[-1])
    shaped = lambda a, g: a if g == 0 else a.reshape(ATT_GROUPS[g][1], s // ATT_GROUPS[g][1], a.shape[-1])
    o_dil, l_dil = [], []
    for g in range(N_GROUPS):
        o, l = _attn_fwd(flat(qkv_dil[g]), g)
        o_dil.append(shaped(o, g))
        l_dil.append(shaped(l, g))
    h1 = _mix_fwd(x, zr, o_dil, l_dil, wbd, scale, w_ao, w_po, w_out, ts=256)
    dh1, mt, dh2t, hid, df, loss, dg2, dg3 = _mlp_fwd_bwd(h1, tgt, g2, g3, w_mi, w_mo, ts=256)
    g_mi = _weight_grad(mt, df, "grad_w_mlp_in", transpose_out=False)
    g_mo = _weight_grad(dh2t, hid, "grad_w_mlp_out", transpose_out=True)
    dzr, do_dil, c_dil, (g_out, g_ao, g_po, g_bd, g_scale) = _mix_bwd(
        dh1, zr, o_dil, l_dil, wbd, scale, w_ao, w_po, w_out, ts=256)
    dqkv_dil = []
    for g in range(N_GROUPS):
        dq, dk, dv = _attn_bwd(flat(qkv_dil[g]), flat(do_dil[g]), flat(l_dil[g]), flat(c_dil[g]), g)
        dqkv_dil.append(shaped(jnp.concatenate([dq, dk, dv], axis=1), g))
    dx, dz, dg1 = _in_proj_bwd(x, dh1, dzr, dqkv_dil, g1, w_in, ts=256)
    g_in = _weight_grad(ut, dz, "grad_w_in", transpose_out=False)
    g_grp = jnp.stack([g_bd[g * POOL_GW:(g + 1) * POOL_GW, g * POOL_GW:(g + 1) * POOL_GW] for g in range(4)])
    return dict(loss=loss, dx=dx, g1=dg1, g2=dg2, g3=dg3, g_in=g_in, g_ao=g_ao, g_grp=g_grp, g_scale=g_scale,
                g_po=g_po, g_out=g_out, g_mi=g_mi, g_mo=g_mo)


def kernel(x, norm_mix_g, w_in, w_att_out, w_pool_grp, pool_scale, w_pool_out, w_out, norm_mlp_g, w_mlp_in, w_mlp_out, norm_final_g, loss_target, m_norm_mix_g, m_w_in, m_w_att_out, m_w_pool_grp, m_pool_scale, m_w_pool_out, m_w_out, m_norm_mlp_g, m_w_mlp_in, m_w_mlp_out, m_norm_final_g, v_norm_mix_g, v_w_in, v_w_att_out, v_w_pool_grp, v_pool_scale, v_w_pool_out, v_w_out, v_norm_mlp_g, v_w_mlp_in, v_w_mlp_out, v_norm_final_g):
    shards = [w_in[0], w_att_out[0], w_pool_out[0], w_out[0], w_mlp_in[0], w_mlp_out[0]]
    gathered = _exchange([a.astype(BF16) for a in shards], [True] * 6, "gather_weights")
    cols = lambda a: jnp.transpose(a, (1, 0, 2)).reshape(a.shape[1], N_DEV * a.shape[2])
    rows = lambda a: a.reshape(N_DEV * a.shape[1], a.shape[2])
    f_in, f_ao, f_po, f_out, f_mi, f_mo = (cols(gathered[0]), cols(gathered[1]), cols(gathered[2]),
                                           rows(gathered[3]), cols(gathered[4]), rows(gathered[5]))

    r = _local_step(x[0], loss_target[0], norm_mix_g, norm_mlp_g, norm_final_g.reshape(1, D_MODEL),
                    f_in, f_ao, _block_diag(w_pool_grp[0]).astype(BF16), pool_scale, f_po, f_out, f_mi, f_mo)

    blocks_of_cols = lambda a: jnp.transpose(a.reshape(a.shape[0], N_DEV, a.shape[1] // N_DEV), (1, 0, 2))
    blocks_of_rows = lambda a: a.reshape(N_DEV, a.shape[0] // N_DEV, a.shape[1])
    small = _pack_small([r["loss"], r["g1"], r["g_scale"], r["g2"], r["g3"], r["g_grp"]])
    sent = [r["g_in"], blocks_of_cols(r["g_ao"]).astype(BF16), blocks_of_cols(r["g_po"]).astype(BF16),
            blocks_of_rows(r["g_out"]).astype(BF16), r["g_mi"], r["g_mo"], small]
    got = _exchange(sent, [False] * 6 + [True], "scatter_grads")

    names = ["w_in", "w_att_out", "w_pool_out", "w_out", "w_mlp_in", "w_mlp_out"]
    ms = [m_w_in, m_w_att_out, m_w_pool_out, m_w_out, m_w_mlp_in, m_w_mlp_out]
    vs = [v_w_in, v_w_att_out, v_w_pool_out, v_w_out, v_w_mlp_in, v_w_mlp_out]
    upd = {}
    for k, name in enumerate(names):
        res = _adamw(got[k], shards[k], ms[k][0], vs[k][0], "adamw_" + name, tr=256)
        upd[name] = [a[None] for a in res]

    small_w = [jnp.zeros((1,), F32), norm_mix_g, pool_scale, norm_mlp_g, norm_final_g, w_pool_grp]
    small_m = [jnp.zeros((1,), F32), m_norm_mix_g, m_pool_scale, m_norm_mlp_g, m_norm_final_g, m_w_pool_grp]
    small_v = [jnp.ones((1,), F32), v_norm_mix_g, v_pool_scale, v_norm_mlp_g, v_norm_final_g, v_w_pool_grp]
    res = _adamw(got[6], _pack_small(small_w), _pack_small(small_m), _pack_small(small_v), "adamw_small", tr=2048)
    shapes = [(), norm_mix_g.shape, pool_scale.shape, norm_mlp_g.shape, norm_final_g.shape, w_pool_grp.shape]
    small_names = ["loss", "norm_mix_g", "pool_scale", "norm_mlp_g", "norm_final_g", "w_pool_grp"]
    unpacked = [_unpack_small(a, shapes) for a in res]
    for k, name in enumerate(small_names):
        upd[name] = [unpacked[q][k] for q in range(4)]

    order = ["norm_mix_g", "w_in", "w_att_out", "w_pool_grp", "pool_scale", "w_pool_out", "w_out", "norm_mlp_g",
             "w_mlp_in", "w_mlp_out", "norm_final_g"]
    out = [upd["loss"][0], r["dx"][None]]
    for q in range(4):
        out += [upd[name][q] for name in order]
    return tuple(out)
```

```python
import functools
import math

import jax
import jax.numpy as jnp
from jax import lax
from jax.experimental import pallas as pl
from jax.experimental.pallas import tpu as pltpu

F32 = jnp.float32
BF16 = jnp.bfloat16

D_MODEL = 1024
HEAD_DIM = 64
GROUP_W = 256
ATT_GROUPS = ((128, 1), (512, 4), (2048, 16))
N_GROUPS = 3
BLK = 128
ATT_W = 768
POOL_W = 768
POOL_GW = 192
D_FF = 4096
N_IN = 5120
REST_W = N_IN - 3 * ATT_W
NORM_EPS = 1e-6
ALIBI_MAX_BIAS = 8.0
N_DEV = 8
HALO = 32

ADAM_LR = 0.001
ADAM_B1 = 0.9
ADAM_B2 = 0.999
ADAM_EPS = 1e-08
ADAM_WD = 0.01
ADAM_STEP = 10

VMEM_LIMIT = 56 * 1024 * 1024
ANY = pl.BlockSpec(memory_space=pl.ANY)


def _params(**kw):
    return pltpu.CompilerParams(vmem_limit_bytes=VMEM_LIMIT, **kw)


def _dot(a, b):
    return jnp.dot(a, b, preferred_element_type=F32)


def _dot_nt(a, b):
    return lax.dot_general(a, b, (((1,), (1,)), ((), ())), preferred_element_type=F32)


def _dot_tn(a, b):
    return lax.dot_general(a, b, (((0,), (0,)), ((), ())), preferred_element_type=F32)


def _slope(head):
    return 2.0 ** (-ALIBI_MAX_BIAS * (head + 1.0) / 12.0)


def _load_resident(step, pairs, sem):
    @pl.when(step == 0)
    def _():
        copies = [pltpu.make_async_copy(src, dst, sem.at[n]) for n, (src, dst) in enumerate(pairs)]
        for cp in copies:
            cp.start()
        for cp in copies:
            cp.wait()


LANES = 128


def _to_class_order(value, dil, buf, ref, col0):
    ts, w = value.shape
    if dil == 1:
        ref[:, col0:col0 + w] = value.astype(ref.dtype)
        return
    for c in range(w // LANES):
        buf[c] = value[:, c * LANES:(c + 1) * LANES]
        for r in range(dil):
            ref[r, :, col0 + c * LANES:col0 + (c + 1) * LANES] = (
                buf[c, pl.ds(r, ts // dil, stride=dil), :].astype(ref.dtype))


def _to_token_order(ref, dil, buf, ts):
    if dil == 1:
        return ref[...].astype(F32)
    w = ref.shape[-1]
    for c in range(w // LANES):
        for r in range(dil):
            buf[c, pl.ds(r, ts // dil, stride=dil), :] = ref[r, :, c * LANES:(c + 1) * LANES].astype(F32)
    return jnp.concatenate([buf[c] for c in range(w // LANES)], axis=1)


def _rms(x):
    return lax.rsqrt(jnp.mean(x * x, axis=-1, keepdims=True) + NORM_EPS)


def _rms_bwd(dn, n, r):
    return r * (dn - n * jnp.mean(dn * n, axis=-1, keepdims=True))


def _in_proj_fwd(x, g1, w_in, ts):
    s = x.shape[0]
    dils = [d for _, d in ATT_GROUPS]

    def body(x_ref, g_ref, w_hbm, ut_ref, q0_ref, q1_ref, q2_ref, zr_ref, w_ref, zbuf, sem):
        i = pl.program_id(0)
        _load_resident(i, [(w_hbm, w_ref)], sem)
        x = x_ref[...]
        u = x * _rms(x) * g_ref[...]
        ut_ref[...] = u.T.astype(BF16)
        ub = u.astype(BF16)
        outs = (q0_ref, q1_ref, q2_ref)
        for sec in range(3):
            for g in range(N_GROUPS):
                c0 = sec * ATT_W + g * GROUP_W
                zc = _dot(ub, w_ref[:, c0:c0 + GROUP_W])
                _to_class_order(zc, dils[g], zbuf, outs[g], sec * GROUP_W)
        for c0 in range(0, REST_W, 256):
            zr_ref[:, c0:c0 + 256] = _dot(ub, w_ref[:, 3 * ATT_W + c0:3 * ATT_W + c0 + 256]).astype(BF16)

    n = s // ts
    return pl.pallas_call(
        body, name="in_proj_fwd", grid=(n,),
        in_specs=[pl.BlockSpec((ts, D_MODEL), lambda i: (i, 0)),
                  pl.BlockSpec((1, D_MODEL), lambda i: (0, 0)), ANY],
        out_specs=[pl.BlockSpec((D_MODEL, ts), lambda i: (0, i)),
                   pl.BlockSpec((ts, ATT_W), lambda i: (i, 0)),
                   pl.BlockSpec((4, ts // 4, ATT_W), lambda i: (0, i, 0)),
                   pl.BlockSpec((16, ts // 16, ATT_W), lambda i: (0, i, 0)),
                   pl.BlockSpec((ts, REST_W), lambda i: (i, 0))],
        out_shape=[jax.ShapeDtypeStruct((D_MODEL, s), BF16),
                   jax.ShapeDtypeStruct((s, ATT_W), BF16),
                   jax.ShapeDtypeStruct((4, s // 4, ATT_W), BF16),
                   jax.ShapeDtypeStruct((16, s // 16, ATT_W), BF16),
                   jax.ShapeDtypeStruct((s, REST_W), BF16)],
        scratch_shapes=[pltpu.VMEM((D_MODEL, N_IN), BF16), pltpu.VMEM((GROUP_W // LANES, ts, LANES), F32),
                        pltpu.SemaphoreType.DMA((1,))],
        compiler_params=_params(dimension_semantics=("arbitrary",)),
    )(x, g1, w_in)


ATT_TILE = 4 * BLK


HEADS = GROUP_W // HEAD_DIM
STACK = HEADS * BLK


def _band_consts(group, dil):
    row = lax.broadcasted_iota(jnp.int32, (STACK, 2 * BLK), 0)
    kj = lax.broadcasted_iota(jnp.int32, (STACK, 2 * BLK), 1)
    head = row // BLK
    steps = BLK + (row % BLK) - kj
    slope = jnp.full((STACK, 2 * BLK), _slope(4 * group + HEADS - 1), F32)
    for h in range(HEADS - 1):
        slope = jnp.where(head == h, _slope(4 * group + h), slope)
    in_band = (steps >= 0) & (steps <= BLK)
    return in_band, kj, slope * (steps.astype(F32) * float(dil))


def _head_of_col():
    return lax.broadcasted_iota(jnp.int32, (BLK, GROUP_W), 1) // HEAD_DIM


def _stack_heads(xb):
    head_of = _head_of_col()
    return jnp.concatenate([jnp.where(head_of == h, xb, jnp.zeros_like(xb)) for h in range(HEADS)], axis=0)


def _unstack_heads(y):
    head_of = _head_of_col()
    acc = y[0:BLK]
    for h in range(1, HEADS):
        acc = jnp.where(head_of == h, y[h * BLK:(h + 1) * BLK], acc)
    return acc


def _wide(x):
    return jnp.concatenate([x, x], axis=1)


def _band_softmax(qs, kb, valid, bias):
    sc = _dot_nt(qs, kb) * (HEAD_DIM ** -0.5)
    sc = jnp.where(valid, sc - bias, -jnp.inf)
    mx = jnp.max(sc, axis=1, keepdims=True)
    e = jnp.exp(sc - mx)
    eb = e.astype(BF16)
    den = _dot(eb, jnp.ones((2 * BLK, LANES), BF16))
    return e, eb, mx, den


def _attn_fwd(qkv, group):
    s = qkv.shape[0]
    dil = ATT_GROUPS[group][1]
    nbc = s // (BLK * dil)
    nb = ATT_TILE // BLK

    def body(q_ref, kc_ref, kp_ref, vc_ref, vp_ref, o_ref, l_ref, kbuf, vbuf):
        i = pl.program_id(0)
        kbuf[0:BLK] = kp_ref[...]
        kbuf[BLK:BLK + ATT_TILE] = kc_ref[...]
        vbuf[0:BLK] = vp_ref[...]
        vbuf[BLK:BLK + ATT_TILE] = vc_ref[...]
        in_band, kj, bias = _band_consts(group, dil)
        for b in range(nb):
            first_key = jnp.where(((i * nb + b) % nbc) != 0, 0, BLK)
            valid = in_band & (kj >= first_key)
            kb = kbuf[b * BLK:b * BLK + 2 * BLK, :]
            vb = vbuf[b * BLK:b * BLK + 2 * BLK, :]
            qs = _stack_heads(q_ref[b * BLK:(b + 1) * BLK, :])
            e, eb, mx, den = _band_softmax(qs, kb, valid, bias)
            o_ref[b * BLK:(b + 1) * BLK, :] = _unstack_heads(_dot(eb, vb) * _wide(1.0 / den))
            l_ref[b * BLK:(b + 1) * BLK, :] = _unstack_heads(_wide(mx + jnp.log(den)))

    n = s // ATT_TILE
    cur = lambda c: pl.BlockSpec((ATT_TILE, GROUP_W), lambda i: (i, c))
    prev = lambda c: pl.BlockSpec((BLK, GROUP_W), lambda i: (jnp.maximum(i * nb - 1, 0), c))
    return pl.pallas_call(
        body, name=f"attn_fwd_g{group}", grid=(n,),
        in_specs=[cur(0), cur(1), prev(1), cur(2), prev(2)],
        out_specs=[cur(0), cur(0)],
        out_shape=[jax.ShapeDtypeStruct((s, GROUP_W), F32), jax.ShapeDtypeStruct((s, GROUP_W), F32)],
        scratch_shapes=[pltpu.VMEM((BLK + ATT_TILE, GROUP_W), BF16), pltpu.VMEM((BLK + ATT_TILE, GROUP_W), BF16)],
        compiler_params=_params(dimension_semantics=("arbitrary",)),
    )(qkv, qkv, qkv, qkv, qkv)


def _attn_bwd(qkv, do, corr, group):
    s = qkv.shape[0]
    dil = ATT_GROUPS[group][1]
    nbc = s // (BLK * dil)
    nb = ATT_TILE // BLK
    n = s // ATT_TILE

    def body(q_ref, kc_ref, kp_ref, vc_ref, vp_ref, do_ref, c_ref, out_ref,
             kbuf, vbuf, dkbuf, dvbuf, dqpend, dkpend, dvpend):
        i = pl.program_id(0)

        @pl.when(i == 0)
        def _():
            dqpend[...] = jnp.zeros_like(dqpend)
            dkpend[...] = jnp.zeros_like(dkpend)
            dvpend[...] = jnp.zeros_like(dvpend)

        out_ref[:, 0:GROUP_W] = dqpend[...]
        dkbuf[...] = jnp.zeros_like(dkbuf)
        dvbuf[...] = jnp.zeros_like(dvbuf)

        @pl.when(i < n)
        def _():
            kbuf[0:BLK] = kp_ref[...]
            kbuf[BLK:BLK + ATT_TILE] = kc_ref[...]
            vbuf[0:BLK] = vp_ref[...]
            vbuf[BLK:BLK + ATT_TILE] = vc_ref[...]
            in_band, kj, bias = _band_consts(group, dil)
            head_of = _head_of_col()
            for b in range(nb):
                rows = slice(b * BLK, (b + 1) * BLK)
                band = slice(b * BLK, b * BLK + 2 * BLK)
                first_key = jnp.where(((i * nb + b) % nbc) != 0, 0, BLK)
                valid = in_band & (kj >= first_key)
                kb = kbuf[band, :]
                vb = vbuf[band, :]
                qs = _stack_heads(q_ref[rows, :])
                dos = _stack_heads(do_ref[rows, :])
                cb = c_ref[rows, :]
                cor = jnp.concatenate(
                    [jnp.max(jnp.where(head_of == h, cb, -jnp.inf), axis=1, keepdims=True) for h in range(HEADS)],
                    axis=0)
                e, _, _, den = _band_softmax(qs, kb, valid, bias)
                p = e * _wide(1.0 / den)
                dp = _dot_nt(dos, vb)
                ds = (p * (dp + cor) * (HEAD_DIM ** -0.5)).astype(BF16)
                dqpend[rows, :] = _unstack_heads(_dot(ds, kb)).astype(BF16)
                dkbuf[band, :] += _dot_tn(ds, qs)
                dvbuf[band, :] += _dot_tn(p.astype(BF16), dos)

        tail = slice(ATT_TILE - BLK, ATT_TILE)
        dkpend[tail, :] += dkbuf[0:BLK, :]
        dvpend[tail, :] += dvbuf[0:BLK, :]
        out_ref[:, GROUP_W:2 * GROUP_W] = dkpend[...].astype(BF16)
        out_ref[:, 2 * GROUP_W:] = dvpend[...].astype(BF16)
        dkpend[...] = dkbuf[BLK:BLK + ATT_TILE, :]
        dvpend[...] = dvbuf[BLK:BLK + ATT_TILE, :]

    last = n - 1
    cur = lambda c: pl.BlockSpec((ATT_TILE, GROUP_W), lambda i: (jnp.minimum(i, last), c))
    prev = lambda c: pl.BlockSpec(
        (BLK, GROUP_W), lambda i: (jnp.maximum(jnp.minimum(i, last) * nb - 1, 0), c))
    return pl.pallas_call(
        body, name=f"attn_bwd_g{group}", grid=(n + 1,),
        in_specs=[cur(0), cur(1), prev(1), cur(2), prev(2), cur(0), cur(0)],
        out_specs=pl.BlockSpec((ATT_TILE, ATT_W), lambda i: (jnp.maximum(i - 1, 0), 0)),
        out_shape=jax.ShapeDtypeStruct((s, ATT_W), BF16),
        scratch_shapes=[pltpu.VMEM((BLK + ATT_TILE, GROUP_W), BF16), pltpu.VMEM((BLK + ATT_TILE, GROUP_W), BF16),
                        pltpu.VMEM((BLK + ATT_TILE, GROUP_W), F32), pltpu.VMEM((BLK + ATT_TILE, GROUP_W), F32),
                        pltpu.VMEM((ATT_TILE, GROUP_W), BF16),
                        pltpu.VMEM((ATT_TILE, GROUP_W), F32), pltpu.VMEM((ATT_TILE, GROUP_W), F32)],
        compiler_params=_params(dimension_semantics=("arbitrary",)),
    )(qkv, qkv, qkv, qkv, qkv, do, corr)


def _dil_specs(ts, width, idx):
    return [pl.BlockSpec((ts, width), lambda i: (idx(i), 0)),
            pl.BlockSpec((4, ts // 4, width), lambda i: (0, idx(i), 0)),
            pl.BlockSpec((16, ts // 16, width), lambda i: (0, idx(i), 0))]


def _gather_rows(refs, buf, ts):
    return [_to_token_order(refs[g], ATT_GROUPS[g][1], buf, ts) for g in range(N_GROUPS)]


def _pool_fwd(ebuf, s2, s4, s8, t0, ts):
    n = ts + HALO
    s2[8:n] = ebuf[8:n] + ebuf[7:n - 1]
    s4[16:n] = s2[16:n] + s2[14:n - 2]
    s8[24:n] = s4[24:n] + s4[20:n - 4]
    s16 = s8[32:n] + s8[24:n - 8]
    col = lax.broadcasted_iota(jnp.int32, (ts, POOL_W), 1)
    psum = jnp.where(col < POOL_GW, s2[32:n],
                     jnp.where(col < 2 * POOL_GW, s4[32:n], jnp.where(col < 3 * POOL_GW, s8[32:n], s16)))
    win = jnp.where(col < POOL_GW, 2, jnp.where(col < 2 * POOL_GW, 4, jnp.where(col < 3 * POOL_GW, 8, 16)))
    t = t0 + lax.broadcasted_iota(jnp.int32, (ts, POOL_W), 0)
    count = jnp.minimum(t + 1, win).astype(F32)
    return psum / count - ebuf[32:n], count


def _mix_core(zr, pooled, outs, lses, wbd, scale, wao, wpo):
    mixed = _dot(pooled.astype(BF16), wbd)
    p = mixed * scale
    l0, l1, l2 = lses
    mx = jnp.maximum(jnp.maximum(l0, l1), l2)
    e0, e1, e2 = jnp.exp(l0 - mx), jnp.exp(l1 - mx), jnp.exp(l2 - mx)
    inv = 1.0 / (e0 + e1 + e2)
    wts = (e0 * inv, e1 * inv, e2 * inv)
    a = wts[0] * outs[0] + wts[1] * outs[1] + wts[2] * outs[2]
    att = _dot(a.astype(BF16), wao)
    pol = _dot(p.astype(BF16), wpo)
    sga = jax.nn.sigmoid(zr[:, POOL_W:POOL_W + D_MODEL].astype(F32))
    sgp = jax.nn.sigmoid(zr[:, POOL_W + D_MODEL:].astype(F32))
    mg = sga * att + sgp * pol
    return dict(mixed=mixed, p=p, wts=wts, a=a, att=att, pol=pol, sga=sga, sgp=sgp, mg=mg)


def _fill_pool_input(ebuf, zr_ref, halo_ref, t0):
    ts = zr_ref.shape[0]
    halo = halo_ref[...].astype(F32)
    t = t0 - HALO + lax.broadcasted_iota(jnp.int32, (HALO, POOL_W), 0)
    ebuf[0:HALO] = jnp.where(t >= 0, halo, 0.0)
    ebuf[HALO:HALO + ts] = zr_ref[:, 0:POOL_W].astype(F32)


def _mix_fwd(x, zr, o_dil, l_dil, wbd, scale, wao, wpo, wout, ts):
    s = x.shape[0]
    n = s // ts

    def body(x_ref, zr_ref, halo_ref, o0, o1, o2, l0, l1, l2, wbd_ref, sc_ref, wao_ref, wpo_ref, wout_ref,
             h1_ref, ebuf, s2, s4, s8, rbuf):
        i = pl.program_id(0)
        _fill_pool_input(ebuf, zr_ref, halo_ref, i * ts)
        pooled, _ = _pool_fwd(ebuf, s2, s4, s8, i * ts, ts)
        outs = _gather_rows((o0, o1, o2), rbuf, ts)
        lses = _gather_rows((l0, l1, l2), rbuf, ts)
        f = _mix_core(zr_ref[...], pooled, outs, lses, wbd_ref[...], sc_ref[...], wao_ref[...], wpo_ref[...])
        h1_ref[...] = x_ref[...] + _dot(f["mg"].astype(BF16), wout_ref[...])

    whole = lambda a: pl.BlockSpec(a.shape, lambda i: (0,) * a.ndim)
    idx = lambda i: i
    return pl.pallas_call(
        body, name="mix_fwd", grid=(n,),
        in_specs=[pl.BlockSpec((ts, D_MODEL), lambda i: (i, 0)),
                  pl.BlockSpec((ts, REST_W), lambda i: (i, 0)),
                  pl.BlockSpec((HALO, POOL_W), lambda i: (jnp.maximum(i * (ts // HALO) - 1, 0), 0))]
                 + _dil_specs(ts, GROUP_W, idx) + _dil_specs(ts, GROUP_W, idx)
                 + [whole(wbd), whole(scale), whole(wao), whole(wpo), whole(wout)],
        out_specs=pl.BlockSpec((ts, D_MODEL), lambda i: (i, 0)),
        out_shape=jax.ShapeDtypeStruct((s, D_MODEL), F32),
        scratch_shapes=[pltpu.VMEM((ts + HALO, POOL_W), F32)] * 4
                       + [pltpu.VMEM((GROUP_W // LANES, ts, LANES), F32)],
        compiler_params=_params(dimension_semantics=("arbitrary",)),
    )(x, zr, zr, *o_dil, *l_dil, wbd, scale, wao, wpo, wout)


def _mix_bwd(dh1, zr, o_dil, l_dil, wbd, scale, wao, wpo, wout, ts):
    s = dh1.shape[0]
    n = s // ts

    def body(dh_ref, zr_ref, halo_ref, o0, o1, o2, l0, l1, l2, sc_ref, wbd_hbm, wao_hbm, wpo_hbm, wout_hbm,
             dzr_ref, do0, do1, do2, c0, c1, c2, gsc_ref, gwout_hbm, gwao_hbm, gwpo_hbm, gwbd_hbm,
             ebuf, s2, s4, s8, gbuf, t2, t4, t8, rbuf,
             wbd_ref, wao_ref, wpo_ref, wout_ref, gwout_ref, gwao_ref, gwpo_ref, gwbd_ref, sem):
        j = pl.program_id(0)
        i = n - 1 - j
        _load_resident(j, [(wbd_hbm, wbd_ref), (wao_hbm, wao_ref), (wpo_hbm, wpo_ref), (wout_hbm, wout_ref)], sem)

        @pl.when(j == 0)
        def _():
            gwout_ref[...] = jnp.zeros_like(gwout_ref)
            gwao_ref[...] = jnp.zeros_like(gwao_ref)
            gwpo_ref[...] = jnp.zeros_like(gwpo_ref)
            gwbd_ref[...] = jnp.zeros_like(gwbd_ref)
            gsc_ref[...] = jnp.zeros_like(gsc_ref)
            gbuf[ts:ts + HALO] = jnp.zeros((HALO, POOL_W), F32)

        _fill_pool_input(ebuf, zr_ref, halo_ref, i * ts)
        pooled, count = _pool_fwd(ebuf, s2, s4, s8, i * ts, ts)
        outs = _gather_rows((o0, o1, o2), rbuf, ts)
        lses = _gather_rows((l0, l1, l2), rbuf, ts)
        zr = zr_ref[...]
        wbd, wao, wpo, wout = wbd_ref[...], wao_ref[...], wpo_ref[...], wout_ref[...]
        scale = sc_ref[...]
        f = _mix_core(zr, pooled, outs, lses, wbd, scale, wao, wpo)

        dhb = dh_ref[...].astype(BF16)
        gwout_ref[...] += _dot(f["mg"].T.astype(BF16), dhb)
        dmg = _dot_nt(dhb, wout)
        sga, sgp, att, pol = f["sga"], f["sgp"], f["att"], f["pol"]
        datt = dmg * sga
        dpol = dmg * sgp
        dzr_ref[:, POOL_W:POOL_W + D_MODEL] = (dmg * att * sga * (1.0 - sga)).astype(BF16)
        dzr_ref[:, POOL_W + D_MODEL:] = (dmg * pol * sgp * (1.0 - sgp)).astype(BF16)
        dattb = datt.astype(BF16)
        dpolb = dpol.astype(BF16)
        gwao_ref[...] += _dot(f["a"].T.astype(BF16), dattb)
        gwpo_ref[...] += _dot(f["p"].T.astype(BF16), dpolb)
        da = _dot_nt(dattb, wao)
        dp = _dot_nt(dpolb, wpo)

        gsc_ref[...] += jnp.sum(f["mixed"] * dp, axis=0, keepdims=True)
        dmixed = (dp * scale).astype(BF16)
        gwbd_ref[...] += _dot(pooled.T.astype(BF16), dmixed)
        dpooled = _dot_nt(dmixed, wbd)
        gbuf[0:ts] = dpooled / count
        m = ts + HALO
        t2[0:m - 8] = gbuf[0:m - 8] + gbuf[1:m - 7]
        t4[0:m - 16] = t2[0:m - 16] + t2[2:m - 14]
        t8[0:m - 24] = t4[0:m - 24] + t4[4:m - 20]
        t16 = t8[0:ts] + t8[8:ts + 8]
        col = lax.broadcasted_iota(jnp.int32, (ts, POOL_W), 1)
        back = jnp.where(col < POOL_GW, t2[0:ts],
                         jnp.where(col < 2 * POOL_GW, t4[0:ts], jnp.where(col < 3 * POOL_GW, t8[0:ts], t16)))
        dzr_ref[:, 0:POOL_W] = (back - dpooled).astype(BF16)
        gbuf[ts:ts + HALO] = gbuf[0:HALO]

        head_of = lax.broadcasted_iota(jnp.int32, (ts, GROUP_W), 1) // HEAD_DIM
        prod = da * f["a"]
        inner = jnp.zeros((ts, GROUP_W), F32)
        for h in range(4):
            hm = head_of == h
            tot = jnp.sum(jnp.where(hm, prod, 0.0), axis=1, keepdims=True)
            inner = jnp.where(hm, tot, inner)
        for g, (do_ref, c_ref) in enumerate(((do0, c0), (do1, c1), (do2, c2))):
            dil = ATT_GROUPS[g][1]
            _to_class_order(f["wts"][g] * da, dil, rbuf, do_ref, 0)
            _to_class_order(-f["wts"][g] * inner, dil, rbuf, c_ref, 0)

        @pl.when(j == n - 1)
        def _():
            pairs = ((gwout_ref, gwout_hbm), (gwao_ref, gwao_hbm), (gwpo_ref, gwpo_hbm), (gwbd_ref, gwbd_hbm))
            copies = [pltpu.make_async_copy(src, dst, sem.at[k]) for k, (src, dst) in enumerate(pairs)]
            for cp in copies:
                cp.start()
            for cp in copies:
                cp.wait()

    idx = lambda j: n - 1 - j
    do_shapes = [jax.ShapeDtypeStruct((s, GROUP_W), BF16), jax.ShapeDtypeStruct((4, s // 4, GROUP_W), BF16),
                 jax.ShapeDtypeStruct((16, s // 16, GROUP_W), BF16)]
    c_shapes = [jax.ShapeDtypeStruct(a.shape, F32) for a in do_shapes]
    weights = (wbd, wao, wpo, wout)
    grad_shapes = [(D_MODEL, D_MODEL), (GROUP_W, D_MODEL), (POOL_W, D_MODEL), (POOL_W, POOL_W)]
    tile_buf = pltpu.VMEM((ts + HALO, POOL_W), F32)
    outs = pl.pallas_call(
        body, name="mix_bwd", grid=(n,),
        in_specs=[pl.BlockSpec((ts, D_MODEL), lambda j: (idx(j), 0)),
                  pl.BlockSpec((ts, REST_W), lambda j: (idx(j), 0)),
                  pl.BlockSpec((HALO, POOL_W), lambda j: (jnp.maximum(idx(j) * (ts // HALO) - 1, 0), 0))]
                 + _dil_specs(ts, GROUP_W, idx) + _dil_specs(ts, GROUP_W, idx)
                 + [pl.BlockSpec((1, POOL_W), lambda j: (0, 0))] + [ANY] * 4,
        out_specs=[pl.BlockSpec((ts, REST_W), lambda j: (idx(j), 0))]
                  + _dil_specs(ts, GROUP_W, idx) + _dil_specs(ts, GROUP_W, idx)
                  + [pl.BlockSpec((1, POOL_W), lambda j: (0, 0))] + [ANY] * 4,
        out_shape=[jax.ShapeDtypeStruct((s, REST_W), BF16)] + do_shapes + c_shapes
                  + [jax.ShapeDtypeStruct((1, POOL_W), F32)]
                  + [jax.ShapeDtypeStruct(shape, F32) for shape in grad_shapes],
        scratch_shapes=[tile_buf] * 8 + [pltpu.VMEM((GROUP_W // LANES, ts, LANES), F32)]
                       + [pltpu.VMEM(w.shape, BF16) for w in weights]
                       + [pltpu.VMEM(shape, F32) for shape in grad_shapes]
                       + [pltpu.SemaphoreType.DMA((4,))],
        compiler_params=_params(dimension_semantics=("arbitrary",)),
    )(dh1, zr, zr, *o_dil, *l_dil, scale, wbd, wao, wpo, wout)
    dzr, do_dil, c_dil, g_scale = outs[0], outs[1:4], outs[4:7], outs[7]
    g_out, g_ao, g_po, g_bd = outs[8:]
    return dzr, do_dil, c_dil, (g_out, g_ao, g_po, g_bd, g_scale)


FF_CHUNK = 1024


def _mlp_fwd_bwd(h1, tgt, g2, g3, wmi, wmo, ts):
    s = h1.shape[0]
    n = s // ts
    nchunk = D_FF // FF_CHUNK

    def body(h1_ref, t_ref, g2_ref, g3_ref, wmi_hbm, wmo_hbm,
             dh1_ref, mt_ref, dh2t_ref, hid_ref, df_ref, loss_ref, dg2_ref, dg3_ref,
             wmi, wmo, relu_buf, sem):
        i = pl.program_id(0)
        _load_resident(i, [(wmi_hbm, wmi), (wmo_hbm, wmo)], sem)

        @pl.when(i == 0)
        def _():
            loss_ref[...] = jnp.zeros_like(loss_ref)
            dg2_ref[...] = jnp.zeros_like(dg2_ref)
            dg3_ref[...] = jnp.zeros_like(dg3_ref)

        h1 = h1_ref[...]
        g2 = g2_ref[...]
        g3 = g3_ref[...]
        r2 = _rms(h1)
        n2 = h1 * r2
        m = n2 * g2
        mb = m.astype(BF16)
        mt_ref[...] = m.T.astype(BF16)
        h2 = h1
        for c in range(nchunk):
            cols = slice(c * FF_CHUNK, (c + 1) * FF_CHUNK)
            rl = jnp.maximum(_dot(mb, wmi[:, cols]), 0.0)
            relu_buf[:, cols] = rl
            hb = (rl * rl).astype(BF16)
            hid_ref[:, cols] = hb
            h2 = h2 + _dot(hb, wmo[cols, :])
        r3 = _rms(h2)
        n3 = h2 * r3
        diff = n3 * g3 - t_ref[...]
        loss_ref[...] += jnp.sum(0.5 * jnp.sum(diff * diff, axis=1, keepdims=True) / D_MODEL,
                                 axis=0, keepdims=True)
        dy = diff * (1.0 / D_MODEL)
        dg3_ref[...] += jnp.sum(dy * n3, axis=0, keepdims=True)
        dh2 = _rms_bwd(dy * g3, n3, r3)
        dh2b = dh2.astype(BF16)
        dh2t_ref[...] = dh2.T.astype(BF16)
        dm = jnp.zeros((ts, D_MODEL), F32)
        for c in range(nchunk):
            cols = slice(c * FF_CHUNK, (c + 1) * FF_CHUNK)
            dfb = (_dot_nt(dh2b, wmo[cols, :]) * (2.0 * relu_buf[:, cols])).astype(BF16)
            df_ref[:, cols] = dfb
            dm = dm + _dot_nt(dfb, wmi[:, cols])
        dg2_ref[...] += jnp.sum(dm * n2, axis=0, keepdims=True)
        dh1_ref[...] = dh2 + _rms_bwd(dm * g2, n2, r2)

    row = lambda w: pl.BlockSpec((ts, w), lambda i: (i, 0))
    colb = pl.BlockSpec((D_MODEL, ts), lambda i: (0, i))
    vec = pl.BlockSpec((1, D_MODEL), lambda i: (0, 0))
    return pl.pallas_call(
        body, name="mlp_fwd_bwd", grid=(n,),
        in_specs=[row(D_MODEL), row(D_MODEL), vec, vec, ANY, ANY],
        out_specs=[row(D_MODEL), colb, colb, row(D_FF), row(D_FF),
                   pl.BlockSpec((1, 1), lambda i: (0, 0)), vec, vec],
        out_shape=[jax.ShapeDtypeStruct((s, D_MODEL), F32),
                   jax.ShapeDtypeStruct((D_MODEL, s), BF16), jax.ShapeDtypeStruct((D_MODEL, s), BF16),
                   jax.ShapeDtypeStruct((s, D_FF), BF16), jax.ShapeDtypeStruct((s, D_FF), BF16),
                   jax.ShapeDtypeStruct((1, 1), F32),
                   jax.ShapeDtypeStruct((1, D_MODEL), F32), jax.ShapeDtypeStruct((1, D_MODEL), F32)],
        scratch_shapes=[pltpu.VMEM((D_MODEL, D_FF), BF16), pltpu.VMEM((D_FF, D_MODEL), BF16),
                        pltpu.VMEM((ts, D_FF), F32), pltpu.SemaphoreType.DMA((2,))],
        compiler_params=_params(dimension_semantics=("arbitrary",)),
    )(h1, tgt, g2, g3, wmi, wmo)


def _in_proj_bwd(x, dh1, dzr, dqkv_dil, g1, w_in, ts):
    s = x.shape[0]
    n = s // ts

    def body(x_ref, dh_ref, dzr_ref, q0, q1, q2, g_ref, w_hbm, dx_ref, dz_ref, dg_ref, w_ref, qbuf, sem):
        i = pl.program_id(0)
        _load_resident(i, [(w_hbm, w_ref)], sem)

        @pl.when(i == 0)
        def _():
            dg_ref[...] = jnp.zeros_like(dg_ref)

        for g, dqkv in enumerate(_gather_rows((q0, q1, q2), qbuf, ts)):
            for sec in range(3):
                c0 = sec * ATT_W + g * GROUP_W
                dz_ref[:, c0:c0 + GROUP_W] = dqkv[:, sec * GROUP_W:(sec + 1) * GROUP_W].astype(BF16)
        dz_ref[:, 3 * ATT_W:] = dzr_ref[...]
        du = _dot_nt(dz_ref[...], w_ref[...])
        x = x_ref[...]
        r1 = _rms(x)
        n1 = x * r1
        g1 = g_ref[...]
        dg_ref[...] += jnp.sum(du * n1, axis=0, keepdims=True)
        dx_ref[...] = dh_ref[...] + _rms_bwd(du * g1, n1, r1)

    row = lambda w: pl.BlockSpec((ts, w), lambda i: (i, 0))
    vec = pl.BlockSpec((1, D_MODEL), lambda i: (0, 0))
    return pl.pallas_call(
        body, name="in_proj_bwd", grid=(n,),
        in_specs=[row(D_MODEL), row(D_MODEL), row(REST_W)] + _dil_specs(ts, ATT_W, lambda i: i) + [vec, ANY],
        out_specs=[row(D_MODEL), row(N_IN), vec],
        out_shape=[jax.ShapeDtypeStruct((s, D_MODEL), F32), jax.ShapeDtypeStruct((s, N_IN), BF16),
                   jax.ShapeDtypeStruct((1, D_MODEL), F32)],
        scratch_shapes=[pltpu.VMEM((D_MODEL, N_IN), BF16), pltpu.VMEM((ATT_W // LANES, ts, LANES), F32),
                        pltpu.SemaphoreType.DMA((1,))],
        compiler_params=_params(dimension_semantics=("arbitrary",)),
    )(x, dh1, dzr, *dqkv_dil, g1, w_in)


def _weight_grad(at, b, name, transpose_out, tk=2048):
    m, s = at.shape
    nn = b.shape[1]
    tn = nn // N_DEV
    tk = min(tk, s)
    nk = s // tk

    def body(at_ref, b_ref, out_ref, acc):
        k = pl.program_id(1)

        @pl.when(k == 0)
        def _():
            acc[...] = jnp.zeros_like(acc)

        acc[...] += _dot(at_ref[...], b_ref[...])

        @pl.when(k == nk - 1)
        def _():
            res = acc[...]
            out_ref[...] = (res.T if transpose_out else res).astype(BF16)

    oshape = (tn, m) if transpose_out else (m, tn)
    return pl.pallas_call(
        body, name=name, grid=(N_DEV, nk),
        in_specs=[pl.BlockSpec((m, tk), lambda j, k: (0, k)), pl.BlockSpec((tk, tn), lambda j, k: (k, j))],
        out_specs=pl.BlockSpec((None,) + oshape, lambda j, k: (j, 0, 0)),
        out_shape=jax.ShapeDtypeStruct((N_DEV,) + oshape, BF16),
        scratch_shapes=[pltpu.VMEM((m, tn), F32)],
        compiler_params=_params(dimension_semantics=("arbitrary", "arbitrary")),
    )(at, b)


def _my_place():
    x, y, c = lax.axis_index("x"), lax.axis_index("y"), lax.axis_index("c")
    return x, y, c


def _peer(place, k):
    x, y, c = place
    return (1 - x if k & 4 else x, 1 - y if k & 2 else y, 1 - c if k & 1 else c)


def _linear(place):
    x, y, c = place
    return 4 * x + 2 * y + c


def _exchange(arrays, gather, name):
    nw = len(arrays)

    def body(*refs):
        ins, outs = refs[:nw], refs[nw:2 * nw]
        send_sems, recv_sems, local_sems = refs[2 * nw:]
        me = _my_place()
        mine = _linear(me)
        local = []
        for w in range(nw):
            src = ins[w] if gather[w] else ins[w].at[mine]
            cp = pltpu.make_async_copy(src, outs[w].at[mine], local_sems.at[w])
            cp.start()
            local.append(cp)
        remote = []
        for k in range(1, N_DEV):
            peer = _peer(me, k)
            for w in range(nw):
                src = ins[w] if gather[w] else ins[w].at[_linear(peer)]
                cp = pltpu.make_async_remote_copy(
                    src_ref=src, dst_ref=outs[w].at[mine],
                    send_sem=send_sems.at[w, k - 1], recv_sem=recv_sems.at[w, k - 1],
                    device_id=peer, device_id_type=pl.DeviceIdType.MESH)
                cp.start()
                remote.append(cp)
        for cp in remote:
            cp.wait_recv()
        for cp in remote:
            cp.wait_send()
        for cp in local:
            cp.wait()

    out_shape = []
    for a, g in zip(arrays, gather):
        block = a.shape if g else a.shape[1:]
        out_shape.append(jax.ShapeDtypeStruct((N_DEV,) + tuple(block), a.dtype))
    return pl.pallas_call(
        body, name=name,
        in_specs=[ANY] * nw, out_specs=[ANY] * nw, out_shape=out_shape,
        scratch_shapes=[pltpu.SemaphoreType.DMA((nw, N_DEV - 1)), pltpu.SemaphoreType.DMA((nw, N_DEV - 1)),
                        pltpu.SemaphoreType.DMA((nw,))],
    )(*arrays)


def _adamw(parts, w, m, v, name, tr):
    rows, cols = w.shape
    tr = min(tr, rows)

    def body(p_ref, w_ref, m_ref, v_ref, g_ref, d_ref, nm_ref, nv_ref):
        g = p_ref[0].astype(F32)
        for j in range(1, N_DEV):
            g = g + p_ref[j].astype(F32)
        nm = ADAM_B1 * m_ref[...] + (1.0 - ADAM_B1) * g
        nv = ADAM_B2 * v_ref[...] + (1.0 - ADAM_B2) * (g * g)
        m_hat = nm / (1.0 - ADAM_B1 ** ADAM_STEP)
        v_hat = nv / (1.0 - ADAM_B2 ** ADAM_STEP)
        g_ref[...] = g
        d_ref[...] = -ADAM_LR * (m_hat / (jnp.sqrt(v_hat) + ADAM_EPS) + ADAM_WD * w_ref[...])
        nm_ref[...] = nm
        nv_ref[...] = nv

    blk = pl.BlockSpec((tr, cols), lambda i: (i, 0))
    return pl.pallas_call(
        body, name=name, grid=(rows // tr,),
        in_specs=[pl.BlockSpec((N_DEV, tr, cols), lambda i: (0, i, 0)), blk, blk, blk],
        out_specs=[blk] * 4,
        out_shape=[jax.ShapeDtypeStruct((rows, cols), F32)] * 4,
        compiler_params=_params(dimension_semantics=("arbitrary",)),
    )(parts, w, m, v)


SMALL_PARTS = (("loss", 1), ("norm_mix_g", D_MODEL), ("pool_scale", POOL_W), ("norm_mlp_g", D_MODEL),
               ("norm_final_g", D_MODEL), ("w_pool_grp", 4 * POOL_GW * POOL_GW))


def _pack_small(values):
    rows = []
    for (name, size), v in zip(SMALL_PARTS, values):
        padded = -(-size // 1024) * 1024
        flat = jnp.pad(v.reshape(-1).astype(F32), (0, padded - size))
        rows.append(flat.reshape(padded // 128, 128))
    return jnp.concatenate(rows, axis=0)


def _unpack_small(packed, shapes):
    out, r = [], 0
    for (name, size), shape in zip(SMALL_PARTS, shapes):
        nrow = -(-size // 1024) * 8
        out.append(packed[r:r + nrow].reshape(-1)[:size].reshape(shape))
        r += nrow
    return out


def _block_diag(w_grp):
    z = jnp.zeros((POOL_GW, POOL_GW), w_grp.dtype)
    return jnp.concatenate(
        [jnp.concatenate([w_grp[g] if c == g else z for c in range(4)], axis=1) for g in range(4)], axis=0)


def _local_step(x, tgt, g1, g2, g3, w_in, w_ao, wbd, scale, w_po, w_out, w_mi, w_mo):
    ut, qkv0, qkv1, qkv2, zr = _in_proj_fwd(x, g1, w_in, ts=512)
    s = x.shape[0]
    qkv_dil = (qkv0, qkv1, qkv2)
    flat = lambda a: a.reshape(s, a.shape[-1])
    shaped = lambda a, g: a if g == 0 else a.reshape(ATT_GROUPS[g][1], s // ATT_GROUPS[g][1], a.shape[-1])
    o_dil, l_dil = [], []
    for g in range(N_GROUPS):
        o, l = _attn_fwd(flat(qkv_dil[g]), g)
        o_dil.append(shaped(o, g))
        l_dil.append(shaped(l, g))
    h1 = _mix_fwd(x, zr, o_dil, l_dil, wbd, scale, w_ao, w_po, w_out, ts=256)
    dh1, mt, dh2t, hid, df, loss, dg2, dg3 = _mlp_fwd_bwd(h1, tgt, g2, g3, w_mi, w_mo, ts=256)
    g_mi = _weight_grad(mt, df, "grad_w_mlp_in", transpose_out=False)
    g_mo = _weight_grad(dh2t, hid, "grad_w_mlp_out", transpose_out=True)
    dzr, do_dil, c_dil, (g_out, g_ao, g_po, g_bd, g_scale) = _mix_bwd(
        dh1, zr, o_dil, l_dil, wbd, scale, w_ao, w_po, w_out, ts=256)
    dqkv_dil = []
    for g in range(N_GROUPS):
        dqkv_dil.append(shaped(_attn_bwd(flat(qkv_dil[g]), flat(do_dil[g]), flat(c_dil[g]), g), g))
    dx, dz, dg1 = _in_proj_bwd(x, dh1, dzr, dqkv_dil, g1, w_in, ts=256)
    g_in = _weight_grad(ut, dz, "grad_w_in", transpose_out=False)
    g_grp = jnp.stack([g_bd[g * POOL_GW:(g + 1) * POOL_GW, g * POOL_GW:(g + 1) * POOL_GW] for g in range(4)])
    return dict(loss=loss, dx=dx, g1=dg1, g2=dg2, g3=dg3, g_in=g_in, g_ao=g_ao, g_grp=g_grp, g_scale=g_scale,
                g_po=g_po, g_out=g_out, g_mi=g_mi, g_mo=g_mo)


def kernel(x, norm_mix_g, w_in, w_att_out, w_pool_grp, pool_scale, w_pool_out, w_out, norm_mlp_g, w_mlp_in, w_mlp_out, norm_final_g, loss_target, m_norm_mix_g, m_w_in, m_w_att_out, m_w_pool_grp, m_pool_scale, m_w_pool_out, m_w_out, m_norm_mlp_g, m_w_mlp_in, m_w_mlp_out, m_norm_final_g, v_norm_mix_g, v_w_in, v_w_att_out, v_w_pool_grp, v_pool_scale, v_w_pool_out, v_w_out, v_norm_mlp_g, v_w_mlp_in, v_w_mlp_out, v_norm_final_g):
    shards = [w_in[0], w_att_out[0], w_pool_out[0], w_out[0], w_mlp_in[0], w_mlp_out[0]]
    gathered = _exchange([a.astype(BF16) for a in shards], [True] * 6, "gather_weights")
    cols = lambda a: jnp.transpose(a, (1, 0, 2)).reshape(a.shape[1], N_DEV * a.shape[2])
    rows = lambda a: a.reshape(N_DEV * a.shape[1], a.shape[2])
    f_in, f_ao, f_po, f_out, f_mi, f_mo = (cols(gathered[0]), cols(gathered[1]), cols(gathered[2]),
                                           rows(gathered[3]), cols(gathered[4]), rows(gathered[5]))

    r = _local_step(x[0], loss_target[0], norm_mix_g, norm_mlp_g, norm_final_g.reshape(1, D_MODEL),
                    f_in, f_ao, _block_diag(w_pool_grp[0]).astype(BF16), pool_scale, f_po, f_out, f_mi, f_mo)

    blocks_of_cols = lambda a: jnp.transpose(a.reshape(a.shape[0], N_DEV, a.shape[1] // N_DEV), (1, 0, 2))
    blocks_of_rows = lambda a: a.reshape(N_DEV, a.shape[0] // N_DEV, a.shape[1])
    small = _pack_small([r["loss"], r["g1"], r["g_scale"], r["g2"], r["g3"], r["g_grp"]])
    sent = [r["g_in"], blocks_of_cols(r["g_ao"]).astype(BF16), blocks_of_cols(r["g_po"]).astype(BF16),
            blocks_of_rows(r["g_out"]).astype(BF16), r["g_mi"], r["g_mo"], small]
    got = _exchange(sent, [False] * 6 + [True], "scatter_grads")

    names = ["w_in", "w_att_out", "w_pool_out", "w_out", "w_mlp_in", "w_mlp_out"]
    ms = [m_w_in, m_w_att_out, m_w_pool_out, m_w_out, m_w_mlp_in, m_w_mlp_out]
    vs = [v_w_in, v_w_att_out, v_w_pool_out, v_w_out, v_w_mlp_in, v_w_mlp_out]
    upd = {}
    for k, name in enumerate(names):
        res = _adamw(got[k], shards[k], ms[k][0], vs[k][0], "adamw_" + name, tr=256)
        upd[name] = [a[None] for a in res]

    small_w = [jnp.zeros((1,), F32), norm_mix_g, pool_scale, norm_mlp_g, norm_final_g, w_pool_grp]
    small_m = [jnp.zeros((1,), F32), m_norm_mix_g, m_pool_scale, m_norm_mlp_g, m_norm_final_g, m_w_pool_grp]
    small_v = [jnp.ones((1,), F32), v_norm_mix_g, v_pool_scale, v_norm_mlp_g, v_norm_final_g, v_w_pool_grp]
    res = _adamw(got[6], _pack_small(small_w), _pack_small(small_m), _pack_small(small_v), "adamw_small", tr=2048)
    shapes = [(), norm_mix_g.shape, pool_scale.shape, norm_mlp_g.shape, norm_final_g.shape, w_pool_grp.shape]
    small_names = ["loss", "norm_mix_g", "pool_scale", "norm_mlp_g", "norm_final_g", "w_pool_grp"]
    unpacked = [_unpack_small(a, shapes) for a in res]
    for k, name in enumerate(small_names):
        upd[name] = [unpacked[q][k] for q in range(4)]

    order = ["norm_mix_g", "w_in", "w_att_out", "w_pool_grp", "pool_scale", "w_pool_out", "w_out", "norm_mlp_g",
             "w_mlp_in", "w_mlp_out", "norm_final_g"]
    out = [upd["loss"][0], r["dx"][None]]
    for q in range(4):
        out += [upd[name][q] for name in order]
    return tuple(out)
```

```python
import functools
import math

import jax
import jax.numpy as jnp
from jax import lax
from jax.experimental import pallas as pl
from jax.experimental.pallas import tpu as pltpu

F32 = jnp.float32
BF16 = jnp.bfloat16

D_MODEL = 1024
HEAD_DIM = 64
GROUP_W = 256
ATT_GROUPS = ((128, 1), (512, 4), (2048, 16))
N_GROUPS = 3
BLK = 128
ATT_W = 768
POOL_W = 768
POOL_GW = 192
D_FF = 4096
N_IN = 5120
REST_W = N_IN - 3 * ATT_W
NORM_EPS = 1e-6
ALIBI_MAX_BIAS = 8.0
N_DEV = 8
HALO = 32

ADAM_LR = 0.001
ADAM_B1 = 0.9
ADAM_B2 = 0.999
ADAM_EPS = 1e-08
ADAM_WD = 0.01
ADAM_STEP = 10

VMEM_LIMIT = 56 * 1024 * 1024
ANY = pl.BlockSpec(memory_space=pl.ANY)


def _params(**kw):
    return pltpu.CompilerParams(vmem_limit_bytes=VMEM_LIMIT, **kw)


def _dot(a, b):
    return jnp.dot(a, b, preferred_element_type=F32)


def _dot_nt(a, b):
    return lax.dot_general(a, b, (((1,), (1,)), ((), ())), preferred_element_type=F32)


def _dot_tn(a, b):
    return lax.dot_general(a, b, (((0,), (0,)), ((), ())), preferred_element_type=F32)


def _slope(head):
    return 2.0 ** (-ALIBI_MAX_BIAS * (head + 1.0) / 12.0)


def _load_resident(step, pairs, sem):
    @pl.when(step == 0)
    def _():
        copies = [pltpu.make_async_copy(src, dst, sem.at[n]) for n, (src, dst) in enumerate(pairs)]
        for cp in copies:
            cp.start()
        for cp in copies:
            cp.wait()


LANES = 128


def _to_class_order(value, dil, buf, ref, col0):
    ts, w = value.shape
    if dil == 1:
        ref[:, col0:col0 + w] = value.astype(ref.dtype)
        return
    for c in range(w // LANES):
        buf[c] = value[:, c * LANES:(c + 1) * LANES]
        for r in range(dil):
            ref[r, :, col0 + c * LANES:col0 + (c + 1) * LANES] = (
                buf[c, pl.ds(r, ts // dil, stride=dil), :].astype(ref.dtype))


def _to_token_order(ref, dil, buf, ts):
    if dil == 1:
        return ref[...].astype(F32)
    w = ref.shape[-1]
    for c in range(w // LANES):
        for r in range(dil):
            buf[c, pl.ds(r, ts // dil, stride=dil), :] = ref[r, :, c * LANES:(c + 1) * LANES].astype(F32)
    return jnp.concatenate([buf[c] for c in range(w // LANES)], axis=1)


def _rms(x):
    return lax.rsqrt(jnp.mean(x * x, axis=-1, keepdims=True) + NORM_EPS)


def _rms_bwd(dn, n, r):
    return r * (dn - n * jnp.mean(dn * n, axis=-1, keepdims=True))


def _split_refs(refs, counts):
    out, at = [], 0
    for c in counts:
        out.append(refs[at:at + c])
        at += c
    return out


def _carry_start(ex, step, ins, outs, sems):
    @pl.when(step == 0)
    def _():
        ex.start(ins, outs, sems)


def _carry_wait(ex, step, last, ins, outs, sems):
    @pl.when(step == last)
    def _():
        ex.wait(ins, outs, sems)


def _in_proj_fwd(x, g1, w_in, ts, ex):
    s = x.shape[0]
    n = s // ts
    dils = [d for _, d in ATT_GROUPS]

    def body(*refs):
        (x_ref, g_ref, w_hbm), ex_ins, (ut_ref, q0_ref, q1_ref, q2_ref, zr_ref), ex_outs, (w_ref, zbuf, sem), ex_sems = (
            _split_refs(refs, (3, ex.nw, 5, ex.nw, 3, 3)))
        i = pl.program_id(0)
        _carry_start(ex, i, ex_ins, ex_outs, ex_sems)
        _load_resident(i, [(w_hbm, w_ref)], sem)
        x = x_ref[...]
        u = x * _rms(x) * g_ref[...]
        ut_ref[...] = u.T.astype(BF16)
        ub = u.astype(BF16)
        outs = (q0_ref, q1_ref, q2_ref)
        for sec in range(3):
            for g in range(N_GROUPS):
                c0 = sec * ATT_W + g * GROUP_W
                zc = _dot(ub, w_ref[:, c0:c0 + GROUP_W])
                _to_class_order(zc, dils[g], zbuf, outs[g], sec * GROUP_W)
        for c0 in range(0, REST_W, 256):
            zr_ref[:, c0:c0 + 256] = _dot(ub, w_ref[:, 3 * ATT_W + c0:3 * ATT_W + c0 + 256]).astype(BF16)
        _carry_wait(ex, i, n - 1, ex_ins, ex_outs, ex_sems)

    outs = pl.pallas_call(
        body, name="in_proj_fwd", grid=(n,),
        in_specs=[pl.BlockSpec((ts, D_MODEL), lambda i: (i, 0)),
                  pl.BlockSpec((1, D_MODEL), lambda i: (0, 0)), ANY] + ex.specs,
        out_specs=[pl.BlockSpec((D_MODEL, ts), lambda i: (0, i)),
                   pl.BlockSpec((ts, ATT_W), lambda i: (i, 0)),
                   pl.BlockSpec((4, ts // 4, ATT_W), lambda i: (0, i, 0)),
                   pl.BlockSpec((16, ts // 16, ATT_W), lambda i: (0, i, 0)),
                   pl.BlockSpec((ts, REST_W), lambda i: (i, 0))] + ex.specs,
        out_shape=[jax.ShapeDtypeStruct((D_MODEL, s), BF16),
                   jax.ShapeDtypeStruct((s, ATT_W), BF16),
                   jax.ShapeDtypeStruct((4, s // 4, ATT_W), BF16),
                   jax.ShapeDtypeStruct((16, s // 16, ATT_W), BF16),
                   jax.ShapeDtypeStruct((s, REST_W), BF16)] + ex.out_shape,
        scratch_shapes=[pltpu.VMEM((D_MODEL, N_IN), BF16), pltpu.VMEM((GROUP_W // LANES, ts, LANES), F32),
                        pltpu.SemaphoreType.DMA((1,))] + ex.scratch,
        compiler_params=_params(dimension_semantics=("arbitrary",)),
    )(x, g1, w_in, *ex.arrays)
    return outs[:5], outs[5:]


ATT_TILE = 4 * BLK


HEADS = GROUP_W // HEAD_DIM
STACK = HEADS * BLK


def _band_consts(group, dil):
    row = lax.broadcasted_iota(jnp.int32, (STACK, 2 * BLK), 0)
    kj = lax.broadcasted_iota(jnp.int32, (STACK, 2 * BLK), 1)
    head = row // BLK
    steps = BLK + (row % BLK) - kj
    slope = jnp.full((STACK, 2 * BLK), _slope(4 * group + HEADS - 1), F32)
    for h in range(HEADS - 1):
        slope = jnp.where(head == h, _slope(4 * group + h), slope)
    in_band = (steps >= 0) & (steps <= BLK)
    return in_band, kj, slope * (steps.astype(F32) * float(dil))


def _head_of_col():
    return lax.broadcasted_iota(jnp.int32, (BLK, GROUP_W), 1) // HEAD_DIM


def _stack_heads(xb):
    head_of = _head_of_col()
    return jnp.concatenate([jnp.where(head_of == h, xb, jnp.zeros_like(xb)) for h in range(HEADS)], axis=0)


def _unstack_heads(y):
    head_of = _head_of_col()
    acc = y[0:BLK]
    for h in range(1, HEADS):
        acc = jnp.where(head_of == h, y[h * BLK:(h + 1) * BLK], acc)
    return acc


def _wide(x):
    return jnp.concatenate([x, x], axis=1)


def _band_softmax(qs, kb, valid, bias):
    sc = _dot_nt(qs, kb) * (HEAD_DIM ** -0.5)
    sc = jnp.where(valid, sc - bias, -jnp.inf)
    mx = jnp.max(sc, axis=1, keepdims=True)
    e = jnp.exp(sc - mx)
    eb = e.astype(BF16)
    den = _dot(eb, jnp.ones((2 * BLK, LANES), BF16))
    return e, eb, mx, den


def _attn_fwd(qkv, group):
    s = qkv.shape[0]
    dil = ATT_GROUPS[group][1]
    nbc = s // (BLK * dil)
    nb = ATT_TILE // BLK

    def body(q_ref, kc_ref, kp_ref, vc_ref, vp_ref, o_ref, l_ref, kbuf, vbuf):
        i = pl.program_id(0)
        kbuf[0:BLK] = kp_ref[...]
        kbuf[BLK:BLK + ATT_TILE] = kc_ref[...]
        vbuf[0:BLK] = vp_ref[...]
        vbuf[BLK:BLK + ATT_TILE] = vc_ref[...]
        in_band, kj, bias = _band_consts(group, dil)
        for b in range(nb):
            first_key = jnp.where(((i * nb + b) % nbc) != 0, 0, BLK)
            valid = in_band & (kj >= first_key)
            kb = kbuf[b * BLK:b * BLK + 2 * BLK, :]
            vb = vbuf[b * BLK:b * BLK + 2 * BLK, :]
            qs = _stack_heads(q_ref[b * BLK:(b + 1) * BLK, :])
            e, eb, mx, den = _band_softmax(qs, kb, valid, bias)
            o_ref[b * BLK:(b + 1) * BLK, :] = _unstack_heads(_dot(eb, vb) * _wide(1.0 / den))
            l_ref[b * BLK:(b + 1) * BLK, :] = _unstack_heads(_wide(mx + jnp.log(den)))

    n = s // ATT_TILE
    cur = lambda c: pl.BlockSpec((ATT_TILE, GROUP_W), lambda i: (i, c))
    prev = lambda c: pl.BlockSpec((BLK, GROUP_W), lambda i: (jnp.maximum(i * nb - 1, 0), c))
    return pl.pallas_call(
        body, name=f"attn_fwd_g{group}", grid=(n,),
        in_specs=[cur(0), cur(1), prev(1), cur(2), prev(2)],
        out_specs=[cur(0), cur(0)],
        out_shape=[jax.ShapeDtypeStruct((s, GROUP_W), F32), jax.ShapeDtypeStruct((s, GROUP_W), F32)],
        scratch_shapes=[pltpu.VMEM((BLK + ATT_TILE, GROUP_W), BF16), pltpu.VMEM((BLK + ATT_TILE, GROUP_W), BF16)],
        compiler_params=_params(dimension_semantics=("arbitrary",)),
    )(qkv, qkv, qkv, qkv, qkv)


def _attn_bwd(qkv, do, corr, group, ex=None):
    s = qkv.shape[0]
    dil = ATT_GROUPS[group][1]
    nbc = s // (BLK * dil)
    nb = ATT_TILE // BLK
    n = s // ATT_TILE
    nex = ex.nw if ex else 0

    def body(*refs):
        ((q_ref, kc_ref, kp_ref, vc_ref, vp_ref, do_ref, c_ref), ex_ins, (out_ref,), ex_outs,
         (kbuf, vbuf, dkbuf, dvbuf, dqpend, dkpend, dvpend), ex_sems) = _split_refs(refs, (7, nex, 1, nex, 7, 3 if ex else 0))
        i = pl.program_id(0)
        if ex:
            _carry_start(ex, i, ex_ins, ex_outs, ex_sems)

        @pl.when(i == 0)
        def _():
            dqpend[...] = jnp.zeros_like(dqpend)
            dkpend[...] = jnp.zeros_like(dkpend)
            dvpend[...] = jnp.zeros_like(dvpend)

        out_ref[:, 0:GROUP_W] = dqpend[...]
        dkbuf[...] = jnp.zeros_like(dkbuf)
        dvbuf[...] = jnp.zeros_like(dvbuf)

        @pl.when(i < n)
        def _():
            kbuf[0:BLK] = kp_ref[...]
            kbuf[BLK:BLK + ATT_TILE] = kc_ref[...]
            vbuf[0:BLK] = vp_ref[...]
            vbuf[BLK:BLK + ATT_TILE] = vc_ref[...]
            in_band, kj, bias = _band_consts(group, dil)
            head_of = _head_of_col()
            for b in range(nb):
                rows = slice(b * BLK, (b + 1) * BLK)
                band = slice(b * BLK, b * BLK + 2 * BLK)
                first_key = jnp.where(((i * nb + b) % nbc) != 0, 0, BLK)
                valid = in_band & (kj >= first_key)
                kb = kbuf[band, :]
                vb = vbuf[band, :]
                qs = _stack_heads(q_ref[rows, :])
                dos = _stack_heads(do_ref[rows, :])
                cb = c_ref[rows, :]
                cor = jnp.concatenate(
                    [jnp.max(jnp.where(head_of == h, cb, -jnp.inf), axis=1, keepdims=True) for h in range(HEADS)],
                    axis=0)
                e, _, _, den = _band_softmax(qs, kb, valid, bias)
                p = e * _wide(1.0 / den)
                dp = _dot_nt(dos, vb)
                ds = (p * (dp + cor) * (HEAD_DIM ** -0.5)).astype(BF16)
                dqpend[rows, :] = _unstack_heads(_dot(ds, kb)).astype(BF16)
                dkbuf[band, :] += _dot_tn(ds, qs)
                dvbuf[band, :] += _dot_tn(p.astype(BF16), dos)

        tail = slice(ATT_TILE - BLK, ATT_TILE)
        dkpend[tail, :] += dkbuf[0:BLK, :]
        dvpend[tail, :] += dvbuf[0:BLK, :]
        out_ref[:, GROUP_W:2 * GROUP_W] = dkpend[...].astype(BF16)
        out_ref[:, 2 * GROUP_W:] = dvpend[...].astype(BF16)
        dkpend[...] = dkbuf[BLK:BLK + ATT_TILE, :]
        dvpend[...] = dvbuf[BLK:BLK + ATT_TILE, :]
        if ex:
            _carry_wait(ex, i, n, ex_ins, ex_outs, ex_sems)

    last = n - 1
    cur = lambda c: pl.BlockSpec((ATT_TILE, GROUP_W), lambda i: (jnp.minimum(i, last), c))
    prev = lambda c: pl.BlockSpec(
        (BLK, GROUP_W), lambda i: (jnp.maximum(jnp.minimum(i, last) * nb - 1, 0), c))
    outs = pl.pallas_call(
        body, name=f"attn_bwd_g{group}", grid=(n + 1,),
        in_specs=[cur(0), cur(1), prev(1), cur(2), prev(2), cur(0), cur(0)] + (ex.specs if ex else []),
        out_specs=[pl.BlockSpec((ATT_TILE, ATT_W), lambda i: (jnp.maximum(i - 1, 0), 0))] + (ex.specs if ex else []),
        out_shape=[jax.ShapeDtypeStruct((s, ATT_W), BF16)] + (ex.out_shape if ex else []),
        scratch_shapes=[pltpu.VMEM((BLK + ATT_TILE, GROUP_W), BF16), pltpu.VMEM((BLK + ATT_TILE, GROUP_W), BF16),
                        pltpu.VMEM((BLK + ATT_TILE, GROUP_W), F32), pltpu.VMEM((BLK + ATT_TILE, GROUP_W), F32),
                        pltpu.VMEM((ATT_TILE, GROUP_W), BF16),
                        pltpu.VMEM((ATT_TILE, GROUP_W), F32), pltpu.VMEM((ATT_TILE, GROUP_W), F32)]
                       + (ex.scratch if ex else []),
        compiler_params=_params(dimension_semantics=("arbitrary",)),
    )(qkv, qkv, qkv, qkv, qkv, do, corr, *(ex.arrays if ex else []))
    return (outs[0], outs[1:]) if ex else outs[0]


def _dil_specs(ts, width, idx):
    return [pl.BlockSpec((ts, width), lambda i: (idx(i), 0)),
            pl.BlockSpec((4, ts // 4, width), lambda i: (0, idx(i), 0)),
            pl.BlockSpec((16, ts // 16, width), lambda i: (0, idx(i), 0))]


def _gather_rows(refs, buf, ts):
    return [_to_token_order(refs[g], ATT_GROUPS[g][1], buf, ts) for g in range(N_GROUPS)]


def _pool_fwd(ebuf, s2, s4, s8, t0, ts):
    n = ts + HALO
    s2[8:n] = ebuf[8:n] + ebuf[7:n - 1]
    s4[16:n] = s2[16:n] + s2[14:n - 2]
    s8[24:n] = s4[24:n] + s4[20:n - 4]
    s16 = s8[32:n] + s8[24:n - 8]
    col = lax.broadcasted_iota(jnp.int32, (ts, POOL_W), 1)
    psum = jnp.where(col < POOL_GW, s2[32:n],
                     jnp.where(col < 2 * POOL_GW, s4[32:n], jnp.where(col < 3 * POOL_GW, s8[32:n], s16)))
    win = jnp.where(col < POOL_GW, 2, jnp.where(col < 2 * POOL_GW, 4, jnp.where(col < 3 * POOL_GW, 8, 16)))
    t = t0 + lax.broadcasted_iota(jnp.int32, (ts, POOL_W), 0)
    count = jnp.minimum(t + 1, win).astype(F32)
    return psum / count - ebuf[32:n], count


def _mix_core(zr, pooled, outs, lses, wbd, scale, wao, wpo):
    mixed = _dot(pooled.astype(BF16), wbd)
    p = mixed * scale
    l0, l1, l2 = lses
    mx = jnp.maximum(jnp.maximum(l0, l1), l2)
    e0, e1, e2 = jnp.exp(l0 - mx), jnp.exp(l1 - mx), jnp.exp(l2 - mx)
    inv = 1.0 / (e0 + e1 + e2)
    wts = (e0 * inv, e1 * inv, e2 * inv)
    a = wts[0] * outs[0] + wts[1] * outs[1] + wts[2] * outs[2]
    att = _dot(a.astype(BF16), wao)
    pol = _dot(p.astype(BF16), wpo)
    sga = jax.nn.sigmoid(zr[:, POOL_W:POOL_W + D_MODEL].astype(F32))
    sgp = jax.nn.sigmoid(zr[:, POOL_W + D_MODEL:].astype(F32))
    mg = sga * att + sgp * pol
    return dict(mixed=mixed, p=p, wts=wts, a=a, att=att, pol=pol, sga=sga, sgp=sgp, mg=mg)


def _fill_pool_input(ebuf, zr_ref, halo_ref, t0):
    ts = zr_ref.shape[0]
    halo = halo_ref[...].astype(F32)
    t = t0 - HALO + lax.broadcasted_iota(jnp.int32, (HALO, POOL_W), 0)
    ebuf[0:HALO] = jnp.where(t >= 0, halo, 0.0)
    ebuf[HALO:HALO + ts] = zr_ref[:, 0:POOL_W].astype(F32)


def _mix_fwd(x, zr, o_dil, l_dil, wbd, scale, wao, wpo, wout, ts):
    s = x.shape[0]
    n = s // ts

    def body(x_ref, zr_ref, halo_ref, o0, o1, o2, l0, l1, l2, wbd_ref, sc_ref, wao_ref, wpo_ref, wout_ref,
             h1_ref, ebuf, s2, s4, s8, rbuf):
        i = pl.program_id(0)
        _fill_pool_input(ebuf, zr_ref, halo_ref, i * ts)
        pooled, _ = _pool_fwd(ebuf, s2, s4, s8, i * ts, ts)
        outs = _gather_rows((o0, o1, o2), rbuf, ts)
        lses = _gather_rows((l0, l1, l2), rbuf, ts)
        f = _mix_core(zr_ref[...], pooled, outs, lses, wbd_ref[...], sc_ref[...], wao_ref[...], wpo_ref[...])
        h1_ref[...] = x_ref[...] + _dot(f["mg"].astype(BF16), wout_ref[...])

    whole = lambda a: pl.BlockSpec(a.shape, lambda i: (0,) * a.ndim)
    idx = lambda i: i
    return pl.pallas_call(
        body, name="mix_fwd", grid=(n,),
        in_specs=[pl.BlockSpec((ts, D_MODEL), lambda i: (i, 0)),
                  pl.BlockSpec((ts, REST_W), lambda i: (i, 0)),
                  pl.BlockSpec((HALO, POOL_W), lambda i: (jnp.maximum(i * (ts // HALO) - 1, 0), 0))]
                 + _dil_specs(ts, GROUP_W, idx) + _dil_specs(ts, GROUP_W, idx)
                 + [whole(wbd), whole(scale), whole(wao), whole(wpo), whole(wout)],
        out_specs=pl.BlockSpec((ts, D_MODEL), lambda i: (i, 0)),
        out_shape=jax.ShapeDtypeStruct((s, D_MODEL), F32),
        scratch_shapes=[pltpu.VMEM((ts + HALO, POOL_W), F32)] * 4
                       + [pltpu.VMEM((GROUP_W // LANES, ts, LANES), F32)],
        compiler_params=_params(dimension_semantics=("arbitrary",)),
    )(x, zr, zr, *o_dil, *l_dil, wbd, scale, wao, wpo, wout)


def _mix_bwd(dh1, zr, o_dil, l_dil, wbd, scale, wao, wpo, wout, ts):
    s = dh1.shape[0]
    n = s // ts

    def body(dh_ref, zr_ref, halo_ref, o0, o1, o2, l0, l1, l2, sc_ref, wbd_hbm, wao_hbm, wpo_hbm, wout_hbm,
             dzr_ref, do0, do1, do2, c0, c1, c2, gsc_ref, gwout_hbm, gwao_hbm, gwpo_hbm, gwbd_hbm,
             ebuf, s2, s4, s8, gbuf, t2, t4, t8, rbuf,
             wbd_ref, wao_ref, wpo_ref, wout_ref, gwout_ref, gwao_ref, gwpo_ref, gwbd_ref, sem):
        j = pl.program_id(0)
        i = n - 1 - j
        _load_resident(j, [(wbd_hbm, wbd_ref), (wao_hbm, wao_ref), (wpo_hbm, wpo_ref), (wout_hbm, wout_ref)], sem)

        @pl.when(j == 0)
        def _():
            gwout_ref[...] = jnp.zeros_like(gwout_ref)
            gwao_ref[...] = jnp.zeros_like(gwao_ref)
            gwpo_ref[...] = jnp.zeros_like(gwpo_ref)
            gwbd_ref[...] = jnp.zeros_like(gwbd_ref)
            gsc_ref[...] = jnp.zeros_like(gsc_ref)
            gbuf[ts:ts + HALO] = jnp.zeros((HALO, POOL_W), F32)

        _fill_pool_input(ebuf, zr_ref, halo_ref, i * ts)
        pooled, count = _pool_fwd(ebuf, s2, s4, s8, i * ts, ts)
        outs = _gather_rows((o0, o1, o2), rbuf, ts)
        lses = _gather_rows((l0, l1, l2), rbuf, ts)
        zr = zr_ref[...]
        wbd, wao, wpo, wout = wbd_ref[...], wao_ref[...], wpo_ref[...], wout_ref[...]
        scale = sc_ref[...]
        f = _mix_core(zr, pooled, outs, lses, wbd, scale, wao, wpo)

        dhb = dh_ref[...].astype(BF16)
        gwout_ref[...] += _dot(f["mg"].T.astype(BF16), dhb)
        dmg = _dot_nt(dhb, wout)
        sga, sgp, att, pol = f["sga"], f["sgp"], f["att"], f["pol"]
        datt = dmg * sga
        dpol = dmg * sgp
        dzr_ref[:, POOL_W:POOL_W + D_MODEL] = (dmg * att * sga * (1.0 - sga)).astype(BF16)
        dzr_ref[:, POOL_W + D_MODEL:] = (dmg * pol * sgp * (1.0 - sgp)).astype(BF16)
        dattb = datt.astype(BF16)
        dpolb = dpol.astype(BF16)
        gwao_ref[...] += _dot(f["a"].T.astype(BF16), dattb)
        gwpo_ref[...] += _dot(f["p"].T.astype(BF16), dpolb)
        da = _dot_nt(dattb, wao)
        dp = _dot_nt(dpolb, wpo)

        gsc_ref[...] += jnp.sum(f["mixed"] * dp, axis=0, keepdims=True)
        dmixed = (dp * scale).astype(BF16)
        gwbd_ref[...] += _dot(pooled.T.astype(BF16), dmixed)
        dpooled = _dot_nt(dmixed, wbd)
        gbuf[0:ts] = dpooled / count
        m = ts + HALO
        t2[0:m - 8] = gbuf[0:m - 8] + gbuf[1:m - 7]
        t4[0:m - 16] = t2[0:m - 16] + t2[2:m - 14]
        t8[0:m - 24] = t4[0:m - 24] + t4[4:m - 20]
        t16 = t8[0:ts] + t8[8:ts + 8]
        col = lax.broadcasted_iota(jnp.int32, (ts, POOL_W), 1)
        back = jnp.where(col < POOL_GW, t2[0:ts],
                         jnp.where(col < 2 * POOL_GW, t4[0:ts], jnp.where(col < 3 * POOL_GW, t8[0:ts], t16)))
        dzr_ref[:, 0:POOL_W] = (back - dpooled).astype(BF16)
        gbuf[ts:ts + HALO] = gbuf[0:HALO]

        head_of = lax.broadcasted_iota(jnp.int32, (ts, GROUP_W), 1) // HEAD_DIM
        prod = da * f["a"]
        inner = jnp.zeros((ts, GROUP_W), F32)
        for h in range(4):
            hm = head_of == h
            tot = jnp.sum(jnp.where(hm, prod, 0.0), axis=1, keepdims=True)
            inner = jnp.where(hm, tot, inner)
        for g, (do_ref, c_ref) in enumerate(((do0, c0), (do1, c1), (do2, c2))):
            dil = ATT_GROUPS[g][1]
            _to_class_order(f["wts"][g] * da, dil, rbuf, do_ref, 0)
            _to_class_order(-f["wts"][g] * inner, dil, rbuf, c_ref, 0)

        @pl.when(j == n - 1)
        def _():
            pairs = ((gwout_ref, gwout_hbm), (gwao_ref, gwao_hbm), (gwpo_ref, gwpo_hbm), (gwbd_ref, gwbd_hbm))
            copies = [pltpu.make_async_copy(src, dst, sem.at[k]) for k, (src, dst) in enumerate(pairs)]
            for cp in copies:
                cp.start()
            for cp in copies:
                cp.wait()

    idx = lambda j: n - 1 - j
    do_shapes = [jax.ShapeDtypeStruct((s, GROUP_W), BF16), jax.ShapeDtypeStruct((4, s // 4, GROUP_W), BF16),
                 jax.ShapeDtypeStruct((16, s // 16, GROUP_W), BF16)]
    c_shapes = [jax.ShapeDtypeStruct(a.shape, F32) for a in do_shapes]
    weights = (wbd, wao, wpo, wout)
    grad_shapes = [(D_MODEL, D_MODEL), (GROUP_W, D_MODEL), (POOL_W, D_MODEL), (POOL_W, POOL_W)]
    tile_buf = pltpu.VMEM((ts + HALO, POOL_W), F32)
    outs = pl.pallas_call(
        body, name="mix_bwd", grid=(n,),
        in_specs=[pl.BlockSpec((ts, D_MODEL), lambda j: (idx(j), 0)),
                  pl.BlockSpec((ts, REST_W), lambda j: (idx(j), 0)),
                  pl.BlockSpec((HALO, POOL_W), lambda j: (jnp.maximum(idx(j) * (ts // HALO) - 1, 0), 0))]
                 + _dil_specs(ts, GROUP_W, idx) + _dil_specs(ts, GROUP_W, idx)
                 + [pl.BlockSpec((1, POOL_W), lambda j: (0, 0))] + [ANY] * 4,
        out_specs=[pl.BlockSpec((ts, REST_W), lambda j: (idx(j), 0))]
                  + _dil_specs(ts, GROUP_W, idx) + _dil_specs(ts, GROUP_W, idx)
                  + [pl.BlockSpec((1, POOL_W), lambda j: (0, 0))] + [ANY] * 4,
        out_shape=[jax.ShapeDtypeStruct((s, REST_W), BF16)] + do_shapes + c_shapes
                  + [jax.ShapeDtypeStruct((1, POOL_W), F32)]
                  + [jax.ShapeDtypeStruct(shape, F32) for shape in grad_shapes],
        scratch_shapes=[tile_buf] * 8 + [pltpu.VMEM((GROUP_W // LANES, ts, LANES), F32)]
                       + [pltpu.VMEM(w.shape, BF16) for w in weights]
                       + [pltpu.VMEM(shape, F32) for shape in grad_shapes]
                       + [pltpu.SemaphoreType.DMA((4,))],
        compiler_params=_params(dimension_semantics=("arbitrary",)),
    )(dh1, zr, zr, *o_dil, *l_dil, scale, wbd, wao, wpo, wout)
    dzr, do_dil, c_dil, g_scale = outs[0], outs[1:4], outs[4:7], outs[7]
    g_out, g_ao, g_po, g_bd = outs[8:]
    return dzr, do_dil, c_dil, (g_out, g_ao, g_po, g_bd, g_scale)


FF_CHUNK = 1024


def _mlp_fwd_bwd(h1, tgt, g2, g3, wmi, wmo, ts):
    s = h1.shape[0]
    n = s // ts
    nchunk = D_FF // FF_CHUNK

    def body(h1_ref, t_ref, g2_ref, g3_ref, wmi_hbm, wmo_hbm,
             dh1_ref, mt_ref, dh2t_ref, hid_ref, df_ref, loss_ref, dg2_ref, dg3_ref,
             wmi, wmo, relu_buf, sem):
        i = pl.program_id(0)
        _load_resident(i, [(wmi_hbm, wmi), (wmo_hbm, wmo)], sem)

        @pl.when(i == 0)
        def _():
            loss_ref[...] = jnp.zeros_like(loss_ref)
            dg2_ref[...] = jnp.zeros_like(dg2_ref)
            dg3_ref[...] = jnp.zeros_like(dg3_ref)

        h1 = h1_ref[...]
        g2 = g2_ref[...]
        g3 = g3_ref[...]
        r2 = _rms(h1)
        n2 = h1 * r2
        m = n2 * g2
        mb = m.astype(BF16)
        mt_ref[...] = m.T.astype(BF16)
        h2 = h1
        for c in range(nchunk):
            cols = slice(c * FF_CHUNK, (c + 1) * FF_CHUNK)
            rl = jnp.maximum(_dot(mb, wmi[:, cols]), 0.0)
            relu_buf[:, cols] = rl
            hb = (rl * rl).astype(BF16)
            hid_ref[:, cols] = hb
            h2 = h2 + _dot(hb, wmo[cols, :])
        r3 = _rms(h2)
        n3 = h2 * r3
        diff = n3 * g3 - t_ref[...]
        loss_ref[...] += jnp.sum(0.5 * jnp.sum(diff * diff, axis=1, keepdims=True) / D_MODEL,
                                 axis=0, keepdims=True)
        dy = diff * (1.0 / D_MODEL)
        dg3_ref[...] += jnp.sum(dy * n3, axis=0, keepdims=True)
        dh2 = _rms_bwd(dy * g3, n3, r3)
        dh2b = dh2.astype(BF16)
        dh2t_ref[...] = dh2.T.astype(BF16)
        dm = jnp.zeros((ts, D_MODEL), F32)
        for c in range(nchunk):
            cols = slice(c * FF_CHUNK, (c + 1) * FF_CHUNK)
            dfb = (_dot_nt(dh2b, wmo[cols, :]) * (2.0 * relu_buf[:, cols])).astype(BF16)
            df_ref[:, cols] = dfb
            dm = dm + _dot_nt(dfb, wmi[:, cols])
        dg2_ref[...] += jnp.sum(dm * n2, axis=0, keepdims=True)
        dh1_ref[...] = dh2 + _rms_bwd(dm * g2, n2, r2)

    row = lambda w: pl.BlockSpec((ts, w), lambda i: (i, 0))
    colb = pl.BlockSpec((D_MODEL, ts), lambda i: (0, i))
    vec = pl.BlockSpec((1, D_MODEL), lambda i: (0, 0))
    return pl.pallas_call(
        body, name="mlp_fwd_bwd", grid=(n,),
        in_specs=[row(D_MODEL), row(D_MODEL), vec, vec, ANY, ANY],
        out_specs=[row(D_MODEL), colb, colb, row(D_FF), row(D_FF),
                   pl.BlockSpec((1, 1), lambda i: (0, 0)), vec, vec],
        out_shape=[jax.ShapeDtypeStruct((s, D_MODEL), F32),
                   jax.ShapeDtypeStruct((D_MODEL, s), BF16), jax.ShapeDtypeStruct((D_MODEL, s), BF16),
                   jax.ShapeDtypeStruct((s, D_FF), BF16), jax.ShapeDtypeStruct((s, D_FF), BF16),
                   jax.ShapeDtypeStruct((1, 1), F32),
                   jax.ShapeDtypeStruct((1, D_MODEL), F32), jax.ShapeDtypeStruct((1, D_MODEL), F32)],
        scratch_shapes=[pltpu.VMEM((D_MODEL, D_FF), BF16), pltpu.VMEM((D_FF, D_MODEL), BF16),
                        pltpu.VMEM((ts, D_FF), F32), pltpu.SemaphoreType.DMA((2,))],
        compiler_params=_params(dimension_semantics=("arbitrary",)),
    )(h1, tgt, g2, g3, wmi, wmo)


def _in_proj_bwd(x, dh1, dzr, dqkv_dil, g1, w_in, ts):
    s = x.shape[0]
    n = s // ts

    def body(x_ref, dh_ref, dzr_ref, q0, q1, q2, g_ref, w_hbm, dx_ref, dz_ref, dg_ref, w_ref, qbuf, sem):
        i = pl.program_id(0)
        _load_resident(i, [(w_hbm, w_ref)], sem)

        @pl.when(i == 0)
        def _():
            dg_ref[...] = jnp.zeros_like(dg_ref)

        for g, dqkv in enumerate(_gather_rows((q0, q1, q2), qbuf, ts)):
            for sec in range(3):
                c0 = sec * ATT_W + g * GROUP_W
                dz_ref[:, c0:c0 + GROUP_W] = dqkv[:, sec * GROUP_W:(sec + 1) * GROUP_W].astype(BF16)
        dz_ref[:, 3 * ATT_W:] = dzr_ref[...]
        du = _dot_nt(dz_ref[...], w_ref[...])
        x = x_ref[...]
        r1 = _rms(x)
        n1 = x * r1
        g1 = g_ref[...]
        dg_ref[...] += jnp.sum(du * n1, axis=0, keepdims=True)
        dx_ref[...] = dh_ref[...] + _rms_bwd(du * g1, n1, r1)

    row = lambda w: pl.BlockSpec((ts, w), lambda i: (i, 0))
    vec = pl.BlockSpec((1, D_MODEL), lambda i: (0, 0))
    return pl.pallas_call(
        body, name="in_proj_bwd", grid=(n,),
        in_specs=[row(D_MODEL), row(D_MODEL), row(REST_W)] + _dil_specs(ts, ATT_W, lambda i: i) + [vec, ANY],
        out_specs=[row(D_MODEL), row(N_IN), vec],
        out_shape=[jax.ShapeDtypeStruct((s, D_MODEL), F32), jax.ShapeDtypeStruct((s, N_IN), BF16),
                   jax.ShapeDtypeStruct((1, D_MODEL), F32)],
        scratch_shapes=[pltpu.VMEM((D_MODEL, N_IN), BF16), pltpu.VMEM((ATT_W // LANES, ts, LANES), F32),
                        pltpu.SemaphoreType.DMA((1,))],
        compiler_params=_params(dimension_semantics=("arbitrary",)),
    )(x, dh1, dzr, *dqkv_dil, g1, w_in)


def _weight_grad(at, b, name, transpose_out, tk=2048):
    m, s = at.shape
    nn = b.shape[1]
    tn = nn // N_DEV
    tk = min(tk, s)
    nk = s // tk
    oshape = (tn, m) if transpose_out else (m, tn)
    owner = lambda jj: N_DEV - 1 - jj
    order = jnp.stack([_linear(_peer(_my_place(), owner(jj))) for jj in range(N_DEV)]).astype(jnp.int32)

    def body(order_ref, at_ref, b_ref, got_ref, acc, res, send_sems, recv_sems, local_sem):
        j, k = pl.program_id(0), pl.program_id(1)
        me = _my_place()
        mine = _linear(me)

        def send(jj):
            return pltpu.make_async_remote_copy(
                src_ref=res.at[jj % 2], dst_ref=got_ref.at[mine], send_sem=send_sems.at[jj], recv_sem=recv_sems.at[jj],
                device_id=_peer(me, owner(jj)), device_id_type=pl.DeviceIdType.MESH)

        @pl.when(k == 0)
        def _():
            acc[...] = jnp.zeros_like(acc)

        acc[...] += _dot(at_ref[...], b_ref[...])

        @pl.when(k == nk - 1)
        def _():
            for jj in range(2, N_DEV):
                @pl.when(j == jj)
                def _():
                    send(jj - 2).wait_send()

            r = acc[...]
            res[j % 2] = (r.T if transpose_out else r).astype(BF16)
            for jj in range(N_DEV - 1):
                @pl.when(j == jj)
                def _():
                    send(jj).start()

            @pl.when(j == N_DEV - 1)
            def _():
                own = pltpu.make_async_copy(res.at[(N_DEV - 1) % 2], got_ref.at[mine], local_sem.at[0])
                own.start()
                send(N_DEV - 2).wait_send()
                for jj in range(N_DEV - 1):
                    send(jj).wait_recv()
                own.wait()

    return pl.pallas_call(
        body, name=name,
        grid_spec=pltpu.PrefetchScalarGridSpec(
            num_scalar_prefetch=1, grid=(N_DEV, nk),
            in_specs=[pl.BlockSpec((m, tk), lambda j, k, o: (0, k)),
                      pl.BlockSpec((tk, tn), lambda j, k, o: (k, o[j]))],
            out_specs=ANY,
            scratch_shapes=[pltpu.VMEM((m, tn), F32), pltpu.VMEM((2,) + oshape, BF16),
                            pltpu.SemaphoreType.DMA((N_DEV - 1,)), pltpu.SemaphoreType.DMA((N_DEV - 1,)),
                            pltpu.SemaphoreType.DMA((1,))]),
        out_shape=jax.ShapeDtypeStruct((N_DEV,) + oshape, BF16),
        compiler_params=_params(dimension_semantics=("arbitrary", "arbitrary")),
    )(order, at, b)


def _my_place():
    x, y, c = lax.axis_index("x"), lax.axis_index("y"), lax.axis_index("c")
    return x, y, c


def _peer(place, k):
    x, y, c = place
    return (1 - x if k & 4 else x, 1 - y if k & 2 else y, 1 - c if k & 1 else c)


def _linear(place):
    x, y, c = place
    return 4 * x + 2 * y + c


class _Exchange:
    def __init__(self, arrays, gather):
        self.arrays, self.gather, self.nw = list(arrays), list(gather), len(arrays)
        self.out_shape = []
        for a, g in zip(arrays, gather):
            block = a.shape if g else a.shape[1:]
            self.out_shape.append(jax.ShapeDtypeStruct((N_DEV,) + tuple(block), a.dtype))
        self.specs = [ANY] * self.nw
        self.scratch = [pltpu.SemaphoreType.DMA((self.nw, N_DEV - 1)), pltpu.SemaphoreType.DMA((self.nw, N_DEV - 1)),
                        pltpu.SemaphoreType.DMA((self.nw,))]

    def _copies(self, ins, outs, sems):
        send_sems, recv_sems, local_sems = sems
        me = _my_place()
        mine = _linear(me)
        copies = []
        for w in range(self.nw):
            src = ins[w] if self.gather[w] else ins[w].at[mine]
            copies.append(pltpu.make_async_copy(src, outs[w].at[mine], local_sems.at[w]))
        for k in range(1, N_DEV):
            peer = _peer(me, k)
            for w in range(self.nw):
                src = ins[w] if self.gather[w] else ins[w].at[_linear(peer)]
                copies.append(pltpu.make_async_remote_copy(
                    src_ref=src, dst_ref=outs[w].at[mine],
                    send_sem=send_sems.at[w, k - 1], recv_sem=recv_sems.at[w, k - 1],
                    device_id=peer, device_id_type=pl.DeviceIdType.MESH))
        return copies

    def start(self, ins, outs, sems):
        for cp in self._copies(ins, outs, sems):
            cp.start()

    def wait(self, ins, outs, sems):
        copies = self._copies(ins, outs, sems)
        for cp in copies[self.nw:]:
            cp.wait_recv()
        for cp in copies[self.nw:]:
            cp.wait_send()
        for cp in copies[:self.nw]:
            cp.wait()


def _exchange(arrays, gather, name):
    ex = _Exchange(arrays, gather)

    def body(*refs):
        ins, outs, sems = refs[:ex.nw], refs[ex.nw:2 * ex.nw], refs[2 * ex.nw:]
        ex.start(ins, outs, sems)
        ex.wait(ins, outs, sems)

    return pl.pallas_call(
        body, name=name, in_specs=ex.specs, out_specs=ex.specs, out_shape=ex.out_shape, scratch_shapes=ex.scratch,
    )(*arrays)


def _adamw(parts, w, m, v, name, tr):
    rows, cols = w.shape
    tr = min(tr, rows)

    def body(p_ref, w_ref, m_ref, v_ref, g_ref, d_ref, nm_ref, nv_ref):
        g = p_ref[0].astype(F32)
        for j in range(1, N_DEV):
            g = g + p_ref[j].astype(F32)
        nm = ADAM_B1 * m_ref[...] + (1.0 - ADAM_B1) * g
        nv = ADAM_B2 * v_ref[...] + (1.0 - ADAM_B2) * (g * g)
        m_hat = nm / (1.0 - ADAM_B1 ** ADAM_STEP)
        v_hat = nv / (1.0 - ADAM_B2 ** ADAM_STEP)
        g_ref[...] = g
        d_ref[...] = -ADAM_LR * (m_hat / (jnp.sqrt(v_hat) + ADAM_EPS) + ADAM_WD * w_ref[...])
        nm_ref[...] = nm
        nv_ref[...] = nv

    blk = pl.BlockSpec((tr, cols), lambda i: (i, 0))
    return pl.pallas_call(
        body, name=name, grid=(rows // tr,),
        in_specs=[pl.BlockSpec((N_DEV, tr, cols), lambda i: (0, i, 0)), blk, blk, blk],
        out_specs=[blk] * 4,
        out_shape=[jax.ShapeDtypeStruct((rows, cols), F32)] * 4,
        compiler_params=_params(dimension_semantics=("arbitrary",)),
    )(parts, w, m, v)


def _pack_small(values):
    rows = []
    for v in values:
        size = math.prod(v.shape)
        padded = -(-size // 1024) * 1024
        flat = jnp.pad(v.reshape(-1).astype(F32), (0, padded - size))
        rows.append(flat.reshape(padded // 128, 128))
    return jnp.concatenate(rows, axis=0) if len(rows) > 1 else rows[0]


def _unpack_small(packed, shapes):
    out, r = [], 0
    for shape in shapes:
        size = math.prod(shape)
        nrow = -(-size // 1024) * 8
        out.append(packed[r:r + nrow].reshape(-1)[:size].reshape(shape))
        r += nrow
    return out


def _block_diag(w_grp):
    z = jnp.zeros((POOL_GW, POOL_GW), w_grp.dtype)
    return jnp.concatenate(
        [jnp.concatenate([w_grp[g] if c == g else z for c in range(4)], axis=1) for g in range(4)], axis=0)


def kernel(x, norm_mix_g, w_in, w_att_out, w_pool_grp, pool_scale, w_pool_out, w_out, norm_mlp_g, w_mlp_in, w_mlp_out, norm_final_g, loss_target, m_norm_mix_g, m_w_in, m_w_att_out, m_w_pool_grp, m_pool_scale, m_w_pool_out, m_w_out, m_norm_mlp_g, m_w_mlp_in, m_w_mlp_out, m_norm_final_g, v_norm_mix_g, v_w_in, v_w_att_out, v_w_pool_grp, v_pool_scale, v_w_pool_out, v_w_out, v_norm_mlp_g, v_w_mlp_in, v_w_mlp_out, v_norm_final_g):
    x, tgt = x[0], loss_target[0]
    s = x.shape[0]
    g1, g2, g3 = norm_mix_g, norm_mlp_g, norm_final_g.reshape(1, D_MODEL)
    shards = [w_in[0], w_att_out[0], w_pool_out[0], w_out[0], w_mlp_in[0], w_mlp_out[0]]
    wire = [a.astype(BF16) for a in shards]
    cols = lambda a: jnp.transpose(a, (1, 0, 2)).reshape(a.shape[1], N_DEV * a.shape[2])
    rows = lambda a: a.reshape(N_DEV * a.shape[1], a.shape[2])
    blocks_of_cols = lambda a: jnp.transpose(a.reshape(a.shape[0], N_DEV, a.shape[1] // N_DEV), (1, 0, 2))
    blocks_of_rows = lambda a: a.reshape(N_DEV, a.shape[0] // N_DEV, a.shape[1])
    wbd = _block_diag(w_pool_grp[0]).astype(BF16)

    f_in = cols(_exchange(wire[:1], [True], "gather_w_in")[0])
    (ut, qkv0, qkv1, qkv2, zr), later = _in_proj_fwd(x, g1, f_in, 512, _Exchange(wire[1:], [True] * 5))
    f_ao, f_po, f_out, f_mi, f_mo = cols(later[0]), cols(later[1]), rows(later[2]), cols(later[3]), rows(later[4])
    qkv_dil = (qkv0, qkv1, qkv2)
    flat = lambda a: a.reshape(s, a.shape[-1])
    shaped = lambda a, g: a if g == 0 else a.reshape(ATT_GROUPS[g][1], s // ATT_GROUPS[g][1], a.shape[-1])
    o_dil, l_dil = [], []
    for g in range(N_GROUPS):
        o, l = _attn_fwd(flat(qkv_dil[g]), g)
        o_dil.append(shaped(o, g))
        l_dil.append(shaped(l, g))
    h1 = _mix_fwd(x, zr, o_dil, l_dil, wbd, pool_scale, f_ao, f_po, f_out, ts=256)

    dh1, mt, dh2t, hid, df, loss, dg2, dg3 = _mlp_fwd_bwd(h1, tgt, g2, g3, f_mi, f_mo, ts=256)
    got = {"w_mlp_in": _weight_grad(mt, df, "grad_w_mlp_in", transpose_out=False),
           "w_mlp_out": _weight_grad(dh2t, hid, "grad_w_mlp_out", transpose_out=True)}
    dzr, do_dil, c_dil, (g_out, g_ao, g_po, g_bd, g_scale) = _mix_bwd(
        dh1, zr, o_dil, l_dil, wbd, pool_scale, f_ao, f_po, f_out, ts=256)
    g_grp = jnp.stack([g_bd[g * POOL_GW:(g + 1) * POOL_GW, g * POOL_GW:(g + 1) * POOL_GW] for g in range(4)])
    early = _Exchange([blocks_of_cols(g_ao).astype(BF16), blocks_of_cols(g_po).astype(BF16),
                       blocks_of_rows(g_out).astype(BF16), _pack_small([g_grp])], [False, False, False, True])
    dqkv0, (got["w_att_out"], got["w_pool_out"], got["w_out"], got_grp) = _attn_bwd(
        flat(qkv_dil[0]), flat(do_dil[0]), flat(c_dil[0]), 0, early)
    dqkv_dil = [dqkv0] + [shaped(_attn_bwd(flat(qkv_dil[g]), flat(do_dil[g]), flat(c_dil[g]), g), g) for g in (1, 2)]
    dx, dz, dg1 = _in_proj_bwd(x, dh1, dzr, dqkv_dil, g1, f_in, ts=256)
    got["w_in"] = _weight_grad(ut, dz, "grad_w_in", transpose_out=False)
    got_vec = _exchange([_pack_small([loss, dg1, g_scale, dg2, dg3])], [True], "gather_small_grads")[0]

    names = ["w_in", "w_att_out", "w_pool_out", "w_out", "w_mlp_in", "w_mlp_out"]
    ms = [m_w_in, m_w_att_out, m_w_pool_out, m_w_out, m_w_mlp_in, m_w_mlp_out]
    vs = [v_w_in, v_w_att_out, v_w_pool_out, v_w_out, v_w_mlp_in, v_w_mlp_out]
    upd = {}
    for k, name in enumerate(names):
        res = _adamw(got[name], shards[k], ms[k][0], vs[k][0], "adamw_" + name, tr=256)
        upd[name] = [a[None] for a in res]

    res = _adamw(got_grp, _pack_small([w_pool_grp]), _pack_small([m_w_pool_grp]), _pack_small([v_w_pool_grp]),
                 "adamw_w_pool_grp", tr=2048)
    upd["w_pool_grp"] = [_unpack_small(a, [w_pool_grp.shape])[0] for a in res]
    vec_w = [jnp.zeros((1,), F32), norm_mix_g, pool_scale, norm_mlp_g, norm_final_g]
    vec_m = [jnp.zeros((1,), F32), m_norm_mix_g, m_pool_scale, m_norm_mlp_g, m_norm_final_g]
    vec_v = [jnp.ones((1,), F32), v_norm_mix_g, v_pool_scale, v_norm_mlp_g, v_norm_final_g]
    res = _adamw(got_vec, _pack_small(vec_w), _pack_small(vec_m), _pack_small(vec_v), "adamw_vectors", tr=2048)
    shapes = [(), norm_mix_g.shape, pool_scale.shape, norm_mlp_g.shape, norm_final_g.shape]
    unpacked = [_unpack_small(a, shapes) for a in res]
    for k, name in enumerate(["loss", "norm_mix_g", "pool_scale", "norm_mlp_g", "norm_final_g"]):
        upd[name] = [unpacked[q][k] for q in range(4)]

    order = ["norm_mix_g", "w_in", "w_att_out", "w_pool_grp", "pool_scale", "w_pool_out", "w_out", "norm_mlp_g",
             "w_mlp_in", "w_mlp_out", "norm_final_g"]
    out = [upd["loss"][0], dx[None]]
    for q in range(4):
        out += [upd[name][q] for name in order]
    return tuple(out)
```

```python
import functools
import math

import jax
import jax.numpy as jnp
from jax import lax
from jax.experimental import pallas as pl
from jax.experimental.pallas import tpu as pltpu

F32 = jnp.float32
BF16 = jnp.bfloat16

D_MODEL = 1024
HEAD_DIM = 64
GROUP_W = 256
ATT_GROUPS = ((128, 1), (512, 4), (2048, 16))
N_GROUPS = 3
BLK = 128
ATT_W = 768
POOL_W = 768
POOL_GW = 192
D_FF = 4096
N_IN = 5120
REST_W = N_IN - 3 * ATT_W
NORM_EPS = 1e-6
ALIBI_MAX_BIAS = 8.0
N_DEV = 8
HALO = 32

ADAM_LR = 0.001
ADAM_B1 = 0.9
ADAM_B2 = 0.999
ADAM_EPS = 1e-08
ADAM_WD = 0.01
ADAM_STEP = 10

VMEM_LIMIT = 56 * 1024 * 1024
ANY = pl.BlockSpec(memory_space=pl.ANY)


def _params(**kw):
    return pltpu.CompilerParams(vmem_limit_bytes=VMEM_LIMIT, **kw)


def _dot(a, b):
    return jnp.dot(a, b, preferred_element_type=F32)


def _dot_nt(a, b):
    return lax.dot_general(a, b, (((1,), (1,)), ((), ())), preferred_element_type=F32)


def _dot_tn(a, b):
    return lax.dot_general(a, b, (((0,), (0,)), ((), ())), preferred_element_type=F32)


def _slope(head):
    return 2.0 ** (-ALIBI_MAX_BIAS * (head + 1.0) / 12.0)


def _load_resident(step, pairs, sem):
    @pl.when(step == 0)
    def _():
        copies = [pltpu.make_async_copy(src, dst, sem.at[n]) for n, (src, dst) in enumerate(pairs)]
        for cp in copies:
            cp.start()
        for cp in copies:
            cp.wait()


LANES = 128


def _to_class_order(value, dil, buf, ref, col0):
    ts, w = value.shape
    if dil == 1:
        ref[:, col0:col0 + w] = value.astype(ref.dtype)
        return
    for c in range(w // LANES):
        buf[c] = value[:, c * LANES:(c + 1) * LANES]
        for r in range(dil):
            ref[r, :, col0 + c * LANES:col0 + (c + 1) * LANES] = (
                buf[c, pl.ds(r, ts // dil, stride=dil), :].astype(ref.dtype))


def _to_token_order(ref, dil, buf, ts):
    if dil == 1:
        return ref[...].astype(F32)
    w = ref.shape[-1]
    for c in range(w // LANES):
        for r in range(dil):
            buf[c, pl.ds(r, ts // dil, stride=dil), :] = ref[r, :, c * LANES:(c + 1) * LANES].astype(F32)
    return jnp.concatenate([buf[c] for c in range(w // LANES)], axis=1)


def _rms(x):
    return lax.rsqrt(jnp.mean(x * x, axis=-1, keepdims=True) + NORM_EPS)


def _rms_bwd(dn, n, r):
    return r * (dn - n * jnp.mean(dn * n, axis=-1, keepdims=True))


def _split_refs(refs, counts):
    out, at = [], 0
    for c in counts:
        out.append(refs[at:at + c])
        at += c
    return out


def _carry_start(ex, step, ins, outs, sems):
    @pl.when(step == 0)
    def _():
        ex.start(ins, outs, sems)


def _carry_wait(ex, step, last, ins, outs, sems):
    @pl.when(step == last)
    def _():
        ex.wait(ins, outs, sems)


def _rms_u(x, g1, ts, ex):
    s = x.shape[0]
    n = s // ts

    def body(*refs):
        (x_ref, g_ref), ex_ins, (u_ref, ut_ref), ex_outs, ex_sems = _split_refs(refs, (2, ex.nw, 2, ex.nw, 3))
        i = pl.program_id(0)
        _carry_start(ex, i, ex_ins, ex_outs, ex_sems)
        x = x_ref[...]
        u = x * _rms(x) * g_ref[...]
        u_ref[...] = u.astype(BF16)
        ut_ref[...] = u.T.astype(BF16)
        _carry_wait(ex, i, n - 1, ex_ins, ex_outs, ex_sems)

    outs = pl.pallas_call(
        body, name="rms_u", grid=(n,),
        in_specs=[pl.BlockSpec((ts, D_MODEL), lambda i: (i, 0)), pl.BlockSpec((1, D_MODEL), lambda i: (0, 0))]
                 + ex.specs,
        out_specs=[pl.BlockSpec((ts, D_MODEL), lambda i: (i, 0)), pl.BlockSpec((D_MODEL, ts), lambda i: (0, i))]
                  + ex.specs,
        out_shape=[jax.ShapeDtypeStruct((s, D_MODEL), BF16), jax.ShapeDtypeStruct((D_MODEL, s), BF16)] + ex.out_shape,
        scratch_shapes=ex.scratch,
        compiler_params=_params(dimension_semantics=("arbitrary",)),
    )(x, g1, *ex.arrays)
    return outs[:2], outs[2:]


def _in_proj_fwd(u, w_in, ts, ex):
    s = u.shape[0]
    n = s // ts
    dils = [d for _, d in ATT_GROUPS]

    def body(*refs):
        (u_ref, w_hbm), ex_ins, (q0_ref, q1_ref, q2_ref, zr_ref), ex_outs, (w_ref, zbuf, sem), ex_sems = (
            _split_refs(refs, (2, ex.nw, 4, ex.nw, 3, 3)))
        i = pl.program_id(0)
        _carry_start(ex, i, ex_ins, ex_outs, ex_sems)
        _load_resident(i, [(w_hbm, w_ref)], sem)
        ub = u_ref[...]
        outs = (q0_ref, q1_ref, q2_ref)
        for sec in range(3):
            for g in range(N_GROUPS):
                c0 = sec * ATT_W + g * GROUP_W
                zc = _dot(ub, w_ref[:, c0:c0 + GROUP_W])
                _to_class_order(zc, dils[g], zbuf, outs[g], sec * GROUP_W)
        for c0 in range(0, REST_W, 256):
            zr_ref[:, c0:c0 + 256] = _dot(ub, w_ref[:, 3 * ATT_W + c0:3 * ATT_W + c0 + 256]).astype(BF16)
        _carry_wait(ex, i, n - 1, ex_ins, ex_outs, ex_sems)

    outs = pl.pallas_call(
        body, name="in_proj_fwd", grid=(n,),
        in_specs=[pl.BlockSpec((ts, D_MODEL), lambda i: (i, 0)), ANY] + ex.specs,
        out_specs=[pl.BlockSpec((ts, ATT_W), lambda i: (i, 0)),
                   pl.BlockSpec((4, ts // 4, ATT_W), lambda i: (0, i, 0)),
                   pl.BlockSpec((16, ts // 16, ATT_W), lambda i: (0, i, 0)),
                   pl.BlockSpec((ts, REST_W), lambda i: (i, 0))] + ex.specs,
        out_shape=[jax.ShapeDtypeStruct((s, ATT_W), BF16),
                   jax.ShapeDtypeStruct((4, s // 4, ATT_W), BF16),
                   jax.ShapeDtypeStruct((16, s // 16, ATT_W), BF16),
                   jax.ShapeDtypeStruct((s, REST_W), BF16)] + ex.out_shape,
        scratch_shapes=[pltpu.VMEM((D_MODEL, N_IN), BF16), pltpu.VMEM((GROUP_W // LANES, ts, LANES), F32),
                        pltpu.SemaphoreType.DMA((1,))] + ex.scratch,
        compiler_params=_params(dimension_semantics=("arbitrary",)),
    )(u, w_in, *ex.arrays)
    return outs[:4], outs[4:]


ATT_TILE = 4 * BLK


HEADS = GROUP_W // HEAD_DIM
STACK = HEADS * BLK


SCORE_SCALE = HEAD_DIM ** -0.5


def _band_consts(group, dil):
    row = lax.broadcasted_iota(jnp.int32, (STACK, 2 * BLK), 0)
    kj = lax.broadcasted_iota(jnp.int32, (STACK, 2 * BLK), 1)
    head = row // BLK
    steps = BLK + (row % BLK) - kj
    slope = jnp.full((STACK, 2 * BLK), _slope(4 * group + HEADS - 1), F32)
    for h in range(HEADS - 1):
        slope = jnp.where(head == h, _slope(4 * group + h), slope)
    in_band = (steps >= 0) & (steps <= BLK)
    return jnp.where(in_band, slope * (steps.astype(F32) * float(dil)), jnp.inf), kj


def _first_key(block, nbc, nb, b):
    if nbc % nb == 0 and b != 0:
        return None
    return jnp.where((block % nbc) != 0, 0, BLK)


def _head_of_col():
    return lax.broadcasted_iota(jnp.int32, (BLK, GROUP_W), 1) // HEAD_DIM


def _stack_heads(xb):
    head_of = _head_of_col()
    return jnp.concatenate([jnp.where(head_of == h, xb, jnp.zeros_like(xb)) for h in range(HEADS)], axis=0)


def _unstack_heads(y):
    head_of = _head_of_col()
    acc = y[0:BLK]
    for h in range(1, HEADS):
        acc = jnp.where(head_of == h, y[h * BLK:(h + 1) * BLK], acc)
    return acc


def _per_head_cols(stacked, rhs, scale_rows=None):
    lane = lax.broadcasted_iota(jnp.int32, (BLK, LANES), 1)
    halves = []
    for pair in range(HEADS // 2):
        tile = rhs[:, pair * LANES:(pair + 1) * LANES]
        parts = []
        for h in (2 * pair, 2 * pair + 1):
            part = _dot(stacked[h * BLK:(h + 1) * BLK], tile)
            parts.append(part if scale_rows is None else part * scale_rows[h * BLK:(h + 1) * BLK])
        halves.append(jnp.where(lane < HEAD_DIM, parts[0], parts[1]))
    return jnp.concatenate(halves, axis=1)


def _per_head_rows(col):
    lane = lax.broadcasted_iota(jnp.int32, (BLK, LANES), 1)
    return jnp.concatenate(
        [jnp.where(lane < HEAD_DIM, col[(2 * pair) * BLK:(2 * pair + 1) * BLK], col[(2 * pair + 1) * BLK:(2 * pair + 2) * BLK])
         for pair in range(HEADS // 2)], axis=1)


def _band_softmax(qs, kb, penalty, kj, first_key):
    sc = _dot_nt(qs, kb) - penalty
    if first_key is not None:
        sc = jnp.where(kj >= first_key, sc, -jnp.inf)
    mx = jnp.max(sc, axis=1, keepdims=True)
    e = jnp.exp(sc - mx)
    return e, mx, jnp.sum(e, axis=1, keepdims=True)


def _attn_fwd(qkv, group):
    s = qkv.shape[0]
    dil = ATT_GROUPS[group][1]
    nbc = s // (BLK * dil)
    nb = ATT_TILE // BLK

    def body(q_ref, kc_ref, kp_ref, vc_ref, vp_ref, o_ref, l_ref, kbuf, vbuf):
        i = pl.program_id(0)
        kbuf[0:BLK] = kp_ref[...]
        kbuf[BLK:BLK + ATT_TILE] = kc_ref[...]
        vbuf[0:BLK] = vp_ref[...]
        vbuf[BLK:BLK + ATT_TILE] = vc_ref[...]
        penalty, kj = _band_consts(group, dil)
        for b in range(nb):
            kb = kbuf[b * BLK:b * BLK + 2 * BLK, :]
            vb = vbuf[b * BLK:b * BLK + 2 * BLK, :]
            qs = _stack_heads(q_ref[b * BLK:(b + 1) * BLK, :] * SCORE_SCALE)
            e, mx, den = _band_softmax(qs, kb, penalty, kj, _first_key(i * nb + b, nbc, nb, b))
            o_ref[b * BLK:(b + 1) * BLK, :] = _per_head_cols(e.astype(BF16), vb, 1.0 / den)
            l_ref[b * BLK:(b + 1) * BLK, :] = _per_head_rows(mx + jnp.log(den))

    n = s // ATT_TILE
    cur = lambda c: pl.BlockSpec((ATT_TILE, GROUP_W), lambda i: (i, c))
    prev = lambda c: pl.BlockSpec((BLK, GROUP_W), lambda i: (jnp.maximum(i * nb - 1, 0), c))
    return pl.pallas_call(
        body, name=f"attn_fwd_g{group}", grid=(n,),
        in_specs=[cur(0), cur(1), prev(1), cur(2), prev(2)],
        out_specs=[cur(0), cur(0)],
        out_shape=[jax.ShapeDtypeStruct((s, GROUP_W), F32), jax.ShapeDtypeStruct((s, GROUP_W), F32)],
        scratch_shapes=[pltpu.VMEM((BLK + ATT_TILE, GROUP_W), BF16), pltpu.VMEM((BLK + ATT_TILE, GROUP_W), BF16)],
        compiler_params=_params(dimension_semantics=("arbitrary",)),
    )(qkv, qkv, qkv, qkv, qkv)


def _attn_bwd(qkv, do, corr, group, ex=None):
    s = qkv.shape[0]
    dil = ATT_GROUPS[group][1]
    nbc = s // (BLK * dil)
    nb = ATT_TILE // BLK
    n = s // ATT_TILE
    nex = ex.nw if ex else 0

    def body(*refs):
        ((q_ref, kc_ref, kp_ref, vc_ref, vp_ref, do_ref, c_ref), ex_ins, (out_ref,), ex_outs,
         (kbuf, vbuf, dkbuf, dvbuf, dqpend, dkpend, dvpend), ex_sems) = _split_refs(refs, (7, nex, 1, nex, 7, 3 if ex else 0))
        i = pl.program_id(0)
        if ex:
            _carry_start(ex, i, ex_ins, ex_outs, ex_sems)

        @pl.when(i == 0)
        def _():
            dqpend[...] = jnp.zeros_like(dqpend)
            dkpend[...] = jnp.zeros_like(dkpend)
            dvpend[...] = jnp.zeros_like(dvpend)

        out_ref[:, 0:GROUP_W] = dqpend[...]
        dkbuf[...] = jnp.zeros_like(dkbuf)
        dvbuf[...] = jnp.zeros_like(dvbuf)

        @pl.when(i < n)
        def _():
            kbuf[0:BLK] = kp_ref[...]
            kbuf[BLK:BLK + ATT_TILE] = kc_ref[...]
            vbuf[0:BLK] = vp_ref[...]
            vbuf[BLK:BLK + ATT_TILE] = vc_ref[...]
            penalty, kj = _band_consts(group, dil)
            head_of = _head_of_col()
            for b in range(nb):
                rows = slice(b * BLK, (b + 1) * BLK)
                band = slice(b * BLK, b * BLK + 2 * BLK)
                kb = kbuf[band, :]
                vb = vbuf[band, :]
                qs = _stack_heads(q_ref[rows, :] * SCORE_SCALE)
                dos = _stack_heads(do_ref[rows, :])
                cb = c_ref[rows, :]
                cor = jnp.concatenate(
                    [jnp.max(jnp.where(head_of == h, cb, -jnp.inf), axis=1, keepdims=True) for h in range(HEADS)],
                    axis=0)
                e, _, den = _band_softmax(qs, kb, penalty, kj, _first_key(i * nb + b, nbc, nb, b))
                p = e * (1.0 / den)
                ds = (p * (_dot_nt(dos, vb) + cor)).astype(BF16)
                dqpend[rows, :] = _per_head_cols(ds, kb * SCORE_SCALE).astype(BF16)
                dkbuf[band, :] += _dot_tn(ds, qs)
                dvbuf[band, :] += _dot_tn(p.astype(BF16), dos)

        tail = slice(ATT_TILE - BLK, ATT_TILE)
        dkpend[tail, :] += dkbuf[0:BLK, :]
        dvpend[tail, :] += dvbuf[0:BLK, :]
        out_ref[:, GROUP_W:2 * GROUP_W] = dkpend[...].astype(BF16)
        out_ref[:, 2 * GROUP_W:] = dvpend[...].astype(BF16)
        dkpend[...] = dkbuf[BLK:BLK + ATT_TILE, :]
        dvpend[...] = dvbuf[BLK:BLK + ATT_TILE, :]
        if ex:
            _carry_wait(ex, i, n, ex_ins, ex_outs, ex_sems)

    last = n - 1
    cur = lambda c: pl.BlockSpec((ATT_TILE, GROUP_W), lambda i: (jnp.minimum(i, last), c))
    prev = lambda c: pl.BlockSpec(
        (BLK, GROUP_W), lambda i: (jnp.maximum(jnp.minimum(i, last) * nb - 1, 0), c))
    outs = pl.pallas_call(
        body, name=f"attn_bwd_g{group}", grid=(n + 1,),
        in_specs=[cur(0), cur(1), prev(1), cur(2), prev(2), cur(0), cur(0)] + (ex.specs if ex else []),
        out_specs=[pl.BlockSpec((ATT_TILE, ATT_W), lambda i: (jnp.maximum(i - 1, 0), 0))] + (ex.specs if ex else []),
        out_shape=[jax.ShapeDtypeStruct((s, ATT_W), BF16)] + (ex.out_shape if ex else []),
        scratch_shapes=[pltpu.VMEM((BLK + ATT_TILE, GROUP_W), BF16), pltpu.VMEM((BLK + ATT_TILE, GROUP_W), BF16),
                        pltpu.VMEM((BLK + ATT_TILE, GROUP_W), F32), pltpu.VMEM((BLK + ATT_TILE, GROUP_W), F32),
                        pltpu.VMEM((ATT_TILE, GROUP_W), BF16),
                        pltpu.VMEM((ATT_TILE, GROUP_W), F32), pltpu.VMEM((ATT_TILE, GROUP_W), F32)]
                       + (ex.scratch if ex else []),
        compiler_params=_params(dimension_semantics=("arbitrary",)),
    )(qkv, qkv, qkv, qkv, qkv, do, corr, *(ex.arrays if ex else []))
    return (outs[0], outs[1:]) if ex else outs[0]


def _dil_specs(ts, width, idx):
    return [pl.BlockSpec((ts, width), lambda i: (idx(i), 0)),
            pl.BlockSpec((4, ts // 4, width), lambda i: (0, idx(i), 0)),
            pl.BlockSpec((16, ts // 16, width), lambda i: (0, idx(i), 0))]


def _gather_rows(refs, buf, ts):
    return [_to_token_order(refs[g], ATT_GROUPS[g][1], buf, ts) for g in range(N_GROUPS)]


def _pool_fwd(ebuf, s2, s4, s8, t0, ts):
    n = ts + HALO
    s2[8:n] = ebuf[8:n] + ebuf[7:n - 1]
    s4[16:n] = s2[16:n] + s2[14:n - 2]
    s8[24:n] = s4[24:n] + s4[20:n - 4]
    s16 = s8[32:n] + s8[24:n - 8]
    col = lax.broadcasted_iota(jnp.int32, (ts, POOL_W), 1)
    psum = jnp.where(col < POOL_GW, s2[32:n],
                     jnp.where(col < 2 * POOL_GW, s4[32:n], jnp.where(col < 3 * POOL_GW, s8[32:n], s16)))
    win = jnp.where(col < POOL_GW, 2, jnp.where(col < 2 * POOL_GW, 4, jnp.where(col < 3 * POOL_GW, 8, 16)))
    t = t0 + lax.broadcasted_iota(jnp.int32, (ts, POOL_W), 0)
    count = jnp.minimum(t + 1, win).astype(F32)
    return psum / count - ebuf[32:n], count


def _mix_core(zr, pooled, outs, lses, wbd, scale, wao, wpo):
    mixed = _dot(pooled.astype(BF16), wbd)
    p = mixed * scale
    l0, l1, l2 = lses
    mx = jnp.maximum(jnp.maximum(l0, l1), l2)
    e0, e1, e2 = jnp.exp(l0 - mx), jnp.exp(l1 - mx), jnp.exp(l2 - mx)
    inv = 1.0 / (e0 + e1 + e2)
    wts = (e0 * inv, e1 * inv, e2 * inv)
    a = wts[0] * outs[0] + wts[1] * outs[1] + wts[2] * outs[2]
    att = _dot(a.astype(BF16), wao)
    pol = _dot(p.astype(BF16), wpo)
    sga = jax.nn.sigmoid(zr[:, POOL_W:POOL_W + D_MODEL].astype(F32))
    sgp = jax.nn.sigmoid(zr[:, POOL_W + D_MODEL:].astype(F32))
    mg = sga * att + sgp * pol
    return dict(mixed=mixed, p=p, wts=wts, a=a, att=att, pol=pol, sga=sga, sgp=sgp, mg=mg)


def _fill_pool_input(ebuf, zr_ref, halo_ref, t0):
    ts = zr_ref.shape[0]
    halo = halo_ref[...].astype(F32)
    t = t0 - HALO + lax.broadcasted_iota(jnp.int32, (HALO, POOL_W), 0)
    ebuf[0:HALO] = jnp.where(t >= 0, halo, 0.0)
    ebuf[HALO:HALO + ts] = zr_ref[:, 0:POOL_W].astype(F32)


def _mix_fwd(x, zr, o_dil, l_dil, wbd, scale, wao, wpo, wout, ts):
    s = x.shape[0]
    n = s // ts

    def body(x_ref, zr_ref, halo_ref, o0, o1, o2, l0, l1, l2, wbd_ref, sc_ref, wao_ref, wpo_ref, wout_ref,
             h1_ref, ebuf, s2, s4, s8, rbuf):
        i = pl.program_id(0)
        _fill_pool_input(ebuf, zr_ref, halo_ref, i * ts)
        pooled, _ = _pool_fwd(ebuf, s2, s4, s8, i * ts, ts)
        outs = _gather_rows((o0, o1, o2), rbuf, ts)
        lses = _gather_rows((l0, l1, l2), rbuf, ts)
        f = _mix_core(zr_ref[...], pooled, outs, lses, wbd_ref[...], sc_ref[...], wao_ref[...], wpo_ref[...])
        h1_ref[...] = x_ref[...] + _dot(f["mg"].astype(BF16), wout_ref[...])

    whole = lambda a: pl.BlockSpec(a.shape, lambda i: (0,) * a.ndim)
    idx = lambda i: i
    return pl.pallas_call(
        body, name="mix_fwd", grid=(n,),
        in_specs=[pl.BlockSpec((ts, D_MODEL), lambda i: (i, 0)),
                  pl.BlockSpec((ts, REST_W), lambda i: (i, 0)),
                  pl.BlockSpec((HALO, POOL_W), lambda i: (jnp.maximum(i * (ts // HALO) - 1, 0), 0))]
                 + _dil_specs(ts, GROUP_W, idx) + _dil_specs(ts, GROUP_W, idx)
                 + [whole(wbd), whole(scale), whole(wao), whole(wpo), whole(wout)],
        out_specs=pl.BlockSpec((ts, D_MODEL), lambda i: (i, 0)),
        out_shape=jax.ShapeDtypeStruct((s, D_MODEL), F32),
        scratch_shapes=[pltpu.VMEM((ts + HALO, POOL_W), F32)] * 4
                       + [pltpu.VMEM((GROUP_W // LANES, ts, LANES), F32)],
        compiler_params=_params(dimension_semantics=("arbitrary",)),
    )(x, zr, zr, *o_dil, *l_dil, wbd, scale, wao, wpo, wout)


def _mix_bwd(dh1, zr, o_dil, l_dil, wbd, scale, wao, wpo, wout, ts):
    s = dh1.shape[0]
    n = s // ts

    def body(dh_ref, zr_ref, halo_ref, o0, o1, o2, l0, l1, l2, sc_ref, wbd_hbm, wao_hbm, wpo_hbm, wout_hbm,
             dzr_ref, do0, do1, do2, c0, c1, c2, gsc_ref, gwout_hbm, gwao_hbm, gwpo_hbm, gwbd_hbm,
             ebuf, s2, s4, s8, gbuf, t2, t4, t8, rbuf,
             wbd_ref, wao_ref, wpo_ref, wout_ref, gwout_ref, gwao_ref, gwpo_ref, gwbd_ref, sem):
        j = pl.program_id(0)
        i = n - 1 - j
        _load_resident(j, [(wbd_hbm, wbd_ref), (wao_hbm, wao_ref), (wpo_hbm, wpo_ref), (wout_hbm, wout_ref)], sem)

        @pl.when(j == 0)
        def _():
            gwout_ref[...] = jnp.zeros_like(gwout_ref)
            gwao_ref[...] = jnp.zeros_like(gwao_ref)
            gwpo_ref[...] = jnp.zeros_like(gwpo_ref)
            gwbd_ref[...] = jnp.zeros_like(gwbd_ref)
            gsc_ref[...] = jnp.zeros_like(gsc_ref)
            gbuf[ts:ts + HALO] = jnp.zeros((HALO, POOL_W), F32)

        _fill_pool_input(ebuf, zr_ref, halo_ref, i * ts)
        pooled, count = _pool_fwd(ebuf, s2, s4, s8, i * ts, ts)
        outs = _gather_rows((o0, o1, o2), rbuf, ts)
        lses = _gather_rows((l0, l1, l2), rbuf, ts)
        zr = zr_ref[...]
        wbd, wao, wpo, wout = wbd_ref[...], wao_ref[...], wpo_ref[...], wout_ref[...]
        scale = sc_ref[...]
        f = _mix_core(zr, pooled, outs, lses, wbd, scale, wao, wpo)

        dhb = dh_ref[...].astype(BF16)
        gwout_ref[...] += _dot(f["mg"].T.astype(BF16), dhb)
        dmg = _dot_nt(dhb, wout)
        sga, sgp, att, pol = f["sga"], f["sgp"], f["att"], f["pol"]
        datt = dmg * sga
        dpol = dmg * sgp
        dzr_ref[:, POOL_W:POOL_W + D_MODEL] = (dmg * att * sga * (1.0 - sga)).astype(BF16)
        dzr_ref[:, POOL_W + D_MODEL:] = (dmg * pol * sgp * (1.0 - sgp)).astype(BF16)
        dattb = datt.astype(BF16)
        dpolb = dpol.astype(BF16)
        gwao_ref[...] += _dot(f["a"].T.astype(BF16), dattb)
        gwpo_ref[...] += _dot(f["p"].T.astype(BF16), dpolb)
        da = _dot_nt(dattb, wao)
        dp = _dot_nt(dpolb, wpo)

        gsc_ref[...] += jnp.sum(f["mixed"] * dp, axis=0, keepdims=True)
        dmixed = (dp * scale).astype(BF16)
        gwbd_ref[...] += _dot(pooled.T.astype(BF16), dmixed)
        dpooled = _dot_nt(dmixed, wbd)
        gbuf[0:ts] = dpooled / count
        m = ts + HALO
        t2[0:m - 8] = gbuf[0:m - 8] + gbuf[1:m - 7]
        t4[0:m - 16] = t2[0:m - 16] + t2[2:m - 14]
        t8[0:m - 24] = t4[0:m - 24] + t4[4:m - 20]
        t16 = t8[0:ts] + t8[8:ts + 8]
        col = lax.broadcasted_iota(jnp.int32, (ts, POOL_W), 1)
        back = jnp.where(col < POOL_GW, t2[0:ts],
                         jnp.where(col < 2 * POOL_GW, t4[0:ts], jnp.where(col < 3 * POOL_GW, t8[0:ts], t16)))
        dzr_ref[:, 0:POOL_W] = (back - dpooled).astype(BF16)
        gbuf[ts:ts + HALO] = gbuf[0:HALO]

        head_of = lax.broadcasted_iota(jnp.int32, (ts, GROUP_W), 1) // HEAD_DIM
        prod = da * f["a"]
        inner = jnp.zeros((ts, GROUP_W), F32)
        for h in range(4):
            hm = head_of == h
            tot = jnp.sum(jnp.where(hm, prod, 0.0), axis=1, keepdims=True)
            inner = jnp.where(hm, tot, inner)
        for g, (do_ref, c_ref) in enumerate(((do0, c0), (do1, c1), (do2, c2))):
            dil = ATT_GROUPS[g][1]
            _to_class_order(f["wts"][g] * da, dil, rbuf, do_ref, 0)
            _to_class_order(-f["wts"][g] * inner, dil, rbuf, c_ref, 0)

        @pl.when(j == n - 1)
        def _():
            pairs = ((gwout_ref, gwout_hbm), (gwao_ref, gwao_hbm), (gwpo_ref, gwpo_hbm), (gwbd_ref, gwbd_hbm))
            copies = [pltpu.make_async_copy(src, dst, sem.at[k]) for k, (src, dst) in enumerate(pairs)]
            for cp in copies:
                cp.start()
            for cp in copies:
                cp.wait()

    idx = lambda j: n - 1 - j
    do_shapes = [jax.ShapeDtypeStruct((s, GROUP_W), BF16), jax.ShapeDtypeStruct((4, s // 4, GROUP_W), BF16),
                 jax.ShapeDtypeStruct((16, s // 16, GROUP_W), BF16)]
    c_shapes = [jax.ShapeDtypeStruct(a.shape, F32) for a in do_shapes]
    weights = (wbd, wao, wpo, wout)
    grad_shapes = [(D_MODEL, D_MODEL), (GROUP_W, D_MODEL), (POOL_W, D_MODEL), (POOL_W, POOL_W)]
    tile_buf = pltpu.VMEM((ts + HALO, POOL_W), F32)
    outs = pl.pallas_call(
        body, name="mix_bwd", grid=(n,),
        in_specs=[pl.BlockSpec((ts, D_MODEL), lambda j: (idx(j), 0)),
                  pl.BlockSpec((ts, REST_W), lambda j: (idx(j), 0)),
                  pl.BlockSpec((HALO, POOL_W), lambda j: (jnp.maximum(idx(j) * (ts // HALO) - 1, 0), 0))]
                 + _dil_specs(ts, GROUP_W, idx) + _dil_specs(ts, GROUP_W, idx)
                 + [pl.BlockSpec((1, POOL_W), lambda j: (0, 0))] + [ANY] * 4,
        out_specs=[pl.BlockSpec((ts, REST_W), lambda j: (idx(j), 0))]
                  + _dil_specs(ts, GROUP_W, idx) + _dil_specs(ts, GROUP_W, idx)
                  + [pl.BlockSpec((1, POOL_W), lambda j: (0, 0))] + [ANY] * 4,
        out_shape=[jax.ShapeDtypeStruct((s, REST_W), BF16)] + do_shapes + c_shapes
                  + [jax.ShapeDtypeStruct((1, POOL_W), F32)]
                  + [jax.ShapeDtypeStruct(shape, F32) for shape in grad_shapes],
        scratch_shapes=[tile_buf] * 8 + [pltpu.VMEM((GROUP_W // LANES, ts, LANES), F32)]
                       + [pltpu.VMEM(w.shape, BF16) for w in weights]
                       + [pltpu.VMEM(shape, F32) for shape in grad_shapes]
                       + [pltpu.SemaphoreType.DMA((4,))],
        compiler_params=_params(dimension_semantics=("arbitrary",)),
    )(dh1, zr, zr, *o_dil, *l_dil, scale, wbd, wao, wpo, wout)
    dzr, do_dil, c_dil, g_scale = outs[0], outs[1:4], outs[4:7], outs[7]
    g_out, g_ao, g_po, g_bd = outs[8:]
    return dzr, do_dil, c_dil, (g_out, g_ao, g_po, g_bd, g_scale)


FF_CHUNK = 1024


def _mlp_fwd_bwd(h1, tgt, g2, g3, wmi, wmo, ts):
    s = h1.shape[0]
    n = s // ts
    nchunk = D_FF // FF_CHUNK

    def body(h1_ref, t_ref, g2_ref, g3_ref, wmi_hbm, wmo_hbm,
             dh1_ref, mt_ref, dh2t_ref, hid_ref, df_ref, loss_ref, dg2_ref, dg3_ref,
             wmi, wmo, relu_buf, sem):
        i = pl.program_id(0)
        _load_resident(i, [(wmi_hbm, wmi), (wmo_hbm, wmo)], sem)

        @pl.when(i == 0)
        def _():
            loss_ref[...] = jnp.zeros_like(loss_ref)
            dg2_ref[...] = jnp.zeros_like(dg2_ref)
            dg3_ref[...] = jnp.zeros_like(dg3_ref)

        h1 = h1_ref[...]
        g2 = g2_ref[...]
        g3 = g3_ref[...]
        r2 = _rms(h1)
        n2 = h1 * r2
        m = n2 * g2
        mb = m.astype(BF16)
        mt_ref[...] = m.T.astype(BF16)
        h2 = h1
        for c in range(nchunk):
            cols = slice(c * FF_CHUNK, (c + 1) * FF_CHUNK)
            rl = jnp.maximum(_dot(mb, wmi[:, cols]), 0.0)
            relu_buf[:, cols] = rl
            hb = (rl * rl).astype(BF16)
            hid_ref[:, cols] = hb
            h2 = h2 + _dot(hb, wmo[cols, :])
        r3 = _rms(h2)
        n3 = h2 * r3
        diff = n3 * g3 - t_ref[...]
        loss_ref[...] += jnp.sum(0.5 * jnp.sum(diff * diff, axis=1, keepdims=True) / D_MODEL,
                                 axis=0, keepdims=True)
        dy = diff * (1.0 / D_MODEL)
        dg3_ref[...] += jnp.sum(dy * n3, axis=0, keepdims=True)
        dh2 = _rms_bwd(dy * g3, n3, r3)
        dh2b = dh2.astype(BF16)
        dh2t_ref[...] = dh2.T.astype(BF16)
        dm = jnp.zeros((ts, D_MODEL), F32)
        for c in range(nchunk):
            cols = slice(c * FF_CHUNK, (c + 1) * FF_CHUNK)
            dfb = (_dot_nt(dh2b, wmo[cols, :]) * (2.0 * relu_buf[:, cols])).astype(BF16)
            df_ref[:, cols] = dfb
            dm = dm + _dot_nt(dfb, wmi[:, cols])
        dg2_ref[...] += jnp.sum(dm * n2, axis=0, keepdims=True)
        dh1_ref[...] = dh2 + _rms_bwd(dm * g2, n2, r2)

    row = lambda w: pl.BlockSpec((ts, w), lambda i: (i, 0))
    colb = pl.BlockSpec((D_MODEL, ts), lambda i: (0, i))
    vec = pl.BlockSpec((1, D_MODEL), lambda i: (0, 0))
    return pl.pallas_call(
        body, name="mlp_fwd_bwd", grid=(n,),
        in_specs=[row(D_MODEL), row(D_MODEL), vec, vec, ANY, ANY],
        out_specs=[row(D_MODEL), colb, colb, row(D_FF), row(D_FF),
                   pl.BlockSpec((1, 1), lambda i: (0, 0)), vec, vec],
        out_shape=[jax.ShapeDtypeStruct((s, D_MODEL), F32),
                   jax.ShapeDtypeStruct((D_MODEL, s), BF16), jax.ShapeDtypeStruct((D_MODEL, s), BF16),
                   jax.ShapeDtypeStruct((s, D_FF), BF16), jax.ShapeDtypeStruct((s, D_FF), BF16),
                   jax.ShapeDtypeStruct((1, 1), F32),
                   jax.ShapeDtypeStruct((1, D_MODEL), F32), jax.ShapeDtypeStruct((1, D_MODEL), F32)],
        scratch_shapes=[pltpu.VMEM((D_MODEL, D_FF), BF16), pltpu.VMEM((D_FF, D_MODEL), BF16),
                        pltpu.VMEM((ts, D_FF), F32), pltpu.SemaphoreType.DMA((2,))],
        compiler_params=_params(dimension_semantics=("arbitrary",)),
    )(h1, tgt, g2, g3, wmi, wmo)


def _in_proj_bwd(x, dh1, dzr, dqkv_dil, g1, w_in, ts):
    s = x.shape[0]
    n = s // ts

    def body(x_ref, dh_ref, dzr_ref, q0, q1, q2, g_ref, w_hbm, dx_ref, dz_ref, dg_ref, w_ref, qbuf, sem):
        i = pl.program_id(0)
        _load_resident(i, [(w_hbm, w_ref)], sem)

        @pl.when(i == 0)
        def _():
            dg_ref[...] = jnp.zeros_like(dg_ref)

        for g, dqkv in enumerate(_gather_rows((q0, q1, q2), qbuf, ts)):
            for sec in range(3):
                c0 = sec * ATT_W + g * GROUP_W
                dz_ref[:, c0:c0 + GROUP_W] = dqkv[:, sec * GROUP_W:(sec + 1) * GROUP_W].astype(BF16)
        dz_ref[:, 3 * ATT_W:] = dzr_ref[...]
        du = _dot_nt(dz_ref[...], w_ref[...])
        x = x_ref[...]
        r1 = _rms(x)
        n1 = x * r1
        g1 = g_ref[...]
        dg_ref[...] += jnp.sum(du * n1, axis=0, keepdims=True)
        dx_ref[...] = dh_ref[...] + _rms_bwd(du * g1, n1, r1)

    row = lambda w: pl.BlockSpec((ts, w), lambda i: (i, 0))
    vec = pl.BlockSpec((1, D_MODEL), lambda i: (0, 0))
    return pl.pallas_call(
        body, name="in_proj_bwd", grid=(n,),
        in_specs=[row(D_MODEL), row(D_MODEL), row(REST_W)] + _dil_specs(ts, ATT_W, lambda i: i) + [vec, ANY],
        out_specs=[row(D_MODEL), row(N_IN), vec],
        out_shape=[jax.ShapeDtypeStruct((s, D_MODEL), F32), jax.ShapeDtypeStruct((s, N_IN), BF16),
                   jax.ShapeDtypeStruct((1, D_MODEL), F32)],
        scratch_shapes=[pltpu.VMEM((D_MODEL, N_IN), BF16), pltpu.VMEM((ATT_W // LANES, ts, LANES), F32),
                        pltpu.SemaphoreType.DMA((1,))],
        compiler_params=_params(dimension_semantics=("arbitrary",)),
    )(x, dh1, dzr, *dqkv_dil, g1, w_in)


def _weight_grad(at, b, name, transpose_out, tk=2048):
    m, s = at.shape
    nn = b.shape[1]
    tn = nn // N_DEV
    tk = min(tk, s)
    nk = s // tk
    oshape = (tn, m) if transpose_out else (m, tn)
    owner = lambda jj: N_DEV - 1 - jj
    order = jnp.stack([_linear(_peer(_my_place(), owner(jj))) for jj in range(N_DEV)]).astype(jnp.int32)

    def body(order_ref, at_ref, b_ref, got_ref, acc, res, send_sems, recv_sems, local_sem):
        j, k = pl.program_id(0), pl.program_id(1)
        me = _my_place()
        mine = _linear(me)

        def send(jj):
            return pltpu.make_async_remote_copy(
                src_ref=res.at[jj % 2], dst_ref=got_ref.at[mine], send_sem=send_sems.at[jj], recv_sem=recv_sems.at[jj],
                device_id=_peer(me, owner(jj)), device_id_type=pl.DeviceIdType.MESH)

        @pl.when(k == 0)
        def _():
            acc[...] = jnp.zeros_like(acc)

        acc[...] += _dot(at_ref[...], b_ref[...])

        @pl.when(k == nk - 1)
        def _():
            for jj in range(2, N_DEV):
                @pl.when(j == jj)
                def _():
                    send(jj - 2).wait_send()

            r = acc[...]
            res[j % 2] = (r.T if transpose_out else r).astype(BF16)
            for jj in range(N_DEV - 1):
                @pl.when(j == jj)
                def _():
                    send(jj).start()

            @pl.when(j == N_DEV - 1)
            def _():
                own = pltpu.make_async_copy(res.at[(N_DEV - 1) % 2], got_ref.at[mine], local_sem.at[0])
                own.start()
                send(N_DEV - 2).wait_send()
                for jj in range(N_DEV - 1):
                    send(jj).wait_recv()
                own.wait()

    return pl.pallas_call(
        body, name=name,
        grid_spec=pltpu.PrefetchScalarGridSpec(
            num_scalar_prefetch=1, grid=(N_DEV, nk),
            in_specs=[pl.BlockSpec((m, tk), lambda j, k, o: (0, k)),
                      pl.BlockSpec((tk, tn), lambda j, k, o: (k, o[j]))],
            out_specs=ANY,
            scratch_shapes=[pltpu.VMEM((m, tn), F32), pltpu.VMEM((2,) + oshape, BF16),
                            pltpu.SemaphoreType.DMA((N_DEV - 1,)), pltpu.SemaphoreType.DMA((N_DEV - 1,)),
                            pltpu.SemaphoreType.DMA((1,))]),
        out_shape=jax.ShapeDtypeStruct((N_DEV,) + oshape, BF16),
        compiler_params=_params(dimension_semantics=("arbitrary", "arbitrary")),
    )(order, at, b)


def _my_place():
    x, y, c = lax.axis_index("x"), lax.axis_index("y"), lax.axis_index("c")
    return x, y, c


def _peer(place, k):
    x, y, c = place
    return (1 - x if k & 4 else x, 1 - y if k & 2 else y, 1 - c if k & 1 else c)


def _linear(place):
    x, y, c = place
    return 4 * x + 2 * y + c


class _Exchange:
    def __init__(self, arrays, gather):
        self.arrays, self.gather, self.nw = list(arrays), list(gather), len(arrays)
        self.out_shape = []
        for a, g in zip(arrays, gather):
            block = a.shape if g else a.shape[1:]
            self.out_shape.append(jax.ShapeDtypeStruct((N_DEV,) + tuple(block), a.dtype))
        self.specs = [ANY] * self.nw
        self.scratch = [pltpu.SemaphoreType.DMA((self.nw, N_DEV - 1)), pltpu.SemaphoreType.DMA((self.nw, N_DEV - 1)),
                        pltpu.SemaphoreType.DMA((self.nw,))]

    def _copies(self, ins, outs, sems):
        send_sems, recv_sems, local_sems = sems
        me = _my_place()
        mine = _linear(me)
        copies = []
        for w in range(self.nw):
            src = ins[w] if self.gather[w] else ins[w].at[mine]
            copies.append(pltpu.make_async_copy(src, outs[w].at[mine], local_sems.at[w]))
        for k in range(1, N_DEV):
            peer = _peer(me, k)
            for w in range(self.nw):
                src = ins[w] if self.gather[w] else ins[w].at[_linear(peer)]
                copies.append(pltpu.make_async_remote_copy(
                    src_ref=src, dst_ref=outs[w].at[mine],
                    send_sem=send_sems.at[w, k - 1], recv_sem=recv_sems.at[w, k - 1],
                    device_id=peer, device_id_type=pl.DeviceIdType.MESH))
        return copies

    def start(self, ins, outs, sems):
        for cp in self._copies(ins, outs, sems):
            cp.start()

    def wait(self, ins, outs, sems):
        copies = self._copies(ins, outs, sems)
        for cp in copies[self.nw:]:
            cp.wait_recv()
        for cp in copies[self.nw:]:
            cp.wait_send()
        for cp in copies[:self.nw]:
            cp.wait()


class _Gather:
    def __init__(self, arrays):
        self.arrays, self.nw = list(arrays), len(arrays)
        self.out_shape = [jax.ShapeDtypeStruct((N_DEV,) + tuple(a.shape), a.dtype) for a in arrays]
        self.specs = [ANY] * self.nw
        self.scratch = [pltpu.SemaphoreType.DMA((self.nw, N_DEV - 1)), pltpu.SemaphoreType.DMA((self.nw, N_DEV - 1)),
                        pltpu.SemaphoreType.DMA((self.nw,))]

    @staticmethod
    def _places():
        x, y, c = _my_place()
        return (x, y, c), (x, y, 1 - c), [(1 - x, y), (x, 1 - y), (1 - x, 1 - y)]

    @staticmethod
    def _copy(outs, sems, w, k, block, to, src=None):
        rows = outs[w].at[_linear(block)]
        return pltpu.make_async_remote_copy(
            src_ref=rows if src is None else src, dst_ref=rows, send_sem=sems[0].at[w, k], recv_sem=sems[1].at[w, k],
            device_id=to, device_id_type=pl.DeviceIdType.MESH)

    def _first(self, ins, outs, sems, w):
        me, sibling, chips = self._places()
        return ([self._copy(outs, sems, w, 0, me, sibling, src=ins[w])]
                + [self._copy(outs, sems, w, 1 + j, me, (*chip, me[2]), src=ins[w]) for j, chip in enumerate(chips)])

    def _passed(self, outs, sems, w):
        me, sibling, chips = self._places()
        return [self._copy(outs, sems, w, 4 + j, (*chip, me[2]), sibling) for j, chip in enumerate(chips)]

    def _local(self, ins, outs, sems, w):
        return pltpu.make_async_copy(ins[w], outs[w].at[_linear(self._places()[0])], sems[2].at[w])

    def start(self, ins, outs, sems):
        for w in range(self.nw):
            self._local(ins, outs, sems, w).start()
            for cp in self._first(ins, outs, sems, w):
                cp.start()

    def wait(self, ins, outs, sems):
        me, sibling, chips = self._places()
        for j, chip in enumerate(chips):
            for w in range(self.nw):
                self._copy(outs, sems, w, 1 + j, (*chip, me[2]), me).wait_recv()
                self._passed(outs, sems, w)[j].start()
        for w in range(self.nw):
            self._copy(outs, sems, w, 0, sibling, me).wait_recv()
            for j, chip in enumerate(chips):
                self._copy(outs, sems, w, 4 + j, (*chip, sibling[2]), me).wait_recv()
            for cp in self._first(ins, outs, sems, w) + self._passed(outs, sems, w):
                cp.wait_send()
            self._local(ins, outs, sems, w).wait()


def _exchange(arrays, gather, name):
    ex = _Exchange(arrays, gather)

    def body(*refs):
        ins, outs, sems = refs[:ex.nw], refs[ex.nw:2 * ex.nw], refs[2 * ex.nw:]
        ex.start(ins, outs, sems)
        ex.wait(ins, outs, sems)

    return pl.pallas_call(
        body, name=name, in_specs=ex.specs, out_specs=ex.specs, out_shape=ex.out_shape, scratch_shapes=ex.scratch,
    )(*arrays)


def _adamw(parts, w, m, v, name, tr):
    rows, cols = w.shape
    tr = min(tr, rows)

    def body(p_ref, w_ref, m_ref, v_ref, g_ref, d_ref, nm_ref, nv_ref):
        g = p_ref[0].astype(F32)
        for j in range(1, N_DEV):
            g = g + p_ref[j].astype(F32)
        nm = ADAM_B1 * m_ref[...] + (1.0 - ADAM_B1) * g
        nv = ADAM_B2 * v_ref[...] + (1.0 - ADAM_B2) * (g * g)
        m_hat = nm / (1.0 - ADAM_B1 ** ADAM_STEP)
        v_hat = nv / (1.0 - ADAM_B2 ** ADAM_STEP)
        g_ref[...] = g
        d_ref[...] = -ADAM_LR * (m_hat / (jnp.sqrt(v_hat) + ADAM_EPS) + ADAM_WD * w_ref[...])
        nm_ref[...] = nm
        nv_ref[...] = nv

    blk = pl.BlockSpec((tr, cols), lambda i: (i, 0))
    return pl.pallas_call(
        body, name=name, grid=(rows // tr,),
        in_specs=[pl.BlockSpec((N_DEV, tr, cols), lambda i: (0, i, 0)), blk, blk, blk],
        out_specs=[blk] * 4,
        out_shape=[jax.ShapeDtypeStruct((rows, cols), F32)] * 4,
        compiler_params=_params(dimension_semantics=("arbitrary",)),
    )(parts, w, m, v)


def _pack_small(values):
    rows = []
    for v in values:
        size = math.prod(v.shape)
        padded = -(-size // 1024) * 1024
        flat = jnp.pad(v.reshape(-1).astype(F32), (0, padded - size))
        rows.append(flat.reshape(padded // 128, 128))
    return jnp.concatenate(rows, axis=0) if len(rows) > 1 else rows[0]


def _unpack_small(packed, shapes):
    out, r = [], 0
    for shape in shapes:
        size = math.prod(shape)
        nrow = -(-size // 1024) * 8
        out.append(packed[r:r + nrow].reshape(-1)[:size].reshape(shape))
        r += nrow
    return out


def _block_diag(w_grp):
    z = jnp.zeros((POOL_GW, POOL_GW), w_grp.dtype)
    return jnp.concatenate(
        [jnp.concatenate([w_grp[g] if c == g else z for c in range(4)], axis=1) for g in range(4)], axis=0)


def kernel(x, norm_mix_g, w_in, w_att_out, w_pool_grp, pool_scale, w_pool_out, w_out, norm_mlp_g, w_mlp_in, w_mlp_out, norm_final_g, loss_target, m_norm_mix_g, m_w_in, m_w_att_out, m_w_pool_grp, m_pool_scale, m_w_pool_out, m_w_out, m_norm_mlp_g, m_w_mlp_in, m_w_mlp_out, m_norm_final_g, v_norm_mix_g, v_w_in, v_w_att_out, v_w_pool_grp, v_pool_scale, v_w_pool_out, v_w_out, v_norm_mlp_g, v_w_mlp_in, v_w_mlp_out, v_norm_final_g):
    x, tgt = x[0], loss_target[0]
    s = x.shape[0]
    g1, g2, g3 = norm_mix_g, norm_mlp_g, norm_final_g.reshape(1, D_MODEL)
    shards = [w_in[0], w_att_out[0], w_pool_out[0], w_out[0], w_mlp_in[0], w_mlp_out[0]]
    wire = [a.astype(BF16) for a in shards]
    cols = lambda a: jnp.transpose(a, (1, 0, 2)).reshape(a.shape[1], N_DEV * a.shape[2])
    rows = lambda a: a.reshape(N_DEV * a.shape[1], a.shape[2])
    blocks_of_cols = lambda a: jnp.transpose(a.reshape(a.shape[0], N_DEV, a.shape[1] // N_DEV), (1, 0, 2))
    blocks_of_rows = lambda a: a.reshape(N_DEV, a.shape[0] // N_DEV, a.shape[1])
    wbd = _block_diag(w_pool_grp[0]).astype(BF16)

    (u, ut), (first,) = _rms_u(x, g1, 512, _Gather(wire[:1]))
    f_in = cols(first)
    (qkv0, qkv1, qkv2, zr), later = _in_proj_fwd(u, f_in, 512, _Gather(wire[1:]))
    f_ao, f_po, f_out, f_mi, f_mo = cols(later[0]), cols(later[1]), rows(later[2]), cols(later[3]), rows(later[4])
    qkv_dil = (qkv0, qkv1, qkv2)
    flat = lambda a: a.reshape(s, a.shape[-1])
    shaped = lambda a, g: a if g == 0 else a.reshape(ATT_GROUPS[g][1], s // ATT_GROUPS[g][1], a.shape[-1])
    o_dil, l_dil = [], []
    for g in range(N_GROUPS):
        o, l = _attn_fwd(flat(qkv_dil[g]), g)
        o_dil.append(shaped(o, g))
        l_dil.append(shaped(l, g))
    h1 = _mix_fwd(x, zr, o_dil, l_dil, wbd, pool_scale, f_ao, f_po, f_out, ts=256)

    dh1, mt, dh2t, hid, df, loss, dg2, dg3 = _mlp_fwd_bwd(h1, tgt, g2, g3, f_mi, f_mo, ts=256)
    got = {"w_mlp_in": _weight_grad(mt, df, "grad_w_mlp_in", transpose_out=False),
           "w_mlp_out": _weight_grad(dh2t, hid, "grad_w_mlp_out", transpose_out=True)}
    dzr, do_dil, c_dil, (g_out, g_ao, g_po, g_bd, g_scale) = _mix_bwd(
        dh1, zr, o_dil, l_dil, wbd, pool_scale, f_ao, f_po, f_out, ts=256)
    g_grp = jnp.stack([g_bd[g * POOL_GW:(g + 1) * POOL_GW, g * POOL_GW:(g + 1) * POOL_GW] for g in range(4)])
    early = _Exchange([blocks_of_cols(g_ao).astype(BF16), blocks_of_cols(g_po).astype(BF16),
                       blocks_of_rows(g_out).astype(BF16), _pack_small([g_grp])], [False, False, False, True])
    dqkv0, (got["w_att_out"], got["w_pool_out"], got["w_out"], got_grp) = _attn_bwd(
        flat(qkv_dil[0]), flat(do_dil[0]), flat(c_dil[0]), 0, early)
    dqkv_dil = [dqkv0] + [shaped(_attn_bwd(flat(qkv_dil[g]), flat(do_dil[g]), flat(c_dil[g]), g), g) for g in (1, 2)]
    dx, dz, dg1 = _in_proj_bwd(x, dh1, dzr, dqkv_dil, g1, f_in, ts=256)
    got["w_in"] = _weight_grad(ut, dz, "grad_w_in", transpose_out=False)
    got_vec = _exchange([_pack_small([loss, dg1, g_scale, dg2, dg3])], [True], "gather_small_grads")[0]

    names = ["w_in", "w_att_out", "w_pool_out", "w_out", "w_mlp_in", "w_mlp_out"]
    ms = [m_w_in, m_w_att_out, m_w_pool_out, m_w_out, m_w_mlp_in, m_w_mlp_out]
    vs = [v_w_in, v_w_att_out, v_w_pool_out, v_w_out, v_w_mlp_in, v_w_mlp_out]
    upd = {}
    for k, name in enumerate(names):
        res = _adamw(got[name], shards[k], ms[k][0], vs[k][0], "adamw_" + name, tr=256)
        upd[name] = [a[None] for a in res]

    res = _adamw(got_grp, _pack_small([w_pool_grp]), _pack_small([m_w_pool_grp]), _pack_small([v_w_pool_grp]),
                 "adamw_w_pool_grp", tr=2048)
    upd["w_pool_grp"] = [_unpack_small(a, [w_pool_grp.shape])[0] for a in res]
    vec_w = [jnp.zeros((1,), F32), norm_mix_g, pool_scale, norm_mlp_g, norm_final_g]
    vec_m = [jnp.zeros((1,), F32), m_norm_mix_g, m_pool_scale, m_norm_mlp_g, m_norm_final_g]
    vec_v = [jnp.ones((1,), F32), v_norm_mix_g, v_pool_scale, v_norm_mlp_g, v_norm_final_g]
    res = _adamw(got_vec, _pack_small(vec_w), _pack_small(vec_m), _pack_small(vec_v), "adamw_vectors", tr=2048)
    shapes = [(), norm_mix_g.shape, pool_scale.shape, norm_mlp_g.shape, norm_final_g.shape]
    unpacked = [_unpack_small(a, shapes) for a in res]
    for k, name in enumerate(["loss", "norm_mix_g", "pool_scale", "norm_mlp_g", "norm_final_g"]):
        upd[name] = [unpacked[q][k] for q in range(4)]

    order = ["norm_mix_g", "w_in", "w_att_out", "w_pool_grp", "pool_scale", "w_pool_out", "w_out", "norm_mlp_g",
             "w_mlp_in", "w_mlp_out", "norm_final_g"]
    out = [upd["loss"][0], dx[None]]
    for q in range(4):
        out += [upd[name][q] for name in order]
    return tuple(out)
```

```python
import functools
import math

import jax
import jax.numpy as jnp
from jax import lax
from jax.experimental import pallas as pl
from jax.experimental.pallas import tpu as pltpu

F32 = jnp.float32
BF16 = jnp.bfloat16

D_MODEL = 1024
HEAD_DIM = 64
GROUP_W = 256
ATT_GROUPS = ((128, 1), (512, 4), (2048, 16))
N_GROUPS = 3
BLK = 128
ATT_W = 768
POOL_W = 768
POOL_GW = 192
D_FF = 4096
N_IN = 5120
REST_W = N_IN - 3 * ATT_W
NORM_EPS = 1e-6
ALIBI_MAX_BIAS = 8.0
N_DEV = 8
HALO = 32

ADAM_LR = 0.001
ADAM_B1 = 0.9
ADAM_B2 = 0.999
ADAM_EPS = 1e-08
ADAM_WD = 0.01
ADAM_STEP = 10

VMEM_LIMIT = 56 * 1024 * 1024
ANY = pl.BlockSpec(memory_space=pl.ANY)


def _params(**kw):
    return pltpu.CompilerParams(vmem_limit_bytes=VMEM_LIMIT, **kw)


def _dot(a, b):
    return jnp.dot(a, b, preferred_element_type=F32)


def _dot_nt(a, b):
    return lax.dot_general(a, b, (((1,), (1,)), ((), ())), preferred_element_type=F32)


def _dot_tn(a, b):
    return lax.dot_general(a, b, (((0,), (0,)), ((), ())), preferred_element_type=F32)


def _slope(head):
    return 2.0 ** (-ALIBI_MAX_BIAS * (head + 1.0) / 12.0)


def _load_resident(step, pairs, sem):
    @pl.when(step == 0)
    def _():
        copies = [pltpu.make_async_copy(src, dst, sem.at[n]) for n, (src, dst) in enumerate(pairs)]
        for cp in copies:
            cp.start()
        for cp in copies:
            cp.wait()


LANES = 128


def _to_class_order(value, dil, buf, ref, col0):
    ts, w = value.shape
    if dil == 1:
        ref[:, col0:col0 + w] = value.astype(ref.dtype)
        return
    for c in range(w // LANES):
        buf[c] = value[:, c * LANES:(c + 1) * LANES]
        for r in range(dil):
            ref[r, :, col0 + c * LANES:col0 + (c + 1) * LANES] = (
                buf[c, pl.ds(r, ts // dil, stride=dil), :].astype(ref.dtype))


def _to_token_order(ref, dil, buf, ts):
    if dil == 1:
        return ref[...].astype(F32)
    w = ref.shape[-1]
    for c in range(w // LANES):
        for r in range(dil):
            buf[c, pl.ds(r, ts // dil, stride=dil), :] = ref[r, :, c * LANES:(c + 1) * LANES].astype(F32)
    return jnp.concatenate([buf[c] for c in range(w // LANES)], axis=1)


def _rms(x):
    return lax.rsqrt(jnp.mean(x * x, axis=-1, keepdims=True) + NORM_EPS)


def _rms_bwd(dn, n, r):
    return r * (dn - n * jnp.mean(dn * n, axis=-1, keepdims=True))


def _split_refs(refs, counts):
    out, at = [], 0
    for c in counts:
        out.append(refs[at:at + c])
        at += c
    return out


def _carry_start(ex, step, ins, outs, sems):
    @pl.when(step == 0)
    def _():
        ex.start(ins, outs, sems)


def _carry_wait(ex, step, last, ins, outs, sems):
    @pl.when(step == last)
    def _():
        ex.wait(ins, outs, sems)


def _rms_u(x, g1, ts, ex):
    s = x.shape[0]
    n = s // ts

    def body(*refs):
        (x_ref, g_ref), ex_ins, (u_ref, ut_ref), ex_outs, ex_sems = _split_refs(refs, (2, ex.nw, 2, ex.nw, 3))
        i = pl.program_id(0)
        _carry_start(ex, i, ex_ins, ex_outs, ex_sems)
        x = x_ref[...]
        u = x * _rms(x) * g_ref[...]
        u_ref[...] = u.astype(BF16)
        ut_ref[...] = u.T.astype(BF16)
        _carry_wait(ex, i, n - 1, ex_ins, ex_outs, ex_sems)

    outs = pl.pallas_call(
        body, name="rms_u", grid=(n,),
        in_specs=[pl.BlockSpec((ts, D_MODEL), lambda i: (i, 0)), pl.BlockSpec((1, D_MODEL), lambda i: (0, 0))]
                 + ex.specs,
        out_specs=[pl.BlockSpec((ts, D_MODEL), lambda i: (i, 0)), pl.BlockSpec((D_MODEL, ts), lambda i: (0, i))]
                  + ex.specs,
        out_shape=[jax.ShapeDtypeStruct((s, D_MODEL), BF16), jax.ShapeDtypeStruct((D_MODEL, s), BF16)] + ex.out_shape,
        scratch_shapes=ex.scratch,
        compiler_params=_params(dimension_semantics=("arbitrary",)),
    )(x, g1, *ex.arrays)
    return outs[:2], outs[2:]


def _in_proj_fwd(u, w_in, ts, ex):
    s = u.shape[0]
    n = s // ts
    dils = [d for _, d in ATT_GROUPS]

    def body(*refs):
        (u_ref, w_hbm), ex_ins, (q0_ref, q1_ref, q2_ref, zr_ref), ex_outs, (w_ref, zbuf, sem), ex_sems = (
            _split_refs(refs, (2, ex.nw, 4, ex.nw, 3, 3)))
        i = pl.program_id(0)
        _carry_start(ex, i, ex_ins, ex_outs, ex_sems)
        _load_resident(i, [(w_hbm, w_ref)], sem)
        ub = u_ref[...]
        outs = (q0_ref, q1_ref, q2_ref)
        for sec in range(3):
            for g in range(N_GROUPS):
                c0 = sec * ATT_W + g * GROUP_W
                zc = _dot(ub, w_ref[:, c0:c0 + GROUP_W])
                _to_class_order(zc, dils[g], zbuf, outs[g], sec * GROUP_W)
        for c0 in range(0, REST_W, 256):
            zr_ref[:, c0:c0 + 256] = _dot(ub, w_ref[:, 3 * ATT_W + c0:3 * ATT_W + c0 + 256]).astype(BF16)
        _carry_wait(ex, i, n - 1, ex_ins, ex_outs, ex_sems)

    outs = pl.pallas_call(
        body, name="in_proj_fwd", grid=(n,),
        in_specs=[pl.BlockSpec((ts, D_MODEL), lambda i: (i, 0)), ANY] + ex.specs,
        out_specs=[pl.BlockSpec((ts, ATT_W), lambda i: (i, 0)),
                   pl.BlockSpec((4, ts // 4, ATT_W), lambda i: (0, i, 0)),
                   pl.BlockSpec((16, ts // 16, ATT_W), lambda i: (0, i, 0)),
                   pl.BlockSpec((ts, REST_W), lambda i: (i, 0))] + ex.specs,
        out_shape=[jax.ShapeDtypeStruct((s, ATT_W), BF16),
                   jax.ShapeDtypeStruct((4, s // 4, ATT_W), BF16),
                   jax.ShapeDtypeStruct((16, s // 16, ATT_W), BF16),
                   jax.ShapeDtypeStruct((s, REST_W), BF16)] + ex.out_shape,
        scratch_shapes=[pltpu.VMEM((D_MODEL, N_IN), BF16), pltpu.VMEM((GROUP_W // LANES, ts, LANES), F32),
                        pltpu.SemaphoreType.DMA((1,))] + ex.scratch,
        compiler_params=_params(dimension_semantics=("arbitrary",)),
    )(u, w_in, *ex.arrays)
    return outs[:4], outs[4:]


ATT_TILE = 8 * BLK


HEADS = GROUP_W // HEAD_DIM
STACK = HEADS * BLK


SCORE_SCALE = HEAD_DIM ** -0.5


def _band_consts(group, dil):
    row = lax.broadcasted_iota(jnp.int32, (STACK, 2 * BLK), 0)
    kj = lax.broadcasted_iota(jnp.int32, (STACK, 2 * BLK), 1)
    head = row // BLK
    steps = BLK + (row % BLK) - kj
    slope = jnp.full((STACK, 2 * BLK), _slope(4 * group + HEADS - 1), F32)
    for h in range(HEADS - 1):
        slope = jnp.where(head == h, _slope(4 * group + h), slope)
    in_band = (steps >= 0) & (steps <= BLK)
    return jnp.where(in_band, slope * (steps.astype(F32) * float(dil)), jnp.inf), kj


def _first_key(block, nbc, nb, b):
    if nbc % nb == 0 and b != 0:
        return None
    return jnp.where((block % nbc) != 0, 0, BLK)


def _head_of_col():
    return lax.broadcasted_iota(jnp.int32, (BLK, GROUP_W), 1) // HEAD_DIM


def _stack_heads(xb):
    head_of = _head_of_col()
    return jnp.concatenate([jnp.where(head_of == h, xb, jnp.zeros_like(xb)) for h in range(HEADS)], axis=0)


def _unstack_heads(y):
    head_of = _head_of_col()
    acc = y[0:BLK]
    for h in range(1, HEADS):
        acc = jnp.where(head_of == h, y[h * BLK:(h + 1) * BLK], acc)
    return acc


def _per_head_cols(stacked, rhs, scale_rows=None):
    lane = lax.broadcasted_iota(jnp.int32, (BLK, LANES), 1)
    halves = []
    for pair in range(HEADS // 2):
        tile = rhs[:, pair * LANES:(pair + 1) * LANES]
        parts = []
        for h in (2 * pair, 2 * pair + 1):
            part = _dot(stacked[h * BLK:(h + 1) * BLK], tile)
            parts.append(part if scale_rows is None else part * scale_rows[h * BLK:(h + 1) * BLK])
        halves.append(jnp.where(lane < HEAD_DIM, parts[0], parts[1]))
    return jnp.concatenate(halves, axis=1)


def _per_head_rows(col):
    lane = lax.broadcasted_iota(jnp.int32, (BLK, LANES), 1)
    return jnp.concatenate(
        [jnp.where(lane < HEAD_DIM, col[(2 * pair) * BLK:(2 * pair + 1) * BLK], col[(2 * pair + 1) * BLK:(2 * pair + 2) * BLK])
         for pair in range(HEADS // 2)], axis=1)


def _band_softmax(qs, kb, penalty, kj, first_key):
    sc = _dot_nt(qs, kb) - penalty
    if first_key is not None:
        sc = jnp.where(kj >= first_key, sc, -jnp.inf)
    mx = jnp.max(sc, axis=1, keepdims=True)
    e = jnp.exp(sc - mx)
    return e, mx, jnp.sum(e, axis=1, keepdims=True)


def _attn_fwd(qkv, group):
    s = qkv.shape[0]
    dil = ATT_GROUPS[group][1]
    nbc = s // (BLK * dil)
    nb = ATT_TILE // BLK

    def body(q_ref, kc_ref, kp_ref, vc_ref, vp_ref, o_ref, l_ref, kbuf, vbuf):
        i = pl.program_id(0)
        kbuf[0:BLK] = kp_ref[...]
        kbuf[BLK:BLK + ATT_TILE] = kc_ref[...]
        vbuf[0:BLK] = vp_ref[...]
        vbuf[BLK:BLK + ATT_TILE] = vc_ref[...]
        penalty, kj = _band_consts(group, dil)
        blocks = range(nb)
        qss = [_stack_heads(q_ref[b * BLK:(b + 1) * BLK, :] * SCORE_SCALE) for b in blocks]
        soft = [_band_softmax(qss[b], kbuf[b * BLK:b * BLK + 2 * BLK, :], penalty, kj,
                              _first_key(i * nb + b, nbc, nb, b)) for b in blocks]
        for b in blocks:
            e, mx, den = soft[b]
            o_ref[b * BLK:(b + 1) * BLK, :] = _per_head_cols(e.astype(BF16), vbuf[b * BLK:b * BLK + 2 * BLK, :], 1.0 / den)
        for b in blocks:
            e, mx, den = soft[b]
            l_ref[b * BLK:(b + 1) * BLK, :] = _per_head_rows(mx + jnp.log(den))

    n = s // ATT_TILE
    cur = lambda c: pl.BlockSpec((ATT_TILE, GROUP_W), lambda i: (i, c))
    prev = lambda c: pl.BlockSpec((BLK, GROUP_W), lambda i: (jnp.maximum(i * nb - 1, 0), c))
    return pl.pallas_call(
        body, name=f"attn_fwd_g{group}", grid=(n,),
        in_specs=[cur(0), cur(1), prev(1), cur(2), prev(2)],
        out_specs=[cur(0), cur(0)],
        out_shape=[jax.ShapeDtypeStruct((s, GROUP_W), F32), jax.ShapeDtypeStruct((s, GROUP_W), F32)],
        scratch_shapes=[pltpu.VMEM((BLK + ATT_TILE, GROUP_W), BF16), pltpu.VMEM((BLK + ATT_TILE, GROUP_W), BF16)],
        compiler_params=_params(dimension_semantics=("arbitrary",)),
    )(qkv, qkv, qkv, qkv, qkv)


def _attn_bwd(qkv, do, corr, group, ex=None):
    s = qkv.shape[0]
    dil = ATT_GROUPS[group][1]
    nbc = s // (BLK * dil)
    nb = ATT_TILE // BLK
    n = s // ATT_TILE
    nex = ex.nw if ex else 0

    def body(*refs):
        ((q_ref, kc_ref, kp_ref, vc_ref, vp_ref, do_ref, c_ref), ex_ins, (out_ref,), ex_outs,
         (kbuf, vbuf, dqpend, dkpend, dvpend), ex_sems) = _split_refs(refs, (7, nex, 1, nex, 5, 3 if ex else 0))
        i = pl.program_id(0)
        if ex:
            _carry_start(ex, i, ex_ins, ex_outs, ex_sems)

        @pl.when(i == 0)
        def _():
            dqpend[...] = jnp.zeros_like(dqpend)
            dkpend[...] = jnp.zeros_like(dkpend)
            dvpend[...] = jnp.zeros_like(dvpend)

        out_ref[:, 0:GROUP_W] = dqpend[...]
        body_rows = slice(0, ATT_TILE - BLK)
        tail = slice(ATT_TILE - BLK, ATT_TILE)
        pends = ((dkpend, GROUP_W), (dvpend, 2 * GROUP_W))
        for pend, c0 in pends:
            out_ref[body_rows, c0:c0 + GROUP_W] = pend[body_rows, :].astype(BF16)

        @pl.when(i < n)
        def _():
            kbuf[0:BLK] = kp_ref[...]
            kbuf[BLK:BLK + ATT_TILE] = kc_ref[...]
            vbuf[0:BLK] = vp_ref[...]
            vbuf[BLK:BLK + ATT_TILE] = vc_ref[...]
            penalty, kj = _band_consts(group, dil)
            head_of = _head_of_col()
            blocks = range(nb)
            rows = [slice(b * BLK, (b + 1) * BLK) for b in blocks]
            kbs = [kbuf[b * BLK:b * BLK + 2 * BLK, :] for b in blocks]
            vbs = [vbuf[b * BLK:b * BLK + 2 * BLK, :] for b in blocks]
            qss = [_stack_heads(q_ref[rows[b], :] * SCORE_SCALE) for b in blocks]
            doss = [_stack_heads(do_ref[rows[b], :]) for b in blocks]
            cors = []
            for b in blocks:
                cb = c_ref[rows[b], :]
                cors.append(jnp.concatenate(
                    [jnp.max(jnp.where(head_of == h, cb, -jnp.inf), axis=1, keepdims=True) for h in range(HEADS)],
                    axis=0))
            soft = [_band_softmax(qss[b], kbs[b], penalty, kj, _first_key(i * nb + b, nbc, nb, b)) for b in blocks]
            dps = [_dot_nt(doss[b], vbs[b]) for b in blocks]
            ps = [soft[b][0] * (1.0 / soft[b][2]) for b in blocks]
            dss = [(ps[b] * (dps[b] + cors[b])).astype(BF16) for b in blocks]
            for b in blocks:
                dqpend[rows[b], :] = _per_head_cols(dss[b], kbs[b] * SCORE_SCALE).astype(BF16)
            bands = [(_dot_tn(dss[b], qss[b]), _dot_tn(ps[b].astype(BF16), doss[b])) for b in blocks]
            for which, (pend, c0) in enumerate(pends):
                out_ref[tail, c0:c0 + GROUP_W] = (pend[tail, :] + bands[0][which][0:BLK]).astype(BF16)
                for b in range(nb):
                    own = bands[b][which][BLK:2 * BLK]
                    pend[b * BLK:(b + 1) * BLK, :] = own + bands[b + 1][which][0:BLK] if b + 1 < nb else own

        @pl.when(i == n)
        def _():
            for pend, c0 in pends:
                out_ref[tail, c0:c0 + GROUP_W] = pend[tail, :].astype(BF16)

        if ex:
            _carry_wait(ex, i, n, ex_ins, ex_outs, ex_sems)

    last = n - 1
    cur = lambda c: pl.BlockSpec((ATT_TILE, GROUP_W), lambda i: (jnp.minimum(i, last), c))
    prev = lambda c: pl.BlockSpec(
        (BLK, GROUP_W), lambda i: (jnp.maximum(jnp.minimum(i, last) * nb - 1, 0), c))
    outs = pl.pallas_call(
        body, name=f"attn_bwd_g{group}", grid=(n + 1,),
        in_specs=[cur(0), cur(1), prev(1), cur(2), prev(2), cur(0), cur(0)] + (ex.specs if ex else []),
        out_specs=[pl.BlockSpec((ATT_TILE, ATT_W), lambda i: (jnp.maximum(i - 1, 0), 0))] + (ex.specs if ex else []),
        out_shape=[jax.ShapeDtypeStruct((s, ATT_W), BF16)] + (ex.out_shape if ex else []),
        scratch_shapes=[pltpu.VMEM((BLK + ATT_TILE, GROUP_W), BF16), pltpu.VMEM((BLK + ATT_TILE, GROUP_W), BF16),
                        pltpu.VMEM((ATT_TILE, GROUP_W), BF16),
                        pltpu.VMEM((ATT_TILE, GROUP_W), F32), pltpu.VMEM((ATT_TILE, GROUP_W), F32)]
                       + (ex.scratch if ex else []),
        compiler_params=_params(dimension_semantics=("arbitrary",)),
    )(qkv, qkv, qkv, qkv, qkv, do, corr, *(ex.arrays if ex else []))
    return (outs[0], outs[1:]) if ex else outs[0]


def _dil_specs(ts, width, idx):
    return [pl.BlockSpec((ts, width), lambda i: (idx(i), 0)),
            pl.BlockSpec((4, ts // 4, width), lambda i: (0, idx(i), 0)),
            pl.BlockSpec((16, ts // 16, width), lambda i: (0, idx(i), 0))]


def _gather_rows(refs, buf, ts):
    return [_to_token_order(refs[g], ATT_GROUPS[g][1], buf, ts) for g in range(N_GROUPS)]


def _pool_fwd(ebuf, s2, s4, s8, t0, ts):
    n = ts + HALO
    s2[8:n] = ebuf[8:n] + ebuf[7:n - 1]
    s4[16:n] = s2[16:n] + s2[14:n - 2]
    s8[24:n] = s4[24:n] + s4[20:n - 4]
    s16 = s8[32:n] + s8[24:n - 8]
    col = lax.broadcasted_iota(jnp.int32, (ts, POOL_W), 1)
    psum = jnp.where(col < POOL_GW, s2[32:n],
                     jnp.where(col < 2 * POOL_GW, s4[32:n], jnp.where(col < 3 * POOL_GW, s8[32:n], s16)))
    win = jnp.where(col < POOL_GW, 2, jnp.where(col < 2 * POOL_GW, 4, jnp.where(col < 3 * POOL_GW, 8, 16)))
    t = t0 + lax.broadcasted_iota(jnp.int32, (ts, POOL_W), 0)
    count = jnp.minimum(t + 1, win).astype(F32)
    return psum / count - ebuf[32:n], count


def _mix_core(zr, pooled, outs, lses, wbd, scale, wao, wpo):
    mixed = _dot(pooled.astype(BF16), wbd)
    p = mixed * scale
    l0, l1, l2 = lses
    mx = jnp.maximum(jnp.maximum(l0, l1), l2)
    e0, e1, e2 = jnp.exp(l0 - mx), jnp.exp(l1 - mx), jnp.exp(l2 - mx)
    inv = 1.0 / (e0 + e1 + e2)
    wts = (e0 * inv, e1 * inv, e2 * inv)
    a = wts[0] * outs[0] + wts[1] * outs[1] + wts[2] * outs[2]
    att = _dot(a.astype(BF16), wao)
    pol = _dot(p.astype(BF16), wpo)
    sga = jax.nn.sigmoid(zr[:, POOL_W:POOL_W + D_MODEL].astype(F32))
    sgp = jax.nn.sigmoid(zr[:, POOL_W + D_MODEL:].astype(F32))
    mg = sga * att + sgp * pol
    return dict(mixed=mixed, p=p, wts=wts, a=a, att=att, pol=pol, sga=sga, sgp=sgp, mg=mg)


def _fill_pool_input(ebuf, zr_ref, halo_ref, t0):
    ts = zr_ref.shape[0]
    halo = halo_ref[...].astype(F32)
    t = t0 - HALO + lax.broadcasted_iota(jnp.int32, (HALO, POOL_W), 0)
    ebuf[0:HALO] = jnp.where(t >= 0, halo, 0.0)
    ebuf[HALO:HALO + ts] = zr_ref[:, 0:POOL_W].astype(F32)


def _mix_fwd(x, zr, o_dil, l_dil, wbd, scale, wao, wpo, wout, ts):
    s = x.shape[0]
    n = s // ts

    def body(x_ref, zr_ref, halo_ref, o0, o1, o2, l0, l1, l2, wbd_ref, sc_ref, wao_ref, wpo_ref, wout_ref,
             h1_ref, ebuf, s2, s4, s8, rbuf):
        i = pl.program_id(0)
        _fill_pool_input(ebuf, zr_ref, halo_ref, i * ts)
        pooled, _ = _pool_fwd(ebuf, s2, s4, s8, i * ts, ts)
        outs = _gather_rows((o0, o1, o2), rbuf, ts)
        lses = _gather_rows((l0, l1, l2), rbuf, ts)
        f = _mix_core(zr_ref[...], pooled, outs, lses, wbd_ref[...], sc_ref[...], wao_ref[...], wpo_ref[...])
        h1_ref[...] = x_ref[...] + _dot(f["mg"].astype(BF16), wout_ref[...])

    whole = lambda a: pl.BlockSpec(a.shape, lambda i: (0,) * a.ndim)
    idx = lambda i: i
    return pl.pallas_call(
        body, name="mix_fwd", grid=(n,),
        in_specs=[pl.BlockSpec((ts, D_MODEL), lambda i: (i, 0)),
                  pl.BlockSpec((ts, REST_W), lambda i: (i, 0)),
                  pl.BlockSpec((HALO, POOL_W), lambda i: (jnp.maximum(i * (ts // HALO) - 1, 0), 0))]
                 + _dil_specs(ts, GROUP_W, idx) + _dil_specs(ts, GROUP_W, idx)
                 + [whole(wbd), whole(scale), whole(wao), whole(wpo), whole(wout)],
        out_specs=pl.BlockSpec((ts, D_MODEL), lambda i: (i, 0)),
        out_shape=jax.ShapeDtypeStruct((s, D_MODEL), F32),
        scratch_shapes=[pltpu.VMEM((ts + HALO, POOL_W), F32)] * 4
                       + [pltpu.VMEM((GROUP_W // LANES, ts, LANES), F32)],
        compiler_params=_params(dimension_semantics=("arbitrary",)),
    )(x, zr, zr, *o_dil, *l_dil, wbd, scale, wao, wpo, wout)


def _mix_bwd(dh1, zr, o_dil, l_dil, wbd, scale, wao, wpo, wout, ts):
    s = dh1.shape[0]
    n = s // ts

    def body(dh_ref, zr_ref, halo_ref, o0, o1, o2, l0, l1, l2, sc_ref, wbd_hbm, wao_hbm, wpo_hbm, wout_hbm,
             dzr_ref, do0, do1, do2, c0, c1, c2, gsc_ref, gwout_hbm, gwao_hbm, gwpo_hbm, gwbd_hbm,
             ebuf, s2, s4, s8, gbuf, t2, t4, t8, rbuf,
             wbd_ref, wao_ref, wpo_ref, wout_ref, gwout_ref, gwao_ref, gwpo_ref, gwbd_ref, sem):
        j = pl.program_id(0)
        i = n - 1 - j
        _load_resident(j, [(wbd_hbm, wbd_ref), (wao_hbm, wao_ref), (wpo_hbm, wpo_ref), (wout_hbm, wout_ref)], sem)

        @pl.when(j == 0)
        def _():
            gwout_ref[...] = jnp.zeros_like(gwout_ref)
            gwao_ref[...] = jnp.zeros_like(gwao_ref)
            gwpo_ref[...] = jnp.zeros_like(gwpo_ref)
            gwbd_ref[...] = jnp.zeros_like(gwbd_ref)
            gsc_ref[...] = jnp.zeros_like(gsc_ref)
            gbuf[ts:ts + HALO] = jnp.zeros((HALO, POOL_W), F32)

        _fill_pool_input(ebuf, zr_ref, halo_ref, i * ts)
        pooled, count = _pool_fwd(ebuf, s2, s4, s8, i * ts, ts)
        outs = _gather_rows((o0, o1, o2), rbuf, ts)
        lses = _gather_rows((l0, l1, l2), rbuf, ts)
        zr = zr_ref[...]
        wbd, wao, wpo, wout = wbd_ref[...], wao_ref[...], wpo_ref[...], wout_ref[...]
        scale = sc_ref[...]
        f = _mix_core(zr, pooled, outs, lses, wbd, scale, wao, wpo)

        dhb = dh_ref[...].astype(BF16)
        gwout_ref[...] += _dot(f["mg"].T.astype(BF16), dhb)
        dmg = _dot_nt(dhb, wout)
        sga, sgp, att, pol = f["sga"], f["sgp"], f["att"], f["pol"]
        datt = dmg * sga
        dpol = dmg * sgp
        dzr_ref[:, POOL_W:POOL_W + D_MODEL] = (dmg * att * sga * (1.0 - sga)).astype(BF16)
        dzr_ref[:, POOL_W + D_MODEL:] = (dmg * pol * sgp * (1.0 - sgp)).astype(BF16)
        dattb = datt.astype(BF16)
        dpolb = dpol.astype(BF16)
        gwao_ref[...] += _dot(f["a"].T.astype(BF16), dattb)
        gwpo_ref[...] += _dot(f["p"].T.astype(BF16), dpolb)
        da = _dot_nt(dattb, wao)
        dp = _dot_nt(dpolb, wpo)

        gsc_ref[...] += jnp.sum(f["mixed"] * dp, axis=0, keepdims=True)
        dmixed = (dp * scale).astype(BF16)
        gwbd_ref[...] += _dot(pooled.T.astype(BF16), dmixed)
        dpooled = _dot_nt(dmixed, wbd)
        gbuf[0:ts] = dpooled / count
        m = ts + HALO
        t2[0:m - 8] = gbuf[0:m - 8] + gbuf[1:m - 7]
        t4[0:m - 16] = t2[0:m - 16] + t2[2:m - 14]
        t8[0:m - 24] = t4[0:m - 24] + t4[4:m - 20]
        t16 = t8[0:ts] + t8[8:ts + 8]
        col = lax.broadcasted_iota(jnp.int32, (ts, POOL_W), 1)
        back = jnp.where(col < POOL_GW, t2[0:ts],
                         jnp.where(col < 2 * POOL_GW, t4[0:ts], jnp.where(col < 3 * POOL_GW, t8[0:ts], t16)))
        dzr_ref[:, 0:POOL_W] = (back - dpooled).astype(BF16)
        gbuf[ts:ts + HALO] = gbuf[0:HALO]

        head_of = lax.broadcasted_iota(jnp.int32, (ts, GROUP_W), 1) // HEAD_DIM
        prod = da * f["a"]
        inner = jnp.zeros((ts, GROUP_W), F32)
        for h in range(4):
            hm = head_of == h
            tot = jnp.sum(jnp.where(hm, prod, 0.0), axis=1, keepdims=True)
            inner = jnp.where(hm, tot, inner)
        for g, (do_ref, c_ref) in enumerate(((do0, c0), (do1, c1), (do2, c2))):
            dil = ATT_GROUPS[g][1]
            _to_class_order(f["wts"][g] * da, dil, rbuf, do_ref, 0)
            _to_class_order(-f["wts"][g] * inner, dil, rbuf, c_ref, 0)

        @pl.when(j == n - 1)
        def _():
            pairs = ((gwout_ref, gwout_hbm), (gwao_ref, gwao_hbm), (gwpo_ref, gwpo_hbm), (gwbd_ref, gwbd_hbm))
            copies = [pltpu.make_async_copy(src, dst, sem.at[k]) for k, (src, dst) in enumerate(pairs)]
            for cp in copies:
                cp.start()
            for cp in copies:
                cp.wait()

    idx = lambda j: n - 1 - j
    do_shapes = [jax.ShapeDtypeStruct((s, GROUP_W), BF16), jax.ShapeDtypeStruct((4, s // 4, GROUP_W), BF16),
                 jax.ShapeDtypeStruct((16, s // 16, GROUP_W), BF16)]
    c_shapes = [jax.ShapeDtypeStruct(a.shape, F32) for a in do_shapes]
    weights = (wbd, wao, wpo, wout)
    grad_shapes = [(D_MODEL, D_MODEL), (GROUP_W, D_MODEL), (POOL_W, D_MODEL), (POOL_W, POOL_W)]
    tile_buf = pltpu.VMEM((ts + HALO, POOL_W), F32)
    outs = pl.pallas_call(
        body, name="mix_bwd", grid=(n,),
        in_specs=[pl.BlockSpec((ts, D_MODEL), lambda j: (idx(j), 0)),
                  pl.BlockSpec((ts, REST_W), lambda j: (idx(j), 0)),
                  pl.BlockSpec((HALO, POOL_W), lambda j: (jnp.maximum(idx(j) * (ts // HALO) - 1, 0), 0))]
                 + _dil_specs(ts, GROUP_W, idx) + _dil_specs(ts, GROUP_W, idx)
                 + [pl.BlockSpec((1, POOL_W), lambda j: (0, 0))] + [ANY] * 4,
        out_specs=[pl.BlockSpec((ts, REST_W), lambda j: (idx(j), 0))]
                  + _dil_specs(ts, GROUP_W, idx) + _dil_specs(ts, GROUP_W, idx)
                  + [pl.BlockSpec((1, POOL_W), lambda j: (0, 0))] + [ANY] * 4,
        out_shape=[jax.ShapeDtypeStruct((s, REST_W), BF16)] + do_shapes + c_shapes
                  + [jax.ShapeDtypeStruct((1, POOL_W), F32)]
                  + [jax.ShapeDtypeStruct(shape, F32) for shape in grad_shapes],
        scratch_shapes=[tile_buf] * 8 + [pltpu.VMEM((GROUP_W // LANES, ts, LANES), F32)]
                       + [pltpu.VMEM(w.shape, BF16) for w in weights]
                       + [pltpu.VMEM(shape, F32) for shape in grad_shapes]
                       + [pltpu.SemaphoreType.DMA((4,))],
        compiler_params=_params(dimension_semantics=("arbitrary",)),
    )(dh1, zr, zr, *o_dil, *l_dil, scale, wbd, wao, wpo, wout)
    dzr, do_dil, c_dil, g_scale = outs[0], outs[1:4], outs[4:7], outs[7]
    g_out, g_ao, g_po, g_bd = outs[8:]
    return dzr, do_dil, c_dil, (g_out, g_ao, g_po, g_bd, g_scale)


FF_CHUNK = 1024


def _mlp_fwd_bwd(h1, tgt, g2, g3, wmi, wmo, ts):
    s = h1.shape[0]
    n = s // ts
    nchunk = D_FF // FF_CHUNK

    def body(h1_ref, t_ref, g2_ref, g3_ref, wmi_hbm, wmo_hbm,
             dh1_ref, mt_ref, dh2t_ref, hid_ref, df_ref, loss_ref, dg2_ref, dg3_ref,
             wmi, wmo, relu_buf, sem):
        i = pl.program_id(0)
        _load_resident(i, [(wmi_hbm, wmi), (wmo_hbm, wmo)], sem)

        @pl.when(i == 0)
        def _():
            loss_ref[...] = jnp.zeros_like(loss_ref)
            dg2_ref[...] = jnp.zeros_like(dg2_ref)
            dg3_ref[...] = jnp.zeros_like(dg3_ref)

        h1 = h1_ref[...]
        g2 = g2_ref[...]
        g3 = g3_ref[...]
        r2 = _rms(h1)
        n2 = h1 * r2
        m = n2 * g2
        mb = m.astype(BF16)
        mt_ref[...] = m.T.astype(BF16)
        h2 = h1
        for c in range(nchunk):
            cols = slice(c * FF_CHUNK, (c + 1) * FF_CHUNK)
            rl = jnp.maximum(_dot(mb, wmi[:, cols]), 0.0)
            relu_buf[:, cols] = rl
            hb = (rl * rl).astype(BF16)
            hid_ref[:, cols] = hb
            h2 = h2 + _dot(hb, wmo[cols, :])
        r3 = _rms(h2)
        n3 = h2 * r3
        diff = n3 * g3 - t_ref[...]
        loss_ref[...] += jnp.sum(0.5 * jnp.sum(diff * diff, axis=1, keepdims=True) / D_MODEL,
                                 axis=0, keepdims=True)
        dy = diff * (1.0 / D_MODEL)
        dg3_ref[...] += jnp.sum(dy * n3, axis=0, keepdims=True)
        dh2 = _rms_bwd(dy * g3, n3, r3)
        dh2b = dh2.astype(BF16)
        dh2t_ref[...] = dh2.T.astype(BF16)
        dm = jnp.zeros((ts, D_MODEL), F32)
        for c in range(nchunk):
            cols = slice(c * FF_CHUNK, (c + 1) * FF_CHUNK)
            dfb = (_dot_nt(dh2b, wmo[cols, :]) * (2.0 * relu_buf[:, cols])).astype(BF16)
            df_ref[:, cols] = dfb
            dm = dm + _dot_nt(dfb, wmi[:, cols])
        dg2_ref[...] += jnp.sum(dm * n2, axis=0, keepdims=True)
        dh1_ref[...] = dh2 + _rms_bwd(dm * g2, n2, r2)

    row = lambda w: pl.BlockSpec((ts, w), lambda i: (i, 0))
    colb = pl.BlockSpec((D_MODEL, ts), lambda i: (0, i))
    vec = pl.BlockSpec((1, D_MODEL), lambda i: (0, 0))
    return pl.pallas_call(
        body, name="mlp_fwd_bwd", grid=(n,),
        in_specs=[row(D_MODEL), row(D_MODEL), vec, vec, ANY, ANY],
        out_specs=[row(D_MODEL), colb, colb, row(D_FF), row(D_FF),
                   pl.BlockSpec((1, 1), lambda i: (0, 0)), vec, vec],
        out_shape=[jax.ShapeDtypeStruct((s, D_MODEL), F32),
                   jax.ShapeDtypeStruct((D_MODEL, s), BF16), jax.ShapeDtypeStruct((D_MODEL, s), BF16),
                   jax.ShapeDtypeStruct((s, D_FF), BF16), jax.ShapeDtypeStruct((s, D_FF), BF16),
                   jax.ShapeDtypeStruct((1, 1), F32),
                   jax.ShapeDtypeStruct((1, D_MODEL), F32), jax.ShapeDtypeStruct((1, D_MODEL), F32)],
        scratch_shapes=[pltpu.VMEM((D_MODEL, D_FF), BF16), pltpu.VMEM((D_FF, D_MODEL), BF16),
                        pltpu.VMEM((ts, D_FF), F32), pltpu.SemaphoreType.DMA((2,))],
        compiler_params=_params(dimension_semantics=("arbitrary",)),
    )(h1, tgt, g2, g3, wmi, wmo)


def _in_proj_bwd(x, dh1, dzr, dqkv_dil, g1, w_in, ts):
    s = x.shape[0]
    n = s // ts

    def body(x_ref, dh_ref, dzr_ref, q0, q1, q2, g_ref, w_hbm, dx_ref, dz_ref, dg_ref, w_ref, qbuf, sem):
        i = pl.program_id(0)
        _load_resident(i, [(w_hbm, w_ref)], sem)

        @pl.when(i == 0)
        def _():
            dg_ref[...] = jnp.zeros_like(dg_ref)

        for g, dqkv in enumerate(_gather_rows((q0, q1, q2), qbuf, ts)):
            for sec in range(3):
                c0 = sec * ATT_W + g * GROUP_W
                dz_ref[:, c0:c0 + GROUP_W] = dqkv[:, sec * GROUP_W:(sec + 1) * GROUP_W].astype(BF16)
        dz_ref[:, 3 * ATT_W:] = dzr_ref[...]
        du = _dot_nt(dz_ref[...], w_ref[...])
        x = x_ref[...]
        r1 = _rms(x)
        n1 = x * r1
        g1 = g_ref[...]
        dg_ref[...] += jnp.sum(du * n1, axis=0, keepdims=True)
        dx_ref[...] = dh_ref[...] + _rms_bwd(du * g1, n1, r1)

    row = lambda w: pl.BlockSpec((ts, w), lambda i: (i, 0))
    vec = pl.BlockSpec((1, D_MODEL), lambda i: (0, 0))
    return pl.pallas_call(
        body, name="in_proj_bwd", grid=(n,),
        in_specs=[row(D_MODEL), row(D_MODEL), row(REST_W)] + _dil_specs(ts, ATT_W, lambda i: i) + [vec, ANY],
        out_specs=[row(D_MODEL), row(N_IN), vec],
        out_shape=[jax.ShapeDtypeStruct((s, D_MODEL), F32), jax.ShapeDtypeStruct((s, N_IN), BF16),
                   jax.ShapeDtypeStruct((1, D_MODEL), F32)],
        scratch_shapes=[pltpu.VMEM((D_MODEL, N_IN), BF16), pltpu.VMEM((ATT_W // LANES, ts, LANES), F32),
                        pltpu.SemaphoreType.DMA((1,))],
        compiler_params=_params(dimension_semantics=("arbitrary",)),
    )(x, dh1, dzr, *dqkv_dil, g1, w_in)


GRAD_PASS = 2


def _weight_grad(at, b, name, transpose_out, tk=2048):
    m, s = at.shape
    nn = b.shape[1]
    tn = nn // N_DEV
    tk = min(tk, s)
    nk = s // tk
    npass = N_DEV // GRAD_PASS
    oshape = (tn, m) if transpose_out else (m, tn)
    owner = lambda jj: N_DEV - 1 - jj
    order = jnp.stack([_linear(_peer(_my_place(), owner(jj))) for jj in range(N_DEV)]).astype(jnp.int32)
    sent_in = lambda p: [jj for jj in range(N_DEV - 1) if jj // GRAD_PASS == p]

    def body(order_ref, at_ref, *refs):
        b_refs, (got_ref, acc, res, send_sems, recv_sems, local_sem) = refs[:GRAD_PASS], refs[GRAD_PASS:]
        j, k = pl.program_id(0), pl.program_id(1)
        me = _my_place()
        mine = _linear(me)

        def send(jj):
            return pltpu.make_async_remote_copy(
                src_ref=res.at[(jj // GRAD_PASS) % 2, jj % GRAD_PASS], dst_ref=got_ref.at[mine],
                send_sem=send_sems.at[jj], recv_sem=recv_sems.at[jj],
                device_id=_peer(me, owner(jj)), device_id_type=pl.DeviceIdType.MESH)

        @pl.when(k == 0)
        def _():
            acc[...] = jnp.zeros_like(acc)

        a = at_ref[...]
        for g in range(GRAD_PASS):
            acc[g] += _dot(a, b_refs[g][...])

        @pl.when(k == nk - 1)
        def _():
            for p in range(2, npass):
                @pl.when(j == p)
                def _():
                    for jj in sent_in(p - 2):
                        send(jj).wait_send()

            for g in range(GRAD_PASS):
                r = acc[g]
                res[j % 2, g] = (r.T if transpose_out else r).astype(BF16)
            for p in range(npass):
                @pl.when(j == p)
                def _():
                    for jj in sent_in(p):
                        send(jj).start()

            @pl.when(j == npass - 1)
            def _():
                own = pltpu.make_async_copy(res.at[(npass - 1) % 2, GRAD_PASS - 1], got_ref.at[mine], local_sem.at[0])
                own.start()
                for p in range(max(npass - 2, 0), npass):
                    for jj in sent_in(p):
                        send(jj).wait_send()
                for jj in range(N_DEV - 1):
                    send(jj).wait_recv()
                own.wait()

    b_spec = lambda g: pl.BlockSpec((tk, tn), lambda j, k, o: (k, o[GRAD_PASS * j + g]))
    return pl.pallas_call(
        body, name=name,
        grid_spec=pltpu.PrefetchScalarGridSpec(
            num_scalar_prefetch=1, grid=(npass, nk),
            in_specs=[pl.BlockSpec((m, tk), lambda j, k, o: (0, k))] + [b_spec(g) for g in range(GRAD_PASS)],
            out_specs=ANY,
            scratch_shapes=[pltpu.VMEM((GRAD_PASS, m, tn), F32), pltpu.VMEM((2, GRAD_PASS) + oshape, BF16),
                            pltpu.SemaphoreType.DMA((N_DEV - 1,)), pltpu.SemaphoreType.DMA((N_DEV - 1,)),
                            pltpu.SemaphoreType.DMA((1,))]),
        out_shape=jax.ShapeDtypeStruct((N_DEV,) + oshape, BF16),
        compiler_params=_params(dimension_semantics=("arbitrary", "arbitrary")),
    )(order, at, *([b] * GRAD_PASS))


def _my_place():
    x, y, c = lax.axis_index("x"), lax.axis_index("y"), lax.axis_index("c")
    return x, y, c


def _peer(place, k):
    x, y, c = place
    return (1 - x if k & 4 else x, 1 - y if k & 2 else y, 1 - c if k & 1 else c)


def _linear(place):
    x, y, c = place
    return 4 * x + 2 * y + c


class _Exchange:
    def __init__(self, arrays, gather):
        self.arrays, self.gather, self.nw = list(arrays), list(gather), len(arrays)
        self.out_shape = []
        for a, g in zip(arrays, gather):
            block = a.shape if g else a.shape[1:]
            self.out_shape.append(jax.ShapeDtypeStruct((N_DEV,) + tuple(block), a.dtype))
        self.specs = [ANY] * self.nw
        self.scratch = [pltpu.SemaphoreType.DMA((self.nw, N_DEV - 1)), pltpu.SemaphoreType.DMA((self.nw, N_DEV - 1)),
                        pltpu.SemaphoreType.DMA((self.nw,))]

    def _copies(self, ins, outs, sems):
        send_sems, recv_sems, local_sems = sems
        me = _my_place()
        mine = _linear(me)
        copies = []
        for w in range(self.nw):
            src = ins[w] if self.gather[w] else ins[w].at[mine]
            copies.append(pltpu.make_async_copy(src, outs[w].at[mine], local_sems.at[w]))
        for k in range(1, N_DEV):
            peer = _peer(me, k)
            for w in range(self.nw):
                src = ins[w] if self.gather[w] else ins[w].at[_linear(peer)]
                copies.append(pltpu.make_async_remote_copy(
                    src_ref=src, dst_ref=outs[w].at[mine],
                    send_sem=send_sems.at[w, k - 1], recv_sem=recv_sems.at[w, k - 1],
                    device_id=peer, device_id_type=pl.DeviceIdType.MESH))
        return copies

    def start(self, ins, outs, sems):
        for cp in self._copies(ins, outs, sems):
            cp.start()

    def wait(self, ins, outs, sems):
        copies = self._copies(ins, outs, sems)
        for cp in copies[self.nw:]:
            cp.wait_recv()
        for cp in copies[self.nw:]:
            cp.wait_send()
        for cp in copies[:self.nw]:
            cp.wait()


class _Gather:
    def __init__(self, arrays):
        self.arrays, self.nw = list(arrays), len(arrays)
        self.out_shape = [jax.ShapeDtypeStruct((N_DEV,) + tuple(a.shape), a.dtype) for a in arrays]
        self.specs = [ANY] * self.nw
        self.scratch = [pltpu.SemaphoreType.DMA((self.nw, N_DEV - 1)), pltpu.SemaphoreType.DMA((self.nw, N_DEV - 1)),
                        pltpu.SemaphoreType.DMA((self.nw,))]

    @staticmethod
    def _places():
        x, y, c = _my_place()
        return (x, y, c), (x, y, 1 - c), [(1 - x, y), (x, 1 - y), (1 - x, 1 - y)]

    @staticmethod
    def _copy(outs, sems, w, k, block, to, src=None):
        rows = outs[w].at[_linear(block)]
        return pltpu.make_async_remote_copy(
            src_ref=rows if src is None else src, dst_ref=rows, send_sem=sems[0].at[w, k], recv_sem=sems[1].at[w, k],
            device_id=to, device_id_type=pl.DeviceIdType.MESH)

    def _first(self, ins, outs, sems, w):
        me, sibling, chips = self._places()
        return ([self._copy(outs, sems, w, 0, me, sibling, src=ins[w])]
                + [self._copy(outs, sems, w, 1 + j, me, (*chip, me[2]), src=ins[w]) for j, chip in enumerate(chips)])

    def _passed(self, outs, sems, w):
        me, sibling, chips = self._places()
        return [self._copy(outs, sems, w, 4 + j, (*chip, me[2]), sibling) for j, chip in enumerate(chips)]

    def _local(self, ins, outs, sems, w):
        return pltpu.make_async_copy(ins[w], outs[w].at[_linear(self._places()[0])], sems[2].at[w])

    def start(self, ins, outs, sems):
        for w in range(self.nw):
            self._local(ins, outs, sems, w).start()
            for cp in self._first(ins, outs, sems, w):
                cp.start()

    def wait(self, ins, outs, sems):
        me, sibling, chips = self._places()
        for j, chip in enumerate(chips):
            for w in range(self.nw):
                self._copy(outs, sems, w, 1 + j, (*chip, me[2]), me).wait_recv()
                self._passed(outs, sems, w)[j].start()
        for w in range(self.nw):
            self._copy(outs, sems, w, 0, sibling, me).wait_recv()
            for j, chip in enumerate(chips):
                self._copy(outs, sems, w, 4 + j, (*chip, sibling[2]), me).wait_recv()
            for cp in self._first(ins, outs, sems, w) + self._passed(outs, sems, w):
                cp.wait_send()
            self._local(ins, outs, sems, w).wait()


def _exchange(arrays, gather, name):
    ex = _Exchange(arrays, gather)

    def body(*refs):
        ins, outs, sems = refs[:ex.nw], refs[ex.nw:2 * ex.nw], refs[2 * ex.nw:]
        ex.start(ins, outs, sems)
        ex.wait(ins, outs, sems)

    return pl.pallas_call(
        body, name=name, in_specs=ex.specs, out_specs=ex.specs, out_shape=ex.out_shape, scratch_shapes=ex.scratch,
    )(*arrays)


def _adamw(parts, w, m, v, name, tr):
    rows, cols = w.shape
    tr = min(tr, rows)

    def body(p_ref, w_ref, m_ref, v_ref, g_ref, d_ref, nm_ref, nv_ref):
        g = p_ref[0].astype(F32)
        for j in range(1, N_DEV):
            g = g + p_ref[j].astype(F32)
        nm = ADAM_B1 * m_ref[...] + (1.0 - ADAM_B1) * g
        nv = ADAM_B2 * v_ref[...] + (1.0 - ADAM_B2) * (g * g)
        m_hat = nm / (1.0 - ADAM_B1 ** ADAM_STEP)
        v_hat = nv / (1.0 - ADAM_B2 ** ADAM_STEP)
        g_ref[...] = g
        d_ref[...] = -ADAM_LR * (m_hat / (jnp.sqrt(v_hat) + ADAM_EPS) + ADAM_WD * w_ref[...])
        nm_ref[...] = nm
        nv_ref[...] = nv

    blk = pl.BlockSpec((tr, cols), lambda i: (i, 0))
    return pl.pallas_call(
        body, name=name, grid=(rows // tr,),
        in_specs=[pl.BlockSpec((N_DEV, tr, cols), lambda i: (0, i, 0)), blk, blk, blk],
        out_specs=[blk] * 4,
        out_shape=[jax.ShapeDtypeStruct((rows, cols), F32)] * 4,
        compiler_params=_params(dimension_semantics=("arbitrary",)),
    )(parts, w, m, v)


def _pack_small(values):
    rows = []
    for v in values:
        size = math.prod(v.shape)
        padded = -(-size // 1024) * 1024
        flat = jnp.pad(v.reshape(-1).astype(F32), (0, padded - size))
        rows.append(flat.reshape(padded // 128, 128))
    return jnp.concatenate(rows, axis=0) if len(rows) > 1 else rows[0]


def _unpack_small(packed, shapes):
    out, r = [], 0
    for shape in shapes:
        size = math.prod(shape)
        nrow = -(-size // 1024) * 8
        out.append(packed[r:r + nrow].reshape(-1)[:size].reshape(shape))
        r += nrow
    return out


def _block_diag(w_grp):
    z = jnp.zeros((POOL_GW, POOL_GW), w_grp.dtype)
    return jnp.concatenate(
        [jnp.concatenate([w_grp[g] if c == g else z for c in range(4)], axis=1) for g in range(4)], axis=0)


def kernel(x, norm_mix_g, w_in, w_att_out, w_pool_grp, pool_scale, w_pool_out, w_out, norm_mlp_g, w_mlp_in, w_mlp_out, norm_final_g, loss_target, m_norm_mix_g, m_w_in, m_w_att_out, m_w_pool_grp, m_pool_scale, m_w_pool_out, m_w_out, m_norm_mlp_g, m_w_mlp_in, m_w_mlp_out, m_norm_final_g, v_norm_mix_g, v_w_in, v_w_att_out, v_w_pool_grp, v_pool_scale, v_w_pool_out, v_w_out, v_norm_mlp_g, v_w_mlp_in, v_w_mlp_out, v_norm_final_g):
    x, tgt = x[0], loss_target[0]
    s = x.shape[0]
    g1, g2, g3 = norm_mix_g, norm_mlp_g, norm_final_g.reshape(1, D_MODEL)
    shards = [w_in[0], w_att_out[0], w_pool_out[0], w_out[0], w_mlp_in[0], w_mlp_out[0]]
    wire = [a.astype(BF16) for a in shards]
    cols = lambda a: jnp.transpose(a, (1, 0, 2)).reshape(a.shape[1], N_DEV * a.shape[2])
    rows = lambda a: a.reshape(N_DEV * a.shape[1], a.shape[2])
    blocks_of_cols = lambda a: jnp.transpose(a.reshape(a.shape[0], N_DEV, a.shape[1] // N_DEV), (1, 0, 2))
    blocks_of_rows = lambda a: a.reshape(N_DEV, a.shape[0] // N_DEV, a.shape[1])
    wbd = _block_diag(w_pool_grp[0]).astype(BF16)

    (u, ut), (first,) = _rms_u(x, g1, 512, _Gather(wire[:1]))
    f_in = cols(first)
    (qkv0, qkv1, qkv2, zr), later = _in_proj_fwd(u, f_in, 512, _Gather(wire[1:]))
    f_ao, f_po, f_out, f_mi, f_mo = cols(later[0]), cols(later[1]), rows(later[2]), cols(later[3]), rows(later[4])
    qkv_dil = (qkv0, qkv1, qkv2)
    flat = lambda a: a.reshape(s, a.shape[-1])
    shaped = lambda a, g: a if g == 0 else a.reshape(ATT_GROUPS[g][1], s // ATT_GROUPS[g][1], a.shape[-1])
    o_dil, l_dil = [], []
    for g in range(N_GROUPS):
        o, l = _attn_fwd(flat(qkv_dil[g]), g)
        o_dil.append(shaped(o, g))
        l_dil.append(shaped(l, g))
    h1 = _mix_fwd(x, zr, o_dil, l_dil, wbd, pool_scale, f_ao, f_po, f_out, ts=256)

    dh1, mt, dh2t, hid, df, loss, dg2, dg3 = _mlp_fwd_bwd(h1, tgt, g2, g3, f_mi, f_mo, ts=256)
    got = {"w_mlp_in": _weight_grad(mt, df, "grad_w_mlp_in", transpose_out=False),
           "w_mlp_out": _weight_grad(dh2t, hid, "grad_w_mlp_out", transpose_out=True)}
    dzr, do_dil, c_dil, (g_out, g_ao, g_po, g_bd, g_scale) = _mix_bwd(
        dh1, zr, o_dil, l_dil, wbd, pool_scale, f_ao, f_po, f_out, ts=256)
    g_grp = jnp.stack([g_bd[g * POOL_GW:(g + 1) * POOL_GW, g * POOL_GW:(g + 1) * POOL_GW] for g in range(4)])
    early = _Exchange([blocks_of_cols(g_ao).astype(BF16), blocks_of_cols(g_po).astype(BF16),
                       blocks_of_rows(g_out).astype(BF16), _pack_small([g_grp])], [False, False, False, True])
    dqkv0, (got["w_att_out"], got["w_pool_out"], got["w_out"], got_grp) = _attn_bwd(
        flat(qkv_dil[0]), flat(do_dil[0]), flat(c_dil[0]), 0, early)
    dqkv_dil = [dqkv0] + [shaped(_attn_bwd(flat(qkv_dil[g]), flat(do_dil[g]), flat(c_dil[g]), g), g) for g in (1, 2)]
    dx, dz, dg1 = _in_proj_bwd(x, dh1, dzr, dqkv_dil, g1, f_in, ts=256)
    got["w_in"] = _weight_grad(ut, dz, "grad_w_in", transpose_out=False)
    got_vec = _exchange([_pack_small([loss, dg1, g_scale, dg2, dg3])], [True], "gather_small_grads")[0]

    names = ["w_in", "w_att_out", "w_pool_out", "w_out", "w_mlp_in", "w_mlp_out"]
    ms = [m_w_in, m_w_att_out, m_w_pool_out, m_w_out, m_w_mlp_in, m_w_mlp_out]
    vs = [v_w_in, v_w_att_out, v_w_pool_out, v_w_out, v_w_mlp_in, v_w_mlp_out]
    upd = {}
    for k, name in enumerate(names):
        res = _adamw(got[name], shards[k], ms[k][0], vs[k][0], "adamw_" + name, tr=256)
        upd[name] = [a[None] for a in res]

    res = _adamw(got_grp, _pack_small([w_pool_grp]), _pack_small([m_w_pool_grp]), _pack_small([v_w_pool_grp]),
                 "adamw_w_pool_grp", tr=2048)
    upd["w_pool_grp"] = [_unpack_small(a, [w_pool_grp.shape])[0] for a in res]
    vec_w = [jnp.zeros((1,), F32), norm_mix_g, pool_scale, norm_mlp_g, norm_final_g]
    vec_m = [jnp.zeros((1,), F32), m_norm_mix_g, m_pool_scale, m_norm_mlp_g, m_norm_final_g]
    vec_v = [jnp.ones((1,), F32), v_norm_mix_g, v_pool_scale, v_norm_mlp_g, v_norm_final_g]
    res = _adamw(got_vec, _pack_small(vec_w), _pack_small(vec_m), _pack_small(vec_v), "adamw_vectors", tr=2048)
    shapes = [(), norm_mix_g.shape, pool_scale.shape, norm_mlp_g.shape, norm_final_g.shape]
    unpacked = [_unpack_small(a, shapes) for a in res]
    for k, name in enumerate(["loss", "norm_mix_g", "pool_scale", "norm_mlp_g", "norm_final_g"]):
        upd[name] = [unpacked[q][k] for q in range(4)]

    order = ["norm_mix_g", "w_in", "w_att_out", "w_pool_grp", "pool_scale", "w_pool_out", "w_out", "norm_mlp_g",
             "w_mlp_in", "w_mlp_out", "norm_final_g"]
    out = [upd["loss"][0], dx[None]]
    for q in range(4):
        out += [upd[name][q] for name in order]
    return tuple(out)
```

```python
import functools
import math

import jax
import jax.numpy as jnp
from jax import lax
from jax.experimental import pallas as pl
from jax.experimental.pallas import tpu as pltpu

F32 = jnp.float32
BF16 = jnp.bfloat16

D_MODEL = 1024
HEAD_DIM = 64
GROUP_W = 256
ATT_GROUPS = ((128, 1), (512, 4), (2048, 16))
N_GROUPS = 3
BLK = 128
ATT_W = 768
POOL_W = 768
POOL_GW = 192
D_FF = 4096
N_IN = 5120
REST_W = N_IN - 3 * ATT_W
NORM_EPS = 1e-6
ALIBI_MAX_BIAS = 8.0
N_DEV = 8
HALO = 32

ADAM_LR = 0.001
ADAM_B1 = 0.9
ADAM_B2 = 0.999
ADAM_EPS = 1e-08
ADAM_WD = 0.01
ADAM_STEP = 10

VMEM_LIMIT = 56 * 1024 * 1024
ANY = pl.BlockSpec(memory_space=pl.ANY)


def _params(**kw):
    return pltpu.CompilerParams(vmem_limit_bytes=VMEM_LIMIT, **kw)


def _dot(a, b):
    return jnp.dot(a, b, preferred_element_type=F32)


def _dot_nt(a, b):
    return lax.dot_general(a, b, (((1,), (1,)), ((), ())), preferred_element_type=F32)


def _dot_tn(a, b):
    return lax.dot_general(a, b, (((0,), (0,)), ((), ())), preferred_element_type=F32)


def _slope(head):
    return 2.0 ** (-ALIBI_MAX_BIAS * (head + 1.0) / 12.0)


def _load_resident(step, pairs, sem):
    @pl.when(step == 0)
    def _():
        copies = [pltpu.make_async_copy(src, dst, sem.at[n]) for n, (src, dst) in enumerate(pairs)]
        for cp in copies:
            cp.start()
        for cp in copies:
            cp.wait()


LANES = 128


def _to_class_order(value, dil, buf, ref, col0):
    ts, w = value.shape
    if dil == 1:
        ref[:, col0:col0 + w] = value.astype(ref.dtype)
        return
    for c in range(w // LANES):
        buf[c] = value[:, c * LANES:(c + 1) * LANES]
        for r in range(dil):
            ref[r, :, col0 + c * LANES:col0 + (c + 1) * LANES] = (
                buf[c, pl.ds(r, ts // dil, stride=dil), :].astype(ref.dtype))


def _to_token_order(ref, dil, buf, ts):
    if dil == 1:
        return ref[...].astype(F32)
    w = ref.shape[-1]
    for c in range(w // LANES):
        for r in range(dil):
            buf[c, pl.ds(r, ts // dil, stride=dil), :] = ref[r, :, c * LANES:(c + 1) * LANES].astype(F32)
    return jnp.concatenate([buf[c] for c in range(w // LANES)], axis=1)


def _rms(x):
    return lax.rsqrt(jnp.mean(x * x, axis=-1, keepdims=True) + NORM_EPS)


def _rms_bwd(dn, n, r):
    return r * (dn - n * jnp.mean(dn * n, axis=-1, keepdims=True))


def _split_refs(refs, counts):
    out, at = [], 0
    for c in counts:
        out.append(refs[at:at + c])
        at += c
    return out


def _carry_start(ex, step, ins, outs, sems):
    @pl.when(step == 0)
    def _():
        ex.start(ins, outs, sems)


def _carry_wait(ex, step, last, ins, outs, sems, pass_at=None):
    if ex.passes_on:
        @pl.when(step == (last if pass_at is None else pass_at))
        def _():
            ex.pass_on(ins, outs, sems)

    @pl.when(step == last)
    def _():
        ex.wait(ins, outs, sems)


def _rms_u(x, g1, ts, ex):
    s = x.shape[0]
    n = s // ts

    def body(*refs):
        (x_ref, g_ref), ex_ins, (u_ref, ut_ref), ex_outs, ex_sems = _split_refs(refs, (2, ex.nw, 2, ex.nw, 3))
        i = pl.program_id(0)
        _carry_start(ex, i, ex_ins, ex_outs, ex_sems)
        x = x_ref[...]
        u = x * _rms(x) * g_ref[...]
        u_ref[...] = u.astype(BF16)
        ut_ref[...] = u.T.astype(BF16)
        _carry_wait(ex, i, n - 1, ex_ins, ex_outs, ex_sems)

    outs = pl.pallas_call(
        body, name="rms_u", grid=(n,),
        in_specs=[pl.BlockSpec((ts, D_MODEL), lambda i: (i, 0)), pl.BlockSpec((1, D_MODEL), lambda i: (0, 0))]
                 + ex.specs,
        out_specs=[pl.BlockSpec((ts, D_MODEL), lambda i: (i, 0)), pl.BlockSpec((D_MODEL, ts), lambda i: (0, i))]
                  + ex.specs,
        out_shape=[jax.ShapeDtypeStruct((s, D_MODEL), BF16), jax.ShapeDtypeStruct((D_MODEL, s), BF16)] + ex.out_shape,
        scratch_shapes=ex.scratch,
        compiler_params=_params(dimension_semantics=("arbitrary",)),
    )(x, g1, *ex.arrays)
    return outs[:2], outs[2:]


def _in_proj_fwd(u, w_in, ts, ex):
    s = u.shape[0]
    n = s // ts
    dils = [d for _, d in ATT_GROUPS]

    def body(*refs):
        (u_ref, w_hbm), ex_ins, (q0_ref, q1_ref, q2_ref, zr_ref), ex_outs, (w_ref, zbuf, sem), ex_sems = (
            _split_refs(refs, (2, ex.nw, 4, ex.nw, 3, 3)))
        i = pl.program_id(0)
        _carry_start(ex, i, ex_ins, ex_outs, ex_sems)
        _load_resident(i, [(w_hbm, w_ref)], sem)
        ub = u_ref[...]
        outs = (q0_ref, q1_ref, q2_ref)
        for sec in range(3):
            for g in range(N_GROUPS):
                c0 = sec * ATT_W + g * GROUP_W
                zc = _dot(ub, w_ref[:, c0:c0 + GROUP_W])
                _to_class_order(zc, dils[g], zbuf, outs[g], sec * GROUP_W)
        for c0 in range(0, REST_W, 256):
            zr_ref[:, c0:c0 + 256] = _dot(ub, w_ref[:, 3 * ATT_W + c0:3 * ATT_W + c0 + 256]).astype(BF16)
        _carry_wait(ex, i, n - 1, ex_ins, ex_outs, ex_sems, pass_at=max(n - 5, 0))

    outs = pl.pallas_call(
        body, name="in_proj_fwd", grid=(n,),
        in_specs=[pl.BlockSpec((ts, D_MODEL), lambda i: (i, 0)), ANY] + ex.specs,
        out_specs=[pl.BlockSpec((ts, ATT_W), lambda i: (i, 0)),
                   pl.BlockSpec((4, ts // 4, ATT_W), lambda i: (0, i, 0)),
                   pl.BlockSpec((16, ts // 16, ATT_W), lambda i: (0, i, 0)),
                   pl.BlockSpec((ts, REST_W), lambda i: (i, 0))] + ex.specs,
        out_shape=[jax.ShapeDtypeStruct((s, ATT_W), BF16),
                   jax.ShapeDtypeStruct((4, s // 4, ATT_W), BF16),
                   jax.ShapeDtypeStruct((16, s // 16, ATT_W), BF16),
                   jax.ShapeDtypeStruct((s, REST_W), BF16)] + ex.out_shape,
        scratch_shapes=[pltpu.VMEM((D_MODEL, N_IN), BF16), pltpu.VMEM((GROUP_W // LANES, ts, LANES), F32),
                        pltpu.SemaphoreType.DMA((1,))] + ex.scratch,
        compiler_params=_params(dimension_semantics=("arbitrary",)),
    )(u, w_in, *ex.arrays)
    return outs[:4], outs[4:]


ATT_TILE = 8 * BLK


HEADS = GROUP_W // HEAD_DIM
STACK = HEADS * BLK


SCORE_SCALE = HEAD_DIM ** -0.5


def _band_consts(group, dil):
    row = lax.broadcasted_iota(jnp.int32, (STACK, 2 * BLK), 0)
    kj = lax.broadcasted_iota(jnp.int32, (STACK, 2 * BLK), 1)
    head = row // BLK
    steps = BLK + (row % BLK) - kj
    slope = jnp.full((STACK, 2 * BLK), _slope(4 * group + HEADS - 1), F32)
    for h in range(HEADS - 1):
        slope = jnp.where(head == h, _slope(4 * group + h), slope)
    in_band = (steps >= 0) & (steps <= BLK)
    return jnp.where(in_band, slope * (steps.astype(F32) * float(dil)), jnp.inf), kj


def _first_key(block, nbc, nb, b):
    if nbc % nb == 0 and b != 0:
        return None
    return jnp.where((block % nbc) != 0, 0, BLK)


def _head_of_col():
    return lax.broadcasted_iota(jnp.int32, (BLK, GROUP_W), 1) // HEAD_DIM


def _stack_heads(xb):
    head_of = _head_of_col()
    return jnp.concatenate([jnp.where(head_of == h, xb, jnp.zeros_like(xb)) for h in range(HEADS)], axis=0)


def _unstack_heads(y):
    head_of = _head_of_col()
    acc = y[0:BLK]
    for h in range(1, HEADS):
        acc = jnp.where(head_of == h, y[h * BLK:(h + 1) * BLK], acc)
    return acc


def _per_head_cols(stacked, rhs, scale_rows=None):
    lane = lax.broadcasted_iota(jnp.int32, (BLK, LANES), 1)
    halves = []
    for pair in range(HEADS // 2):
        tile = rhs[:, pair * LANES:(pair + 1) * LANES]
        parts = []
        for h in (2 * pair, 2 * pair + 1):
            part = _dot(stacked[h * BLK:(h + 1) * BLK], tile)
            parts.append(part if scale_rows is None else part * scale_rows[h * BLK:(h + 1) * BLK])
        halves.append(jnp.where(lane < HEAD_DIM, parts[0], parts[1]))
    return jnp.concatenate(halves, axis=1)


def _per_head_rows(col):
    lane = lax.broadcasted_iota(jnp.int32, (BLK, LANES), 1)
    return jnp.concatenate(
        [jnp.where(lane < HEAD_DIM, col[(2 * pair) * BLK:(2 * pair + 1) * BLK], col[(2 * pair + 1) * BLK:(2 * pair + 2) * BLK])
         for pair in range(HEADS // 2)], axis=1)


def _band_softmax(qs, kb, penalty, kj, first_key):
    sc = _dot_nt(qs, kb) - penalty
    if first_key is not None:
        sc = jnp.where(kj >= first_key, sc, -jnp.inf)
    mx = jnp.max(sc, axis=1, keepdims=True)
    e = jnp.exp(sc - mx)
    return e, mx, jnp.sum(e, axis=1, keepdims=True)


def _attn_fwd(qkv, group):
    s = qkv.shape[0]
    dil = ATT_GROUPS[group][1]
    nbc = s // (BLK * dil)
    nb = ATT_TILE // BLK

    def body(q_ref, kc_ref, kp_ref, vc_ref, vp_ref, o_ref, l_ref, kbuf, vbuf):
        i = pl.program_id(0)
        kbuf[0:BLK] = kp_ref[...]
        kbuf[BLK:BLK + ATT_TILE] = kc_ref[...]
        vbuf[0:BLK] = vp_ref[...]
        vbuf[BLK:BLK + ATT_TILE] = vc_ref[...]
        penalty, kj = _band_consts(group, dil)
        blocks = range(nb)
        qss = [_stack_heads(q_ref[b * BLK:(b + 1) * BLK, :] * SCORE_SCALE) for b in blocks]
        soft = [_band_softmax(qss[b], kbuf[b * BLK:b * BLK + 2 * BLK, :], penalty, kj,
                              _first_key(i * nb + b, nbc, nb, b)) for b in blocks]
        for b in blocks:
            e, mx, den = soft[b]
            o_ref[b * BLK:(b + 1) * BLK, :] = _per_head_cols(e.astype(BF16), vbuf[b * BLK:b * BLK + 2 * BLK, :], 1.0 / den)
        for b in blocks:
            e, mx, den = soft[b]
            l_ref[b * BLK:(b + 1) * BLK, :] = _per_head_rows(mx + jnp.log(den))

    n = s // ATT_TILE
    cur = lambda c: pl.BlockSpec((ATT_TILE, GROUP_W), lambda i: (i, c))
    prev = lambda c: pl.BlockSpec((BLK, GROUP_W), lambda i: (jnp.maximum(i * nb - 1, 0), c))
    return pl.pallas_call(
        body, name=f"attn_fwd_g{group}", grid=(n,),
        in_specs=[cur(0), cur(1), prev(1), cur(2), prev(2)],
        out_specs=[cur(0), cur(0)],
        out_shape=[jax.ShapeDtypeStruct((s, GROUP_W), F32), jax.ShapeDtypeStruct((s, GROUP_W), F32)],
        scratch_shapes=[pltpu.VMEM((BLK + ATT_TILE, GROUP_W), BF16), pltpu.VMEM((BLK + ATT_TILE, GROUP_W), BF16)],
        compiler_params=_params(dimension_semantics=("arbitrary",)),
    )(qkv, qkv, qkv, qkv, qkv)


def _attn_bwd(qkv, do, corr, group, ex=None):
    s = qkv.shape[0]
    dil = ATT_GROUPS[group][1]
    nbc = s // (BLK * dil)
    nb = ATT_TILE // BLK
    n = s // ATT_TILE
    nex = ex.nw if ex else 0

    def body(*refs):
        ((q_ref, kc_ref, kp_ref, vc_ref, vp_ref, do_ref, c_ref), ex_ins, (out_ref,), ex_outs,
         (kbuf, vbuf, dqpend, dkpend, dvpend), ex_sems) = _split_refs(refs, (7, nex, 1, nex, 5, 3 if ex else 0))
        i = pl.program_id(0)
        if ex:
            _carry_start(ex, i, ex_ins, ex_outs, ex_sems)

        @pl.when(i == 0)
        def _():
            dqpend[...] = jnp.zeros_like(dqpend)
            dkpend[...] = jnp.zeros_like(dkpend)
            dvpend[...] = jnp.zeros_like(dvpend)

        out_ref[:, 0:GROUP_W] = dqpend[...]
        body_rows = slice(0, ATT_TILE - BLK)
        tail = slice(ATT_TILE - BLK, ATT_TILE)
        pends = ((dkpend, GROUP_W), (dvpend, 2 * GROUP_W))
        for pend, c0 in pends:
            out_ref[body_rows, c0:c0 + GROUP_W] = pend[body_rows, :].astype(BF16)

        @pl.when(i < n)
        def _():
            kbuf[0:BLK] = kp_ref[...]
            kbuf[BLK:BLK + ATT_TILE] = kc_ref[...]
            vbuf[0:BLK] = vp_ref[...]
            vbuf[BLK:BLK + ATT_TILE] = vc_ref[...]
            penalty, kj = _band_consts(group, dil)
            head_of = _head_of_col()
            blocks = range(nb)
            rows = [slice(b * BLK, (b + 1) * BLK) for b in blocks]
            kbs = [kbuf[b * BLK:b * BLK + 2 * BLK, :] for b in blocks]
            vbs = [vbuf[b * BLK:b * BLK + 2 * BLK, :] for b in blocks]
            qss = [_stack_heads(q_ref[rows[b], :] * SCORE_SCALE) for b in blocks]
            doss = [_stack_heads(do_ref[rows[b], :]) for b in blocks]
            cors = []
            for b in blocks:
                cb = c_ref[rows[b], :]
                cors.append(jnp.concatenate(
                    [jnp.max(jnp.where(head_of == h, cb, -jnp.inf), axis=1, keepdims=True) for h in range(HEADS)],
                    axis=0))
            soft = [_band_softmax(qss[b], kbs[b], penalty, kj, _first_key(i * nb + b, nbc, nb, b)) for b in blocks]
            dps = [_dot_nt(doss[b], vbs[b]) for b in blocks]
            ps = [soft[b][0] * (1.0 / soft[b][2]) for b in blocks]
            dss = [(ps[b] * (dps[b] + cors[b])).astype(BF16) for b in blocks]
            for b in blocks:
                dqpend[rows[b], :] = _per_head_cols(dss[b], kbs[b] * SCORE_SCALE).astype(BF16)
            bands = [(_dot_tn(dss[b], qss[b]), _dot_tn(ps[b].astype(BF16), doss[b])) for b in blocks]
            for which, (pend, c0) in enumerate(pends):
                out_ref[tail, c0:c0 + GROUP_W] = (pend[tail, :] + bands[0][which][0:BLK]).astype(BF16)
                for b in range(nb):
                    own = bands[b][which][BLK:2 * BLK]
                    pend[b * BLK:(b + 1) * BLK, :] = own + bands[b + 1][which][0:BLK] if b + 1 < nb else own

        @pl.when(i == n)
        def _():
            for pend, c0 in pends:
                out_ref[tail, c0:c0 + GROUP_W] = pend[tail, :].astype(BF16)

        if ex:
            _carry_wait(ex, i, n, ex_ins, ex_outs, ex_sems)

    last = n - 1
    cur = lambda c: pl.BlockSpec((ATT_TILE, GROUP_W), lambda i: (jnp.minimum(i, last), c))
    prev = lambda c: pl.BlockSpec(
        (BLK, GROUP_W), lambda i: (jnp.maximum(jnp.minimum(i, last) * nb - 1, 0), c))
    outs = pl.pallas_call(
        body, name=f"attn_bwd_g{group}", grid=(n + 1,),
        in_specs=[cur(0), cur(1), prev(1), cur(2), prev(2), cur(0), cur(0)] + (ex.specs if ex else []),
        out_specs=[pl.BlockSpec((ATT_TILE, ATT_W), lambda i: (jnp.maximum(i - 1, 0), 0))] + (ex.specs if ex else []),
        out_shape=[jax.ShapeDtypeStruct((s, ATT_W), BF16)] + (ex.out_shape if ex else []),
        scratch_shapes=[pltpu.VMEM((BLK + ATT_TILE, GROUP_W), BF16), pltpu.VMEM((BLK + ATT_TILE, GROUP_W), BF16),
                        pltpu.VMEM((ATT_TILE, GROUP_W), BF16),
                        pltpu.VMEM((ATT_TILE, GROUP_W), F32), pltpu.VMEM((ATT_TILE, GROUP_W), F32)]
                       + (ex.scratch if ex else []),
        compiler_params=_params(dimension_semantics=("arbitrary",)),
    )(qkv, qkv, qkv, qkv, qkv, do, corr, *(ex.arrays if ex else []))
    return (outs[0], outs[1:]) if ex else outs[0]


def _dil_specs(ts, width, idx):
    return [pl.BlockSpec((ts, width), lambda i: (idx(i), 0)),
            pl.BlockSpec((4, ts // 4, width), lambda i: (0, idx(i), 0)),
            pl.BlockSpec((16, ts // 16, width), lambda i: (0, idx(i), 0))]


def _gather_rows(refs, buf, ts):
    return [_to_token_order(refs[g], ATT_GROUPS[g][1], buf, ts) for g in range(N_GROUPS)]


def _pool_fwd(ebuf, s2, s4, s8, t0, ts):
    n = ts + HALO
    s2[8:n] = ebuf[8:n] + ebuf[7:n - 1]
    s4[16:n] = s2[16:n] + s2[14:n - 2]
    s8[24:n] = s4[24:n] + s4[20:n - 4]
    s16 = s8[32:n] + s8[24:n - 8]
    col = lax.broadcasted_iota(jnp.int32, (ts, POOL_W), 1)
    psum = jnp.where(col < POOL_GW, s2[32:n],
                     jnp.where(col < 2 * POOL_GW, s4[32:n], jnp.where(col < 3 * POOL_GW, s8[32:n], s16)))
    win = jnp.where(col < POOL_GW, 2, jnp.where(col < 2 * POOL_GW, 4, jnp.where(col < 3 * POOL_GW, 8, 16)))
    t = t0 + lax.broadcasted_iota(jnp.int32, (ts, POOL_W), 0)
    count = jnp.minimum(t + 1, win).astype(F32)
    return psum / count - ebuf[32:n], count


def _mix_core(zr, pooled, outs, lses, wbd, scale, wao, wpo):
    mixed = _dot(pooled.astype(BF16), wbd)
    p = mixed * scale
    l0, l1, l2 = lses
    mx = jnp.maximum(jnp.maximum(l0, l1), l2)
    e0, e1, e2 = jnp.exp(l0 - mx), jnp.exp(l1 - mx), jnp.exp(l2 - mx)
    inv = 1.0 / (e0 + e1 + e2)
    wts = (e0 * inv, e1 * inv, e2 * inv)
    a = wts[0] * outs[0] + wts[1] * outs[1] + wts[2] * outs[2]
    att = _dot(a.astype(BF16), wao)
    pol = _dot(p.astype(BF16), wpo)
    sga = jax.nn.sigmoid(zr[:, POOL_W:POOL_W + D_MODEL].astype(F32))
    sgp = jax.nn.sigmoid(zr[:, POOL_W + D_MODEL:].astype(F32))
    mg = sga * att + sgp * pol
    return dict(mixed=mixed, p=p, wts=wts, a=a, att=att, pol=pol, sga=sga, sgp=sgp, mg=mg)


def _fill_pool_input(ebuf, zr_ref, halo_ref, t0):
    ts = zr_ref.shape[0]
    halo = halo_ref[...].astype(F32)
    t = t0 - HALO + lax.broadcasted_iota(jnp.int32, (HALO, POOL_W), 0)
    ebuf[0:HALO] = jnp.where(t >= 0, halo, 0.0)
    ebuf[HALO:HALO + ts] = zr_ref[:, 0:POOL_W].astype(F32)


def _mix_fwd(x, zr, o_dil, l_dil, wbd, scale, wao, wpo, wout, ts):
    s = x.shape[0]
    n = s // ts

    def body(x_ref, zr_ref, halo_ref, o0, o1, o2, l0, l1, l2, wbd_ref, sc_ref, wao_ref, wpo_ref, wout_ref,
             h1_ref, ebuf, s2, s4, s8, rbuf):
        i = pl.program_id(0)
        _fill_pool_input(ebuf, zr_ref, halo_ref, i * ts)
        pooled, _ = _pool_fwd(ebuf, s2, s4, s8, i * ts, ts)
        outs = _gather_rows((o0, o1, o2), rbuf, ts)
        lses = _gather_rows((l0, l1, l2), rbuf, ts)
        f = _mix_core(zr_ref[...], pooled, outs, lses, wbd_ref[...], sc_ref[...], wao_ref[...], wpo_ref[...])
        h1_ref[...] = x_ref[...] + _dot(f["mg"].astype(BF16), wout_ref[...])

    whole = lambda a: pl.BlockSpec(a.shape, lambda i: (0,) * a.ndim)
    idx = lambda i: i
    return pl.pallas_call(
        body, name="mix_fwd", grid=(n,),
        in_specs=[pl.BlockSpec((ts, D_MODEL), lambda i: (i, 0)),
                  pl.BlockSpec((ts, REST_W), lambda i: (i, 0)),
                  pl.BlockSpec((HALO, POOL_W), lambda i: (jnp.maximum(i * (ts // HALO) - 1, 0), 0))]
                 + _dil_specs(ts, GROUP_W, idx) + _dil_specs(ts, GROUP_W, idx)
                 + [whole(wbd), whole(scale), whole(wao), whole(wpo), whole(wout)],
        out_specs=pl.BlockSpec((ts, D_MODEL), lambda i: (i, 0)),
        out_shape=jax.ShapeDtypeStruct((s, D_MODEL), F32),
        scratch_shapes=[pltpu.VMEM((ts + HALO, POOL_W), F32)] * 4
                       + [pltpu.VMEM((GROUP_W // LANES, ts, LANES), F32)],
        compiler_params=_params(dimension_semantics=("arbitrary",)),
    )(x, zr, zr, *o_dil, *l_dil, wbd, scale, wao, wpo, wout)


def _mix_bwd(dh1, zr, o_dil, l_dil, wbd, scale, wao, wpo, wout, ts):
    s = dh1.shape[0]
    n = s // ts

    def body(dh_ref, zr_ref, halo_ref, o0, o1, o2, l0, l1, l2, sc_ref, wbd_hbm, wao_hbm, wpo_hbm, wout_hbm,
             dzr_ref, do0, do1, do2, c0, c1, c2, gsc_ref, gwout_hbm, gwao_hbm, gwpo_hbm, gwbd_hbm,
             ebuf, s2, s4, s8, gbuf, t2, t4, t8, rbuf,
             wbd_ref, wao_ref, wpo_ref, wout_ref, gwout_ref, gwao_ref, gwpo_ref, gwbd_ref, sem):
        j = pl.program_id(0)
        i = n - 1 - j
        _load_resident(j, [(wbd_hbm, wbd_ref), (wao_hbm, wao_ref), (wpo_hbm, wpo_ref), (wout_hbm, wout_ref)], sem)

        @pl.when(j == 0)
        def _():
            gwout_ref[...] = jnp.zeros_like(gwout_ref)
            gwao_ref[...] = jnp.zeros_like(gwao_ref)
            gwpo_ref[...] = jnp.zeros_like(gwpo_ref)
            gwbd_ref[...] = jnp.zeros_like(gwbd_ref)
            gsc_ref[...] = jnp.zeros_like(gsc_ref)
            gbuf[ts:ts + HALO] = jnp.zeros((HALO, POOL_W), F32)

        _fill_pool_input(ebuf, zr_ref, halo_ref, i * ts)
        pooled, count = _pool_fwd(ebuf, s2, s4, s8, i * ts, ts)
        outs = _gather_rows((o0, o1, o2), rbuf, ts)
        lses = _gather_rows((l0, l1, l2), rbuf, ts)
        zr = zr_ref[...]
        wbd, wao, wpo, wout = wbd_ref[...], wao_ref[...], wpo_ref[...], wout_ref[...]
        scale = sc_ref[...]
        f = _mix_core(zr, pooled, outs, lses, wbd, scale, wao, wpo)

        dhb = dh_ref[...].astype(BF16)
        gwout_ref[...] += _dot(f["mg"].T.astype(BF16), dhb)
        dmg = _dot_nt(dhb, wout)
        sga, sgp, att, pol = f["sga"], f["sgp"], f["att"], f["pol"]
        datt = dmg * sga
        dpol = dmg * sgp
        dzr_ref[:, POOL_W:POOL_W + D_MODEL] = (dmg * att * sga * (1.0 - sga)).astype(BF16)
        dzr_ref[:, POOL_W + D_MODEL:] = (dmg * pol * sgp * (1.0 - sgp)).astype(BF16)
        dattb = datt.astype(BF16)
        dpolb = dpol.astype(BF16)
        gwao_ref[...] += _dot(f["a"].T.astype(BF16), dattb)
        gwpo_ref[...] += _dot(f["p"].T.astype(BF16), dpolb)
        da = _dot_nt(dattb, wao)
        dp = _dot_nt(dpolb, wpo)

        gsc_ref[...] += jnp.sum(f["mixed"] * dp, axis=0, keepdims=True)
        dmixed = (dp * scale).astype(BF16)
        gwbd_ref[...] += _dot(pooled.T.astype(BF16), dmixed)
        dpooled = _dot_nt(dmixed, wbd)
        gbuf[0:ts] = dpooled / count
        m = ts + HALO
        t2[0:m - 8] = gbuf[0:m - 8] + gbuf[1:m - 7]
        t4[0:m - 16] = t2[0:m - 16] + t2[2:m - 14]
        t8[0:m - 24] = t4[0:m - 24] + t4[4:m - 20]
        t16 = t8[0:ts] + t8[8:ts + 8]
        col = lax.broadcasted_iota(jnp.int32, (ts, POOL_W), 1)
        back = jnp.where(col < POOL_GW, t2[0:ts],
                         jnp.where(col < 2 * POOL_GW, t4[0:ts], jnp.where(col < 3 * POOL_GW, t8[0:ts], t16)))
        dzr_ref[:, 0:POOL_W] = (back - dpooled).astype(BF16)
        gbuf[ts:ts + HALO] = gbuf[0:HALO]

        head_of = lax.broadcasted_iota(jnp.int32, (ts, GROUP_W), 1) // HEAD_DIM
        prod = da * f["a"]
        inner = jnp.zeros((ts, GROUP_W), F32)
        for h in range(4):
            hm = head_of == h
            tot = jnp.sum(jnp.where(hm, prod, 0.0), axis=1, keepdims=True)
            inner = jnp.where(hm, tot, inner)
        for g, (do_ref, c_ref) in enumerate(((do0, c0), (do1, c1), (do2, c2))):
            dil = ATT_GROUPS[g][1]
            _to_class_order(f["wts"][g] * da, dil, rbuf, do_ref, 0)
            _to_class_order(-f["wts"][g] * inner, dil, rbuf, c_ref, 0)

        @pl.when(j == n - 1)
        def _():
            pairs = ((gwout_ref, gwout_hbm), (gwao_ref, gwao_hbm), (gwpo_ref, gwpo_hbm), (gwbd_ref, gwbd_hbm))
            copies = [pltpu.make_async_copy(src, dst, sem.at[k]) for k, (src, dst) in enumerate(pairs)]
            for cp in copies:
                cp.start()
            for cp in copies:
                cp.wait()

    idx = lambda j: n - 1 - j
    do_shapes = [jax.ShapeDtypeStruct((s, GROUP_W), BF16), jax.ShapeDtypeStruct((4, s // 4, GROUP_W), BF16),
                 jax.ShapeDtypeStruct((16, s // 16, GROUP_W), BF16)]
    c_shapes = [jax.ShapeDtypeStruct(a.shape, F32) for a in do_shapes]
    weights = (wbd, wao, wpo, wout)
    grad_shapes = [(D_MODEL, D_MODEL), (GROUP_W, D_MODEL), (POOL_W, D_MODEL), (POOL_W, POOL_W)]
    tile_buf = pltpu.VMEM((ts + HALO, POOL_W), F32)
    outs = pl.pallas_call(
        body, name="mix_bwd", grid=(n,),
        in_specs=[pl.BlockSpec((ts, D_MODEL), lambda j: (idx(j), 0)),
                  pl.BlockSpec((ts, REST_W), lambda j: (idx(j), 0)),
                  pl.BlockSpec((HALO, POOL_W), lambda j: (jnp.maximum(idx(j) * (ts // HALO) - 1, 0), 0))]
                 + _dil_specs(ts, GROUP_W, idx) + _dil_specs(ts, GROUP_W, idx)
                 + [pl.BlockSpec((1, POOL_W), lambda j: (0, 0))] + [ANY] * 4,
        out_specs=[pl.BlockSpec((ts, REST_W), lambda j: (idx(j), 0))]
                  + _dil_specs(ts, GROUP_W, idx) + _dil_specs(ts, GROUP_W, idx)
                  + [pl.BlockSpec((1, POOL_W), lambda j: (0, 0))] + [ANY] * 4,
        out_shape=[jax.ShapeDtypeStruct((s, REST_W), BF16)] + do_shapes + c_shapes
                  + [jax.ShapeDtypeStruct((1, POOL_W), F32)]
                  + [jax.ShapeDtypeStruct(shape, F32) for shape in grad_shapes],
        scratch_shapes=[tile_buf] * 8 + [pltpu.VMEM((GROUP_W // LANES, ts, LANES), F32)]
                       + [pltpu.VMEM(w.shape, BF16) for w in weights]
                       + [pltpu.VMEM(shape, F32) for shape in grad_shapes]
                       + [pltpu.SemaphoreType.DMA((4,))],
        compiler_params=_params(dimension_semantics=("arbitrary",)),
    )(dh1, zr, zr, *o_dil, *l_dil, scale, wbd, wao, wpo, wout)
    dzr, do_dil, c_dil, g_scale = outs[0], outs[1:4], outs[4:7], outs[7]
    g_out, g_ao, g_po, g_bd = outs[8:]
    return dzr, do_dil, c_dil, (g_out, g_ao, g_po, g_bd, g_scale)


FF_CHUNK = 1024


def _mlp_fwd_bwd(h1, tgt, g2, g3, wmi, wmo, ts):
    s = h1.shape[0]
    n = s // ts
    nchunk = D_FF // FF_CHUNK

    def body(h1_ref, t_ref, g2_ref, g3_ref, wmi_hbm, wmo_hbm,
             dh1_ref, mt_ref, dh2t_ref, hid_ref, df_ref, loss_ref, dg2_ref, dg3_ref,
             wmi, wmo, relu_buf, sem):
        i = pl.program_id(0)
        _load_resident(i, [(wmi_hbm, wmi), (wmo_hbm, wmo)], sem)

        @pl.when(i == 0)
        def _():
            loss_ref[...] = jnp.zeros_like(loss_ref)
            dg2_ref[...] = jnp.zeros_like(dg2_ref)
            dg3_ref[...] = jnp.zeros_like(dg3_ref)

        h1 = h1_ref[...]
        g2 = g2_ref[...]
        g3 = g3_ref[...]
        r2 = _rms(h1)
        n2 = h1 * r2
        m = n2 * g2
        mb = m.astype(BF16)
        mt_ref[...] = m.T.astype(BF16)
        h2 = h1
        for c in range(nchunk):
            cols = slice(c * FF_CHUNK, (c + 1) * FF_CHUNK)
            rl = jnp.maximum(_dot(mb, wmi[:, cols]), 0.0)
            relu_buf[:, cols] = rl
            hb = (rl * rl).astype(BF16)
            hid_ref[:, cols] = hb
            h2 = h2 + _dot(hb, wmo[cols, :])
        r3 = _rms(h2)
        n3 = h2 * r3
        diff = n3 * g3 - t_ref[...]
        loss_ref[...] += jnp.sum(0.5 * jnp.sum(diff * diff, axis=1, keepdims=True) / D_MODEL,
                                 axis=0, keepdims=True)
        dy = diff * (1.0 / D_MODEL)
        dg3_ref[...] += jnp.sum(dy * n3, axis=0, keepdims=True)
        dh2 = _rms_bwd(dy * g3, n3, r3)
        dh2b = dh2.astype(BF16)
        dh2t_ref[...] = dh2.T.astype(BF16)
        dm = jnp.zeros((ts, D_MODEL), F32)
        for c in range(nchunk):
            cols = slice(c * FF_CHUNK, (c + 1) * FF_CHUNK)
            dfb = (_dot_nt(dh2b, wmo[cols, :]) * (2.0 * relu_buf[:, cols])).astype(BF16)
            df_ref[:, cols] = dfb
            dm = dm + _dot_nt(dfb, wmi[:, cols])
        dg2_ref[...] += jnp.sum(dm * n2, axis=0, keepdims=True)
        dh1_ref[...] = dh2 + _rms_bwd(dm * g2, n2, r2)

    row = lambda w: pl.BlockSpec((ts, w), lambda i: (i, 0))
    colb = pl.BlockSpec((D_MODEL, ts), lambda i: (0, i))
    vec = pl.BlockSpec((1, D_MODEL), lambda i: (0, 0))
    return pl.pallas_call(
        body, name="mlp_fwd_bwd", grid=(n,),
        in_specs=[row(D_MODEL), row(D_MODEL), vec, vec, ANY, ANY],
        out_specs=[row(D_MODEL), colb, colb, row(D_FF), row(D_FF),
                   pl.BlockSpec((1, 1), lambda i: (0, 0)), vec, vec],
        out_shape=[jax.ShapeDtypeStruct((s, D_MODEL), F32),
                   jax.ShapeDtypeStruct((D_MODEL, s), BF16), jax.ShapeDtypeStruct((D_MODEL, s), BF16),
                   jax.ShapeDtypeStruct((s, D_FF), BF16), jax.ShapeDtypeStruct((s, D_FF), BF16),
                   jax.ShapeDtypeStruct((1, 1), F32),
                   jax.ShapeDtypeStruct((1, D_MODEL), F32), jax.ShapeDtypeStruct((1, D_MODEL), F32)],
        scratch_shapes=[pltpu.VMEM((D_MODEL, D_FF), BF16), pltpu.VMEM((D_FF, D_MODEL), BF16),
                        pltpu.VMEM((ts, D_FF), F32), pltpu.SemaphoreType.DMA((2,))],
        compiler_params=_params(dimension_semantics=("arbitrary",)),
    )(h1, tgt, g2, g3, wmi, wmo)


def _in_proj_bwd(x, dh1, dzr, dqkv_dil, g1, w_in, ts):
    s = x.shape[0]
    n = s // ts

    def body(x_ref, dh_ref, dzr_ref, q0, q1, q2, g_ref, w_hbm, dx_ref, dz_ref, dg_ref, w_ref, qbuf, sem):
        i = pl.program_id(0)
        _load_resident(i, [(w_hbm, w_ref)], sem)

        @pl.when(i == 0)
        def _():
            dg_ref[...] = jnp.zeros_like(dg_ref)

        for g, dqkv in enumerate(_gather_rows((q0, q1, q2), qbuf, ts)):
            for sec in range(3):
                c0 = sec * ATT_W + g * GROUP_W
                dz_ref[:, c0:c0 + GROUP_W] = dqkv[:, sec * GROUP_W:(sec + 1) * GROUP_W].astype(BF16)
        dz_ref[:, 3 * ATT_W:] = dzr_ref[...]
        du = _dot_nt(dz_ref[...], w_ref[...])
        x = x_ref[...]
        r1 = _rms(x)
        n1 = x * r1
        g1 = g_ref[...]
        dg_ref[...] += jnp.sum(du * n1, axis=0, keepdims=True)
        dx_ref[...] = dh_ref[...] + _rms_bwd(du * g1, n1, r1)

    row = lambda w: pl.BlockSpec((ts, w), lambda i: (i, 0))
    vec = pl.BlockSpec((1, D_MODEL), lambda i: (0, 0))
    return pl.pallas_call(
        body, name="in_proj_bwd", grid=(n,),
        in_specs=[row(D_MODEL), row(D_MODEL), row(REST_W)] + _dil_specs(ts, ATT_W, lambda i: i) + [vec, ANY],
        out_specs=[row(D_MODEL), row(N_IN), vec],
        out_shape=[jax.ShapeDtypeStruct((s, D_MODEL), F32), jax.ShapeDtypeStruct((s, N_IN), BF16),
                   jax.ShapeDtypeStruct((1, D_MODEL), F32)],
        scratch_shapes=[pltpu.VMEM((D_MODEL, N_IN), BF16), pltpu.VMEM((ATT_W // LANES, ts, LANES), F32),
                        pltpu.SemaphoreType.DMA((1,))],
        compiler_params=_params(dimension_semantics=("arbitrary",)),
    )(x, dh1, dzr, *dqkv_dil, g1, w_in)


GRAD_PASS = 2


def _weight_grad(at, b, name, transpose_out, tk=2048):
    m, s = at.shape
    nn = b.shape[1]
    tn = nn // N_DEV
    tk = min(tk, s)
    nk = s // tk
    npass = N_DEV // GRAD_PASS
    oshape = (tn, m) if transpose_out else (m, tn)
    owner = lambda jj: N_DEV - 1 - jj
    order = jnp.stack([_linear(_peer(_my_place(), owner(jj))) for jj in range(N_DEV)]).astype(jnp.int32)
    sent_in = lambda p: [jj for jj in range(N_DEV - 1) if jj // GRAD_PASS == p]

    def body(order_ref, at_ref, *refs):
        b_refs, (got_ref, acc, res, send_sems, recv_sems, local_sem) = refs[:GRAD_PASS], refs[GRAD_PASS:]
        j, k = pl.program_id(0), pl.program_id(1)
        me = _my_place()
        mine = _linear(me)

        def send(jj):
            return pltpu.make_async_remote_copy(
                src_ref=res.at[(jj // GRAD_PASS) % 2, jj % GRAD_PASS], dst_ref=got_ref.at[mine],
                send_sem=send_sems.at[jj], recv_sem=recv_sems.at[jj],
                device_id=_peer(me, owner(jj)), device_id_type=pl.DeviceIdType.MESH)

        @pl.when(k == 0)
        def _():
            acc[...] = jnp.zeros_like(acc)

        a = at_ref[...]
        for g in range(GRAD_PASS):
            acc[g] += _dot(a, b_refs[g][...])

        @pl.when(k == nk - 1)
        def _():
            for p in range(2, npass):
                @pl.when(j == p)
                def _():
                    for jj in sent_in(p - 2):
                        send(jj).wait_send()

            for g in range(GRAD_PASS):
                r = acc[g]
                res[j % 2, g] = (r.T if transpose_out else r).astype(BF16)
            for p in range(npass):
                @pl.when(j == p)
                def _():
                    for jj in sent_in(p):
                        send(jj).start()

            @pl.when(j == npass - 1)
            def _():
                own = pltpu.make_async_copy(res.at[(npass - 1) % 2, GRAD_PASS - 1], got_ref.at[mine], local_sem.at[0])
                own.start()
                for p in range(max(npass - 2, 0), npass):
                    for jj in sent_in(p):
                        send(jj).wait_send()
                for jj in range(N_DEV - 1):
                    send(jj).wait_recv()
                own.wait()

    b_spec = lambda g: pl.BlockSpec((tk, tn), lambda j, k, o: (k, o[GRAD_PASS * j + g]))
    return pl.pallas_call(
        body, name=name,
        grid_spec=pltpu.PrefetchScalarGridSpec(
            num_scalar_prefetch=1, grid=(npass, nk),
            in_specs=[pl.BlockSpec((m, tk), lambda j, k, o: (0, k))] + [b_spec(g) for g in range(GRAD_PASS)],
            out_specs=ANY,
            scratch_shapes=[pltpu.VMEM((GRAD_PASS, m, tn), F32), pltpu.VMEM((2, GRAD_PASS) + oshape, BF16),
                            pltpu.SemaphoreType.DMA((N_DEV - 1,)), pltpu.SemaphoreType.DMA((N_DEV - 1,)),
                            pltpu.SemaphoreType.DMA((1,))]),
        out_shape=jax.ShapeDtypeStruct((N_DEV,) + oshape, BF16),
        compiler_params=_params(dimension_semantics=("arbitrary", "arbitrary")),
    )(order, at, *([b] * GRAD_PASS))


def _my_place():
    x, y, c = lax.axis_index("x"), lax.axis_index("y"), lax.axis_index("c")
    return x, y, c


def _peer(place, k):
    x, y, c = place
    return (1 - x if k & 4 else x, 1 - y if k & 2 else y, 1 - c if k & 1 else c)


def _linear(place):
    x, y, c = place
    return 4 * x + 2 * y + c


class _Exchange:
    passes_on = False

    def __init__(self, arrays, gather):
        self.arrays, self.gather, self.nw = list(arrays), list(gather), len(arrays)
        self.out_shape = []
        for a, g in zip(arrays, gather):
            block = a.shape if g else a.shape[1:]
            self.out_shape.append(jax.ShapeDtypeStruct((N_DEV,) + tuple(block), a.dtype))
        self.specs = [ANY] * self.nw
        self.scratch = [pltpu.SemaphoreType.DMA((self.nw, N_DEV - 1)), pltpu.SemaphoreType.DMA((self.nw, N_DEV - 1)),
                        pltpu.SemaphoreType.DMA((self.nw,))]

    def _copies(self, ins, outs, sems):
        send_sems, recv_sems, local_sems = sems
        me = _my_place()
        mine = _linear(me)
        copies = []
        for w in range(self.nw):
            src = ins[w] if self.gather[w] else ins[w].at[mine]
            copies.append(pltpu.make_async_copy(src, outs[w].at[mine], local_sems.at[w]))
        for k in range(1, N_DEV):
            peer = _peer(me, k)
            for w in range(self.nw):
                src = ins[w] if self.gather[w] else ins[w].at[_linear(peer)]
                copies.append(pltpu.make_async_remote_copy(
                    src_ref=src, dst_ref=outs[w].at[mine],
                    send_sem=send_sems.at[w, k - 1], recv_sem=recv_sems.at[w, k - 1],
                    device_id=peer, device_id_type=pl.DeviceIdType.MESH))
        return copies

    def start(self, ins, outs, sems):
        for cp in self._copies(ins, outs, sems):
            cp.start()

    def wait(self, ins, outs, sems):
        copies = self._copies(ins, outs, sems)
        for cp in copies[self.nw:]:
            cp.wait_recv()
        for cp in copies[self.nw:]:
            cp.wait_send()
        for cp in copies[:self.nw]:
            cp.wait()


class _Gather:
    passes_on = True

    def __init__(self, arrays):
        self.arrays, self.nw = list(arrays), len(arrays)
        self.out_shape = [jax.ShapeDtypeStruct((N_DEV,) + tuple(a.shape), a.dtype) for a in arrays]
        self.specs = [ANY] * self.nw
        self.scratch = [pltpu.SemaphoreType.DMA((self.nw, N_DEV - 1)), pltpu.SemaphoreType.DMA((self.nw, N_DEV - 1)),
                        pltpu.SemaphoreType.DMA((self.nw,))]

    @staticmethod
    def _places():
        x, y, c = _my_place()
        return (x, y, c), (x, y, 1 - c), [(1 - x, y), (x, 1 - y), (1 - x, 1 - y)]

    @staticmethod
    def _copy(outs, sems, w, k, block, to, src=None):
        rows = outs[w].at[_linear(block)]
        return pltpu.make_async_remote_copy(
            src_ref=rows if src is None else src, dst_ref=rows, send_sem=sems[0].at[w, k], recv_sem=sems[1].at[w, k],
            device_id=to, device_id_type=pl.DeviceIdType.MESH)

    def _first(self, ins, outs, sems, w):
        me, sibling, chips = self._places()
        return ([self._copy(outs, sems, w, 0, me, sibling, src=ins[w])]
                + [self._copy(outs, sems, w, 1 + j, me, (*chip, me[2]), src=ins[w]) for j, chip in enumerate(chips)])

    def _passed(self, outs, sems, w):
        me, sibling, chips = self._places()
        return [self._copy(outs, sems, w, 4 + j, (*chip, me[2]), sibling) for j, chip in enumerate(chips)]

    def _local(self, ins, outs, sems, w):
        return pltpu.make_async_copy(ins[w], outs[w].at[_linear(self._places()[0])], sems[2].at[w])

    def start(self, ins, outs, sems):
        for w in range(self.nw):
            self._local(ins, outs, sems, w).start()
            for cp in self._first(ins, outs, sems, w):
                cp.start()

    def pass_on(self, ins, outs, sems):
        me, sibling, chips = self._places()
        for j, chip in enumerate(chips):
            for w in range(self.nw):
                self._copy(outs, sems, w, 1 + j, (*chip, me[2]), me).wait_recv()
                self._passed(outs, sems, w)[j].start()

    def wait(self, ins, outs, sems):
        me, sibling, chips = self._places()
        for w in range(self.nw):
            self._copy(outs, sems, w, 0, sibling, me).wait_recv()
            for j, chip in enumerate(chips):
                self._copy(outs, sems, w, 4 + j, (*chip, sibling[2]), me).wait_recv()
            for cp in self._first(ins, outs, sems, w) + self._passed(outs, sems, w):
                cp.wait_send()
            self._local(ins, outs, sems, w).wait()


def _exchange(arrays, gather, name):
    ex = _Exchange(arrays, gather)

    def body(*refs):
        ins, outs, sems = refs[:ex.nw], refs[ex.nw:2 * ex.nw], refs[2 * ex.nw:]
        ex.start(ins, outs, sems)
        ex.wait(ins, outs, sems)

    return pl.pallas_call(
        body, name=name, in_specs=ex.specs, out_specs=ex.specs, out_shape=ex.out_shape, scratch_shapes=ex.scratch,
    )(*arrays)


def _adamw(parts, w, m, v, name, tr):
    rows, cols = w.shape
    tr = min(tr, rows)

    def body(p_ref, w_ref, m_ref, v_ref, g_ref, d_ref, nm_ref, nv_ref):
        g = p_ref[0].astype(F32)
        for j in range(1, N_DEV):
            g = g + p_ref[j].astype(F32)
        nm = ADAM_B1 * m_ref[...] + (1.0 - ADAM_B1) * g
        nv = ADAM_B2 * v_ref[...] + (1.0 - ADAM_B2) * (g * g)
        m_hat = nm / (1.0 - ADAM_B1 ** ADAM_STEP)
        v_hat = nv / (1.0 - ADAM_B2 ** ADAM_STEP)
        g_ref[...] = g
        d_ref[...] = -ADAM_LR * (m_hat / (jnp.sqrt(v_hat) + ADAM_EPS) + ADAM_WD * w_ref[...])
        nm_ref[...] = nm
        nv_ref[...] = nv

    blk = pl.BlockSpec((tr, cols), lambda i: (i, 0))
    return pl.pallas_call(
        body, name=name, grid=(rows // tr,),
        in_specs=[pl.BlockSpec((N_DEV, tr, cols), lambda i: (0, i, 0)), blk, blk, blk],
        out_specs=[blk] * 4,
        out_shape=[jax.ShapeDtypeStruct((rows, cols), F32)] * 4,
        compiler_params=_params(dimension_semantics=("arbitrary",)),
    )(parts, w, m, v)


def _pack_small(values):
    rows = []
    for v in values:
        size = math.prod(v.shape)
        padded = -(-size // 1024) * 1024
        flat = jnp.pad(v.reshape(-1).astype(F32), (0, padded - size))
        rows.append(flat.reshape(padded // 128, 128))
    return jnp.concatenate(rows, axis=0) if len(rows) > 1 else rows[0]


def _unpack_small(packed, shapes):
    out, r = [], 0
    for shape in shapes:
        size = math.prod(shape)
        nrow = -(-size // 1024) * 8
        out.append(packed[r:r + nrow].reshape(-1)[:size].reshape(shape))
        r += nrow
    return out


def _block_diag(w_grp):
    z = jnp.zeros((POOL_GW, POOL_GW), w_grp.dtype)
    return jnp.concatenate(
        [jnp.concatenate([w_grp[g] if c == g else z for c in range(4)], axis=1) for g in range(4)], axis=0)


def kernel(x, norm_mix_g, w_in, w_att_out, w_pool_grp, pool_scale, w_pool_out, w_out, norm_mlp_g, w_mlp_in, w_mlp_out, norm_final_g, loss_target, m_norm_mix_g, m_w_in, m_w_att_out, m_w_pool_grp, m_pool_scale, m_w_pool_out, m_w_out, m_norm_mlp_g, m_w_mlp_in, m_w_mlp_out, m_norm_final_g, v_norm_mix_g, v_w_in, v_w_att_out, v_w_pool_grp, v_pool_scale, v_w_pool_out, v_w_out, v_norm_mlp_g, v_w_mlp_in, v_w_mlp_out, v_norm_final_g):
    x, tgt = x[0], loss_target[0]
    s = x.shape[0]
    g1, g2, g3 = norm_mix_g, norm_mlp_g, norm_final_g.reshape(1, D_MODEL)
    shards = [w_in[0], w_att_out[0], w_pool_out[0], w_out[0], w_mlp_in[0], w_mlp_out[0]]
    wire = [a.astype(BF16) for a in shards]
    cols = lambda a: jnp.transpose(a, (1, 0, 2)).reshape(a.shape[1], N_DEV * a.shape[2])
    rows = lambda a: a.reshape(N_DEV * a.shape[1], a.shape[2])
    blocks_of_cols = lambda a: jnp.transpose(a.reshape(a.shape[0], N_DEV, a.shape[1] // N_DEV), (1, 0, 2))
    blocks_of_rows = lambda a: a.reshape(N_DEV, a.shape[0] // N_DEV, a.shape[1])
    wbd = _block_diag(w_pool_grp[0]).astype(BF16)

    (u, ut), (first,) = _rms_u(x, g1, 512, _Gather(wire[:1]))
    f_in = cols(first)
    (qkv0, qkv1, qkv2, zr), later = _in_proj_fwd(u, f_in, 512, _Gather(wire[1:]))
    f_ao, f_po, f_out, f_mi, f_mo = cols(later[0]), cols(later[1]), rows(later[2]), cols(later[3]), rows(later[4])
    qkv_dil = (qkv0, qkv1, qkv2)
    flat = lambda a: a.reshape(s, a.shape[-1])
    shaped = lambda a, g: a if g == 0 else a.reshape(ATT_GROUPS[g][1], s // ATT_GROUPS[g][1], a.shape[-1])
    o_dil, l_dil = [], []
    for g in range(N_GROUPS):
        o, l = _attn_fwd(flat(qkv_dil[g]), g)
        o_dil.append(shaped(o, g))
        l_dil.append(shaped(l, g))
    h1 = _mix_fwd(x, zr, o_dil, l_dil, wbd, pool_scale, f_ao, f_po, f_out, ts=512)

    dh1, mt, dh2t, hid, df, loss, dg2, dg3 = _mlp_fwd_bwd(h1, tgt, g2, g3, f_mi, f_mo, ts=256)
    got = {"w_mlp_in": _weight_grad(mt, df, "grad_w_mlp_in", transpose_out=False),
           "w_mlp_out": _weight_grad(dh2t, hid, "grad_w_mlp_out", transpose_out=True)}
    dzr, do_dil, c_dil, (g_out, g_ao, g_po, g_bd, g_scale) = _mix_bwd(
        dh1, zr, o_dil, l_dil, wbd, pool_scale, f_ao, f_po, f_out, ts=256)
    g_grp = jnp.stack([g_bd[g * POOL_GW:(g + 1) * POOL_GW, g * POOL_GW:(g + 1) * POOL_GW] for g in range(4)])
    early = [_Exchange([blocks_of_rows(g_out).astype(BF16)], [False]),
             _Exchange([blocks_of_cols(g_ao).astype(BF16), blocks_of_cols(g_po).astype(BF16)], [False, False]),
             _Exchange([_pack_small([g_grp])], [True])]
    dqkv_dil, arrived = [], []
    for g in range(N_GROUPS):
        dqkv, rode = _attn_bwd(flat(qkv_dil[g]), flat(do_dil[g]), flat(c_dil[g]), g, early[g])
        dqkv_dil.append(shaped(dqkv, g))
        arrived += list(rode)
    got["w_out"], got["w_att_out"], got["w_pool_out"], got_grp = arrived
    dx, dz, dg1 = _in_proj_bwd(x, dh1, dzr, dqkv_dil, g1, f_in, ts=512)
    got["w_in"] = _weight_grad(ut, dz, "grad_w_in", transpose_out=False)
    got_vec = _exchange([_pack_small([loss, dg1, g_scale, dg2, dg3])], [True], "gather_small_grads")[0]

    names = ["w_in", "w_att_out", "w_pool_out", "w_out", "w_mlp_in", "w_mlp_out"]
    ms = [m_w_in, m_w_att_out, m_w_pool_out, m_w_out, m_w_mlp_in, m_w_mlp_out]
    vs = [v_w_in, v_w_att_out, v_w_pool_out, v_w_out, v_w_mlp_in, v_w_mlp_out]
    upd = {}
    for k, name in enumerate(names):
        res = _adamw(got[name], shards[k], ms[k][0], vs[k][0], "adamw_" + name, tr=256)
        upd[name] = [a[None] for a in res]

    res = _adamw(got_grp, _pack_small([w_pool_grp]), _pack_small([m_w_pool_grp]), _pack_small([v_w_pool_grp]),
                 "adamw_w_pool_grp", tr=2048)
    upd["w_pool_grp"] = [_unpack_small(a, [w_pool_grp.shape])[0] for a in res]
    vec_w = [jnp.zeros((1,), F32), norm_mix_g, pool_scale, norm_mlp_g, norm_final_g]
    vec_m = [jnp.zeros((1,), F32), m_norm_mix_g, m_pool_scale, m_norm_mlp_g, m_norm_final_g]
    vec_v = [jnp.ones((1,), F32), v_norm_mix_g, v_pool_scale, v_norm_mlp_g, v_norm_final_g]
    res = _adamw(got_vec, _pack_small(vec_w), _pack_small(vec_m), _pack_small(vec_v), "adamw_vectors", tr=2048)
    shapes = [(), norm_mix_g.shape, pool_scale.shape, norm_mlp_g.shape, norm_final_g.shape]
    unpacked = [_unpack_small(a, shapes) for a in res]
    for k, name in enumerate(["loss", "norm_mix_g", "pool_scale", "norm_mlp_g", "norm_final_g"]):
        upd[name] = [unpacked[q][k] for q in range(4)]

    order = ["norm_mix_g", "w_in", "w_att_out", "w_pool_grp", "pool_scale", "w_pool_out", "w_out", "norm_mlp_g",
             "w_mlp_in", "w_mlp_out", "norm_final_g"]
    out = [upd["loss"][0], dx[None]]
    for q in range(4):
        out += [upd[name][q] for name in order]
    return tuple(out)
```

```python
import functools
import math

import jax
import jax.numpy as jnp
from jax import lax
from jax.experimental import pallas as pl
from jax.experimental.pallas import tpu as pltpu

F32 = jnp.float32
BF16 = jnp.bfloat16

D_MODEL = 1024
HEAD_DIM = 64
GROUP_W = 256
ATT_GROUPS = ((128, 1), (512, 4), (2048, 16))
N_GROUPS = 3
BLK = 128
ATT_W = 768
POOL_W = 768
POOL_GW = 192
D_FF = 4096
N_IN = 5120
REST_W = N_IN - 3 * ATT_W
NORM_EPS = 1e-6
ALIBI_MAX_BIAS = 8.0
N_DEV = 8
HALO = 32

ADAM_LR = 0.001
ADAM_B1 = 0.9
ADAM_B2 = 0.999
ADAM_EPS = 1e-08
ADAM_WD = 0.01
ADAM_STEP = 10

VMEM_LIMIT = 56 * 1024 * 1024
ANY = pl.BlockSpec(memory_space=pl.ANY)


def _params(**kw):
    return pltpu.CompilerParams(vmem_limit_bytes=VMEM_LIMIT, **kw)


def _dot(a, b):
    return jnp.dot(a, b, preferred_element_type=F32)


def _dot_nt(a, b):
    return lax.dot_general(a, b, (((1,), (1,)), ((), ())), preferred_element_type=F32)


def _dot_tn(a, b):
    return lax.dot_general(a, b, (((0,), (0,)), ((), ())), preferred_element_type=F32)


def _slope(head):
    return 2.0 ** (-ALIBI_MAX_BIAS * (head + 1.0) / 12.0)


def _load_resident(step, pairs, sem):
    @pl.when(step == 0)
    def _():
        copies = [pltpu.make_async_copy(src, dst, sem.at[n]) for n, (src, dst) in enumerate(pairs)]
        for cp in copies:
            cp.start()
        for cp in copies:
            cp.wait()


LANES = 128


def _to_class_order(value, dil, buf, ref, col0):
    ts, w = value.shape
    if dil == 1:
        ref[:, col0:col0 + w] = value.astype(ref.dtype)
        return
    for c in range(w // LANES):
        buf[c] = value[:, c * LANES:(c + 1) * LANES]
        for r in range(dil):
            ref[r, :, col0 + c * LANES:col0 + (c + 1) * LANES] = (
                buf[c, pl.ds(r, ts // dil, stride=dil), :].astype(ref.dtype))


def _to_token_order(ref, dil, buf, ts):
    if dil == 1:
        return ref[...].astype(F32)
    w = ref.shape[-1]
    for c in range(w // LANES):
        for r in range(dil):
            buf[c, pl.ds(r, ts // dil, stride=dil), :] = ref[r, :, c * LANES:(c + 1) * LANES].astype(F32)
    return jnp.concatenate([buf[c] for c in range(w // LANES)], axis=1)


def _sigmoid(x):
    return 0.5 * jnp.tanh(0.5 * x) + 0.5


def _rms(x):
    return lax.rsqrt(jnp.mean(x * x, axis=-1, keepdims=True) + NORM_EPS)


def _rms_bwd(dn, n, r):
    return r * (dn - n * jnp.mean(dn * n, axis=-1, keepdims=True))


def _split_refs(refs, counts):
    out, at = [], 0
    for c in counts:
        out.append(refs[at:at + c])
        at += c
    return out


def _carry_start(ex, step, ins, outs, sems):
    @pl.when(step == 0)
    def _():
        ex.start(ins, outs, sems)


def _carry_wait(ex, step, last, ins, outs, sems, pass_at=None):
    if ex.passes_on:
        @pl.when(step == (last if pass_at is None else pass_at))
        def _():
            ex.pass_on(ins, outs, sems)

    @pl.when(step == last)
    def _():
        ex.wait(ins, outs, sems)


def _rms_u(x, g1, ts, ex):
    s = x.shape[0]
    n = s // ts

    def body(*refs):
        (x_ref, g_ref), ex_ins, (u_ref, ut_ref), ex_outs, ex_sems = _split_refs(refs, (2, ex.nw, 2, ex.nw, 3))
        i = pl.program_id(0)
        _carry_start(ex, i, ex_ins, ex_outs, ex_sems)
        x = x_ref[...]
        u = x * _rms(x) * g_ref[...]
        u_ref[...] = u.astype(BF16)
        ut_ref[...] = u.T.astype(BF16)
        _carry_wait(ex, i, n - 1, ex_ins, ex_outs, ex_sems)

    outs = pl.pallas_call(
        body, name="rms_u", grid=(n,),
        in_specs=[pl.BlockSpec((ts, D_MODEL), lambda i: (i, 0)), pl.BlockSpec((1, D_MODEL), lambda i: (0, 0))]
                 + ex.specs,
        out_specs=[pl.BlockSpec((ts, D_MODEL), lambda i: (i, 0)), pl.BlockSpec((D_MODEL, ts), lambda i: (0, i))]
                  + ex.specs,
        out_shape=[jax.ShapeDtypeStruct((s, D_MODEL), BF16), jax.ShapeDtypeStruct((D_MODEL, s), BF16)] + ex.out_shape,
        scratch_shapes=ex.scratch,
        compiler_params=_params(dimension_semantics=("arbitrary",)),
    )(x, g1, *ex.arrays)
    return outs[:2], outs[2:]


def _in_proj_fwd(u, w_in, ts, ex):
    s = u.shape[0]
    n = s // ts
    dils = [d for _, d in ATT_GROUPS]

    def body(*refs):
        (u_ref, w_hbm), ex_ins, (q0_ref, q1_ref, q2_ref, zr_ref), ex_outs, (w_ref, zbuf, sem), ex_sems = (
            _split_refs(refs, (2, ex.nw, 4, ex.nw, 3, 3)))
        i = pl.program_id(0)
        _carry_start(ex, i, ex_ins, ex_outs, ex_sems)
        _load_resident(i, [(w_hbm, w_ref)], sem)
        ub = u_ref[...]
        outs = (q0_ref, q1_ref, q2_ref)
        for sec in range(3):
            for g in range(N_GROUPS):
                c0 = sec * ATT_W + g * GROUP_W
                zc = _dot(ub, w_ref[:, c0:c0 + GROUP_W])
                _to_class_order(zc, dils[g], zbuf, outs[g], sec * GROUP_W)
        for c0 in range(0, REST_W, 256):
            zr_ref[:, c0:c0 + 256] = _dot(ub, w_ref[:, 3 * ATT_W + c0:3 * ATT_W + c0 + 256]).astype(BF16)
        _carry_wait(ex, i, n - 1, ex_ins, ex_outs, ex_sems, pass_at=max(n - 3, 0))

    outs = pl.pallas_call(
        body, name="in_proj_fwd", grid=(n,),
        in_specs=[pl.BlockSpec((ts, D_MODEL), lambda i: (i, 0)), ANY] + ex.specs,
        out_specs=[pl.BlockSpec((ts, ATT_W), lambda i: (i, 0)),
                   pl.BlockSpec((4, ts // 4, ATT_W), lambda i: (0, i, 0)),
                   pl.BlockSpec((16, ts // 16, ATT_W), lambda i: (0, i, 0)),
                   pl.BlockSpec((ts, REST_W), lambda i: (i, 0))] + ex.specs,
        out_shape=[jax.ShapeDtypeStruct((s, ATT_W), BF16),
                   jax.ShapeDtypeStruct((4, s // 4, ATT_W), BF16),
                   jax.ShapeDtypeStruct((16, s // 16, ATT_W), BF16),
                   jax.ShapeDtypeStruct((s, REST_W), BF16)] + ex.out_shape,
        scratch_shapes=[pltpu.VMEM((D_MODEL, N_IN), BF16), pltpu.VMEM((GROUP_W // LANES, ts, LANES), F32),
                        pltpu.SemaphoreType.DMA((1,))] + ex.scratch,
        compiler_params=_params(dimension_semantics=("arbitrary",)),
    )(u, w_in, *ex.arrays)
    return outs[:4], outs[4:]


ATT_TILE = 8 * BLK


HEADS = GROUP_W // HEAD_DIM
STACK = HEADS * BLK


SCORE_SCALE = HEAD_DIM ** -0.5


def _band_consts(group, dil):
    row = lax.broadcasted_iota(jnp.int32, (STACK, 2 * BLK), 0)
    kj = lax.broadcasted_iota(jnp.int32, (STACK, 2 * BLK), 1)
    head = row // BLK
    steps = BLK + (row % BLK) - kj
    slope = jnp.full((STACK, 2 * BLK), _slope(4 * group + HEADS - 1), F32)
    for h in range(HEADS - 1):
        slope = jnp.where(head == h, _slope(4 * group + h), slope)
    in_band = (steps >= 0) & (steps <= BLK)
    return jnp.where(in_band, slope * (steps.astype(F32) * float(dil)), jnp.inf), kj


def _first_key(block, nbc, nb, b):
    if nbc % nb == 0 and b != 0:
        return None
    return jnp.where((block % nbc) != 0, 0, BLK)


def _head_of_col():
    return lax.broadcasted_iota(jnp.int32, (BLK, GROUP_W), 1) // HEAD_DIM


def _stack_heads(xb):
    head_of = _head_of_col()
    return jnp.concatenate([jnp.where(head_of == h, xb, jnp.zeros_like(xb)) for h in range(HEADS)], axis=0)


def _unstack_heads(y):
    head_of = _head_of_col()
    acc = y[0:BLK]
    for h in range(1, HEADS):
        acc = jnp.where(head_of == h, y[h * BLK:(h + 1) * BLK], acc)
    return acc


def _per_head_cols(stacked, rhs, scale_rows=None):
    lane = lax.broadcasted_iota(jnp.int32, (BLK, LANES), 1)
    halves = []
    for pair in range(HEADS // 2):
        tile = rhs[:, pair * LANES:(pair + 1) * LANES]
        parts = []
        for h in (2 * pair, 2 * pair + 1):
            part = _dot(stacked[h * BLK:(h + 1) * BLK], tile)
            parts.append(part if scale_rows is None else part * scale_rows[h * BLK:(h + 1) * BLK])
        halves.append(jnp.where(lane < HEAD_DIM, parts[0], parts[1]))
    return jnp.concatenate(halves, axis=1)


def _per_head_rows(col):
    lane = lax.broadcasted_iota(jnp.int32, (BLK, LANES), 1)
    return jnp.concatenate(
        [jnp.where(lane < HEAD_DIM, col[(2 * pair) * BLK:(2 * pair + 1) * BLK], col[(2 * pair + 1) * BLK:(2 * pair + 2) * BLK])
         for pair in range(HEADS // 2)], axis=1)


def _band_softmax(qs, kb, penalty, kj, first_key):
    sc = _dot_nt(qs, kb) - penalty
    if first_key is not None:
        sc = jnp.where(kj >= first_key, sc, -jnp.inf)
    mx = jnp.max(sc, axis=1, keepdims=True)
    e = jnp.exp(sc - mx)
    return e, mx, jnp.sum(e, axis=1, keepdims=True)


def _attn_fwd(qkv, group):
    s = qkv.shape[0]
    dil = ATT_GROUPS[group][1]
    nbc = s // (BLK * dil)
    nb = ATT_TILE // BLK

    def body(q_ref, kc_ref, kp_ref, vc_ref, vp_ref, o_ref, l_ref, kbuf, vbuf):
        i = pl.program_id(0)
        kbuf[0:BLK] = kp_ref[...]
        kbuf[BLK:BLK + ATT_TILE] = kc_ref[...]
        vbuf[0:BLK] = vp_ref[...]
        vbuf[BLK:BLK + ATT_TILE] = vc_ref[...]
        penalty, kj = _band_consts(group, dil)
        blocks = range(nb)
        qss = [_stack_heads(q_ref[b * BLK:(b + 1) * BLK, :] * SCORE_SCALE) for b in blocks]
        soft = [_band_softmax(qss[b], kbuf[b * BLK:b * BLK + 2 * BLK, :], penalty, kj,
                              _first_key(i * nb + b, nbc, nb, b)) for b in blocks]
        for b in blocks:
            e, mx, den = soft[b]
            o_ref[b * BLK:(b + 1) * BLK, :] = _per_head_cols(e.astype(BF16), vbuf[b * BLK:b * BLK + 2 * BLK, :], 1.0 / den)
        for b in blocks:
            e, mx, den = soft[b]
            l_ref[b * BLK:(b + 1) * BLK, :] = _per_head_rows(mx + jnp.log(den))

    n = s // ATT_TILE
    cur = lambda c: pl.BlockSpec((ATT_TILE, GROUP_W), lambda i: (i, c))
    prev = lambda c: pl.BlockSpec((BLK, GROUP_W), lambda i: (jnp.maximum(i * nb - 1, 0), c))
    return pl.pallas_call(
        body, name=f"attn_fwd_g{group}", grid=(n,),
        in_specs=[cur(0), cur(1), prev(1), cur(2), prev(2)],
        out_specs=[cur(0), cur(0)],
        out_shape=[jax.ShapeDtypeStruct((s, GROUP_W), F32), jax.ShapeDtypeStruct((s, GROUP_W), F32)],
        scratch_shapes=[pltpu.VMEM((BLK + ATT_TILE, GROUP_W), BF16), pltpu.VMEM((BLK + ATT_TILE, GROUP_W), BF16)],
        compiler_params=_params(dimension_semantics=("arbitrary",)),
    )(qkv, qkv, qkv, qkv, qkv)


def _attn_bwd(qkv, do, corr, group, ex=None):
    s = qkv.shape[0]
    dil = ATT_GROUPS[group][1]
    nbc = s // (BLK * dil)
    nb = ATT_TILE // BLK
    n = s // ATT_TILE
    nex = ex.nw if ex else 0

    def body(*refs):
        ((q_ref, kc_ref, kp_ref, vc_ref, vp_ref, do_ref, c_ref), ex_ins, (out_ref,), ex_outs,
         (kbuf, vbuf, dqpend, dkpend, dvpend), ex_sems) = _split_refs(refs, (7, nex, 1, nex, 5, 3 if ex else 0))
        i = pl.program_id(0)
        if ex:
            _carry_start(ex, i, ex_ins, ex_outs, ex_sems)

        @pl.when(i == 0)
        def _():
            dqpend[...] = jnp.zeros_like(dqpend)
            dkpend[...] = jnp.zeros_like(dkpend)
            dvpend[...] = jnp.zeros_like(dvpend)

        out_ref[:, 0:GROUP_W] = dqpend[...]
        body_rows = slice(0, ATT_TILE - BLK)
        tail = slice(ATT_TILE - BLK, ATT_TILE)
        pends = ((dkpend, GROUP_W), (dvpend, 2 * GROUP_W))
        for pend, c0 in pends:
            out_ref[body_rows, c0:c0 + GROUP_W] = pend[body_rows, :].astype(BF16)

        @pl.when(i < n)
        def _():
            kbuf[0:BLK] = kp_ref[...]
            kbuf[BLK:BLK + ATT_TILE] = kc_ref[...]
            vbuf[0:BLK] = vp_ref[...]
            vbuf[BLK:BLK + ATT_TILE] = vc_ref[...]
            penalty, kj = _band_consts(group, dil)
            head_of = _head_of_col()
            blocks = range(nb)
            rows = [slice(b * BLK, (b + 1) * BLK) for b in blocks]
            kbs = [kbuf[b * BLK:b * BLK + 2 * BLK, :] for b in blocks]
            vbs = [vbuf[b * BLK:b * BLK + 2 * BLK, :] for b in blocks]
            qss = [_stack_heads(q_ref[rows[b], :] * SCORE_SCALE) for b in blocks]
            doss = [_stack_heads(do_ref[rows[b], :]) for b in blocks]
            cors = []
            for b in blocks:
                cb = c_ref[rows[b], :]
                cors.append(jnp.concatenate(
                    [jnp.max(jnp.where(head_of == h, cb, -jnp.inf), axis=1, keepdims=True) for h in range(HEADS)],
                    axis=0))
            soft = [_band_softmax(qss[b], kbs[b], penalty, kj, _first_key(i * nb + b, nbc, nb, b)) for b in blocks]
            dps = [_dot_nt(doss[b], vbs[b]) for b in blocks]
            ps = [soft[b][0] * (1.0 / soft[b][2]) for b in blocks]
            dss = [(ps[b] * (dps[b] + cors[b])).astype(BF16) for b in blocks]
            for b in blocks:
                dqpend[rows[b], :] = _per_head_cols(dss[b], kbs[b] * SCORE_SCALE).astype(BF16)
            bands = [(_dot_tn(dss[b], qss[b]), _dot_tn(ps[b].astype(BF16), doss[b])) for b in blocks]
            for which, (pend, c0) in enumerate(pends):
                out_ref[tail, c0:c0 + GROUP_W] = (pend[tail, :] + bands[0][which][0:BLK]).astype(BF16)
                for b in range(nb):
                    own = bands[b][which][BLK:2 * BLK]
                    pend[b * BLK:(b + 1) * BLK, :] = own + bands[b + 1][which][0:BLK] if b + 1 < nb else own

        @pl.when(i == n)
        def _():
            for pend, c0 in pends:
                out_ref[tail, c0:c0 + GROUP_W] = pend[tail, :].astype(BF16)

        if ex:
            _carry_wait(ex, i, n, ex_ins, ex_outs, ex_sems)

    last = n - 1
    cur = lambda c: pl.BlockSpec((ATT_TILE, GROUP_W), lambda i: (jnp.minimum(i, last), c))
    prev = lambda c: pl.BlockSpec(
        (BLK, GROUP_W), lambda i: (jnp.maximum(jnp.minimum(i, last) * nb - 1, 0), c))
    outs = pl.pallas_call(
        body, name=f"attn_bwd_g{group}", grid=(n + 1,),
        in_specs=[cur(0), cur(1), prev(1), cur(2), prev(2), cur(0), cur(0)] + (ex.specs if ex else []),
        out_specs=[pl.BlockSpec((ATT_TILE, ATT_W), lambda i: (jnp.maximum(i - 1, 0), 0))] + (ex.specs if ex else []),
        out_shape=[jax.ShapeDtypeStruct((s, ATT_W), BF16)] + (ex.out_shape if ex else []),
        scratch_shapes=[pltpu.VMEM((BLK + ATT_TILE, GROUP_W), BF16), pltpu.VMEM((BLK + ATT_TILE, GROUP_W), BF16),
                        pltpu.VMEM((ATT_TILE, GROUP_W), BF16),
                        pltpu.VMEM((ATT_TILE, GROUP_W), F32), pltpu.VMEM((ATT_TILE, GROUP_W), F32)]
                       + (ex.scratch if ex else []),
        compiler_params=_params(dimension_semantics=("arbitrary",)),
    )(qkv, qkv, qkv, qkv, qkv, do, corr, *(ex.arrays if ex else []))
    return (outs[0], outs[1:]) if ex else outs[0]


def _dil_specs(ts, width, idx):
    return [pl.BlockSpec((ts, width), lambda i: (idx(i), 0)),
            pl.BlockSpec((4, ts // 4, width), lambda i: (0, idx(i), 0)),
            pl.BlockSpec((16, ts // 16, width), lambda i: (0, idx(i), 0))]


def _gather_rows(refs, buf, ts):
    return [_to_token_order(refs[g], ATT_GROUPS[g][1], buf, ts) for g in range(N_GROUPS)]


def _inverse_counts(ts):
    col = jnp.arange(POOL_W)
    win = jnp.where(col < POOL_GW, 2, jnp.where(col < 2 * POOL_GW, 4, jnp.where(col < 3 * POOL_GW, 8, 16)))
    first = jnp.minimum(jnp.arange(ts)[:, None] + 1, win[None, :])
    return 1.0 / jnp.stack([first, jnp.broadcast_to(win[None, :], (ts, POOL_W))]).astype(F32)


def _pool_fwd(ebuf, s2, s4, s8, inv_count, ts):
    n = ts + HALO
    s2[8:n] = ebuf[8:n] + ebuf[7:n - 1]
    s4[16:n] = s2[16:n] + s2[14:n - 2]
    s8[24:n] = s4[24:n] + s4[20:n - 4]
    s16 = s8[32:n] + s8[24:n - 8]
    col = lax.broadcasted_iota(jnp.int32, (ts, POOL_W), 1)
    psum = jnp.where(col < POOL_GW, s2[32:n],
                     jnp.where(col < 2 * POOL_GW, s4[32:n], jnp.where(col < 3 * POOL_GW, s8[32:n], s16)))
    return psum * inv_count - ebuf[32:n]


GROUP_WIN = ((0, 0), (128, 64), (384, 0), (512, 64))
WIN = 256


def _window_weights(w_grp):
    return jnp.stack([jnp.pad(w_grp[g], ((off, WIN - off - POOL_GW), (off, WIN - off - POOL_GW)))
                      for g, (_, off) in enumerate(GROUP_WIN)])


def _group_matmul(xb, wwin_ref, transposed=False):
    outs = []
    for g, (start, _) in enumerate(GROUP_WIN):
        xw = xb[:, start:start + WIN]
        outs.append(_dot_nt(xw, wwin_ref[g]) if transposed else _dot(xw, wwin_ref[g]))
    half = WIN // 2
    return jnp.concatenate([outs[0][:, :half], outs[0][:, half:] + outs[1][:, :half], outs[1][:, half:],
                            outs[2][:, :half], outs[2][:, half:] + outs[3][:, :half], outs[3][:, half:]], axis=1)


def _mix_core(zr, pooled, outs, lses, wwin_ref, scale, wao, wpo):
    mixed = _group_matmul(pooled.astype(BF16), wwin_ref)
    p = mixed * scale
    l0, l1, l2 = lses
    mx = jnp.maximum(jnp.maximum(l0, l1), l2)
    e0, e1, e2 = jnp.exp(l0 - mx), jnp.exp(l1 - mx), jnp.exp(l2 - mx)
    inv = 1.0 / (e0 + e1 + e2)
    wts = (e0 * inv, e1 * inv, e2 * inv)
    a = wts[0] * outs[0] + wts[1] * outs[1] + wts[2] * outs[2]
    att = _dot(a.astype(BF16), wao)
    pol = _dot(p.astype(BF16), wpo)
    sga = _sigmoid(zr[:, POOL_W:POOL_W + D_MODEL].astype(F32))
    sgp = _sigmoid(zr[:, POOL_W + D_MODEL:].astype(F32))
    mg = sga * att + sgp * pol
    return dict(mixed=mixed, p=p, wts=wts, a=a, att=att, pol=pol, sga=sga, sgp=sgp, mg=mg)


def _fill_pool_input(ebuf, zr_ref, halo_ref, t0):
    ts = zr_ref.shape[0]
    halo = halo_ref[...].astype(F32)
    t = t0 - HALO + lax.broadcasted_iota(jnp.int32, (HALO, POOL_W), 0)
    ebuf[0:HALO] = jnp.where(t >= 0, halo, 0.0)
    ebuf[HALO:HALO + ts] = zr_ref[:, 0:POOL_W].astype(F32)


def _mix_fwd(x, zr, o_dil, l_dil, wbd, scale, wao, wpo, wout, ts):
    s = x.shape[0]
    n = s // ts

    def body(x_ref, zr_ref, halo_ref, ic_ref, o0, o1, o2, l0, l1, l2, wbd_ref, sc_ref, wao_ref, wpo_ref, wout_ref,
             h1_ref, ebuf, s2, s4, s8, rbuf):
        i = pl.program_id(0)
        _fill_pool_input(ebuf, zr_ref, halo_ref, i * ts)
        pooled = _pool_fwd(ebuf, s2, s4, s8, ic_ref[...], ts)
        outs = _gather_rows((o0, o1, o2), rbuf, ts)
        lses = _gather_rows((l0, l1, l2), rbuf, ts)
        f = _mix_core(zr_ref[...], pooled, outs, lses, wbd_ref, sc_ref[...], wao_ref[...], wpo_ref[...])
        h1_ref[...] = x_ref[...] + _dot(f["mg"].astype(BF16), wout_ref[...])

    whole = lambda a: pl.BlockSpec(a.shape, lambda i: (0,) * a.ndim)
    idx = lambda i: i
    return pl.pallas_call(
        body, name="mix_fwd", grid=(n,),
        in_specs=[pl.BlockSpec((ts, D_MODEL), lambda i: (i, 0)),
                  pl.BlockSpec((ts, REST_W), lambda i: (i, 0)),
                  pl.BlockSpec((HALO, POOL_W), lambda i: (jnp.maximum(i * (ts // HALO) - 1, 0), 0)),
                  pl.BlockSpec((None, ts, POOL_W), lambda i: (jnp.minimum(i, 1), 0, 0))]
                 + _dil_specs(ts, GROUP_W, idx) + _dil_specs(ts, GROUP_W, idx)
                 + [whole(wbd), whole(scale), whole(wao), whole(wpo), whole(wout)],
        out_specs=pl.BlockSpec((ts, D_MODEL), lambda i: (i, 0)),
        out_shape=jax.ShapeDtypeStruct((s, D_MODEL), F32),
        scratch_shapes=[pltpu.VMEM((ts + HALO, POOL_W), F32)] * 4
                       + [pltpu.VMEM((GROUP_W // LANES, ts, LANES), F32)],
        compiler_params=_params(dimension_semantics=("arbitrary",)),
    )(x, zr, zr, _inverse_counts(ts), *o_dil, *l_dil, wbd, scale, wao, wpo, wout)


def _mix_bwd(dh1, zr, o_dil, l_dil, wbd, scale, wao, wpo, wout, ts):
    s = dh1.shape[0]
    n = s // ts

    def body(dh_ref, zr_ref, halo_ref, ic_ref, o0, o1, o2, l0, l1, l2, sc_ref, wbd_hbm, wao_hbm, wpo_hbm, wout_hbm,
             dzr_ref, do0, do1, do2, c0, c1, c2, gsc_ref, gwout_hbm, gwao_hbm, gwpo_hbm, gwbd_hbm,
             ebuf, s2, s4, s8, gbuf, t2, t4, t8, rbuf,
             wbd_ref, wao_ref, wpo_ref, wout_ref, gwout_ref, gwao_ref, gwpo_ref, gwbd_ref, sem):
        j = pl.program_id(0)
        i = n - 1 - j
        _load_resident(j, [(wbd_hbm, wbd_ref), (wao_hbm, wao_ref), (wpo_hbm, wpo_ref), (wout_hbm, wout_ref)], sem)

        @pl.when(j == 0)
        def _():
            gwout_ref[...] = jnp.zeros_like(gwout_ref)
            gwao_ref[...] = jnp.zeros_like(gwao_ref)
            gwpo_ref[...] = jnp.zeros_like(gwpo_ref)
            gwbd_ref[...] = jnp.zeros_like(gwbd_ref)
            gsc_ref[...] = jnp.zeros_like(gsc_ref)
            gbuf[ts:ts + HALO] = jnp.zeros((HALO, POOL_W), F32)

        _fill_pool_input(ebuf, zr_ref, halo_ref, i * ts)
        inv_count = ic_ref[...]
        pooled = _pool_fwd(ebuf, s2, s4, s8, inv_count, ts)
        outs = _gather_rows((o0, o1, o2), rbuf, ts)
        lses = _gather_rows((l0, l1, l2), rbuf, ts)
        zr = zr_ref[...]
        wao, wpo, wout = wao_ref[...], wpo_ref[...], wout_ref[...]
        scale = sc_ref[...]
        f = _mix_core(zr, pooled, outs, lses, wbd_ref, scale, wao, wpo)

        dhb = dh_ref[...].astype(BF16)
        gwout_ref[...] += _dot(f["mg"].T.astype(BF16), dhb)
        dmg = _dot_nt(dhb, wout)
        sga, sgp, att, pol = f["sga"], f["sgp"], f["att"], f["pol"]
        datt = dmg * sga
        dpol = dmg * sgp
        dzr_ref[:, POOL_W:POOL_W + D_MODEL] = (dmg * att * sga * (1.0 - sga)).astype(BF16)
        dzr_ref[:, POOL_W + D_MODEL:] = (dmg * pol * sgp * (1.0 - sgp)).astype(BF16)
        dattb = datt.astype(BF16)
        dpolb = dpol.astype(BF16)
        gwao_ref[...] += _dot(f["a"].T.astype(BF16), dattb)
        gwpo_ref[...] += _dot(f["p"].T.astype(BF16), dpolb)
        da = _dot_nt(dattb, wao)
        dp = _dot_nt(dpolb, wpo)

        gsc_ref[...] += jnp.sum(f["mixed"] * dp, axis=0, keepdims=True)
        dmixed = (dp * scale).astype(BF16)
        pooled_t = pooled.T.astype(BF16)
        for g, (start, _) in enumerate(GROUP_WIN):
            gwbd_ref[g] += _dot(pooled_t[start:start + WIN, :], dmixed[:, start:start + WIN])
        dpooled = _group_matmul(dmixed, wbd_ref, transposed=True)
        gbuf[0:ts] = dpooled * inv_count
        m = ts + HALO
        t2[0:m - 8] = gbuf[0:m - 8] + gbuf[1:m - 7]
        t4[0:m - 16] = t2[0:m - 16] + t2[2:m - 14]
        t8[0:m - 24] = t4[0:m - 24] + t4[4:m - 20]
        t16 = t8[0:ts] + t8[8:ts + 8]
        col = lax.broadcasted_iota(jnp.int32, (ts, POOL_W), 1)
        back = jnp.where(col < POOL_GW, t2[0:ts],
                         jnp.where(col < 2 * POOL_GW, t4[0:ts], jnp.where(col < 3 * POOL_GW, t8[0:ts], t16)))
        dzr_ref[:, 0:POOL_W] = (back - dpooled).astype(BF16)
        gbuf[ts:ts + HALO] = gbuf[0:HALO]

        head_of = lax.broadcasted_iota(jnp.int32, (ts, GROUP_W), 1) // HEAD_DIM
        prod = da * f["a"]
        inner = jnp.zeros((ts, GROUP_W), F32)
        for h in range(4):
            hm = head_of == h
            tot = jnp.sum(jnp.where(hm, prod, 0.0), axis=1, keepdims=True)
            inner = jnp.where(hm, tot, inner)
        for g, (do_ref, c_ref) in enumerate(((do0, c0), (do1, c1), (do2, c2))):
            dil = ATT_GROUPS[g][1]
            _to_class_order(f["wts"][g] * da, dil, rbuf, do_ref, 0)
            _to_class_order(-f["wts"][g] * inner, dil, rbuf, c_ref, 0)

        @pl.when(j == n - 1)
        def _():
            pairs = ((gwout_ref, gwout_hbm), (gwao_ref, gwao_hbm), (gwpo_ref, gwpo_hbm), (gwbd_ref, gwbd_hbm))
            copies = [pltpu.make_async_copy(src, dst, sem.at[k]) for k, (src, dst) in enumerate(pairs)]
            for cp in copies:
                cp.start()
            for cp in copies:
                cp.wait()

    idx = lambda j: n - 1 - j
    do_shapes = [jax.ShapeDtypeStruct((s, GROUP_W), BF16), jax.ShapeDtypeStruct((4, s // 4, GROUP_W), BF16),
                 jax.ShapeDtypeStruct((16, s // 16, GROUP_W), BF16)]
    c_shapes = [jax.ShapeDtypeStruct(a.shape, F32) for a in do_shapes]
    weights = (wbd, wao, wpo, wout)
    grad_shapes = [(D_MODEL, D_MODEL), (GROUP_W, D_MODEL), (POOL_W, D_MODEL), (len(GROUP_WIN), WIN, WIN)]
    tile_buf = pltpu.VMEM((ts + HALO, POOL_W), F32)
    outs = pl.pallas_call(
        body, name="mix_bwd", grid=(n,),
        in_specs=[pl.BlockSpec((ts, D_MODEL), lambda j: (idx(j), 0)),
                  pl.BlockSpec((ts, REST_W), lambda j: (idx(j), 0)),
                  pl.BlockSpec((HALO, POOL_W), lambda j: (jnp.maximum(idx(j) * (ts // HALO) - 1, 0), 0)),
                  pl.BlockSpec((None, ts, POOL_W), lambda j: (jnp.minimum(idx(j), 1), 0, 0))]
                 + _dil_specs(ts, GROUP_W, idx) + _dil_specs(ts, GROUP_W, idx)
                 + [pl.BlockSpec((1, POOL_W), lambda j: (0, 0))] + [ANY] * 4,
        out_specs=[pl.BlockSpec((ts, REST_W), lambda j: (idx(j), 0))]
                  + _dil_specs(ts, GROUP_W, idx) + _dil_specs(ts, GROUP_W, idx)
                  + [pl.BlockSpec((1, POOL_W), lambda j: (0, 0))] + [ANY] * 4,
        out_shape=[jax.ShapeDtypeStruct((s, REST_W), BF16)] + do_shapes + c_shapes
                  + [jax.ShapeDtypeStruct((1, POOL_W), F32)]
                  + [jax.ShapeDtypeStruct(shape, F32) for shape in grad_shapes],
        scratch_shapes=[tile_buf] * 8 + [pltpu.VMEM((GROUP_W // LANES, ts, LANES), F32)]
                       + [pltpu.VMEM(w.shape, BF16) for w in weights]
                       + [pltpu.VMEM(shape, F32) for shape in grad_shapes]
                       + [pltpu.SemaphoreType.DMA((4,))],
        compiler_params=_params(dimension_semantics=("arbitrary",)),
    )(dh1, zr, zr, _inverse_counts(ts), *o_dil, *l_dil, scale, wbd, wao, wpo, wout)
    dzr, do_dil, c_dil, g_scale = outs[0], outs[1:4], outs[4:7], outs[7]
    g_out, g_ao, g_po, g_bd = outs[8:]
    return dzr, do_dil, c_dil, (g_out, g_ao, g_po, g_bd, g_scale)


FF_CHUNK = 1024


def _mlp_fwd_bwd(h1, tgt, g2, g3, wmi, wmo, ts):
    s = h1.shape[0]
    n = s // ts
    nchunk = D_FF // FF_CHUNK

    def body(h1_ref, t_ref, g2_ref, g3_ref, wmi_hbm, wmo_hbm,
             dh1_ref, mt_ref, dh2t_ref, hid_ref, df_ref, loss_ref, dg2_ref, dg3_ref,
             wmi, wmo, relu_buf, sem):
        i = pl.program_id(0)
        _load_resident(i, [(wmi_hbm, wmi), (wmo_hbm, wmo)], sem)

        @pl.when(i == 0)
        def _():
            loss_ref[...] = jnp.zeros_like(loss_ref)
            dg2_ref[...] = jnp.zeros_like(dg2_ref)
            dg3_ref[...] = jnp.zeros_like(dg3_ref)

        h1 = h1_ref[...]
        g2 = g2_ref[...]
        g3 = g3_ref[...]
        r2 = _rms(h1)
        n2 = h1 * r2
        m = n2 * g2
        mb = m.astype(BF16)
        mt_ref[...] = m.T.astype(BF16)
        h2 = h1
        for c in range(nchunk):
            cols = slice(c * FF_CHUNK, (c + 1) * FF_CHUNK)
            rl = jnp.maximum(_dot(mb, wmi[:, cols]), 0.0)
            relu_buf[:, cols] = rl
            hb = (rl * rl).astype(BF16)
            hid_ref[:, cols] = hb
            h2 = h2 + _dot(hb, wmo[cols, :])
        r3 = _rms(h2)
        n3 = h2 * r3
        diff = n3 * g3 - t_ref[...]
        loss_ref[...] += jnp.sum(0.5 * jnp.sum(diff * diff, axis=1, keepdims=True) / D_MODEL,
                                 axis=0, keepdims=True)
        dy = diff * (1.0 / D_MODEL)
        dg3_ref[...] += jnp.sum(dy * n3, axis=0, keepdims=True)
        dh2 = _rms_bwd(dy * g3, n3, r3)
        dh2b = dh2.astype(BF16)
        dh2t_ref[...] = dh2.T.astype(BF16)
        dm = jnp.zeros((ts, D_MODEL), F32)
        for c in range(nchunk):
            cols = slice(c * FF_CHUNK, (c + 1) * FF_CHUNK)
            dfb = (_dot_nt(dh2b, wmo[cols, :]) * (2.0 * relu_buf[:, cols])).astype(BF16)
            df_ref[:, cols] = dfb
            dm = dm + _dot_nt(dfb, wmi[:, cols])
        dg2_ref[...] += jnp.sum(dm * n2, axis=0, keepdims=True)
        dh1_ref[...] = dh2 + _rms_bwd(dm * g2, n2, r2)

    row = lambda w: pl.BlockSpec((ts, w), lambda i: (i, 0))
    colb = pl.BlockSpec((D_MODEL, ts), lambda i: (0, i))
    vec = pl.BlockSpec((1, D_MODEL), lambda i: (0, 0))
    return pl.pallas_call(
        body, name="mlp_fwd_bwd", grid=(n,),
        in_specs=[row(D_MODEL), row(D_MODEL), vec, vec, ANY, ANY],
        out_specs=[row(D_MODEL), colb, colb, row(D_FF), row(D_FF),
                   pl.BlockSpec((1, 1), lambda i: (0, 0)), vec, vec],
        out_shape=[jax.ShapeDtypeStruct((s, D_MODEL), F32),
                   jax.ShapeDtypeStruct((D_MODEL, s), BF16), jax.ShapeDtypeStruct((D_MODEL, s), BF16),
                   jax.ShapeDtypeStruct((s, D_FF), BF16), jax.ShapeDtypeStruct((s, D_FF), BF16),
                   jax.ShapeDtypeStruct((1, 1), F32),
                   jax.ShapeDtypeStruct((1, D_MODEL), F32), jax.ShapeDtypeStruct((1, D_MODEL), F32)],
        scratch_shapes=[pltpu.VMEM((D_MODEL, D_FF), BF16), pltpu.VMEM((D_FF, D_MODEL), BF16),
                        pltpu.VMEM((ts, D_FF), F32), pltpu.SemaphoreType.DMA((2,))],
        compiler_params=_params(dimension_semantics=("arbitrary",)),
    )(h1, tgt, g2, g3, wmi, wmo)


def _in_proj_bwd(x, dh1, dzr, dqkv_dil, g1, w_in, ts):
    s = x.shape[0]
    n = s // ts

    def body(x_ref, dh_ref, dzr_ref, q0, q1, q2, g_ref, w_hbm, dx_ref, dz_ref, dg_ref, w_ref, qbuf, sem):
        i = pl.program_id(0)
        _load_resident(i, [(w_hbm, w_ref)], sem)

        @pl.when(i == 0)
        def _():
            dg_ref[...] = jnp.zeros_like(dg_ref)

        for g, dqkv in enumerate(_gather_rows((q0, q1, q2), qbuf, ts)):
            for sec in range(3):
                c0 = sec * ATT_W + g * GROUP_W
                dz_ref[:, c0:c0 + GROUP_W] = dqkv[:, sec * GROUP_W:(sec + 1) * GROUP_W].astype(BF16)
        dz_ref[:, 3 * ATT_W:] = dzr_ref[...]
        du = _dot_nt(dz_ref[...], w_ref[...])
        x = x_ref[...]
        r1 = _rms(x)
        n1 = x * r1
        g1 = g_ref[...]
        dg_ref[...] += jnp.sum(du * n1, axis=0, keepdims=True)
        dx_ref[...] = dh_ref[...] + _rms_bwd(du * g1, n1, r1)

    row = lambda w: pl.BlockSpec((ts, w), lambda i: (i, 0))
    vec = pl.BlockSpec((1, D_MODEL), lambda i: (0, 0))
    return pl.pallas_call(
        body, name="in_proj_bwd", grid=(n,),
        in_specs=[row(D_MODEL), row(D_MODEL), row(REST_W)] + _dil_specs(ts, ATT_W, lambda i: i) + [vec, ANY],
        out_specs=[row(D_MODEL), row(N_IN), vec],
        out_shape=[jax.ShapeDtypeStruct((s, D_MODEL), F32), jax.ShapeDtypeStruct((s, N_IN), BF16),
                   jax.ShapeDtypeStruct((1, D_MODEL), F32)],
        scratch_shapes=[pltpu.VMEM((D_MODEL, N_IN), BF16), pltpu.VMEM((ATT_W // LANES, ts, LANES), F32),
                        pltpu.SemaphoreType.DMA((1,))],
        compiler_params=_params(dimension_semantics=("arbitrary",)),
    )(x, dh1, dzr, *dqkv_dil, g1, w_in)


GRAD_PASS = 2


def _weight_grad(at, b, name, transpose_out, tk=2048):
    m, s = at.shape
    nn = b.shape[1]
    tn = nn // N_DEV
    tk = min(tk, s)
    nk = s // tk
    npass = N_DEV // GRAD_PASS
    oshape = (tn, m) if transpose_out else (m, tn)
    owner = lambda jj: N_DEV - 1 - jj
    order = jnp.stack([_linear(_peer(_my_place(), owner(jj))) for jj in range(N_DEV)]).astype(jnp.int32)
    sent_in = lambda p: [jj for jj in range(N_DEV - 1) if jj // GRAD_PASS == p]

    def body(order_ref, at_ref, *refs):
        b_refs, (got_ref, acc, res, send_sems, recv_sems, local_sem) = refs[:GRAD_PASS], refs[GRAD_PASS:]
        j, k = pl.program_id(0), pl.program_id(1)
        me = _my_place()
        mine = _linear(me)

        def send(jj):
            return pltpu.make_async_remote_copy(
                src_ref=res.at[(jj // GRAD_PASS) % 2, jj % GRAD_PASS], dst_ref=got_ref.at[mine],
                send_sem=send_sems.at[jj], recv_sem=recv_sems.at[jj],
                device_id=_peer(me, owner(jj)), device_id_type=pl.DeviceIdType.MESH)

        @pl.when(k == 0)
        def _():
            acc[...] = jnp.zeros_like(acc)

        a = at_ref[...]
        for g in range(GRAD_PASS):
            acc[g] += _dot(a, b_refs[g][...])

        @pl.when(k == nk - 1)
        def _():
            for p in range(2, npass):
                @pl.when(j == p)
                def _():
                    for jj in sent_in(p - 2):
                        send(jj).wait_send()

            for g in range(GRAD_PASS):
                r = acc[g]
                res[j % 2, g] = (r.T if transpose_out else r).astype(BF16)
            for p in range(npass):
                @pl.when(j == p)
                def _():
                    for jj in sent_in(p):
                        send(jj).start()

            @pl.when(j == npass - 1)
            def _():
                own = pltpu.make_async_copy(res.at[(npass - 1) % 2, GRAD_PASS - 1], got_ref.at[mine], local_sem.at[0])
                own.start()
                for p in range(max(npass - 2, 0), npass):
                    for jj in sent_in(p):
                        send(jj).wait_send()
                for jj in range(N_DEV - 1):
                    send(jj).wait_recv()
                own.wait()

    b_spec = lambda g: pl.BlockSpec((tk, tn), lambda j, k, o: (k, o[GRAD_PASS * j + g]))
    return pl.pallas_call(
        body, name=name,
        grid_spec=pltpu.PrefetchScalarGridSpec(
            num_scalar_prefetch=1, grid=(npass, nk),
            in_specs=[pl.BlockSpec((m, tk), lambda j, k, o: (0, k))] + [b_spec(g) for g in range(GRAD_PASS)],
            out_specs=ANY,
            scratch_shapes=[pltpu.VMEM((GRAD_PASS, m, tn), F32), pltpu.VMEM((2, GRAD_PASS) + oshape, BF16),
                            pltpu.SemaphoreType.DMA((N_DEV - 1,)), pltpu.SemaphoreType.DMA((N_DEV - 1,)),
                            pltpu.SemaphoreType.DMA((1,))]),
        out_shape=jax.ShapeDtypeStruct((N_DEV,) + oshape, BF16),
        compiler_params=_params(dimension_semantics=("arbitrary", "arbitrary")),
    )(order, at, *([b] * GRAD_PASS))


def _my_place():
    x, y, c = lax.axis_index("x"), lax.axis_index("y"), lax.axis_index("c")
    return x, y, c


def _peer(place, k):
    x, y, c = place
    return (1 - x if k & 4 else x, 1 - y if k & 2 else y, 1 - c if k & 1 else c)


def _linear(place):
    x, y, c = place
    return 4 * x + 2 * y + c


class _Exchange:
    passes_on = False

    def __init__(self, arrays, gather):
        self.arrays, self.gather, self.nw = list(arrays), list(gather), len(arrays)
        self.out_shape = []
        for a, g in zip(arrays, gather):
            block = a.shape if g else a.shape[1:]
            self.out_shape.append(jax.ShapeDtypeStruct((N_DEV,) + tuple(block), a.dtype))
        self.specs = [ANY] * self.nw
        self.scratch = [pltpu.SemaphoreType.DMA((self.nw, N_DEV - 1)), pltpu.SemaphoreType.DMA((self.nw, N_DEV - 1)),
                        pltpu.SemaphoreType.DMA((self.nw,))]

    def _copies(self, ins, outs, sems):
        send_sems, recv_sems, local_sems = sems
        me = _my_place()
        mine = _linear(me)
        copies = []
        for w in range(self.nw):
            src = ins[w] if self.gather[w] else ins[w].at[mine]
            copies.append(pltpu.make_async_copy(src, outs[w].at[mine], local_sems.at[w]))
        for k in range(1, N_DEV):
            peer = _peer(me, k)
            for w in range(self.nw):
                src = ins[w] if self.gather[w] else ins[w].at[_linear(peer)]
                copies.append(pltpu.make_async_remote_copy(
                    src_ref=src, dst_ref=outs[w].at[mine],
                    send_sem=send_sems.at[w, k - 1], recv_sem=recv_sems.at[w, k - 1],
                    device_id=peer, device_id_type=pl.DeviceIdType.MESH))
        return copies

    def start(self, ins, outs, sems):
        for cp in self._copies(ins, outs, sems):
            cp.start()

    def wait(self, ins, outs, sems):
        copies = self._copies(ins, outs, sems)
        for cp in copies[self.nw:]:
            cp.wait_recv()
        for cp in copies[self.nw:]:
            cp.wait_send()
        for cp in copies[:self.nw]:
            cp.wait()


class _Gather:
    passes_on = True

    def __init__(self, arrays):
        self.arrays, self.nw = list(arrays), len(arrays)
        self.out_shape = [jax.ShapeDtypeStruct((N_DEV,) + tuple(a.shape), a.dtype) for a in arrays]
        self.specs = [ANY] * self.nw
        self.scratch = [pltpu.SemaphoreType.DMA((self.nw, N_DEV - 1)), pltpu.SemaphoreType.DMA((self.nw, N_DEV - 1)),
                        pltpu.SemaphoreType.DMA((self.nw,))]

    @staticmethod
    def _places():
        x, y, c = _my_place()
        return (x, y, c), (x, y, 1 - c), [(1 - x, y), (x, 1 - y), (1 - x, 1 - y)]

    @staticmethod
    def _copy(outs, sems, w, k, block, to, src=None):
        rows = outs[w].at[_linear(block)]
        return pltpu.make_async_remote_copy(
            src_ref=rows if src is None else src, dst_ref=rows, send_sem=sems[0].at[w, k], recv_sem=sems[1].at[w, k],
            device_id=to, device_id_type=pl.DeviceIdType.MESH)

    def _first(self, ins, outs, sems, w):
        me, sibling, chips = self._places()
        return ([self._copy(outs, sems, w, 0, me, sibling, src=ins[w])]
                + [self._copy(outs, sems, w, 1 + j, me, (*chip, me[2]), src=ins[w]) for j, chip in enumerate(chips)])

    def _passed(self, outs, sems, w):
        me, sibling, chips = self._places()
        return [self._copy(outs, sems, w, 4 + j, (*chip, me[2]), sibling) for j, chip in enumerate(chips)]

    def _local(self, ins, outs, sems, w):
        return pltpu.make_async_copy(ins[w], outs[w].at[_linear(self._places()[0])], sems[2].at[w])

    def start(self, ins, outs, sems):
        for w in range(self.nw):
            self._local(ins, outs, sems, w).start()
            for cp in self._first(ins, outs, sems, w):
                cp.start()

    def pass_on(self, ins, outs, sems):
        me, sibling, chips = self._places()
        for j, chip in enumerate(chips):
            for w in range(self.nw):
                self._copy(outs, sems, w, 1 + j, (*chip, me[2]), me).wait_recv()
                self._passed(outs, sems, w)[j].start()

    def wait(self, ins, outs, sems):
        me, sibling, chips = self._places()
        for w in range(self.nw):
            self._copy(outs, sems, w, 0, sibling, me).wait_recv()
            for j, chip in enumerate(chips):
                self._copy(outs, sems, w, 4 + j, (*chip, sibling[2]), me).wait_recv()
            for cp in self._first(ins, outs, sems, w) + self._passed(outs, sems, w):
                cp.wait_send()
            self._local(ins, outs, sems, w).wait()


def _exchange(arrays, gather, name):
    ex = _Exchange(arrays, gather)

    def body(*refs):
        ins, outs, sems = refs[:ex.nw], refs[ex.nw:2 * ex.nw], refs[2 * ex.nw:]
        ex.start(ins, outs, sems)
        ex.wait(ins, outs, sems)

    return pl.pallas_call(
        body, name=name, in_specs=ex.specs, out_specs=ex.specs, out_shape=ex.out_shape, scratch_shapes=ex.scratch,
    )(*arrays)


def _adamw(parts, w, m, v, name, tr):
    rows, cols = w.shape
    tr = min(tr, rows)

    def body(p_ref, w_ref, m_ref, v_ref, g_ref, d_ref, nm_ref, nv_ref):
        g = p_ref[0].astype(F32)
        for j in range(1, N_DEV):
            g = g + p_ref[j].astype(F32)
        nm = ADAM_B1 * m_ref[...] + (1.0 - ADAM_B1) * g
        nv = ADAM_B2 * v_ref[...] + (1.0 - ADAM_B2) * (g * g)
        m_hat = nm / (1.0 - ADAM_B1 ** ADAM_STEP)
        v_hat = nv / (1.0 - ADAM_B2 ** ADAM_STEP)
        g_ref[...] = g
        d_ref[...] = -ADAM_LR * (m_hat / (jnp.sqrt(v_hat) + ADAM_EPS) + ADAM_WD * w_ref[...])
        nm_ref[...] = nm
        nv_ref[...] = nv

    blk = pl.BlockSpec((tr, cols), lambda i: (i, 0))
    return pl.pallas_call(
        body, name=name, grid=(rows // tr,),
        in_specs=[pl.BlockSpec((N_DEV, tr, cols), lambda i: (0, i, 0)), blk, blk, blk],
        out_specs=[blk] * 4,
        out_shape=[jax.ShapeDtypeStruct((rows, cols), F32)] * 4,
        compiler_params=_params(dimension_semantics=("arbitrary",)),
    )(parts, w, m, v)


def _pack_small(values):
    rows = []
    for v in values:
        size = math.prod(v.shape)
        padded = -(-size // 1024) * 1024
        flat = jnp.pad(v.reshape(-1).astype(F32), (0, padded - size))
        rows.append(flat.reshape(padded // 128, 128))
    return jnp.concatenate(rows, axis=0) if len(rows) > 1 else rows[0]


def _unpack_small(packed, shapes):
    out, r = [], 0
    for shape in shapes:
        size = math.prod(shape)
        nrow = -(-size // 1024) * 8
        out.append(packed[r:r + nrow].reshape(-1)[:size].reshape(shape))
        r += nrow
    return out


def kernel(x, norm_mix_g, w_in, w_att_out, w_pool_grp, pool_scale, w_pool_out, w_out, norm_mlp_g, w_mlp_in, w_mlp_out, norm_final_g, loss_target, m_norm_mix_g, m_w_in, m_w_att_out, m_w_pool_grp, m_pool_scale, m_w_pool_out, m_w_out, m_norm_mlp_g, m_w_mlp_in, m_w_mlp_out, m_norm_final_g, v_norm_mix_g, v_w_in, v_w_att_out, v_w_pool_grp, v_pool_scale, v_w_pool_out, v_w_out, v_norm_mlp_g, v_w_mlp_in, v_w_mlp_out, v_norm_final_g):
    x, tgt = x[0], loss_target[0]
    s = x.shape[0]
    g1, g2, g3 = norm_mix_g, norm_mlp_g, norm_final_g.reshape(1, D_MODEL)
    shards = [w_in[0], w_att_out[0], w_pool_out[0], w_out[0], w_mlp_in[0], w_mlp_out[0]]
    wire = [a.astype(BF16) for a in shards]
    cols = lambda a: jnp.transpose(a, (1, 0, 2)).reshape(a.shape[1], N_DEV * a.shape[2])
    rows = lambda a: a.reshape(N_DEV * a.shape[1], a.shape[2])
    blocks_of_cols = lambda a: jnp.transpose(a.reshape(a.shape[0], N_DEV, a.shape[1] // N_DEV), (1, 0, 2))
    blocks_of_rows = lambda a: a.reshape(N_DEV, a.shape[0] // N_DEV, a.shape[1])
    wbd = _window_weights(w_pool_grp[0]).astype(BF16)

    (u, ut), (first,) = _rms_u(x, g1, 1024, _Gather(wire[:1]))
    f_in = cols(first)
    (qkv0, qkv1, qkv2, zr), later = _in_proj_fwd(u, f_in, 1024, _Gather(wire[1:]))
    f_ao, f_po, f_out, f_mi, f_mo = cols(later[0]), cols(later[1]), rows(later[2]), cols(later[3]), rows(later[4])
    qkv_dil = (qkv0, qkv1, qkv2)
    flat = lambda a: a.reshape(s, a.shape[-1])
    shaped = lambda a, g: a if g == 0 else a.reshape(ATT_GROUPS[g][1], s // ATT_GROUPS[g][1], a.shape[-1])
    o_dil, l_dil = [], []
    for g in range(N_GROUPS):
        o, l = _attn_fwd(flat(qkv_dil[g]), g)
        o_dil.append(shaped(o, g))
        l_dil.append(shaped(l, g))
    h1 = _mix_fwd(x, zr, o_dil, l_dil, wbd, pool_scale, f_ao, f_po, f_out, ts=512)

    dh1, mt, dh2t, hid, df, loss, dg2, dg3 = _mlp_fwd_bwd(h1, tgt, g2, g3, f_mi, f_mo, ts=256)
    got = {"w_mlp_in": _weight_grad(mt, df, "grad_w_mlp_in", transpose_out=False),
           "w_mlp_out": _weight_grad(dh2t, hid, "grad_w_mlp_out", transpose_out=True)}
    dzr, do_dil, c_dil, (g_out, g_ao, g_po, g_bd, g_scale) = _mix_bwd(
        dh1, zr, o_dil, l_dil, wbd, pool_scale, f_ao, f_po, f_out, ts=256)
    g_grp = jnp.stack([g_bd[g, off:off + POOL_GW, off:off + POOL_GW] for g, (_, off) in enumerate(GROUP_WIN)])
    early = [_Exchange([blocks_of_rows(g_out).astype(BF16)], [False]),
             _Exchange([blocks_of_cols(g_ao).astype(BF16), blocks_of_cols(g_po).astype(BF16)], [False, False]),
             _Exchange([_pack_small([g_grp])], [True])]
    dqkv_dil, arrived = [], []
    for g in range(N_GROUPS):
        dqkv, rode = _attn_bwd(flat(qkv_dil[g]), flat(do_dil[g]), flat(c_dil[g]), g, early[g])
        dqkv_dil.append(shaped(dqkv, g))
        arrived += list(rode)
    got["w_out"], got["w_att_out"], got["w_pool_out"], got_grp = arrived
    dx, dz, dg1 = _in_proj_bwd(x, dh1, dzr, dqkv_dil, g1, f_in, ts=512)
    got["w_in"] = _weight_grad(ut, dz, "grad_w_in", transpose_out=False)
    got_vec = _exchange([_pack_small([loss, dg1, g_scale, dg2, dg3])], [True], "gather_small_grads")[0]

    names = ["w_in", "w_att_out", "w_pool_out", "w_out", "w_mlp_in", "w_mlp_out"]
    ms = [m_w_in, m_w_att_out, m_w_pool_out, m_w_out, m_w_mlp_in, m_w_mlp_out]
    vs = [v_w_in, v_w_att_out, v_w_pool_out, v_w_out, v_w_mlp_in, v_w_mlp_out]
    upd = {}
    for k, name in enumerate(names):
        res = _adamw(got[name], shards[k], ms[k][0], vs[k][0], "adamw_" + name, tr=256)
        upd[name] = [a[None] for a in res]

    res = _adamw(got_grp, _pack_small([w_pool_grp]), _pack_small([m_w_pool_grp]), _pack_small([v_w_pool_grp]),
                 "adamw_w_pool_grp", tr=2048)
    upd["w_pool_grp"] = [_unpack_small(a, [w_pool_grp.shape])[0] for a in res]
    vec_w = [jnp.zeros((1,), F32), norm_mix_g, pool_scale, norm_mlp_g, norm_final_g]
    vec_m = [jnp.zeros((1,), F32), m_norm_mix_g, m_pool_scale, m_norm_mlp_g, m_norm_final_g]
    vec_v = [jnp.ones((1,), F32), v_norm_mix_g, v_pool_scale, v_norm_mlp_g, v_norm_final_g]
    res = _adamw(got_vec, _pack_small(vec_w), _pack_small(vec_m), _pack_small(vec_v), "adamw_vectors", tr=2048)
    shapes = [(), norm_mix_g.shape, pool_scale.shape, norm_mlp_g.shape, norm_final_g.shape]
    unpacked = [_unpack_small(a, shapes) for a in res]
    for k, name in enumerate(["loss", "norm_mix_g", "pool_scale", "norm_mlp_g", "norm_final_g"]):
        upd[name] = [unpacked[q][k] for q in range(4)]

    order = ["norm_mix_g", "w_in", "w_att_out", "w_pool_grp", "pool_scale", "w_pool_out", "w_out", "norm_mlp_g",
             "w_mlp_in", "w_mlp_out", "norm_final_g"]
    out = [upd["loss"][0], dx[None]]
    for q in range(4):
        out += [upd[name][q] for name in order]
    return tuple(out)
```

```python
import functools
import math

import jax
import jax.numpy as jnp
from jax import lax
from jax.experimental import pallas as pl
from jax.experimental.pallas import tpu as pltpu

F32 = jnp.float32
BF16 = jnp.bfloat16

D_MODEL = 1024
HEAD_DIM = 64
GROUP_W = 256
ATT_GROUPS = ((128, 1), (512, 4), (2048, 16))
N_GROUPS = 3
BLK = 128
ATT_W = 768
POOL_W = 768
POOL_GW = 192
D_FF = 4096
N_IN = 5120
REST_W = N_IN - 3 * ATT_W
NORM_EPS = 1e-6
ALIBI_MAX_BIAS = 8.0
N_DEV = 8
HALO = 32

ADAM_LR = 0.001
ADAM_B1 = 0.9
ADAM_B2 = 0.999
ADAM_EPS = 1e-08
ADAM_WD = 0.01
ADAM_STEP = 10

VMEM_LIMIT = 56 * 1024 * 1024
ANY = pl.BlockSpec(memory_space=pl.ANY)


def _params(**kw):
    return pltpu.CompilerParams(vmem_limit_bytes=VMEM_LIMIT, **kw)


def _dot(a, b):
    return jnp.dot(a, b, preferred_element_type=F32)


def _dot_nt(a, b):
    return lax.dot_general(a, b, (((1,), (1,)), ((), ())), preferred_element_type=F32)


def _dot_tn(a, b):
    return lax.dot_general(a, b, (((0,), (0,)), ((), ())), preferred_element_type=F32)


def _slope(head):
    return 2.0 ** (-ALIBI_MAX_BIAS * (head + 1.0) / 12.0)


def _load_resident(step, pairs, sem):
    @pl.when(step == 0)
    def _():
        copies = [pltpu.make_async_copy(src, dst, sem.at[n]) for n, (src, dst) in enumerate(pairs)]
        for cp in copies:
            cp.start()
        for cp in copies:
            cp.wait()


LANES = 128


def _to_class_order(value, dil, buf, ref, col0):
    ts, w = value.shape
    if dil == 1:
        ref[:, col0:col0 + w] = value.astype(ref.dtype)
        return
    for c in range(w // LANES):
        buf[c] = value[:, c * LANES:(c + 1) * LANES]
        for r in range(dil):
            ref[r, :, col0 + c * LANES:col0 + (c + 1) * LANES] = (
                buf[c, pl.ds(r, ts // dil, stride=dil), :].astype(ref.dtype))


def _to_token_order(ref, dil, buf, ts):
    if dil == 1:
        return ref[...].astype(F32)
    w = ref.shape[-1]
    for c in range(w // LANES):
        for r in range(dil):
            buf[c, pl.ds(r, ts // dil, stride=dil), :] = ref[r, :, c * LANES:(c + 1) * LANES].astype(F32)
    return jnp.concatenate([buf[c] for c in range(w // LANES)], axis=1)


def _sigmoid(x):
    return 0.5 * jnp.tanh(0.5 * x) + 0.5


def _rms(x):
    return lax.rsqrt(jnp.mean(x * x, axis=-1, keepdims=True) + NORM_EPS)


def _rms_bwd(dn, n, r):
    return r * (dn - n * jnp.mean(dn * n, axis=-1, keepdims=True))


def _split_refs(refs, counts):
    out, at = [], 0
    for c in counts:
        out.append(refs[at:at + c])
        at += c
    return out


def _carry_start(ex, step, ins, outs, sems):
    @pl.when(step == 0)
    def _():
        ex.start(ins, outs, sems)


def _carry_wait(ex, step, last, ins, outs, sems, pass_at=None):
    if ex.passes_on:
        @pl.when(step == (last if pass_at is None else pass_at))
        def _():
            ex.pass_on(ins, outs, sems)

    @pl.when(step == last)
    def _():
        ex.wait(ins, outs, sems)


def _rms_u(x, g1, ts, ex):
    s = x.shape[0]
    n = s // ts

    def body(*refs):
        (x_ref, g_ref), ex_ins, (u_ref, ut_ref), ex_outs, ex_sems = _split_refs(refs, (2, ex.nw, 2, ex.nw, 3))
        i = pl.program_id(0)
        _carry_start(ex, i, ex_ins, ex_outs, ex_sems)
        x = x_ref[...]
        u = x * _rms(x) * g_ref[...]
        u_ref[...] = u.astype(BF16)
        ut_ref[...] = u.T.astype(BF16)
        _carry_wait(ex, i, n - 1, ex_ins, ex_outs, ex_sems)

    outs = pl.pallas_call(
        body, name="rms_u", grid=(n,),
        in_specs=[pl.BlockSpec((ts, D_MODEL), lambda i: (i, 0)), pl.BlockSpec((1, D_MODEL), lambda i: (0, 0))]
                 + ex.specs,
        out_specs=[pl.BlockSpec((ts, D_MODEL), lambda i: (i, 0)), pl.BlockSpec((D_MODEL, ts), lambda i: (0, i))]
                  + ex.specs,
        out_shape=[jax.ShapeDtypeStruct((s, D_MODEL), BF16), jax.ShapeDtypeStruct((D_MODEL, s), BF16)] + ex.out_shape,
        scratch_shapes=ex.scratch,
        compiler_params=_params(dimension_semantics=("arbitrary",)),
    )(x, g1, *ex.arrays)
    return outs[:2], outs[2:]


def _in_proj_fwd(u, w_in, ts, ex):
    s = u.shape[0]
    n = s // ts
    dils = [d for _, d in ATT_GROUPS]

    def body(*refs):
        (u_ref, w_hbm), ex_ins, (q0_ref, q1_ref, q2_ref, zr_ref), ex_outs, (w_ref, zbuf, sem), ex_sems = (
            _split_refs(refs, (2, ex.nw, 4, ex.nw, 3, 3)))
        i = pl.program_id(0)
        _carry_start(ex, i, ex_ins, ex_outs, ex_sems)
        _load_resident(i, [(w_hbm, w_ref)], sem)
        ub = u_ref[...]
        outs = (q0_ref, q1_ref, q2_ref)
        for sec in range(3):
            for g in range(N_GROUPS):
                c0 = sec * ATT_W + g * GROUP_W
                zc = _dot(ub, w_ref[:, c0:c0 + GROUP_W])
                _to_class_order(zc, dils[g], zbuf, outs[g], sec * GROUP_W)
        for c0 in range(0, REST_W, 256):
            zr_ref[:, c0:c0 + 256] = _dot(ub, w_ref[:, 3 * ATT_W + c0:3 * ATT_W + c0 + 256]).astype(BF16)
        _carry_wait(ex, i, n - 1, ex_ins, ex_outs, ex_sems, pass_at=max(n - 3, 0))

    outs = pl.pallas_call(
        body, name="in_proj_fwd", grid=(n,),
        in_specs=[pl.BlockSpec((ts, D_MODEL), lambda i: (i, 0)), ANY] + ex.specs,
        out_specs=[pl.BlockSpec((ts, ATT_W), lambda i: (i, 0)),
                   pl.BlockSpec((4, ts // 4, ATT_W), lambda i: (0, i, 0)),
                   pl.BlockSpec((16, ts // 16, ATT_W), lambda i: (0, i, 0)),
                   pl.BlockSpec((ts, REST_W), lambda i: (i, 0))] + ex.specs,
        out_shape=[jax.ShapeDtypeStruct((s, ATT_W), BF16),
                   jax.ShapeDtypeStruct((4, s // 4, ATT_W), BF16),
                   jax.ShapeDtypeStruct((16, s // 16, ATT_W), BF16),
                   jax.ShapeDtypeStruct((s, REST_W), BF16)] + ex.out_shape,
        scratch_shapes=[pltpu.VMEM((D_MODEL, N_IN), BF16), pltpu.VMEM((GROUP_W // LANES, ts, LANES), F32),
                        pltpu.SemaphoreType.DMA((1,))] + ex.scratch,
        compiler_params=_params(dimension_semantics=("arbitrary",)),
    )(u, w_in, *ex.arrays)
    return outs[:4], outs[4:]


ATT_TILE = 8 * BLK


HEADS = GROUP_W // HEAD_DIM
STACK = HEADS * BLK


SCORE_SCALE = HEAD_DIM ** -0.5


def _band_consts(group, dil):
    row = lax.broadcasted_iota(jnp.int32, (STACK, 2 * BLK), 0)
    kj = lax.broadcasted_iota(jnp.int32, (STACK, 2 * BLK), 1)
    head = row // BLK
    steps = BLK + (row % BLK) - kj
    slope = jnp.full((STACK, 2 * BLK), _slope(4 * group + HEADS - 1), F32)
    for h in range(HEADS - 1):
        slope = jnp.where(head == h, _slope(4 * group + h), slope)
    in_band = (steps >= 0) & (steps <= BLK)
    return jnp.where(in_band, slope * (steps.astype(F32) * float(dil)), jnp.inf), kj


def _first_key(block, nbc, nb, b):
    if nbc % nb == 0 and b != 0:
        return None
    return jnp.where((block % nbc) != 0, 0, BLK)


def _head_of_col():
    return lax.broadcasted_iota(jnp.int32, (BLK, GROUP_W), 1) // HEAD_DIM


def _stack_heads(xb):
    head_of = _head_of_col()
    return jnp.concatenate([jnp.where(head_of == h, xb, jnp.zeros_like(xb)) for h in range(HEADS)], axis=0)


def _unstack_heads(y):
    head_of = _head_of_col()
    acc = y[0:BLK]
    for h in range(1, HEADS):
        acc = jnp.where(head_of == h, y[h * BLK:(h + 1) * BLK], acc)
    return acc


def _per_head_cols(stacked, rhs, scale_rows=None):
    lane = lax.broadcasted_iota(jnp.int32, (BLK, LANES), 1)
    halves = []
    for pair in range(HEADS // 2):
        tile = rhs[:, pair * LANES:(pair + 1) * LANES]
        parts = []
        for h in (2 * pair, 2 * pair + 1):
            part = _dot(stacked[h * BLK:(h + 1) * BLK], tile)
            parts.append(part if scale_rows is None else part * scale_rows[h * BLK:(h + 1) * BLK])
        halves.append(jnp.where(lane < HEAD_DIM, parts[0], parts[1]))
    return jnp.concatenate(halves, axis=1)


def _per_head_rows(col):
    lane = lax.broadcasted_iota(jnp.int32, (BLK, LANES), 1)
    return jnp.concatenate(
        [jnp.where(lane < HEAD_DIM, col[(2 * pair) * BLK:(2 * pair + 1) * BLK], col[(2 * pair + 1) * BLK:(2 * pair + 2) * BLK])
         for pair in range(HEADS // 2)], axis=1)


def _band_softmax(qs, kb, penalty, kj, first_key):
    sc = _dot_nt(qs, kb) - penalty
    if first_key is not None:
        sc = jnp.where(kj >= first_key, sc, -jnp.inf)
    mx = jnp.max(sc, axis=1, keepdims=True)
    e = jnp.exp(sc - mx)
    return e, mx, jnp.sum(e, axis=1, keepdims=True)


def _attn_fwd(qkv, group):
    s = qkv.shape[0]
    dil = ATT_GROUPS[group][1]
    nbc = s // (BLK * dil)
    nb = ATT_TILE // BLK

    def body(q_ref, kc_ref, kp_ref, vc_ref, vp_ref, ol_ref, kbuf, vbuf):
        i = pl.program_id(0)
        kbuf[0:BLK] = kp_ref[...]
        kbuf[BLK:BLK + ATT_TILE] = kc_ref[...]
        vbuf[0:BLK] = vp_ref[...]
        vbuf[BLK:BLK + ATT_TILE] = vc_ref[...]
        penalty, kj = _band_consts(group, dil)
        blocks = range(nb)
        qss = [_stack_heads(q_ref[b * BLK:(b + 1) * BLK, :] * SCORE_SCALE) for b in blocks]
        soft = [_band_softmax(qss[b], kbuf[b * BLK:b * BLK + 2 * BLK, :], penalty, kj,
                              _first_key(i * nb + b, nbc, nb, b)) for b in blocks]
        for b in blocks:
            e, mx, den = soft[b]
            ol_ref[b * BLK:(b + 1) * BLK, 0:GROUP_W] = _per_head_cols(
                e.astype(BF16), vbuf[b * BLK:b * BLK + 2 * BLK, :], 1.0 / den)
        for b in blocks:
            e, mx, den = soft[b]
            ol_ref[b * BLK:(b + 1) * BLK, GROUP_W:] = _per_head_rows(mx + jnp.log(den))

    n = s // ATT_TILE
    cur = lambda c: pl.BlockSpec((ATT_TILE, GROUP_W), lambda i: (i, c))
    prev = lambda c: pl.BlockSpec((BLK, GROUP_W), lambda i: (jnp.maximum(i * nb - 1, 0), c))
    return pl.pallas_call(
        body, name=f"attn_fwd_g{group}", grid=(n,),
        in_specs=[cur(0), cur(1), prev(1), cur(2), prev(2)],
        out_specs=pl.BlockSpec((ATT_TILE, 2 * GROUP_W), lambda i: (i, 0)),
        out_shape=jax.ShapeDtypeStruct((s, 2 * GROUP_W), F32),
        scratch_shapes=[pltpu.VMEM((BLK + ATT_TILE, GROUP_W), BF16), pltpu.VMEM((BLK + ATT_TILE, GROUP_W), BF16)],
        compiler_params=_params(dimension_semantics=("arbitrary",)),
    )(qkv, qkv, qkv, qkv, qkv)


def _attn_bwd(qkv, dc, group, ex=None):
    s = qkv.shape[0]
    dil = ATT_GROUPS[group][1]
    nbc = s // (BLK * dil)
    nb = ATT_TILE // BLK
    n = s // ATT_TILE
    nex = ex.nw if ex else 0

    def body(*refs):
        ((q_ref, kc_ref, kp_ref, vc_ref, vp_ref, do_ref, c_ref), ex_ins, (out_ref,), ex_outs,
         (kbuf, vbuf, dqpend, dkpend, dvpend), ex_sems) = _split_refs(refs, (7, nex, 1, nex, 5, 3 if ex else 0))
        i = pl.program_id(0)
        if ex:
            _carry_start(ex, i, ex_ins, ex_outs, ex_sems)

        @pl.when(i == 0)
        def _():
            dqpend[...] = jnp.zeros_like(dqpend)
            dkpend[...] = jnp.zeros_like(dkpend)
            dvpend[...] = jnp.zeros_like(dvpend)

        out_ref[:, 0:GROUP_W] = dqpend[...]
        body_rows = slice(0, ATT_TILE - BLK)
        tail = slice(ATT_TILE - BLK, ATT_TILE)
        pends = ((dkpend, GROUP_W), (dvpend, 2 * GROUP_W))
        for pend, c0 in pends:
            out_ref[body_rows, c0:c0 + GROUP_W] = pend[body_rows, :].astype(BF16)

        @pl.when(i < n)
        def _():
            kbuf[0:BLK] = kp_ref[...]
            kbuf[BLK:BLK + ATT_TILE] = kc_ref[...]
            vbuf[0:BLK] = vp_ref[...]
            vbuf[BLK:BLK + ATT_TILE] = vc_ref[...]
            penalty, kj = _band_consts(group, dil)
            head_of = _head_of_col()
            blocks = range(nb)
            rows = [slice(b * BLK, (b + 1) * BLK) for b in blocks]
            kbs = [kbuf[b * BLK:b * BLK + 2 * BLK, :] for b in blocks]
            vbs = [vbuf[b * BLK:b * BLK + 2 * BLK, :] for b in blocks]
            qss = [_stack_heads(q_ref[rows[b], :] * SCORE_SCALE) for b in blocks]
            doss = [_stack_heads(do_ref[rows[b], :].astype(BF16)) for b in blocks]
            cors = []
            for b in blocks:
                cb = c_ref[rows[b], :]
                cors.append(jnp.concatenate(
                    [jnp.max(jnp.where(head_of == h, cb, -jnp.inf), axis=1, keepdims=True) for h in range(HEADS)],
                    axis=0))
            soft = [_band_softmax(qss[b], kbs[b], penalty, kj, _first_key(i * nb + b, nbc, nb, b)) for b in blocks]
            dps = [_dot_nt(doss[b], vbs[b]) for b in blocks]
            ps = [soft[b][0] * (1.0 / soft[b][2]) for b in blocks]
            dss = [(ps[b] * (dps[b] + cors[b])).astype(BF16) for b in blocks]
            for b in blocks:
                dqpend[rows[b], :] = _per_head_cols(dss[b], kbs[b] * SCORE_SCALE).astype(BF16)
            bands = [(_dot_tn(dss[b], qss[b]), _dot_tn(ps[b].astype(BF16), doss[b])) for b in blocks]
            for which, (pend, c0) in enumerate(pends):
                out_ref[tail, c0:c0 + GROUP_W] = (pend[tail, :] + bands[0][which][0:BLK]).astype(BF16)
                for b in range(nb):
                    own = bands[b][which][BLK:2 * BLK]
                    pend[b * BLK:(b + 1) * BLK, :] = own + bands[b + 1][which][0:BLK] if b + 1 < nb else own

        @pl.when(i == n)
        def _():
            for pend, c0 in pends:
                out_ref[tail, c0:c0 + GROUP_W] = pend[tail, :].astype(BF16)

        if ex:
            _carry_wait(ex, i, n, ex_ins, ex_outs, ex_sems)

    last = n - 1
    cur = lambda c: pl.BlockSpec((ATT_TILE, GROUP_W), lambda i: (jnp.minimum(i, last), c))
    prev = lambda c: pl.BlockSpec(
        (BLK, GROUP_W), lambda i: (jnp.maximum(jnp.minimum(i, last) * nb - 1, 0), c))
    outs = pl.pallas_call(
        body, name=f"attn_bwd_g{group}", grid=(n + 1,),
        in_specs=[cur(0), cur(1), prev(1), cur(2), prev(2), cur(0), cur(1)] + (ex.specs if ex else []),
        out_specs=[pl.BlockSpec((ATT_TILE, ATT_W), lambda i: (jnp.maximum(i - 1, 0), 0))] + (ex.specs if ex else []),
        out_shape=[jax.ShapeDtypeStruct((s, ATT_W), BF16)] + (ex.out_shape if ex else []),
        scratch_shapes=[pltpu.VMEM((BLK + ATT_TILE, GROUP_W), BF16), pltpu.VMEM((BLK + ATT_TILE, GROUP_W), BF16),
                        pltpu.VMEM((ATT_TILE, GROUP_W), BF16),
                        pltpu.VMEM((ATT_TILE, GROUP_W), F32), pltpu.VMEM((ATT_TILE, GROUP_W), F32)]
                       + (ex.scratch if ex else []),
        compiler_params=_params(dimension_semantics=("arbitrary",)),
    )(qkv, qkv, qkv, qkv, qkv, dc, dc, *(ex.arrays if ex else []))
    return (outs[0], outs[1:]) if ex else outs[0]


def _dil_specs(ts, width, idx):
    return [pl.BlockSpec((ts, width), lambda i: (idx(i), 0)),
            pl.BlockSpec((4, ts // 4, width), lambda i: (0, idx(i), 0)),
            pl.BlockSpec((16, ts // 16, width), lambda i: (0, idx(i), 0))]


def _gather_rows(refs, buf, ts):
    return [_to_token_order(refs[g], ATT_GROUPS[g][1], buf, ts) for g in range(N_GROUPS)]


def _inverse_counts(ts):
    col = jnp.arange(POOL_W)
    win = jnp.where(col < POOL_GW, 2, jnp.where(col < 2 * POOL_GW, 4, jnp.where(col < 3 * POOL_GW, 8, 16)))
    first = jnp.minimum(jnp.arange(ts)[:, None] + 1, win[None, :])
    return 1.0 / jnp.stack([first, jnp.broadcast_to(win[None, :], (ts, POOL_W))]).astype(F32)


def _pool_fwd(ebuf, s2, s4, s8, inv_count, ts):
    n = ts + HALO
    s2[8:n] = ebuf[8:n] + ebuf[7:n - 1]
    s4[16:n] = s2[16:n] + s2[14:n - 2]
    s8[24:n] = s4[24:n] + s4[20:n - 4]
    s16 = s8[32:n] + s8[24:n - 8]
    col = lax.broadcasted_iota(jnp.int32, (ts, POOL_W), 1)
    psum = jnp.where(col < POOL_GW, s2[32:n],
                     jnp.where(col < 2 * POOL_GW, s4[32:n], jnp.where(col < 3 * POOL_GW, s8[32:n], s16)))
    return psum * inv_count - ebuf[32:n]


GROUP_WIN = ((0, 0), (128, 64), (384, 0), (512, 64))
WIN = 256


def _window_weights(w_grp):
    return jnp.stack([jnp.pad(w_grp[g], ((off, WIN - off - POOL_GW), (off, WIN - off - POOL_GW)))
                      for g, (_, off) in enumerate(GROUP_WIN)])


def _group_matmul(xb, wwin_ref, transposed=False):
    outs = []
    for g, (start, _) in enumerate(GROUP_WIN):
        xw = xb[:, start:start + WIN]
        outs.append(_dot_nt(xw, wwin_ref[g]) if transposed else _dot(xw, wwin_ref[g]))
    half = WIN // 2
    return jnp.concatenate([outs[0][:, :half], outs[0][:, half:] + outs[1][:, :half], outs[1][:, half:],
                            outs[2][:, :half], outs[2][:, half:] + outs[3][:, :half], outs[3][:, half:]], axis=1)


def _mix_core(zr, pooled, outs, lses, wwin_ref, scale, wao, wpo):
    mixed = _group_matmul(pooled.astype(BF16), wwin_ref)
    p = mixed * scale
    l0, l1, l2 = lses
    mx = jnp.maximum(jnp.maximum(l0, l1), l2)
    e0, e1, e2 = jnp.exp(l0 - mx), jnp.exp(l1 - mx), jnp.exp(l2 - mx)
    inv = 1.0 / (e0 + e1 + e2)
    wts = (e0 * inv, e1 * inv, e2 * inv)
    a = wts[0] * outs[0] + wts[1] * outs[1] + wts[2] * outs[2]
    att = _dot(a.astype(BF16), wao)
    pol = _dot(p.astype(BF16), wpo)
    sga = _sigmoid(zr[:, POOL_W:POOL_W + D_MODEL].astype(F32))
    sgp = _sigmoid(zr[:, POOL_W + D_MODEL:].astype(F32))
    mg = sga * att + sgp * pol
    return dict(mixed=mixed, p=p, wts=wts, a=a, att=att, pol=pol, sga=sga, sgp=sgp, mg=mg)


def _fill_pool_input(ebuf, zr_ref, halo_ref, t0):
    ts = zr_ref.shape[0]
    halo = halo_ref[...].astype(F32)
    t = t0 - HALO + lax.broadcasted_iota(jnp.int32, (HALO, POOL_W), 0)
    ebuf[0:HALO] = jnp.where(t >= 0, halo, 0.0)
    ebuf[HALO:HALO + ts] = zr_ref[:, 0:POOL_W].astype(F32)


def _mix_fwd(x, zr, ol_dil, wbd, scale, wao, wpo, wout, ts):
    s = x.shape[0]
    n = s // ts

    def body(x_ref, zr_ref, halo_ref, ic_ref, ol0, ol1, ol2, wbd_ref, sc_ref, wao_ref, wpo_ref, wout_ref,
             h1_ref, ebuf, s2, s4, s8, rbuf):
        i = pl.program_id(0)
        _fill_pool_input(ebuf, zr_ref, halo_ref, i * ts)
        pooled = _pool_fwd(ebuf, s2, s4, s8, ic_ref[...], ts)
        ols = _gather_rows((ol0, ol1, ol2), rbuf, ts)
        outs, lses = [a[:, :GROUP_W] for a in ols], [a[:, GROUP_W:] for a in ols]
        f = _mix_core(zr_ref[...], pooled, outs, lses, wbd_ref, sc_ref[...], wao_ref[...], wpo_ref[...])
        h1_ref[...] = x_ref[...] + _dot(f["mg"].astype(BF16), wout_ref[...])

    whole = lambda a: pl.BlockSpec(a.shape, lambda i: (0,) * a.ndim)
    idx = lambda i: i
    return pl.pallas_call(
        body, name="mix_fwd", grid=(n,),
        in_specs=[pl.BlockSpec((ts, D_MODEL), lambda i: (i, 0)),
                  pl.BlockSpec((ts, REST_W), lambda i: (i, 0)),
                  pl.BlockSpec((HALO, POOL_W), lambda i: (jnp.maximum(i * (ts // HALO) - 1, 0), 0)),
                  pl.BlockSpec((None, ts, POOL_W), lambda i: (jnp.minimum(i, 1), 0, 0))]
                 + _dil_specs(ts, 2 * GROUP_W, idx)
                 + [whole(wbd), whole(scale), whole(wao), whole(wpo), whole(wout)],
        out_specs=pl.BlockSpec((ts, D_MODEL), lambda i: (i, 0)),
        out_shape=jax.ShapeDtypeStruct((s, D_MODEL), F32),
        scratch_shapes=[pltpu.VMEM((ts + HALO, POOL_W), F32)] * 4
                       + [pltpu.VMEM((2 * GROUP_W // LANES, ts, LANES), F32)],
        compiler_params=_params(dimension_semantics=("arbitrary",)),
    )(x, zr, zr, _inverse_counts(ts), *ol_dil, wbd, scale, wao, wpo, wout)


def _mix_bwd(dh1, zr, ol_dil, wbd, scale, wao, wpo, wout, ts):
    s = dh1.shape[0]
    n = s // ts

    def body(dh_ref, zr_ref, halo_ref, ic_ref, ol0, ol1, ol2, sc_ref, wbd_hbm, wao_hbm, wpo_hbm, wout_hbm,
             dzr_ref, dc0, dc1, dc2, gsc_ref, gwout_hbm, gwao_hbm, gwpo_hbm, gwbd_hbm,
             ebuf, s2, s4, s8, gbuf, t2, t4, t8, rbuf,
             wbd_ref, wao_ref, wpo_ref, wout_ref, gwout_ref, gwao_ref, gwpo_ref, gwbd_ref, sem):
        j = pl.program_id(0)
        i = n - 1 - j
        _load_resident(j, [(wbd_hbm, wbd_ref), (wao_hbm, wao_ref), (wpo_hbm, wpo_ref), (wout_hbm, wout_ref)], sem)

        @pl.when(j == 0)
        def _():
            gwout_ref[...] = jnp.zeros_like(gwout_ref)
            gwao_ref[...] = jnp.zeros_like(gwao_ref)
            gwpo_ref[...] = jnp.zeros_like(gwpo_ref)
            gwbd_ref[...] = jnp.zeros_like(gwbd_ref)
            gsc_ref[...] = jnp.zeros_like(gsc_ref)
            gbuf[ts:ts + HALO] = jnp.zeros((HALO, POOL_W), F32)

        _fill_pool_input(ebuf, zr_ref, halo_ref, i * ts)
        inv_count = ic_ref[...]
        pooled = _pool_fwd(ebuf, s2, s4, s8, inv_count, ts)
        ols = _gather_rows((ol0, ol1, ol2), rbuf, ts)
        outs, lses = [a[:, :GROUP_W] for a in ols], [a[:, GROUP_W:] for a in ols]
        zr = zr_ref[...]
        wao, wpo, wout = wao_ref[...], wpo_ref[...], wout_ref[...]
        scale = sc_ref[...]
        f = _mix_core(zr, pooled, outs, lses, wbd_ref, scale, wao, wpo)

        dhb = dh_ref[...].astype(BF16)
        gwout_ref[...] += _dot(f["mg"].T.astype(BF16), dhb)
        dmg = _dot_nt(dhb, wout)
        sga, sgp, att, pol = f["sga"], f["sgp"], f["att"], f["pol"]
        datt = dmg * sga
        dpol = dmg * sgp
        dzr_ref[:, POOL_W:POOL_W + D_MODEL] = (dmg * att * sga * (1.0 - sga)).astype(BF16)
        dzr_ref[:, POOL_W + D_MODEL:] = (dmg * pol * sgp * (1.0 - sgp)).astype(BF16)
        dattb = datt.astype(BF16)
        dpolb = dpol.astype(BF16)
        gwao_ref[...] += _dot(f["a"].T.astype(BF16), dattb)
        gwpo_ref[...] += _dot(f["p"].T.astype(BF16), dpolb)
        da = _dot_nt(dattb, wao)
        dp = _dot_nt(dpolb, wpo)

        gsc_ref[...] += jnp.sum(f["mixed"] * dp, axis=0, keepdims=True)
        dmixed = (dp * scale).astype(BF16)
        pooled_t = pooled.T.astype(BF16)
        for g, (start, _) in enumerate(GROUP_WIN):
            gwbd_ref[g] += _dot(pooled_t[start:start + WIN, :], dmixed[:, start:start + WIN])
        dpooled = _group_matmul(dmixed, wbd_ref, transposed=True)
        gbuf[0:ts] = dpooled * inv_count
        m = ts + HALO
        t2[0:m - 8] = gbuf[0:m - 8] + gbuf[1:m - 7]
        t4[0:m - 16] = t2[0:m - 16] + t2[2:m - 14]
        t8[0:m - 24] = t4[0:m - 24] + t4[4:m - 20]
        t16 = t8[0:ts] + t8[8:ts + 8]
        col = lax.broadcasted_iota(jnp.int32, (ts, POOL_W), 1)
        back = jnp.where(col < POOL_GW, t2[0:ts],
                         jnp.where(col < 2 * POOL_GW, t4[0:ts], jnp.where(col < 3 * POOL_GW, t8[0:ts], t16)))
        dzr_ref[:, 0:POOL_W] = (back - dpooled).astype(BF16)
        gbuf[ts:ts + HALO] = gbuf[0:HALO]

        head_of = lax.broadcasted_iota(jnp.int32, (ts, GROUP_W), 1) // HEAD_DIM
        prod = da * f["a"]
        inner = jnp.zeros((ts, GROUP_W), F32)
        for h in range(4):
            hm = head_of == h
            tot = jnp.sum(jnp.where(hm, prod, 0.0), axis=1, keepdims=True)
            inner = jnp.where(hm, tot, inner)
        for g, dc_ref in enumerate((dc0, dc1, dc2)):
            both = jnp.concatenate([f["wts"][g] * da, -f["wts"][g] * inner], axis=1)
            _to_class_order(both, ATT_GROUPS[g][1], rbuf, dc_ref, 0)

        @pl.when(j == n - 1)
        def _():
            pairs = ((gwout_ref, gwout_hbm), (gwao_ref, gwao_hbm), (gwpo_ref, gwpo_hbm), (gwbd_ref, gwbd_hbm))
            copies = [pltpu.make_async_copy(src, dst, sem.at[k]) for k, (src, dst) in enumerate(pairs)]
            for cp in copies:
                cp.start()
            for cp in copies:
                cp.wait()

    idx = lambda j: n - 1 - j
    dc_shapes = [jax.ShapeDtypeStruct((s, 2 * GROUP_W), F32), jax.ShapeDtypeStruct((4, s // 4, 2 * GROUP_W), F32),
                 jax.ShapeDtypeStruct((16, s // 16, 2 * GROUP_W), F32)]
    weights = (wbd, wao, wpo, wout)
    grad_shapes = [(D_MODEL, D_MODEL), (GROUP_W, D_MODEL), (POOL_W, D_MODEL), (len(GROUP_WIN), WIN, WIN)]
    tile_buf = pltpu.VMEM((ts + HALO, POOL_W), F32)
    outs = pl.pallas_call(
        body, name="mix_bwd", grid=(n,),
        in_specs=[pl.BlockSpec((ts, D_MODEL), lambda j: (idx(j), 0)),
                  pl.BlockSpec((ts, REST_W), lambda j: (idx(j), 0)),
                  pl.BlockSpec((HALO, POOL_W), lambda j: (jnp.maximum(idx(j) * (ts // HALO) - 1, 0), 0)),
                  pl.BlockSpec((None, ts, POOL_W), lambda j: (jnp.minimum(idx(j), 1), 0, 0))]
                 + _dil_specs(ts, 2 * GROUP_W, idx)
                 + [pl.BlockSpec((1, POOL_W), lambda j: (0, 0))] + [ANY] * 4,
        out_specs=[pl.BlockSpec((ts, REST_W), lambda j: (idx(j), 0))]
                  + _dil_specs(ts, 2 * GROUP_W, idx)
                  + [pl.BlockSpec((1, POOL_W), lambda j: (0, 0))] + [ANY] * 4,
        out_shape=[jax.ShapeDtypeStruct((s, REST_W), BF16)] + dc_shapes
                  + [jax.ShapeDtypeStruct((1, POOL_W), F32)]
                  + [jax.ShapeDtypeStruct(shape, F32) for shape in grad_shapes],
        scratch_shapes=[tile_buf] * 8 + [pltpu.VMEM((2 * GROUP_W // LANES, ts, LANES), F32)]
                       + [pltpu.VMEM(w.shape, BF16) for w in weights]
                       + [pltpu.VMEM(shape, F32) for shape in grad_shapes]
                       + [pltpu.SemaphoreType.DMA((4,))],
        compiler_params=_params(dimension_semantics=("arbitrary",)),
    )(dh1, zr, zr, _inverse_counts(ts), *ol_dil, scale, wbd, wao, wpo, wout)
    dzr, dc_dil, g_scale = outs[0], outs[1:4], outs[4]
    g_out, g_ao, g_po, g_bd = outs[5:]
    return dzr, dc_dil, (g_out, g_ao, g_po, g_bd, g_scale)


FF_CHUNK = 1024


def _mlp_fwd_bwd(h1, tgt, g2, g3, wmi, wmo, ts):
    s = h1.shape[0]
    n = s // ts
    nchunk = D_FF // FF_CHUNK

    def body(h1_ref, t_ref, g2_ref, g3_ref, wmi_hbm, wmo_hbm,
             dh1_ref, mt_ref, dh2t_ref, hid_ref, df_ref, loss_ref, dg2_ref, dg3_ref,
             wmi, wmo, relu_buf, sem):
        i = pl.program_id(0)
        _load_resident(i, [(wmi_hbm, wmi), (wmo_hbm, wmo)], sem)

        @pl.when(i == 0)
        def _():
            loss_ref[...] = jnp.zeros_like(loss_ref)
            dg2_ref[...] = jnp.zeros_like(dg2_ref)
            dg3_ref[...] = jnp.zeros_like(dg3_ref)

        h1 = h1_ref[...]
        g2 = g2_ref[...]
        g3 = g3_ref[...]
        r2 = _rms(h1)
        n2 = h1 * r2
        m = n2 * g2
        mb = m.astype(BF16)
        mt_ref[...] = m.T.astype(BF16)
        h2 = h1
        for c in range(nchunk):
            cols = slice(c * FF_CHUNK, (c + 1) * FF_CHUNK)
            rl = jnp.maximum(_dot(mb, wmi[:, cols]), 0.0)
            relu_buf[:, cols] = rl
            hb = (rl * rl).astype(BF16)
            hid_ref[:, cols] = hb
            h2 = h2 + _dot(hb, wmo[cols, :])
        r3 = _rms(h2)
        n3 = h2 * r3
        diff = n3 * g3 - t_ref[...]
        loss_ref[...] += jnp.sum(0.5 * jnp.sum(diff * diff, axis=1, keepdims=True) / D_MODEL,
                                 axis=0, keepdims=True)
        dy = diff * (1.0 / D_MODEL)
        dg3_ref[...] += jnp.sum(dy * n3, axis=0, keepdims=True)
        dh2 = _rms_bwd(dy * g3, n3, r3)
        dh2b = dh2.astype(BF16)
        dh2t_ref[...] = dh2.T.astype(BF16)
        dm = jnp.zeros((ts, D_MODEL), F32)
        for c in range(nchunk):
            cols = slice(c * FF_CHUNK, (c + 1) * FF_CHUNK)
            dfb = (_dot_nt(dh2b, wmo[cols, :]) * (2.0 * relu_buf[:, cols])).astype(BF16)
            df_ref[:, cols] = dfb
            dm = dm + _dot_nt(dfb, wmi[:, cols])
        dg2_ref[...] += jnp.sum(dm * n2, axis=0, keepdims=True)
        dh1_ref[...] = dh2 + _rms_bwd(dm * g2, n2, r2)

    row = lambda w: pl.BlockSpec((ts, w), lambda i: (i, 0))
    colb = pl.BlockSpec((D_MODEL, ts), lambda i: (0, i))
    vec = pl.BlockSpec((1, D_MODEL), lambda i: (0, 0))
    return pl.pallas_call(
        body, name="mlp_fwd_bwd", grid=(n,),
        in_specs=[row(D_MODEL), row(D_MODEL), vec, vec, ANY, ANY],
        out_specs=[row(D_MODEL), colb, colb, row(D_FF), row(D_FF),
                   pl.BlockSpec((1, 1), lambda i: (0, 0)), vec, vec],
        out_shape=[jax.ShapeDtypeStruct((s, D_MODEL), F32),
                   jax.ShapeDtypeStruct((D_MODEL, s), BF16), jax.ShapeDtypeStruct((D_MODEL, s), BF16),
                   jax.ShapeDtypeStruct((s, D_FF), BF16), jax.ShapeDtypeStruct((s, D_FF), BF16),
                   jax.ShapeDtypeStruct((1, 1), F32),
                   jax.ShapeDtypeStruct((1, D_MODEL), F32), jax.ShapeDtypeStruct((1, D_MODEL), F32)],
        scratch_shapes=[pltpu.VMEM((D_MODEL, D_FF), BF16), pltpu.VMEM((D_FF, D_MODEL), BF16),
                        pltpu.VMEM((ts, D_FF), F32), pltpu.SemaphoreType.DMA((2,))],
        compiler_params=_params(dimension_semantics=("arbitrary",)),
    )(h1, tgt, g2, g3, wmi, wmo)


def _in_proj_bwd(x, dh1, dzr, dqkv_dil, g1, w_in, ts):
    s = x.shape[0]
    n = s // ts

    def body(x_ref, dh_ref, dzr_ref, q0, q1, q2, g_ref, w_hbm, dx_ref, dz_ref, dg_ref, w_ref, qbuf, sem):
        i = pl.program_id(0)
        _load_resident(i, [(w_hbm, w_ref)], sem)

        @pl.when(i == 0)
        def _():
            dg_ref[...] = jnp.zeros_like(dg_ref)

        for g, dqkv in enumerate(_gather_rows((q0, q1, q2), qbuf, ts)):
            for sec in range(3):
                c0 = sec * ATT_W + g * GROUP_W
                dz_ref[:, c0:c0 + GROUP_W] = dqkv[:, sec * GROUP_W:(sec + 1) * GROUP_W].astype(BF16)
        dz_ref[:, 3 * ATT_W:] = dzr_ref[...]
        du = _dot_nt(dz_ref[...], w_ref[...])
        x = x_ref[...]
        r1 = _rms(x)
        n1 = x * r1
        g1 = g_ref[...]
        dg_ref[...] += jnp.sum(du * n1, axis=0, keepdims=True)
        dx_ref[...] = dh_ref[...] + _rms_bwd(du * g1, n1, r1)

    row = lambda w: pl.BlockSpec((ts, w), lambda i: (i, 0))
    vec = pl.BlockSpec((1, D_MODEL), lambda i: (0, 0))
    return pl.pallas_call(
        body, name="in_proj_bwd", grid=(n,),
        in_specs=[row(D_MODEL), row(D_MODEL), row(REST_W)] + _dil_specs(ts, ATT_W, lambda i: i) + [vec, ANY],
        out_specs=[row(D_MODEL), row(N_IN), vec],
        out_shape=[jax.ShapeDtypeStruct((s, D_MODEL), F32), jax.ShapeDtypeStruct((s, N_IN), BF16),
                   jax.ShapeDtypeStruct((1, D_MODEL), F32)],
        scratch_shapes=[pltpu.VMEM((D_MODEL, N_IN), BF16), pltpu.VMEM((ATT_W // LANES, ts, LANES), F32),
                        pltpu.SemaphoreType.DMA((1,))],
        compiler_params=_params(dimension_semantics=("arbitrary",)),
    )(x, dh1, dzr, *dqkv_dil, g1, w_in)


GRAD_PASS = 2


def _weight_grad(at, b, name, transpose_out, tk=2048):
    m, s = at.shape
    nn = b.shape[1]
    tn = nn // N_DEV
    tk = min(tk, s)
    nk = s // tk
    npass = N_DEV // GRAD_PASS
    oshape = (tn, m) if transpose_out else (m, tn)
    owner = lambda jj: N_DEV - 1 - jj
    order = jnp.stack([_linear(_peer(_my_place(), owner(jj))) for jj in range(N_DEV)]).astype(jnp.int32)
    sent_in = lambda p: [jj for jj in range(N_DEV - 1) if jj // GRAD_PASS == p]

    def body(order_ref, at_ref, *refs):
        b_refs, (got_ref, acc, res, send_sems, recv_sems, local_sem) = refs[:GRAD_PASS], refs[GRAD_PASS:]
        j, k = pl.program_id(0), pl.program_id(1)
        me = _my_place()
        mine = _linear(me)

        def send(jj):
            return pltpu.make_async_remote_copy(
                src_ref=res.at[(jj // GRAD_PASS) % 2, jj % GRAD_PASS], dst_ref=got_ref.at[mine],
                send_sem=send_sems.at[jj], recv_sem=recv_sems.at[jj],
                device_id=_peer(me, owner(jj)), device_id_type=pl.DeviceIdType.MESH)

        @pl.when(k == 0)
        def _():
            acc[...] = jnp.zeros_like(acc)

        a = at_ref[...]
        for g in range(GRAD_PASS):
            acc[g] += _dot(a, b_refs[g][...])

        @pl.when(k == nk - 1)
        def _():
            for p in range(2, npass):
                @pl.when(j == p)
                def _():
                    for jj in sent_in(p - 2):
                        send(jj).wait_send()

            for g in range(GRAD_PASS):
                r = acc[g]
                res[j % 2, g] = (r.T if transpose_out else r).astype(BF16)
            for p in range(npass):
                @pl.when(j == p)
                def _():
                    for jj in sent_in(p):
                        send(jj).start()

            @pl.when(j == npass - 1)
            def _():
                own = pltpu.make_async_copy(res.at[(npass - 1) % 2, GRAD_PASS - 1], got_ref.at[mine], local_sem.at[0])
                own.start()
                for p in range(max(npass - 2, 0), npass):
                    for jj in sent_in(p):
                        send(jj).wait_send()
                for jj in range(N_DEV - 1):
                    send(jj).wait_recv()
                own.wait()

    b_spec = lambda g: pl.BlockSpec((tk, tn), lambda j, k, o: (k, o[GRAD_PASS * j + g]))
    return pl.pallas_call(
        body, name=name,
        grid_spec=pltpu.PrefetchScalarGridSpec(
            num_scalar_prefetch=1, grid=(npass, nk),
            in_specs=[pl.BlockSpec((m, tk), lambda j, k, o: (0, k))] + [b_spec(g) for g in range(GRAD_PASS)],
            out_specs=ANY,
            scratch_shapes=[pltpu.VMEM((GRAD_PASS, m, tn), F32), pltpu.VMEM((2, GRAD_PASS) + oshape, BF16),
                            pltpu.SemaphoreType.DMA((N_DEV - 1,)), pltpu.SemaphoreType.DMA((N_DEV - 1,)),
                            pltpu.SemaphoreType.DMA((1,))]),
        out_shape=jax.ShapeDtypeStruct((N_DEV,) + oshape, BF16),
        compiler_params=_params(dimension_semantics=("arbitrary", "arbitrary")),
    )(order, at, *([b] * GRAD_PASS))


def _my_place():
    x, y, c = lax.axis_index("x"), lax.axis_index("y"), lax.axis_index("c")
    return x, y, c


def _peer(place, k):
    x, y, c = place
    return (1 - x if k & 4 else x, 1 - y if k & 2 else y, 1 - c if k & 1 else c)


def _linear(place):
    x, y, c = place
    return 4 * x + 2 * y + c


class _Exchange:
    passes_on = False

    def __init__(self, arrays, gather):
        self.arrays, self.gather, self.nw = list(arrays), list(gather), len(arrays)
        self.out_shape = []
        for a, g in zip(arrays, gather):
            block = a.shape if g else a.shape[1:]
            self.out_shape.append(jax.ShapeDtypeStruct((N_DEV,) + tuple(block), a.dtype))
        self.specs = [ANY] * self.nw
        self.scratch = [pltpu.SemaphoreType.DMA((self.nw, N_DEV - 1)), pltpu.SemaphoreType.DMA((self.nw, N_DEV - 1)),
                        pltpu.SemaphoreType.DMA((self.nw,))]

    def _copies(self, ins, outs, sems):
        send_sems, recv_sems, local_sems = sems
        me = _my_place()
        mine = _linear(me)
        copies = []
        for w in range(self.nw):
            src = ins[w] if self.gather[w] else ins[w].at[mine]
            copies.append(pltpu.make_async_copy(src, outs[w].at[mine], local_sems.at[w]))
        for k in range(1, N_DEV):
            peer = _peer(me, k)
            for w in range(self.nw):
                src = ins[w] if self.gather[w] else ins[w].at[_linear(peer)]
                copies.append(pltpu.make_async_remote_copy(
                    src_ref=src, dst_ref=outs[w].at[mine],
                    send_sem=send_sems.at[w, k - 1], recv_sem=recv_sems.at[w, k - 1],
                    device_id=peer, device_id_type=pl.DeviceIdType.MESH))
        return copies

    def start(self, ins, outs, sems):
        for cp in self._copies(ins, outs, sems):
            cp.start()

    def wait(self, ins, outs, sems):
        copies = self._copies(ins, outs, sems)
        for cp in copies[self.nw:]:
            cp.wait_recv()
        for cp in copies[self.nw:]:
            cp.wait_send()
        for cp in copies[:self.nw]:
            cp.wait()


class _Gather:
    passes_on = True

    def __init__(self, arrays):
        self.arrays, self.nw = list(arrays), len(arrays)
        self.out_shape = [jax.ShapeDtypeStruct((N_DEV,) + tuple(a.shape), a.dtype) for a in arrays]
        self.specs = [ANY] * self.nw
        self.scratch = [pltpu.SemaphoreType.DMA((self.nw, N_DEV - 1)), pltpu.SemaphoreType.DMA((self.nw, N_DEV - 1)),
                        pltpu.SemaphoreType.DMA((self.nw,))]

    @staticmethod
    def _places():
        x, y, c = _my_place()
        return (x, y, c), (x, y, 1 - c), [(1 - x, y), (x, 1 - y), (1 - x, 1 - y)]

    @staticmethod
    def _copy(outs, sems, w, k, block, to, src=None):
        rows = outs[w].at[_linear(block)]
        return pltpu.make_async_remote_copy(
            src_ref=rows if src is None else src, dst_ref=rows, send_sem=sems[0].at[w, k], recv_sem=sems[1].at[w, k],
            device_id=to, device_id_type=pl.DeviceIdType.MESH)

    def _first(self, ins, outs, sems, w):
        me, sibling, chips = self._places()
        return ([self._copy(outs, sems, w, 0, me, sibling, src=ins[w])]
                + [self._copy(outs, sems, w, 1 + j, me, (*chip, me[2]), src=ins[w]) for j, chip in enumerate(chips)])

    def _passed(self, outs, sems, w):
        me, sibling, chips = self._places()
        return [self._copy(outs, sems, w, 4 + j, (*chip, me[2]), sibling) for j, chip in enumerate(chips)]

    def _local(self, ins, outs, sems, w):
        return pltpu.make_async_copy(ins[w], outs[w].at[_linear(self._places()[0])], sems[2].at[w])

    def start(self, ins, outs, sems):
        for w in range(self.nw):
            self._local(ins, outs, sems, w).start()
            for cp in self._first(ins, outs, sems, w):
                cp.start()

    def pass_on(self, ins, outs, sems):
        me, sibling, chips = self._places()
        for j, chip in enumerate(chips):
            for w in range(self.nw):
                self._copy(outs, sems, w, 1 + j, (*chip, me[2]), me).wait_recv()
                self._passed(outs, sems, w)[j].start()

    def wait(self, ins, outs, sems):
        me, sibling, chips = self._places()
        for w in range(self.nw):
            self._copy(outs, sems, w, 0, sibling, me).wait_recv()
            for j, chip in enumerate(chips):
                self._copy(outs, sems, w, 4 + j, (*chip, sibling[2]), me).wait_recv()
            for cp in self._first(ins, outs, sems, w) + self._passed(outs, sems, w):
                cp.wait_send()
            self._local(ins, outs, sems, w).wait()


def _exchange(arrays, gather, name):
    ex = _Exchange(arrays, gather)

    def body(*refs):
        ins, outs, sems = refs[:ex.nw], refs[ex.nw:2 * ex.nw], refs[2 * ex.nw:]
        ex.start(ins, outs, sems)
        ex.wait(ins, outs, sems)

    return pl.pallas_call(
        body, name=name, in_specs=ex.specs, out_specs=ex.specs, out_shape=ex.out_shape, scratch_shapes=ex.scratch,
    )(*arrays)


def _adamw(parts, w, m, v, name, tr):
    rows, cols = w.shape
    tr = min(tr, rows)

    def body(p_ref, w_ref, m_ref, v_ref, g_ref, d_ref, nm_ref, nv_ref):
        g = p_ref[0].astype(F32)
        for j in range(1, N_DEV):
            g = g + p_ref[j].astype(F32)
        nm = ADAM_B1 * m_ref[...] + (1.0 - ADAM_B1) * g
        nv = ADAM_B2 * v_ref[...] + (1.0 - ADAM_B2) * (g * g)
        m_hat = nm / (1.0 - ADAM_B1 ** ADAM_STEP)
        v_hat = nv / (1.0 - ADAM_B2 ** ADAM_STEP)
        g_ref[...] = g
        d_ref[...] = -ADAM_LR * (m_hat / (jnp.sqrt(v_hat) + ADAM_EPS) + ADAM_WD * w_ref[...])
        nm_ref[...] = nm
        nv_ref[...] = nv

    blk = pl.BlockSpec((tr, cols), lambda i: (i, 0))
    return pl.pallas_call(
        body, name=name, grid=(rows // tr,),
        in_specs=[pl.BlockSpec((N_DEV, tr, cols), lambda i: (0, i, 0)), blk, blk, blk],
        out_specs=[blk] * 4,
        out_shape=[jax.ShapeDtypeStruct((rows, cols), F32)] * 4,
        compiler_params=_params(dimension_semantics=("arbitrary",)),
    )(parts, w, m, v)


def _pack_small(values):
    rows = []
    for v in values:
        size = math.prod(v.shape)
        padded = -(-size // 1024) * 1024
        flat = jnp.pad(v.reshape(-1).astype(F32), (0, padded - size))
        rows.append(flat.reshape(padded // 128, 128))
    return jnp.concatenate(rows, axis=0) if len(rows) > 1 else rows[0]


def _unpack_small(packed, shapes):
    out, r = [], 0
    for shape in shapes:
        size = math.prod(shape)
        nrow = -(-size // 1024) * 8
        out.append(packed[r:r + nrow].reshape(-1)[:size].reshape(shape))
        r += nrow
    return out


def kernel(x, norm_mix_g, w_in, w_att_out, w_pool_grp, pool_scale, w_pool_out, w_out, norm_mlp_g, w_mlp_in, w_mlp_out, norm_final_g, loss_target, m_norm_mix_g, m_w_in, m_w_att_out, m_w_pool_grp, m_pool_scale, m_w_pool_out, m_w_out, m_norm_mlp_g, m_w_mlp_in, m_w_mlp_out, m_norm_final_g, v_norm_mix_g, v_w_in, v_w_att_out, v_w_pool_grp, v_pool_scale, v_w_pool_out, v_w_out, v_norm_mlp_g, v_w_mlp_in, v_w_mlp_out, v_norm_final_g):
    x, tgt = x[0], loss_target[0]
    s = x.shape[0]
    g1, g2, g3 = norm_mix_g, norm_mlp_g, norm_final_g.reshape(1, D_MODEL)
    shards = [w_in[0], w_att_out[0], w_pool_out[0], w_out[0], w_mlp_in[0], w_mlp_out[0]]
    wire = [a.astype(BF16) for a in shards]
    cols = lambda a: jnp.transpose(a, (1, 0, 2)).reshape(a.shape[1], N_DEV * a.shape[2])
    rows = lambda a: a.reshape(N_DEV * a.shape[1], a.shape[2])
    blocks_of_cols = lambda a: jnp.transpose(a.reshape(a.shape[0], N_DEV, a.shape[1] // N_DEV), (1, 0, 2))
    blocks_of_rows = lambda a: a.reshape(N_DEV, a.shape[0] // N_DEV, a.shape[1])
    wbd = _window_weights(w_pool_grp[0]).astype(BF16)

    (u, ut), (first,) = _rms_u(x, g1, 1024, _Gather(wire[:1]))
    f_in = cols(first)
    (qkv0, qkv1, qkv2, zr), later = _in_proj_fwd(u, f_in, 1024, _Gather(wire[1:]))
    f_ao, f_po, f_out, f_mi, f_mo = cols(later[0]), cols(later[1]), rows(later[2]), cols(later[3]), rows(later[4])
    qkv_dil = (qkv0, qkv1, qkv2)
    flat = lambda a: a.reshape(s, a.shape[-1])
    shaped = lambda a, g: a if g == 0 else a.reshape(ATT_GROUPS[g][1], s // ATT_GROUPS[g][1], a.shape[-1])
    ol_dil = [shaped(_attn_fwd(flat(qkv_dil[g]), g), g) for g in range(N_GROUPS)]
    h1 = _mix_fwd(x, zr, ol_dil, wbd, pool_scale, f_ao, f_po, f_out, ts=512)

    dh1, mt, dh2t, hid, df, loss, dg2, dg3 = _mlp_fwd_bwd(h1, tgt, g2, g3, f_mi, f_mo, ts=256)
    got = {"w_mlp_in": _weight_grad(mt, df, "grad_w_mlp_in", transpose_out=False),
           "w_mlp_out": _weight_grad(dh2t, hid, "grad_w_mlp_out", transpose_out=True)}
    dzr, dc_dil, (g_out, g_ao, g_po, g_bd, g_scale) = _mix_bwd(
        dh1, zr, ol_dil, wbd, pool_scale, f_ao, f_po, f_out, ts=256)
    g_grp = jnp.stack([g_bd[g, off:off + POOL_GW, off:off + POOL_GW] for g, (_, off) in enumerate(GROUP_WIN)])
    early = [_Exchange([blocks_of_rows(g_out).astype(BF16)], [False]),
             _Exchange([blocks_of_cols(g_ao).astype(BF16), blocks_of_cols(g_po).astype(BF16)], [False, False]),
             _Exchange([_pack_small([g_grp])], [True])]
    dqkv_dil, arrived = [], []
    for g in range(N_GROUPS):
        dqkv, rode = _attn_bwd(flat(qkv_dil[g]), flat(dc_dil[g]), g, early[g])
        dqkv_dil.append(shaped(dqkv, g))
        arrived += list(rode)
    got["w_out"], got["w_att_out"], got["w_pool_out"], got_grp = arrived
    dx, dz, dg1 = _in_proj_bwd(x, dh1, dzr, dqkv_dil, g1, f_in, ts=512)
    got["w_in"] = _weight_grad(ut, dz, "grad_w_in", transpose_out=False)
    got_vec = _exchange([_pack_small([loss, dg1, g_scale, dg2, dg3])], [True], "gather_small_grads")[0]

    names = ["w_in", "w_att_out", "w_pool_out", "w_out", "w_mlp_in", "w_mlp_out"]
    ms = [m_w_in, m_w_att_out, m_w_pool_out, m_w_out, m_w_mlp_in, m_w_mlp_out]
    vs = [v_w_in, v_w_att_out, v_w_pool_out, v_w_out, v_w_mlp_in, v_w_mlp_out]
    upd = {}
    for k, name in enumerate(names):
        res = _adamw(got[name], shards[k], ms[k][0], vs[k][0], "adamw_" + name, tr=256)
        upd[name] = [a[None] for a in res]

    res = _adamw(got_grp, _pack_small([w_pool_grp]), _pack_small([m_w_pool_grp]), _pack_small([v_w_pool_grp]),
                 "adamw_w_pool_grp", tr=2048)
    upd["w_pool_grp"] = [_unpack_small(a, [w_pool_grp.shape])[0] for a in res]
    vec_w = [jnp.zeros((1,), F32), norm_mix_g, pool_scale, norm_mlp_g, norm_final_g]
    vec_m = [jnp.zeros((1,), F32), m_norm_mix_g, m_pool_scale, m_norm_mlp_g, m_norm_final_g]
    vec_v = [jnp.ones((1,), F32), v_norm_mix_g, v_pool_scale, v_norm_mlp_g, v_norm_final_g]
    res = _adamw(got_vec, _pack_small(vec_w), _pack_small(vec_m), _pack_small(vec_v), "adamw_vectors", tr=2048)
    shapes = [(), norm_mix_g.shape, pool_scale.shape, norm_mlp_g.shape, norm_final_g.shape]
    unpacked = [_unpack_small(a, shapes) for a in res]
    for k, name in enumerate(["loss", "norm_mix_g", "pool_scale", "norm_mlp_g", "norm_final_g"]):
        upd[name] = [unpacked[q][k] for q in range(4)]

    order = ["norm_mix_g", "w_in", "w_att_out", "w_pool_grp", "pool_scale", "w_pool_out", "w_out", "norm_mlp_g",
             "w_mlp_in", "w_mlp_out", "norm_final_g"]
    out = [upd["loss"][0], dx[None]]
    for q in range(4):
        out += [upd[name][q] for name in order]
    return tuple(out)
```

```python
import math

import jax
import jax.numpy as jnp
from jax import lax
from jax.experimental import pallas as pl
from jax.experimental.pallas import tpu as pltpu

F32 = jnp.float32
BF16 = jnp.bfloat16

D_MODEL = 1024
HEAD_DIM = 64
GROUP_W = 256
ATT_GROUPS = ((128, 1), (512, 4), (2048, 16))
N_GROUPS = 3
BLK = 128
ATT_W = 768
POOL_W = 768
POOL_GW = 192
D_FF = 4096
N_IN = 5120
REST_W = N_IN - 3 * ATT_W
NORM_EPS = 1e-6
ALIBI_MAX_BIAS = 8.0
N_DEV = 8
HALO = 32

ADAM_LR = 0.001
ADAM_B1 = 0.9
ADAM_B2 = 0.999
ADAM_EPS = 1e-08
ADAM_WD = 0.01
ADAM_STEP = 10

VMEM_LIMIT = 56 * 1024 * 1024
ANY = pl.BlockSpec(memory_space=pl.ANY)


def _params(**kw):
    return pltpu.CompilerParams(vmem_limit_bytes=VMEM_LIMIT, **kw)


def _dot(a, b):
    return jnp.dot(a, b, preferred_element_type=F32)


def _dot_nt(a, b):
    return lax.dot_general(a, b, (((1,), (1,)), ((), ())), preferred_element_type=F32)


def _dot_tn(a, b):
    return lax.dot_general(a, b, (((0,), (0,)), ((), ())), preferred_element_type=F32)


def _slope(head):
    return 2.0 ** (-ALIBI_MAX_BIAS * (head + 1.0) / 12.0)


def _load_resident(step, pairs, sem):
    @pl.when(step == 0)
    def _():
        copies = [pltpu.make_async_copy(src, dst, sem.at[n]) for n, (src, dst) in enumerate(pairs)]
        for cp in copies:
            cp.start()
        for cp in copies:
            cp.wait()


LANES = 128


def _to_class_order(value, dil, buf, ref, col0):
    ts, w = value.shape
    if dil == 1:
        ref[:, col0:col0 + w] = value.astype(ref.dtype)
        return
    for c in range(w // LANES):
        buf[c] = value[:, c * LANES:(c + 1) * LANES]
        for r in range(dil):
            ref[r, :, col0 + c * LANES:col0 + (c + 1) * LANES] = (
                buf[c, pl.ds(r, ts // dil, stride=dil), :].astype(ref.dtype))


def _to_token_order(ref, dil, buf, ts):
    if dil == 1:
        return ref[...].astype(F32)
    w = ref.shape[-1]
    for c in range(w // LANES):
        for r in range(dil):
            buf[c, pl.ds(r, ts // dil, stride=dil), :] = ref[r, :, c * LANES:(c + 1) * LANES].astype(F32)
    return jnp.concatenate([buf[c] for c in range(w // LANES)], axis=1)


def _sigmoid(x):
    return 0.5 * jnp.tanh(0.5 * x) + 0.5


def _rms(x):
    return lax.rsqrt(jnp.mean(x * x, axis=-1, keepdims=True) + NORM_EPS)


def _rms_bwd(dn, n, r):
    return r * (dn - n * jnp.mean(dn * n, axis=-1, keepdims=True))


def _split_refs(refs, counts):
    out, at = [], 0
    for c in counts:
        out.append(refs[at:at + c])
        at += c
    return out


def _carry_start(ex, step, ins, outs, sems):
    @pl.when(step == 0)
    def _():
        ex.start(ins, outs, sems)


def _carry_wait(ex, step, last, ins, outs, sems, pass_at=None):
    if ex.passes_on:
        @pl.when(step == (last if pass_at is None else pass_at))
        def _():
            ex.pass_on(ins, outs, sems)

    @pl.when(step == last)
    def _():
        ex.wait(ins, outs, sems)


def _rms_u(x, g1, ts, ex):
    s = x.shape[0]
    n = s // ts

    def body(*refs):
        (x_ref, g_ref), ex_ins, (u_ref, ut_ref), ex_outs, ex_sems = _split_refs(refs, (2, ex.nw, 2, ex.nw, 3))
        i = pl.program_id(0)
        _carry_start(ex, i, ex_ins, ex_outs, ex_sems)
        x = x_ref[...]
        u = x * _rms(x) * g_ref[...]
        u_ref[...] = u.astype(BF16)
        ut_ref[...] = u.T.astype(BF16)
        _carry_wait(ex, i, n - 1, ex_ins, ex_outs, ex_sems)

    outs = pl.pallas_call(
        body, name="rms_u", grid=(n,),
        in_specs=[pl.BlockSpec((ts, D_MODEL), lambda i: (i, 0)), pl.BlockSpec((1, D_MODEL), lambda i: (0, 0))]
                 + ex.specs,
        out_specs=[pl.BlockSpec((ts, D_MODEL), lambda i: (i, 0)), pl.BlockSpec((D_MODEL, ts), lambda i: (0, i))]
                  + ex.specs,
        out_shape=[jax.ShapeDtypeStruct((s, D_MODEL), BF16), jax.ShapeDtypeStruct((D_MODEL, s), BF16)] + ex.out_shape,
        scratch_shapes=ex.scratch,
        compiler_params=_params(dimension_semantics=("arbitrary",)),
    )(x, g1, *ex.arrays)
    return outs[:2], outs[2:]


def _in_proj_fwd(u, w_in, ts, ex):
    s = u.shape[0]
    n = s // ts
    dils = [d for _, d in ATT_GROUPS]

    def body(*refs):
        (u_ref, w_hbm), ex_ins, (q0_ref, q1_ref, q2_ref, zr_ref), ex_outs, (w_ref, zbuf, sem), ex_sems = (
            _split_refs(refs, (2, ex.nw, 4, ex.nw, 3, 3)))
        i = pl.program_id(0)
        _carry_start(ex, i, ex_ins, ex_outs, ex_sems)
        _load_resident(i, [(w_hbm, w_ref)], sem)
        ub = u_ref[...]
        outs = (q0_ref, q1_ref, q2_ref)
        for sec in range(3):
            for g in range(N_GROUPS):
                c0 = sec * ATT_W + g * GROUP_W
                zc = _dot(ub, w_ref[:, c0:c0 + GROUP_W])
                _to_class_order(zc, dils[g], zbuf, outs[g], sec * GROUP_W)
        for c0 in range(0, REST_W, 256):
            zr_ref[:, c0:c0 + 256] = _dot(ub, w_ref[:, 3 * ATT_W + c0:3 * ATT_W + c0 + 256]).astype(BF16)
        _carry_wait(ex, i, n - 1, ex_ins, ex_outs, ex_sems, pass_at=max(n - 3, 0))

    outs = pl.pallas_call(
        body, name="in_proj_fwd", grid=(n,),
        in_specs=[pl.BlockSpec((ts, D_MODEL), lambda i: (i, 0)), ANY] + ex.specs,
        out_specs=[pl.BlockSpec((ts, ATT_W), lambda i: (i, 0)),
                   pl.BlockSpec((4, ts // 4, ATT_W), lambda i: (0, i, 0)),
                   pl.BlockSpec((16, ts // 16, ATT_W), lambda i: (0, i, 0)),
                   pl.BlockSpec((ts, REST_W), lambda i: (i, 0))] + ex.specs,
        out_shape=[jax.ShapeDtypeStruct((s, ATT_W), BF16),
                   jax.ShapeDtypeStruct((4, s // 4, ATT_W), BF16),
                   jax.ShapeDtypeStruct((16, s // 16, ATT_W), BF16),
                   jax.ShapeDtypeStruct((s, REST_W), BF16)] + ex.out_shape,
        scratch_shapes=[pltpu.VMEM((D_MODEL, N_IN), BF16), pltpu.VMEM((GROUP_W // LANES, ts, LANES), F32),
                        pltpu.SemaphoreType.DMA((1,))] + ex.scratch,
        compiler_params=_params(dimension_semantics=("arbitrary",)),
    )(u, w_in, *ex.arrays)
    return outs[:4], outs[4:]


ATT_TILE = 8 * BLK


HEADS = GROUP_W // HEAD_DIM
STACK = HEADS * BLK


SCORE_SCALE = HEAD_DIM ** -0.5


def _band_consts(group, dil):
    row = lax.broadcasted_iota(jnp.int32, (STACK, 2 * BLK), 0)
    kj = lax.broadcasted_iota(jnp.int32, (STACK, 2 * BLK), 1)
    head = row // BLK
    steps = BLK + (row % BLK) - kj
    slope = jnp.full((STACK, 2 * BLK), _slope(4 * group + HEADS - 1), F32)
    for h in range(HEADS - 1):
        slope = jnp.where(head == h, _slope(4 * group + h), slope)
    in_band = (steps >= 0) & (steps <= BLK)
    return jnp.where(in_band, slope * (steps.astype(F32) * float(dil)), jnp.inf), kj


def _first_key(block, nbc, nb, b):
    if nbc % nb == 0 and b != 0:
        return None
    return jnp.where((block % nbc) != 0, 0, BLK)


def _head_of_col():
    return lax.broadcasted_iota(jnp.int32, (BLK, GROUP_W), 1) // HEAD_DIM


def _stack_heads(xb):
    head_of = _head_of_col()
    return jnp.concatenate([jnp.where(head_of == h, xb, jnp.zeros_like(xb)) for h in range(HEADS)], axis=0)


def _per_head_cols(stacked, rhs, scale_rows=None):
    lane = lax.broadcasted_iota(jnp.int32, (BLK, LANES), 1)
    halves = []
    for pair in range(HEADS // 2):
        tile = rhs[:, pair * LANES:(pair + 1) * LANES]
        parts = []
        for h in (2 * pair, 2 * pair + 1):
            part = _dot(stacked[h * BLK:(h + 1) * BLK], tile)
            parts.append(part if scale_rows is None else part * scale_rows[h * BLK:(h + 1) * BLK])
        halves.append(jnp.where(lane < HEAD_DIM, parts[0], parts[1]))
    return jnp.concatenate(halves, axis=1)


def _per_head_rows(col):
    lane = lax.broadcasted_iota(jnp.int32, (BLK, LANES), 1)
    return jnp.concatenate(
        [jnp.where(lane < HEAD_DIM, col[(2 * pair) * BLK:(2 * pair + 1) * BLK], col[(2 * pair + 1) * BLK:(2 * pair + 2) * BLK])
         for pair in range(HEADS // 2)], axis=1)


def _band_softmax(qs, kb, penalty, kj, first_key):
    sc = _dot_nt(qs, kb) - penalty
    if first_key is not None:
        sc = jnp.where(kj >= first_key, sc, -jnp.inf)
    mx = jnp.max(sc, axis=1, keepdims=True)
    e = jnp.exp(sc - mx)
    return e, mx, jnp.sum(e, axis=1, keepdims=True)


def _attn_fwd(qkv, group):
    s = qkv.shape[0]
    dil = ATT_GROUPS[group][1]
    nbc = s // (BLK * dil)
    nb = ATT_TILE // BLK

    def body(q_ref, kc_ref, kp_ref, vc_ref, vp_ref, ol_ref, kbuf, vbuf):
        i = pl.program_id(0)
        kbuf[0:BLK] = kp_ref[...]
        kbuf[BLK:BLK + ATT_TILE] = kc_ref[...]
        vbuf[0:BLK] = vp_ref[...]
        vbuf[BLK:BLK + ATT_TILE] = vc_ref[...]
        penalty, kj = _band_consts(group, dil)
        blocks = range(nb)
        qss = [_stack_heads(q_ref[b * BLK:(b + 1) * BLK, :] * SCORE_SCALE) for b in blocks]
        soft = [_band_softmax(qss[b], kbuf[b * BLK:b * BLK + 2 * BLK, :], penalty, kj,
                              _first_key(i * nb + b, nbc, nb, b)) for b in blocks]
        for b in blocks:
            e, mx, den = soft[b]
            ol_ref[b * BLK:(b + 1) * BLK, 0:GROUP_W] = _per_head_cols(
                e.astype(BF16), vbuf[b * BLK:b * BLK + 2 * BLK, :], 1.0 / den)
        for b in blocks:
            e, mx, den = soft[b]
            ol_ref[b * BLK:(b + 1) * BLK, GROUP_W:] = _per_head_rows(mx + jnp.log(den))

    n = s // ATT_TILE
    cur = lambda c: pl.BlockSpec((ATT_TILE, GROUP_W), lambda i: (i, c))
    prev = lambda c: pl.BlockSpec((BLK, GROUP_W), lambda i: (jnp.maximum(i * nb - 1, 0), c))
    return pl.pallas_call(
        body, name=f"attn_fwd_g{group}", grid=(n,),
        in_specs=[cur(0), cur(1), prev(1), cur(2), prev(2)],
        out_specs=pl.BlockSpec((ATT_TILE, 2 * GROUP_W), lambda i: (i, 0)),
        out_shape=jax.ShapeDtypeStruct((s, 2 * GROUP_W), F32),
        scratch_shapes=[pltpu.VMEM((BLK + ATT_TILE, GROUP_W), BF16), pltpu.VMEM((BLK + ATT_TILE, GROUP_W), BF16)],
        compiler_params=_params(dimension_semantics=("arbitrary",)),
    )(qkv, qkv, qkv, qkv, qkv)


def _attn_bwd(qkv, dc, group, ex=None):
    s = qkv.shape[0]
    dil = ATT_GROUPS[group][1]
    nbc = s // (BLK * dil)
    nb = ATT_TILE // BLK
    n = s // ATT_TILE
    nex = ex.nw if ex else 0

    def body(*refs):
        ((q_ref, kc_ref, kp_ref, vc_ref, vp_ref, do_ref, c_ref), ex_ins, (out_ref,), ex_outs,
         (kbuf, vbuf, dqpend, dkpend, dvpend), ex_sems) = _split_refs(refs, (7, nex, 1, nex, 5, 3 if ex else 0))
        i = pl.program_id(0)
        if ex:
            _carry_start(ex, i, ex_ins, ex_outs, ex_sems)

        @pl.when(i == 0)
        def _():
            dqpend[...] = jnp.zeros_like(dqpend)
            dkpend[...] = jnp.zeros_like(dkpend)
            dvpend[...] = jnp.zeros_like(dvpend)

        out_ref[:, 0:GROUP_W] = dqpend[...]
        body_rows = slice(0, ATT_TILE - BLK)
        tail = slice(ATT_TILE - BLK, ATT_TILE)
        pends = ((dkpend, GROUP_W), (dvpend, 2 * GROUP_W))
        for pend, c0 in pends:
            out_ref[body_rows, c0:c0 + GROUP_W] = pend[body_rows, :].astype(BF16)

        @pl.when(i < n)
        def _():
            kbuf[0:BLK] = kp_ref[...]
            kbuf[BLK:BLK + ATT_TILE] = kc_ref[...]
            vbuf[0:BLK] = vp_ref[...]
            vbuf[BLK:BLK + ATT_TILE] = vc_ref[...]
            penalty, kj = _band_consts(group, dil)
            head_of = _head_of_col()
            blocks = range(nb)
            rows = [slice(b * BLK, (b + 1) * BLK) for b in blocks]
            kbs = [kbuf[b * BLK:b * BLK + 2 * BLK, :] for b in blocks]
            vbs = [vbuf[b * BLK:b * BLK + 2 * BLK, :] for b in blocks]
            qss = [_stack_heads(q_ref[rows[b], :] * SCORE_SCALE) for b in blocks]
            doss = [_stack_heads(do_ref[rows[b], :].astype(BF16)) for b in blocks]
            cors = []
            for b in blocks:
                cb = c_ref[rows[b], :]
                cors.append(jnp.concatenate(
                    [jnp.max(jnp.where(head_of == h, cb, -jnp.inf), axis=1, keepdims=True) for h in range(HEADS)],
                    axis=0))
            soft = [_band_softmax(qss[b], kbs[b], penalty, kj, _first_key(i * nb + b, nbc, nb, b)) for b in blocks]
            dps = [_dot_nt(doss[b], vbs[b]) for b in blocks]
            ps = [soft[b][0] * (1.0 / soft[b][2]) for b in blocks]
            dss = [(ps[b] * (dps[b] + cors[b])).astype(BF16) for b in blocks]
            for b in blocks:
                dqpend[rows[b], :] = _per_head_cols(dss[b], kbs[b] * SCORE_SCALE).astype(BF16)
            bands = [(_dot_tn(dss[b], qss[b]), _dot_tn(ps[b].astype(BF16), doss[b])) for b in blocks]
            for which, (pend, c0) in enumerate(pends):
                out_ref[tail, c0:c0 + GROUP_W] = (pend[tail, :] + bands[0][which][0:BLK]).astype(BF16)
                for b in range(nb):
                    own = bands[b][which][BLK:2 * BLK]
                    pend[b * BLK:(b + 1) * BLK, :] = own + bands[b + 1][which][0:BLK] if b + 1 < nb else own

        @pl.when(i == n)
        def _():
            for pend, c0 in pends:
                out_ref[tail, c0:c0 + GROUP_W] = pend[tail, :].astype(BF16)

        if ex:
            _carry_wait(ex, i, n, ex_ins, ex_outs, ex_sems)

    last = n - 1
    cur = lambda c: pl.BlockSpec((ATT_TILE, GROUP_W), lambda i: (jnp.minimum(i, last), c))
    prev = lambda c: pl.BlockSpec(
        (BLK, GROUP_W), lambda i: (jnp.maximum(jnp.minimum(i, last) * nb - 1, 0), c))
    outs = pl.pallas_call(
        body, name=f"attn_bwd_g{group}", grid=(n + 1,),
        in_specs=[cur(0), cur(1), prev(1), cur(2), prev(2), cur(0), cur(1)] + (ex.specs if ex else []),
        out_specs=[pl.BlockSpec((ATT_TILE, ATT_W), lambda i: (jnp.maximum(i - 1, 0), 0))] + (ex.specs if ex else []),
        out_shape=[jax.ShapeDtypeStruct((s, ATT_W), BF16)] + (ex.out_shape if ex else []),
        scratch_shapes=[pltpu.VMEM((BLK + ATT_TILE, GROUP_W), BF16), pltpu.VMEM((BLK + ATT_TILE, GROUP_W), BF16),
                        pltpu.VMEM((ATT_TILE, GROUP_W), BF16),
                        pltpu.VMEM((ATT_TILE, GROUP_W), F32), pltpu.VMEM((ATT_TILE, GROUP_W), F32)]
                       + (ex.scratch if ex else []),
        compiler_params=_params(dimension_semantics=("arbitrary",)),
    )(qkv, qkv, qkv, qkv, qkv, dc, dc, *(ex.arrays if ex else []))
    return (outs[0], outs[1:]) if ex else outs[0]


def _dil_specs(ts, width, idx):
    return [pl.BlockSpec((ts, width), lambda i: (idx(i), 0)),
            pl.BlockSpec((4, ts // 4, width), lambda i: (0, idx(i), 0)),
            pl.BlockSpec((16, ts // 16, width), lambda i: (0, idx(i), 0))]


def _gather_rows(refs, buf, ts):
    return [_to_token_order(refs[g], ATT_GROUPS[g][1], buf, ts) for g in range(N_GROUPS)]


def _inverse_counts(ts):
    col = jnp.arange(POOL_W)
    win = jnp.where(col < POOL_GW, 2, jnp.where(col < 2 * POOL_GW, 4, jnp.where(col < 3 * POOL_GW, 8, 16)))
    first = jnp.minimum(jnp.arange(ts)[:, None] + 1, win[None, :])
    return 1.0 / jnp.stack([first, jnp.broadcast_to(win[None, :], (ts, POOL_W))]).astype(F32)


def _pool_fwd(ebuf, s2, s4, s8, inv_count, ts):
    n = ts + HALO
    s2[8:n] = ebuf[8:n] + ebuf[7:n - 1]
    s4[16:n] = s2[16:n] + s2[14:n - 2]
    s8[24:n] = s4[24:n] + s4[20:n - 4]
    s16 = s8[32:n] + s8[24:n - 8]
    col = lax.broadcasted_iota(jnp.int32, (ts, POOL_W), 1)
    psum = jnp.where(col < POOL_GW, s2[32:n],
                     jnp.where(col < 2 * POOL_GW, s4[32:n], jnp.where(col < 3 * POOL_GW, s8[32:n], s16)))
    return psum * inv_count - ebuf[32:n]


GROUP_WIN = ((0, 0), (128, 64), (384, 0), (512, 64))
WIN = 256


def _window_weights(w_grp):
    return jnp.stack([jnp.pad(w_grp[g], ((off, WIN - off - POOL_GW), (off, WIN - off - POOL_GW)))
                      for g, (_, off) in enumerate(GROUP_WIN)])


def _group_matmul(xb, wwin_ref, transposed=False):
    outs = []
    for g, (start, _) in enumerate(GROUP_WIN):
        xw = xb[:, start:start + WIN]
        outs.append(_dot_nt(xw, wwin_ref[g]) if transposed else _dot(xw, wwin_ref[g]))
    half = WIN // 2
    return jnp.concatenate([outs[0][:, :half], outs[0][:, half:] + outs[1][:, :half], outs[1][:, half:],
                            outs[2][:, :half], outs[2][:, half:] + outs[3][:, :half], outs[3][:, half:]], axis=1)


def _mix_core(zr, pooled, outs, lses, wwin_ref, scale, wao, wpo):
    mixed = _group_matmul(pooled.astype(BF16), wwin_ref)
    p = mixed * scale
    l0, l1, l2 = lses
    mx = jnp.maximum(jnp.maximum(l0, l1), l2)
    e0, e1, e2 = jnp.exp(l0 - mx), jnp.exp(l1 - mx), jnp.exp(l2 - mx)
    inv = 1.0 / (e0 + e1 + e2)
    wts = (e0 * inv, e1 * inv, e2 * inv)
    a = wts[0] * outs[0] + wts[1] * outs[1] + wts[2] * outs[2]
    att = _dot(a.astype(BF16), wao)
    pol = _dot(p.astype(BF16), wpo)
    sga = _sigmoid(zr[:, POOL_W:POOL_W + D_MODEL].astype(F32))
    sgp = _sigmoid(zr[:, POOL_W + D_MODEL:].astype(F32))
    mg = sga * att + sgp * pol
    return dict(mixed=mixed, p=p, wts=wts, a=a, att=att, pol=pol, sga=sga, sgp=sgp, mg=mg)


def _fill_pool_input(ebuf, zr_ref, halo_ref, t0):
    ts = zr_ref.shape[0]
    halo = halo_ref[...].astype(F32)
    t = t0 - HALO + lax.broadcasted_iota(jnp.int32, (HALO, POOL_W), 0)
    ebuf[0:HALO] = jnp.where(t >= 0, halo, 0.0)
    ebuf[HALO:HALO + ts] = zr_ref[:, 0:POOL_W].astype(F32)


def _mix_fwd(x, zr, ol_dil, wbd, scale, wao, wpo, wout, ts):
    s = x.shape[0]
    n = s // ts

    def body(x_ref, zr_ref, halo_ref, ic_ref, ol0, ol1, ol2, wbd_ref, sc_ref, wao_ref, wpo_ref, wout_ref,
             h1_ref, ebuf, s2, s4, s8, rbuf):
        i = pl.program_id(0)
        _fill_pool_input(ebuf, zr_ref, halo_ref, i * ts)
        pooled = _pool_fwd(ebuf, s2, s4, s8, ic_ref[...], ts)
        ols = _gather_rows((ol0, ol1, ol2), rbuf, ts)
        outs, lses = [a[:, :GROUP_W] for a in ols], [a[:, GROUP_W:] for a in ols]
        f = _mix_core(zr_ref[...], pooled, outs, lses, wbd_ref, sc_ref[...], wao_ref[...], wpo_ref[...])
        h1_ref[...] = x_ref[...] + _dot(f["mg"].astype(BF16), wout_ref[...])

    whole = lambda a: pl.BlockSpec(a.shape, lambda i: (0,) * a.ndim)
    idx = lambda i: i
    return pl.pallas_call(
        body, name="mix_fwd", grid=(n,),
        in_specs=[pl.BlockSpec((ts, D_MODEL), lambda i: (i, 0)),
                  pl.BlockSpec((ts, REST_W), lambda i: (i, 0)),
                  pl.BlockSpec((HALO, POOL_W), lambda i: (jnp.maximum(i * (ts // HALO) - 1, 0), 0)),
                  pl.BlockSpec((None, ts, POOL_W), lambda i: (jnp.minimum(i, 1), 0, 0))]
                 + _dil_specs(ts, 2 * GROUP_W, idx)
                 + [whole(wbd), whole(scale), whole(wao), whole(wpo), whole(wout)],
        out_specs=pl.BlockSpec((ts, D_MODEL), lambda i: (i, 0)),
        out_shape=jax.ShapeDtypeStruct((s, D_MODEL), F32),
        scratch_shapes=[pltpu.VMEM((ts + HALO, POOL_W), F32)] * 4
                       + [pltpu.VMEM((2 * GROUP_W // LANES, ts, LANES), F32)],
        compiler_params=_params(dimension_semantics=("arbitrary",)),
    )(x, zr, zr, _inverse_counts(ts), *ol_dil, wbd, scale, wao, wpo, wout)


def _mix_bwd(dh1, zr, ol_dil, wbd, scale, wao, wpo, wout, ts):
    s = dh1.shape[0]
    n = s // ts

    def body(dh_ref, zr_ref, halo_ref, ic_ref, ol0, ol1, ol2, sc_ref, wbd_hbm, wao_hbm, wpo_hbm, wout_hbm,
             dzr_ref, dc0, dc1, dc2, gsc_ref, gwout_hbm, gwao_hbm, gwpo_hbm, gwbd_hbm,
             ebuf, s2, s4, s8, gbuf, t2, t4, t8, rbuf,
             wbd_ref, wao_ref, wpo_ref, wout_ref, gwout_ref, gwao_ref, gwpo_ref, gwbd_ref, sem):
        j = pl.program_id(0)
        i = n - 1 - j
        _load_resident(j, [(wbd_hbm, wbd_ref), (wao_hbm, wao_ref), (wpo_hbm, wpo_ref), (wout_hbm, wout_ref)], sem)

        @pl.when(j == 0)
        def _():
            gwout_ref[...] = jnp.zeros_like(gwout_ref)
            gwao_ref[...] = jnp.zeros_like(gwao_ref)
            gwpo_ref[...] = jnp.zeros_like(gwpo_ref)
            gwbd_ref[...] = jnp.zeros_like(gwbd_ref)
            gsc_ref[...] = jnp.zeros_like(gsc_ref)
            gbuf[ts:ts + HALO] = jnp.zeros((HALO, POOL_W), F32)

        _fill_pool_input(ebuf, zr_ref, halo_ref, i * ts)
        inv_count = ic_ref[...]
        pooled = _pool_fwd(ebuf, s2, s4, s8, inv_count, ts)
        ols = _gather_rows((ol0, ol1, ol2), rbuf, ts)
        outs, lses = [a[:, :GROUP_W] for a in ols], [a[:, GROUP_W:] for a in ols]
        zr = zr_ref[...]
        wao, wpo, wout = wao_ref[...], wpo_ref[...], wout_ref[...]
        scale = sc_ref[...]
        f = _mix_core(zr, pooled, outs, lses, wbd_ref, scale, wao, wpo)

        dhb = dh_ref[...].astype(BF16)
        gwout_ref[...] += _dot(f["mg"].T.astype(BF16), dhb)
        dmg = _dot_nt(dhb, wout)
        sga, sgp, att, pol = f["sga"], f["sgp"], f["att"], f["pol"]
        datt = dmg * sga
        dpol = dmg * sgp
        dzr_ref[:, POOL_W:POOL_W + D_MODEL] = (dmg * att * sga * (1.0 - sga)).astype(BF16)
        dzr_ref[:, POOL_W + D_MODEL:] = (dmg * pol * sgp * (1.0 - sgp)).astype(BF16)
        dattb = datt.astype(BF16)
        dpolb = dpol.astype(BF16)
        gwao_ref[...] += _dot(f["a"].T.astype(BF16), dattb)
        gwpo_ref[...] += _dot(f["p"].T.astype(BF16), dpolb)
        da = _dot_nt(dattb, wao)
        dp = _dot_nt(dpolb, wpo)

        gsc_ref[...] += jnp.sum(f["mixed"] * dp, axis=0, keepdims=True)
        dmixed = (dp * scale).astype(BF16)
        pooled_t = pooled.T.astype(BF16)
        for g, (start, _) in enumerate(GROUP_WIN):
            gwbd_ref[g] += _dot(pooled_t[start:start + WIN, :], dmixed[:, start:start + WIN])
        dpooled = _group_matmul(dmixed, wbd_ref, transposed=True)
        gbuf[0:ts] = dpooled * inv_count
        m = ts + HALO
        t2[0:m - 8] = gbuf[0:m - 8] + gbuf[1:m - 7]
        t4[0:m - 16] = t2[0:m - 16] + t2[2:m - 14]
        t8[0:m - 24] = t4[0:m - 24] + t4[4:m - 20]
        t16 = t8[0:ts] + t8[8:ts + 8]
        col = lax.broadcasted_iota(jnp.int32, (ts, POOL_W), 1)
        back = jnp.where(col < POOL_GW, t2[0:ts],
                         jnp.where(col < 2 * POOL_GW, t4[0:ts], jnp.where(col < 3 * POOL_GW, t8[0:ts], t16)))
        dzr_ref[:, 0:POOL_W] = (back - dpooled).astype(BF16)
        gbuf[ts:ts + HALO] = gbuf[0:HALO]

        head_of = lax.broadcasted_iota(jnp.int32, (ts, GROUP_W), 1) // HEAD_DIM
        prod = da * f["a"]
        inner = jnp.zeros((ts, GROUP_W), F32)
        for h in range(4):
            hm = head_of == h
            tot = jnp.sum(jnp.where(hm, prod, 0.0), axis=1, keepdims=True)
            inner = jnp.where(hm, tot, inner)
        for g, dc_ref in enumerate((dc0, dc1, dc2)):
            both = jnp.concatenate([f["wts"][g] * da, -f["wts"][g] * inner], axis=1)
            _to_class_order(both, ATT_GROUPS[g][1], rbuf, dc_ref, 0)

        @pl.when(j == n - 1)
        def _():
            pairs = ((gwout_ref, gwout_hbm), (gwao_ref, gwao_hbm), (gwpo_ref, gwpo_hbm), (gwbd_ref, gwbd_hbm))
            copies = [pltpu.make_async_copy(src, dst, sem.at[k]) for k, (src, dst) in enumerate(pairs)]
            for cp in copies:
                cp.start()
            for cp in copies:
                cp.wait()

    idx = lambda j: n - 1 - j
    dc_shapes = [jax.ShapeDtypeStruct((s, 2 * GROUP_W), F32), jax.ShapeDtypeStruct((4, s // 4, 2 * GROUP_W), F32),
                 jax.ShapeDtypeStruct((16, s // 16, 2 * GROUP_W), F32)]
    weights = (wbd, wao, wpo, wout)
    grad_shapes = [(D_MODEL, D_MODEL), (GROUP_W, D_MODEL), (POOL_W, D_MODEL), (len(GROUP_WIN), WIN, WIN)]
    tile_buf = pltpu.VMEM((ts + HALO, POOL_W), F32)
    outs = pl.pallas_call(
        body, name="mix_bwd", grid=(n,),
        in_specs=[pl.BlockSpec((ts, D_MODEL), lambda j: (idx(j), 0)),
                  pl.BlockSpec((ts, REST_W), lambda j: (idx(j), 0)),
                  pl.BlockSpec((HALO, POOL_W), lambda j: (jnp.maximum(idx(j) * (ts // HALO) - 1, 0), 0)),
                  pl.BlockSpec((None, ts, POOL_W), lambda j: (jnp.minimum(idx(j), 1), 0, 0))]
                 + _dil_specs(ts, 2 * GROUP_W, idx)
                 + [pl.BlockSpec((1, POOL_W), lambda j: (0, 0))] + [ANY] * 4,
        out_specs=[pl.BlockSpec((ts, REST_W), lambda j: (idx(j), 0))]
                  + _dil_specs(ts, 2 * GROUP_W, idx)
                  + [pl.BlockSpec((1, POOL_W), lambda j: (0, 0))] + [ANY] * 4,
        out_shape=[jax.ShapeDtypeStruct((s, REST_W), BF16)] + dc_shapes
                  + [jax.ShapeDtypeStruct((1, POOL_W), F32)]
                  + [jax.ShapeDtypeStruct(shape, F32) for shape in grad_shapes],
        scratch_shapes=[tile_buf] * 8 + [pltpu.VMEM((2 * GROUP_W // LANES, ts, LANES), F32)]
                       + [pltpu.VMEM(w.shape, BF16) for w in weights]
                       + [pltpu.VMEM(shape, F32) for shape in grad_shapes]
                       + [pltpu.SemaphoreType.DMA((4,))],
        compiler_params=_params(dimension_semantics=("arbitrary",)),
    )(dh1, zr, zr, _inverse_counts(ts), *ol_dil, scale, wbd, wao, wpo, wout)
    dzr, dc_dil, g_scale = outs[0], outs[1:4], outs[4]
    g_out, g_ao, g_po, g_bd = outs[5:]
    return dzr, dc_dil, (g_out, g_ao, g_po, g_bd, g_scale)


FF_CHUNK = 1024


STAT_ROWS = 8


def _mlp_fwd_bwd(h1, tgt, gains, wmi, wmo, ts):
    s = h1.shape[0]
    n = s // ts
    nchunk = D_FF // FF_CHUNK

    def body(h1_ref, t_ref, g_ref, wmi_hbm, wmo_hbm, dh1_ref, mdt_ref, hd_ref, stat_ref, wmi, wmo, relu_buf, sem):
        i = pl.program_id(0)
        _load_resident(i, [(wmi_hbm, wmi), (wmo_hbm, wmo)], sem)

        @pl.when(i == 0)
        def _():
            stat_ref[...] = jnp.zeros_like(stat_ref)

        h1 = h1_ref[...]
        g2 = g_ref[0:1, :]
        g3 = g_ref[1:2, :]
        r2 = _rms(h1)
        n2 = h1 * r2
        m = n2 * g2
        mb = m.astype(BF16)
        mdt_ref[0] = m.T.astype(BF16)
        h2 = h1
        for c in range(nchunk):
            cols = slice(c * FF_CHUNK, (c + 1) * FF_CHUNK)
            rl = jnp.maximum(_dot(mb, wmi[:, cols]), 0.0)
            relu_buf[:, cols] = rl
            hb = (rl * rl).astype(BF16)
            hd_ref[0, :, cols] = hb
            h2 = h2 + _dot(hb, wmo[cols, :])
        r3 = _rms(h2)
        n3 = h2 * r3
        diff = n3 * g3 - t_ref[...]
        loss = jnp.sum(0.5 * jnp.sum(diff * diff, axis=1, keepdims=True) / D_MODEL, axis=0, keepdims=True)
        dy = diff * (1.0 / D_MODEL)
        dg3 = jnp.sum(dy * n3, axis=0, keepdims=True)
        dh2 = _rms_bwd(dy * g3, n3, r3)
        dh2b = dh2.astype(BF16)
        mdt_ref[1] = dh2.T.astype(BF16)
        dm = jnp.zeros((ts, D_MODEL), F32)
        for c in range(nchunk):
            cols = slice(c * FF_CHUNK, (c + 1) * FF_CHUNK)
            dfb = (_dot_nt(dh2b, wmo[cols, :]) * (2.0 * relu_buf[:, cols])).astype(BF16)
            hd_ref[1, :, cols] = dfb
            dm = dm + _dot_nt(dfb, wmi[:, cols])
        dg2 = jnp.sum(dm * n2, axis=0, keepdims=True)
        dh1_ref[...] = dh2 + _rms_bwd(dm * g2, n2, r2)
        row = lax.broadcasted_iota(jnp.int32, (STAT_ROWS, D_MODEL), 0)
        stat_ref[...] += jnp.where(row == 0, loss, jnp.where(row == 1, dg2, jnp.where(row == 2, dg3, 0.0)))

    row = lambda w: pl.BlockSpec((ts, w), lambda i: (i, 0))
    return pl.pallas_call(
        body, name="mlp_fwd_bwd", grid=(n,),
        in_specs=[row(D_MODEL), row(D_MODEL), pl.BlockSpec((2, D_MODEL), lambda i: (0, 0)), ANY, ANY],
        out_specs=[row(D_MODEL), pl.BlockSpec((2, D_MODEL, ts), lambda i: (0, 0, i)),
                   pl.BlockSpec((2, ts, D_FF), lambda i: (0, i, 0)),
                   pl.BlockSpec((STAT_ROWS, D_MODEL), lambda i: (0, 0))],
        out_shape=[jax.ShapeDtypeStruct((s, D_MODEL), F32), jax.ShapeDtypeStruct((2, D_MODEL, s), BF16),
                   jax.ShapeDtypeStruct((2, s, D_FF), BF16), jax.ShapeDtypeStruct((STAT_ROWS, D_MODEL), F32)],
        scratch_shapes=[pltpu.VMEM((D_MODEL, D_FF), BF16), pltpu.VMEM((D_FF, D_MODEL), BF16),
                        pltpu.VMEM((ts, D_FF), F32), pltpu.SemaphoreType.DMA((2,))],
        compiler_params=_params(dimension_semantics=("arbitrary",)),
    )(h1, tgt, gains, wmi, wmo)


def _in_proj_bwd(x, dh1, dzr, dqkv_dil, g1, w_in, ts):
    s = x.shape[0]
    n = s // ts

    def body(x_ref, dh_ref, dzr_ref, q0, q1, q2, g_ref, w_hbm, dx_ref, dz_ref, dg_ref, w_ref, qbuf, sem):
        i = pl.program_id(0)
        _load_resident(i, [(w_hbm, w_ref)], sem)

        @pl.when(i == 0)
        def _():
            dg_ref[...] = jnp.zeros_like(dg_ref)

        for g, dqkv in enumerate(_gather_rows((q0, q1, q2), qbuf, ts)):
            for sec in range(3):
                c0 = sec * ATT_W + g * GROUP_W
                dz_ref[:, c0:c0 + GROUP_W] = dqkv[:, sec * GROUP_W:(sec + 1) * GROUP_W].astype(BF16)
        dz_ref[:, 3 * ATT_W:] = dzr_ref[...]
        du = _dot_nt(dz_ref[...], w_ref[...])
        x = x_ref[...]
        r1 = _rms(x)
        n1 = x * r1
        g1 = g_ref[...]
        dg_ref[...] += jnp.sum(du * n1, axis=0, keepdims=True)
        dx_ref[...] = dh_ref[...] + _rms_bwd(du * g1, n1, r1)

    row = lambda w: pl.BlockSpec((ts, w), lambda i: (i, 0))
    vec = pl.BlockSpec((1, D_MODEL), lambda i: (0, 0))
    return pl.pallas_call(
        body, name="in_proj_bwd", grid=(n,),
        in_specs=[row(D_MODEL), row(D_MODEL), row(REST_W)] + _dil_specs(ts, ATT_W, lambda i: i) + [vec, ANY],
        out_specs=[row(D_MODEL), row(N_IN), vec],
        out_shape=[jax.ShapeDtypeStruct((s, D_MODEL), F32), jax.ShapeDtypeStruct((s, N_IN), BF16),
                   jax.ShapeDtypeStruct((1, D_MODEL), F32)],
        scratch_shapes=[pltpu.VMEM((D_MODEL, N_IN), BF16), pltpu.VMEM((ATT_W // LANES, ts, LANES), F32),
                        pltpu.SemaphoreType.DMA((1,))],
        compiler_params=_params(dimension_semantics=("arbitrary",)),
    )(x, dh1, dzr, *dqkv_dil, g1, w_in)


GRAD_PASS = 2


def _weight_grad(at, b, name, transpose_out, tk=2048, at_slot=None, b_slot=None):
    m, s = at.shape[-2:]
    nn = b.shape[-1]
    tn = nn // N_DEV
    tk = min(tk, s)
    nk = s // tk
    npass = N_DEV // GRAD_PASS
    oshape = (tn, m) if transpose_out else (m, tn)
    owner = lambda jj: N_DEV - 1 - jj
    order = jnp.stack([_linear(_peer(_my_place(), owner(jj))) for jj in range(N_DEV)]).astype(jnp.int32)
    sent_in = lambda p: [jj for jj in range(N_DEV - 1) if jj // GRAD_PASS == p]

    def body(order_ref, at_ref, *refs):
        b_refs, (got_ref, acc, res, send_sems, recv_sems, local_sem) = refs[:GRAD_PASS], refs[GRAD_PASS:]
        j, k = pl.program_id(0), pl.program_id(1)
        me = _my_place()
        mine = _linear(me)

        def send(jj):
            return pltpu.make_async_remote_copy(
                src_ref=res.at[(jj // GRAD_PASS) % 2, jj % GRAD_PASS], dst_ref=got_ref.at[mine],
                send_sem=send_sems.at[jj], recv_sem=recv_sems.at[jj],
                device_id=_peer(me, owner(jj)), device_id_type=pl.DeviceIdType.MESH)

        @pl.when(k == 0)
        def _():
            acc[...] = jnp.zeros_like(acc)

        a = at_ref[...]
        for g in range(GRAD_PASS):
            acc[g] += _dot(a, b_refs[g][...])

        @pl.when(k == nk - 1)
        def _():
            for p in range(2, npass):
                @pl.when(j == p)
                def _():
                    for jj in sent_in(p - 2):
                        send(jj).wait_send()

            for g in range(GRAD_PASS):
                r = acc[g]
                res[j % 2, g] = (r.T if transpose_out else r).astype(BF16)
            for p in range(npass):
                @pl.when(j == p)
                def _():
                    for jj in sent_in(p):
                        send(jj).start()

            @pl.when(j == npass - 1)
            def _():
                own = pltpu.make_async_copy(res.at[(npass - 1) % 2, GRAD_PASS - 1], got_ref.at[mine], local_sem.at[0])
                own.start()
                for p in range(max(npass - 2, 0), npass):
                    for jj in sent_in(p):
                        send(jj).wait_send()
                for jj in range(N_DEV - 1):
                    send(jj).wait_recv()
                own.wait()

    if at_slot is None:
        at_spec = pl.BlockSpec((m, tk), lambda j, k, o: (0, k))
    else:
        at_spec = pl.BlockSpec((None, m, tk), lambda j, k, o: (at_slot, 0, k))
    if b_slot is None:
        b_spec = lambda g: pl.BlockSpec((tk, tn), lambda j, k, o: (k, o[GRAD_PASS * j + g]))
    else:
        b_spec = lambda g: pl.BlockSpec((None, tk, tn), lambda j, k, o: (b_slot, k, o[GRAD_PASS * j + g]))
    return pl.pallas_call(
        body, name=name,
        grid_spec=pltpu.PrefetchScalarGridSpec(
            num_scalar_prefetch=1, grid=(npass, nk),
            in_specs=[at_spec] + [b_spec(g) for g in range(GRAD_PASS)],
            out_specs=ANY,
            scratch_shapes=[pltpu.VMEM((GRAD_PASS, m, tn), F32), pltpu.VMEM((2, GRAD_PASS) + oshape, BF16),
                            pltpu.SemaphoreType.DMA((N_DEV - 1,)), pltpu.SemaphoreType.DMA((N_DEV - 1,)),
                            pltpu.SemaphoreType.DMA((1,))]),
        out_shape=jax.ShapeDtypeStruct((N_DEV,) + oshape, BF16),
        compiler_params=_params(dimension_semantics=("arbitrary", "arbitrary")),
    )(order, at, *([b] * GRAD_PASS))


def _my_place():
    x, y, c = lax.axis_index("x"), lax.axis_index("y"), lax.axis_index("c")
    return x, y, c


def _peer(place, k):
    x, y, c = place
    return (1 - x if k & 4 else x, 1 - y if k & 2 else y, 1 - c if k & 1 else c)


def _linear(place):
    x, y, c = place
    return 4 * x + 2 * y + c


class _Exchange:
    passes_on = False

    def __init__(self, arrays, gather):
        self.arrays, self.gather, self.nw = list(arrays), list(gather), len(arrays)
        self.out_shape = []
        for a, g in zip(arrays, gather):
            block = a.shape if g else a.shape[1:]
            self.out_shape.append(jax.ShapeDtypeStruct((N_DEV,) + tuple(block), a.dtype))
        self.specs = [ANY] * self.nw
        self.scratch = [pltpu.SemaphoreType.DMA((self.nw, N_DEV - 1)), pltpu.SemaphoreType.DMA((self.nw, N_DEV - 1)),
                        pltpu.SemaphoreType.DMA((self.nw,))]

    def _copies(self, ins, outs, sems):
        send_sems, recv_sems, local_sems = sems
        me = _my_place()
        mine = _linear(me)
        copies = []
        for w in range(self.nw):
            src = ins[w] if self.gather[w] else ins[w].at[mine]
            copies.append(pltpu.make_async_copy(src, outs[w].at[mine], local_sems.at[w]))
        for k in range(1, N_DEV):
            peer = _peer(me, k)
            for w in range(self.nw):
                src = ins[w] if self.gather[w] else ins[w].at[_linear(peer)]
                copies.append(pltpu.make_async_remote_copy(
                    src_ref=src, dst_ref=outs[w].at[mine],
                    send_sem=send_sems.at[w, k - 1], recv_sem=recv_sems.at[w, k - 1],
                    device_id=peer, device_id_type=pl.DeviceIdType.MESH))
        return copies

    def start(self, ins, outs, sems):
        for cp in self._copies(ins, outs, sems):
            cp.start()

    def wait(self, ins, outs, sems):
        copies = self._copies(ins, outs, sems)
        for cp in copies[self.nw:]:
            cp.wait_recv()
        for cp in copies[self.nw:]:
            cp.wait_send()
        for cp in copies[:self.nw]:
            cp.wait()


class _Gather:
    passes_on = True

    def __init__(self, arrays):
        self.arrays, self.nw = list(arrays), len(arrays)
        self.out_shape = [jax.ShapeDtypeStruct((N_DEV,) + tuple(a.shape), a.dtype) for a in arrays]
        self.specs = [ANY] * self.nw
        self.scratch = [pltpu.SemaphoreType.DMA((self.nw, N_DEV - 1)), pltpu.SemaphoreType.DMA((self.nw, N_DEV - 1)),
                        pltpu.SemaphoreType.DMA((self.nw,))]

    @staticmethod
    def _places():
        x, y, c = _my_place()
        return (x, y, c), (x, y, 1 - c), [(1 - x, y), (x, 1 - y), (1 - x, 1 - y)]

    @staticmethod
    def _copy(outs, sems, w, k, block, to, src=None):
        rows = outs[w].at[_linear(block)]
        return pltpu.make_async_remote_copy(
            src_ref=rows if src is None else src, dst_ref=rows, send_sem=sems[0].at[w, k], recv_sem=sems[1].at[w, k],
            device_id=to, device_id_type=pl.DeviceIdType.MESH)

    def _first(self, ins, outs, sems, w):
        me, sibling, chips = self._places()
        return ([self._copy(outs, sems, w, 0, me, sibling, src=ins[w])]
                + [self._copy(outs, sems, w, 1 + j, me, (*chip, me[2]), src=ins[w]) for j, chip in enumerate(chips)])

    def _passed(self, outs, sems, w):
        me, sibling, chips = self._places()
        return [self._copy(outs, sems, w, 4 + j, (*chip, me[2]), sibling) for j, chip in enumerate(chips)]

    def _local(self, ins, outs, sems, w):
        return pltpu.make_async_copy(ins[w], outs[w].at[_linear(self._places()[0])], sems[2].at[w])

    def start(self, ins, outs, sems):
        for w in range(self.nw):
            self._local(ins, outs, sems, w).start()
            for cp in self._first(ins, outs, sems, w):
                cp.start()

    def pass_on(self, ins, outs, sems):
        me, sibling, chips = self._places()
        for j, chip in enumerate(chips):
            for w in range(self.nw):
                self._copy(outs, sems, w, 1 + j, (*chip, me[2]), me).wait_recv()
                self._passed(outs, sems, w)[j].start()

    def wait(self, ins, outs, sems):
        me, sibling, chips = self._places()
        for w in range(self.nw):
            self._copy(outs, sems, w, 0, sibling, me).wait_recv()
            for j, chip in enumerate(chips):
                self._copy(outs, sems, w, 4 + j, (*chip, sibling[2]), me).wait_recv()
            for cp in self._first(ins, outs, sems, w) + self._passed(outs, sems, w):
                cp.wait_send()
            self._local(ins, outs, sems, w).wait()


def _exchange(arrays, gather, name):
    ex = _Exchange(arrays, gather)

    def body(*refs):
        ins, outs, sems = refs[:ex.nw], refs[ex.nw:2 * ex.nw], refs[2 * ex.nw:]
        ex.start(ins, outs, sems)
        ex.wait(ins, outs, sems)

    return pl.pallas_call(
        body, name=name, in_specs=ex.specs, out_specs=ex.specs, out_shape=ex.out_shape, scratch_shapes=ex.scratch,
    )(*arrays)


def _adamw(parts, w, m, v, name, tr):
    rows, cols = w.shape
    tr = min(tr, rows)

    def body(p_ref, w_ref, m_ref, v_ref, g_ref, d_ref, nm_ref, nv_ref):
        g = p_ref[0].astype(F32)
        for j in range(1, N_DEV):
            g = g + p_ref[j].astype(F32)
        nm = ADAM_B1 * m_ref[...] + (1.0 - ADAM_B1) * g
        nv = ADAM_B2 * v_ref[...] + (1.0 - ADAM_B2) * (g * g)
        m_hat = nm / (1.0 - ADAM_B1 ** ADAM_STEP)
        v_hat = nv / (1.0 - ADAM_B2 ** ADAM_STEP)
        g_ref[...] = g
        d_ref[...] = -ADAM_LR * (m_hat / (jnp.sqrt(v_hat) + ADAM_EPS) + ADAM_WD * w_ref[...])
        nm_ref[...] = nm
        nv_ref[...] = nv

    blk = pl.BlockSpec((tr, cols), lambda i: (i, 0))
    return pl.pallas_call(
        body, name=name, grid=(rows // tr,),
        in_specs=[pl.BlockSpec((N_DEV, tr, cols), lambda i: (0, i, 0)), blk, blk, blk],
        out_specs=[blk] * 4,
        out_shape=[jax.ShapeDtypeStruct((rows, cols), F32)] * 4,
        compiler_params=_params(dimension_semantics=("arbitrary",)),
    )(parts, w, m, v)


def _pack_small(values):
    rows = []
    for v in values:
        size = math.prod(v.shape)
        padded = -(-size // 1024) * 1024
        flat = jnp.pad(v.reshape(-1).astype(F32), (0, padded - size))
        rows.append(flat.reshape(padded // 128, 128))
    return jnp.concatenate(rows, axis=0) if len(rows) > 1 else rows[0]


def _unpack_small(packed, shapes):
    out, r = [], 0
    for shape in shapes:
        size = math.prod(shape)
        nrow = -(-size // 1024) * 8
        out.append(packed[r:r + nrow].reshape(-1)[:size].reshape(shape))
        r += nrow
    return out


def kernel(x, norm_mix_g, w_in, w_att_out, w_pool_grp, pool_scale, w_pool_out, w_out, norm_mlp_g, w_mlp_in, w_mlp_out, norm_final_g, loss_target, m_norm_mix_g, m_w_in, m_w_att_out, m_w_pool_grp, m_pool_scale, m_w_pool_out, m_w_out, m_norm_mlp_g, m_w_mlp_in, m_w_mlp_out, m_norm_final_g, v_norm_mix_g, v_w_in, v_w_att_out, v_w_pool_grp, v_pool_scale, v_w_pool_out, v_w_out, v_norm_mlp_g, v_w_mlp_in, v_w_mlp_out, v_norm_final_g):
    x, tgt = x[0], loss_target[0]
    s = x.shape[0]
    g1, g2, g3 = norm_mix_g, norm_mlp_g, norm_final_g.reshape(1, D_MODEL)
    shards = [w_in[0], w_att_out[0], w_pool_out[0], w_out[0], w_mlp_in[0], w_mlp_out[0]]
    wire = [a.astype(BF16) for a in shards]
    cols = lambda a: jnp.transpose(a, (1, 0, 2)).reshape(a.shape[1], N_DEV * a.shape[2])
    rows = lambda a: a.reshape(N_DEV * a.shape[1], a.shape[2])
    blocks_of_cols = lambda a: jnp.transpose(a.reshape(a.shape[0], N_DEV, a.shape[1] // N_DEV), (1, 0, 2))
    blocks_of_rows = lambda a: a.reshape(N_DEV, a.shape[0] // N_DEV, a.shape[1])
    wbd = _window_weights(w_pool_grp[0]).astype(BF16)

    (u, ut), (first,) = _rms_u(x, g1, 1024, _Gather(wire[:1]))
    f_in = cols(first)
    (qkv0, qkv1, qkv2, zr), later = _in_proj_fwd(u, f_in, 1024, _Gather(wire[1:]))
    f_ao, f_po, f_out, f_mi, f_mo = cols(later[0]), cols(later[1]), rows(later[2]), cols(later[3]), rows(later[4])
    qkv_dil = (qkv0, qkv1, qkv2)
    flat = lambda a: a.reshape(s, a.shape[-1])
    shaped = lambda a, g: a if g == 0 else a.reshape(ATT_GROUPS[g][1], s // ATT_GROUPS[g][1], a.shape[-1])
    ol_dil = [shaped(_attn_fwd(flat(qkv_dil[g]), g), g) for g in range(N_GROUPS)]
    h1 = _mix_fwd(x, zr, ol_dil, wbd, pool_scale, f_ao, f_po, f_out, ts=512)

    dh1, mdt, hd, stats = _mlp_fwd_bwd(h1, tgt, jnp.concatenate([g2, g3], axis=0), f_mi, f_mo, ts=256)
    loss, dg2, dg3 = stats[0:1, 0:1], stats[1:2], stats[2:3]
    got = {"w_mlp_in": _weight_grad(mdt, hd, "grad_w_mlp_in", transpose_out=False, at_slot=0, b_slot=1),
           "w_mlp_out": _weight_grad(mdt, hd, "grad_w_mlp_out", transpose_out=True, at_slot=1, b_slot=0)}
    dzr, dc_dil, (g_out, g_ao, g_po, g_bd, g_scale) = _mix_bwd(
        dh1, zr, ol_dil, wbd, pool_scale, f_ao, f_po, f_out, ts=256)
    g_grp = jnp.stack([g_bd[g, off:off + POOL_GW, off:off + POOL_GW] for g, (_, off) in enumerate(GROUP_WIN)])
    early = [_Exchange([blocks_of_rows(g_out).astype(BF16)], [False]),
             _Exchange([blocks_of_cols(g_ao).astype(BF16), blocks_of_cols(g_po).astype(BF16)], [False, False]),
             _Exchange([_pack_small([g_grp])], [True])]
    dqkv_dil, arrived = [], []
    for g in range(N_GROUPS):
        dqkv, rode = _attn_bwd(flat(qkv_dil[g]), flat(dc_dil[g]), g, early[g])
        dqkv_dil.append(shaped(dqkv, g))
        arrived += list(rode)
    got["w_out"], got["w_att_out"], got["w_pool_out"], got_grp = arrived
    dx, dz, dg1 = _in_proj_bwd(x, dh1, dzr, dqkv_dil, g1, f_in, ts=512)
    got["w_in"] = _weight_grad(ut, dz, "grad_w_in", transpose_out=False)
    got_vec = _exchange([_pack_small([loss, dg1, g_scale, dg2, dg3])], [True], "gather_small_grads")[0]

    names = ["w_in", "w_att_out", "w_pool_out", "w_out", "w_mlp_in", "w_mlp_out"]
    ms = [m_w_in, m_w_att_out, m_w_pool_out, m_w_out, m_w_mlp_in, m_w_mlp_out]
    vs = [v_w_in, v_w_att_out, v_w_pool_out, v_w_out, v_w_mlp_in, v_w_mlp_out]
    upd = {}
    for k, name in enumerate(names):
        res = _adamw(got[name], shards[k], ms[k][0], vs[k][0], "adamw_" + name, tr=256)
        upd[name] = [a[None] for a in res]

    res = _adamw(got_grp, _pack_small([w_pool_grp]), _pack_small([m_w_pool_grp]), _pack_small([v_w_pool_grp]),
                 "adamw_w_pool_grp", tr=2048)
    upd["w_pool_grp"] = [_unpack_small(a, [w_pool_grp.shape])[0] for a in res]
    vec_w = [jnp.zeros((1,), F32), norm_mix_g, pool_scale, norm_mlp_g, norm_final_g]
    vec_m = [jnp.zeros((1,), F32), m_norm_mix_g, m_pool_scale, m_norm_mlp_g, m_norm_final_g]
    vec_v = [jnp.ones((1,), F32), v_norm_mix_g, v_pool_scale, v_norm_mlp_g, v_norm_final_g]
    res = _adamw(got_vec, _pack_small(vec_w), _pack_small(vec_m), _pack_small(vec_v), "adamw_vectors", tr=2048)
    shapes = [(), norm_mix_g.shape, pool_scale.shape, norm_mlp_g.shape, norm_final_g.shape]
    unpacked = [_unpack_small(a, shapes) for a in res]
    for k, name in enumerate(["loss", "norm_mix_g", "pool_scale", "norm_mlp_g", "norm_final_g"]):
        upd[name] = [unpacked[q][k] for q in range(4)]

    order = ["norm_mix_g", "w_in", "w_att_out", "w_pool_grp", "pool_scale", "w_pool_out", "w_out", "norm_mlp_g",
             "w_mlp_in", "w_mlp_out", "norm_final_g"]
    out = [upd["loss"][0], dx[None]]
    for q in range(4):
        out += [upd[name][q] for name in order]
    return tuple(out)
```

```python
import math

import jax
import jax.numpy as jnp
from jax import lax
from jax.experimental import pallas as pl
from jax.experimental.pallas import tpu as pltpu

F32 = jnp.float32
BF16 = jnp.bfloat16

D_MODEL = 1024
HEAD_DIM = 64
GROUP_W = 256
ATT_GROUPS = ((128, 1), (512, 4), (2048, 16))
N_GROUPS = 3
BLK = 128
ATT_W = 768
POOL_W = 768
POOL_GW = 192
D_FF = 4096
N_IN = 5120
REST_W = N_IN - 3 * ATT_W
NORM_EPS = 1e-6
ALIBI_MAX_BIAS = 8.0
N_DEV = 8
HALO = 32

ADAM_LR = 0.001
ADAM_B1 = 0.9
ADAM_B2 = 0.999
ADAM_EPS = 1e-08
ADAM_WD = 0.01
ADAM_STEP = 10

VMEM_LIMIT = 56 * 1024 * 1024
ANY = pl.BlockSpec(memory_space=pl.ANY)


def _params(**kw):
    return pltpu.CompilerParams(vmem_limit_bytes=VMEM_LIMIT, **kw)


def _dot(a, b):
    return jnp.dot(a, b, preferred_element_type=F32)


def _dot_nt(a, b):
    return lax.dot_general(a, b, (((1,), (1,)), ((), ())), preferred_element_type=F32)


def _dot_tn(a, b):
    return lax.dot_general(a, b, (((0,), (0,)), ((), ())), preferred_element_type=F32)


def _slope(head):
    return 2.0 ** (-ALIBI_MAX_BIAS * (head + 1.0) / 12.0)


def _load_resident(step, pairs, sem):
    @pl.when(step == 0)
    def _():
        copies = [pltpu.make_async_copy(src, dst, sem.at[n]) for n, (src, dst) in enumerate(pairs)]
        for cp in copies:
            cp.start()
        for cp in copies:
            cp.wait()


LANES = 128


def _to_class_order(value, dil, buf, ref, col0):
    ts, w = value.shape
    if dil == 1:
        ref[:, col0:col0 + w] = value.astype(ref.dtype)
        return
    for c in range(w // LANES):
        buf[c] = value[:, c * LANES:(c + 1) * LANES]
        for r in range(dil):
            ref[r, :, col0 + c * LANES:col0 + (c + 1) * LANES] = (
                buf[c, pl.ds(r, ts // dil, stride=dil), :].astype(ref.dtype))


def _to_token_order(ref, dil, buf, ts):
    if dil == 1:
        return ref[...].astype(F32)
    w = ref.shape[-1]
    for c in range(w // LANES):
        for r in range(dil):
            buf[c, pl.ds(r, ts // dil, stride=dil), :] = ref[r, :, c * LANES:(c + 1) * LANES].astype(F32)
    return jnp.concatenate([buf[c] for c in range(w // LANES)], axis=1)


def _column_blocks(stacked, dst):
    tn = stacked.shape[2]
    return [(stacked.at[j], dst.at[:, pl.ds(j * tn, tn)]) for j in range(N_DEV)]


def _sigmoid(x):
    return 0.5 * jnp.tanh(0.5 * x) + 0.5


def _rms(x):
    return lax.rsqrt(jnp.mean(x * x, axis=-1, keepdims=True) + NORM_EPS)


def _rms_bwd(dn, n, r):
    return r * (dn - n * jnp.mean(dn * n, axis=-1, keepdims=True))


def _split_refs(refs, counts):
    out, at = [], 0
    for c in counts:
        out.append(refs[at:at + c])
        at += c
    return out


def _carry_start(ex, step, ins, outs, sems):
    @pl.when(step == 0)
    def _():
        ex.start(ins, outs, sems)


def _carry_wait(ex, step, last, ins, outs, sems, pass_at=None):
    if ex.passes_on:
        @pl.when(step == (last if pass_at is None else pass_at))
        def _():
            ex.pass_on(ins, outs, sems)

    @pl.when(step == last)
    def _():
        ex.wait(ins, outs, sems)


def _rms_u(x, g1, ts, ex):
    s = x.shape[0]
    n = s // ts

    def body(*refs):
        (x_ref, g_ref), ex_ins, (u_ref, ut_ref), ex_outs, ex_sems = _split_refs(refs, (2, ex.nw, 2, ex.nw, 3))
        i = pl.program_id(0)
        _carry_start(ex, i, ex_ins, ex_outs, ex_sems)
        x = x_ref[...]
        u = x * _rms(x) * g_ref[...]
        u_ref[...] = u.astype(BF16)
        ut_ref[...] = u.T.astype(BF16)
        _carry_wait(ex, i, n - 1, ex_ins, ex_outs, ex_sems)

    outs = pl.pallas_call(
        body, name="rms_u", grid=(n,),
        in_specs=[pl.BlockSpec((ts, D_MODEL), lambda i: (i, 0)), pl.BlockSpec((1, D_MODEL), lambda i: (0, 0))]
                 + ex.specs,
        out_specs=[pl.BlockSpec((ts, D_MODEL), lambda i: (i, 0)), pl.BlockSpec((D_MODEL, ts), lambda i: (0, i))]
                  + ex.specs,
        out_shape=[jax.ShapeDtypeStruct((s, D_MODEL), BF16), jax.ShapeDtypeStruct((D_MODEL, s), BF16)] + ex.out_shape,
        scratch_shapes=ex.scratch,
        compiler_params=_params(dimension_semantics=("arbitrary",)),
    )(x, g1, *ex.arrays)
    return outs[:2], outs[2:]


def _in_proj_fwd(u, w_in, ts, ex):
    s = u.shape[0]
    n = s // ts
    dils = [d for _, d in ATT_GROUPS]

    def body(*refs):
        (u_ref, w_hbm), ex_ins, (q0_ref, q1_ref, q2_ref, zr_ref), ex_outs, (w_ref, zbuf, sem), ex_sems = (
            _split_refs(refs, (2, ex.nw, 4, ex.nw, 3, 3)))
        i = pl.program_id(0)
        _carry_start(ex, i, ex_ins, ex_outs, ex_sems)
        _load_resident(i, _column_blocks(w_hbm, w_ref), sem)
        ub = u_ref[...]
        outs = (q0_ref, q1_ref, q2_ref)
        for sec in range(3):
            for g in range(N_GROUPS):
                c0 = sec * ATT_W + g * GROUP_W
                zc = _dot(ub, w_ref[:, c0:c0 + GROUP_W])
                _to_class_order(zc, dils[g], zbuf, outs[g], sec * GROUP_W)
        for c0 in range(0, REST_W, 256):
            zr_ref[:, c0:c0 + 256] = _dot(ub, w_ref[:, 3 * ATT_W + c0:3 * ATT_W + c0 + 256]).astype(BF16)
        _carry_wait(ex, i, n - 1, ex_ins, ex_outs, ex_sems, pass_at=max(n - 3, 0))

    outs = pl.pallas_call(
        body, name="in_proj_fwd", grid=(n,),
        in_specs=[pl.BlockSpec((ts, D_MODEL), lambda i: (i, 0)), ANY] + ex.specs,
        out_specs=[pl.BlockSpec((ts, ATT_W), lambda i: (i, 0)),
                   pl.BlockSpec((4, ts // 4, ATT_W), lambda i: (0, i, 0)),
                   pl.BlockSpec((16, ts // 16, ATT_W), lambda i: (0, i, 0)),
                   pl.BlockSpec((ts, REST_W), lambda i: (i, 0))] + ex.specs,
        out_shape=[jax.ShapeDtypeStruct((s, ATT_W), BF16),
                   jax.ShapeDtypeStruct((4, s // 4, ATT_W), BF16),
                   jax.ShapeDtypeStruct((16, s // 16, ATT_W), BF16),
                   jax.ShapeDtypeStruct((s, REST_W), BF16)] + ex.out_shape,
        scratch_shapes=[pltpu.VMEM((D_MODEL, N_IN), BF16), pltpu.VMEM((GROUP_W // LANES, ts, LANES), F32),
                        pltpu.SemaphoreType.DMA((N_DEV,))] + ex.scratch,
        compiler_params=_params(dimension_semantics=("arbitrary",)),
    )(u, w_in, *ex.arrays)
    return outs[:4], outs[4:]


ATT_TILE = 8 * BLK


HEADS = GROUP_W // HEAD_DIM
STACK = HEADS * BLK


SCORE_SCALE = HEAD_DIM ** -0.5


def _band_consts(group, dil):
    row = lax.broadcasted_iota(jnp.int32, (STACK, 2 * BLK), 0)
    kj = lax.broadcasted_iota(jnp.int32, (STACK, 2 * BLK), 1)
    head = row // BLK
    steps = BLK + (row % BLK) - kj
    slope = jnp.full((STACK, 2 * BLK), _slope(4 * group + HEADS - 1), F32)
    for h in range(HEADS - 1):
        slope = jnp.where(head == h, _slope(4 * group + h), slope)
    in_band = (steps >= 0) & (steps <= BLK)
    return jnp.where(in_band, slope * (steps.astype(F32) * float(dil)), jnp.inf), kj


def _first_key(block, nbc, nb, b):
    if nbc % nb == 0 and b != 0:
        return None
    return jnp.where((block % nbc) != 0, 0, BLK)


def _head_of_col():
    return lax.broadcasted_iota(jnp.int32, (BLK, GROUP_W), 1) // HEAD_DIM


def _stack_heads(xb):
    head_of = _head_of_col()
    return jnp.concatenate([jnp.where(head_of == h, xb, jnp.zeros_like(xb)) for h in range(HEADS)], axis=0)


def _per_head_cols(stacked, rhs, scale_rows=None):
    lane = lax.broadcasted_iota(jnp.int32, (BLK, LANES), 1)
    halves = []
    for pair in range(HEADS // 2):
        tile = rhs[:, pair * LANES:(pair + 1) * LANES]
        parts = []
        for h in (2 * pair, 2 * pair + 1):
            part = _dot(stacked[h * BLK:(h + 1) * BLK], tile)
            parts.append(part if scale_rows is None else part * scale_rows[h * BLK:(h + 1) * BLK])
        halves.append(jnp.where(lane < HEAD_DIM, parts[0], parts[1]))
    return jnp.concatenate(halves, axis=1)


def _per_head_rows(col):
    lane = lax.broadcasted_iota(jnp.int32, (BLK, LANES), 1)
    return jnp.concatenate(
        [jnp.where(lane < HEAD_DIM, col[(2 * pair) * BLK:(2 * pair + 1) * BLK], col[(2 * pair + 1) * BLK:(2 * pair + 2) * BLK])
         for pair in range(HEADS // 2)], axis=1)


def _band_softmax(qs, kb, penalty, kj, first_key):
    sc = _dot_nt(qs, kb) - penalty
    if first_key is not None:
        sc = jnp.where(kj >= first_key, sc, -jnp.inf)
    mx = jnp.max(sc, axis=1, keepdims=True)
    e = jnp.exp(sc - mx)
    return e, mx, jnp.sum(e, axis=1, keepdims=True)


def _attn_fwd(qkv, group):
    s = qkv.shape[0]
    dil = ATT_GROUPS[group][1]
    nbc = s // (BLK * dil)
    nb = ATT_TILE // BLK

    def body(q_ref, kc_ref, kp_ref, vc_ref, vp_ref, ol_ref, kbuf, vbuf):
        i = pl.program_id(0)
        kbuf[0:BLK] = kp_ref[...]
        kbuf[BLK:BLK + ATT_TILE] = kc_ref[...]
        vbuf[0:BLK] = vp_ref[...]
        vbuf[BLK:BLK + ATT_TILE] = vc_ref[...]
        penalty, kj = _band_consts(group, dil)
        blocks = range(nb)
        qss = [_stack_heads(q_ref[b * BLK:(b + 1) * BLK, :] * SCORE_SCALE) for b in blocks]
        soft = [_band_softmax(qss[b], kbuf[b * BLK:b * BLK + 2 * BLK, :], penalty, kj,
                              _first_key(i * nb + b, nbc, nb, b)) for b in blocks]
        for b in blocks:
            e, mx, den = soft[b]
            ol_ref[b * BLK:(b + 1) * BLK, 0:GROUP_W] = _per_head_cols(
                e.astype(BF16), vbuf[b * BLK:b * BLK + 2 * BLK, :], 1.0 / den)
        for b in blocks:
            e, mx, den = soft[b]
            ol_ref[b * BLK:(b + 1) * BLK, GROUP_W:] = _per_head_rows(mx + jnp.log(den))

    n = s // ATT_TILE
    cur = lambda c: pl.BlockSpec((ATT_TILE, GROUP_W), lambda i: (i, c))
    prev = lambda c: pl.BlockSpec((BLK, GROUP_W), lambda i: (jnp.maximum(i * nb - 1, 0), c))
    return pl.pallas_call(
        body, name=f"attn_fwd_g{group}", grid=(n,),
        in_specs=[cur(0), cur(1), prev(1), cur(2), prev(2)],
        out_specs=pl.BlockSpec((ATT_TILE, 2 * GROUP_W), lambda i: (i, 0)),
        out_shape=jax.ShapeDtypeStruct((s, 2 * GROUP_W), F32),
        scratch_shapes=[pltpu.VMEM((BLK + ATT_TILE, GROUP_W), BF16), pltpu.VMEM((BLK + ATT_TILE, GROUP_W), BF16)],
        compiler_params=_params(dimension_semantics=("arbitrary",)),
    )(qkv, qkv, qkv, qkv, qkv)


def _attn_bwd(qkv, dc, group, ex=None):
    s = qkv.shape[0]
    dil = ATT_GROUPS[group][1]
    nbc = s // (BLK * dil)
    nb = ATT_TILE // BLK
    n = s // ATT_TILE
    nex = ex.nw if ex else 0

    def body(*refs):
        ((q_ref, kc_ref, kp_ref, vc_ref, vp_ref, do_ref, c_ref), ex_ins, (out_ref,), ex_outs,
         (kbuf, vbuf, dqpend, dkpend, dvpend), ex_sems) = _split_refs(refs, (7, nex, 1, nex, 5, 3 if ex else 0))
        i = pl.program_id(0)
        if ex:
            _carry_start(ex, i, ex_ins, ex_outs, ex_sems)

        @pl.when(i == 0)
        def _():
            dqpend[...] = jnp.zeros_like(dqpend)
            dkpend[...] = jnp.zeros_like(dkpend)
            dvpend[...] = jnp.zeros_like(dvpend)

        out_ref[:, 0:GROUP_W] = dqpend[...]
        body_rows = slice(0, ATT_TILE - BLK)
        tail = slice(ATT_TILE - BLK, ATT_TILE)
        pends = ((dkpend, GROUP_W), (dvpend, 2 * GROUP_W))
        for pend, c0 in pends:
            out_ref[body_rows, c0:c0 + GROUP_W] = pend[body_rows, :].astype(BF16)

        @pl.when(i < n)
        def _():
            kbuf[0:BLK] = kp_ref[...]
            kbuf[BLK:BLK + ATT_TILE] = kc_ref[...]
            vbuf[0:BLK] = vp_ref[...]
            vbuf[BLK:BLK + ATT_TILE] = vc_ref[...]
            penalty, kj = _band_consts(group, dil)
            head_of = _head_of_col()
            blocks = range(nb)
            rows = [slice(b * BLK, (b + 1) * BLK) for b in blocks]
            kbs = [kbuf[b * BLK:b * BLK + 2 * BLK, :] for b in blocks]
            vbs = [vbuf[b * BLK:b * BLK + 2 * BLK, :] for b in blocks]
            qss = [_stack_heads(q_ref[rows[b], :] * SCORE_SCALE) for b in blocks]
            doss = [_stack_heads(do_ref[rows[b], :].astype(BF16)) for b in blocks]
            cors = []
            for b in blocks:
                cb = c_ref[rows[b], :]
                cors.append(jnp.concatenate(
                    [jnp.max(jnp.where(head_of == h, cb, -jnp.inf), axis=1, keepdims=True) for h in range(HEADS)],
                    axis=0))
            soft = [_band_softmax(qss[b], kbs[b], penalty, kj, _first_key(i * nb + b, nbc, nb, b)) for b in blocks]
            dps = [_dot_nt(doss[b], vbs[b]) for b in blocks]
            ps = [soft[b][0] * (1.0 / soft[b][2]) for b in blocks]
            dss = [(ps[b] * (dps[b] + cors[b])).astype(BF16) for b in blocks]
            for b in blocks:
                dqpend[rows[b], :] = _per_head_cols(dss[b], kbs[b] * SCORE_SCALE).astype(BF16)
            bands = [(_dot_tn(dss[b], qss[b]), _dot_tn(ps[b].astype(BF16), doss[b])) for b in blocks]
            for which, (pend, c0) in enumerate(pends):
                out_ref[tail, c0:c0 + GROUP_W] = (pend[tail, :] + bands[0][which][0:BLK]).astype(BF16)
                for b in range(nb):
                    own = bands[b][which][BLK:2 * BLK]
                    pend[b * BLK:(b + 1) * BLK, :] = own + bands[b + 1][which][0:BLK] if b + 1 < nb else own

        @pl.when(i == n)
        def _():
            for pend, c0 in pends:
                out_ref[tail, c0:c0 + GROUP_W] = pend[tail, :].astype(BF16)

        if ex:
            _carry_wait(ex, i, n, ex_ins, ex_outs, ex_sems)

    last = n - 1
    cur = lambda c: pl.BlockSpec((ATT_TILE, GROUP_W), lambda i: (jnp.minimum(i, last), c))
    prev = lambda c: pl.BlockSpec(
        (BLK, GROUP_W), lambda i: (jnp.maximum(jnp.minimum(i, last) * nb - 1, 0), c))
    outs = pl.pallas_call(
        body, name=f"attn_bwd_g{group}", grid=(n + 1,),
        in_specs=[cur(0), cur(1), prev(1), cur(2), prev(2), cur(0), cur(1)] + (ex.specs if ex else []),
        out_specs=[pl.BlockSpec((ATT_TILE, ATT_W), lambda i: (jnp.maximum(i - 1, 0), 0))] + (ex.specs if ex else []),
        out_shape=[jax.ShapeDtypeStruct((s, ATT_W), BF16)] + (ex.out_shape if ex else []),
        scratch_shapes=[pltpu.VMEM((BLK + ATT_TILE, GROUP_W), BF16), pltpu.VMEM((BLK + ATT_TILE, GROUP_W), BF16),
                        pltpu.VMEM((ATT_TILE, GROUP_W), BF16),
                        pltpu.VMEM((ATT_TILE, GROUP_W), F32), pltpu.VMEM((ATT_TILE, GROUP_W), F32)]
                       + (ex.scratch if ex else []),
        compiler_params=_params(dimension_semantics=("arbitrary",)),
    )(qkv, qkv, qkv, qkv, qkv, dc, dc, *(ex.arrays if ex else []))
    return (outs[0], outs[1:]) if ex else outs[0]


def _dil_specs(ts, width, idx):
    return [pl.BlockSpec((ts, width), lambda i: (idx(i), 0)),
            pl.BlockSpec((4, ts // 4, width), lambda i: (0, idx(i), 0)),
            pl.BlockSpec((16, ts // 16, width), lambda i: (0, idx(i), 0))]


def _gather_rows(refs, buf, ts):
    return [_to_token_order(refs[g], ATT_GROUPS[g][1], buf, ts) for g in range(N_GROUPS)]


def _inverse_counts(ts):
    col = jnp.arange(POOL_W)
    win = jnp.where(col < POOL_GW, 2, jnp.where(col < 2 * POOL_GW, 4, jnp.where(col < 3 * POOL_GW, 8, 16)))
    first = jnp.minimum(jnp.arange(ts)[:, None] + 1, win[None, :])
    return 1.0 / jnp.stack([first, jnp.broadcast_to(win[None, :], (ts, POOL_W))]).astype(F32)


def _pool_fwd(ebuf, s2, s4, s8, inv_count, ts):
    n = ts + HALO
    s2[8:n] = ebuf[8:n] + ebuf[7:n - 1]
    s4[16:n] = s2[16:n] + s2[14:n - 2]
    s8[24:n] = s4[24:n] + s4[20:n - 4]
    s16 = s8[32:n] + s8[24:n - 8]
    col = lax.broadcasted_iota(jnp.int32, (ts, POOL_W), 1)
    psum = jnp.where(col < POOL_GW, s2[32:n],
                     jnp.where(col < 2 * POOL_GW, s4[32:n], jnp.where(col < 3 * POOL_GW, s8[32:n], s16)))
    return psum * inv_count - ebuf[32:n]


GROUP_WIN = ((0, 0), (128, 64), (384, 0), (512, 64))
WIN = 256


def _window_weights(w_grp):
    return jnp.stack([jnp.pad(w_grp[g], ((off, WIN - off - POOL_GW), (off, WIN - off - POOL_GW)))
                      for g, (_, off) in enumerate(GROUP_WIN)])


def _group_matmul(xb, wwin_ref, transposed=False):
    outs = []
    for g, (start, _) in enumerate(GROUP_WIN):
        xw = xb[:, start:start + WIN]
        outs.append(_dot_nt(xw, wwin_ref[g]) if transposed else _dot(xw, wwin_ref[g]))
    half = WIN // 2
    return jnp.concatenate([outs[0][:, :half], outs[0][:, half:] + outs[1][:, :half], outs[1][:, half:],
                            outs[2][:, :half], outs[2][:, half:] + outs[3][:, :half], outs[3][:, half:]], axis=1)


def _mix_core(zr, pooled, outs, lses, wwin_ref, scale, wao, wpo):
    mixed = _group_matmul(pooled.astype(BF16), wwin_ref)
    p = mixed * scale
    l0, l1, l2 = lses
    mx = jnp.maximum(jnp.maximum(l0, l1), l2)
    e0, e1, e2 = jnp.exp(l0 - mx), jnp.exp(l1 - mx), jnp.exp(l2 - mx)
    inv = 1.0 / (e0 + e1 + e2)
    wts = (e0 * inv, e1 * inv, e2 * inv)
    a = wts[0] * outs[0] + wts[1] * outs[1] + wts[2] * outs[2]
    att = _dot(a.astype(BF16), wao)
    pol = _dot(p.astype(BF16), wpo)
    sga = _sigmoid(zr[:, POOL_W:POOL_W + D_MODEL].astype(F32))
    sgp = _sigmoid(zr[:, POOL_W + D_MODEL:].astype(F32))
    mg = sga * att + sgp * pol
    return dict(mixed=mixed, p=p, wts=wts, a=a, att=att, pol=pol, sga=sga, sgp=sgp, mg=mg)


def _fill_pool_input(ebuf, zr_ref, halo_ref, t0):
    ts = zr_ref.shape[0]
    halo = halo_ref[...].astype(F32)
    t = t0 - HALO + lax.broadcasted_iota(jnp.int32, (HALO, POOL_W), 0)
    ebuf[0:HALO] = jnp.where(t >= 0, halo, 0.0)
    ebuf[HALO:HALO + ts] = zr_ref[:, 0:POOL_W].astype(F32)


def _mix_fwd(x, zr, ol_dil, wbd, scale, wao, wpo, wout, ts):
    s = x.shape[0]
    n = s // ts

    def body(x_ref, zr_ref, halo_ref, ic_ref, ol0, ol1, ol2, wbd_ref, sc_ref, wao_ref, wpo_ref, wout_ref,
             h1_ref, ebuf, s2, s4, s8, rbuf):
        i = pl.program_id(0)
        _fill_pool_input(ebuf, zr_ref, halo_ref, i * ts)
        pooled = _pool_fwd(ebuf, s2, s4, s8, ic_ref[...], ts)
        ols = _gather_rows((ol0, ol1, ol2), rbuf, ts)
        outs, lses = [a[:, :GROUP_W] for a in ols], [a[:, GROUP_W:] for a in ols]
        f = _mix_core(zr_ref[...], pooled, outs, lses, wbd_ref, sc_ref[...], wao_ref[...], wpo_ref[...])
        h1_ref[...] = x_ref[...] + _dot(f["mg"].astype(BF16), wout_ref[...])

    whole = lambda a: pl.BlockSpec(a.shape, lambda i: (0,) * a.ndim)
    idx = lambda i: i
    return pl.pallas_call(
        body, name="mix_fwd", grid=(n,),
        in_specs=[pl.BlockSpec((ts, D_MODEL), lambda i: (i, 0)),
                  pl.BlockSpec((ts, REST_W), lambda i: (i, 0)),
                  pl.BlockSpec((HALO, POOL_W), lambda i: (jnp.maximum(i * (ts // HALO) - 1, 0), 0)),
                  pl.BlockSpec((None, ts, POOL_W), lambda i: (jnp.minimum(i, 1), 0, 0))]
                 + _dil_specs(ts, 2 * GROUP_W, idx)
                 + [whole(wbd), whole(scale), whole(wao), whole(wpo), whole(wout)],
        out_specs=pl.BlockSpec((ts, D_MODEL), lambda i: (i, 0)),
        out_shape=jax.ShapeDtypeStruct((s, D_MODEL), F32),
        scratch_shapes=[pltpu.VMEM((ts + HALO, POOL_W), F32)] * 4
                       + [pltpu.VMEM((2 * GROUP_W // LANES, ts, LANES), F32)],
        compiler_params=_params(dimension_semantics=("arbitrary",)),
    )(x, zr, zr, _inverse_counts(ts), *ol_dil, wbd, scale, wao, wpo, wout)


def _mix_bwd(dh1, zr, ol_dil, wbd, scale, wao, wpo, wout, ts):
    s = dh1.shape[0]
    n = s // ts

    def body(dh_ref, zr_ref, halo_ref, ic_ref, ol0, ol1, ol2, sc_ref, wbd_hbm, wao_hbm, wpo_hbm, wout_hbm,
             dzr_ref, dc0, dc1, dc2, gsc_ref, gwout_hbm, gwao_hbm, gwpo_hbm, gwbd_hbm,
             ebuf, s2, s4, s8, gbuf, t2, t4, t8, rbuf,
             wbd_ref, wao_ref, wpo_ref, wout_ref, gwout_ref, gwao_ref, gwpo_ref, gwbd_ref, sem):
        j = pl.program_id(0)
        i = n - 1 - j
        _load_resident(j, [(wbd_hbm, wbd_ref), (wao_hbm, wao_ref), (wpo_hbm, wpo_ref), (wout_hbm, wout_ref)], sem)

        @pl.when(j == 0)
        def _():
            gwout_ref[...] = jnp.zeros_like(gwout_ref)
            gwao_ref[...] = jnp.zeros_like(gwao_ref)
            gwpo_ref[...] = jnp.zeros_like(gwpo_ref)
            gwbd_ref[...] = jnp.zeros_like(gwbd_ref)
            gsc_ref[...] = jnp.zeros_like(gsc_ref)
            gbuf[ts:ts + HALO] = jnp.zeros((HALO, POOL_W), F32)

        _fill_pool_input(ebuf, zr_ref, halo_ref, i * ts)
        inv_count = ic_ref[...]
        pooled = _pool_fwd(ebuf, s2, s4, s8, inv_count, ts)
        ols = _gather_rows((ol0, ol1, ol2), rbuf, ts)
        outs, lses = [a[:, :GROUP_W] for a in ols], [a[:, GROUP_W:] for a in ols]
        zr = zr_ref[...]
        wao, wpo, wout = wao_ref[...], wpo_ref[...], wout_ref[...]
        scale = sc_ref[...]
        f = _mix_core(zr, pooled, outs, lses, wbd_ref, scale, wao, wpo)

        dhb = dh_ref[...].astype(BF16)
        gwout_ref[...] += _dot(f["mg"].T.astype(BF16), dhb)
        dmg = _dot_nt(dhb, wout)
        sga, sgp, att, pol = f["sga"], f["sgp"], f["att"], f["pol"]
        datt = dmg * sga
        dpol = dmg * sgp
        dzr_ref[:, POOL_W:POOL_W + D_MODEL] = (dmg * att * sga * (1.0 - sga)).astype(BF16)
        dzr_ref[:, POOL_W + D_MODEL:] = (dmg * pol * sgp * (1.0 - sgp)).astype(BF16)
        dattb = datt.astype(BF16)
        dpolb = dpol.astype(BF16)
        gwao_ref[...] += _dot(f["a"].T.astype(BF16), dattb)
        gwpo_ref[...] += _dot(f["p"].T.astype(BF16), dpolb)
        da = _dot_nt(dattb, wao)
        dp = _dot_nt(dpolb, wpo)

        gsc_ref[...] += jnp.sum(f["mixed"] * dp, axis=0, keepdims=True)
        dmixed = (dp * scale).astype(BF16)
        pooled_t = pooled.T.astype(BF16)
        for g, (start, _) in enumerate(GROUP_WIN):
            gwbd_ref[g] += _dot(pooled_t[start:start + WIN, :], dmixed[:, start:start + WIN])
        dpooled = _group_matmul(dmixed, wbd_ref, transposed=True)
        gbuf[0:ts] = dpooled * inv_count
        m = ts + HALO
        t2[0:m - 8] = gbuf[0:m - 8] + gbuf[1:m - 7]
        t4[0:m - 16] = t2[0:m - 16] + t2[2:m - 14]
        t8[0:m - 24] = t4[0:m - 24] + t4[4:m - 20]
        t16 = t8[0:ts] + t8[8:ts + 8]
        col = lax.broadcasted_iota(jnp.int32, (ts, POOL_W), 1)
        back = jnp.where(col < POOL_GW, t2[0:ts],
                         jnp.where(col < 2 * POOL_GW, t4[0:ts], jnp.where(col < 3 * POOL_GW, t8[0:ts], t16)))
        dzr_ref[:, 0:POOL_W] = (back - dpooled).astype(BF16)
        gbuf[ts:ts + HALO] = gbuf[0:HALO]

        head_of = lax.broadcasted_iota(jnp.int32, (ts, GROUP_W), 1) // HEAD_DIM
        prod = da * f["a"]
        inner = jnp.zeros((ts, GROUP_W), F32)
        for h in range(4):
            hm = head_of == h
            tot = jnp.sum(jnp.where(hm, prod, 0.0), axis=1, keepdims=True)
            inner = jnp.where(hm, tot, inner)
        for g, dc_ref in enumerate((dc0, dc1, dc2)):
            both = jnp.concatenate([f["wts"][g] * da, -f["wts"][g] * inner], axis=1)
            _to_class_order(both, ATT_GROUPS[g][1], rbuf, dc_ref, 0)

        @pl.when(j == n - 1)
        def _():
            pairs = ((gwout_ref, gwout_hbm), (gwao_ref, gwao_hbm), (gwpo_ref, gwpo_hbm), (gwbd_ref, gwbd_hbm))
            copies = [pltpu.make_async_copy(src, dst, sem.at[k]) for k, (src, dst) in enumerate(pairs)]
            for cp in copies:
                cp.start()
            for cp in copies:
                cp.wait()

    idx = lambda j: n - 1 - j
    dc_shapes = [jax.ShapeDtypeStruct((s, 2 * GROUP_W), F32), jax.ShapeDtypeStruct((4, s // 4, 2 * GROUP_W), F32),
                 jax.ShapeDtypeStruct((16, s // 16, 2 * GROUP_W), F32)]
    weights = (wbd, wao, wpo, wout)
    grad_shapes = [(D_MODEL, D_MODEL), (GROUP_W, D_MODEL), (POOL_W, D_MODEL), (len(GROUP_WIN), WIN, WIN)]
    tile_buf = pltpu.VMEM((ts + HALO, POOL_W), F32)
    outs = pl.pallas_call(
        body, name="mix_bwd", grid=(n,),
        in_specs=[pl.BlockSpec((ts, D_MODEL), lambda j: (idx(j), 0)),
                  pl.BlockSpec((ts, REST_W), lambda j: (idx(j), 0)),
                  pl.BlockSpec((HALO, POOL_W), lambda j: (jnp.maximum(idx(j) * (ts // HALO) - 1, 0), 0)),
                  pl.BlockSpec((None, ts, POOL_W), lambda j: (jnp.minimum(idx(j), 1), 0, 0))]
                 + _dil_specs(ts, 2 * GROUP_W, idx)
                 + [pl.BlockSpec((1, POOL_W), lambda j: (0, 0))] + [ANY] * 4,
        out_specs=[pl.BlockSpec((ts, REST_W), lambda j: (idx(j), 0))]
                  + _dil_specs(ts, 2 * GROUP_W, idx)
                  + [pl.BlockSpec((1, POOL_W), lambda j: (0, 0))] + [ANY] * 4,
        out_shape=[jax.ShapeDtypeStruct((s, REST_W), BF16)] + dc_shapes
                  + [jax.ShapeDtypeStruct((1, POOL_W), F32)]
                  + [jax.ShapeDtypeStruct(shape, F32) for shape in grad_shapes],
        scratch_shapes=[tile_buf] * 8 + [pltpu.VMEM((2 * GROUP_W // LANES, ts, LANES), F32)]
                       + [pltpu.VMEM(w.shape, BF16) for w in weights]
                       + [pltpu.VMEM(shape, F32) for shape in grad_shapes]
                       + [pltpu.SemaphoreType.DMA((4,))],
        compiler_params=_params(dimension_semantics=("arbitrary",)),
    )(dh1, zr, zr, _inverse_counts(ts), *ol_dil, scale, wbd, wao, wpo, wout)
    dzr, dc_dil, g_scale = outs[0], outs[1:4], outs[4]
    g_out, g_ao, g_po, g_bd = outs[5:]
    return dzr, dc_dil, (g_out, g_ao, g_po, g_bd, g_scale)


FF_CHUNK = 1024


STAT_ROWS = 8


def _mlp_fwd_bwd(h1, tgt, gains, wmi, wmo, ts):
    s = h1.shape[0]
    n = s // ts
    nchunk = D_FF // FF_CHUNK

    def body(h1_ref, t_ref, g_ref, wmi_hbm, wmo_hbm, dh1_ref, mdt_ref, hd_ref, stat_ref, wmi, wmo, relu_buf, sem):
        i = pl.program_id(0)
        _load_resident(i, _column_blocks(wmi_hbm, wmi) + [(wmo_hbm, wmo)], sem)

        @pl.when(i == 0)
        def _():
            stat_ref[...] = jnp.zeros_like(stat_ref)

        h1 = h1_ref[...]
        g2 = g_ref[0:1, :]
        g3 = g_ref[1:2, :]
        r2 = _rms(h1)
        n2 = h1 * r2
        m = n2 * g2
        mb = m.astype(BF16)
        mdt_ref[0] = m.T.astype(BF16)
        h2 = h1
        for c in range(nchunk):
            cols = slice(c * FF_CHUNK, (c + 1) * FF_CHUNK)
            rl = jnp.maximum(_dot(mb, wmi[:, cols]), 0.0)
            relu_buf[:, cols] = rl
            hb = (rl * rl).astype(BF16)
            hd_ref[0, :, cols] = hb
            h2 = h2 + _dot(hb, wmo[cols, :])
        r3 = _rms(h2)
        n3 = h2 * r3
        diff = n3 * g3 - t_ref[...]
        loss = jnp.sum(0.5 * jnp.sum(diff * diff, axis=1, keepdims=True) / D_MODEL, axis=0, keepdims=True)
        dy = diff * (1.0 / D_MODEL)
        dg3 = jnp.sum(dy * n3, axis=0, keepdims=True)
        dh2 = _rms_bwd(dy * g3, n3, r3)
        dh2b = dh2.astype(BF16)
        mdt_ref[1] = dh2.T.astype(BF16)
        dm = jnp.zeros((ts, D_MODEL), F32)
        for c in range(nchunk):
            cols = slice(c * FF_CHUNK, (c + 1) * FF_CHUNK)
            dfb = (_dot_nt(dh2b, wmo[cols, :]) * (2.0 * relu_buf[:, cols])).astype(BF16)
            hd_ref[1, :, cols] = dfb
            dm = dm + _dot_nt(dfb, wmi[:, cols])
        dg2 = jnp.sum(dm * n2, axis=0, keepdims=True)
        dh1_ref[...] = dh2 + _rms_bwd(dm * g2, n2, r2)
        row = lax.broadcasted_iota(jnp.int32, (STAT_ROWS, D_MODEL), 0)
        stat_ref[...] += jnp.where(row == 0, loss, jnp.where(row == 1, dg2, jnp.where(row == 2, dg3, 0.0)))

    row = lambda w: pl.BlockSpec((ts, w), lambda i: (i, 0))
    return pl.pallas_call(
        body, name="mlp_fwd_bwd", grid=(n,),
        in_specs=[row(D_MODEL), row(D_MODEL), pl.BlockSpec((2, D_MODEL), lambda i: (0, 0)), ANY, ANY],
        out_specs=[row(D_MODEL), pl.BlockSpec((2, D_MODEL, ts), lambda i: (0, 0, i)),
                   pl.BlockSpec((2, ts, D_FF), lambda i: (0, i, 0)),
                   pl.BlockSpec((STAT_ROWS, D_MODEL), lambda i: (0, 0))],
        out_shape=[jax.ShapeDtypeStruct((s, D_MODEL), F32), jax.ShapeDtypeStruct((2, D_MODEL, s), BF16),
                   jax.ShapeDtypeStruct((2, s, D_FF), BF16), jax.ShapeDtypeStruct((STAT_ROWS, D_MODEL), F32)],
        scratch_shapes=[pltpu.VMEM((D_MODEL, D_FF), BF16), pltpu.VMEM((D_FF, D_MODEL), BF16),
                        pltpu.VMEM((ts, D_FF), F32), pltpu.SemaphoreType.DMA((N_DEV + 1,))],
        compiler_params=_params(dimension_semantics=("arbitrary",)),
    )(h1, tgt, gains, wmi, wmo)


def _in_proj_bwd(x, dh1, dzr, dqkv_dil, g1, w_in, ts):
    s = x.shape[0]
    n = s // ts

    def body(x_ref, dh_ref, dzr_ref, q0, q1, q2, g_ref, w_hbm, dx_ref, dz_ref, dg_ref, w_ref, qbuf, sem):
        i = pl.program_id(0)
        _load_resident(i, _column_blocks(w_hbm, w_ref), sem)

        @pl.when(i == 0)
        def _():
            dg_ref[...] = jnp.zeros_like(dg_ref)

        for g, dqkv in enumerate(_gather_rows((q0, q1, q2), qbuf, ts)):
            for sec in range(3):
                c0 = sec * ATT_W + g * GROUP_W
                dz_ref[:, c0:c0 + GROUP_W] = dqkv[:, sec * GROUP_W:(sec + 1) * GROUP_W].astype(BF16)
        dz_ref[:, 3 * ATT_W:] = dzr_ref[...]
        du = _dot_nt(dz_ref[...], w_ref[...])
        x = x_ref[...]
        r1 = _rms(x)
        n1 = x * r1
        g1 = g_ref[...]
        dg_ref[...] += jnp.sum(du * n1, axis=0, keepdims=True)
        dx_ref[...] = dh_ref[...] + _rms_bwd(du * g1, n1, r1)

    row = lambda w: pl.BlockSpec((ts, w), lambda i: (i, 0))
    vec = pl.BlockSpec((1, D_MODEL), lambda i: (0, 0))
    return pl.pallas_call(
        body, name="in_proj_bwd", grid=(n,),
        in_specs=[row(D_MODEL), row(D_MODEL), row(REST_W)] + _dil_specs(ts, ATT_W, lambda i: i) + [vec, ANY],
        out_specs=[row(D_MODEL), row(N_IN), vec],
        out_shape=[jax.ShapeDtypeStruct((s, D_MODEL), F32), jax.ShapeDtypeStruct((s, N_IN), BF16),
                   jax.ShapeDtypeStruct((1, D_MODEL), F32)],
        scratch_shapes=[pltpu.VMEM((D_MODEL, N_IN), BF16), pltpu.VMEM((ATT_W // LANES, ts, LANES), F32),
                        pltpu.SemaphoreType.DMA((N_DEV,))],
        compiler_params=_params(dimension_semantics=("arbitrary",)),
    )(x, dh1, dzr, *dqkv_dil, g1, w_in)


GRAD_PASS = 2


def _weight_grad(at, b, name, transpose_out, tk=2048, at_slot=None, b_slot=None, ex=None):
    m, s = at.shape[-2:]
    nn = b.shape[-1]
    tn = nn // N_DEV
    tk = min(tk, s)
    nk = s // tk
    npass = N_DEV // GRAD_PASS
    oshape = (tn, m) if transpose_out else (m, tn)
    owner = lambda jj: N_DEV - 1 - jj
    order = jnp.stack([_linear(_peer(_my_place(), owner(jj))) for jj in range(N_DEV)]).astype(jnp.int32)
    sent_in = lambda p: [jj for jj in range(N_DEV - 1) if jj // GRAD_PASS == p]

    nex = ex.nw if ex else 0

    def body(order_ref, at_ref, *refs):
        b_refs, ex_ins, (got_ref,), ex_outs, (acc, res, send_sems, recv_sems, local_sem), ex_sems = _split_refs(
            refs, (GRAD_PASS, nex, 1, nex, 5, 3 if ex else 0))
        j, k = pl.program_id(0), pl.program_id(1)
        me = _my_place()
        mine = _linear(me)
        if ex:
            _carry_start(ex, j * nk + k, ex_ins, ex_outs, ex_sems)

        def send(jj):
            return pltpu.make_async_remote_copy(
                src_ref=res.at[(jj // GRAD_PASS) % 2, jj % GRAD_PASS], dst_ref=got_ref.at[mine],
                send_sem=send_sems.at[jj], recv_sem=recv_sems.at[jj],
                device_id=_peer(me, owner(jj)), device_id_type=pl.DeviceIdType.MESH)

        @pl.when(k == 0)
        def _():
            acc[...] = jnp.zeros_like(acc)

        a = at_ref[...]
        for g in range(GRAD_PASS):
            acc[g] += _dot(a, b_refs[g][...])

        @pl.when(k == nk - 1)
        def _():
            for p in range(2, npass):
                @pl.when(j == p)
                def _():
                    for jj in sent_in(p - 2):
                        send(jj).wait_send()

            for g in range(GRAD_PASS):
                r = acc[g]
                res[j % 2, g] = (r.T if transpose_out else r).astype(BF16)
            for p in range(npass):
                @pl.when(j == p)
                def _():
                    for jj in sent_in(p):
                        send(jj).start()

            @pl.when(j == npass - 1)
            def _():
                own = pltpu.make_async_copy(res.at[(npass - 1) % 2, GRAD_PASS - 1], got_ref.at[mine], local_sem.at[0])
                own.start()
                for p in range(max(npass - 2, 0), npass):
                    for jj in sent_in(p):
                        send(jj).wait_send()
                for jj in range(N_DEV - 1):
                    send(jj).wait_recv()
                own.wait()

        if ex:
            _carry_wait(ex, j * nk + k, npass * nk - 1, ex_ins, ex_outs, ex_sems)

    if at_slot is None:
        at_spec = pl.BlockSpec((m, tk), lambda j, k, o: (0, k))
    else:
        at_spec = pl.BlockSpec((None, m, tk), lambda j, k, o: (at_slot, 0, k))
    if b_slot is None:
        b_spec = lambda g: pl.BlockSpec((tk, tn), lambda j, k, o: (k, o[GRAD_PASS * j + g]))
    else:
        b_spec = lambda g: pl.BlockSpec((None, tk, tn), lambda j, k, o: (b_slot, k, o[GRAD_PASS * j + g]))
    outs = pl.pallas_call(
        body, name=name,
        grid_spec=pltpu.PrefetchScalarGridSpec(
            num_scalar_prefetch=1, grid=(npass, nk),
            in_specs=[at_spec] + [b_spec(g) for g in range(GRAD_PASS)] + (ex.specs if ex else []),
            out_specs=[ANY] + (ex.specs if ex else []),
            scratch_shapes=[pltpu.VMEM((GRAD_PASS, m, tn), F32), pltpu.VMEM((2, GRAD_PASS) + oshape, BF16),
                            pltpu.SemaphoreType.DMA((N_DEV - 1,)), pltpu.SemaphoreType.DMA((N_DEV - 1,)),
                            pltpu.SemaphoreType.DMA((1,))] + (ex.scratch if ex else [])),
        out_shape=[jax.ShapeDtypeStruct((N_DEV,) + oshape, BF16)] + (ex.out_shape if ex else []),
        compiler_params=_params(dimension_semantics=("arbitrary", "arbitrary")),
    )(order, at, *([b] * GRAD_PASS), *(ex.arrays if ex else []))
    return (outs[0], outs[1:]) if ex else outs[0]


def _my_place():
    x, y, c = lax.axis_index("x"), lax.axis_index("y"), lax.axis_index("c")
    return x, y, c


def _peer(place, k):
    x, y, c = place
    return (1 - x if k & 4 else x, 1 - y if k & 2 else y, 1 - c if k & 1 else c)


def _linear(place):
    x, y, c = place
    return 4 * x + 2 * y + c


class _Exchange:
    passes_on = False

    def __init__(self, arrays, gather):
        self.arrays, self.gather, self.nw = list(arrays), list(gather), len(arrays)
        self.out_shape = []
        for a, g in zip(arrays, gather):
            block = a.shape if g else a.shape[1:]
            self.out_shape.append(jax.ShapeDtypeStruct((N_DEV,) + tuple(block), a.dtype))
        self.specs = [ANY] * self.nw
        self.scratch = [pltpu.SemaphoreType.DMA((self.nw, N_DEV - 1)), pltpu.SemaphoreType.DMA((self.nw, N_DEV - 1)),
                        pltpu.SemaphoreType.DMA((self.nw,))]

    def _copies(self, ins, outs, sems):
        send_sems, recv_sems, local_sems = sems
        me = _my_place()
        mine = _linear(me)
        copies = []
        for w in range(self.nw):
            src = ins[w] if self.gather[w] else ins[w].at[mine]
            copies.append(pltpu.make_async_copy(src, outs[w].at[mine], local_sems.at[w]))
        for k in range(1, N_DEV):
            peer = _peer(me, k)
            for w in range(self.nw):
                src = ins[w] if self.gather[w] else ins[w].at[_linear(peer)]
                copies.append(pltpu.make_async_remote_copy(
                    src_ref=src, dst_ref=outs[w].at[mine],
                    send_sem=send_sems.at[w, k - 1], recv_sem=recv_sems.at[w, k - 1],
                    device_id=peer, device_id_type=pl.DeviceIdType.MESH))
        return copies

    def start(self, ins, outs, sems):
        for cp in self._copies(ins, outs, sems):
            cp.start()

    def wait(self, ins, outs, sems):
        copies = self._copies(ins, outs, sems)
        for cp in copies[self.nw:]:
            cp.wait_recv()
        for cp in copies[self.nw:]:
            cp.wait_send()
        for cp in copies[:self.nw]:
            cp.wait()


class _Gather:
    passes_on = True

    def __init__(self, arrays):
        self.arrays, self.nw = list(arrays), len(arrays)
        self.out_shape = [jax.ShapeDtypeStruct((N_DEV,) + tuple(a.shape), a.dtype) for a in arrays]
        self.specs = [ANY] * self.nw
        self.scratch = [pltpu.SemaphoreType.DMA((self.nw, N_DEV - 1)), pltpu.SemaphoreType.DMA((self.nw, N_DEV - 1)),
                        pltpu.SemaphoreType.DMA((self.nw,))]

    @staticmethod
    def _places():
        x, y, c = _my_place()
        return (x, y, c), (x, y, 1 - c), [(1 - x, y), (x, 1 - y), (1 - x, 1 - y)]

    @staticmethod
    def _copy(outs, sems, w, k, block, to, src=None):
        rows = outs[w].at[_linear(block)]
        return pltpu.make_async_remote_copy(
            src_ref=rows if src is None else src, dst_ref=rows, send_sem=sems[0].at[w, k], recv_sem=sems[1].at[w, k],
            device_id=to, device_id_type=pl.DeviceIdType.MESH)

    def _first(self, ins, outs, sems, w):
        me, sibling, chips = self._places()
        return ([self._copy(outs, sems, w, 0, me, sibling, src=ins[w])]
                + [self._copy(outs, sems, w, 1 + j, me, (*chip, me[2]), src=ins[w]) for j, chip in enumerate(chips)])

    def _passed(self, outs, sems, w):
        me, sibling, chips = self._places()
        return [self._copy(outs, sems, w, 4 + j, (*chip, me[2]), sibling) for j, chip in enumerate(chips)]

    def _local(self, ins, outs, sems, w):
        return pltpu.make_async_copy(ins[w], outs[w].at[_linear(self._places()[0])], sems[2].at[w])

    def start(self, ins, outs, sems):
        for w in range(self.nw):
            self._local(ins, outs, sems, w).start()
            for cp in self._first(ins, outs, sems, w):
                cp.start()

    def pass_on(self, ins, outs, sems):
        me, sibling, chips = self._places()
        for j, chip in enumerate(chips):
            for w in range(self.nw):
                self._copy(outs, sems, w, 1 + j, (*chip, me[2]), me).wait_recv()
                self._passed(outs, sems, w)[j].start()

    def wait(self, ins, outs, sems):
        me, sibling, chips = self._places()
        for w in range(self.nw):
            self._copy(outs, sems, w, 0, sibling, me).wait_recv()
            for j, chip in enumerate(chips):
                self._copy(outs, sems, w, 4 + j, (*chip, sibling[2]), me).wait_recv()
            for cp in self._first(ins, outs, sems, w) + self._passed(outs, sems, w):
                cp.wait_send()
            self._local(ins, outs, sems, w).wait()


def _adamw(parts, w, m, v, name, tr):
    rows, cols = w.shape
    tr = min(tr, rows)

    def body(p_ref, w_ref, m_ref, v_ref, g_ref, d_ref, nm_ref, nv_ref):
        g = p_ref[0].astype(F32)
        for j in range(1, N_DEV):
            g = g + p_ref[j].astype(F32)
        nm = ADAM_B1 * m_ref[...] + (1.0 - ADAM_B1) * g
        nv = ADAM_B2 * v_ref[...] + (1.0 - ADAM_B2) * (g * g)
        m_hat = nm / (1.0 - ADAM_B1 ** ADAM_STEP)
        v_hat = nv / (1.0 - ADAM_B2 ** ADAM_STEP)
        g_ref[...] = g
        d_ref[...] = -ADAM_LR * (m_hat / (jnp.sqrt(v_hat) + ADAM_EPS) + ADAM_WD * w_ref[...])
        nm_ref[...] = nm
        nv_ref[...] = nv

    blk = pl.BlockSpec((tr, cols), lambda i: (i, 0))
    return pl.pallas_call(
        body, name=name, grid=(rows // tr,),
        in_specs=[pl.BlockSpec((N_DEV, tr, cols), lambda i: (0, i, 0)), blk, blk, blk],
        out_specs=[blk] * 4,
        out_shape=[jax.ShapeDtypeStruct((rows, cols), F32)] * 4,
        compiler_params=_params(dimension_semantics=("arbitrary",)),
    )(parts, w, m, v)


def _pack_small(values):
    rows = []
    for v in values:
        size = math.prod(v.shape)
        padded = -(-size // 1024) * 1024
        flat = jnp.pad(v.reshape(-1).astype(F32), (0, padded - size))
        rows.append(flat.reshape(padded // 128, 128))
    return jnp.concatenate(rows, axis=0) if len(rows) > 1 else rows[0]


def _unpack_small(packed, shapes):
    out, r = [], 0
    for shape in shapes:
        size = math.prod(shape)
        nrow = -(-size // 1024) * 8
        out.append(packed[r:r + nrow].reshape(-1)[:size].reshape(shape))
        r += nrow
    return out


def kernel(x, norm_mix_g, w_in, w_att_out, w_pool_grp, pool_scale, w_pool_out, w_out, norm_mlp_g, w_mlp_in, w_mlp_out, norm_final_g, loss_target, m_norm_mix_g, m_w_in, m_w_att_out, m_w_pool_grp, m_pool_scale, m_w_pool_out, m_w_out, m_norm_mlp_g, m_w_mlp_in, m_w_mlp_out, m_norm_final_g, v_norm_mix_g, v_w_in, v_w_att_out, v_w_pool_grp, v_pool_scale, v_w_pool_out, v_w_out, v_norm_mlp_g, v_w_mlp_in, v_w_mlp_out, v_norm_final_g):
    x, tgt = x[0], loss_target[0]
    s = x.shape[0]
    g1, g2, g3 = norm_mix_g, norm_mlp_g, norm_final_g.reshape(1, D_MODEL)
    shards = [w_in[0], w_att_out[0], w_pool_out[0], w_out[0], w_mlp_in[0], w_mlp_out[0]]
    wire = [a.astype(BF16) for a in shards]
    cols = lambda a: jnp.transpose(a, (1, 0, 2)).reshape(a.shape[1], N_DEV * a.shape[2])
    rows = lambda a: a.reshape(N_DEV * a.shape[1], a.shape[2])
    blocks_of_cols = lambda a: jnp.transpose(a.reshape(a.shape[0], N_DEV, a.shape[1] // N_DEV), (1, 0, 2))
    blocks_of_rows = lambda a: a.reshape(N_DEV, a.shape[0] // N_DEV, a.shape[1])
    wbd = _window_weights(w_pool_grp[0]).astype(BF16)

    (u, ut), (f_in,) = _rms_u(x, g1, 1024, _Gather(wire[:1]))
    (qkv0, qkv1, qkv2, zr), later = _in_proj_fwd(u, f_in, 1024, _Gather(wire[1:]))
    f_ao, f_po, f_out, f_mi, f_mo = cols(later[0]), cols(later[1]), rows(later[2]), later[3], rows(later[4])
    qkv_dil = (qkv0, qkv1, qkv2)
    flat = lambda a: a.reshape(s, a.shape[-1])
    shaped = lambda a, g: a if g == 0 else a.reshape(ATT_GROUPS[g][1], s // ATT_GROUPS[g][1], a.shape[-1])
    ol_dil = [shaped(_attn_fwd(flat(qkv_dil[g]), g), g) for g in range(N_GROUPS)]
    h1 = _mix_fwd(x, zr, ol_dil, wbd, pool_scale, f_ao, f_po, f_out, ts=512)

    dh1, mdt, hd, stats = _mlp_fwd_bwd(h1, tgt, jnp.concatenate([g2, g3], axis=0), f_mi, f_mo, ts=256)
    loss, dg2, dg3 = stats[0:1, 0:1], stats[1:2], stats[2:3]
    got = {"w_mlp_in": _weight_grad(mdt, hd, "grad_w_mlp_in", transpose_out=False, at_slot=0, b_slot=1),
           "w_mlp_out": _weight_grad(mdt, hd, "grad_w_mlp_out", transpose_out=True, at_slot=1, b_slot=0)}
    dzr, dc_dil, (g_out, g_ao, g_po, g_bd, g_scale) = _mix_bwd(
        dh1, zr, ol_dil, wbd, pool_scale, f_ao, f_po, f_out, ts=256)
    g_grp = jnp.stack([g_bd[g, off:off + POOL_GW, off:off + POOL_GW] for g, (_, off) in enumerate(GROUP_WIN)])
    early = [_Exchange([blocks_of_rows(g_out).astype(BF16)], [False]),
             _Exchange([blocks_of_cols(g_ao).astype(BF16), blocks_of_cols(g_po).astype(BF16)], [False, False]),
             _Exchange([_pack_small([g_grp])], [True])]
    dqkv_dil, arrived = [], []
    for g in range(N_GROUPS):
        dqkv, rode = _attn_bwd(flat(qkv_dil[g]), flat(dc_dil[g]), g, early[g])
        dqkv_dil.append(shaped(dqkv, g))
        arrived += list(rode)
    got["w_out"], got["w_att_out"], got["w_pool_out"], got_grp = arrived
    dx, dz, dg1 = _in_proj_bwd(x, dh1, dzr, dqkv_dil, g1, f_in, ts=512)
    vectors = _Exchange([_pack_small([loss, dg1, g_scale, dg2, dg3])], [True])
    got["w_in"], (got_vec,) = _weight_grad(ut, dz, "grad_w_in", transpose_out=False, ex=vectors)

    names = ["w_in", "w_att_out", "w_pool_out", "w_out", "w_mlp_in", "w_mlp_out"]
    ms = [m_w_in, m_w_att_out, m_w_pool_out, m_w_out, m_w_mlp_in, m_w_mlp_out]
    vs = [v_w_in, v_w_att_out, v_w_pool_out, v_w_out, v_w_mlp_in, v_w_mlp_out]
    upd = {}
    for k, name in enumerate(names):
        res = _adamw(got[name], shards[k], ms[k][0], vs[k][0], "adamw_" + name, tr=256)
        upd[name] = [a[None] for a in res]

    res = _adamw(got_grp, _pack_small([w_pool_grp]), _pack_small([m_w_pool_grp]), _pack_small([v_w_pool_grp]),
                 "adamw_w_pool_grp", tr=2048)
    upd["w_pool_grp"] = [_unpack_small(a, [w_pool_grp.shape])[0] for a in res]
    vec_w = [jnp.zeros((1,), F32), norm_mix_g, pool_scale, norm_mlp_g, norm_final_g]
    vec_m = [jnp.zeros((1,), F32), m_norm_mix_g, m_pool_scale, m_norm_mlp_g, m_norm_final_g]
    vec_v = [jnp.ones((1,), F32), v_norm_mix_g, v_pool_scale, v_norm_mlp_g, v_norm_final_g]
    res = _adamw(got_vec, _pack_small(vec_w), _pack_small(vec_m), _pack_small(vec_v), "adamw_vectors", tr=2048)
    shapes = [(), norm_mix_g.shape, pool_scale.shape, norm_mlp_g.shape, norm_final_g.shape]
    unpacked = [_unpack_small(a, shapes) for a in res]
    for k, name in enumerate(["loss", "norm_mix_g", "pool_scale", "norm_mlp_g", "norm_final_g"]):
        upd[name] = [unpacked[q][k] for q in range(4)]

    order = ["norm_mix_g", "w_in", "w_att_out", "w_pool_grp", "pool_scale", "w_pool_out", "w_out", "norm_mlp_g",
             "w_mlp_in", "w_mlp_out", "norm_final_g"]
    out = [upd["loss"][0], dx[None]]
    for q in range(4):
        out += [upd[name][q] for name in order]
    return tuple(out)
```

```python
import jax
import jax.numpy as jnp
from jax import lax
from jax.experimental import pallas as pl
from jax.experimental.pallas import tpu as pltpu

F32 = jnp.float32
BF16 = jnp.bfloat16

D_MODEL = 1024
HEAD_DIM = 64
GROUP_W = 256
ATT_GROUPS = ((128, 1), (512, 4), (2048, 16))
N_GROUPS = 3
BLK = 128
ATT_W = 768
POOL_W = 768
POOL_GW = 192
D_FF = 4096
N_IN = 5120
REST_W = N_IN - 3 * ATT_W
NORM_EPS = 1e-6
ALIBI_MAX_BIAS = 8.0
N_DEV = 8
HALO = 32

ADAM_LR = 0.001
ADAM_B1 = 0.9
ADAM_B2 = 0.999
ADAM_EPS = 1e-08
ADAM_WD = 0.01
ADAM_STEP = 10

VMEM_LIMIT = 56 * 1024 * 1024
ANY = pl.BlockSpec(memory_space=pl.ANY)


def _params(**kw):
    return pltpu.CompilerParams(vmem_limit_bytes=VMEM_LIMIT, **kw)


def _dot(a, b):
    return jnp.dot(a, b, preferred_element_type=F32)


def _dot_nt(a, b):
    return lax.dot_general(a, b, (((1,), (1,)), ((), ())), preferred_element_type=F32)


def _dot_tn(a, b):
    return lax.dot_general(a, b, (((0,), (0,)), ((), ())), preferred_element_type=F32)


def _slope(head):
    return 2.0 ** (-ALIBI_MAX_BIAS * (head + 1.0) / 12.0)


def _load_resident(step, pairs, sem):
    @pl.when(step == 0)
    def _():
        copies = [pltpu.make_async_copy(src, dst, sem.at[n]) for n, (src, dst) in enumerate(pairs)]
        for cp in copies:
            cp.start()
        for cp in copies:
            cp.wait()


LANES = 128


def _to_class_order(value, dil, buf, ref, col0):
    ts, w = value.shape
    if dil == 1:
        ref[:, col0:col0 + w] = value.astype(ref.dtype)
        return
    for c in range(w // LANES):
        buf[c] = value[:, c * LANES:(c + 1) * LANES]
        for r in range(dil):
            ref[r, :, col0 + c * LANES:col0 + (c + 1) * LANES] = (
                buf[c, pl.ds(r, ts // dil, stride=dil), :].astype(ref.dtype))


def _to_token_order(ref, dil, buf, ts):
    if dil == 1:
        return ref[...].astype(F32)
    w = ref.shape[-1]
    for c in range(w // LANES):
        for r in range(dil):
            buf[c, pl.ds(r, ts // dil, stride=dil), :] = ref[r, :, c * LANES:(c + 1) * LANES].astype(F32)
    return jnp.concatenate([buf[c] for c in range(w // LANES)], axis=1)


def _column_blocks(stacked, dst):
    tn = stacked.shape[2]
    return [(stacked.at[j], dst.at[:, pl.ds(j * tn, tn)]) for j in range(N_DEV)]


def _sigmoid(x):
    return 0.5 * jnp.tanh(0.5 * x) + 0.5


def _rms(x):
    return lax.rsqrt(jnp.mean(x * x, axis=-1, keepdims=True) + NORM_EPS)


def _rms_bwd(dn, n, r):
    return r * (dn - n * jnp.mean(dn * n, axis=-1, keepdims=True))


def _split_refs(refs, counts):
    out, at = [], 0
    for c in counts:
        out.append(refs[at:at + c])
        at += c
    return out


def _carry_start(ex, step, ins, outs, sems):
    @pl.when(step == 0)
    def _():
        ex.start(ins, outs, sems)


def _carry_wait(ex, step, last, ins, outs, sems, pass_at=None):
    if ex.passes_on:
        @pl.when(step == (last if pass_at is None else pass_at))
        def _():
            ex.pass_on(ins, outs, sems)

    @pl.when(step == last)
    def _():
        ex.wait(ins, outs, sems)


def _rms_u(x, g1, ts, ex):
    s = x.shape[0]
    n = s // ts

    def body(*refs):
        (x_ref, g_ref), ex_ins, (u_ref, ut_ref), ex_outs, ex_sems = _split_refs(refs, (2, ex.nw, 2, ex.nw, 3))
        i = pl.program_id(0)
        _carry_start(ex, i, ex_ins, ex_outs, ex_sems)
        x = x_ref[...]
        u = x * _rms(x) * g_ref[...]
        u_ref[...] = u.astype(BF16)
        ut_ref[...] = u.T.astype(BF16)
        _carry_wait(ex, i, n - 1, ex_ins, ex_outs, ex_sems)

    outs = pl.pallas_call(
        body, name="rms_u", grid=(n,),
        in_specs=[pl.BlockSpec((ts, D_MODEL), lambda i: (i, 0)), pl.BlockSpec((1, D_MODEL), lambda i: (0, 0))]
                 + ex.specs,
        out_specs=[pl.BlockSpec((ts, D_MODEL), lambda i: (i, 0)), pl.BlockSpec((D_MODEL, ts), lambda i: (0, i))]
                  + ex.specs,
        out_shape=[jax.ShapeDtypeStruct((s, D_MODEL), BF16), jax.ShapeDtypeStruct((D_MODEL, s), BF16)] + ex.out_shape,
        scratch_shapes=ex.scratch,
        compiler_params=_params(dimension_semantics=("arbitrary",)),
    )(x, g1, *ex.arrays)
    return outs[:2], outs[2:]


def _in_proj_fwd(u, w_in, ts, ex):
    s = u.shape[0]
    n = s // ts
    dils = [d for _, d in ATT_GROUPS]

    def body(*refs):
        (u_ref, w_hbm), ex_ins, (q0_ref, q1_ref, q2_ref, zr_ref), ex_outs, (w_ref, zbuf, sem), ex_sems = (
            _split_refs(refs, (2, ex.nw, 4, ex.nw, 3, 3)))
        i = pl.program_id(0)
        _carry_start(ex, i, ex_ins, ex_outs, ex_sems)
        _load_resident(i, _column_blocks(w_hbm, w_ref), sem)
        ub = u_ref[...]
        outs = (q0_ref, q1_ref, q2_ref)
        for sec in range(3):
            for g in range(N_GROUPS):
                c0 = sec * ATT_W + g * GROUP_W
                zc = _dot(ub, w_ref[:, c0:c0 + GROUP_W])
                _to_class_order(zc, dils[g], zbuf, outs[g], sec * GROUP_W)
        for c0 in range(0, REST_W, 256):
            zr_ref[:, c0:c0 + 256] = _dot(ub, w_ref[:, 3 * ATT_W + c0:3 * ATT_W + c0 + 256]).astype(BF16)
        _carry_wait(ex, i, n - 1, ex_ins, ex_outs, ex_sems, pass_at=max(n - 3, 0))

    outs = pl.pallas_call(
        body, name="in_proj_fwd", grid=(n,),
        in_specs=[pl.BlockSpec((ts, D_MODEL), lambda i: (i, 0)), ANY] + ex.specs,
        out_specs=[pl.BlockSpec((ts, ATT_W), lambda i: (i, 0)),
                   pl.BlockSpec((4, ts // 4, ATT_W), lambda i: (0, i, 0)),
                   pl.BlockSpec((16, ts // 16, ATT_W), lambda i: (0, i, 0)),
                   pl.BlockSpec((ts, REST_W), lambda i: (i, 0))] + ex.specs,
        out_shape=[jax.ShapeDtypeStruct((s, ATT_W), BF16),
                   jax.ShapeDtypeStruct((4, s // 4, ATT_W), BF16),
                   jax.ShapeDtypeStruct((16, s // 16, ATT_W), BF16),
                   jax.ShapeDtypeStruct((s, REST_W), BF16)] + ex.out_shape,
        scratch_shapes=[pltpu.VMEM((D_MODEL, N_IN), BF16), pltpu.VMEM((GROUP_W // LANES, ts, LANES), F32),
                        pltpu.SemaphoreType.DMA((N_DEV,))] + ex.scratch,
        compiler_params=_params(dimension_semantics=("arbitrary",)),
    )(u, w_in, *ex.arrays)
    return outs[:4], outs[4:]


ATT_TILE = 8 * BLK


HEADS = GROUP_W // HEAD_DIM
STACK = HEADS * BLK


SCORE_SCALE = HEAD_DIM ** -0.5


def _band_consts(group, dil):
    row = lax.broadcasted_iota(jnp.int32, (STACK, 2 * BLK), 0)
    kj = lax.broadcasted_iota(jnp.int32, (STACK, 2 * BLK), 1)
    head = row // BLK
    steps = BLK + (row % BLK) - kj
    slope = jnp.full((STACK, 2 * BLK), _slope(4 * group + HEADS - 1), F32)
    for h in range(HEADS - 1):
        slope = jnp.where(head == h, _slope(4 * group + h), slope)
    in_band = (steps >= 0) & (steps <= BLK)
    return jnp.where(in_band, slope * (steps.astype(F32) * float(dil)), jnp.inf), kj


def _first_key(block, nbc, nb, b):
    if nbc % nb == 0 and b != 0:
        return None
    return jnp.where((block % nbc) != 0, 0, BLK)


def _head_of_col():
    return lax.broadcasted_iota(jnp.int32, (BLK, GROUP_W), 1) // HEAD_DIM


def _stack_heads(xb):
    head_of = _head_of_col()
    return jnp.concatenate([jnp.where(head_of == h, xb, jnp.zeros_like(xb)) for h in range(HEADS)], axis=0)


def _per_head_cols(stacked, rhs, scale_rows=None):
    lane = lax.broadcasted_iota(jnp.int32, (BLK, LANES), 1)
    halves = []
    for pair in range(HEADS // 2):
        tile = rhs[:, pair * LANES:(pair + 1) * LANES]
        parts = []
        for h in (2 * pair, 2 * pair + 1):
            part = _dot(stacked[h * BLK:(h + 1) * BLK], tile)
            parts.append(part if scale_rows is None else part * scale_rows[h * BLK:(h + 1) * BLK])
        halves.append(jnp.where(lane < HEAD_DIM, parts[0], parts[1]))
    return jnp.concatenate(halves, axis=1)


def _per_head_rows(col):
    lane = lax.broadcasted_iota(jnp.int32, (BLK, LANES), 1)
    return jnp.concatenate(
        [jnp.where(lane < HEAD_DIM, col[(2 * pair) * BLK:(2 * pair + 1) * BLK], col[(2 * pair + 1) * BLK:(2 * pair + 2) * BLK])
         for pair in range(HEADS // 2)], axis=1)


def _band_softmax(qs, kb, penalty, kj, first_key):
    sc = _dot_nt(qs, kb) - penalty
    if first_key is not None:
        sc = jnp.where(kj >= first_key, sc, -jnp.inf)
    mx = jnp.max(sc, axis=1, keepdims=True)
    e = jnp.exp(sc - mx)
    return e, mx, jnp.sum(e, axis=1, keepdims=True)


def _attn_fwd(qkv, group):
    s = qkv.shape[0]
    dil = ATT_GROUPS[group][1]
    nbc = s // (BLK * dil)
    nb = ATT_TILE // BLK

    def body(q_ref, kc_ref, kp_ref, vc_ref, vp_ref, ol_ref, kbuf, vbuf):
        i = pl.program_id(0)
        kbuf[0:BLK] = kp_ref[...]
        kbuf[BLK:BLK + ATT_TILE] = kc_ref[...]
        vbuf[0:BLK] = vp_ref[...]
        vbuf[BLK:BLK + ATT_TILE] = vc_ref[...]
        penalty, kj = _band_consts(group, dil)
        blocks = range(nb)
        qss = [_stack_heads(q_ref[b * BLK:(b + 1) * BLK, :] * SCORE_SCALE) for b in blocks]
        soft = [_band_softmax(qss[b], kbuf[b * BLK:b * BLK + 2 * BLK, :], penalty, kj,
                              _first_key(i * nb + b, nbc, nb, b)) for b in blocks]
        for b in blocks:
            e, mx, den = soft[b]
            ol_ref[b * BLK:(b + 1) * BLK, 0:GROUP_W] = _per_head_cols(
                e.astype(BF16), vbuf[b * BLK:b * BLK + 2 * BLK, :], 1.0 / den)
        for b in blocks:
            e, mx, den = soft[b]
            ol_ref[b * BLK:(b + 1) * BLK, GROUP_W:] = _per_head_rows(mx + jnp.log(den))

    n = s // ATT_TILE
    cur = lambda c: pl.BlockSpec((ATT_TILE, GROUP_W), lambda i: (i, c))
    prev = lambda c: pl.BlockSpec((BLK, GROUP_W), lambda i: (jnp.maximum(i * nb - 1, 0), c))
    return pl.pallas_call(
        body, name=f"attn_fwd_g{group}", grid=(n,),
        in_specs=[cur(0), cur(1), prev(1), cur(2), prev(2)],
        out_specs=pl.BlockSpec((ATT_TILE, 2 * GROUP_W), lambda i: (i, 0)),
        out_shape=jax.ShapeDtypeStruct((s, 2 * GROUP_W), F32),
        scratch_shapes=[pltpu.VMEM((BLK + ATT_TILE, GROUP_W), BF16), pltpu.VMEM((BLK + ATT_TILE, GROUP_W), BF16)],
        compiler_params=_params(dimension_semantics=("arbitrary",)),
    )(qkv, qkv, qkv, qkv, qkv)


def _attn_bwd(qkv, dc, group, ex=None):
    s = qkv.shape[0]
    dil = ATT_GROUPS[group][1]
    nbc = s // (BLK * dil)
    nb = ATT_TILE // BLK
    n = s // ATT_TILE
    nex = ex.nw if ex else 0

    def body(*refs):
        ((q_ref, kc_ref, kp_ref, vc_ref, vp_ref, do_ref, c_ref), ex_ins, (out_ref,), ex_outs,
         (kbuf, vbuf, dqpend, dkpend, dvpend), ex_sems) = _split_refs(refs, (7, nex, 1, nex, 5, 3 if ex else 0))
        i = pl.program_id(0)
        if ex:
            _carry_start(ex, i, ex_ins, ex_outs, ex_sems)

        @pl.when(i == 0)
        def _():
            dqpend[...] = jnp.zeros_like(dqpend)
            dkpend[...] = jnp.zeros_like(dkpend)
            dvpend[...] = jnp.zeros_like(dvpend)

        out_ref[:, 0:GROUP_W] = dqpend[...]
        body_rows = slice(0, ATT_TILE - BLK)
        tail = slice(ATT_TILE - BLK, ATT_TILE)
        pends = ((dkpend, GROUP_W), (dvpend, 2 * GROUP_W))
        for pend, c0 in pends:
            out_ref[body_rows, c0:c0 + GROUP_W] = pend[body_rows, :].astype(BF16)

        @pl.when(i < n)
        def _():
            kbuf[0:BLK] = kp_ref[...]
            kbuf[BLK:BLK + ATT_TILE] = kc_ref[...]
            vbuf[0:BLK] = vp_ref[...]
            vbuf[BLK:BLK + ATT_TILE] = vc_ref[...]
            penalty, kj = _band_consts(group, dil)
            head_of = _head_of_col()
            blocks = range(nb)
            rows = [slice(b * BLK, (b + 1) * BLK) for b in blocks]
            kbs = [kbuf[b * BLK:b * BLK + 2 * BLK, :] for b in blocks]
            vbs = [vbuf[b * BLK:b * BLK + 2 * BLK, :] for b in blocks]
            qss = [_stack_heads(q_ref[rows[b], :] * SCORE_SCALE) for b in blocks]
            doss = [_stack_heads(do_ref[rows[b], :].astype(BF16)) for b in blocks]
            cors = []
            for b in blocks:
                cb = c_ref[rows[b], :]
                cors.append(jnp.concatenate(
                    [jnp.max(jnp.where(head_of == h, cb, -jnp.inf), axis=1, keepdims=True) for h in range(HEADS)],
                    axis=0))
            soft = [_band_softmax(qss[b], kbs[b], penalty, kj, _first_key(i * nb + b, nbc, nb, b)) for b in blocks]
            dps = [_dot_nt(doss[b], vbs[b]) for b in blocks]
            ps = [soft[b][0] * (1.0 / soft[b][2]) for b in blocks]
            dss = [(ps[b] * (dps[b] + cors[b])).astype(BF16) for b in blocks]
            for b in blocks:
                dqpend[rows[b], :] = _per_head_cols(dss[b], kbs[b] * SCORE_SCALE).astype(BF16)
            bands = [(_dot_tn(dss[b], qss[b]), _dot_tn(ps[b].astype(BF16), doss[b])) for b in blocks]
            for which, (pend, c0) in enumerate(pends):
                out_ref[tail, c0:c0 + GROUP_W] = (pend[tail, :] + bands[0][which][0:BLK]).astype(BF16)
                for b in range(nb):
                    own = bands[b][which][BLK:2 * BLK]
                    pend[b * BLK:(b + 1) * BLK, :] = own + bands[b + 1][which][0:BLK] if b + 1 < nb else own

        @pl.when(i == n)
        def _():
            for pend, c0 in pends:
                out_ref[tail, c0:c0 + GROUP_W] = pend[tail, :].astype(BF16)

        if ex:
            _carry_wait(ex, i, n, ex_ins, ex_outs, ex_sems)

    last = n - 1
    cur = lambda c: pl.BlockSpec((ATT_TILE, GROUP_W), lambda i: (jnp.minimum(i, last), c))
    prev = lambda c: pl.BlockSpec(
        (BLK, GROUP_W), lambda i: (jnp.maximum(jnp.minimum(i, last) * nb - 1, 0), c))
    outs = pl.pallas_call(
        body, name=f"attn_bwd_g{group}", grid=(n + 1,),
        in_specs=[cur(0), cur(1), prev(1), cur(2), prev(2), cur(0), cur(1)] + (ex.specs if ex else []),
        out_specs=[pl.BlockSpec((ATT_TILE, ATT_W), lambda i: (jnp.maximum(i - 1, 0), 0))] + (ex.specs if ex else []),
        out_shape=[jax.ShapeDtypeStruct((s, ATT_W), BF16)] + (ex.out_shape if ex else []),
        scratch_shapes=[pltpu.VMEM((BLK + ATT_TILE, GROUP_W), BF16), pltpu.VMEM((BLK + ATT_TILE, GROUP_W), BF16),
                        pltpu.VMEM((ATT_TILE, GROUP_W), BF16),
                        pltpu.VMEM((ATT_TILE, GROUP_W), F32), pltpu.VMEM((ATT_TILE, GROUP_W), F32)]
                       + (ex.scratch if ex else []),
        compiler_params=_params(dimension_semantics=("arbitrary",)),
    )(qkv, qkv, qkv, qkv, qkv, dc, dc, *(ex.arrays if ex else []))
    return (outs[0], outs[1:]) if ex else outs[0]


def _dil_specs(ts, width, idx):
    return [pl.BlockSpec((ts, width), lambda i: (idx(i), 0)),
            pl.BlockSpec((4, ts // 4, width), lambda i: (0, idx(i), 0)),
            pl.BlockSpec((16, ts // 16, width), lambda i: (0, idx(i), 0))]


def _gather_rows(refs, buf, ts):
    return [_to_token_order(refs[g], ATT_GROUPS[g][1], buf, ts) for g in range(N_GROUPS)]


def _inverse_counts(ts):
    col = jnp.arange(POOL_W)
    win = jnp.where(col < POOL_GW, 2, jnp.where(col < 2 * POOL_GW, 4, jnp.where(col < 3 * POOL_GW, 8, 16)))
    first = jnp.minimum(jnp.arange(ts)[:, None] + 1, win[None, :])
    return 1.0 / jnp.stack([first, jnp.broadcast_to(win[None, :], (ts, POOL_W))]).astype(F32)


def _pool_fwd(ebuf, s2, s4, s8, inv_count, ts):
    n = ts + HALO
    s2[8:n] = ebuf[8:n] + ebuf[7:n - 1]
    s4[16:n] = s2[16:n] + s2[14:n - 2]
    s8[24:n] = s4[24:n] + s4[20:n - 4]
    s16 = s8[32:n] + s8[24:n - 8]
    col = lax.broadcasted_iota(jnp.int32, (ts, POOL_W), 1)
    psum = jnp.where(col < POOL_GW, s2[32:n],
                     jnp.where(col < 2 * POOL_GW, s4[32:n], jnp.where(col < 3 * POOL_GW, s8[32:n], s16)))
    return psum * inv_count - ebuf[32:n]


GROUP_WIN = ((0, 0), (128, 64), (384, 0), (512, 64))
WIN = 256


def _window_weights(w_grp):
    return jnp.stack([jnp.pad(w_grp[g], ((off, WIN - off - POOL_GW), (off, WIN - off - POOL_GW)))
                      for g, (_, off) in enumerate(GROUP_WIN)])


def _group_matmul(xb, wwin_ref, transposed=False):
    outs = []
    for g, (start, _) in enumerate(GROUP_WIN):
        xw = xb[:, start:start + WIN]
        outs.append(_dot_nt(xw, wwin_ref[g]) if transposed else _dot(xw, wwin_ref[g]))
    half = WIN // 2
    return jnp.concatenate([outs[0][:, :half], outs[0][:, half:] + outs[1][:, :half], outs[1][:, half:],
                            outs[2][:, :half], outs[2][:, half:] + outs[3][:, :half], outs[3][:, half:]], axis=1)


def _mix_core(zr, pooled, outs, lses, wwin_ref, scale, wao, wpo):
    mixed = _group_matmul(pooled.astype(BF16), wwin_ref)
    p = mixed * scale
    l0, l1, l2 = lses
    mx = jnp.maximum(jnp.maximum(l0, l1), l2)
    e0, e1, e2 = jnp.exp(l0 - mx), jnp.exp(l1 - mx), jnp.exp(l2 - mx)
    inv = 1.0 / (e0 + e1 + e2)
    wts = (e0 * inv, e1 * inv, e2 * inv)
    a = wts[0] * outs[0] + wts[1] * outs[1] + wts[2] * outs[2]
    att = _dot(a.astype(BF16), wao)
    pol = _dot(p.astype(BF16), wpo)
    sga = _sigmoid(zr[:, POOL_W:POOL_W + D_MODEL].astype(F32))
    sgp = _sigmoid(zr[:, POOL_W + D_MODEL:].astype(F32))
    mg = sga * att + sgp * pol
    return dict(mixed=mixed, p=p, wts=wts, a=a, att=att, pol=pol, sga=sga, sgp=sgp, mg=mg)


def _fill_pool_input(ebuf, zr_ref, halo_ref, t0):
    ts = zr_ref.shape[0]
    halo = halo_ref[...].astype(F32)
    t = t0 - HALO + lax.broadcasted_iota(jnp.int32, (HALO, POOL_W), 0)
    ebuf[0:HALO] = jnp.where(t >= 0, halo, 0.0)
    ebuf[HALO:HALO + ts] = zr_ref[:, 0:POOL_W].astype(F32)


def _mix_fwd(x, zr, ol_dil, wbd, scale, wao, wpo, wout, ts):
    s = x.shape[0]
    n = s // ts

    def body(x_ref, zr_ref, halo_ref, ic_ref, ol0, ol1, ol2, wbd_ref, sc_ref, wao_ref, wpo_ref, wout_ref,
             h1_ref, ebuf, s2, s4, s8, rbuf):
        i = pl.program_id(0)
        _fill_pool_input(ebuf, zr_ref, halo_ref, i * ts)
        pooled = _pool_fwd(ebuf, s2, s4, s8, ic_ref[...], ts)
        ols = _gather_rows((ol0, ol1, ol2), rbuf, ts)
        outs, lses = [a[:, :GROUP_W] for a in ols], [a[:, GROUP_W:] for a in ols]
        f = _mix_core(zr_ref[...], pooled, outs, lses, wbd_ref, sc_ref[...], wao_ref[...], wpo_ref[...])
        h1_ref[...] = x_ref[...] + _dot(f["mg"].astype(BF16), wout_ref[...])

    whole = lambda a: pl.BlockSpec(a.shape, lambda i: (0,) * a.ndim)
    idx = lambda i: i
    return pl.pallas_call(
        body, name="mix_fwd", grid=(n,),
        in_specs=[pl.BlockSpec((ts, D_MODEL), lambda i: (i, 0)),
                  pl.BlockSpec((ts, REST_W), lambda i: (i, 0)),
                  pl.BlockSpec((HALO, POOL_W), lambda i: (jnp.maximum(i * (ts // HALO) - 1, 0), 0)),
                  pl.BlockSpec((None, ts, POOL_W), lambda i: (jnp.minimum(i, 1), 0, 0))]
                 + _dil_specs(ts, 2 * GROUP_W, idx)
                 + [whole(wbd), whole(scale), whole(wao), whole(wpo), whole(wout)],
        out_specs=pl.BlockSpec((ts, D_MODEL), lambda i: (i, 0)),
        out_shape=jax.ShapeDtypeStruct((s, D_MODEL), F32),
        scratch_shapes=[pltpu.VMEM((ts + HALO, POOL_W), F32)] * 4
                       + [pltpu.VMEM((2 * GROUP_W // LANES, ts, LANES), F32)],
        compiler_params=_params(dimension_semantics=("arbitrary",)),
    )(x, zr, zr, _inverse_counts(ts), *ol_dil, wbd, scale, wao, wpo, wout)


def _mix_bwd(dh1, zr, ol_dil, wbd, scale, wao, wpo, wout, ts):
    s = dh1.shape[0]
    n = s // ts

    def body(dh_ref, zr_ref, halo_ref, ic_ref, ol0, ol1, ol2, sc_ref, wbd_hbm, wao_hbm, wpo_hbm, wout_hbm,
             dzr_ref, dc0, dc1, dc2, gsc_ref, gwout_hbm, gwao_hbm, gwpo_hbm, gwbd_hbm,
             ebuf, s2, s4, s8, gbuf, t2, t4, t8, rbuf,
             wbd_ref, wao_ref, wpo_ref, wout_ref, gwout_ref, gwao_ref, gwpo_ref, gwbd_ref, sem):
        j = pl.program_id(0)
        i = n - 1 - j
        _load_resident(j, [(wbd_hbm, wbd_ref), (wao_hbm, wao_ref), (wpo_hbm, wpo_ref), (wout_hbm, wout_ref)], sem)

        @pl.when(j == 0)
        def _():
            gwout_ref[...] = jnp.zeros_like(gwout_ref)
            gwao_ref[...] = jnp.zeros_like(gwao_ref)
            gwpo_ref[...] = jnp.zeros_like(gwpo_ref)
            gwbd_ref[...] = jnp.zeros_like(gwbd_ref)
            gsc_ref[...] = jnp.zeros_like(gsc_ref)
            gbuf[ts:ts + HALO] = jnp.zeros((HALO, POOL_W), F32)

        _fill_pool_input(ebuf, zr_ref, halo_ref, i * ts)
        inv_count = ic_ref[...]
        pooled = _pool_fwd(ebuf, s2, s4, s8, inv_count, ts)
        ols = _gather_rows((ol0, ol1, ol2), rbuf, ts)
        outs, lses = [a[:, :GROUP_W] for a in ols], [a[:, GROUP_W:] for a in ols]
        zr = zr_ref[...]
        wao, wpo, wout = wao_ref[...], wpo_ref[...], wout_ref[...]
        scale = sc_ref[...]
        f = _mix_core(zr, pooled, outs, lses, wbd_ref, scale, wao, wpo)

        dhb = dh_ref[...].astype(BF16)
        gwout_ref[...] += _dot(f["mg"].T.astype(BF16), dhb)
        dmg = _dot_nt(dhb, wout)
        sga, sgp, att, pol = f["sga"], f["sgp"], f["att"], f["pol"]
        datt = dmg * sga
        dpol = dmg * sgp
        dzr_ref[:, POOL_W:POOL_W + D_MODEL] = (dmg * att * sga * (1.0 - sga)).astype(BF16)
        dzr_ref[:, POOL_W + D_MODEL:] = (dmg * pol * sgp * (1.0 - sgp)).astype(BF16)
        dattb = datt.astype(BF16)
        dpolb = dpol.astype(BF16)
        gwao_ref[...] += _dot(f["a"].T.astype(BF16), dattb)
        gwpo_ref[...] += _dot(f["p"].T.astype(BF16), dpolb)
        da = _dot_nt(dattb, wao)
        dp = _dot_nt(dpolb, wpo)

        gsc_ref[...] += jnp.sum(f["mixed"] * dp, axis=0, keepdims=True)
        dmixed = (dp * scale).astype(BF16)
        pooled_t = pooled.T.astype(BF16)
        for g, (start, _) in enumerate(GROUP_WIN):
            gwbd_ref[g] += _dot(pooled_t[start:start + WIN, :], dmixed[:, start:start + WIN])
        dpooled = _group_matmul(dmixed, wbd_ref, transposed=True)
        gbuf[0:ts] = dpooled * inv_count
        m = ts + HALO
        t2[0:m - 8] = gbuf[0:m - 8] + gbuf[1:m - 7]
        t4[0:m - 16] = t2[0:m - 16] + t2[2:m - 14]
        t8[0:m - 24] = t4[0:m - 24] + t4[4:m - 20]
        t16 = t8[0:ts] + t8[8:ts + 8]
        col = lax.broadcasted_iota(jnp.int32, (ts, POOL_W), 1)
        back = jnp.where(col < POOL_GW, t2[0:ts],
                         jnp.where(col < 2 * POOL_GW, t4[0:ts], jnp.where(col < 3 * POOL_GW, t8[0:ts], t16)))
        dzr_ref[:, 0:POOL_W] = (back - dpooled).astype(BF16)
        gbuf[ts:ts + HALO] = gbuf[0:HALO]

        head_of = lax.broadcasted_iota(jnp.int32, (ts, GROUP_W), 1) // HEAD_DIM
        prod = da * f["a"]
        inner = jnp.zeros((ts, GROUP_W), F32)
        for h in range(4):
            hm = head_of == h
            tot = jnp.sum(jnp.where(hm, prod, 0.0), axis=1, keepdims=True)
            inner = jnp.where(hm, tot, inner)
        for g, dc_ref in enumerate((dc0, dc1, dc2)):
            both = jnp.concatenate([f["wts"][g] * da, -f["wts"][g] * inner], axis=1)
            _to_class_order(both, ATT_GROUPS[g][1], rbuf, dc_ref, 0)

        @pl.when(j == n - 1)
        def _():
            wout_ref[...] = gwout_ref[...].astype(BF16)
            wao_ref[...] = gwao_ref[...].astype(BF16)
            wpo_ref[...] = gwpo_ref[...].astype(BF16)
            tn = D_MODEL // N_DEV
            pairs = [(wout_ref, gwout_hbm), (gwbd_ref, gwbd_hbm)]
            for staged, dst in ((wao_ref, gwao_hbm), (wpo_ref, gwpo_hbm)):
                pairs += [(staged.at[:, pl.ds(k * tn, tn)], dst.at[k]) for k in range(N_DEV)]
            copies = [pltpu.make_async_copy(src, dst, sem.at[k]) for k, (src, dst) in enumerate(pairs)]
            for cp in copies:
                cp.start()
            for cp in copies:
                cp.wait()

    idx = lambda j: n - 1 - j
    dc_shapes = [jax.ShapeDtypeStruct((s, 2 * GROUP_W), F32), jax.ShapeDtypeStruct((4, s // 4, 2 * GROUP_W), F32),
                 jax.ShapeDtypeStruct((16, s // 16, 2 * GROUP_W), F32)]
    weights = (wbd, wao, wpo, wout)
    grad_shapes = [(D_MODEL, D_MODEL), (GROUP_W, D_MODEL), (POOL_W, D_MODEL), (len(GROUP_WIN), WIN, WIN)]
    tile_buf = pltpu.VMEM((ts + HALO, POOL_W), F32)
    outs = pl.pallas_call(
        body, name="mix_bwd", grid=(n,),
        in_specs=[pl.BlockSpec((ts, D_MODEL), lambda j: (idx(j), 0)),
                  pl.BlockSpec((ts, REST_W), lambda j: (idx(j), 0)),
                  pl.BlockSpec((HALO, POOL_W), lambda j: (jnp.maximum(idx(j) * (ts // HALO) - 1, 0), 0)),
                  pl.BlockSpec((None, ts, POOL_W), lambda j: (jnp.minimum(idx(j), 1), 0, 0))]
                 + _dil_specs(ts, 2 * GROUP_W, idx)
                 + [pl.BlockSpec((1, POOL_W), lambda j: (0, 0))] + [ANY] * 4,
        out_specs=[pl.BlockSpec((ts, REST_W), lambda j: (idx(j), 0))]
                  + _dil_specs(ts, 2 * GROUP_W, idx)
                  + [pl.BlockSpec((1, POOL_W), lambda j: (0, 0))] + [ANY] * 4,
        out_shape=[jax.ShapeDtypeStruct((s, REST_W), BF16)] + dc_shapes
                  + [jax.ShapeDtypeStruct((1, POOL_W), F32),
                     jax.ShapeDtypeStruct((D_MODEL, D_MODEL), BF16),
                     jax.ShapeDtypeStruct((N_DEV, GROUP_W, D_MODEL // N_DEV), BF16),
                     jax.ShapeDtypeStruct((N_DEV, POOL_W, D_MODEL // N_DEV), BF16),
                     jax.ShapeDtypeStruct(grad_shapes[3], F32)],
        scratch_shapes=[tile_buf] * 8 + [pltpu.VMEM((2 * GROUP_W // LANES, ts, LANES), F32)]
                       + [pltpu.VMEM(w.shape, BF16) for w in weights]
                       + [pltpu.VMEM(shape, F32) for shape in grad_shapes]
                       + [pltpu.SemaphoreType.DMA((2 + 2 * N_DEV,))],
        compiler_params=_params(dimension_semantics=("arbitrary",)),
    )(dh1, zr, zr, _inverse_counts(ts), *ol_dil, scale, wbd, wao, wpo, wout)
    dzr, dc_dil, g_scale = outs[0], outs[1:4], outs[4]
    g_out, g_ao, g_po, g_bd = outs[5:]
    return dzr, dc_dil, (g_out.reshape(N_DEV, D_MODEL // N_DEV, D_MODEL), g_ao, g_po, g_bd, g_scale)


FF_CHUNK = 1024


STAT_ROWS = 8


def _mlp_fwd_bwd(h1, tgt, gains, wmi, wmo, ts):
    s = h1.shape[0]
    n = s // ts
    nchunk = D_FF // FF_CHUNK

    def body(h1_ref, t_ref, g_ref, wmi_hbm, wmo_hbm, dh1_ref, mdt_ref, hd_ref, stat_ref, wmi, wmo, relu_buf, sem):
        i = pl.program_id(0)
        _load_resident(i, _column_blocks(wmi_hbm, wmi) + [(wmo_hbm, wmo)], sem)

        @pl.when(i == 0)
        def _():
            stat_ref[...] = jnp.zeros_like(stat_ref)

        h1 = h1_ref[...]
        g2 = g_ref[0:1, :]
        g3 = g_ref[1:2, :]
        r2 = _rms(h1)
        n2 = h1 * r2
        m = n2 * g2
        mb = m.astype(BF16)
        mdt_ref[0] = m.T.astype(BF16)
        h2 = h1
        for c in range(nchunk):
            cols = slice(c * FF_CHUNK, (c + 1) * FF_CHUNK)
            rl = jnp.maximum(_dot(mb, wmi[:, cols]), 0.0)
            relu_buf[:, cols] = rl
            hb = (rl * rl).astype(BF16)
            hd_ref[0, :, cols] = hb
            h2 = h2 + _dot(hb, wmo[cols, :])
        r3 = _rms(h2)
        n3 = h2 * r3
        diff = n3 * g3 - t_ref[...]
        loss = jnp.sum(0.5 * jnp.sum(diff * diff, axis=1, keepdims=True) / D_MODEL, axis=0, keepdims=True)
        dy = diff * (1.0 / D_MODEL)
        dg3 = jnp.sum(dy * n3, axis=0, keepdims=True)
        dh2 = _rms_bwd(dy * g3, n3, r3)
        dh2b = dh2.astype(BF16)
        mdt_ref[1] = dh2.T.astype(BF16)
        dm = jnp.zeros((ts, D_MODEL), F32)
        for c in range(nchunk):
            cols = slice(c * FF_CHUNK, (c + 1) * FF_CHUNK)
            dfb = (_dot_nt(dh2b, wmo[cols, :]) * (2.0 * relu_buf[:, cols])).astype(BF16)
            hd_ref[1, :, cols] = dfb
            dm = dm + _dot_nt(dfb, wmi[:, cols])
        dg2 = jnp.sum(dm * n2, axis=0, keepdims=True)
        dh1_ref[...] = dh2 + _rms_bwd(dm * g2, n2, r2)
        row = lax.broadcasted_iota(jnp.int32, (STAT_ROWS, D_MODEL), 0)
        stat_ref[...] += jnp.where(row == 0, loss, jnp.where(row == 1, dg2, jnp.where(row == 2, dg3, 0.0)))

    row = lambda w: pl.BlockSpec((ts, w), lambda i: (i, 0))
    return pl.pallas_call(
        body, name="mlp_fwd_bwd", grid=(n,),
        in_specs=[row(D_MODEL), row(D_MODEL), pl.BlockSpec((2, D_MODEL), lambda i: (0, 0)), ANY, ANY],
        out_specs=[row(D_MODEL), pl.BlockSpec((2, D_MODEL, ts), lambda i: (0, 0, i)),
                   pl.BlockSpec((2, ts, D_FF), lambda i: (0, i, 0)),
                   pl.BlockSpec((STAT_ROWS, D_MODEL), lambda i: (0, 0))],
        out_shape=[jax.ShapeDtypeStruct((s, D_MODEL), F32), jax.ShapeDtypeStruct((2, D_MODEL, s), BF16),
                   jax.ShapeDtypeStruct((2, s, D_FF), BF16), jax.ShapeDtypeStruct((STAT_ROWS, D_MODEL), F32)],
        scratch_shapes=[pltpu.VMEM((D_MODEL, D_FF), BF16), pltpu.VMEM((D_FF, D_MODEL), BF16),
                        pltpu.VMEM((ts, D_FF), F32), pltpu.SemaphoreType.DMA((N_DEV + 1,))],
        compiler_params=_params(dimension_semantics=("arbitrary",)),
    )(h1, tgt, gains, wmi, wmo)


def _in_proj_bwd(x, dh1, dzr, dqkv_dil, g1, w_in, ts):
    s = x.shape[0]
    n = s // ts

    def body(x_ref, dh_ref, dzr_ref, q0, q1, q2, g_ref, w_hbm, dx_ref, dz_ref, dg_ref, w_ref, qbuf, sem):
        i = pl.program_id(0)
        _load_resident(i, _column_blocks(w_hbm, w_ref), sem)

        @pl.when(i == 0)
        def _():
            dg_ref[...] = jnp.zeros_like(dg_ref)

        for g, dqkv in enumerate(_gather_rows((q0, q1, q2), qbuf, ts)):
            for sec in range(3):
                c0 = sec * ATT_W + g * GROUP_W
                dz_ref[:, c0:c0 + GROUP_W] = dqkv[:, sec * GROUP_W:(sec + 1) * GROUP_W].astype(BF16)
        dz_ref[:, 3 * ATT_W:] = dzr_ref[...]
        du = _dot_nt(dz_ref[...], w_ref[...])
        x = x_ref[...]
        r1 = _rms(x)
        n1 = x * r1
        g1 = g_ref[...]
        dg_ref[...] += jnp.sum(du * n1, axis=0, keepdims=True)
        dx_ref[...] = dh_ref[...] + _rms_bwd(du * g1, n1, r1)

    row = lambda w: pl.BlockSpec((ts, w), lambda i: (i, 0))
    vec = pl.BlockSpec((1, D_MODEL), lambda i: (0, 0))
    return pl.pallas_call(
        body, name="in_proj_bwd", grid=(n,),
        in_specs=[row(D_MODEL), row(D_MODEL), row(REST_W)] + _dil_specs(ts, ATT_W, lambda i: i) + [vec, ANY],
        out_specs=[row(D_MODEL), row(N_IN), vec],
        out_shape=[jax.ShapeDtypeStruct((s, D_MODEL), F32), jax.ShapeDtypeStruct((s, N_IN), BF16),
                   jax.ShapeDtypeStruct((1, D_MODEL), F32)],
        scratch_shapes=[pltpu.VMEM((D_MODEL, N_IN), BF16), pltpu.VMEM((ATT_W // LANES, ts, LANES), F32),
                        pltpu.SemaphoreType.DMA((N_DEV,))],
        compiler_params=_params(dimension_semantics=("arbitrary",)),
    )(x, dh1, dzr, *dqkv_dil, g1, w_in)


GRAD_PASS = 2


def _weight_grad(at, b, name, transpose_out, tk=2048, at_slot=None, b_slot=None, ex=None):
    m, s = at.shape[-2:]
    nn = b.shape[-1]
    tn = nn // N_DEV
    tk = min(tk, s)
    nk = s // tk
    npass = N_DEV // GRAD_PASS
    oshape = (tn, m) if transpose_out else (m, tn)
    owner = lambda jj: N_DEV - 1 - jj
    order = jnp.stack([_linear(_peer(_my_place(), owner(jj))) for jj in range(N_DEV)]).astype(jnp.int32)
    sent_in = lambda p: [jj for jj in range(N_DEV - 1) if jj // GRAD_PASS == p]

    nex = ex.nw if ex else 0

    def body(order_ref, at_ref, *refs):
        b_refs, ex_ins, (got_ref,), ex_outs, (acc, res, send_sems, recv_sems, local_sem), ex_sems = _split_refs(
            refs, (GRAD_PASS, nex, 1, nex, 5, 3 if ex else 0))
        j, k = pl.program_id(0), pl.program_id(1)
        me = _my_place()
        mine = _linear(me)
        if ex:
            _carry_start(ex, j * nk + k, ex_ins, ex_outs, ex_sems)

        def send(jj):
            return pltpu.make_async_remote_copy(
                src_ref=res.at[(jj // GRAD_PASS) % 2, jj % GRAD_PASS], dst_ref=got_ref.at[mine],
                send_sem=send_sems.at[jj], recv_sem=recv_sems.at[jj],
                device_id=_peer(me, owner(jj)), device_id_type=pl.DeviceIdType.MESH)

        @pl.when(k == 0)
        def _():
            acc[...] = jnp.zeros_like(acc)

        a = at_ref[...]
        for g in range(GRAD_PASS):
            acc[g] += _dot(a, b_refs[g][...])

        @pl.when(k == nk - 1)
        def _():
            for p in range(2, npass):
                @pl.when(j == p)
                def _():
                    for jj in sent_in(p - 2):
                        send(jj).wait_send()

            for g in range(GRAD_PASS):
                r = acc[g]
                res[j % 2, g] = (r.T if transpose_out else r).astype(BF16)
            for p in range(npass):
                @pl.when(j == p)
                def _():
                    for jj in sent_in(p):
                        send(jj).start()

            @pl.when(j == npass - 1)
            def _():
                own = pltpu.make_async_copy(res.at[(npass - 1) % 2, GRAD_PASS - 1], got_ref.at[mine], local_sem.at[0])
                own.start()
                for p in range(max(npass - 2, 0), npass):
                    for jj in sent_in(p):
                        send(jj).wait_send()
                for jj in range(N_DEV - 1):
                    send(jj).wait_recv()
                own.wait()

        if ex:
            _carry_wait(ex, j * nk + k, npass * nk - 1, ex_ins, ex_outs, ex_sems)

    if at_slot is None:
        at_spec = pl.BlockSpec((m, tk), lambda j, k, o: (0, k))
    else:
        at_spec = pl.BlockSpec((None, m, tk), lambda j, k, o: (at_slot, 0, k))
    if b_slot is None:
        b_spec = lambda g: pl.BlockSpec((tk, tn), lambda j, k, o: (k, o[GRAD_PASS * j + g]))
    else:
        b_spec = lambda g: pl.BlockSpec((None, tk, tn), lambda j, k, o: (b_slot, k, o[GRAD_PASS * j + g]))
    outs = pl.pallas_call(
        body, name=name,
        grid_spec=pltpu.PrefetchScalarGridSpec(
            num_scalar_prefetch=1, grid=(npass, nk),
            in_specs=[at_spec] + [b_spec(g) for g in range(GRAD_PASS)] + (ex.specs if ex else []),
            out_specs=[ANY] + (ex.specs if ex else []),
            scratch_shapes=[pltpu.VMEM((GRAD_PASS, m, tn), F32), pltpu.VMEM((2, GRAD_PASS) + oshape, BF16),
                            pltpu.SemaphoreType.DMA((N_DEV - 1,)), pltpu.SemaphoreType.DMA((N_DEV - 1,)),
                            pltpu.SemaphoreType.DMA((1,))] + (ex.scratch if ex else [])),
        out_shape=[jax.ShapeDtypeStruct((N_DEV,) + oshape, BF16)] + (ex.out_shape if ex else []),
        compiler_params=_params(dimension_semantics=("arbitrary", "arbitrary")),
    )(order, at, *([b] * GRAD_PASS), *(ex.arrays if ex else []))
    return (outs[0], outs[1:]) if ex else outs[0]


def _my_place():
    x, y, c = lax.axis_index("x"), lax.axis_index("y"), lax.axis_index("c")
    return x, y, c


def _peer(place, k):
    x, y, c = place
    return (1 - x if k & 4 else x, 1 - y if k & 2 else y, 1 - c if k & 1 else c)


def _linear(place):
    x, y, c = place
    return 4 * x + 2 * y + c


class _Exchange:
    passes_on = False

    def __init__(self, arrays, gather):
        self.arrays, self.gather, self.nw = list(arrays), list(gather), len(arrays)
        self.out_shape = []
        for a, g in zip(arrays, gather):
            block = a.shape if g else a.shape[1:]
            self.out_shape.append(jax.ShapeDtypeStruct((N_DEV,) + tuple(block), a.dtype))
        self.specs = [ANY] * self.nw
        self.scratch = [pltpu.SemaphoreType.DMA((self.nw, N_DEV - 1)), pltpu.SemaphoreType.DMA((self.nw, N_DEV - 1)),
                        pltpu.SemaphoreType.DMA((self.nw,))]

    def _copies(self, ins, outs, sems):
        send_sems, recv_sems, local_sems = sems
        me = _my_place()
        mine = _linear(me)
        copies = []
        for w in range(self.nw):
            src = ins[w] if self.gather[w] else ins[w].at[mine]
            copies.append(pltpu.make_async_copy(src, outs[w].at[mine], local_sems.at[w]))
        for k in range(1, N_DEV):
            peer = _peer(me, k)
            for w in range(self.nw):
                src = ins[w] if self.gather[w] else ins[w].at[_linear(peer)]
                copies.append(pltpu.make_async_remote_copy(
                    src_ref=src, dst_ref=outs[w].at[mine],
                    send_sem=send_sems.at[w, k - 1], recv_sem=recv_sems.at[w, k - 1],
                    device_id=peer, device_id_type=pl.DeviceIdType.MESH))
        return copies

    def start(self, ins, outs, sems):
        for cp in self._copies(ins, outs, sems):
            cp.start()

    def wait(self, ins, outs, sems):
        copies = self._copies(ins, outs, sems)
        for cp in copies[self.nw:]:
            cp.wait_recv()
        for cp in copies[self.nw:]:
            cp.wait_send()
        for cp in copies[:self.nw]:
            cp.wait()


class _Gather:
    passes_on = True

    def __init__(self, arrays):
        self.arrays, self.nw = list(arrays), len(arrays)
        self.out_shape = [jax.ShapeDtypeStruct((N_DEV,) + tuple(a.shape), a.dtype) for a in arrays]
        self.specs = [ANY] * self.nw
        self.scratch = [pltpu.SemaphoreType.DMA((self.nw, N_DEV - 1)), pltpu.SemaphoreType.DMA((self.nw, N_DEV - 1)),
                        pltpu.SemaphoreType.DMA((self.nw,))]

    @staticmethod
    def _places():
        x, y, c = _my_place()
        return (x, y, c), (x, y, 1 - c), [(1 - x, y), (x, 1 - y), (1 - x, 1 - y)]

    @staticmethod
    def _copy(outs, sems, w, k, block, to, src=None):
        rows = outs[w].at[_linear(block)]
        return pltpu.make_async_remote_copy(
            src_ref=rows if src is None else src, dst_ref=rows, send_sem=sems[0].at[w, k], recv_sem=sems[1].at[w, k],
            device_id=to, device_id_type=pl.DeviceIdType.MESH)

    def _first(self, ins, outs, sems, w):
        me, sibling, chips = self._places()
        return ([self._copy(outs, sems, w, 0, me, sibling, src=ins[w])]
                + [self._copy(outs, sems, w, 1 + j, me, (*chip, me[2]), src=ins[w]) for j, chip in enumerate(chips)])

    def _passed(self, outs, sems, w):
        me, sibling, chips = self._places()
        return [self._copy(outs, sems, w, 4 + j, (*chip, me[2]), sibling) for j, chip in enumerate(chips)]

    def _local(self, ins, outs, sems, w):
        return pltpu.make_async_copy(ins[w], outs[w].at[_linear(self._places()[0])], sems[2].at[w])

    def start(self, ins, outs, sems):
        for w in range(self.nw):
            self._local(ins, outs, sems, w).start()
            for cp in self._first(ins, outs, sems, w):
                cp.start()

    def pass_on(self, ins, outs, sems):
        me, sibling, chips = self._places()
        for j, chip in enumerate(chips):
            for w in range(self.nw):
                self._copy(outs, sems, w, 1 + j, (*chip, me[2]), me).wait_recv()
                self._passed(outs, sems, w)[j].start()

    def wait(self, ins, outs, sems):
        me, sibling, chips = self._places()
        for w in range(self.nw):
            self._copy(outs, sems, w, 0, sibling, me).wait_recv()
            for j, chip in enumerate(chips):
                self._copy(outs, sems, w, 4 + j, (*chip, sibling[2]), me).wait_recv()
            for cp in self._first(ins, outs, sems, w) + self._passed(outs, sems, w):
                cp.wait_send()
            self._local(ins, outs, sems, w).wait()


def _sum_parts(p_ref):
    g = p_ref[0].astype(F32)
    for j in range(1, N_DEV):
        g = g + p_ref[j].astype(F32)
    return g


def _adam_update(g, w, m, v):
    nm = ADAM_B1 * m + (1.0 - ADAM_B1) * g
    nv = ADAM_B2 * v + (1.0 - ADAM_B2) * (g * g)
    m_hat = nm / (1.0 - ADAM_B1 ** ADAM_STEP)
    v_hat = nv / (1.0 - ADAM_B2 ** ADAM_STEP)
    return -ADAM_LR * (m_hat / (jnp.sqrt(v_hat) + ADAM_EPS) + ADAM_WD * w), nm, nv


def _adamw(parts, w, m, v, name, tr):
    rows, cols = w.shape
    tr = min(tr, rows)

    def body(p_ref, w_ref, m_ref, v_ref, g_ref, d_ref, nm_ref, nv_ref):
        g = _sum_parts(p_ref)
        g_ref[...] = g
        d_ref[...], nm_ref[...], nv_ref[...] = _adam_update(g, w_ref[...], m_ref[...], v_ref[...])

    blk = pl.BlockSpec((tr, cols), lambda i: (i, 0))
    return pl.pallas_call(
        body, name=name, grid=(rows // tr,),
        in_specs=[pl.BlockSpec((N_DEV, tr, cols), lambda i: (0, i, 0)), blk, blk, blk],
        out_specs=[blk] * 4,
        out_shape=[jax.ShapeDtypeStruct((rows, cols), F32)] * 4,
        compiler_params=_params(dimension_semantics=("arbitrary",)),
    )(parts, w, m, v)


def _adamw_vectors(got_stats, got_g1, got_scale, params):
    def body(st_ref, g1_ref, sc_ref, *refs):
        ins, outs = refs[:12], refs[12:]
        stats = _sum_parts(st_ref)
        outs[0][...] = stats[0:1, 0:1]
        grads = (_sum_parts(g1_ref), _sum_parts(sc_ref), stats[1:2, :], stats[2:3, :])
        for k, g in enumerate(grads):
            w_ref, m_ref, v_ref = ins[3 * k:3 * k + 3]
            g_ref, d_ref, nm_ref, nv_ref = outs[1 + 4 * k:5 + 4 * k]
            g_ref[...] = g
            d_ref[...], nm_ref[...], nv_ref[...] = _adam_update(g, w_ref[...], m_ref[...], v_ref[...])

    flat = [a for p in params for a in p]
    out_shape = [jax.ShapeDtypeStruct((1, 1), F32)]
    for w, _, _ in params:
        out_shape += [jax.ShapeDtypeStruct(w.shape, F32)] * 4
    res = pl.pallas_call(body, name="adamw_vectors", out_shape=out_shape)(got_stats, got_g1, got_scale, *flat)
    return res[0], [res[1 + 4 * k:5 + 4 * k] for k in range(len(params))]


def kernel(x, norm_mix_g, w_in, w_att_out, w_pool_grp, pool_scale, w_pool_out, w_out, norm_mlp_g, w_mlp_in, w_mlp_out, norm_final_g, loss_target, m_norm_mix_g, m_w_in, m_w_att_out, m_w_pool_grp, m_pool_scale, m_w_pool_out, m_w_out, m_norm_mlp_g, m_w_mlp_in, m_w_mlp_out, m_norm_final_g, v_norm_mix_g, v_w_in, v_w_att_out, v_w_pool_grp, v_pool_scale, v_w_pool_out, v_w_out, v_norm_mlp_g, v_w_mlp_in, v_w_mlp_out, v_norm_final_g):
    x, tgt = x[0], loss_target[0]
    s = x.shape[0]
    g1, g2, g3 = norm_mix_g, norm_mlp_g, norm_final_g.reshape(1, D_MODEL)
    shards = [w_in[0], w_att_out[0], w_pool_out[0], w_out[0], w_mlp_in[0], w_mlp_out[0]]
    wire = [a.astype(BF16) for a in shards]
    cols = lambda a: jnp.transpose(a, (1, 0, 2)).reshape(a.shape[1], N_DEV * a.shape[2])
    rows = lambda a: a.reshape(N_DEV * a.shape[1], a.shape[2])
    wbd =_window_weights(w_pool_grp[0]).astype(BF16)

    (u, ut), (f_in,) = _rms_u(x, g1, 1024, _Gather(wire[:1]))
    (qkv0, qkv1, qkv2, zr), later = _in_proj_fwd(u, f_in, 1024, _Gather(wire[1:]))
    f_ao, f_po, f_out, f_mi, f_mo = cols(later[0]), cols(later[1]), rows(later[2]), later[3], rows(later[4])
    qkv_dil = (qkv0, qkv1, qkv2)
    flat = lambda a: a.reshape(s, a.shape[-1])
    shaped = lambda a, g: a if g == 0 else a.reshape(ATT_GROUPS[g][1], s // ATT_GROUPS[g][1], a.shape[-1])
    ol_dil = [shaped(_attn_fwd(flat(qkv_dil[g]), g), g) for g in range(N_GROUPS)]
    h1 = _mix_fwd(x, zr, ol_dil, wbd, pool_scale, f_ao, f_po, f_out, ts=512)

    dh1, mdt, hd, stats = _mlp_fwd_bwd(h1, tgt, jnp.concatenate([g2, g3], axis=0), f_mi, f_mo, ts=256)
    got = {"w_mlp_in": _weight_grad(mdt, hd, "grad_w_mlp_in", transpose_out=False, at_slot=0, b_slot=1),
           "w_mlp_out": _weight_grad(mdt, hd, "grad_w_mlp_out", transpose_out=True, at_slot=1, b_slot=0)}
    dzr, dc_dil, (g_out, g_ao, g_po, g_bd, g_scale) = _mix_bwd(
        dh1, zr, ol_dil, wbd, pool_scale, f_ao, f_po, f_out, ts=256)
    g_grp = jnp.stack([g_bd[g, off:off + POOL_GW, off:off + POOL_GW] for g, (_, off) in enumerate(GROUP_WIN)])
    early = [_Exchange([g_out], [False]), _Exchange([g_ao, g_po], [False, False]),
             _Exchange([g_grp.reshape(4 * POOL_GW, POOL_GW)], [True])]
    dqkv_dil, arrived = [], []
    for g in range(N_GROUPS):
        dqkv, rode = _attn_bwd(flat(qkv_dil[g]), flat(dc_dil[g]), g, early[g])
        dqkv_dil.append(shaped(dqkv, g))
        arrived += list(rode)
    got["w_out"], got["w_att_out"], got["w_pool_out"], got_grp = arrived
    dx, dz, dg1 = _in_proj_bwd(x, dh1, dzr, dqkv_dil, g1, f_in, ts=512)
    got["w_in"], got_vectors = _weight_grad(ut, dz, "grad_w_in", transpose_out=False,
                                            ex=_Exchange([stats, dg1, g_scale], [True] * 3))

    names = ["w_in", "w_att_out", "w_pool_out", "w_out", "w_mlp_in", "w_mlp_out"]
    ms = [m_w_in, m_w_att_out, m_w_pool_out, m_w_out, m_w_mlp_in, m_w_mlp_out]
    vs = [v_w_in, v_w_att_out, v_w_pool_out, v_w_out, v_w_mlp_in, v_w_mlp_out]
    upd = {}
    for k, name in enumerate(names):
        res = _adamw(got[name], shards[k], ms[k][0], vs[k][0], "adamw_" + name, tr=256)
        upd[name] = [a[None] for a in res]

    as_rows = lambda a: a.reshape(4 * POOL_GW, POOL_GW)
    res = _adamw(got_grp, as_rows(w_pool_grp), as_rows(m_w_pool_grp), as_rows(v_w_pool_grp), "adamw_w_pool_grp", tr=2048)
    upd["w_pool_grp"] = [a.reshape(w_pool_grp.shape) for a in res]
    vectors = {"norm_mix_g": (norm_mix_g, m_norm_mix_g, v_norm_mix_g), "pool_scale": (pool_scale, m_pool_scale, v_pool_scale),
               "norm_mlp_g": (norm_mlp_g, m_norm_mlp_g, v_norm_mlp_g),
               "norm_final_g": (norm_final_g, m_norm_final_g, v_norm_final_g)}
    loss, vector_res = _adamw_vectors(*got_vectors, [tuple(a.reshape(1, -1) for a in p) for p in vectors.values()])
    for (name, (w, _, _)), res in zip(vectors.items(), vector_res):
        upd[name] = [a.reshape(w.shape) for a in res]

    order = ["norm_mix_g", "w_in", "w_att_out", "w_pool_grp", "pool_scale", "w_pool_out", "w_out", "norm_mlp_g",
             "w_mlp_in", "w_mlp_out", "norm_final_g"]
    out = [loss.reshape(()), dx[None]]
    for q in range(4):
        out += [upd[name][q] for name in order]
    return tuple(out)
```

```python
import jax
import jax.numpy as jnp
import numpy as np
from jax import lax
from jax.experimental import pallas as pl
from jax.experimental.pallas import tpu as pltpu

F32 = jnp.float32
BF16 = jnp.bfloat16

D_MODEL = 1024
HEAD_DIM = 64
GROUP_W = 256
HEADS = GROUP_W // HEAD_DIM
ATT_GROUPS = ((128, 1), (512, 4), (2048, 16))
N_GROUPS = len(ATT_GROUPS)
BLK = 128
ATT_W = N_GROUPS * GROUP_W
POOL_WINDOWS = (2, 4, 8, 16)
POOL_GW = 192
POOL_W = len(POOL_WINDOWS) * POOL_GW
D_FF = 4096
N_IN = 5120
REST_W = N_IN - 3 * ATT_W
NORM_EPS = 1e-6
ALIBI_MAX_BIAS = 8.0
N_DEV = 8
HALO = 32

ADAM_LR = 0.001
ADAM_B1 = 0.9
ADAM_B2 = 0.999
ADAM_EPS = 1e-08
ADAM_WD = 0.01
ADAM_STEP = 10

VMEM_LIMIT = 56 * 1024 * 1024
ANY = pl.BlockSpec(memory_space=pl.ANY)


def _params(**kw):
    return pltpu.CompilerParams(vmem_limit_bytes=VMEM_LIMIT, **kw)


def _dot(a, b):
    return jnp.dot(a, b, preferred_element_type=F32)


def _dot_nt(a, b):
    return lax.dot_general(a, b, (((1,), (1,)), ((), ())), preferred_element_type=F32)


def _dot_tn(a, b):
    return lax.dot_general(a, b, (((0,), (0,)), ((), ())), preferred_element_type=F32)


def _slope(group, h):
    return 2.0 ** (-ALIBI_MAX_BIAS * (HEADS * group + h + 1.0) / (N_GROUPS * HEADS))


def _load_resident(step, pairs, sem):
    @pl.when(step == 0)
    def _():
        copies = [pltpu.make_async_copy(src, dst, sem.at[n]) for n, (src, dst) in enumerate(pairs)]
        for cp in copies:
            cp.start()
        for cp in copies:
            cp.wait()


LANES = 128


def _to_class_order(value, dil, buf, ref, col0):
    ts, w = value.shape
    if dil == 1:
        ref[:, col0:col0 + w] = value.astype(ref.dtype)
        return
    for c in range(w // LANES):
        buf[c] = value[:, c * LANES:(c + 1) * LANES]
        for r in range(dil):
            ref[r, :, col0 + c * LANES:col0 + (c + 1) * LANES] = (
                buf[c, pl.ds(r, ts // dil, stride=dil), :].astype(ref.dtype))


def _to_token_order(ref, dil, buf, ts):
    if dil == 1:
        return ref[...].astype(F32)
    w = ref.shape[-1]
    for c in range(w // LANES):
        for r in range(dil):
            buf[c, pl.ds(r, ts // dil, stride=dil), :] = ref[r, :, c * LANES:(c + 1) * LANES].astype(F32)
    return jnp.concatenate([buf[c] for c in range(w // LANES)], axis=1)


def _dil_shapes(s, width, dtype):
    return [jax.ShapeDtypeStruct((s, width) if d == 1 else (d, s // d, width), dtype) for _, d in ATT_GROUPS]


def _dil_specs(ts, width, idx):
    return [pl.BlockSpec((ts, width), lambda i: (idx(i), 0)) if d == 1 else
            pl.BlockSpec((d, ts // d, width), lambda i: (0, idx(i), 0)) for _, d in ATT_GROUPS]


def _column_blocks(stacked, dst):
    tn = stacked.shape[2]
    return [(stacked.at[j], dst.at[:, pl.ds(j * tn, tn)]) for j in range(N_DEV)]


def _sigmoid(x):
    return 0.5 * jnp.tanh(0.5 * x) + 0.5


def _rms(x):
    return lax.rsqrt(jnp.mean(x * x, axis=-1, keepdims=True) + NORM_EPS)


def _rms_bwd(dn, n, r):
    return r * (dn - n * jnp.mean(dn * n, axis=-1, keepdims=True))


def _split_refs(refs, counts):
    out, at = [], 0
    for c in counts:
        out.append(refs[at:at + c])
        at += c
    return out


def _carry_start(ex, step, ins, outs, sems):
    @pl.when(step == 0)
    def _():
        ex.start(ins, outs, sems)


def _carry_wait(ex, step, last, ins, outs, sems, pass_at=None):
    if ex.passes_on:
        @pl.when(step == (last if pass_at is None else pass_at))
        def _():
            ex.pass_on(ins, outs, sems)

    @pl.when(step == last)
    def _():
        ex.wait(ins, outs, sems)


def _rms_u(x, g1, ts, ex):
    s = x.shape[0]
    n = s // ts

    def body(*refs):
        (x_ref, g_ref), ex_ins, (u_ref, ut_ref), ex_outs, ex_sems = _split_refs(refs, (2, ex.nw, 2, ex.nw, 3))
        i = pl.program_id(0)
        _carry_start(ex, i, ex_ins, ex_outs, ex_sems)
        x = x_ref[...]
        u = x * _rms(x) * g_ref[...]
        u_ref[...] = u.astype(BF16)
        ut_ref[...] = u.T.astype(BF16)
        _carry_wait(ex, i, n - 1, ex_ins, ex_outs, ex_sems)

    outs = pl.pallas_call(
        body, name="rms_u", grid=(n,),
        in_specs=[pl.BlockSpec((ts, D_MODEL), lambda i: (i, 0)), pl.BlockSpec((1, D_MODEL), lambda i: (0, 0))]
                 + ex.specs,
        out_specs=[pl.BlockSpec((ts, D_MODEL), lambda i: (i, 0)), pl.BlockSpec((D_MODEL, ts), lambda i: (0, i))]
                  + ex.specs,
        out_shape=[jax.ShapeDtypeStruct((s, D_MODEL), BF16), jax.ShapeDtypeStruct((D_MODEL, s), BF16)] + ex.out_shape,
        scratch_shapes=ex.scratch,
        compiler_params=_params(dimension_semantics=("arbitrary",)),
    )(x, g1, *ex.arrays)
    return outs[:2], outs[2:]


def _in_proj_fwd(u, w_in, ts, ex):
    s = u.shape[0]
    n = s // ts
    dils = [d for _, d in ATT_GROUPS]

    def body(*refs):
        (u_ref, w_hbm), ex_ins, (q0_ref, q1_ref, q2_ref, zr_ref), ex_outs, (w_ref, zbuf, sem), ex_sems = (
            _split_refs(refs, (2, ex.nw, 4, ex.nw, 3, 3)))
        i = pl.program_id(0)
        _carry_start(ex, i, ex_ins, ex_outs, ex_sems)
        _load_resident(i, _column_blocks(w_hbm, w_ref), sem)
        ub = u_ref[...]
        outs = (q0_ref, q1_ref, q2_ref)
        for sec in range(3):
            for g in range(N_GROUPS):
                c0 = sec * ATT_W + g * GROUP_W
                zc = _dot(ub, w_ref[:, c0:c0 + GROUP_W])
                _to_class_order(zc, dils[g], zbuf, outs[g], sec * GROUP_W)
        for c0 in range(0, REST_W, 256):
            zr_ref[:, c0:c0 + 256] = _dot(ub, w_ref[:, 3 * ATT_W + c0:3 * ATT_W + c0 + 256]).astype(BF16)
        _carry_wait(ex, i, n - 1, ex_ins, ex_outs, ex_sems, pass_at=max(n - 3, 0))

    outs = pl.pallas_call(
        body, name="in_proj_fwd", grid=(n,),
        in_specs=[pl.BlockSpec((ts, D_MODEL), lambda i: (i, 0)), ANY] + ex.specs,
        out_specs=_dil_specs(ts, ATT_W, lambda i: i) + [pl.BlockSpec((ts, REST_W), lambda i: (i, 0))] + ex.specs,
        out_shape=_dil_shapes(s, ATT_W, BF16) + [jax.ShapeDtypeStruct((s, REST_W), BF16)] + ex.out_shape,
        scratch_shapes=[pltpu.VMEM((D_MODEL, N_IN), BF16), pltpu.VMEM((GROUP_W // LANES, ts, LANES), F32),
                        pltpu.SemaphoreType.DMA((N_DEV,))] + ex.scratch,
        compiler_params=_params(dimension_semantics=("arbitrary",)),
    )(u, w_in, *ex.arrays)
    return outs[:4], outs[4:]


ATT_TILE = 8 * BLK


STACK = HEADS * BLK


SCORE_SCALE = HEAD_DIM ** -0.5


def _band_consts(group, dil):
    row = lax.broadcasted_iota(jnp.int32, (STACK, 2 * BLK), 0)
    kj = lax.broadcasted_iota(jnp.int32, (STACK, 2 * BLK), 1)
    head = row // BLK
    steps = BLK + (row % BLK) - kj
    slope = jnp.full((STACK, 2 * BLK), _slope(group, HEADS - 1), F32)
    for h in range(HEADS - 1):
        slope = jnp.where(head == h, _slope(group, h), slope)
    in_band = (steps >= 0) & (steps <= BLK)
    return jnp.where(in_band, slope * (steps.astype(F32) * float(dil)), jnp.inf), kj


def _first_key(block, nbc, nb, b):
    if nbc % nb == 0 and b != 0:
        return None
    return jnp.where((block % nbc) != 0, 0, BLK)


def _head_of_col():
    return lax.broadcasted_iota(jnp.int32, (BLK, GROUP_W), 1) // HEAD_DIM


def _stack_heads(xb):
    head_of = _head_of_col()
    return jnp.concatenate([jnp.where(head_of == h, xb, jnp.zeros_like(xb)) for h in range(HEADS)], axis=0)


def _per_head_cols(stacked, rhs, scale_rows=None):
    lane = lax.broadcasted_iota(jnp.int32, (BLK, LANES), 1)
    halves = []
    for pair in range(HEADS // 2):
        tile = rhs[:, pair * LANES:(pair + 1) * LANES]
        parts = []
        for h in (2 * pair, 2 * pair + 1):
            part = _dot(stacked[h * BLK:(h + 1) * BLK], tile)
            parts.append(part if scale_rows is None else part * scale_rows[h * BLK:(h + 1) * BLK])
        halves.append(jnp.where(lane < HEAD_DIM, parts[0], parts[1]))
    return jnp.concatenate(halves, axis=1)


def _per_head_rows(col):
    lane = lax.broadcasted_iota(jnp.int32, (BLK, LANES), 1)
    return jnp.concatenate(
        [jnp.where(lane < HEAD_DIM, col[(2 * pair) * BLK:(2 * pair + 1) * BLK], col[(2 * pair + 1) * BLK:(2 * pair + 2) * BLK])
         for pair in range(HEADS // 2)], axis=1)


def _band_softmax(qs, kb, penalty, kj, first_key):
    sc = _dot_nt(qs, kb) - penalty
    if first_key is not None:
        sc = jnp.where(kj >= first_key, sc, -jnp.inf)
    mx = jnp.max(sc, axis=1, keepdims=True)
    e = jnp.exp(sc - mx)
    return e, mx, jnp.sum(e, axis=1, keepdims=True)


def _attn_fwd(qkv, group):
    s = qkv.shape[0]
    dil = ATT_GROUPS[group][1]
    nbc = s // (BLK * dil)
    nb = ATT_TILE // BLK

    def body(q_ref, kc_ref, kp_ref, vc_ref, vp_ref, ol_ref, kbuf, vbuf):
        i = pl.program_id(0)
        kbuf[0:BLK] = kp_ref[...]
        kbuf[BLK:BLK + ATT_TILE] = kc_ref[...]
        vbuf[0:BLK] = vp_ref[...]
        vbuf[BLK:BLK + ATT_TILE] = vc_ref[...]
        penalty, kj = _band_consts(group, dil)
        blocks = range(nb)
        qss = [_stack_heads(q_ref[b * BLK:(b + 1) * BLK, :] * SCORE_SCALE) for b in blocks]
        soft = [_band_softmax(qss[b], kbuf[b * BLK:b * BLK + 2 * BLK, :], penalty, kj,
                              _first_key(i * nb + b, nbc, nb, b)) for b in blocks]
        for b in blocks:
            e, mx, den = soft[b]
            ol_ref[b * BLK:(b + 1) * BLK, 0:GROUP_W] = _per_head_cols(
                e.astype(BF16), vbuf[b * BLK:b * BLK + 2 * BLK, :], 1.0 / den)
        for b in blocks:
            e, mx, den = soft[b]
            ol_ref[b * BLK:(b + 1) * BLK, GROUP_W:] = _per_head_rows(mx + jnp.log(den))

    n = s // ATT_TILE
    cur = lambda c: pl.BlockSpec((ATT_TILE, GROUP_W), lambda i: (i, c))
    prev = lambda c: pl.BlockSpec((BLK, GROUP_W), lambda i: (jnp.maximum(i * nb - 1, 0), c))
    return pl.pallas_call(
        body, name=f"attn_fwd_g{group}", grid=(n,),
        in_specs=[cur(0), cur(1), prev(1), cur(2), prev(2)],
        out_specs=pl.BlockSpec((ATT_TILE, 2 * GROUP_W), lambda i: (i, 0)),
        out_shape=jax.ShapeDtypeStruct((s, 2 * GROUP_W), F32),
        scratch_shapes=[pltpu.VMEM((BLK + ATT_TILE, GROUP_W), BF16), pltpu.VMEM((BLK + ATT_TILE, GROUP_W), BF16)],
        compiler_params=_params(dimension_semantics=("arbitrary",)),
    )(qkv, qkv, qkv, qkv, qkv)


def _attn_bwd(qkv, dc, group, ex=None):
    s = qkv.shape[0]
    dil = ATT_GROUPS[group][1]
    nbc = s // (BLK * dil)
    nb = ATT_TILE // BLK
    n = s // ATT_TILE
    nex = ex.nw if ex else 0

    def body(*refs):
        ((q_ref, kc_ref, kp_ref, vc_ref, vp_ref, do_ref, c_ref), ex_ins, (out_ref,), ex_outs,
         (kbuf, vbuf, dqpend, dkpend, dvpend), ex_sems) = _split_refs(refs, (7, nex, 1, nex, 5, 3 if ex else 0))
        i = pl.program_id(0)
        if ex:
            _carry_start(ex, i, ex_ins, ex_outs, ex_sems)

        @pl.when(i == 0)
        def _():
            dqpend[...] = jnp.zeros_like(dqpend)
            dkpend[...] = jnp.zeros_like(dkpend)
            dvpend[...] = jnp.zeros_like(dvpend)

        out_ref[:, 0:GROUP_W] = dqpend[...]
        body_rows = slice(0, ATT_TILE - BLK)
        tail = slice(ATT_TILE - BLK, ATT_TILE)
        pends = ((dkpend, GROUP_W), (dvpend, 2 * GROUP_W))
        for pend, c0 in pends:
            out_ref[body_rows, c0:c0 + GROUP_W] = pend[body_rows, :].astype(BF16)

        @pl.when(i < n)
        def _():
            kbuf[0:BLK] = kp_ref[...]
            kbuf[BLK:BLK + ATT_TILE] = kc_ref[...]
            vbuf[0:BLK] = vp_ref[...]
            vbuf[BLK:BLK + ATT_TILE] = vc_ref[...]
            penalty, kj = _band_consts(group, dil)
            head_of = _head_of_col()
            blocks = range(nb)
            rows = [slice(b * BLK, (b + 1) * BLK) for b in blocks]
            kbs = [kbuf[b * BLK:b * BLK + 2 * BLK, :] for b in blocks]
            vbs = [vbuf[b * BLK:b * BLK + 2 * BLK, :] for b in blocks]
            qss = [_stack_heads(q_ref[rows[b], :] * SCORE_SCALE) for b in blocks]
            doss = [_stack_heads(do_ref[rows[b], :].astype(BF16)) for b in blocks]
            cors = []
            for b in blocks:
                cb = c_ref[rows[b], :]
                cors.append(jnp.concatenate(
                    [jnp.max(jnp.where(head_of == h, cb, -jnp.inf), axis=1, keepdims=True) for h in range(HEADS)],
                    axis=0))
            soft = [_band_softmax(qss[b], kbs[b], penalty, kj, _first_key(i * nb + b, nbc, nb, b)) for b in blocks]
            dps = [_dot_nt(doss[b], vbs[b]) for b in blocks]
            ps = [soft[b][0] * (1.0 / soft[b][2]) for b in blocks]
            dss = [(ps[b] * (dps[b] + cors[b])).astype(BF16) for b in blocks]
            for b in blocks:
                dqpend[rows[b], :] = _per_head_cols(dss[b], kbs[b] * SCORE_SCALE).astype(BF16)
            bands = [(_dot_tn(dss[b], qss[b]), _dot_tn(ps[b].astype(BF16), doss[b])) for b in blocks]
            for which, (pend, c0) in enumerate(pends):
                out_ref[tail, c0:c0 + GROUP_W] = (pend[tail, :] + bands[0][which][0:BLK]).astype(BF16)
                for b in range(nb):
                    own = bands[b][which][BLK:2 * BLK]
                    pend[b * BLK:(b + 1) * BLK, :] = own + bands[b + 1][which][0:BLK] if b + 1 < nb else own

        @pl.when(i == n)
        def _():
            for pend, c0 in pends:
                out_ref[tail, c0:c0 + GROUP_W] = pend[tail, :].astype(BF16)

        if ex:
            _carry_wait(ex, i, n, ex_ins, ex_outs, ex_sems)

    last = n - 1
    cur = lambda c: pl.BlockSpec((ATT_TILE, GROUP_W), lambda i: (jnp.minimum(i, last), c))
    prev = lambda c: pl.BlockSpec(
        (BLK, GROUP_W), lambda i: (jnp.maximum(jnp.minimum(i, last) * nb - 1, 0), c))
    outs = pl.pallas_call(
        body, name=f"attn_bwd_g{group}", grid=(n + 1,),
        in_specs=[cur(0), cur(1), prev(1), cur(2), prev(2), cur(0), cur(1)] + (ex.specs if ex else []),
        out_specs=[pl.BlockSpec((ATT_TILE, ATT_W), lambda i: (jnp.maximum(i - 1, 0), 0))] + (ex.specs if ex else []),
        out_shape=[jax.ShapeDtypeStruct((s, ATT_W), BF16)] + (ex.out_shape if ex else []),
        scratch_shapes=[pltpu.VMEM((BLK + ATT_TILE, GROUP_W), BF16), pltpu.VMEM((BLK + ATT_TILE, GROUP_W), BF16),
                        pltpu.VMEM((ATT_TILE, GROUP_W), BF16),
                        pltpu.VMEM((ATT_TILE, GROUP_W), F32), pltpu.VMEM((ATT_TILE, GROUP_W), F32)]
                       + (ex.scratch if ex else []),
        compiler_params=_params(dimension_semantics=("arbitrary",)),
    )(qkv, qkv, qkv, qkv, qkv, dc, dc, *(ex.arrays if ex else []))
    return (outs[0], outs[1:]) if ex else outs[0]


def _gather_rows(refs, buf, ts):
    return [_to_token_order(refs[g], ATT_GROUPS[g][1], buf, ts) for g in range(N_GROUPS)]


def _inverse_counts(ts):
    win = np.repeat(np.asarray(POOL_WINDOWS), POOL_GW)
    first = np.minimum(np.arange(ts)[:, None] + 1, win[None, :])
    counts = np.stack([first, np.broadcast_to(win[None, :], (ts, POOL_W))]).astype(np.float32)
    return jnp.asarray(np.float32(1.0) / counts)


def _pool_fwd(ebuf, s2, s4, s8, inv_count, ts):
    n = ts + HALO
    s2[8:n] = ebuf[8:n] + ebuf[7:n - 1]
    s4[16:n] = s2[16:n] + s2[14:n - 2]
    s8[24:n] = s4[24:n] + s4[20:n - 4]
    s16 = s8[32:n] + s8[24:n - 8]
    col = lax.broadcasted_iota(jnp.int32, (ts, POOL_W), 1)
    psum = jnp.where(col < POOL_GW, s2[32:n],
                     jnp.where(col < 2 * POOL_GW, s4[32:n], jnp.where(col < 3 * POOL_GW, s8[32:n], s16)))
    return psum * inv_count - ebuf[32:n]


GROUP_WIN = ((0, 0), (128, 64), (384, 0), (512, 64))
WIN = 256


def _window_weights(w_grp):
    return jnp.stack([jnp.pad(w_grp[g], ((off, WIN - off - POOL_GW), (off, WIN - off - POOL_GW)))
                      for g, (_, off) in enumerate(GROUP_WIN)])


def _group_matmul(xb, wwin_ref, transposed=False):
    outs = []
    for g, (start, _) in enumerate(GROUP_WIN):
        xw = xb[:, start:start + WIN]
        outs.append(_dot_nt(xw, wwin_ref[g]) if transposed else _dot(xw, wwin_ref[g]))
    half = WIN // 2
    return jnp.concatenate([outs[0][:, :half], outs[0][:, half:] + outs[1][:, :half], outs[1][:, half:],
                            outs[2][:, :half], outs[2][:, half:] + outs[3][:, :half], outs[3][:, half:]], axis=1)


def _mix_core(zr, pooled, outs, lses, wwin_ref, scale, wao, wpo):
    mixed = _group_matmul(pooled.astype(BF16), wwin_ref)
    p = mixed * scale
    l0, l1, l2 = lses
    mx = jnp.maximum(jnp.maximum(l0, l1), l2)
    e0, e1, e2 = jnp.exp(l0 - mx), jnp.exp(l1 - mx), jnp.exp(l2 - mx)
    inv = 1.0 / (e0 + e1 + e2)
    wts = (e0 * inv, e1 * inv, e2 * inv)
    a = wts[0] * outs[0] + wts[1] * outs[1] + wts[2] * outs[2]
    att = _dot(a.astype(BF16), wao)
    pol = _dot(p.astype(BF16), wpo)
    sga = _sigmoid(zr[:, POOL_W:POOL_W + D_MODEL].astype(F32))
    sgp = _sigmoid(zr[:, POOL_W + D_MODEL:].astype(F32))
    mg = sga * att + sgp * pol
    return dict(mixed=mixed, p=p, wts=wts, a=a, att=att, pol=pol, sga=sga, sgp=sgp, mg=mg)


def _fill_pool_input(ebuf, zr_ref, halo_ref, t0):
    ts = zr_ref.shape[0]
    halo = halo_ref[...].astype(F32)
    t = t0 - HALO + lax.broadcasted_iota(jnp.int32, (HALO, POOL_W), 0)
    ebuf[0:HALO] = jnp.where(t >= 0, halo, 0.0)
    ebuf[HALO:HALO + ts] = zr_ref[:, 0:POOL_W].astype(F32)


def _mix_fwd(x, zr, ol_dil, wbd, scale, wao, wpo, wout, ts):
    s = x.shape[0]
    n = s // ts

    def body(x_ref, zr_ref, halo_ref, ic_ref, ol0, ol1, ol2, wbd_ref, sc_ref, wao_ref, wpo_ref, wout_ref,
             h1_ref, ebuf, s2, s4, s8, rbuf):
        i = pl.program_id(0)
        _fill_pool_input(ebuf, zr_ref, halo_ref, i * ts)
        pooled = _pool_fwd(ebuf, s2, s4, s8, ic_ref[...], ts)
        ols = _gather_rows((ol0, ol1, ol2), rbuf, ts)
        outs, lses = [a[:, :GROUP_W] for a in ols], [a[:, GROUP_W:] for a in ols]
        f = _mix_core(zr_ref[...], pooled, outs, lses, wbd_ref, sc_ref[...], wao_ref[...], wpo_ref[...])
        h1_ref[...] = x_ref[...] + _dot(f["mg"].astype(BF16), wout_ref[...])

    whole = lambda a: pl.BlockSpec(a.shape, lambda i: (0,) * a.ndim)
    idx = lambda i: i
    return pl.pallas_call(
        body, name="mix_fwd", grid=(n,),
        in_specs=[pl.BlockSpec((ts, D_MODEL), lambda i: (i, 0)),
                  pl.BlockSpec((ts, REST_W), lambda i: (i, 0)),
                  pl.BlockSpec((HALO, POOL_W), lambda i: (jnp.maximum(i * (ts // HALO) - 1, 0), 0)),
                  pl.BlockSpec((None, ts, POOL_W), lambda i: (jnp.minimum(i, 1), 0, 0))]
                 + _dil_specs(ts, 2 * GROUP_W, idx)
                 + [whole(wbd), whole(scale), whole(wao), whole(wpo), whole(wout)],
        out_specs=pl.BlockSpec((ts, D_MODEL), lambda i: (i, 0)),
        out_shape=jax.ShapeDtypeStruct((s, D_MODEL), F32),
        scratch_shapes=[pltpu.VMEM((ts + HALO, POOL_W), F32)] * 4
                       + [pltpu.VMEM((2 * GROUP_W // LANES, ts, LANES), F32)],
        compiler_params=_params(dimension_semantics=("arbitrary",)),
    )(x, zr, zr, _inverse_counts(ts), *ol_dil, wbd, scale, wao, wpo, wout)


def _mix_bwd(dh1, zr, ol_dil, wbd, scale, wao, wpo, wout, ts):
    s = dh1.shape[0]
    n = s // ts

    def body(dh_ref, zr_ref, halo_ref, ic_ref, ol0, ol1, ol2, sc_ref, wbd_hbm, wao_hbm, wpo_hbm, wout_hbm,
             dzr_ref, dc0, dc1, dc2, gsc_ref, gwout_hbm, gwao_hbm, gwpo_hbm, gwbd_hbm,
             ebuf, s2, s4, s8, gbuf, t2, t4, t8, rbuf,
             wbd_ref, wao_ref, wpo_ref, wout_ref, gwout_ref, gwao_ref, gwpo_ref, gwbd_ref, sem):
        j = pl.program_id(0)
        i = n - 1 - j
        _load_resident(j, [(wbd_hbm, wbd_ref), (wao_hbm, wao_ref), (wpo_hbm, wpo_ref), (wout_hbm, wout_ref)], sem)

        @pl.when(j == 0)
        def _():
            gwout_ref[...] = jnp.zeros_like(gwout_ref)
            gwao_ref[...] = jnp.zeros_like(gwao_ref)
            gwpo_ref[...] = jnp.zeros_like(gwpo_ref)
            gwbd_ref[...] = jnp.zeros_like(gwbd_ref)
            gsc_ref[...] = jnp.zeros_like(gsc_ref)
            gbuf[ts:ts + HALO] = jnp.zeros((HALO, POOL_W), F32)

        _fill_pool_input(ebuf, zr_ref, halo_ref, i * ts)
        inv_count = ic_ref[...]
        pooled = _pool_fwd(ebuf, s2, s4, s8, inv_count, ts)
        ols = _gather_rows((ol0, ol1, ol2), rbuf, ts)
        outs, lses = [a[:, :GROUP_W] for a in ols], [a[:, GROUP_W:] for a in ols]
        zr = zr_ref[...]
        wao, wpo, wout = wao_ref[...], wpo_ref[...], wout_ref[...]
        scale = sc_ref[...]
        f = _mix_core(zr, pooled, outs, lses, wbd_ref, scale, wao, wpo)

        dhb = dh_ref[...].astype(BF16)
        gwout_ref[...] += _dot(f["mg"].T.astype(BF16), dhb)
        dmg = _dot_nt(dhb, wout)
        sga, sgp, att, pol = f["sga"], f["sgp"], f["att"], f["pol"]
        datt = dmg * sga
        dpol = dmg * sgp
        dzr_ref[:, POOL_W:POOL_W + D_MODEL] = (dmg * att * sga * (1.0 - sga)).astype(BF16)
        dzr_ref[:, POOL_W + D_MODEL:] = (dmg * pol * sgp * (1.0 - sgp)).astype(BF16)
        dattb = datt.astype(BF16)
        dpolb = dpol.astype(BF16)
        gwao_ref[...] += _dot(f["a"].T.astype(BF16), dattb)
        gwpo_ref[...] += _dot(f["p"].T.astype(BF16), dpolb)
        da = _dot_nt(dattb, wao)
        dp = _dot_nt(dpolb, wpo)

        gsc_ref[...] += jnp.sum(f["mixed"] * dp, axis=0, keepdims=True)
        dmixed = (dp * scale).astype(BF16)
        pooled_t = pooled.T.astype(BF16)
        for g, (start, _) in enumerate(GROUP_WIN):
            gwbd_ref[g] += _dot(pooled_t[start:start + WIN, :], dmixed[:, start:start + WIN])
        dpooled = _group_matmul(dmixed, wbd_ref, transposed=True)
        gbuf[0:ts] = dpooled * inv_count
        m = ts + HALO
        t2[0:m - 8] = gbuf[0:m - 8] + gbuf[1:m - 7]
        t4[0:m - 16] = t2[0:m - 16] + t2[2:m - 14]
        t8[0:m - 24] = t4[0:m - 24] + t4[4:m - 20]
        t16 = t8[0:ts] + t8[8:ts + 8]
        col = lax.broadcasted_iota(jnp.int32, (ts, POOL_W), 1)
        back = jnp.where(col < POOL_GW, t2[0:ts],
                         jnp.where(col < 2 * POOL_GW, t4[0:ts], jnp.where(col < 3 * POOL_GW, t8[0:ts], t16)))
        dzr_ref[:, 0:POOL_W] = (back - dpooled).astype(BF16)
        gbuf[ts:ts + HALO] = gbuf[0:HALO]

        head_of = lax.broadcasted_iota(jnp.int32, (ts, GROUP_W), 1) // HEAD_DIM
        prod = da * f["a"]
        inner = jnp.zeros((ts, GROUP_W), F32)
        for h in range(4):
            hm = head_of == h
            tot = jnp.sum(jnp.where(hm, prod, 0.0), axis=1, keepdims=True)
            inner = jnp.where(hm, tot, inner)
        for g, dc_ref in enumerate((dc0, dc1, dc2)):
            both = jnp.concatenate([f["wts"][g] * da, -f["wts"][g] * inner], axis=1)
            _to_class_order(both, ATT_GROUPS[g][1], rbuf, dc_ref, 0)

        @pl.when(j == n - 1)
        def _():
            wout_ref[...] = gwout_ref[...].astype(BF16)
            wao_ref[...] = gwao_ref[...].astype(BF16)
            wpo_ref[...] = gwpo_ref[...].astype(BF16)
            tn = D_MODEL // N_DEV
            pairs = [(wout_ref, gwout_hbm), (gwbd_ref, gwbd_hbm)]
            for staged, dst in ((wao_ref, gwao_hbm), (wpo_ref, gwpo_hbm)):
                pairs += [(staged.at[:, pl.ds(k * tn, tn)], dst.at[k]) for k in range(N_DEV)]
            copies = [pltpu.make_async_copy(src, dst, sem.at[k]) for k, (src, dst) in enumerate(pairs)]
            for cp in copies:
                cp.start()
            for cp in copies:
                cp.wait()

    idx = lambda j: n - 1 - j
    dc_shapes = _dil_shapes(s, 2 * GROUP_W, F32)
    weights = (wbd, wao, wpo, wout)
    grad_shapes = [(D_MODEL, D_MODEL), (GROUP_W, D_MODEL), (POOL_W, D_MODEL), (len(GROUP_WIN), WIN, WIN)]
    tile_buf = pltpu.VMEM((ts + HALO, POOL_W), F32)
    outs = pl.pallas_call(
        body, name="mix_bwd", grid=(n,),
        in_specs=[pl.BlockSpec((ts, D_MODEL), lambda j: (idx(j), 0)),
                  pl.BlockSpec((ts, REST_W), lambda j: (idx(j), 0)),
                  pl.BlockSpec((HALO, POOL_W), lambda j: (jnp.maximum(idx(j) * (ts // HALO) - 1, 0), 0)),
                  pl.BlockSpec((None, ts, POOL_W), lambda j: (jnp.minimum(idx(j), 1), 0, 0))]
                 + _dil_specs(ts, 2 * GROUP_W, idx)
                 + [pl.BlockSpec((1, POOL_W), lambda j: (0, 0))] + [ANY] * 4,
        out_specs=[pl.BlockSpec((ts, REST_W), lambda j: (idx(j), 0))]
                  + _dil_specs(ts, 2 * GROUP_W, idx)
                  + [pl.BlockSpec((1, POOL_W), lambda j: (0, 0))] + [ANY] * 4,
        out_shape=[jax.ShapeDtypeStruct((s, REST_W), BF16)] + dc_shapes
                  + [jax.ShapeDtypeStruct((1, POOL_W), F32),
                     jax.ShapeDtypeStruct((D_MODEL, D_MODEL), BF16),
                     jax.ShapeDtypeStruct((N_DEV, GROUP_W, D_MODEL // N_DEV), BF16),
                     jax.ShapeDtypeStruct((N_DEV, POOL_W, D_MODEL // N_DEV), BF16),
                     jax.ShapeDtypeStruct(grad_shapes[3], F32)],
        scratch_shapes=[tile_buf] * 8 + [pltpu.VMEM((2 * GROUP_W // LANES, ts, LANES), F32)]
                       + [pltpu.VMEM(w.shape, BF16) for w in weights]
                       + [pltpu.VMEM(shape, F32) for shape in grad_shapes]
                       + [pltpu.SemaphoreType.DMA((2 + 2 * N_DEV,))],
        compiler_params=_params(dimension_semantics=("arbitrary",)),
    )(dh1, zr, zr, _inverse_counts(ts), *ol_dil, scale, wbd, wao, wpo, wout)
    dzr, dc_dil, g_scale = outs[0], outs[1:4], outs[4]
    g_out, g_ao, g_po, g_bd = outs[5:]
    return dzr, dc_dil, (g_out.reshape(N_DEV, D_MODEL // N_DEV, D_MODEL), g_ao, g_po, g_bd, g_scale)


FF_CHUNK = 1024


STAT_ROWS = 8


def _mlp_fwd_bwd(h1, tgt, gains, wmi, wmo, ts):
    s = h1.shape[0]
    n = s // ts
    nchunk = D_FF // FF_CHUNK

    def body(h1_ref, t_ref, g_ref, wmi_hbm, wmo_hbm, dh1_ref, mdt_ref, hd_ref, stat_ref, wmi, wmo, relu_buf, sem):
        i = pl.program_id(0)
        _load_resident(i, _column_blocks(wmi_hbm, wmi) + [(wmo_hbm, wmo)], sem)

        @pl.when(i == 0)
        def _():
            stat_ref[...] = jnp.zeros_like(stat_ref)

        h1 = h1_ref[...]
        g2 = g_ref[0:1, :]
        g3 = g_ref[1:2, :]
        r2 = _rms(h1)
        n2 = h1 * r2
        m = n2 * g2
        mb = m.astype(BF16)
        mdt_ref[0] = m.T.astype(BF16)
        h2 = h1
        for c in range(nchunk):
            cols = slice(c * FF_CHUNK, (c + 1) * FF_CHUNK)
            rl = jnp.maximum(_dot(mb, wmi[:, cols]), 0.0)
            relu_buf[:, cols] = rl
            hb = (rl * rl).astype(BF16)
            hd_ref[0, :, cols] = hb
            h2 = h2 + _dot(hb, wmo[cols, :])
        r3 = _rms(h2)
        n3 = h2 * r3
        diff = n3 * g3 - t_ref[...]
        loss = jnp.sum(0.5 * jnp.sum(diff * diff, axis=1, keepdims=True) / D_MODEL, axis=0, keepdims=True)
        dy = diff * (1.0 / D_MODEL)
        dg3 = jnp.sum(dy * n3, axis=0, keepdims=True)
        dh2 = _rms_bwd(dy * g3, n3, r3)
        dh2b = dh2.astype(BF16)
        mdt_ref[1] = dh2.T.astype(BF16)
        dm = jnp.zeros((ts, D_MODEL), F32)
        for c in range(nchunk):
            cols = slice(c * FF_CHUNK, (c + 1) * FF_CHUNK)
            dfb = (_dot_nt(dh2b, wmo[cols, :]) * (2.0 * relu_buf[:, cols])).astype(BF16)
            hd_ref[1, :, cols] = dfb
            dm = dm + _dot_nt(dfb, wmi[:, cols])
        dg2 = jnp.sum(dm * n2, axis=0, keepdims=True)
        dh1_ref[...] = dh2 + _rms_bwd(dm * g2, n2, r2)
        row = lax.broadcasted_iota(jnp.int32, (STAT_ROWS, D_MODEL), 0)
        stat_ref[...] += jnp.where(row == 0, loss, jnp.where(row == 1, dg2, jnp.where(row == 2, dg3, 0.0)))

    row = lambda w: pl.BlockSpec((ts, w), lambda i: (i, 0))
    return pl.pallas_call(
        body, name="mlp_fwd_bwd", grid=(n,),
        in_specs=[row(D_MODEL), row(D_MODEL), pl.BlockSpec((2, D_MODEL), lambda i: (0, 0)), ANY, ANY],
        out_specs=[row(D_MODEL), pl.BlockSpec((2, D_MODEL, ts), lambda i: (0, 0, i)),
                   pl.BlockSpec((2, ts, D_FF), lambda i: (0, i, 0)),
                   pl.BlockSpec((STAT_ROWS, D_MODEL), lambda i: (0, 0))],
        out_shape=[jax.ShapeDtypeStruct((s, D_MODEL), F32), jax.ShapeDtypeStruct((2, D_MODEL, s), BF16),
                   jax.ShapeDtypeStruct((2, s, D_FF), BF16), jax.ShapeDtypeStruct((STAT_ROWS, D_MODEL), F32)],
        scratch_shapes=[pltpu.VMEM((D_MODEL, D_FF), BF16), pltpu.VMEM((D_FF, D_MODEL), BF16),
                        pltpu.VMEM((ts, D_FF), F32), pltpu.SemaphoreType.DMA((N_DEV + 1,))],
        compiler_params=_params(dimension_semantics=("arbitrary",)),
    )(h1, tgt, gains, wmi, wmo)


def _in_proj_bwd(x, dh1, dzr, dqkv_dil, g1, w_in, ts):
    s = x.shape[0]
    n = s // ts

    def body(x_ref, dh_ref, dzr_ref, q0, q1, q2, g_ref, w_hbm, dx_ref, dz_ref, dg_ref, w_ref, qbuf, sem):
        i = pl.program_id(0)
        _load_resident(i, _column_blocks(w_hbm, w_ref), sem)

        @pl.when(i == 0)
        def _():
            dg_ref[...] = jnp.zeros_like(dg_ref)

        for g, dqkv in enumerate(_gather_rows((q0, q1, q2), qbuf, ts)):
            for sec in range(3):
                c0 = sec * ATT_W + g * GROUP_W
                dz_ref[:, c0:c0 + GROUP_W] = dqkv[:, sec * GROUP_W:(sec + 1) * GROUP_W].astype(BF16)
        dz_ref[:, 3 * ATT_W:] = dzr_ref[...]
        du = _dot_nt(dz_ref[...], w_ref[...])
        x = x_ref[...]
        r1 = _rms(x)
        n1 = x * r1
        g1 = g_ref[...]
        dg_ref[...] += jnp.sum(du * n1, axis=0, keepdims=True)
        dx_ref[...] = dh_ref[...] + _rms_bwd(du * g1, n1, r1)

    row = lambda w: pl.BlockSpec((ts, w), lambda i: (i, 0))
    vec = pl.BlockSpec((1, D_MODEL), lambda i: (0, 0))
    return pl.pallas_call(
        body, name="in_proj_bwd", grid=(n,),
        in_specs=[row(D_MODEL), row(D_MODEL), row(REST_W)] + _dil_specs(ts, ATT_W, lambda i: i) + [vec, ANY],
        out_specs=[row(D_MODEL), row(N_IN), vec],
        out_shape=[jax.ShapeDtypeStruct((s, D_MODEL), F32), jax.ShapeDtypeStruct((s, N_IN), BF16),
                   jax.ShapeDtypeStruct((1, D_MODEL), F32)],
        scratch_shapes=[pltpu.VMEM((D_MODEL, N_IN), BF16), pltpu.VMEM((ATT_W // LANES, ts, LANES), F32),
                        pltpu.SemaphoreType.DMA((N_DEV,))],
        compiler_params=_params(dimension_semantics=("arbitrary",)),
    )(x, dh1, dzr, *dqkv_dil, g1, w_in)


GRAD_PASS = 2


def _weight_grad(at, b, name, transpose_out, tk=2048, at_slot=None, b_slot=None, ex=None):
    m, s = at.shape[-2:]
    nn = b.shape[-1]
    tn = nn // N_DEV
    tk = min(tk, s)
    nk = s // tk
    npass = N_DEV // GRAD_PASS
    oshape = (tn, m) if transpose_out else (m, tn)
    owner = lambda jj: N_DEV - 1 - jj
    order = jnp.stack([_linear(_peer(_my_place(), owner(jj))) for jj in range(N_DEV)]).astype(jnp.int32)
    sent_in = lambda p: [jj for jj in range(N_DEV - 1) if jj // GRAD_PASS == p]

    nex = ex.nw if ex else 0

    def body(order_ref, at_ref, *refs):
        b_refs, ex_ins, (got_ref,), ex_outs, (acc, res, send_sems, recv_sems, local_sem), ex_sems = _split_refs(
            refs, (GRAD_PASS, nex, 1, nex, 5, 3 if ex else 0))
        j, k = pl.program_id(0), pl.program_id(1)
        me = _my_place()
        mine = _linear(me)
        if ex:
            _carry_start(ex, j * nk + k, ex_ins, ex_outs, ex_sems)

        def send(jj):
            return pltpu.make_async_remote_copy(
                src_ref=res.at[(jj // GRAD_PASS) % 2, jj % GRAD_PASS], dst_ref=got_ref.at[mine],
                send_sem=send_sems.at[jj], recv_sem=recv_sems.at[jj],
                device_id=_peer(me, owner(jj)), device_id_type=pl.DeviceIdType.MESH)

        @pl.when(k == 0)
        def _():
            acc[...] = jnp.zeros_like(acc)

        a = at_ref[...]
        for g in range(GRAD_PASS):
            acc[g] += _dot(a, b_refs[g][...])

        @pl.when(k == nk - 1)
        def _():
            for p in range(2, npass):
                @pl.when(j == p)
                def _():
                    for jj in sent_in(p - 2):
                        send(jj).wait_send()

            for g in range(GRAD_PASS):
                r = acc[g]
                res[j % 2, g] = (r.T if transpose_out else r).astype(BF16)
            for p in range(npass):
                @pl.when(j == p)
                def _():
                    for jj in sent_in(p):
                        send(jj).start()

            @pl.when(j == npass - 1)
            def _():
                own = pltpu.make_async_copy(res.at[(npass - 1) % 2, GRAD_PASS - 1], got_ref.at[mine], local_sem.at[0])
                own.start()
                for p in range(max(npass - 2, 0), npass):
                    for jj in sent_in(p):
                        send(jj).wait_send()
                for jj in range(N_DEV - 1):
                    send(jj).wait_recv()
                own.wait()

        if ex:
            _carry_wait(ex, j * nk + k, npass * nk - 1, ex_ins, ex_outs, ex_sems)

    if at_slot is None:
        at_spec = pl.BlockSpec((m, tk), lambda j, k, o: (0, k))
    else:
        at_spec = pl.BlockSpec((None, m, tk), lambda j, k, o: (at_slot, 0, k))
    if b_slot is None:
        b_spec = lambda g: pl.BlockSpec((tk, tn), lambda j, k, o: (k, o[GRAD_PASS * j + g]))
    else:
        b_spec = lambda g: pl.BlockSpec((None, tk, tn), lambda j, k, o: (b_slot, k, o[GRAD_PASS * j + g]))
    outs = pl.pallas_call(
        body, name=name,
        grid_spec=pltpu.PrefetchScalarGridSpec(
            num_scalar_prefetch=1, grid=(npass, nk),
            in_specs=[at_spec] + [b_spec(g) for g in range(GRAD_PASS)] + (ex.specs if ex else []),
            out_specs=[ANY] + (ex.specs if ex else []),
            scratch_shapes=[pltpu.VMEM((GRAD_PASS, m, tn), F32), pltpu.VMEM((2, GRAD_PASS) + oshape, BF16),
                            pltpu.SemaphoreType.DMA((N_DEV - 1,)), pltpu.SemaphoreType.DMA((N_DEV - 1,)),
                            pltpu.SemaphoreType.DMA((1,))] + (ex.scratch if ex else [])),
        out_shape=[jax.ShapeDtypeStruct((N_DEV,) + oshape, BF16)] + (ex.out_shape if ex else []),
        compiler_params=_params(dimension_semantics=("arbitrary", "arbitrary")),
    )(order, at, *([b] * GRAD_PASS), *(ex.arrays if ex else []))
    return (outs[0], outs[1:]) if ex else outs[0]


def _my_place():
    x, y, c = lax.axis_index("x"), lax.axis_index("y"), lax.axis_index("c")
    return x, y, c


def _peer(place, k):
    x, y, c = place
    return (1 - x if k & 4 else x, 1 - y if k & 2 else y, 1 - c if k & 1 else c)


def _linear(place):
    x, y, c = place
    return 4 * x + 2 * y + c


class _Exchange:
    passes_on = False

    def __init__(self, arrays, gather):
        self.arrays, self.gather, self.nw = list(arrays), list(gather), len(arrays)
        self.out_shape = []
        for a, g in zip(arrays, gather):
            block = a.shape if g else a.shape[1:]
            self.out_shape.append(jax.ShapeDtypeStruct((N_DEV,) + tuple(block), a.dtype))
        self.specs = [ANY] * self.nw
        self.scratch = [pltpu.SemaphoreType.DMA((self.nw, N_DEV - 1)), pltpu.SemaphoreType.DMA((self.nw, N_DEV - 1)),
                        pltpu.SemaphoreType.DMA((self.nw,))]

    def _copies(self, ins, outs, sems):
        send_sems, recv_sems, local_sems = sems
        me = _my_place()
        mine = _linear(me)
        copies = []
        for w in range(self.nw):
            src = ins[w] if self.gather[w] else ins[w].at[mine]
            copies.append(pltpu.make_async_copy(src, outs[w].at[mine], local_sems.at[w]))
        for k in range(1, N_DEV):
            peer = _peer(me, k)
            for w in range(self.nw):
                src = ins[w] if self.gather[w] else ins[w].at[_linear(peer)]
                copies.append(pltpu.make_async_remote_copy(
                    src_ref=src, dst_ref=outs[w].at[mine],
                    send_sem=send_sems.at[w, k - 1], recv_sem=recv_sems.at[w, k - 1],
                    device_id=peer, device_id_type=pl.DeviceIdType.MESH))
        return copies

    def start(self, ins, outs, sems):
        for cp in self._copies(ins, outs, sems):
            cp.start()

    def wait(self, ins, outs, sems):
        copies = self._copies(ins, outs, sems)
        for cp in copies[self.nw:]:
            cp.wait_recv()
        for cp in copies[self.nw:]:
            cp.wait_send()
        for cp in copies[:self.nw]:
            cp.wait()


class _Gather:
    passes_on = True

    def __init__(self, arrays):
        self.arrays, self.nw = list(arrays), len(arrays)
        self.out_shape = [jax.ShapeDtypeStruct((N_DEV,) + tuple(a.shape), a.dtype) for a in arrays]
        self.specs = [ANY] * self.nw
        self.scratch = [pltpu.SemaphoreType.DMA((self.nw, N_DEV - 1)), pltpu.SemaphoreType.DMA((self.nw, N_DEV - 1)),
                        pltpu.SemaphoreType.DMA((self.nw,))]

    @staticmethod
    def _places():
        x, y, c = _my_place()
        return (x, y, c), (x, y, 1 - c), [(1 - x, y), (x, 1 - y), (1 - x, 1 - y)]

    @staticmethod
    def _copy(outs, sems, w, k, block, to, src=None):
        rows = outs[w].at[_linear(block)]
        return pltpu.make_async_remote_copy(
            src_ref=rows if src is None else src, dst_ref=rows, send_sem=sems[0].at[w, k], recv_sem=sems[1].at[w, k],
            device_id=to, device_id_type=pl.DeviceIdType.MESH)

    def _first(self, ins, outs, sems, w):
        me, sibling, chips = self._places()
        return ([self._copy(outs, sems, w, 0, me, sibling, src=ins[w])]
                + [self._copy(outs, sems, w, 1 + j, me, (*chip, me[2]), src=ins[w]) for j, chip in enumerate(chips)])

    def _passed(self, outs, sems, w):
        me, sibling, chips = self._places()
        return [self._copy(outs, sems, w, 4 + j, (*chip, me[2]), sibling) for j, chip in enumerate(chips)]

    def _local(self, ins, outs, sems, w):
        return pltpu.make_async_copy(ins[w], outs[w].at[_linear(self._places()[0])], sems[2].at[w])

    def start(self, ins, outs, sems):
        for w in range(self.nw):
            self._local(ins, outs, sems, w).start()
            for cp in self._first(ins, outs, sems, w):
                cp.start()

    def pass_on(self, ins, outs, sems):
        me, sibling, chips = self._places()
        for j, chip in enumerate(chips):
            for w in range(self.nw):
                self._copy(outs, sems, w, 1 + j, (*chip, me[2]), me).wait_recv()
                self._passed(outs, sems, w)[j].start()

    def wait(self, ins, outs, sems):
        me, sibling, chips = self._places()
        for w in range(self.nw):
            self._copy(outs, sems, w, 0, sibling, me).wait_recv()
            for j, chip in enumerate(chips):
                self._copy(outs, sems, w, 4 + j, (*chip, sibling[2]), me).wait_recv()
            for cp in self._first(ins, outs, sems, w) + self._passed(outs, sems, w):
                cp.wait_send()
            self._local(ins, outs, sems, w).wait()


def _sum_parts(p_ref):
    g = p_ref[0].astype(F32)
    for j in range(1, N_DEV):
        g = g + p_ref[j].astype(F32)
    return g


def _adam_update(g, w, m, v):
    nm = ADAM_B1 * m + (1.0 - ADAM_B1) * g
    nv = ADAM_B2 * v + (1.0 - ADAM_B2) * (g * g)
    m_hat = nm / (1.0 - ADAM_B1 ** ADAM_STEP)
    v_hat = nv / (1.0 - ADAM_B2 ** ADAM_STEP)
    return -ADAM_LR * (m_hat / (jnp.sqrt(v_hat) + ADAM_EPS) + ADAM_WD * w), nm, nv


def _adamw(parts, w, m, v, name, tr):
    rows, cols = w.shape
    tr = min(tr, rows)

    def body(p_ref, w_ref, m_ref, v_ref, g_ref, d_ref, nm_ref, nv_ref):
        g = _sum_parts(p_ref)
        g_ref[...] = g
        d_ref[...], nm_ref[...], nv_ref[...] = _adam_update(g, w_ref[...], m_ref[...], v_ref[...])

    blk = pl.BlockSpec((tr, cols), lambda i: (i, 0))
    return pl.pallas_call(
        body, name=name, grid=(rows // tr,),
        in_specs=[pl.BlockSpec((N_DEV, tr, cols), lambda i: (0, i, 0)), blk, blk, blk],
        out_specs=[blk] * 4,
        out_shape=[jax.ShapeDtypeStruct((rows, cols), F32)] * 4,
        compiler_params=_params(dimension_semantics=("arbitrary",)),
    )(parts, w, m, v)


def _adamw_vectors(got_stats, got_g1, got_scale, params):
    def body(st_ref, g1_ref, sc_ref, *refs):
        ins, outs = refs[:12], refs[12:]
        stats = _sum_parts(st_ref)
        outs[0][...] = stats[0:1, 0:1]
        grads = (_sum_parts(g1_ref), _sum_parts(sc_ref), stats[1:2, :], stats[2:3, :])
        for k, g in enumerate(grads):
            w_ref, m_ref, v_ref = ins[3 * k:3 * k + 3]
            g_ref, d_ref, nm_ref, nv_ref = outs[1 + 4 * k:5 + 4 * k]
            g_ref[...] = g
            d_ref[...], nm_ref[...], nv_ref[...] = _adam_update(g, w_ref[...], m_ref[...], v_ref[...])

    flat = [a for p in params for a in p]
    out_shape = [jax.ShapeDtypeStruct((1, 1), F32)]
    for w, _, _ in params:
        out_shape += [jax.ShapeDtypeStruct(w.shape, F32)] * 4
    res = pl.pallas_call(body, name="adamw_vectors", out_shape=out_shape)(got_stats, got_g1, got_scale, *flat)
    return res[0], [res[1 + 4 * k:5 + 4 * k] for k in range(len(params))]


def kernel(x, norm_mix_g, w_in, w_att_out, w_pool_grp, pool_scale, w_pool_out, w_out, norm_mlp_g, w_mlp_in, w_mlp_out, norm_final_g, loss_target, m_norm_mix_g, m_w_in, m_w_att_out, m_w_pool_grp, m_pool_scale, m_w_pool_out, m_w_out, m_norm_mlp_g, m_w_mlp_in, m_w_mlp_out, m_norm_final_g, v_norm_mix_g, v_w_in, v_w_att_out, v_w_pool_grp, v_pool_scale, v_w_pool_out, v_w_out, v_norm_mlp_g, v_w_mlp_in, v_w_mlp_out, v_norm_final_g):
    x, tgt = x[0], loss_target[0]
    s = x.shape[0]
    g1, g2, g3 = norm_mix_g, norm_mlp_g, norm_final_g.reshape(1, D_MODEL)
    shards = [w_in[0], w_att_out[0], w_pool_out[0], w_out[0], w_mlp_in[0], w_mlp_out[0]]
    wire = [a.astype(BF16) for a in shards]
    cols = lambda a: jnp.transpose(a, (1, 0, 2)).reshape(a.shape[1], N_DEV * a.shape[2])
    rows = lambda a: a.reshape(N_DEV * a.shape[1], a.shape[2])
    wbd =_window_weights(w_pool_grp[0]).astype(BF16)

    (u, ut), (f_in,) = _rms_u(x, g1, 1024, _Gather(wire[:1]))
    (qkv0, qkv1, qkv2, zr), later = _in_proj_fwd(u, f_in, 1024, _Gather(wire[1:]))
    f_ao, f_po, f_out, f_mi, f_mo = cols(later[0]), cols(later[1]), rows(later[2]), later[3], rows(later[4])
    qkv_dil = (qkv0, qkv1, qkv2)
    flat = lambda a: a.reshape(s, a.shape[-1])
    shaped = lambda a, g: a if g == 0 else a.reshape(ATT_GROUPS[g][1], s // ATT_GROUPS[g][1], a.shape[-1])
    ol_dil = [shaped(_attn_fwd(flat(qkv_dil[g]), g), g) for g in range(N_GROUPS)]
    h1 = _mix_fwd(x, zr, ol_dil, wbd, pool_scale, f_ao, f_po, f_out, ts=512)

    dh1, mdt, hd, stats = _mlp_fwd_bwd(h1, tgt, jnp.concatenate([g2, g3], axis=0), f_mi, f_mo, ts=256)
    got = {"w_mlp_in": _weight_grad(mdt, hd, "grad_w_mlp_in", transpose_out=False, at_slot=0, b_slot=1),
           "w_mlp_out": _weight_grad(mdt, hd, "grad_w_mlp_out", transpose_out=True, at_slot=1, b_slot=0)}
    dzr, dc_dil, (g_out, g_ao, g_po, g_bd, g_scale) = _mix_bwd(
        dh1, zr, ol_dil, wbd, pool_scale, f_ao, f_po, f_out, ts=256)
    g_grp = jnp.stack([g_bd[g, off:off + POOL_GW, off:off + POOL_GW] for g, (_, off) in enumerate(GROUP_WIN)])
    early = [_Exchange([g_out], [False]), _Exchange([g_ao, g_po], [False, False]),
             _Exchange([g_grp.reshape(4 * POOL_GW, POOL_GW)], [True])]
    dqkv_dil, arrived = [], []
    for g in range(N_GROUPS):
        dqkv, rode = _attn_bwd(flat(qkv_dil[g]), flat(dc_dil[g]), g, early[g])
        dqkv_dil.append(shaped(dqkv, g))
        arrived += list(rode)
    got["w_out"], got["w_att_out"], got["w_pool_out"], got_grp = arrived
    dx, dz, dg1 = _in_proj_bwd(x, dh1, dzr, dqkv_dil, g1, f_in, ts=512)
    got["w_in"], got_vectors = _weight_grad(ut, dz, "grad_w_in", transpose_out=False,
                                            ex=_Exchange([stats, dg1, g_scale], [True] * 3))

    names = ["w_in", "w_att_out", "w_pool_out", "w_out", "w_mlp_in", "w_mlp_out"]
    ms = [m_w_in, m_w_att_out, m_w_pool_out, m_w_out, m_w_mlp_in, m_w_mlp_out]
    vs = [v_w_in, v_w_att_out, v_w_pool_out, v_w_out, v_w_mlp_in, v_w_mlp_out]
    upd = {}
    for k, name in enumerate(names):
        res = _adamw(got[name], shards[k], ms[k][0], vs[k][0], "adamw_" + name, tr=256)
        upd[name] = [a[None] for a in res]

    as_rows = lambda a: a.reshape(4 * POOL_GW, POOL_GW)
    res = _adamw(got_grp, as_rows(w_pool_grp), as_rows(m_w_pool_grp), as_rows(v_w_pool_grp), "adamw_w_pool_grp", tr=2048)
    upd["w_pool_grp"] = [a.reshape(w_pool_grp.shape) for a in res]
    vectors = {"norm_mix_g": (norm_mix_g, m_norm_mix_g, v_norm_mix_g), "pool_scale": (pool_scale, m_pool_scale, v_pool_scale),
               "norm_mlp_g": (norm_mlp_g, m_norm_mlp_g, v_norm_mlp_g),
               "norm_final_g": (norm_final_g, m_norm_final_g, v_norm_final_g)}
    loss, vector_res = _adamw_vectors(*got_vectors, [tuple(a.reshape(1, -1) for a in p) for p in vectors.values()])
    for (name, (w, _, _)), res in zip(vectors.items(), vector_res):
        upd[name] = [a.reshape(w.shape) for a in res]

    order = ["norm_mix_g", "w_in", "w_att_out", "w_pool_grp", "pool_scale", "w_pool_out", "w_out", "norm_mlp_g",
             "w_mlp_in", "w_mlp_out", "norm_final_g"]
    out = [loss.reshape(()), dx[None]]
    for q in range(4):
        out += [upd[name][q] for name in order]
    return tuple(out)
```

```python
import jax
import jax.numpy as jnp
import numpy as np
from jax import lax
from jax.experimental import pallas as pl
from jax.experimental.pallas import tpu as pltpu

F32 = jnp.float32
BF16 = jnp.bfloat16

D_MODEL = 1024
HEAD_DIM = 64
GROUP_W = 256
HEADS = GROUP_W // HEAD_DIM
ATT_GROUPS = ((128, 1), (512, 4), (2048, 16))
N_GROUPS = len(ATT_GROUPS)
BLK = 128
ATT_W = N_GROUPS * GROUP_W
POOL_WINDOWS = (2, 4, 8, 16)
POOL_GW = 192
POOL_W = len(POOL_WINDOWS) * POOL_GW
D_FF = 4096
N_IN = 5120
REST_W = N_IN - 3 * ATT_W
NORM_EPS = 1e-6
ALIBI_MAX_BIAS = 8.0
N_DEV = 8
HALO = 32

ADAM_LR = 0.001
ADAM_B1 = 0.9
ADAM_B2 = 0.999
ADAM_EPS = 1e-08
ADAM_WD = 0.01
ADAM_STEP = 10

VMEM_LIMIT = 56 * 1024 * 1024
ANY = pl.BlockSpec(memory_space=pl.ANY)


def _params(**kw):
    return pltpu.CompilerParams(vmem_limit_bytes=VMEM_LIMIT, **kw)


def _dot(a, b):
    return jnp.dot(a, b, preferred_element_type=F32)


def _dot_nt(a, b):
    return lax.dot_general(a, b, (((1,), (1,)), ((), ())), preferred_element_type=F32)


def _dot_tn(a, b):
    return lax.dot_general(a, b, (((0,), (0,)), ((), ())), preferred_element_type=F32)


def _slope(group, h):
    return 2.0 ** (-ALIBI_MAX_BIAS * (HEADS * group + h + 1.0) / (N_GROUPS * HEADS))


def _load_resident(step, pairs, sem):
    @pl.when(step == 0)
    def _():
        copies = [pltpu.make_async_copy(src, dst, sem.at[n]) for n, (src, dst) in enumerate(pairs)]
        for cp in copies:
            cp.start()
        for cp in copies:
            cp.wait()


LANES = 128


def _to_class_order(value, dil, buf, ref, col0):
    ts, w = value.shape
    if dil == 1:
        ref[:, col0:col0 + w] = value.astype(ref.dtype)
        return
    for c in range(w // LANES):
        buf[c] = value[:, c * LANES:(c + 1) * LANES]
        for r in range(dil):
            ref[r, :, col0 + c * LANES:col0 + (c + 1) * LANES] = (
                buf[c, pl.ds(r, ts // dil, stride=dil), :].astype(ref.dtype))


def _to_token_order(ref, dil, buf, ts):
    if dil == 1:
        return ref[...].astype(F32)
    w = ref.shape[-1]
    for c in range(w // LANES):
        for r in range(dil):
            buf[c, pl.ds(r, ts // dil, stride=dil), :] = ref[r, :, c * LANES:(c + 1) * LANES].astype(F32)
    return jnp.concatenate([buf[c] for c in range(w // LANES)], axis=1)


def _dil_shapes(s, width, dtype):
    return [jax.ShapeDtypeStruct((s, width) if d == 1 else (d, s // d, width), dtype) for _, d in ATT_GROUPS]


def _dil_specs(ts, width, idx):
    return [pl.BlockSpec((ts, width), lambda i: (idx(i), 0)) if d == 1 else
            pl.BlockSpec((d, ts // d, width), lambda i: (0, idx(i), 0)) for _, d in ATT_GROUPS]


def _column_blocks(stacked, dst):
    tn = stacked.shape[2]
    return [(stacked.at[j], dst.at[:, pl.ds(j * tn, tn)]) for j in range(N_DEV)]


def _sigmoid(x):
    return 0.5 * jnp.tanh(0.5 * x) + 0.5


def _rms(x):
    return lax.rsqrt(jnp.mean(x * x, axis=-1, keepdims=True) + NORM_EPS)


def _rms_bwd(dn, n, r):
    return r * (dn - n * jnp.mean(dn * n, axis=-1, keepdims=True))


def _split_refs(refs, counts):
    out, at = [], 0
    for c in counts:
        out.append(refs[at:at + c])
        at += c
    return out


def _carry_start(ex, step, ins, outs, sems):
    @pl.when(step == 0)
    def _():
        ex.start(ins, outs, sems)


def _carry_wait(ex, step, last, ins, outs, sems, pass_at=None):
    if ex.passes_on:
        @pl.when(step == (last if pass_at is None else pass_at))
        def _():
            ex.pass_on(ins, outs, sems)

    @pl.when(step == last)
    def _():
        ex.wait(ins, outs, sems)


def _rms_u(x, g1, ts, ex):
    s = x.shape[0]
    n = s // ts

    def body(*refs):
        (x_ref, g_ref), ex_ins, (u_ref, ut_ref), ex_outs, ex_sems = _split_refs(refs, (2, ex.nw, 2, ex.nw, 3))
        i = pl.program_id(0)
        _carry_start(ex, i, ex_ins, ex_outs, ex_sems)
        x = x_ref[...]
        u = x * _rms(x) * g_ref[...]
        u_ref[...] = u.astype(BF16)
        ut_ref[...] = u.T.astype(BF16)
        _carry_wait(ex, i, n - 1, ex_ins, ex_outs, ex_sems)

    outs = pl.pallas_call(
        body, name="rms_u", grid=(n,),
        in_specs=[pl.BlockSpec((ts, D_MODEL), lambda i: (i, 0)), pl.BlockSpec((1, D_MODEL), lambda i: (0, 0))]
                 + ex.specs,
        out_specs=[pl.BlockSpec((ts, D_MODEL), lambda i: (i, 0)), pl.BlockSpec((D_MODEL, ts), lambda i: (0, i))]
                  + ex.specs,
        out_shape=[jax.ShapeDtypeStruct((s, D_MODEL), BF16), jax.ShapeDtypeStruct((D_MODEL, s), BF16)] + ex.out_shape,
        scratch_shapes=ex.scratch,
        compiler_params=_params(dimension_semantics=("arbitrary",)),
    )(x, g1, *ex.arrays)
    return outs[:2], outs[2:]


def _in_proj_fwd(u, w_in, ts, ex):
    s = u.shape[0]
    n = s // ts
    dils = [d for _, d in ATT_GROUPS]

    def body(*refs):
        (u_ref, w_hbm), ex_ins, (q0_ref, q1_ref, q2_ref, zr_ref), ex_outs, (w_ref, zbuf, sem), ex_sems = (
            _split_refs(refs, (2, ex.nw, 4, ex.nw, 3, 3)))
        i = pl.program_id(0)
        _carry_start(ex, i, ex_ins, ex_outs, ex_sems)
        _load_resident(i, _column_blocks(w_hbm, w_ref), sem)
        ub = u_ref[...]
        outs = (q0_ref, q1_ref, q2_ref)
        for sec in range(3):
            for g in range(N_GROUPS):
                c0 = sec * ATT_W + g * GROUP_W
                zc = _dot(ub, w_ref[:, c0:c0 + GROUP_W])
                _to_class_order(zc, dils[g], zbuf, outs[g], sec * GROUP_W)
        for c0 in range(0, REST_W, 256):
            zr_ref[:, c0:c0 + 256] = _dot(ub, w_ref[:, 3 * ATT_W + c0:3 * ATT_W + c0 + 256]).astype(BF16)
        _carry_wait(ex, i, n - 1, ex_ins, ex_outs, ex_sems, pass_at=max(n - 3, 0))

    outs = pl.pallas_call(
        body, name="in_proj_fwd", grid=(n,),
        in_specs=[pl.BlockSpec((ts, D_MODEL), lambda i: (i, 0)), ANY] + ex.specs,
        out_specs=_dil_specs(ts, ATT_W, lambda i: i) + [pl.BlockSpec((ts, REST_W), lambda i: (i, 0))] + ex.specs,
        out_shape=_dil_shapes(s, ATT_W, BF16) + [jax.ShapeDtypeStruct((s, REST_W), BF16)] + ex.out_shape,
        scratch_shapes=[pltpu.VMEM((D_MODEL, N_IN), BF16), pltpu.VMEM((GROUP_W // LANES, ts, LANES), F32),
                        pltpu.SemaphoreType.DMA((N_DEV,))] + ex.scratch,
        compiler_params=_params(dimension_semantics=("arbitrary",)),
    )(u, w_in, *ex.arrays)
    return outs[:4], outs[4:]


def _blocks_per_step(nbc):
    return 16 if nbc % 16 == 0 else 8


STACK = HEADS * BLK


SCORE_SCALE = HEAD_DIM ** -0.5


def _band_consts(group, dil):
    row = lax.broadcasted_iota(jnp.int32, (STACK, 2 * BLK), 0)
    kj = lax.broadcasted_iota(jnp.int32, (STACK, 2 * BLK), 1)
    head = row // BLK
    steps = BLK + (row % BLK) - kj
    slope = jnp.full((STACK, 2 * BLK), _slope(group, HEADS - 1), F32)
    for h in range(HEADS - 1):
        slope = jnp.where(head == h, _slope(group, h), slope)
    in_band = (steps >= 0) & (steps <= BLK)
    return jnp.where(in_band, slope * (steps.astype(F32) * float(dil)), jnp.inf), kj


def _first_key(block, nbc, nb, b):
    if nbc % nb == 0 and b != 0:
        return None
    return jnp.where((block % nbc) != 0, 0, BLK)


def _head_of_col():
    return lax.broadcasted_iota(jnp.int32, (BLK, GROUP_W), 1) // HEAD_DIM


def _stack_heads(xb):
    head_of = _head_of_col()
    return jnp.concatenate([jnp.where(head_of == h, xb, jnp.zeros_like(xb)) for h in range(HEADS)], axis=0)


def _per_head_cols(stacked, rhs, scale_rows=None):
    lane = lax.broadcasted_iota(jnp.int32, (BLK, LANES), 1)
    halves = []
    for pair in range(HEADS // 2):
        tile = rhs[:, pair * LANES:(pair + 1) * LANES]
        parts = []
        for h in (2 * pair, 2 * pair + 1):
            part = _dot(stacked[h * BLK:(h + 1) * BLK], tile)
            parts.append(part if scale_rows is None else part * scale_rows[h * BLK:(h + 1) * BLK])
        halves.append(jnp.where(lane < HEAD_DIM, parts[0], parts[1]))
    return jnp.concatenate(halves, axis=1)


def _per_head_rows(col):
    lane = lax.broadcasted_iota(jnp.int32, (BLK, LANES), 1)
    return jnp.concatenate(
        [jnp.where(lane < HEAD_DIM, col[(2 * pair) * BLK:(2 * pair + 1) * BLK], col[(2 * pair + 1) * BLK:(2 * pair + 2) * BLK])
         for pair in range(HEADS // 2)], axis=1)


def _band_softmax(qs, kb, penalty, kj, first_key):
    sc = _dot_nt(qs, kb) - penalty
    if first_key is not None:
        sc = jnp.where(kj >= first_key, sc, -jnp.inf)
    mx = jnp.max(sc, axis=1, keepdims=True)
    e = jnp.exp(sc - mx)
    return e, mx, jnp.sum(e, axis=1, keepdims=True)


def _attn_fwd(qkv, group):
    s = qkv.shape[0]
    dil = ATT_GROUPS[group][1]
    nbc = s // (BLK * dil)
    nb = _blocks_per_step(nbc)
    tile = nb * BLK

    def body(q_ref, kc_ref, kp_ref, vc_ref, vp_ref, ol_ref, kbuf, vbuf):
        i = pl.program_id(0)
        kbuf[0:BLK] = kp_ref[...]
        kbuf[BLK:BLK + tile] = kc_ref[...]
        vbuf[0:BLK] = vp_ref[...]
        vbuf[BLK:BLK + tile] = vc_ref[...]
        penalty, kj = _band_consts(group, dil)
        blocks = range(nb)
        qss = [_stack_heads(q_ref[b * BLK:(b + 1) * BLK, :] * SCORE_SCALE) for b in blocks]
        soft = [_band_softmax(qss[b], kbuf[b * BLK:b * BLK + 2 * BLK, :], penalty, kj,
                              _first_key(i * nb + b, nbc, nb, b)) for b in blocks]
        for b in blocks:
            e, mx, den = soft[b]
            ol_ref[b * BLK:(b + 1) * BLK, 0:GROUP_W] = _per_head_cols(
                e.astype(BF16), vbuf[b * BLK:b * BLK + 2 * BLK, :], 1.0 / den)
        for b in blocks:
            e, mx, den = soft[b]
            ol_ref[b * BLK:(b + 1) * BLK, GROUP_W:] = _per_head_rows(mx + jnp.log(den))

    n = s // tile
    cur = lambda c: pl.BlockSpec((tile, GROUP_W), lambda i: (i, c))
    prev = lambda c: pl.BlockSpec((BLK, GROUP_W), lambda i: (jnp.maximum(i * nb - 1, 0), c))
    return pl.pallas_call(
        body, name=f"attn_fwd_g{group}", grid=(n,),
        in_specs=[cur(0), cur(1), prev(1), cur(2), prev(2)],
        out_specs=pl.BlockSpec((tile, 2 * GROUP_W), lambda i: (i, 0)),
        out_shape=jax.ShapeDtypeStruct((s, 2 * GROUP_W), F32),
        scratch_shapes=[pltpu.VMEM((BLK + tile, GROUP_W), BF16), pltpu.VMEM((BLK + tile, GROUP_W), BF16)],
        compiler_params=_params(dimension_semantics=("arbitrary",)),
    )(qkv, qkv, qkv, qkv, qkv)


def _attn_bwd(qkv, dc, group, ex=None):
    s = qkv.shape[0]
    dil = ATT_GROUPS[group][1]
    nbc = s // (BLK * dil)
    nb = _blocks_per_step(nbc)
    tile = nb * BLK
    n = s // tile
    nex = ex.nw if ex else 0

    def body(*refs):
        ((q_ref, kc_ref, kp_ref, vc_ref, vp_ref, do_ref, c_ref), ex_ins, (out_ref,), ex_outs,
         (kbuf, vbuf, dqpend, dkpend, dvpend), ex_sems) = _split_refs(refs, (7, nex, 1, nex, 5, 3 if ex else 0))
        i = pl.program_id(0)
        if ex:
            _carry_start(ex, i, ex_ins, ex_outs, ex_sems)

        @pl.when(i == 0)
        def _():
            dqpend[...] = jnp.zeros_like(dqpend)
            dkpend[...] = jnp.zeros_like(dkpend)
            dvpend[...] = jnp.zeros_like(dvpend)

        out_ref[:, 0:GROUP_W] = dqpend[...]
        body_rows = slice(0, tile - BLK)
        tail = slice(tile - BLK, tile)
        pends = ((dkpend, GROUP_W), (dvpend, 2 * GROUP_W))
        for pend, c0 in pends:
            out_ref[body_rows, c0:c0 + GROUP_W] = pend[body_rows, :].astype(BF16)

        @pl.when(i < n)
        def _():
            kbuf[0:BLK] = kp_ref[...]
            kbuf[BLK:BLK + tile] = kc_ref[...]
            vbuf[0:BLK] = vp_ref[...]
            vbuf[BLK:BLK + tile] = vc_ref[...]
            penalty, kj = _band_consts(group, dil)
            head_of = _head_of_col()
            blocks = range(nb)
            rows = [slice(b * BLK, (b + 1) * BLK) for b in blocks]
            kbs = [kbuf[b * BLK:b * BLK + 2 * BLK, :] for b in blocks]
            vbs = [vbuf[b * BLK:b * BLK + 2 * BLK, :] for b in blocks]
            qss = [_stack_heads(q_ref[rows[b], :] * SCORE_SCALE) for b in blocks]
            doss = [_stack_heads(do_ref[rows[b], :].astype(BF16)) for b in blocks]
            cors = []
            for b in blocks:
                cb = c_ref[rows[b], :]
                cors.append(jnp.concatenate(
                    [jnp.max(jnp.where(head_of == h, cb, -jnp.inf), axis=1, keepdims=True) for h in range(HEADS)],
                    axis=0))
            soft = [_band_softmax(qss[b], kbs[b], penalty, kj, _first_key(i * nb + b, nbc, nb, b)) for b in blocks]
            dps = [_dot_nt(doss[b], vbs[b]) for b in blocks]
            ps = [soft[b][0] * (1.0 / soft[b][2]) for b in blocks]
            dss = [(ps[b] * (dps[b] + cors[b])).astype(BF16) for b in blocks]
            for b in blocks:
                dqpend[rows[b], :] = _per_head_cols(dss[b], kbs[b] * SCORE_SCALE).astype(BF16)
            bands = [(_dot_tn(dss[b], qss[b]), _dot_tn(ps[b].astype(BF16), doss[b])) for b in blocks]
            for which, (pend, c0) in enumerate(pends):
                out_ref[tail, c0:c0 + GROUP_W] = (pend[tail, :] + bands[0][which][0:BLK]).astype(BF16)
                for b in range(nb):
                    own = bands[b][which][BLK:2 * BLK]
                    pend[b * BLK:(b + 1) * BLK, :] = own + bands[b + 1][which][0:BLK] if b + 1 < nb else own

        @pl.when(i == n)
        def _():
            for pend, c0 in pends:
                out_ref[tail, c0:c0 + GROUP_W] = pend[tail, :].astype(BF16)

        if ex:
            _carry_wait(ex, i, n, ex_ins, ex_outs, ex_sems)

    last = n - 1
    cur = lambda c: pl.BlockSpec((tile, GROUP_W), lambda i: (jnp.minimum(i, last), c))
    prev = lambda c: pl.BlockSpec(
        (BLK, GROUP_W), lambda i: (jnp.maximum(jnp.minimum(i, last) * nb - 1, 0), c))
    outs = pl.pallas_call(
        body, name=f"attn_bwd_g{group}", grid=(n + 1,),
        in_specs=[cur(0), cur(1), prev(1), cur(2), prev(2), cur(0), cur(1)] + (ex.specs if ex else []),
        out_specs=[pl.BlockSpec((tile, ATT_W), lambda i: (jnp.maximum(i - 1, 0), 0))] + (ex.specs if ex else []),
        out_shape=[jax.ShapeDtypeStruct((s, ATT_W), BF16)] + (ex.out_shape if ex else []),
        scratch_shapes=[pltpu.VMEM((BLK + tile, GROUP_W), BF16), pltpu.VMEM((BLK + tile, GROUP_W), BF16),
                        pltpu.VMEM((tile, GROUP_W), BF16),
                        pltpu.VMEM((tile, GROUP_W), F32), pltpu.VMEM((tile, GROUP_W), F32)]
                       + (ex.scratch if ex else []),
        compiler_params=_params(dimension_semantics=("arbitrary",)),
    )(qkv, qkv, qkv, qkv, qkv, dc, dc, *(ex.arrays if ex else []))
    return (outs[0], outs[1:]) if ex else outs[0]


def _gather_rows(refs, buf, ts):
    return [_to_token_order(refs[g], ATT_GROUPS[g][1], buf, ts) for g in range(N_GROUPS)]


def _inverse_counts(ts):
    win = np.repeat(np.asarray(POOL_WINDOWS), POOL_GW)
    first = np.minimum(np.arange(ts)[:, None] + 1, win[None, :])
    counts = np.stack([first, np.broadcast_to(win[None, :], (ts, POOL_W))]).astype(np.float32)
    return jnp.asarray(np.float32(1.0) / counts)


def _pool_fwd(ebuf, s2, s4, s8, inv_count, ts):
    n = ts + HALO
    s2[8:n] = ebuf[8:n] + ebuf[7:n - 1]
    s4[16:n] = s2[16:n] + s2[14:n - 2]
    s8[24:n] = s4[24:n] + s4[20:n - 4]
    s16 = s8[32:n] + s8[24:n - 8]
    col = lax.broadcasted_iota(jnp.int32, (ts, POOL_W), 1)
    psum = jnp.where(col < POOL_GW, s2[32:n],
                     jnp.where(col < 2 * POOL_GW, s4[32:n], jnp.where(col < 3 * POOL_GW, s8[32:n], s16)))
    return psum * inv_count - ebuf[32:n]


GROUP_WIN = ((0, 0), (128, 64), (384, 0), (512, 64))
WIN = 256


def _window_weights(w_grp):
    return jnp.stack([jnp.pad(w_grp[g], ((off, WIN - off - POOL_GW), (off, WIN - off - POOL_GW)))
                      for g, (_, off) in enumerate(GROUP_WIN)])


def _group_matmul(xb, wwin_ref, transposed=False):
    outs = []
    for g, (start, _) in enumerate(GROUP_WIN):
        xw = xb[:, start:start + WIN]
        outs.append(_dot_nt(xw, wwin_ref[g]) if transposed else _dot(xw, wwin_ref[g]))
    half = WIN // 2
    return jnp.concatenate([outs[0][:, :half], outs[0][:, half:] + outs[1][:, :half], outs[1][:, half:],
                            outs[2][:, :half], outs[2][:, half:] + outs[3][:, :half], outs[3][:, half:]], axis=1)


def _mix_core(zr, pooled, outs, lses, wwin_ref, scale, wao, wpo):
    mixed = _group_matmul(pooled.astype(BF16), wwin_ref)
    p = mixed * scale
    l0, l1, l2 = lses
    mx = jnp.maximum(jnp.maximum(l0, l1), l2)
    e0, e1, e2 = jnp.exp(l0 - mx), jnp.exp(l1 - mx), jnp.exp(l2 - mx)
    inv = 1.0 / (e0 + e1 + e2)
    wts = (e0 * inv, e1 * inv, e2 * inv)
    a = wts[0] * outs[0] + wts[1] * outs[1] + wts[2] * outs[2]
    att = _dot(a.astype(BF16), wao)
    pol = _dot(p.astype(BF16), wpo)
    sga = _sigmoid(zr[:, POOL_W:POOL_W + D_MODEL].astype(F32))
    sgp = _sigmoid(zr[:, POOL_W + D_MODEL:].astype(F32))
    mg = sga * att + sgp * pol
    return dict(mixed=mixed, p=p, wts=wts, a=a, att=att, pol=pol, sga=sga, sgp=sgp, mg=mg)


def _fill_pool_input(ebuf, zr_ref, halo_ref, t0):
    ts = zr_ref.shape[0]
    halo = halo_ref[...].astype(F32)
    t = t0 - HALO + lax.broadcasted_iota(jnp.int32, (HALO, POOL_W), 0)
    ebuf[0:HALO] = jnp.where(t >= 0, halo, 0.0)
    ebuf[HALO:HALO + ts] = zr_ref[:, 0:POOL_W].astype(F32)


def _mix_fwd(x, zr, ol_dil, wbd, scale, wao, wpo, wout, ts):
    s = x.shape[0]
    n = s // ts

    def body(x_ref, zr_ref, halo_ref, ic_ref, ol0, ol1, ol2, wbd_ref, sc_ref, wao_ref, wpo_ref, wout_ref,
             h1_ref, ebuf, s2, s4, s8, rbuf):
        i = pl.program_id(0)
        _fill_pool_input(ebuf, zr_ref, halo_ref, i * ts)
        pooled = _pool_fwd(ebuf, s2, s4, s8, ic_ref[...], ts)
        ols = _gather_rows((ol0, ol1, ol2), rbuf, ts)
        outs, lses = [a[:, :GROUP_W] for a in ols], [a[:, GROUP_W:] for a in ols]
        f = _mix_core(zr_ref[...], pooled, outs, lses, wbd_ref, sc_ref[...], wao_ref[...], wpo_ref[...])
        h1_ref[...] = x_ref[...] + _dot(f["mg"].astype(BF16), wout_ref[...])

    whole = lambda a: pl.BlockSpec(a.shape, lambda i: (0,) * a.ndim)
    idx = lambda i: i
    return pl.pallas_call(
        body, name="mix_fwd", grid=(n,),
        in_specs=[pl.BlockSpec((ts, D_MODEL), lambda i: (i, 0)),
                  pl.BlockSpec((ts, REST_W), lambda i: (i, 0)),
                  pl.BlockSpec((HALO, POOL_W), lambda i: (jnp.maximum(i * (ts // HALO) - 1, 0), 0)),
                  pl.BlockSpec((None, ts, POOL_W), lambda i: (jnp.minimum(i, 1), 0, 0))]
                 + _dil_specs(ts, 2 * GROUP_W, idx)
                 + [whole(wbd), whole(scale), whole(wao), whole(wpo), whole(wout)],
        out_specs=pl.BlockSpec((ts, D_MODEL), lambda i: (i, 0)),
        out_shape=jax.ShapeDtypeStruct((s, D_MODEL), F32),
        scratch_shapes=[pltpu.VMEM((ts + HALO, POOL_W), F32)] * 4
                       + [pltpu.VMEM((2 * GROUP_W // LANES, ts, LANES), F32)],
        compiler_params=_params(dimension_semantics=("arbitrary",)),
    )(x, zr, zr, _inverse_counts(ts), *ol_dil, wbd, scale, wao, wpo, wout)


def _mix_bwd(dh1, zr, ol_dil, wbd, scale, wao, wpo, wout, ts):
    s = dh1.shape[0]
    n = s // ts

    def body(dh_ref, zr_ref, halo_ref, ic_ref, ol0, ol1, ol2, sc_ref, wbd_hbm, wao_hbm, wpo_hbm, wout_hbm,
             dzr_ref, dc0, dc1, dc2, gsc_ref, gwout_hbm, gwao_hbm, gwpo_hbm, gwbd_hbm,
             ebuf, s2, s4, s8, gbuf, t2, t4, t8, rbuf,
             wbd_ref, wao_ref, wpo_ref, wout_ref, gwout_ref, gwao_ref, gwpo_ref, gwbd_ref, sem):
        j = pl.program_id(0)
        i = n - 1 - j
        _load_resident(j, [(wbd_hbm, wbd_ref), (wao_hbm, wao_ref), (wpo_hbm, wpo_ref), (wout_hbm, wout_ref)], sem)

        @pl.when(j == 0)
        def _():
            gwout_ref[...] = jnp.zeros_like(gwout_ref)
            gwao_ref[...] = jnp.zeros_like(gwao_ref)
            gwpo_ref[...] = jnp.zeros_like(gwpo_ref)
            gwbd_ref[...] = jnp.zeros_like(gwbd_ref)
            gsc_ref[...] = jnp.zeros_like(gsc_ref)
            gbuf[ts:ts + HALO] = jnp.zeros((HALO, POOL_W), F32)

        _fill_pool_input(ebuf, zr_ref, halo_ref, i * ts)
        inv_count = ic_ref[...]
        pooled = _pool_fwd(ebuf, s2, s4, s8, inv_count, ts)
        ols = _gather_rows((ol0, ol1, ol2), rbuf, ts)
        outs, lses = [a[:, :GROUP_W] for a in ols], [a[:, GROUP_W:] for a in ols]
        zr = zr_ref[...]
        wao, wpo, wout = wao_ref[...], wpo_ref[...], wout_ref[...]
        scale = sc_ref[...]
        f = _mix_core(zr, pooled, outs, lses, wbd_ref, scale, wao, wpo)

        dhb = dh_ref[...].astype(BF16)
        gwout_ref[...] += _dot(f["mg"].T.astype(BF16), dhb)
        dmg = _dot_nt(dhb, wout)
        sga, sgp, att, pol = f["sga"], f["sgp"], f["att"], f["pol"]
        datt = dmg * sga
        dpol = dmg * sgp
        dzr_ref[:, POOL_W:POOL_W + D_MODEL] = (dmg * att * sga * (1.0 - sga)).astype(BF16)
        dzr_ref[:, POOL_W + D_MODEL:] = (dmg * pol * sgp * (1.0 - sgp)).astype(BF16)
        dattb = datt.astype(BF16)
        dpolb = dpol.astype(BF16)
        gwao_ref[...] += _dot(f["a"].T.astype(BF16), dattb)
        gwpo_ref[...] += _dot(f["p"].T.astype(BF16), dpolb)
        da = _dot_nt(dattb, wao)
        dp = _dot_nt(dpolb, wpo)

        gsc_ref[...] += jnp.sum(f["mixed"] * dp, axis=0, keepdims=True)
        dmixed = (dp * scale).astype(BF16)
        pooled_t = pooled.T.astype(BF16)
        for g, (start, _) in enumerate(GROUP_WIN):
            gwbd_ref[g] += _dot(pooled_t[start:start + WIN, :], dmixed[:, start:start + WIN])
        dpooled = _group_matmul(dmixed, wbd_ref, transposed=True)
        gbuf[0:ts] = dpooled * inv_count
        m = ts + HALO
        t2[0:m - 8] = gbuf[0:m - 8] + gbuf[1:m - 7]
        t4[0:m - 16] = t2[0:m - 16] + t2[2:m - 14]
        t8[0:m - 24] = t4[0:m - 24] + t4[4:m - 20]
        t16 = t8[0:ts] + t8[8:ts + 8]
        col = lax.broadcasted_iota(jnp.int32, (ts, POOL_W), 1)
        back = jnp.where(col < POOL_GW, t2[0:ts],
                         jnp.where(col < 2 * POOL_GW, t4[0:ts], jnp.where(col < 3 * POOL_GW, t8[0:ts], t16)))
        dzr_ref[:, 0:POOL_W] = (back - dpooled).astype(BF16)
        gbuf[ts:ts + HALO] = gbuf[0:HALO]

        head_of = lax.broadcasted_iota(jnp.int32, (ts, GROUP_W), 1) // HEAD_DIM
        prod = da * f["a"]
        inner = jnp.zeros((ts, GROUP_W), F32)
        for h in range(4):
            hm = head_of == h
            tot = jnp.sum(jnp.where(hm, prod, 0.0), axis=1, keepdims=True)
            inner = jnp.where(hm, tot, inner)
        for g, dc_ref in enumerate((dc0, dc1, dc2)):
            both = jnp.concatenate([f["wts"][g] * da, -f["wts"][g] * inner], axis=1)
            _to_class_order(both, ATT_GROUPS[g][1], rbuf, dc_ref, 0)

        @pl.when(j == n - 1)
        def _():
            wout_ref[...] = gwout_ref[...].astype(BF16)
            wao_ref[...] = gwao_ref[...].astype(BF16)
            wpo_ref[...] = gwpo_ref[...].astype(BF16)
            tn = D_MODEL // N_DEV
            pairs = [(wout_ref, gwout_hbm), (gwbd_ref, gwbd_hbm)]
            for staged, dst in ((wao_ref, gwao_hbm), (wpo_ref, gwpo_hbm)):
                pairs += [(staged.at[:, pl.ds(k * tn, tn)], dst.at[k]) for k in range(N_DEV)]
            copies = [pltpu.make_async_copy(src, dst, sem.at[k]) for k, (src, dst) in enumerate(pairs)]
            for cp in copies:
                cp.start()
            for cp in copies:
                cp.wait()

    idx = lambda j: n - 1 - j
    dc_shapes = _dil_shapes(s, 2 * GROUP_W, F32)
    weights = (wbd, wao, wpo, wout)
    grad_shapes = [(D_MODEL, D_MODEL), (GROUP_W, D_MODEL), (POOL_W, D_MODEL), (len(GROUP_WIN), WIN, WIN)]
    tile_buf = pltpu.VMEM((ts + HALO, POOL_W), F32)
    outs = pl.pallas_call(
        body, name="mix_bwd", grid=(n,),
        in_specs=[pl.BlockSpec((ts, D_MODEL), lambda j: (idx(j), 0)),
                  pl.BlockSpec((ts, REST_W), lambda j: (idx(j), 0)),
                  pl.BlockSpec((HALO, POOL_W), lambda j: (jnp.maximum(idx(j) * (ts // HALO) - 1, 0), 0)),
                  pl.BlockSpec((None, ts, POOL_W), lambda j: (jnp.minimum(idx(j), 1), 0, 0))]
                 + _dil_specs(ts, 2 * GROUP_W, idx)
                 + [pl.BlockSpec((1, POOL_W), lambda j: (0, 0))] + [ANY] * 4,
        out_specs=[pl.BlockSpec((ts, REST_W), lambda j: (idx(j), 0))]
                  + _dil_specs(ts, 2 * GROUP_W, idx)
                  + [pl.BlockSpec((1, POOL_W), lambda j: (0, 0))] + [ANY] * 4,
        out_shape=[jax.ShapeDtypeStruct((s, REST_W), BF16)] + dc_shapes
                  + [jax.ShapeDtypeStruct((1, POOL_W), F32),
                     jax.ShapeDtypeStruct((D_MODEL, D_MODEL), BF16),
                     jax.ShapeDtypeStruct((N_DEV, GROUP_W, D_MODEL // N_DEV), BF16),
                     jax.ShapeDtypeStruct((N_DEV, POOL_W, D_MODEL // N_DEV), BF16),
                     jax.ShapeDtypeStruct(grad_shapes[3], F32)],
        scratch_shapes=[tile_buf] * 8 + [pltpu.VMEM((2 * GROUP_W // LANES, ts, LANES), F32)]
                       + [pltpu.VMEM(w.shape, BF16) for w in weights]
                       + [pltpu.VMEM(shape, F32) for shape in grad_shapes]
                       + [pltpu.SemaphoreType.DMA((2 + 2 * N_DEV,))],
        compiler_params=_params(dimension_semantics=("arbitrary",)),
    )(dh1, zr, zr, _inverse_counts(ts), *ol_dil, scale, wbd, wao, wpo, wout)
    dzr, dc_dil, g_scale = outs[0], outs[1:4], outs[4]
    g_out, g_ao, g_po, g_bd = outs[5:]
    return dzr, dc_dil, (g_out.reshape(N_DEV, D_MODEL // N_DEV, D_MODEL), g_ao, g_po, g_bd, g_scale)


FF_CHUNK = 1024


STAT_ROWS = 8


def _mlp_fwd_bwd(h1, tgt, gains, wmi, wmo, ts):
    s = h1.shape[0]
    n = s // ts
    nchunk = D_FF // FF_CHUNK

    def body(h1_ref, t_ref, g_ref, wmi_hbm, wmo_hbm, dh1_ref, mdt_ref, hd_ref, stat_ref, wmi, wmo, relu_buf, sem):
        i = pl.program_id(0)
        _load_resident(i, _column_blocks(wmi_hbm, wmi) + [(wmo_hbm, wmo)], sem)

        @pl.when(i == 0)
        def _():
            stat_ref[...] = jnp.zeros_like(stat_ref)

        h1 = h1_ref[...]
        g2 = g_ref[0:1, :]
        g3 = g_ref[1:2, :]
        r2 = _rms(h1)
        n2 = h1 * r2
        m = n2 * g2
        mb = m.astype(BF16)
        mdt_ref[0] = m.T.astype(BF16)
        h2 = h1
        for c in range(nchunk):
            cols = slice(c * FF_CHUNK, (c + 1) * FF_CHUNK)
            rl = jnp.maximum(_dot(mb, wmi[:, cols]), 0.0)
            relu_buf[:, cols] = rl
            hb = (rl * rl).astype(BF16)
            hd_ref[0, :, cols] = hb
            h2 = h2 + _dot(hb, wmo[cols, :])
        r3 = _rms(h2)
        n3 = h2 * r3
        diff = n3 * g3 - t_ref[...]
        loss = jnp.sum(0.5 * jnp.sum(diff * diff, axis=1, keepdims=True) / D_MODEL, axis=0, keepdims=True)
        dy = diff * (1.0 / D_MODEL)
        dg3 = jnp.sum(dy * n3, axis=0, keepdims=True)
        dh2 = _rms_bwd(dy * g3, n3, r3)
        dh2b = dh2.astype(BF16)
        mdt_ref[1] = dh2.T.astype(BF16)
        dm = jnp.zeros((ts, D_MODEL), F32)
        for c in range(nchunk):
            cols = slice(c * FF_CHUNK, (c + 1) * FF_CHUNK)
            dfb = (_dot_nt(dh2b, wmo[cols, :]) * (2.0 * relu_buf[:, cols])).astype(BF16)
            hd_ref[1, :, cols] = dfb
            dm = dm + _dot_nt(dfb, wmi[:, cols])
        dg2 = jnp.sum(dm * n2, axis=0, keepdims=True)
        dh1_ref[...] = dh2 + _rms_bwd(dm * g2, n2, r2)
        row = lax.broadcasted_iota(jnp.int32, (STAT_ROWS, D_MODEL), 0)
        stat_ref[...] += jnp.where(row == 0, loss, jnp.where(row == 1, dg2, jnp.where(row == 2, dg3, 0.0)))

    row = lambda w: pl.BlockSpec((ts, w), lambda i: (i, 0))
    return pl.pallas_call(
        body, name="mlp_fwd_bwd", grid=(n,),
        in_specs=[row(D_MODEL), row(D_MODEL), pl.BlockSpec((2, D_MODEL), lambda i: (0, 0)), ANY, ANY],
        out_specs=[row(D_MODEL), pl.BlockSpec((2, D_MODEL, ts), lambda i: (0, 0, i)),
                   pl.BlockSpec((2, ts, D_FF), lambda i: (0, i, 0)),
                   pl.BlockSpec((STAT_ROWS, D_MODEL), lambda i: (0, 0))],
        out_shape=[jax.ShapeDtypeStruct((s, D_MODEL), F32), jax.ShapeDtypeStruct((2, D_MODEL, s), BF16),
                   jax.ShapeDtypeStruct((2, s, D_FF), BF16), jax.ShapeDtypeStruct((STAT_ROWS, D_MODEL), F32)],
        scratch_shapes=[pltpu.VMEM((D_MODEL, D_FF), BF16), pltpu.VMEM((D_FF, D_MODEL), BF16),
                        pltpu.VMEM((ts, D_FF), F32), pltpu.SemaphoreType.DMA((N_DEV + 1,))],
        compiler_params=_params(dimension_semantics=("arbitrary",)),
    )(h1, tgt, gains, wmi, wmo)


def _in_proj_bwd(x, dh1, dzr, dqkv_dil, g1, w_in, ts):
    s = x.shape[0]
    n = s // ts

    def body(x_ref, dh_ref, dzr_ref, q0, q1, q2, g_ref, w_hbm, dx_ref, dz_ref, dg_ref, w_ref, qbuf, sem):
        i = pl.program_id(0)
        _load_resident(i, _column_blocks(w_hbm, w_ref), sem)

        @pl.when(i == 0)
        def _():
            dg_ref[...] = jnp.zeros_like(dg_ref)

        for g, dqkv in enumerate(_gather_rows((q0, q1, q2), qbuf, ts)):
            for sec in range(3):
                c0 = sec * ATT_W + g * GROUP_W
                dz_ref[:, c0:c0 + GROUP_W] = dqkv[:, sec * GROUP_W:(sec + 1) * GROUP_W].astype(BF16)
        dz_ref[:, 3 * ATT_W:] = dzr_ref[...]
        du = _dot_nt(dz_ref[...], w_ref[...])
        x = x_ref[...]
        r1 = _rms(x)
        n1 = x * r1
        g1 = g_ref[...]
        dg_ref[...] += jnp.sum(du * n1, axis=0, keepdims=True)
        dx_ref[...] = dh_ref[...] + _rms_bwd(du * g1, n1, r1)

    row = lambda w: pl.BlockSpec((ts, w), lambda i: (i, 0))
    vec = pl.BlockSpec((1, D_MODEL), lambda i: (0, 0))
    return pl.pallas_call(
        body, name="in_proj_bwd", grid=(n,),
        in_specs=[row(D_MODEL), row(D_MODEL), row(REST_W)] + _dil_specs(ts, ATT_W, lambda i: i) + [vec, ANY],
        out_specs=[row(D_MODEL), row(N_IN), vec],
        out_shape=[jax.ShapeDtypeStruct((s, D_MODEL), F32), jax.ShapeDtypeStruct((s, N_IN), BF16),
                   jax.ShapeDtypeStruct((1, D_MODEL), F32)],
        scratch_shapes=[pltpu.VMEM((D_MODEL, N_IN), BF16), pltpu.VMEM((ATT_W // LANES, ts, LANES), F32),
                        pltpu.SemaphoreType.DMA((N_DEV,))],
        compiler_params=_params(dimension_semantics=("arbitrary",)),
    )(x, dh1, dzr, *dqkv_dil, g1, w_in)


GRAD_PASS = 2


def _weight_grad(at, b, name, transpose_out, tk=2048, at_slot=None, b_slot=None, ex=None):
    m, s = at.shape[-2:]
    nn = b.shape[-1]
    tn = nn // N_DEV
    tk = min(tk, s)
    nk = s // tk
    npass = N_DEV // GRAD_PASS
    oshape = (tn, m) if transpose_out else (m, tn)
    owner = lambda jj: N_DEV - 1 - jj
    order = jnp.stack([_linear(_peer(_my_place(), owner(jj))) for jj in range(N_DEV)]).astype(jnp.int32)
    sent_in = lambda p: [jj for jj in range(N_DEV - 1) if jj // GRAD_PASS == p]

    nex = ex.nw if ex else 0

    def body(order_ref, at_ref, *refs):
        b_refs, ex_ins, (got_ref,), ex_outs, (acc, res, send_sems, recv_sems, local_sem), ex_sems = _split_refs(
            refs, (GRAD_PASS, nex, 1, nex, 5, 3 if ex else 0))
        j, k = pl.program_id(0), pl.program_id(1)
        me = _my_place()
        mine = _linear(me)
        if ex:
            _carry_start(ex, j * nk + k, ex_ins, ex_outs, ex_sems)

        def send(jj):
            return pltpu.make_async_remote_copy(
                src_ref=res.at[(jj // GRAD_PASS) % 2, jj % GRAD_PASS], dst_ref=got_ref.at[mine],
                send_sem=send_sems.at[jj], recv_sem=recv_sems.at[jj],
                device_id=_peer(me, owner(jj)), device_id_type=pl.DeviceIdType.MESH)

        @pl.when(k == 0)
        def _():
            acc[...] = jnp.zeros_like(acc)

        a = at_ref[...]
        for g in range(GRAD_PASS):
            acc[g] += _dot(a, b_refs[g][...])

        @pl.when(k == nk - 1)
        def _():
            for p in range(2, npass):
                @pl.when(j == p)
                def _():
                    for jj in sent_in(p - 2):
                        send(jj).wait_send()

            for g in range(GRAD_PASS):
                r = acc[g]
                res[j % 2, g] = (r.T if transpose_out else r).astype(BF16)
            for p in range(npass):
                @pl.when(j == p)
                def _():
                    for jj in sent_in(p):
                        send(jj).start()

            @pl.when(j == npass - 1)
            def _():
                own = pltpu.make_async_copy(res.at[(npass - 1) % 2, GRAD_PASS - 1], got_ref.at[mine], local_sem.at[0])
                own.start()
                for p in range(max(npass - 2, 0), npass):
                    for jj in sent_in(p):
                        send(jj).wait_send()
                for jj in range(N_DEV - 1):
                    send(jj).wait_recv()
                own.wait()

        if ex:
            _carry_wait(ex, j * nk + k, npass * nk - 1, ex_ins, ex_outs, ex_sems)

    if at_slot is None:
        at_spec = pl.BlockSpec((m, tk), lambda j, k, o: (0, k))
    else:
        at_spec = pl.BlockSpec((None, m, tk), lambda j, k, o: (at_slot, 0, k))
    if b_slot is None:
        b_spec = lambda g: pl.BlockSpec((tk, tn), lambda j, k, o: (k, o[GRAD_PASS * j + g]))
    else:
        b_spec = lambda g: pl.BlockSpec((None, tk, tn), lambda j, k, o: (b_slot, k, o[GRAD_PASS * j + g]))
    outs = pl.pallas_call(
        body, name=name,
        grid_spec=pltpu.PrefetchScalarGridSpec(
            num_scalar_prefetch=1, grid=(npass, nk),
            in_specs=[at_spec] + [b_spec(g) for g in range(GRAD_PASS)] + (ex.specs if ex else []),
            out_specs=[ANY] + (ex.specs if ex else []),
            scratch_shapes=[pltpu.VMEM((GRAD_PASS, m, tn), F32), pltpu.VMEM((2, GRAD_PASS) + oshape, BF16),
                            pltpu.SemaphoreType.DMA((N_DEV - 1,)), pltpu.SemaphoreType.DMA((N_DEV - 1,)),
                            pltpu.SemaphoreType.DMA((1,))] + (ex.scratch if ex else [])),
        out_shape=[jax.ShapeDtypeStruct((N_DEV,) + oshape, BF16)] + (ex.out_shape if ex else []),
        compiler_params=_params(dimension_semantics=("arbitrary", "arbitrary")),
    )(order, at, *([b] * GRAD_PASS), *(ex.arrays if ex else []))
    return (outs[0], outs[1:]) if ex else outs[0]


def _my_place():
    x, y, c = lax.axis_index("x"), lax.axis_index("y"), lax.axis_index("c")
    return x, y, c


def _peer(place, k):
    x, y, c = place
    return (1 - x if k & 4 else x, 1 - y if k & 2 else y, 1 - c if k & 1 else c)


def _linear(place):
    x, y, c = place
    return 4 * x + 2 * y + c


class _Exchange:
    passes_on = False

    def __init__(self, arrays, gather):
        self.arrays, self.gather, self.nw = list(arrays), list(gather), len(arrays)
        self.out_shape = []
        for a, g in zip(arrays, gather):
            block = a.shape if g else a.shape[1:]
            self.out_shape.append(jax.ShapeDtypeStruct((N_DEV,) + tuple(block), a.dtype))
        self.specs = [ANY] * self.nw
        self.scratch = [pltpu.SemaphoreType.DMA((self.nw, N_DEV - 1)), pltpu.SemaphoreType.DMA((self.nw, N_DEV - 1)),
                        pltpu.SemaphoreType.DMA((self.nw,))]

    def _copies(self, ins, outs, sems):
        send_sems, recv_sems, local_sems = sems
        me = _my_place()
        mine = _linear(me)
        copies = []
        for w in range(self.nw):
            src = ins[w] if self.gather[w] else ins[w].at[mine]
            copies.append(pltpu.make_async_copy(src, outs[w].at[mine], local_sems.at[w]))
        for k in range(1, N_DEV):
            peer = _peer(me, k)
            for w in range(self.nw):
                src = ins[w] if self.gather[w] else ins[w].at[_linear(peer)]
                copies.append(pltpu.make_async_remote_copy(
                    src_ref=src, dst_ref=outs[w].at[mine],
                    send_sem=send_sems.at[w, k - 1], recv_sem=recv_sems.at[w, k - 1],
                    device_id=peer, device_id_type=pl.DeviceIdType.MESH))
        return copies

    def start(self, ins, outs, sems):
        for cp in self._copies(ins, outs, sems):
            cp.start()

    def wait(self, ins, outs, sems):
        copies = self._copies(ins, outs, sems)
        for cp in copies[self.nw:]:
            cp.wait_recv()
        for cp in copies[self.nw:]:
            cp.wait_send()
        for cp in copies[:self.nw]:
            cp.wait()


class _Gather:
    passes_on = True

    def __init__(self, arrays):
        self.arrays, self.nw = list(arrays), len(arrays)
        self.out_shape = [jax.ShapeDtypeStruct((N_DEV,) + tuple(a.shape), a.dtype) for a in arrays]
        self.specs = [ANY] * self.nw
        self.scratch = [pltpu.SemaphoreType.DMA((self.nw, N_DEV - 1)), pltpu.SemaphoreType.DMA((self.nw, N_DEV - 1)),
                        pltpu.SemaphoreType.DMA((self.nw,))]

    @staticmethod
    def _places():
        x, y, c = _my_place()
        return (x, y, c), (x, y, 1 - c), [(1 - x, y), (x, 1 - y), (1 - x, 1 - y)]

    @staticmethod
    def _copy(outs, sems, w, k, block, to, src=None):
        rows = outs[w].at[_linear(block)]
        return pltpu.make_async_remote_copy(
            src_ref=rows if src is None else src, dst_ref=rows, send_sem=sems[0].at[w, k], recv_sem=sems[1].at[w, k],
            device_id=to, device_id_type=pl.DeviceIdType.MESH)

    def _first(self, ins, outs, sems, w):
        me, sibling, chips = self._places()
        return ([self._copy(outs, sems, w, 0, me, sibling, src=ins[w])]
                + [self._copy(outs, sems, w, 1 + j, me, (*chip, me[2]), src=ins[w]) for j, chip in enumerate(chips)])

    def _passed(self, outs, sems, w):
        me, sibling, chips = self._places()
        return [self._copy(outs, sems, w, 4 + j, (*chip, me[2]), sibling) for j, chip in enumerate(chips)]

    def _local(self, ins, outs, sems, w):
        return pltpu.make_async_copy(ins[w], outs[w].at[_linear(self._places()[0])], sems[2].at[w])

    def start(self, ins, outs, sems):
        for w in range(self.nw):
            self._local(ins, outs, sems, w).start()
            for cp in self._first(ins, outs, sems, w):
                cp.start()

    def pass_on(self, ins, outs, sems):
        me, sibling, chips = self._places()
        for j, chip in enumerate(chips):
            for w in range(self.nw):
                self._copy(outs, sems, w, 1 + j, (*chip, me[2]), me).wait_recv()
                self._passed(outs, sems, w)[j].start()

    def wait(self, ins, outs, sems):
        me, sibling, chips = self._places()
        for w in range(self.nw):
            self._copy(outs, sems, w, 0, sibling, me).wait_recv()
            for j, chip in enumerate(chips):
                self._copy(outs, sems, w, 4 + j, (*chip, sibling[2]), me).wait_recv()
            for cp in self._first(ins, outs, sems, w) + self._passed(outs, sems, w):
                cp.wait_send()
            self._local(ins, outs, sems, w).wait()


def _sum_parts(p_ref):
    g = p_ref[0].astype(F32)
    for j in range(1, N_DEV):
        g = g + p_ref[j].astype(F32)
    return g


def _adam_update(g, w, m, v):
    nm = ADAM_B1 * m + (1.0 - ADAM_B1) * g
    nv = ADAM_B2 * v + (1.0 - ADAM_B2) * (g * g)
    m_hat = nm / (1.0 - ADAM_B1 ** ADAM_STEP)
    v_hat = nv / (1.0 - ADAM_B2 ** ADAM_STEP)
    return -ADAM_LR * (m_hat / (jnp.sqrt(v_hat) + ADAM_EPS) + ADAM_WD * w), nm, nv


def _adamw(parts, w, m, v, name, tr):
    rows, cols = w.shape
    tr = min(tr, rows)
    while rows % tr:
        tr //= 2
    assert tr % 8 == 0, (rows, tr)

    def body(p_ref, w_ref, m_ref, v_ref, g_ref, d_ref, nm_ref, nv_ref):
        g = _sum_parts(p_ref)
        g_ref[...] = g
        d_ref[...], nm_ref[...], nv_ref[...] = _adam_update(g, w_ref[...], m_ref[...], v_ref[...])

    blk = pl.BlockSpec((tr, cols), lambda i: (i, 0))
    return pl.pallas_call(
        body, name=name, grid=(rows // tr,),
        in_specs=[pl.BlockSpec((N_DEV, tr, cols), lambda i: (0, i, 0)), blk, blk, blk],
        out_specs=[blk] * 4,
        out_shape=[jax.ShapeDtypeStruct((rows, cols), F32)] * 4,
        compiler_params=_params(dimension_semantics=("arbitrary",)),
    )(parts, w, m, v)


def _adamw_vectors(got_stats, got_g1, got_scale, params):
    def body(st_ref, g1_ref, sc_ref, *refs):
        ins, outs = refs[:12], refs[12:]
        stats = _sum_parts(st_ref)
        outs[0][...] = stats[0:1, 0:1]
        grads = (_sum_parts(g1_ref), _sum_parts(sc_ref), stats[1:2, :], stats[2:3, :])
        for k, g in enumerate(grads):
            w_ref, m_ref, v_ref = ins[3 * k:3 * k + 3]
            g_ref, d_ref, nm_ref, nv_ref = outs[1 + 4 * k:5 + 4 * k]
            g_ref[...] = g
            d_ref[...], nm_ref[...], nv_ref[...] = _adam_update(g, w_ref[...], m_ref[...], v_ref[...])

    flat = [a for p in params for a in p]
    out_shape = [jax.ShapeDtypeStruct((1, 1), F32)]
    for w, _, _ in params:
        out_shape += [jax.ShapeDtypeStruct(w.shape, F32)] * 4
    res = pl.pallas_call(body, name="adamw_vectors", out_shape=out_shape)(got_stats, got_g1, got_scale, *flat)
    return res[0], [res[1 + 4 * k:5 + 4 * k] for k in range(len(params))]


def kernel(x, norm_mix_g, w_in, w_att_out, w_pool_grp, pool_scale, w_pool_out, w_out, norm_mlp_g, w_mlp_in, w_mlp_out, norm_final_g, loss_target, m_norm_mix_g, m_w_in, m_w_att_out, m_w_pool_grp, m_pool_scale, m_w_pool_out, m_w_out, m_norm_mlp_g, m_w_mlp_in, m_w_mlp_out, m_norm_final_g, v_norm_mix_g, v_w_in, v_w_att_out, v_w_pool_grp, v_pool_scale, v_w_pool_out, v_w_out, v_norm_mlp_g, v_w_mlp_in, v_w_mlp_out, v_norm_final_g):
    x, tgt = x[0], loss_target[0]
    s = x.shape[0]
    g1, g2, g3 = norm_mix_g, norm_mlp_g, norm_final_g.reshape(1, D_MODEL)
    shards = [w_in[0], w_att_out[0], w_pool_out[0], w_out[0], w_mlp_in[0], w_mlp_out[0]]
    wire = [a.astype(BF16) for a in shards]
    cols = lambda a: jnp.transpose(a, (1, 0, 2)).reshape(a.shape[1], N_DEV * a.shape[2])
    rows = lambda a: a.reshape(N_DEV * a.shape[1], a.shape[2])
    wbd =_window_weights(w_pool_grp[0]).astype(BF16)

    (u, ut), (f_in,) = _rms_u(x, g1, 1024, _Gather(wire[:1]))
    (qkv0, qkv1, qkv2, zr), later = _in_proj_fwd(u, f_in, 1024, _Gather(wire[1:]))
    f_ao, f_po, f_out, f_mi, f_mo = cols(later[0]), cols(later[1]), rows(later[2]), later[3], rows(later[4])
    qkv_dil = (qkv0, qkv1, qkv2)
    flat = lambda a: a.reshape(s, a.shape[-1])
    shaped = lambda a, g: a if g == 0 else a.reshape(ATT_GROUPS[g][1], s // ATT_GROUPS[g][1], a.shape[-1])
    ol_dil = [shaped(_attn_fwd(flat(qkv_dil[g]), g), g) for g in range(N_GROUPS)]
    h1 = _mix_fwd(x, zr, ol_dil, wbd, pool_scale, f_ao, f_po, f_out, ts=512)

    dh1, mdt, hd, stats = _mlp_fwd_bwd(h1, tgt, jnp.concatenate([g2, g3], axis=0), f_mi, f_mo, ts=256)
    got = {"w_mlp_in": _weight_grad(mdt, hd, "grad_w_mlp_in", transpose_out=False, at_slot=0, b_slot=1),
           "w_mlp_out": _weight_grad(mdt, hd, "grad_w_mlp_out", transpose_out=True, at_slot=1, b_slot=0)}
    dzr, dc_dil, (g_out, g_ao, g_po, g_bd, g_scale) = _mix_bwd(
        dh1, zr, ol_dil, wbd, pool_scale, f_ao, f_po, f_out, ts=256)
    g_grp = jnp.stack([g_bd[g, off:off + POOL_GW, off:off + POOL_GW] for g, (_, off) in enumerate(GROUP_WIN)])
    early = [_Exchange([g_out], [False]), _Exchange([g_ao, g_po], [False, False]),
             _Exchange([g_grp.reshape(4 * POOL_GW, POOL_GW)], [True])]
    dqkv_dil, arrived = [], []
    for g in range(N_GROUPS):
        dqkv, rode = _attn_bwd(flat(qkv_dil[g]), flat(dc_dil[g]), g, early[g])
        dqkv_dil.append(shaped(dqkv, g))
        arrived += list(rode)
    got["w_out"], got["w_att_out"], got["w_pool_out"], got_grp = arrived
    dx, dz, dg1 = _in_proj_bwd(x, dh1, dzr, dqkv_dil, g1, f_in, ts=512)
    got["w_in"], got_vectors = _weight_grad(ut, dz, "grad_w_in", transpose_out=False,
                                            ex=_Exchange([stats, dg1, g_scale], [True] * 3))

    names = ["w_in", "w_att_out", "w_pool_out", "w_out", "w_mlp_in", "w_mlp_out"]
    ms = [m_w_in, m_w_att_out, m_w_pool_out, m_w_out, m_w_mlp_in, m_w_mlp_out]
    vs = [v_w_in, v_w_att_out, v_w_pool_out, v_w_out, v_w_mlp_in, v_w_mlp_out]
    upd = {}
    for k, name in enumerate(names):
        res = _adamw(got[name], shards[k], ms[k][0], vs[k][0], "adamw_" + name, tr=512)
        upd[name] = [a[None] for a in res]

    as_rows = lambda a: a.reshape(4 * POOL_GW, POOL_GW)
    res = _adamw(got_grp, as_rows(w_pool_grp), as_rows(m_w_pool_grp), as_rows(v_w_pool_grp), "adamw_w_pool_grp", tr=2048)
    upd["w_pool_grp"] = [a.reshape(w_pool_grp.shape) for a in res]
    vectors = {"norm_mix_g": (norm_mix_g, m_norm_mix_g, v_norm_mix_g), "pool_scale": (pool_scale, m_pool_scale, v_pool_scale),
               "norm_mlp_g": (norm_mlp_g, m_norm_mlp_g, v_norm_mlp_g),
               "norm_final_g": (norm_final_g, m_norm_final_g, v_norm_final_g)}
    loss, vector_res = _adamw_vectors(*got_vectors, [tuple(a.reshape(1, -1) for a in p) for p in vectors.values()])
    for (name, (w, _, _)), res in zip(vectors.items(), vector_res):
        upd[name] = [a.reshape(w.shape) for a in res]

    order = ["norm_mix_g", "w_in", "w_att_out", "w_pool_grp", "pool_scale", "w_pool_out", "w_out", "norm_mlp_g",
             "w_mlp_in", "w_mlp_out", "norm_final_g"]
    out = [loss.reshape(()), dx[None]]
    for q in range(4):
        out += [upd[name][q] for name in order]
    return tuple(out)
```

```python
import jax
import jax.numpy as jnp
import numpy as np
from jax import lax
from jax.experimental import pallas as pl
from jax.experimental.pallas import tpu as pltpu

F32 = jnp.float32
BF16 = jnp.bfloat16

D_MODEL = 1024
HEAD_DIM = 64
GROUP_W = 256
HEADS = GROUP_W // HEAD_DIM
ATT_GROUPS = ((128, 1), (512, 4), (2048, 16))
N_GROUPS = len(ATT_GROUPS)
BLK = 128
ATT_W = N_GROUPS * GROUP_W
POOL_WINDOWS = (2, 4, 8, 16)
POOL_GW = 192
POOL_W = len(POOL_WINDOWS) * POOL_GW
D_FF = 4096
N_IN = 5120
REST_W = N_IN - 3 * ATT_W
NORM_EPS = 1e-6
ALIBI_MAX_BIAS = 8.0
N_DEV = 8
HALO = 32

ADAM_LR = 0.001
ADAM_B1 = 0.9
ADAM_B2 = 0.999
ADAM_EPS = 1e-08
ADAM_WD = 0.01
ADAM_STEP = 10

VMEM_LIMIT = 56 * 1024 * 1024
ANY = pl.BlockSpec(memory_space=pl.ANY)


def _params(**kw):
    return pltpu.CompilerParams(vmem_limit_bytes=VMEM_LIMIT, **kw)


def _dot(a, b):
    return jnp.dot(a, b, preferred_element_type=F32)


def _dot_nt(a, b):
    return lax.dot_general(a, b, (((1,), (1,)), ((), ())), preferred_element_type=F32)


def _dot_tn(a, b):
    return lax.dot_general(a, b, (((0,), (0,)), ((), ())), preferred_element_type=F32)


def _slope(group, h):
    return 2.0 ** (-ALIBI_MAX_BIAS * (HEADS * group + h + 1.0) / (N_GROUPS * HEADS))


def _load_resident(step, pairs, sem):
    @pl.when(step == 0)
    def _():
        copies = [pltpu.make_async_copy(src, dst, sem.at[n]) for n, (src, dst) in enumerate(pairs)]
        for cp in copies:
            cp.start()
        for cp in copies:
            cp.wait()


LANES = 128


def _to_class_order(value, dil, buf, ref, col0):
    ts, w = value.shape
    if dil == 1:
        ref[:, col0:col0 + w] = value.astype(ref.dtype)
        return
    for c in range(w // LANES):
        buf[c] = value[:, c * LANES:(c + 1) * LANES]
        for r in range(dil):
            ref[r, :, col0 + c * LANES:col0 + (c + 1) * LANES] = (
                buf[c, pl.ds(r, ts // dil, stride=dil), :].astype(ref.dtype))


def _to_token_order(ref, dil, buf, ts):
    if dil == 1:
        return ref[...].astype(F32)
    w = ref.shape[-1]
    for c in range(w // LANES):
        for r in range(dil):
            buf[c, pl.ds(r, ts // dil, stride=dil), :] = ref[r, :, c * LANES:(c + 1) * LANES].astype(F32)
    return jnp.concatenate([buf[c] for c in range(w // LANES)], axis=1)


def _dil_shapes(s, width, dtype):
    return [jax.ShapeDtypeStruct((s, width) if d == 1 else (d, s // d, width), dtype) for _, d in ATT_GROUPS]


def _dil_specs(ts, width, idx):
    return [pl.BlockSpec((ts, width), lambda i: (idx(i), 0)) if d == 1 else
            pl.BlockSpec((d, ts // d, width), lambda i: (0, idx(i), 0)) for _, d in ATT_GROUPS]


def _column_blocks(stacked, dst):
    tn = stacked.shape[2]
    return [(stacked.at[j], dst.at[:, pl.ds(j * tn, tn)]) for j in range(N_DEV)]


def _sigmoid(x):
    return 0.5 * jnp.tanh(0.5 * x) + 0.5


def _rms(x):
    return lax.rsqrt(jnp.mean(x * x, axis=-1, keepdims=True) + NORM_EPS)


def _rms_bwd(dn, n, r):
    return r * (dn - n * jnp.mean(dn * n, axis=-1, keepdims=True))


def _split_refs(refs, counts):
    out, at = [], 0
    for c in counts:
        out.append(refs[at:at + c])
        at += c
    return out


def _carry_start(ex, step, ins, outs, sems):
    @pl.when(step == 0)
    def _():
        ex.start(ins, outs, sems)


def _carry_wait(ex, step, last, ins, outs, sems, pass_at=None):
    if ex.passes_on:
        @pl.when(step == (last if pass_at is None else pass_at))
        def _():
            ex.pass_on(ins, outs, sems)

    @pl.when(step == last)
    def _():
        ex.wait(ins, outs, sems)


def _rms_u(x, g1, ts, ex):
    s = x.shape[0]
    n = s // ts

    def body(*refs):
        (x_ref, g_ref), ex_ins, (u_ref, ut_ref), ex_outs, ex_sems = _split_refs(refs, (2, ex.nw, 2, ex.nw, 3))
        i = pl.program_id(0)
        _carry_start(ex, i, ex_ins, ex_outs, ex_sems)
        x = x_ref[...]
        u = x * _rms(x) * g_ref[...]
        u_ref[...] = u.astype(BF16)
        ut_ref[...] = u.T.astype(BF16)
        _carry_wait(ex, i, n - 1, ex_ins, ex_outs, ex_sems)

    outs = pl.pallas_call(
        body, name="rms_u", grid=(n,),
        in_specs=[pl.BlockSpec((ts, D_MODEL), lambda i: (i, 0)), pl.BlockSpec((1, D_MODEL), lambda i: (0, 0))]
                 + ex.specs,
        out_specs=[pl.BlockSpec((ts, D_MODEL), lambda i: (i, 0)), pl.BlockSpec((D_MODEL, ts), lambda i: (0, i))]
                  + ex.specs,
        out_shape=[jax.ShapeDtypeStruct((s, D_MODEL), BF16), jax.ShapeDtypeStruct((D_MODEL, s), BF16)] + ex.out_shape,
        scratch_shapes=ex.scratch,
        compiler_params=_params(dimension_semantics=("arbitrary",)),
    )(x, g1, *ex.arrays)
    return outs[:2], outs[2:]


def _in_proj_fwd(u, w_in, ts, ex):
    s = u.shape[0]
    n = s // ts
    dils = [d for _, d in ATT_GROUPS]

    def body(*refs):
        (u_ref, w_hbm), ex_ins, (q0_ref, q1_ref, q2_ref, zr_ref), ex_outs, (w_ref, zbuf, sem), ex_sems = (
            _split_refs(refs, (2, ex.nw, 4, ex.nw, 3, 3)))
        i = pl.program_id(0)
        _carry_start(ex, i, ex_ins, ex_outs, ex_sems)
        _load_resident(i, _column_blocks(w_hbm, w_ref), sem)
        ub = u_ref[...]
        outs = (q0_ref, q1_ref, q2_ref)
        for sec in range(3):
            for g in range(N_GROUPS):
                c0 = sec * ATT_W + g * GROUP_W
                zc = _dot(ub, w_ref[:, c0:c0 + GROUP_W])
                _to_class_order(zc, dils[g], zbuf, outs[g], sec * GROUP_W)
        for c0 in range(0, REST_W, 256):
            zr_ref[:, c0:c0 + 256] = _dot(ub, w_ref[:, 3 * ATT_W + c0:3 * ATT_W + c0 + 256]).astype(BF16)
        _carry_wait(ex, i, n - 1, ex_ins, ex_outs, ex_sems, pass_at=max(n - 3, 0))

    outs = pl.pallas_call(
        body, name="in_proj_fwd", grid=(n,),
        in_specs=[pl.BlockSpec((ts, D_MODEL), lambda i: (i, 0)), ANY] + ex.specs,
        out_specs=_dil_specs(ts, ATT_W, lambda i: i) + [pl.BlockSpec((ts, REST_W), lambda i: (i, 0))] + ex.specs,
        out_shape=_dil_shapes(s, ATT_W, BF16) + [jax.ShapeDtypeStruct((s, REST_W), BF16)] + ex.out_shape,
        scratch_shapes=[pltpu.VMEM((D_MODEL, N_IN), BF16), pltpu.VMEM((GROUP_W // LANES, ts, LANES), F32),
                        pltpu.SemaphoreType.DMA((N_DEV,))] + ex.scratch,
        compiler_params=_params(dimension_semantics=("arbitrary",)),
    )(u, w_in, *ex.arrays)
    return outs[:4], outs[4:]


def _blocks_per_step(nbc):
    return 16 if nbc % 16 == 0 else 8


STACK = HEADS * BLK


SCORE_SCALE = HEAD_DIM ** -0.5


def _band_consts(group, dil):
    row = lax.broadcasted_iota(jnp.int32, (STACK, 2 * BLK), 0)
    kj = lax.broadcasted_iota(jnp.int32, (STACK, 2 * BLK), 1)
    head = row // BLK
    steps = BLK + (row % BLK) - kj
    slope = jnp.full((STACK, 2 * BLK), _slope(group, HEADS - 1), F32)
    for h in range(HEADS - 1):
        slope = jnp.where(head == h, _slope(group, h), slope)
    in_band = (steps >= 0) & (steps <= BLK)
    return jnp.where(in_band, slope * (steps.astype(F32) * float(dil)), jnp.inf), kj


def _first_key(block, nbc, nb, b):
    if nbc % nb == 0 and b != 0:
        return None
    return jnp.where((block % nbc) != 0, 0, BLK)


def _head_of_col():
    return lax.broadcasted_iota(jnp.int32, (BLK, GROUP_W), 1) // HEAD_DIM


def _stack_heads(xb):
    head_of = _head_of_col()
    return jnp.concatenate([jnp.where(head_of == h, xb, jnp.zeros_like(xb)) for h in range(HEADS)], axis=0)


def _per_head_cols(stacked, rhs, scale_rows=None):
    lane = lax.broadcasted_iota(jnp.int32, (BLK, LANES), 1)
    halves = []
    for pair in range(HEADS // 2):
        tile = rhs[:, pair * LANES:(pair + 1) * LANES]
        parts = []
        for h in (2 * pair, 2 * pair + 1):
            part = _dot(stacked[h * BLK:(h + 1) * BLK], tile)
            parts.append(part if scale_rows is None else part * scale_rows[h * BLK:(h + 1) * BLK])
        halves.append(jnp.where(lane < HEAD_DIM, parts[0], parts[1]))
    return jnp.concatenate(halves, axis=1)


def _per_head_rows(col):
    lane = lax.broadcasted_iota(jnp.int32, (BLK, LANES), 1)
    return jnp.concatenate(
        [jnp.where(lane < HEAD_DIM, col[(2 * pair) * BLK:(2 * pair + 1) * BLK], col[(2 * pair + 1) * BLK:(2 * pair + 2) * BLK])
         for pair in range(HEADS // 2)], axis=1)


def _band_softmax(qs, kb, penalty, kj, first_key):
    sc = _dot_nt(qs, kb) - penalty
    if first_key is not None:
        sc = jnp.where(kj >= first_key, sc, -jnp.inf)
    mx = jnp.max(sc, axis=1, keepdims=True)
    e = jnp.exp(sc - mx)
    return e, mx, jnp.sum(e, axis=1, keepdims=True)


def _attn_fwd(qkv, group):
    s = qkv.shape[0]
    dil = ATT_GROUPS[group][1]
    nbc = s // (BLK * dil)
    nb = _blocks_per_step(nbc)
    tile = nb * BLK

    def body(q_ref, kc_ref, kp_ref, vc_ref, vp_ref, ol_ref, kbuf, vbuf):
        i = pl.program_id(0)
        kbuf[0:BLK] = kp_ref[...]
        kbuf[BLK:BLK + tile] = kc_ref[...]
        vbuf[0:BLK] = vp_ref[...]
        vbuf[BLK:BLK + tile] = vc_ref[...]
        penalty, kj = _band_consts(group, dil)
        blocks = range(nb)
        qss = [_stack_heads(q_ref[b * BLK:(b + 1) * BLK, :] * SCORE_SCALE) for b in blocks]
        soft = [_band_softmax(qss[b], kbuf[b * BLK:b * BLK + 2 * BLK, :], penalty, kj,
                              _first_key(i * nb + b, nbc, nb, b)) for b in blocks]
        for b in blocks:
            e, mx, den = soft[b]
            ol_ref[b * BLK:(b + 1) * BLK, 0:GROUP_W] = _per_head_cols(
                e.astype(BF16), vbuf[b * BLK:b * BLK + 2 * BLK, :], 1.0 / den)
        for b in blocks:
            e, mx, den = soft[b]
            ol_ref[b * BLK:(b + 1) * BLK, GROUP_W:] = _per_head_rows(mx + jnp.log(den))

    n = s // tile
    cur = lambda c: pl.BlockSpec((tile, GROUP_W), lambda i: (i, c))
    prev = lambda c: pl.BlockSpec((BLK, GROUP_W), lambda i: (jnp.maximum(i * nb - 1, 0), c))
    return pl.pallas_call(
        body, name=f"attn_fwd_g{group}", grid=(n,),
        in_specs=[cur(0), cur(1), prev(1), cur(2), prev(2)],
        out_specs=pl.BlockSpec((tile, 2 * GROUP_W), lambda i: (i, 0)),
        out_shape=jax.ShapeDtypeStruct((s, 2 * GROUP_W), F32),
        scratch_shapes=[pltpu.VMEM((BLK + tile, GROUP_W), BF16), pltpu.VMEM((BLK + tile, GROUP_W), BF16)],
        compiler_params=_params(dimension_semantics=("arbitrary",)),
    )(qkv, qkv, qkv, qkv, qkv)


def _attn_bwd(qkv, dc, group, ex=None):
    s = qkv.shape[0]
    dil = ATT_GROUPS[group][1]
    nbc = s // (BLK * dil)
    nb = _blocks_per_step(nbc)
    tile = nb * BLK
    n = s // tile
    nex = ex.nw if ex else 0

    def body(*refs):
        ((q_ref, kc_ref, kp_ref, vc_ref, vp_ref, do_ref, c_ref), ex_ins, (out_ref,), ex_outs,
         (kbuf, vbuf, dqpend, dkpend, dvpend), ex_sems) = _split_refs(refs, (7, nex, 1, nex, 5, 3 if ex else 0))
        i = pl.program_id(0)
        if ex:
            _carry_start(ex, i, ex_ins, ex_outs, ex_sems)

        @pl.when(i == 0)
        def _():
            dqpend[...] = jnp.zeros_like(dqpend)
            dkpend[...] = jnp.zeros_like(dkpend)
            dvpend[...] = jnp.zeros_like(dvpend)

        out_ref[:, 0:GROUP_W] = dqpend[...]
        body_rows = slice(0, tile - BLK)
        tail = slice(tile - BLK, tile)
        pends = ((dkpend, GROUP_W), (dvpend, 2 * GROUP_W))
        for pend, c0 in pends:
            out_ref[body_rows, c0:c0 + GROUP_W] = pend[body_rows, :].astype(BF16)

        @pl.when(i < n)
        def _():
            kbuf[0:BLK] = kp_ref[...]
            kbuf[BLK:BLK + tile] = kc_ref[...]
            vbuf[0:BLK] = vp_ref[...]
            vbuf[BLK:BLK + tile] = vc_ref[...]
            penalty, kj = _band_consts(group, dil)
            head_of = _head_of_col()
            blocks = range(nb)
            rows = [slice(b * BLK, (b + 1) * BLK) for b in blocks]
            kbs = [kbuf[b * BLK:b * BLK + 2 * BLK, :] for b in blocks]
            vbs = [vbuf[b * BLK:b * BLK + 2 * BLK, :] for b in blocks]
            qss = [_stack_heads(q_ref[rows[b], :] * SCORE_SCALE) for b in blocks]
            doss = [_stack_heads(do_ref[rows[b], :].astype(BF16)) for b in blocks]
            cors = []
            for b in blocks:
                cb = c_ref[rows[b], :]
                cors.append(jnp.concatenate(
                    [jnp.max(jnp.where(head_of == h, cb, -jnp.inf), axis=1, keepdims=True) for h in range(HEADS)],
                    axis=0))
            soft = [_band_softmax(qss[b], kbs[b], penalty, kj, _first_key(i * nb + b, nbc, nb, b)) for b in blocks]
            dps = [_dot_nt(doss[b], vbs[b]) for b in blocks]
            ps = [soft[b][0] * (1.0 / soft[b][2]) for b in blocks]
            dss = [(ps[b] * (dps[b] + cors[b])).astype(BF16) for b in blocks]
            for b in blocks:
                dqpend[rows[b], :] = _per_head_cols(dss[b], kbs[b] * SCORE_SCALE).astype(BF16)
            bands = [(_dot_tn(dss[b], qss[b]), _dot_tn(ps[b].astype(BF16), doss[b])) for b in blocks]
            for which, (pend, c0) in enumerate(pends):
                out_ref[tail, c0:c0 + GROUP_W] = (pend[tail, :] + bands[0][which][0:BLK]).astype(BF16)
                for b in range(nb):
                    own = bands[b][which][BLK:2 * BLK]
                    pend[b * BLK:(b + 1) * BLK, :] = own + bands[b + 1][which][0:BLK] if b + 1 < nb else own

        @pl.when(i == n)
        def _():
            for pend, c0 in pends:
                out_ref[tail, c0:c0 + GROUP_W] = pend[tail, :].astype(BF16)

        if ex:
            _carry_wait(ex, i, n, ex_ins, ex_outs, ex_sems)

    last = n - 1
    cur = lambda c: pl.BlockSpec((tile, GROUP_W), lambda i: (jnp.minimum(i, last), c))
    prev = lambda c: pl.BlockSpec(
        (BLK, GROUP_W), lambda i: (jnp.maximum(jnp.minimum(i, last) * nb - 1, 0), c))
    outs = pl.pallas_call(
        body, name=f"attn_bwd_g{group}", grid=(n + 1,),
        in_specs=[cur(0), cur(1), prev(1), cur(2), prev(2), cur(0), cur(1)] + (ex.specs if ex else []),
        out_specs=[pl.BlockSpec((tile, ATT_W), lambda i: (jnp.maximum(i - 1, 0), 0))] + (ex.specs if ex else []),
        out_shape=[jax.ShapeDtypeStruct((s, ATT_W), BF16)] + (ex.out_shape if ex else []),
        scratch_shapes=[pltpu.VMEM((BLK + tile, GROUP_W), BF16), pltpu.VMEM((BLK + tile, GROUP_W), BF16),
                        pltpu.VMEM((tile, GROUP_W), BF16),
                        pltpu.VMEM((tile, GROUP_W), F32), pltpu.VMEM((tile, GROUP_W), F32)]
                       + (ex.scratch if ex else []),
        compiler_params=_params(dimension_semantics=("arbitrary",)),
    )(qkv, qkv, qkv, qkv, qkv, dc, dc, *(ex.arrays if ex else []))
    return (outs[0], outs[1:]) if ex else outs[0]


def _gather_rows(refs, buf, ts):
    return [_to_token_order(refs[g], ATT_GROUPS[g][1], buf, ts) for g in range(N_GROUPS)]


def _inverse_counts(ts):
    win = np.repeat(np.asarray(POOL_WINDOWS), POOL_GW)
    first = np.minimum(np.arange(ts)[:, None] + 1, win[None, :])
    counts = np.stack([first, np.broadcast_to(win[None, :], (ts, POOL_W))]).astype(np.float32)
    return jnp.asarray(np.float32(1.0) / counts)


def _pool_fwd(ebuf, s2, s4, s8, inv_count, ts):
    n = ts + HALO
    s2[8:n] = ebuf[8:n] + ebuf[7:n - 1]
    s4[16:n] = s2[16:n] + s2[14:n - 2]
    s8[24:n] = s4[24:n] + s4[20:n - 4]
    s16 = s8[32:n] + s8[24:n - 8]
    col = lax.broadcasted_iota(jnp.int32, (ts, POOL_W), 1)
    psum = jnp.where(col < POOL_GW, s2[32:n],
                     jnp.where(col < 2 * POOL_GW, s4[32:n], jnp.where(col < 3 * POOL_GW, s8[32:n], s16)))
    return psum * inv_count - ebuf[32:n]


GROUP_WIN = ((0, 0), (128, 64), (384, 0), (512, 64))
WIN = 256


def _window_weights(w_grp):
    return jnp.stack([jnp.pad(w_grp[g], ((off, WIN - off - POOL_GW), (off, WIN - off - POOL_GW)))
                      for g, (_, off) in enumerate(GROUP_WIN)])


def _group_matmul(xb, wwin_ref, transposed=False):
    outs = []
    for g, (start, _) in enumerate(GROUP_WIN):
        xw = xb[:, start:start + WIN]
        outs.append(_dot_nt(xw, wwin_ref[g]) if transposed else _dot(xw, wwin_ref[g]))
    half = WIN // 2
    return jnp.concatenate([outs[0][:, :half], outs[0][:, half:] + outs[1][:, :half], outs[1][:, half:],
                            outs[2][:, :half], outs[2][:, half:] + outs[3][:, :half], outs[3][:, half:]], axis=1)


def _mix_core(zr, pooled, outs, lses, wwin_ref, scale, wao, wpo):
    mixed = _group_matmul(pooled.astype(BF16), wwin_ref)
    p = mixed * scale
    l0, l1, l2 = lses
    mx = jnp.maximum(jnp.maximum(l0, l1), l2)
    e0, e1, e2 = jnp.exp(l0 - mx), jnp.exp(l1 - mx), jnp.exp(l2 - mx)
    inv = 1.0 / (e0 + e1 + e2)
    wts = (e0 * inv, e1 * inv, e2 * inv)
    a = wts[0] * outs[0] + wts[1] * outs[1] + wts[2] * outs[2]
    att = _dot(a.astype(BF16), wao)
    pol = _dot(p.astype(BF16), wpo)
    sga = _sigmoid(zr[:, POOL_W:POOL_W + D_MODEL].astype(F32))
    sgp = _sigmoid(zr[:, POOL_W + D_MODEL:].astype(F32))
    mg = sga * att + sgp * pol
    return dict(mixed=mixed, p=p, wts=wts, a=a, att=att, pol=pol, sga=sga, sgp=sgp, mg=mg)


def _fill_pool_input(ebuf, zr_ref, halo_ref, t0):
    ts = zr_ref.shape[0]
    halo = halo_ref[...].astype(F32)
    t = t0 - HALO + lax.broadcasted_iota(jnp.int32, (HALO, POOL_W), 0)
    ebuf[0:HALO] = jnp.where(t >= 0, halo, 0.0)
    ebuf[HALO:HALO + ts] = zr_ref[:, 0:POOL_W].astype(F32)


def _mix_fwd(x, zr, ol_dil, wbd, scale, wao, wpo, wout, ts):
    s = x.shape[0]
    n = s // ts

    def body(x_ref, zr_ref, halo_ref, ic_ref, ol0, ol1, ol2, wbd_ref, sc_ref, wao_ref, wpo_ref, wout_ref,
             h1_ref, ebuf, s2, s4, s8, rbuf):
        i = pl.program_id(0)
        _fill_pool_input(ebuf, zr_ref, halo_ref, i * ts)
        pooled = _pool_fwd(ebuf, s2, s4, s8, ic_ref[...], ts)
        ols = _gather_rows((ol0, ol1, ol2), rbuf, ts)
        outs, lses = [a[:, :GROUP_W] for a in ols], [a[:, GROUP_W:] for a in ols]
        f = _mix_core(zr_ref[...], pooled, outs, lses, wbd_ref, sc_ref[...], wao_ref[...], wpo_ref[...])
        h1_ref[...] = x_ref[...] + _dot(f["mg"].astype(BF16), wout_ref[...])

    whole = lambda a: pl.BlockSpec(a.shape, lambda i: (0,) * a.ndim)
    idx = lambda i: i
    return pl.pallas_call(
        body, name="mix_fwd", grid=(n,),
        in_specs=[pl.BlockSpec((ts, D_MODEL), lambda i: (i, 0)),
                  pl.BlockSpec((ts, REST_W), lambda i: (i, 0)),
                  pl.BlockSpec((HALO, POOL_W), lambda i: (jnp.maximum(i * (ts // HALO) - 1, 0), 0)),
                  pl.BlockSpec((None, ts, POOL_W), lambda i: (jnp.minimum(i, 1), 0, 0))]
                 + _dil_specs(ts, 2 * GROUP_W, idx)
                 + [whole(wbd), whole(scale), whole(wao), whole(wpo), whole(wout)],
        out_specs=pl.BlockSpec((ts, D_MODEL), lambda i: (i, 0)),
        out_shape=jax.ShapeDtypeStruct((s, D_MODEL), F32),
        scratch_shapes=[pltpu.VMEM((ts + HALO, POOL_W), F32)] * 4
                       + [pltpu.VMEM((2 * GROUP_W // LANES, ts, LANES), F32)],
        compiler_params=_params(dimension_semantics=("arbitrary",)),
    )(x, zr, zr, _inverse_counts(ts), *ol_dil, wbd, scale, wao, wpo, wout)


def _mix_bwd(dh1, zr, ol_dil, wbd, scale, wao, wpo, wout, ts):
    s = dh1.shape[0]
    n = s // ts

    def body(dh_ref, zr_ref, halo_ref, ic_ref, ol0, ol1, ol2, sc_ref, wbd_hbm, wao_hbm, wpo_hbm, wout_hbm,
             dzr_ref, dc0, dc1, dc2, gsc_ref, gwout_hbm, gwao_hbm, gwpo_hbm, gwbd_hbm,
             ebuf, s2, s4, s8, gbuf, t2, t4, t8, rbuf,
             wbd_ref, wao_ref, wpo_ref, wout_ref, gwout_ref, gwao_ref, gwpo_ref, gwbd_ref, sem):
        j = pl.program_id(0)
        i = n - 1 - j
        _load_resident(j, [(wbd_hbm, wbd_ref), (wao_hbm, wao_ref), (wpo_hbm, wpo_ref), (wout_hbm, wout_ref)], sem)

        @pl.when(j == 0)
        def _():
            gwout_ref[...] = jnp.zeros_like(gwout_ref)
            gwao_ref[...] = jnp.zeros_like(gwao_ref)
            gwpo_ref[...] = jnp.zeros_like(gwpo_ref)
            gwbd_ref[...] = jnp.zeros_like(gwbd_ref)
            gsc_ref[...] = jnp.zeros_like(gsc_ref)
            gbuf[ts:ts + HALO] = jnp.zeros((HALO, POOL_W), F32)

        _fill_pool_input(ebuf, zr_ref, halo_ref, i * ts)
        inv_count = ic_ref[...]
        pooled = _pool_fwd(ebuf, s2, s4, s8, inv_count, ts)
        ols = _gather_rows((ol0, ol1, ol2), rbuf, ts)
        outs, lses = [a[:, :GROUP_W] for a in ols], [a[:, GROUP_W:] for a in ols]
        zr = zr_ref[...]
        wao, wpo, wout = wao_ref[...], wpo_ref[...], wout_ref[...]
        scale = sc_ref[...]
        f = _mix_core(zr, pooled, outs, lses, wbd_ref, scale, wao, wpo)

        dhb = dh_ref[...].astype(BF16)
        gwout_ref[...] += _dot(f["mg"].T.astype(BF16), dhb)
        dmg = _dot_nt(dhb, wout)
        sga, sgp, att, pol = f["sga"], f["sgp"], f["att"], f["pol"]
        datt = dmg * sga
        dpol = dmg * sgp
        dzr_ref[:, POOL_W:POOL_W + D_MODEL] = (dmg * att * sga * (1.0 - sga)).astype(BF16)
        dzr_ref[:, POOL_W + D_MODEL:] = (dmg * pol * sgp * (1.0 - sgp)).astype(BF16)
        dattb = datt.astype(BF16)
        dpolb = dpol.astype(BF16)
        gwao_ref[...] += _dot(f["a"].T.astype(BF16), dattb)
        gwpo_ref[...] += _dot(f["p"].T.astype(BF16), dpolb)
        da = _dot_nt(dattb, wao)
        dp = _dot_nt(dpolb, wpo)

        gsc_ref[...] += jnp.sum(f["mixed"] * dp, axis=0, keepdims=True)
        dmixed = (dp * scale).astype(BF16)
        pooled_t = pooled.T.astype(BF16)
        for g, (start, _) in enumerate(GROUP_WIN):
            gwbd_ref[g] += _dot(pooled_t[start:start + WIN, :], dmixed[:, start:start + WIN])
        dpooled = _group_matmul(dmixed, wbd_ref, transposed=True)
        gbuf[0:ts] = dpooled * inv_count
        m = ts + HALO
        t2[0:m - 8] = gbuf[0:m - 8] + gbuf[1:m - 7]
        t4[0:m - 16] = t2[0:m - 16] + t2[2:m - 14]
        t8[0:m - 24] = t4[0:m - 24] + t4[4:m - 20]
        t16 = t8[0:ts] + t8[8:ts + 8]
        col = lax.broadcasted_iota(jnp.int32, (ts, POOL_W), 1)
        back = jnp.where(col < POOL_GW, t2[0:ts],
                         jnp.where(col < 2 * POOL_GW, t4[0:ts], jnp.where(col < 3 * POOL_GW, t8[0:ts], t16)))
        dzr_ref[:, 0:POOL_W] = (back - dpooled).astype(BF16)
        gbuf[ts:ts + HALO] = gbuf[0:HALO]

        head_of = lax.broadcasted_iota(jnp.int32, (ts, GROUP_W), 1) // HEAD_DIM
        prod = da * f["a"]
        inner = jnp.zeros((ts, GROUP_W), F32)
        for h in range(4):
            hm = head_of == h
            tot = jnp.sum(jnp.where(hm, prod, 0.0), axis=1, keepdims=True)
            inner = jnp.where(hm, tot, inner)
        for g, dc_ref in enumerate((dc0, dc1, dc2)):
            both = jnp.concatenate([f["wts"][g] * da, -f["wts"][g] * inner], axis=1)
            _to_class_order(both, ATT_GROUPS[g][1], rbuf, dc_ref, 0)

        @pl.when(j == n - 1)
        def _():
            wout_ref[...] = gwout_ref[...].astype(BF16)
            wao_ref[...] = gwao_ref[...].astype(BF16)
            wpo_ref[...] = gwpo_ref[...].astype(BF16)
            tn = D_MODEL // N_DEV
            pairs = [(wout_ref, gwout_hbm), (gwbd_ref, gwbd_hbm)]
            for staged, dst in ((wao_ref, gwao_hbm), (wpo_ref, gwpo_hbm)):
                pairs += [(staged.at[:, pl.ds(k * tn, tn)], dst.at[k]) for k in range(N_DEV)]
            copies = [pltpu.make_async_copy(src, dst, sem.at[k]) for k, (src, dst) in enumerate(pairs)]
            for cp in copies:
                cp.start()
            for cp in copies:
                cp.wait()

    idx = lambda j: n - 1 - j
    dc_shapes = _dil_shapes(s, 2 * GROUP_W, F32)
    weights = (wbd, wao, wpo, wout)
    grad_shapes = [(D_MODEL, D_MODEL), (GROUP_W, D_MODEL), (POOL_W, D_MODEL), (len(GROUP_WIN), WIN, WIN)]
    tile_buf = pltpu.VMEM((ts + HALO, POOL_W), F32)
    outs = pl.pallas_call(
        body, name="mix_bwd", grid=(n,),
        in_specs=[pl.BlockSpec((ts, D_MODEL), lambda j: (idx(j), 0)),
                  pl.BlockSpec((ts, REST_W), lambda j: (idx(j), 0)),
                  pl.BlockSpec((HALO, POOL_W), lambda j: (jnp.maximum(idx(j) * (ts // HALO) - 1, 0), 0)),
                  pl.BlockSpec((None, ts, POOL_W), lambda j: (jnp.minimum(idx(j), 1), 0, 0))]
                 + _dil_specs(ts, 2 * GROUP_W, idx)
                 + [pl.BlockSpec((1, POOL_W), lambda j: (0, 0))] + [ANY] * 4,
        out_specs=[pl.BlockSpec((ts, REST_W), lambda j: (idx(j), 0))]
                  + _dil_specs(ts, 2 * GROUP_W, idx)
                  + [pl.BlockSpec((1, POOL_W), lambda j: (0, 0))] + [ANY] * 4,
        out_shape=[jax.ShapeDtypeStruct((s, REST_W), BF16)] + dc_shapes
                  + [jax.ShapeDtypeStruct((1, POOL_W), F32),
                     jax.ShapeDtypeStruct((D_MODEL, D_MODEL), BF16),
                     jax.ShapeDtypeStruct((N_DEV, GROUP_W, D_MODEL // N_DEV), BF16),
                     jax.ShapeDtypeStruct((N_DEV, POOL_W, D_MODEL // N_DEV), BF16),
                     jax.ShapeDtypeStruct(grad_shapes[3], F32)],
        scratch_shapes=[tile_buf] * 8 + [pltpu.VMEM((2 * GROUP_W // LANES, ts, LANES), F32)]
                       + [pltpu.VMEM(w.shape, BF16) for w in weights]
                       + [pltpu.VMEM(shape, F32) for shape in grad_shapes]
                       + [pltpu.SemaphoreType.DMA((2 + 2 * N_DEV,))],
        compiler_params=_params(dimension_semantics=("arbitrary",)),
    )(dh1, zr, zr, _inverse_counts(ts), *ol_dil, scale, wbd, wao, wpo, wout)
    dzr, dc_dil, g_scale = outs[0], outs[1:4], outs[4]
    g_out, g_ao, g_po, g_bd = outs[5:]
    return dzr, dc_dil, (g_out.reshape(N_DEV, D_MODEL // N_DEV, D_MODEL), g_ao, g_po, g_bd, g_scale)


FF_CHUNK = 1024


STAT_ROWS = 8


def _mlp_fwd_bwd(h1, tgt, gains, wmi, wmo, ts):
    s = h1.shape[0]
    n = s // ts
    nchunk = D_FF // FF_CHUNK

    def body(h1_ref, t_ref, g_ref, wmi_hbm, wmo_hbm, dh1_ref, mdt_ref, hd_ref, stat_ref, wmi, wmo, relu_buf, sem):
        i = pl.program_id(0)
        _load_resident(i, _column_blocks(wmi_hbm, wmi) + [(wmo_hbm, wmo)], sem)

        @pl.when(i == 0)
        def _():
            stat_ref[...] = jnp.zeros_like(stat_ref)

        h1 = h1_ref[...]
        g2 = g_ref[0:1, :]
        g3 = g_ref[1:2, :]
        r2 = _rms(h1)
        n2 = h1 * r2
        m = n2 * g2
        mb = m.astype(BF16)
        mdt_ref[0] = m.T.astype(BF16)
        h2 = h1
        for c in range(nchunk):
            cols = slice(c * FF_CHUNK, (c + 1) * FF_CHUNK)
            rl = jnp.maximum(_dot(mb, wmi[:, cols]), 0.0)
            relu_buf[:, cols] = rl
            hb = (rl * rl).astype(BF16)
            hd_ref[0, :, cols] = hb
            h2 = h2 + _dot(hb, wmo[cols, :])
        r3 = _rms(h2)
        n3 = h2 * r3
        diff = n3 * g3 - t_ref[...]
        loss = jnp.sum(0.5 * jnp.sum(diff * diff, axis=1, keepdims=True) / D_MODEL, axis=0, keepdims=True)
        dy = diff * (1.0 / D_MODEL)
        dg3 = jnp.sum(dy * n3, axis=0, keepdims=True)
        dh2 = _rms_bwd(dy * g3, n3, r3)
        dh2b = dh2.astype(BF16)
        mdt_ref[1] = dh2.T.astype(BF16)
        dm = jnp.zeros((ts, D_MODEL), F32)
        for c in range(nchunk):
            cols = slice(c * FF_CHUNK, (c + 1) * FF_CHUNK)
            dfb = (_dot_nt(dh2b, wmo[cols, :]) * (2.0 * relu_buf[:, cols])).astype(BF16)
            hd_ref[1, :, cols] = dfb
            dm = dm + _dot_nt(dfb, wmi[:, cols])
        dg2 = jnp.sum(dm * n2, axis=0, keepdims=True)
        dh1_ref[...] = dh2 + _rms_bwd(dm * g2, n2, r2)
        row = lax.broadcasted_iota(jnp.int32, (STAT_ROWS, D_MODEL), 0)
        stat_ref[...] += jnp.where(row == 0, loss, jnp.where(row == 1, dg2, jnp.where(row == 2, dg3, 0.0)))

    row = lambda w: pl.BlockSpec((ts, w), lambda i: (i, 0))
    return pl.pallas_call(
        body, name="mlp_fwd_bwd", grid=(n,),
        in_specs=[row(D_MODEL), row(D_MODEL), pl.BlockSpec((2, D_MODEL), lambda i: (0, 0)), ANY, ANY],
        out_specs=[row(D_MODEL), pl.BlockSpec((2, D_MODEL, ts), lambda i: (0, 0, i)),
                   pl.BlockSpec((2, ts, D_FF), lambda i: (0, i, 0)),
                   pl.BlockSpec((STAT_ROWS, D_MODEL), lambda i: (0, 0))],
        out_shape=[jax.ShapeDtypeStruct((s, D_MODEL), F32), jax.ShapeDtypeStruct((2, D_MODEL, s), BF16),
                   jax.ShapeDtypeStruct((2, s, D_FF), BF16), jax.ShapeDtypeStruct((STAT_ROWS, D_MODEL), F32)],
        scratch_shapes=[pltpu.VMEM((D_MODEL, D_FF), BF16), pltpu.VMEM((D_FF, D_MODEL), BF16),
                        pltpu.VMEM((ts, D_FF), F32), pltpu.SemaphoreType.DMA((N_DEV + 1,))],
        compiler_params=_params(dimension_semantics=("arbitrary",)),
    )(h1, tgt, gains, wmi, wmo)


def _in_proj_bwd(x, dh1, dzr, dqkv_dil, g1, w_in, ts):
    s = x.shape[0]
    n = s // ts

    def body(x_ref, dh_ref, dzr_ref, q0, q1, q2, g_ref, w_hbm, dx_ref, dz_ref, dg_ref, w_ref, qbuf, sem):
        i = pl.program_id(0)
        _load_resident(i, _column_blocks(w_hbm, w_ref), sem)

        @pl.when(i == 0)
        def _():
            dg_ref[...] = jnp.zeros_like(dg_ref)

        for g, dqkv in enumerate(_gather_rows((q0, q1, q2), qbuf, ts)):
            for sec in range(3):
                c0 = sec * ATT_W + g * GROUP_W
                dz_ref[:, c0:c0 + GROUP_W] = dqkv[:, sec * GROUP_W:(sec + 1) * GROUP_W].astype(BF16)
        dz_ref[:, 3 * ATT_W:] = dzr_ref[...]
        du = _dot_nt(dz_ref[...], w_ref[...])
        x = x_ref[...]
        r1 = _rms(x)
        n1 = x * r1
        g1 = g_ref[...]
        dg_ref[...] += jnp.sum(du * n1, axis=0, keepdims=True)
        dx_ref[...] = dh_ref[...] + _rms_bwd(du * g1, n1, r1)

    row = lambda w: pl.BlockSpec((ts, w), lambda i: (i, 0))
    vec = pl.BlockSpec((1, D_MODEL), lambda i: (0, 0))
    return pl.pallas_call(
        body, name="in_proj_bwd", grid=(n,),
        in_specs=[row(D_MODEL), row(D_MODEL), row(REST_W)] + _dil_specs(ts, ATT_W, lambda i: i) + [vec, ANY],
        out_specs=[row(D_MODEL), row(N_IN), vec],
        out_shape=[jax.ShapeDtypeStruct((s, D_MODEL), F32), jax.ShapeDtypeStruct((s, N_IN), BF16),
                   jax.ShapeDtypeStruct((1, D_MODEL), F32)],
        scratch_shapes=[pltpu.VMEM((D_MODEL, N_IN), BF16), pltpu.VMEM((ATT_W // LANES, ts, LANES), F32),
                        pltpu.SemaphoreType.DMA((N_DEV,))],
        compiler_params=_params(dimension_semantics=("arbitrary",)),
    )(x, dh1, dzr, *dqkv_dil, g1, w_in)


GRAD_PASS = 2


def _weight_grad(at, b, name, transpose_out, tk=2048, at_slot=None, b_slot=None, ex=None):
    m, s = at.shape[-2:]
    nn = b.shape[-1]
    tn = nn // N_DEV
    tk = min(tk, s)
    nk = s // tk
    npass = N_DEV // GRAD_PASS
    oshape = (tn, m) if transpose_out else (m, tn)
    owner = lambda jj: N_DEV - 1 - jj
    order = jnp.stack([_linear(_peer(_my_place(), owner(jj))) for jj in range(N_DEV)]).astype(jnp.int32)
    sent_in = lambda p: [jj for jj in range(N_DEV - 1) if jj // GRAD_PASS == p]

    nex = ex.nw if ex else 0

    def body(order_ref, at_ref, *refs):
        (b_ref,), ex_ins, (got_ref,), ex_outs, (acc, res, send_sems, recv_sems, local_sem), ex_sems = _split_refs(
            refs, (1, nex, 1, nex, 5, 3 if ex else 0))
        j, k = pl.program_id(0), pl.program_id(1)
        me = _my_place()
        mine = _linear(me)
        if ex:
            _carry_start(ex, j * nk + k, ex_ins, ex_outs, ex_sems)

        def send(jj):
            peer = _peer(me, owner(jj))
            return pltpu.make_async_remote_copy(
                src_ref=res.at[(jj // GRAD_PASS) % 2, peer[2]], dst_ref=got_ref.at[mine],
                send_sem=send_sems.at[jj], recv_sem=recv_sems.at[jj],
                device_id=peer, device_id_type=pl.DeviceIdType.MESH)

        @pl.when(k == 0)
        def _():
            acc[...] = jnp.zeros_like(acc)

        acc[...] += _dot(at_ref[...], b_ref[...])

        @pl.when(k == nk - 1)
        def _():
            for p in range(2, npass):
                @pl.when(j == p)
                def _():
                    for jj in sent_in(p - 2):
                        send(jj).wait_send()

            for core in range(GRAD_PASS):
                r = acc[:, core * tn:(core + 1) * tn]
                res[j % 2, core] = (r.T if transpose_out else r).astype(BF16)
            for p in range(npass):
                @pl.when(j == p)
                def _():
                    for jj in sent_in(p):
                        send(jj).start()

            @pl.when(j == npass - 1)
            def _():
                own = pltpu.make_async_copy(res.at[(npass - 1) % 2, me[2]], got_ref.at[mine], local_sem.at[0])
                own.start()
                for p in range(max(npass - 2, 0), npass):
                    for jj in sent_in(p):
                        send(jj).wait_send()
                for jj in range(N_DEV - 1):
                    send(jj).wait_recv()
                own.wait()

        if ex:
            _carry_wait(ex, j * nk + k, npass * nk - 1, ex_ins, ex_outs, ex_sems)

    if at_slot is None:
        at_spec = pl.BlockSpec((m, tk), lambda j, k, o: (0, k))
    else:
        at_spec = pl.BlockSpec((None, m, tk), lambda j, k, o: (at_slot, 0, k))
    chip_cols = lambda j, o: o[GRAD_PASS * j] // GRAD_PASS
    if b_slot is None:
        b_spec = pl.BlockSpec((tk, GRAD_PASS * tn), lambda j, k, o: (k, chip_cols(j, o)))
    else:
        b_spec = pl.BlockSpec((None, tk, GRAD_PASS * tn), lambda j, k, o: (b_slot, k, chip_cols(j, o)))
    outs = pl.pallas_call(
        body, name=name,
        grid_spec=pltpu.PrefetchScalarGridSpec(
            num_scalar_prefetch=1, grid=(npass, nk),
            in_specs=[at_spec, b_spec] + (ex.specs if ex else []),
            out_specs=[ANY] + (ex.specs if ex else []),
            scratch_shapes=[pltpu.VMEM((m, GRAD_PASS * tn), F32), pltpu.VMEM((2, GRAD_PASS) + oshape, BF16),
                            pltpu.SemaphoreType.DMA((N_DEV - 1,)), pltpu.SemaphoreType.DMA((N_DEV - 1,)),
                            pltpu.SemaphoreType.DMA((1,))] + (ex.scratch if ex else [])),
        out_shape=[jax.ShapeDtypeStruct((N_DEV,) + oshape, BF16)] + (ex.out_shape if ex else []),
        compiler_params=_params(dimension_semantics=("arbitrary", "arbitrary")),
    )(order, at, b, *(ex.arrays if ex else []))
    return (outs[0], outs[1:]) if ex else outs[0]


def _my_place():
    x, y, c = lax.axis_index("x"), lax.axis_index("y"), lax.axis_index("c")
    return x, y, c


def _peer(place, k):
    x, y, c = place
    return (1 - x if k & 4 else x, 1 - y if k & 2 else y, 1 - c if k & 1 else c)


def _linear(place):
    x, y, c = place
    return 4 * x + 2 * y + c


class _Exchange:
    passes_on = False

    def __init__(self, arrays, gather):
        self.arrays, self.gather, self.nw = list(arrays), list(gather), len(arrays)
        self.out_shape = []
        for a, g in zip(arrays, gather):
            block = a.shape if g else a.shape[1:]
            self.out_shape.append(jax.ShapeDtypeStruct((N_DEV,) + tuple(block), a.dtype))
        self.specs = [ANY] * self.nw
        self.scratch = [pltpu.SemaphoreType.DMA((self.nw, N_DEV - 1)), pltpu.SemaphoreType.DMA((self.nw, N_DEV - 1)),
                        pltpu.SemaphoreType.DMA((self.nw,))]

    def _copies(self, ins, outs, sems):
        send_sems, recv_sems, local_sems = sems
        me = _my_place()
        mine = _linear(me)
        copies = []
        for w in range(self.nw):
            src = ins[w] if self.gather[w] else ins[w].at[mine]
            copies.append(pltpu.make_async_copy(src, outs[w].at[mine], local_sems.at[w]))
        for k in range(1, N_DEV):
            peer = _peer(me, k)
            for w in range(self.nw):
                src = ins[w] if self.gather[w] else ins[w].at[_linear(peer)]
                copies.append(pltpu.make_async_remote_copy(
                    src_ref=src, dst_ref=outs[w].at[mine],
                    send_sem=send_sems.at[w, k - 1], recv_sem=recv_sems.at[w, k - 1],
                    device_id=peer, device_id_type=pl.DeviceIdType.MESH))
        return copies

    def start(self, ins, outs, sems):
        for cp in self._copies(ins, outs, sems):
            cp.start()

    def wait(self, ins, outs, sems):
        copies = self._copies(ins, outs, sems)
        for cp in copies[self.nw:]:
            cp.wait_recv()
        for cp in copies[self.nw:]:
            cp.wait_send()
        for cp in copies[:self.nw]:
            cp.wait()


class _Gather:
    passes_on = True

    def __init__(self, arrays):
        self.arrays, self.nw = list(arrays), len(arrays)
        self.out_shape = [jax.ShapeDtypeStruct((N_DEV,) + tuple(a.shape), a.dtype) for a in arrays]
        self.specs = [ANY] * self.nw
        self.scratch = [pltpu.SemaphoreType.DMA((self.nw, N_DEV - 1)), pltpu.SemaphoreType.DMA((self.nw, N_DEV - 1)),
                        pltpu.SemaphoreType.DMA((self.nw,))]

    @staticmethod
    def _places():
        x, y, c = _my_place()
        return (x, y, c), (x, y, 1 - c), [(1 - x, y), (x, 1 - y), (1 - x, 1 - y)]

    @staticmethod
    def _copy(outs, sems, w, k, block, to, src=None):
        rows = outs[w].at[_linear(block)]
        return pltpu.make_async_remote_copy(
            src_ref=rows if src is None else src, dst_ref=rows, send_sem=sems[0].at[w, k], recv_sem=sems[1].at[w, k],
            device_id=to, device_id_type=pl.DeviceIdType.MESH)

    def _first(self, ins, outs, sems, w):
        me, sibling, chips = self._places()
        return ([self._copy(outs, sems, w, 0, me, sibling, src=ins[w])]
                + [self._copy(outs, sems, w, 1 + j, me, (*chip, me[2]), src=ins[w]) for j, chip in enumerate(chips)])

    def _passed(self, outs, sems, w):
        me, sibling, chips = self._places()
        return [self._copy(outs, sems, w, 4 + j, (*chip, me[2]), sibling) for j, chip in enumerate(chips)]

    def _local(self, ins, outs, sems, w):
        return pltpu.make_async_copy(ins[w], outs[w].at[_linear(self._places()[0])], sems[2].at[w])

    def start(self, ins, outs, sems):
        for w in range(self.nw):
            self._local(ins, outs, sems, w).start()
            for cp in self._first(ins, outs, sems, w):
                cp.start()

    def pass_on(self, ins, outs, sems):
        me, sibling, chips = self._places()
        for j, chip in enumerate(chips):
            for w in range(self.nw):
                self._copy(outs, sems, w, 1 + j, (*chip, me[2]), me).wait_recv()
                self._passed(outs, sems, w)[j].start()

    def wait(self, ins, outs, sems):
        me, sibling, chips = self._places()
        for w in range(self.nw):
            self._copy(outs, sems, w, 0, sibling, me).wait_recv()
            for j, chip in enumerate(chips):
                self._copy(outs, sems, w, 4 + j, (*chip, sibling[2]), me).wait_recv()
            for cp in self._first(ins, outs, sems, w) + self._passed(outs, sems, w):
                cp.wait_send()
            self._local(ins, outs, sems, w).wait()


def _sum_parts(p_ref):
    g = p_ref[0].astype(F32)
    for j in range(1, N_DEV):
        g = g + p_ref[j].astype(F32)
    return g


def _adam_update(g, w, m, v):
    nm = ADAM_B1 * m + (1.0 - ADAM_B1) * g
    nv = ADAM_B2 * v + (1.0 - ADAM_B2) * (g * g)
    m_hat = nm / (1.0 - ADAM_B1 ** ADAM_STEP)
    v_hat = nv / (1.0 - ADAM_B2 ** ADAM_STEP)
    return -ADAM_LR * (m_hat / (jnp.sqrt(v_hat) + ADAM_EPS) + ADAM_WD * w), nm, nv


def _adamw(parts, w, m, v, name, tr):
    rows, cols = w.shape
    tr = min(tr, rows)
    while rows % tr:
        tr //= 2
    assert tr % 8 == 0, (rows, tr)

    def body(p_ref, w_ref, m_ref, v_ref, g_ref, d_ref, nm_ref, nv_ref):
        g = _sum_parts(p_ref)
        g_ref[...] = g
        d_ref[...], nm_ref[...], nv_ref[...] = _adam_update(g, w_ref[...], m_ref[...], v_ref[...])

    blk = pl.BlockSpec((tr, cols), lambda i: (i, 0))
    return pl.pallas_call(
        body, name=name, grid=(rows // tr,),
        in_specs=[pl.BlockSpec((N_DEV, tr, cols), lambda i: (0, i, 0)), blk, blk, blk],
        out_specs=[blk] * 4,
        out_shape=[jax.ShapeDtypeStruct((rows, cols), F32)] * 4,
        compiler_params=_params(dimension_semantics=("arbitrary",)),
    )(parts, w, m, v)


def _adamw_vectors(got_stats, got_g1, got_scale, params):
    def body(st_ref, g1_ref, sc_ref, *refs):
        ins, outs = refs[:12], refs[12:]
        stats = _sum_parts(st_ref)
        outs[0][...] = stats[0:1, 0:1]
        grads = (_sum_parts(g1_ref), _sum_parts(sc_ref), stats[1:2, :], stats[2:3, :])
        for k, g in enumerate(grads):
            w_ref, m_ref, v_ref = ins[3 * k:3 * k + 3]
            g_ref, d_ref, nm_ref, nv_ref = outs[1 + 4 * k:5 + 4 * k]
            g_ref[...] = g
            d_ref[...], nm_ref[...], nv_ref[...] = _adam_update(g, w_ref[...], m_ref[...], v_ref[...])

    flat = [a for p in params for a in p]
    out_shape = [jax.ShapeDtypeStruct((1, 1), F32)]
    for w, _, _ in params:
        out_shape += [jax.ShapeDtypeStruct(w.shape, F32)] * 4
    res = pl.pallas_call(body, name="adamw_vectors", out_shape=out_shape)(got_stats, got_g1, got_scale, *flat)
    return res[0], [res[1 + 4 * k:5 + 4 * k] for k in range(len(params))]


def kernel(x, norm_mix_g, w_in, w_att_out, w_pool_grp, pool_scale, w_pool_out, w_out, norm_mlp_g, w_mlp_in, w_mlp_out, norm_final_g, loss_target, m_norm_mix_g, m_w_in, m_w_att_out, m_w_pool_grp, m_pool_scale, m_w_pool_out, m_w_out, m_norm_mlp_g, m_w_mlp_in, m_w_mlp_out, m_norm_final_g, v_norm_mix_g, v_w_in, v_w_att_out, v_w_pool_grp, v_pool_scale, v_w_pool_out, v_w_out, v_norm_mlp_g, v_w_mlp_in, v_w_mlp_out, v_norm_final_g):
    x, tgt = x[0], loss_target[0]
    s = x.shape[0]
    g1, g2, g3 = norm_mix_g, norm_mlp_g, norm_final_g.reshape(1, D_MODEL)
    shards = [w_in[0], w_att_out[0], w_pool_out[0], w_out[0], w_mlp_in[0], w_mlp_out[0]]
    wire = [a.astype(BF16) for a in shards]
    cols = lambda a: jnp.transpose(a, (1, 0, 2)).reshape(a.shape[1], N_DEV * a.shape[2])
    rows = lambda a: a.reshape(N_DEV * a.shape[1], a.shape[2])
    wbd =_window_weights(w_pool_grp[0]).astype(BF16)

    (u, ut), (f_in,) = _rms_u(x, g1, 1024, _Gather(wire[:1]))
    (qkv0, qkv1, qkv2, zr), later = _in_proj_fwd(u, f_in, 1024, _Gather(wire[1:]))
    f_ao, f_po, f_out, f_mi, f_mo = cols(later[0]), cols(later[1]), rows(later[2]), later[3], rows(later[4])
    qkv_dil = (qkv0, qkv1, qkv2)
    flat = lambda a: a.reshape(s, a.shape[-1])
    shaped = lambda a, g: a if g == 0 else a.reshape(ATT_GROUPS[g][1], s // ATT_GROUPS[g][1], a.shape[-1])
    ol_dil = [shaped(_attn_fwd(flat(qkv_dil[g]), g), g) for g in range(N_GROUPS)]
    h1 = _mix_fwd(x, zr, ol_dil, wbd, pool_scale, f_ao, f_po, f_out, ts=512)

    dh1, mdt, hd, stats = _mlp_fwd_bwd(h1, tgt, jnp.concatenate([g2, g3], axis=0), f_mi, f_mo, ts=256)
    got = {"w_mlp_in": _weight_grad(mdt, hd, "grad_w_mlp_in", transpose_out=False, at_slot=0, b_slot=1),
           "w_mlp_out": _weight_grad(mdt, hd, "grad_w_mlp_out", transpose_out=True, at_slot=1, b_slot=0)}
    dzr, dc_dil, (g_out, g_ao, g_po, g_bd, g_scale) = _mix_bwd(
        dh1, zr, ol_dil, wbd, pool_scale, f_ao, f_po, f_out, ts=256)
    g_grp = jnp.stack([g_bd[g, off:off + POOL_GW, off:off + POOL_GW] for g, (_, off) in enumerate(GROUP_WIN)])
    early = [_Exchange([g_out], [False]), _Exchange([g_ao, g_po], [False, False]),
             _Exchange([g_grp.reshape(4 * POOL_GW, POOL_GW)], [True])]
    dqkv_dil, arrived = [], []
    for g in range(N_GROUPS):
        dqkv, rode = _attn_bwd(flat(qkv_dil[g]), flat(dc_dil[g]), g, early[g])
        dqkv_dil.append(shaped(dqkv, g))
        arrived += list(rode)
    got["w_out"], got["w_att_out"], got["w_pool_out"], got_grp = arrived
    dx, dz, dg1 = _in_proj_bwd(x, dh1, dzr, dqkv_dil, g1, f_in, ts=512)
    got["w_in"], got_vectors = _weight_grad(ut, dz, "grad_w_in", transpose_out=False,
                                            ex=_Exchange([stats, dg1, g_scale], [True] * 3))

    names = ["w_in", "w_att_out", "w_pool_out", "w_out", "w_mlp_in", "w_mlp_out"]
    ms = [m_w_in, m_w_att_out, m_w_pool_out, m_w_out, m_w_mlp_in, m_w_mlp_out]
    vs = [v_w_in, v_w_att_out, v_w_pool_out, v_w_out, v_w_mlp_in, v_w_mlp_out]
    upd = {}
    for k, name in enumerate(names):
        res = _adamw(got[name], shards[k], ms[k][0], vs[k][0], "adamw_" + name, tr=512)
        upd[name] = [a[None] for a in res]

    as_rows = lambda a: a.reshape(4 * POOL_GW, POOL_GW)
    res = _adamw(got_grp, as_rows(w_pool_grp), as_rows(m_w_pool_grp), as_rows(v_w_pool_grp), "adamw_w_pool_grp", tr=2048)
    upd["w_pool_grp"] = [a.reshape(w_pool_grp.shape) for a in res]
    vectors = {"norm_mix_g": (norm_mix_g, m_norm_mix_g, v_norm_mix_g), "pool_scale": (pool_scale, m_pool_scale, v_pool_scale),
               "norm_mlp_g": (norm_mlp_g, m_norm_mlp_g, v_norm_mlp_g),
               "norm_final_g": (norm_final_g, m_norm_final_g, v_norm_final_g)}
    loss, vector_res = _adamw_vectors(*got_vectors, [tuple(a.reshape(1, -1) for a in p) for p in vectors.values()])
    for (name, (w, _, _)), res in zip(vectors.items(), vector_res):
        upd[name] = [a.reshape(w.shape) for a in res]

    order = ["norm_mix_g", "w_in", "w_att_out", "w_pool_grp", "pool_scale", "w_pool_out", "w_out", "norm_mlp_g",
             "w_mlp_in", "w_mlp_out", "norm_final_g"]
    out = [loss.reshape(()), dx[None]]
    for q in range(4):
        out += [upd[name][q] for name in order]
    return tuple(out)
```

```python
import jax
import jax.numpy as jnp
import numpy as np
from jax import lax
from jax.experimental import pallas as pl
from jax.experimental.pallas import tpu as pltpu

F32 = jnp.float32
BF16 = jnp.bfloat16

D_MODEL = 1024
HEAD_DIM = 64
GROUP_W = 256
HEADS = GROUP_W // HEAD_DIM
ATT_GROUPS = ((128, 1), (512, 4), (2048, 16))
N_GROUPS = len(ATT_GROUPS)
BLK = 128
ATT_W = N_GROUPS * GROUP_W
POOL_WINDOWS = (2, 4, 8, 16)
POOL_GW = 192
POOL_W = len(POOL_WINDOWS) * POOL_GW
D_FF = 4096
N_IN = 5120
REST_W = N_IN - 3 * ATT_W
NORM_EPS = 1e-6
ALIBI_MAX_BIAS = 8.0
N_DEV = 8
HALO = 32

ADAM_LR = 0.001
ADAM_B1 = 0.9
ADAM_B2 = 0.999
ADAM_EPS = 1e-08
ADAM_WD = 0.01
ADAM_STEP = 10

VMEM_LIMIT = 56 * 1024 * 1024
ANY = pl.BlockSpec(memory_space=pl.ANY)


def _params(**kw):
    return pltpu.CompilerParams(vmem_limit_bytes=VMEM_LIMIT, **kw)


def _dot(a, b):
    return jnp.dot(a, b, preferred_element_type=F32)


def _dot_nt(a, b):
    return lax.dot_general(a, b, (((1,), (1,)), ((), ())), preferred_element_type=F32)


def _dot_tn(a, b):
    return lax.dot_general(a, b, (((0,), (0,)), ((), ())), preferred_element_type=F32)


def _slope(group, h):
    return 2.0 ** (-ALIBI_MAX_BIAS * (HEADS * group + h + 1.0) / (N_GROUPS * HEADS))


def _load_resident(step, pairs, sem):
    @pl.when(step == 0)
    def _():
        copies = [pltpu.make_async_copy(src, dst, sem.at[n]) for n, (src, dst) in enumerate(pairs)]
        for cp in copies:
            cp.start()
        for cp in copies:
            cp.wait()


LANES = 128


FAST_STRIDE = 4


def _to_class_order(value, dil, buf, mid, ref, col0):
    ts, w = value.shape
    if dil == 1:
        ref[:, col0:col0 + w] = value.astype(ref.dtype)
        return
    outer = dil // FAST_STRIDE if dil > FAST_STRIDE else 1
    for c in range(w // LANES):
        cols = slice(col0 + c * LANES, col0 + (c + 1) * LANES)
        buf[c] = value[:, c * LANES:(c + 1) * LANES]
        if outer == 1:
            for r in range(dil):
                ref[r, :, cols] = buf[c, pl.ds(r, ts // dil, stride=dil), :].astype(ref.dtype)
            continue
        for r0 in range(FAST_STRIDE):
            mid[r0, c] = buf[c, pl.ds(r0, ts // FAST_STRIDE, stride=FAST_STRIDE), :]
            for r1 in range(outer):
                ref[outer * r1 + r0, :, cols] = mid[r0, c, pl.ds(r1, ts // dil, stride=outer), :].astype(ref.dtype)


def _to_token_order(ref, dil, buf, mid, ts):
    if dil == 1:
        return ref[...].astype(F32)
    w = ref.shape[-1]
    outer = dil // FAST_STRIDE if dil > FAST_STRIDE else 1
    for c in range(w // LANES):
        cols = slice(c * LANES, (c + 1) * LANES)
        if outer == 1:
            for r in range(dil):
                buf[c, pl.ds(r, ts // dil, stride=dil), :] = ref[r, :, cols].astype(F32)
            continue
        for r0 in range(FAST_STRIDE):
            for r1 in range(outer):
                mid[r0, c, pl.ds(r1, ts // dil, stride=outer), :] = ref[outer * r1 + r0, :, cols].astype(F32)
            buf[c, pl.ds(r0, ts // FAST_STRIDE, stride=FAST_STRIDE), :] = mid[r0, c]
    return jnp.concatenate([buf[c] for c in range(w // LANES)], axis=1)


def _order_scratch(ts, w):
    return [pltpu.VMEM((w // LANES, ts, LANES), F32), pltpu.VMEM((FAST_STRIDE, w // LANES, ts // FAST_STRIDE, LANES), F32)]


def _dil_shapes(s, width, dtype):
    return [jax.ShapeDtypeStruct((s, width) if d == 1 else (d, s // d, width), dtype) for _, d in ATT_GROUPS]


def _dil_specs(ts, width, idx):
    return [pl.BlockSpec((ts, width), lambda i: (idx(i), 0)) if d == 1 else
            pl.BlockSpec((d, ts // d, width), lambda i: (0, idx(i), 0)) for _, d in ATT_GROUPS]


def _column_blocks(stacked, dst):
    tn = stacked.shape[2]
    return [(stacked.at[j], dst.at[:, pl.ds(j * tn, tn)]) for j in range(N_DEV)]


def _sigmoid(x):
    return 0.5 * jnp.tanh(0.5 * x) + 0.5


def _rms(x):
    return lax.rsqrt(jnp.mean(x * x, axis=-1, keepdims=True) + NORM_EPS)


def _rms_bwd(dn, n, r):
    return r * (dn - n * jnp.mean(dn * n, axis=-1, keepdims=True))


def _split_refs(refs, counts):
    out, at = [], 0
    for c in counts:
        out.append(refs[at:at + c])
        at += c
    return out


def _carry_start(ex, step, ins, outs, sems):
    @pl.when(step == 0)
    def _():
        ex.start(ins, outs, sems)


def _carry_wait(ex, step, last, ins, outs, sems, pass_at=None):
    if ex.passes_on:
        @pl.when(step == (last if pass_at is None else pass_at))
        def _():
            ex.pass_on(ins, outs, sems)

    @pl.when(step == last)
    def _():
        ex.wait(ins, outs, sems)


def _rms_u(x, g1, ts, ex):
    s = x.shape[0]
    n = s // ts

    def body(*refs):
        (x_ref, g_ref), ex_ins, (u_ref, ut_ref), ex_outs, ex_sems = _split_refs(refs, (2, ex.nw, 2, ex.nw, 3))
        i = pl.program_id(0)
        _carry_start(ex, i, ex_ins, ex_outs, ex_sems)
        x = x_ref[...]
        u = x * _rms(x) * g_ref[...]
        u_ref[...] = u.astype(BF16)
        ut_ref[...] = u.T.astype(BF16)
        _carry_wait(ex, i, n - 1, ex_ins, ex_outs, ex_sems)

    outs = pl.pallas_call(
        body, name="rms_u", grid=(n,),
        in_specs=[pl.BlockSpec((ts, D_MODEL), lambda i: (i, 0)), pl.BlockSpec((1, D_MODEL), lambda i: (0, 0))]
                 + ex.specs,
        out_specs=[pl.BlockSpec((ts, D_MODEL), lambda i: (i, 0)), pl.BlockSpec((D_MODEL, ts), lambda i: (0, i))]
                  + ex.specs,
        out_shape=[jax.ShapeDtypeStruct((s, D_MODEL), BF16), jax.ShapeDtypeStruct((D_MODEL, s), BF16)] + ex.out_shape,
        scratch_shapes=ex.scratch,
        compiler_params=_params(dimension_semantics=("arbitrary",)),
    )(x, g1, *ex.arrays)
    return outs[:2], outs[2:]


def _in_proj_fwd(u, w_in, ts, ex):
    s = u.shape[0]
    n = s // ts
    dils = [d for _, d in ATT_GROUPS]

    def body(*refs):
        (u_ref, w_hbm), ex_ins, (q0_ref, q1_ref, q2_ref, zr_ref), ex_outs, (w_ref, zbuf, zmid, sem), ex_sems = (
            _split_refs(refs, (2, ex.nw, 4, ex.nw, 4, 3)))
        i = pl.program_id(0)
        _carry_start(ex, i, ex_ins, ex_outs, ex_sems)
        _load_resident(i, _column_blocks(w_hbm, w_ref), sem)
        ub = u_ref[...]
        outs = (q0_ref, q1_ref, q2_ref)
        for sec in range(3):
            for g in range(N_GROUPS):
                c0 = sec * ATT_W + g * GROUP_W
                zc = _dot(ub, w_ref[:, c0:c0 + GROUP_W])
                _to_class_order(zc, dils[g], zbuf, zmid, outs[g], sec * GROUP_W)
        for c0 in range(0, REST_W, 256):
            zr_ref[:, c0:c0 + 256] = _dot(ub, w_ref[:, 3 * ATT_W + c0:3 * ATT_W + c0 + 256]).astype(BF16)
        _carry_wait(ex, i, n - 1, ex_ins, ex_outs, ex_sems, pass_at=max(n - 3, 0))

    outs = pl.pallas_call(
        body, name="in_proj_fwd", grid=(n,),
        in_specs=[pl.BlockSpec((ts, D_MODEL), lambda i: (i, 0)), ANY] + ex.specs,
        out_specs=_dil_specs(ts, ATT_W, lambda i: i) + [pl.BlockSpec((ts, REST_W), lambda i: (i, 0))] + ex.specs,
        out_shape=_dil_shapes(s, ATT_W, BF16) + [jax.ShapeDtypeStruct((s, REST_W), BF16)] + ex.out_shape,
        scratch_shapes=[pltpu.VMEM((D_MODEL, N_IN), BF16)] + _order_scratch(ts, GROUP_W)
                       + [pltpu.SemaphoreType.DMA((N_DEV,))] + ex.scratch,
        compiler_params=_params(dimension_semantics=("arbitrary",)),
    )(u, w_in, *ex.arrays)
    return outs[:4], outs[4:]


def _blocks_per_step(nbc):
    return 16 if nbc % 16 == 0 else 8


STACK = HEADS * BLK


SCORE_SCALE = HEAD_DIM ** -0.5


def _band_consts(group, dil):
    row = lax.broadcasted_iota(jnp.int32, (STACK, 2 * BLK), 0)
    kj = lax.broadcasted_iota(jnp.int32, (STACK, 2 * BLK), 1)
    head = row // BLK
    steps = BLK + (row % BLK) - kj
    slope = jnp.full((STACK, 2 * BLK), _slope(group, HEADS - 1), F32)
    for h in range(HEADS - 1):
        slope = jnp.where(head == h, _slope(group, h), slope)
    in_band = (steps >= 0) & (steps <= BLK)
    return jnp.where(in_band, slope * (steps.astype(F32) * float(dil)), jnp.inf), kj


def _first_key(block, nbc, nb, b):
    if nbc % nb == 0 and b != 0:
        return None
    return jnp.where((block % nbc) != 0, 0, BLK)


def _head_of_col():
    return lax.broadcasted_iota(jnp.int32, (BLK, GROUP_W), 1) // HEAD_DIM


def _stack_heads(xb):
    head_of = _head_of_col()
    return jnp.concatenate([jnp.where(head_of == h, xb, jnp.zeros_like(xb)) for h in range(HEADS)], axis=0)


def _per_head_cols(stacked, rhs, scale_rows=None):
    lane = lax.broadcasted_iota(jnp.int32, (BLK, LANES), 1)
    halves = []
    for pair in range(HEADS // 2):
        tile = rhs[:, pair * LANES:(pair + 1) * LANES]
        parts = []
        for h in (2 * pair, 2 * pair + 1):
            part = _dot(stacked[h * BLK:(h + 1) * BLK], tile)
            parts.append(part if scale_rows is None else part * scale_rows[h * BLK:(h + 1) * BLK])
        halves.append(jnp.where(lane < HEAD_DIM, parts[0], parts[1]))
    return jnp.concatenate(halves, axis=1)


def _per_head_rows(col):
    lane = lax.broadcasted_iota(jnp.int32, (BLK, LANES), 1)
    return jnp.concatenate(
        [jnp.where(lane < HEAD_DIM, col[(2 * pair) * BLK:(2 * pair + 1) * BLK], col[(2 * pair + 1) * BLK:(2 * pair + 2) * BLK])
         for pair in range(HEADS // 2)], axis=1)


def _band_softmax(qs, kb, penalty, kj, first_key):
    sc = _dot_nt(qs, kb) - penalty
    if first_key is not None:
        sc = jnp.where(kj >= first_key, sc, -jnp.inf)
    mx = jnp.max(sc, axis=1, keepdims=True)
    e = jnp.exp(sc - mx)
    return e, mx, jnp.sum(e, axis=1, keepdims=True)


def _attn_fwd(qkv, group):
    s = qkv.shape[0]
    dil = ATT_GROUPS[group][1]
    nbc = s // (BLK * dil)
    nb = _blocks_per_step(nbc)
    tile = nb * BLK

    def body(q_ref, kc_ref, kp_ref, vc_ref, vp_ref, ol_ref, kbuf, vbuf):
        i = pl.program_id(0)
        kbuf[0:BLK] = kp_ref[...]
        kbuf[BLK:BLK + tile] = kc_ref[...]
        vbuf[0:BLK] = vp_ref[...]
        vbuf[BLK:BLK + tile] = vc_ref[...]
        penalty, kj = _band_consts(group, dil)
        blocks = range(nb)
        qss = [_stack_heads(q_ref[b * BLK:(b + 1) * BLK, :] * SCORE_SCALE) for b in blocks]
        soft = [_band_softmax(qss[b], kbuf[b * BLK:b * BLK + 2 * BLK, :], penalty, kj,
                              _first_key(i * nb + b, nbc, nb, b)) for b in blocks]
        for b in blocks:
            e, mx, den = soft[b]
            ol_ref[b * BLK:(b + 1) * BLK, 0:GROUP_W] = _per_head_cols(
                e.astype(BF16), vbuf[b * BLK:b * BLK + 2 * BLK, :], 1.0 / den)
        for b in blocks:
            e, mx, den = soft[b]
            ol_ref[b * BLK:(b + 1) * BLK, GROUP_W:] = _per_head_rows(mx + jnp.log(den))

    n = s // tile
    cur = lambda c: pl.BlockSpec((tile, GROUP_W), lambda i: (i, c))
    prev = lambda c: pl.BlockSpec((BLK, GROUP_W), lambda i: (jnp.maximum(i * nb - 1, 0), c))
    return pl.pallas_call(
        body, name=f"attn_fwd_g{group}", grid=(n,),
        in_specs=[cur(0), cur(1), prev(1), cur(2), prev(2)],
        out_specs=pl.BlockSpec((tile, 2 * GROUP_W), lambda i: (i, 0)),
        out_shape=jax.ShapeDtypeStruct((s, 2 * GROUP_W), F32),
        scratch_shapes=[pltpu.VMEM((BLK + tile, GROUP_W), BF16), pltpu.VMEM((BLK + tile, GROUP_W), BF16)],
        compiler_params=_params(dimension_semantics=("arbitrary",)),
    )(qkv, qkv, qkv, qkv, qkv)


def _attn_bwd(qkv, dc, group, ex=None):
    s = qkv.shape[0]
    dil = ATT_GROUPS[group][1]
    nbc = s // (BLK * dil)
    nb = _blocks_per_step(nbc)
    tile = nb * BLK
    n = s // tile
    nex = ex.nw if ex else 0

    def body(*refs):
        ((q_ref, kc_ref, kp_ref, vc_ref, vp_ref, do_ref, c_ref), ex_ins, (out_ref,), ex_outs,
         (kbuf, vbuf, dqpend, dkpend, dvpend), ex_sems) = _split_refs(refs, (7, nex, 1, nex, 5, 3 if ex else 0))
        i = pl.program_id(0)
        if ex:
            _carry_start(ex, i, ex_ins, ex_outs, ex_sems)

        @pl.when(i == 0)
        def _():
            dqpend[...] = jnp.zeros_like(dqpend)
            dkpend[...] = jnp.zeros_like(dkpend)
            dvpend[...] = jnp.zeros_like(dvpend)

        out_ref[:, 0:GROUP_W] = dqpend[...]
        body_rows = slice(0, tile - BLK)
        tail = slice(tile - BLK, tile)
        pends = ((dkpend, GROUP_W), (dvpend, 2 * GROUP_W))
        for pend, c0 in pends:
            out_ref[body_rows, c0:c0 + GROUP_W] = pend[body_rows, :].astype(BF16)

        @pl.when(i < n)
        def _():
            kbuf[0:BLK] = kp_ref[...]
            kbuf[BLK:BLK + tile] = kc_ref[...]
            vbuf[0:BLK] = vp_ref[...]
            vbuf[BLK:BLK + tile] = vc_ref[...]
            penalty, kj = _band_consts(group, dil)
            head_of = _head_of_col()
            blocks = range(nb)
            rows = [slice(b * BLK, (b + 1) * BLK) for b in blocks]
            kbs = [kbuf[b * BLK:b * BLK + 2 * BLK, :] for b in blocks]
            vbs = [vbuf[b * BLK:b * BLK + 2 * BLK, :] for b in blocks]
            qss = [_stack_heads(q_ref[rows[b], :] * SCORE_SCALE) for b in blocks]
            doss = [_stack_heads(do_ref[rows[b], :].astype(BF16)) for b in blocks]
            cors = []
            for b in blocks:
                cb = c_ref[rows[b], :]
                cors.append(jnp.concatenate(
                    [jnp.max(jnp.where(head_of == h, cb, -jnp.inf), axis=1, keepdims=True) for h in range(HEADS)],
                    axis=0))
            soft = [_band_softmax(qss[b], kbs[b], penalty, kj, _first_key(i * nb + b, nbc, nb, b)) for b in blocks]
            dps = [_dot_nt(doss[b], vbs[b]) for b in blocks]
            ps = [soft[b][0] * (1.0 / soft[b][2]) for b in blocks]
            dss = [(ps[b] * (dps[b] + cors[b])).astype(BF16) for b in blocks]
            for b in blocks:
                dqpend[rows[b], :] = _per_head_cols(dss[b], kbs[b] * SCORE_SCALE).astype(BF16)
            bands = [(_dot_tn(dss[b], qss[b]), _dot_tn(ps[b].astype(BF16), doss[b])) for b in blocks]
            for which, (pend, c0) in enumerate(pends):
                out_ref[tail, c0:c0 + GROUP_W] = (pend[tail, :] + bands[0][which][0:BLK]).astype(BF16)
                for b in range(nb):
                    own = bands[b][which][BLK:2 * BLK]
                    pend[b * BLK:(b + 1) * BLK, :] = own + bands[b + 1][which][0:BLK] if b + 1 < nb else own

        @pl.when(i == n)
        def _():
            for pend, c0 in pends:
                out_ref[tail, c0:c0 + GROUP_W] = pend[tail, :].astype(BF16)

        if ex:
            _carry_wait(ex, i, n, ex_ins, ex_outs, ex_sems)

    last = n - 1
    cur = lambda c: pl.BlockSpec((tile, GROUP_W), lambda i: (jnp.minimum(i, last), c))
    prev = lambda c: pl.BlockSpec(
        (BLK, GROUP_W), lambda i: (jnp.maximum(jnp.minimum(i, last) * nb - 1, 0), c))
    outs = pl.pallas_call(
        body, name=f"attn_bwd_g{group}", grid=(n + 1,),
        in_specs=[cur(0), cur(1), prev(1), cur(2), prev(2), cur(0), cur(1)] + (ex.specs if ex else []),
        out_specs=[pl.BlockSpec((tile, ATT_W), lambda i: (jnp.maximum(i - 1, 0), 0))] + (ex.specs if ex else []),
        out_shape=[jax.ShapeDtypeStruct((s, ATT_W), BF16)] + (ex.out_shape if ex else []),
        scratch_shapes=[pltpu.VMEM((BLK + tile, GROUP_W), BF16), pltpu.VMEM((BLK + tile, GROUP_W), BF16),
                        pltpu.VMEM((tile, GROUP_W), BF16),
                        pltpu.VMEM((tile, GROUP_W), F32), pltpu.VMEM((tile, GROUP_W), F32)]
                       + (ex.scratch if ex else []),
        compiler_params=_params(dimension_semantics=("arbitrary",)),
    )(qkv, qkv, qkv, qkv, qkv, dc, dc, *(ex.arrays if ex else []))
    return (outs[0], outs[1:]) if ex else outs[0]


def _gather_rows(refs, buf, mid, ts):
    return [_to_token_order(refs[g], ATT_GROUPS[g][1], buf, mid, ts) for g in range(N_GROUPS)]


def _inverse_counts(ts):
    win = np.repeat(np.asarray(POOL_WINDOWS), POOL_GW)
    first = np.minimum(np.arange(ts)[:, None] + 1, win[None, :])
    counts = np.stack([first, np.broadcast_to(win[None, :], (ts, POOL_W))]).astype(np.float32)
    return jnp.asarray(np.float32(1.0) / counts)


def _pool_fwd(ebuf, s2, s4, s8, inv_count, ts):
    n = ts + HALO
    s2[8:n] = ebuf[8:n] + ebuf[7:n - 1]
    s4[16:n] = s2[16:n] + s2[14:n - 2]
    s8[24:n] = s4[24:n] + s4[20:n - 4]
    s16 = s8[32:n] + s8[24:n - 8]
    col = lax.broadcasted_iota(jnp.int32, (ts, POOL_W), 1)
    psum = jnp.where(col < POOL_GW, s2[32:n],
                     jnp.where(col < 2 * POOL_GW, s4[32:n], jnp.where(col < 3 * POOL_GW, s8[32:n], s16)))
    return psum * inv_count - ebuf[32:n]


GROUP_WIN = ((0, 0), (128, 64), (384, 0), (512, 64))
WIN = 256


def _window_weights(w_grp):
    return jnp.stack([jnp.pad(w_grp[g], ((off, WIN - off - POOL_GW), (off, WIN - off - POOL_GW)))
                      for g, (_, off) in enumerate(GROUP_WIN)])


def _group_matmul(xb, wwin_ref, transposed=False):
    outs = []
    for g, (start, _) in enumerate(GROUP_WIN):
        xw = xb[:, start:start + WIN]
        outs.append(_dot_nt(xw, wwin_ref[g]) if transposed else _dot(xw, wwin_ref[g]))
    half = WIN // 2
    return jnp.concatenate([outs[0][:, :half], outs[0][:, half:] + outs[1][:, :half], outs[1][:, half:],
                            outs[2][:, :half], outs[2][:, half:] + outs[3][:, :half], outs[3][:, half:]], axis=1)


def _mix_core(zr, pooled, outs, lses, wwin_ref, scale, wao, wpo):
    mixed = _group_matmul(pooled.astype(BF16), wwin_ref)
    p = mixed * scale
    l0, l1, l2 = lses
    mx = jnp.maximum(jnp.maximum(l0, l1), l2)
    e0, e1, e2 = jnp.exp(l0 - mx), jnp.exp(l1 - mx), jnp.exp(l2 - mx)
    inv = 1.0 / (e0 + e1 + e2)
    wts = (e0 * inv, e1 * inv, e2 * inv)
    a = wts[0] * outs[0] + wts[1] * outs[1] + wts[2] * outs[2]
    att = _dot(a.astype(BF16), wao)
    pol = _dot(p.astype(BF16), wpo)
    sga = _sigmoid(zr[:, POOL_W:POOL_W + D_MODEL].astype(F32))
    sgp = _sigmoid(zr[:, POOL_W + D_MODEL:].astype(F32))
    mg = sga * att + sgp * pol
    return dict(mixed=mixed, p=p, wts=wts, a=a, att=att, pol=pol, sga=sga, sgp=sgp, mg=mg)


def _fill_pool_input(ebuf, zr_ref, halo_ref, t0):
    ts = zr_ref.shape[0]
    halo = halo_ref[...].astype(F32)
    t = t0 - HALO + lax.broadcasted_iota(jnp.int32, (HALO, POOL_W), 0)
    ebuf[0:HALO] = jnp.where(t >= 0, halo, 0.0)
    ebuf[HALO:HALO + ts] = zr_ref[:, 0:POOL_W].astype(F32)


def _mix_fwd(x, zr, ol_dil, wbd, scale, wao, wpo, wout, ts):
    s = x.shape[0]
    n = s // ts

    def body(x_ref, zr_ref, halo_ref, ic_ref, ol0, ol1, ol2, wbd_ref, sc_ref, wao_ref, wpo_ref, wout_ref,
             h1_ref, ebuf, s2, s4, s8, rbuf, rmid):
        i = pl.program_id(0)
        _fill_pool_input(ebuf, zr_ref, halo_ref, i * ts)
        pooled = _pool_fwd(ebuf, s2, s4, s8, ic_ref[...], ts)
        ols = _gather_rows((ol0, ol1, ol2), rbuf, rmid, ts)
        outs, lses = [a[:, :GROUP_W] for a in ols], [a[:, GROUP_W:] for a in ols]
        f = _mix_core(zr_ref[...], pooled, outs, lses, wbd_ref, sc_ref[...], wao_ref[...], wpo_ref[...])
        h1_ref[...] = x_ref[...] + _dot(f["mg"].astype(BF16), wout_ref[...])

    whole = lambda a: pl.BlockSpec(a.shape, lambda i: (0,) * a.ndim)
    idx = lambda i: i
    return pl.pallas_call(
        body, name="mix_fwd", grid=(n,),
        in_specs=[pl.BlockSpec((ts, D_MODEL), lambda i: (i, 0)),
                  pl.BlockSpec((ts, REST_W), lambda i: (i, 0)),
                  pl.BlockSpec((HALO, POOL_W), lambda i: (jnp.maximum(i * (ts // HALO) - 1, 0), 0)),
                  pl.BlockSpec((None, ts, POOL_W), lambda i: (jnp.minimum(i, 1), 0, 0))]
                 + _dil_specs(ts, 2 * GROUP_W, idx)
                 + [whole(wbd), whole(scale), whole(wao), whole(wpo), whole(wout)],
        out_specs=pl.BlockSpec((ts, D_MODEL), lambda i: (i, 0)),
        out_shape=jax.ShapeDtypeStruct((s, D_MODEL), F32),
        scratch_shapes=[pltpu.VMEM((ts + HALO, POOL_W), F32)] * 4 + _order_scratch(ts, 2 * GROUP_W),
        compiler_params=_params(dimension_semantics=("arbitrary",)),
    )(x, zr, zr, _inverse_counts(ts), *ol_dil, wbd, scale, wao, wpo, wout)


def _mix_bwd(dh1, zr, ol_dil, wbd, scale, wao, wpo, wout, ts):
    s = dh1.shape[0]
    n = s // ts

    def body(dh_ref, zr_ref, halo_ref, ic_ref, ol0, ol1, ol2, sc_ref, wbd_hbm, wao_hbm, wpo_hbm, wout_hbm,
             dzr_ref, dc0, dc1, dc2, gsc_ref, gwout_hbm, gwao_hbm, gwpo_hbm, gwbd_hbm,
             ebuf, s2, s4, s8, gbuf, t2, t4, t8, rbuf, rmid,
             wbd_ref, wao_ref, wpo_ref, wout_ref, gwout_ref, gwao_ref, gwpo_ref, gwbd_ref, sem):
        j = pl.program_id(0)
        i = n - 1 - j
        _load_resident(j, [(wbd_hbm, wbd_ref), (wao_hbm, wao_ref), (wpo_hbm, wpo_ref), (wout_hbm, wout_ref)], sem)

        @pl.when(j == 0)
        def _():
            gwout_ref[...] = jnp.zeros_like(gwout_ref)
            gwao_ref[...] = jnp.zeros_like(gwao_ref)
            gwpo_ref[...] = jnp.zeros_like(gwpo_ref)
            gwbd_ref[...] = jnp.zeros_like(gwbd_ref)
            gsc_ref[...] = jnp.zeros_like(gsc_ref)
            gbuf[ts:ts + HALO] = jnp.zeros((HALO, POOL_W), F32)

        _fill_pool_input(ebuf, zr_ref, halo_ref, i * ts)
        inv_count = ic_ref[...]
        pooled = _pool_fwd(ebuf, s2, s4, s8, inv_count, ts)
        ols = _gather_rows((ol0, ol1, ol2), rbuf, rmid, ts)
        outs, lses = [a[:, :GROUP_W] for a in ols], [a[:, GROUP_W:] for a in ols]
        zr = zr_ref[...]
        wao, wpo, wout = wao_ref[...], wpo_ref[...], wout_ref[...]
        scale = sc_ref[...]
        f = _mix_core(zr, pooled, outs, lses, wbd_ref, scale, wao, wpo)

        dhb = dh_ref[...].astype(BF16)
        gwout_ref[...] += _dot(f["mg"].T.astype(BF16), dhb)
        dmg = _dot_nt(dhb, wout)
        sga, sgp, att, pol = f["sga"], f["sgp"], f["att"], f["pol"]
        datt = dmg * sga
        dpol = dmg * sgp
        dzr_ref[:, POOL_W:POOL_W + D_MODEL] = (dmg * att * sga * (1.0 - sga)).astype(BF16)
        dzr_ref[:, POOL_W + D_MODEL:] = (dmg * pol * sgp * (1.0 - sgp)).astype(BF16)
        dattb = datt.astype(BF16)
        dpolb = dpol.astype(BF16)
        gwao_ref[...] += _dot(f["a"].T.astype(BF16), dattb)
        gwpo_ref[...] += _dot(f["p"].T.astype(BF16), dpolb)
        da = _dot_nt(dattb, wao)
        dp = _dot_nt(dpolb, wpo)

        gsc_ref[...] += jnp.sum(f["mixed"] * dp, axis=0, keepdims=True)
        dmixed = (dp * scale).astype(BF16)
        pooled_t = pooled.T.astype(BF16)
        for g, (start, _) in enumerate(GROUP_WIN):
            gwbd_ref[g] += _dot(pooled_t[start:start + WIN, :], dmixed[:, start:start + WIN])
        dpooled = _group_matmul(dmixed, wbd_ref, transposed=True)
        gbuf[0:ts] = dpooled * inv_count
        m = ts + HALO
        t2[0:m - 8] = gbuf[0:m - 8] + gbuf[1:m - 7]
        t4[0:m - 16] = t2[0:m - 16] + t2[2:m - 14]
        t8[0:m - 24] = t4[0:m - 24] + t4[4:m - 20]
        t16 = t8[0:ts] + t8[8:ts + 8]
        col = lax.broadcasted_iota(jnp.int32, (ts, POOL_W), 1)
        back = jnp.where(col < POOL_GW, t2[0:ts],
                         jnp.where(col < 2 * POOL_GW, t4[0:ts], jnp.where(col < 3 * POOL_GW, t8[0:ts], t16)))
        dzr_ref[:, 0:POOL_W] = (back - dpooled).astype(BF16)
        gbuf[ts:ts + HALO] = gbuf[0:HALO]

        head_of = lax.broadcasted_iota(jnp.int32, (ts, GROUP_W), 1) // HEAD_DIM
        prod = da * f["a"]
        inner = jnp.zeros((ts, GROUP_W), F32)
        for h in range(4):
            hm = head_of == h
            tot = jnp.sum(jnp.where(hm, prod, 0.0), axis=1, keepdims=True)
            inner = jnp.where(hm, tot, inner)
        for g, dc_ref in enumerate((dc0, dc1, dc2)):
            both = jnp.concatenate([f["wts"][g] * da, -f["wts"][g] * inner], axis=1)
            _to_class_order(both, ATT_GROUPS[g][1], rbuf, rmid, dc_ref, 0)

        @pl.when(j == n - 1)
        def _():
            wout_ref[...] = gwout_ref[...].astype(BF16)
            wao_ref[...] = gwao_ref[...].astype(BF16)
            wpo_ref[...] = gwpo_ref[...].astype(BF16)
            tn = D_MODEL // N_DEV
            pairs = [(wout_ref, gwout_hbm), (gwbd_ref, gwbd_hbm)]
            for staged, dst in ((wao_ref, gwao_hbm), (wpo_ref, gwpo_hbm)):
                pairs += [(staged.at[:, pl.ds(k * tn, tn)], dst.at[k]) for k in range(N_DEV)]
            copies = [pltpu.make_async_copy(src, dst, sem.at[k]) for k, (src, dst) in enumerate(pairs)]
            for cp in copies:
                cp.start()
            for cp in copies:
                cp.wait()

    idx = lambda j: n - 1 - j
    dc_shapes = _dil_shapes(s, 2 * GROUP_W, F32)
    weights = (wbd, wao, wpo, wout)
    grad_shapes = [(D_MODEL, D_MODEL), (GROUP_W, D_MODEL), (POOL_W, D_MODEL), (len(GROUP_WIN), WIN, WIN)]
    tile_buf = pltpu.VMEM((ts + HALO, POOL_W), F32)
    outs = pl.pallas_call(
        body, name="mix_bwd", grid=(n,),
        in_specs=[pl.BlockSpec((ts, D_MODEL), lambda j: (idx(j), 0)),
                  pl.BlockSpec((ts, REST_W), lambda j: (idx(j), 0)),
                  pl.BlockSpec((HALO, POOL_W), lambda j: (jnp.maximum(idx(j) * (ts // HALO) - 1, 0), 0)),
                  pl.BlockSpec((None, ts, POOL_W), lambda j: (jnp.minimum(idx(j), 1), 0, 0))]
                 + _dil_specs(ts, 2 * GROUP_W, idx)
                 + [pl.BlockSpec((1, POOL_W), lambda j: (0, 0))] + [ANY] * 4,
        out_specs=[pl.BlockSpec((ts, REST_W), lambda j: (idx(j), 0))]
                  + _dil_specs(ts, 2 * GROUP_W, idx)
                  + [pl.BlockSpec((1, POOL_W), lambda j: (0, 0))] + [ANY] * 4,
        out_shape=[jax.ShapeDtypeStruct((s, REST_W), BF16)] + dc_shapes
                  + [jax.ShapeDtypeStruct((1, POOL_W), F32),
                     jax.ShapeDtypeStruct((D_MODEL, D_MODEL), BF16),
                     jax.ShapeDtypeStruct((N_DEV, GROUP_W, D_MODEL // N_DEV), BF16),
                     jax.ShapeDtypeStruct((N_DEV, POOL_W, D_MODEL // N_DEV), BF16),
                     jax.ShapeDtypeStruct(grad_shapes[3], F32)],
        scratch_shapes=[tile_buf] * 8 + _order_scratch(ts, 2 * GROUP_W)
                       + [pltpu.VMEM(w.shape, BF16) for w in weights]
                       + [pltpu.VMEM(shape, F32) for shape in grad_shapes]
                       + [pltpu.SemaphoreType.DMA((2 + 2 * N_DEV,))],
        compiler_params=_params(dimension_semantics=("arbitrary",)),
    )(dh1, zr, zr, _inverse_counts(ts), *ol_dil, scale, wbd, wao, wpo, wout)
    dzr, dc_dil, g_scale = outs[0], outs[1:4], outs[4]
    g_out, g_ao, g_po, g_bd = outs[5:]
    return dzr, dc_dil, (g_out.reshape(N_DEV, D_MODEL // N_DEV, D_MODEL), g_ao, g_po, g_bd, g_scale)


FF_CHUNK = 1024


STAT_ROWS = 8


def _mlp_fwd_bwd(h1, tgt, gains, wmi, wmo, ts):
    s = h1.shape[0]
    n = s // ts
    nchunk = D_FF // FF_CHUNK

    def body(h1_ref, t_ref, g_ref, wmi_hbm, wmo_hbm, dh1_ref, mdt_ref, hd_ref, stat_ref, wmi, wmo, relu_buf, sem):
        i = pl.program_id(0)
        _load_resident(i, _column_blocks(wmi_hbm, wmi) + [(wmo_hbm, wmo)], sem)

        @pl.when(i == 0)
        def _():
            stat_ref[...] = jnp.zeros_like(stat_ref)

        h1 = h1_ref[...]
        g2 = g_ref[0:1, :]
        g3 = g_ref[1:2, :]
        r2 = _rms(h1)
        n2 = h1 * r2
        m = n2 * g2
        mb = m.astype(BF16)
        mdt_ref[0] = m.T.astype(BF16)
        h2 = h1
        for c in range(nchunk):
            cols = slice(c * FF_CHUNK, (c + 1) * FF_CHUNK)
            rl = jnp.maximum(_dot(mb, wmi[:, cols]), 0.0)
            relu_buf[:, cols] = rl
            hb = (rl * rl).astype(BF16)
            hd_ref[0, :, cols] = hb
            h2 = h2 + _dot(hb, wmo[cols, :])
        r3 = _rms(h2)
        n3 = h2 * r3
        diff = n3 * g3 - t_ref[...]
        loss = jnp.sum(0.5 * jnp.sum(diff * diff, axis=1, keepdims=True) / D_MODEL, axis=0, keepdims=True)
        dy = diff * (1.0 / D_MODEL)
        dg3 = jnp.sum(dy * n3, axis=0, keepdims=True)
        dh2 = _rms_bwd(dy * g3, n3, r3)
        dh2b = dh2.astype(BF16)
        mdt_ref[1] = dh2.T.astype(BF16)
        dm = jnp.zeros((ts, D_MODEL), F32)
        for c in range(nchunk):
            cols = slice(c * FF_CHUNK, (c + 1) * FF_CHUNK)
            dfb = (_dot_nt(dh2b, wmo[cols, :]) * (2.0 * relu_buf[:, cols])).astype(BF16)
            hd_ref[1, :, cols] = dfb
            dm = dm + _dot_nt(dfb, wmi[:, cols])
        dg2 = jnp.sum(dm * n2, axis=0, keepdims=True)
        dh1_ref[...] = dh2 + _rms_bwd(dm * g2, n2, r2)
        row = lax.broadcasted_iota(jnp.int32, (STAT_ROWS, D_MODEL), 0)
        stat_ref[...] += jnp.where(row == 0, loss, jnp.where(row == 1, dg2, jnp.where(row == 2, dg3, 0.0)))

    row = lambda w: pl.BlockSpec((ts, w), lambda i: (i, 0))
    return pl.pallas_call(
        body, name="mlp_fwd_bwd", grid=(n,),
        in_specs=[row(D_MODEL), row(D_MODEL), pl.BlockSpec((2, D_MODEL), lambda i: (0, 0)), ANY, ANY],
        out_specs=[row(D_MODEL), pl.BlockSpec((2, D_MODEL, ts), lambda i: (0, 0, i)),
                   pl.BlockSpec((2, ts, D_FF), lambda i: (0, i, 0)),
                   pl.BlockSpec((STAT_ROWS, D_MODEL), lambda i: (0, 0))],
        out_shape=[jax.ShapeDtypeStruct((s, D_MODEL), F32), jax.ShapeDtypeStruct((2, D_MODEL, s), BF16),
                   jax.ShapeDtypeStruct((2, s, D_FF), BF16), jax.ShapeDtypeStruct((STAT_ROWS, D_MODEL), F32)],
        scratch_shapes=[pltpu.VMEM((D_MODEL, D_FF), BF16), pltpu.VMEM((D_FF, D_MODEL), BF16),
                        pltpu.VMEM((ts, D_FF), F32), pltpu.SemaphoreType.DMA((N_DEV + 1,))],
        compiler_params=_params(dimension_semantics=("arbitrary",)),
    )(h1, tgt, gains, wmi, wmo)


def _in_proj_bwd(x, dh1, dzr, dqkv_dil, g1, w_in, ts):
    s = x.shape[0]
    n = s // ts

    def body(x_ref, dh_ref, dzr_ref, q0, q1, q2, g_ref, w_hbm, dx_ref, dz_ref, dg_ref, w_ref, qbuf, qmid, sem):
        i = pl.program_id(0)
        _load_resident(i, _column_blocks(w_hbm, w_ref), sem)

        @pl.when(i == 0)
        def _():
            dg_ref[...] = jnp.zeros_like(dg_ref)

        for g, dqkv in enumerate(_gather_rows((q0, q1, q2), qbuf, qmid, ts)):
            for sec in range(3):
                c0 = sec * ATT_W + g * GROUP_W
                dz_ref[:, c0:c0 + GROUP_W] = dqkv[:, sec * GROUP_W:(sec + 1) * GROUP_W].astype(BF16)
        dz_ref[:, 3 * ATT_W:] = dzr_ref[...]
        du = _dot_nt(dz_ref[...], w_ref[...])
        x = x_ref[...]
        r1 = _rms(x)
        n1 = x * r1
        g1 = g_ref[...]
        dg_ref[...] += jnp.sum(du * n1, axis=0, keepdims=True)
        dx_ref[...] = dh_ref[...] + _rms_bwd(du * g1, n1, r1)

    row = lambda w: pl.BlockSpec((ts, w), lambda i: (i, 0))
    vec = pl.BlockSpec((1, D_MODEL), lambda i: (0, 0))
    return pl.pallas_call(
        body, name="in_proj_bwd", grid=(n,),
        in_specs=[row(D_MODEL), row(D_MODEL), row(REST_W)] + _dil_specs(ts, ATT_W, lambda i: i) + [vec, ANY],
        out_specs=[row(D_MODEL), row(N_IN), vec],
        out_shape=[jax.ShapeDtypeStruct((s, D_MODEL), F32), jax.ShapeDtypeStruct((s, N_IN), BF16),
                   jax.ShapeDtypeStruct((1, D_MODEL), F32)],
        scratch_shapes=[pltpu.VMEM((D_MODEL, N_IN), BF16)] + _order_scratch(ts, ATT_W)
                       + [pltpu.SemaphoreType.DMA((N_DEV,))],
        compiler_params=_params(dimension_semantics=("arbitrary",)),
    )(x, dh1, dzr, *dqkv_dil, g1, w_in)


GRAD_PASS = 2


def _weight_grad(at, b, name, transpose_out, tk=2048, at_slot=None, b_slot=None, ex=None):
    m, s = at.shape[-2:]
    nn = b.shape[-1]
    tn = nn // N_DEV
    tk = min(tk, s)
    nk = s // tk
    npass = N_DEV // GRAD_PASS
    oshape = (tn, m) if transpose_out else (m, tn)
    owner = lambda jj: N_DEV - 1 - jj
    order = jnp.stack([_linear(_peer(_my_place(), owner(jj))) for jj in range(N_DEV)]).astype(jnp.int32)
    sent_in = lambda p: [jj for jj in range(N_DEV - 1) if jj // GRAD_PASS == p]

    nex = ex.nw if ex else 0

    def body(order_ref, at_ref, *refs):
        (b_ref,), ex_ins, (got_ref,), ex_outs, (acc, res, send_sems, recv_sems, local_sem), ex_sems = _split_refs(
            refs, (1, nex, 1, nex, 5, 3 if ex else 0))
        j, k = pl.program_id(0), pl.program_id(1)
        me = _my_place()
        mine = _linear(me)
        if ex:
            _carry_start(ex, j * nk + k, ex_ins, ex_outs, ex_sems)

        def send(jj):
            peer = _peer(me, owner(jj))
            return pltpu.make_async_remote_copy(
                src_ref=res.at[(jj // GRAD_PASS) % 2, peer[2]], dst_ref=got_ref.at[mine],
                send_sem=send_sems.at[jj], recv_sem=recv_sems.at[jj],
                device_id=peer, device_id_type=pl.DeviceIdType.MESH)

        @pl.when(k == 0)
        def _():
            acc[...] = jnp.zeros_like(acc)

        acc[...] += _dot(at_ref[...], b_ref[...])

        @pl.when(k == nk - 1)
        def _():
            for p in range(2, npass):
                @pl.when(j == p)
                def _():
                    for jj in sent_in(p - 2):
                        send(jj).wait_send()

            for core in range(GRAD_PASS):
                r = acc[:, core * tn:(core + 1) * tn]
                res[j % 2, core] = (r.T if transpose_out else r).astype(BF16)
            for p in range(npass):
                @pl.when(j == p)
                def _():
                    for jj in sent_in(p):
                        send(jj).start()

            @pl.when(j == npass - 1)
            def _():
                own = pltpu.make_async_copy(res.at[(npass - 1) % 2, me[2]], got_ref.at[mine], local_sem.at[0])
                own.start()
                for p in range(max(npass - 2, 0), npass):
                    for jj in sent_in(p):
                        send(jj).wait_send()
                for jj in range(N_DEV - 1):
                    send(jj).wait_recv()
                own.wait()

        if ex:
            _carry_wait(ex, j * nk + k, npass * nk - 1, ex_ins, ex_outs, ex_sems)

    if at_slot is None:
        at_spec = pl.BlockSpec((m, tk), lambda j, k, o: (0, k))
    else:
        at_spec = pl.BlockSpec((None, m, tk), lambda j, k, o: (at_slot, 0, k))
    chip_cols = lambda j, o: o[GRAD_PASS * j] // GRAD_PASS
    if b_slot is None:
        b_spec = pl.BlockSpec((tk, GRAD_PASS * tn), lambda j, k, o: (k, chip_cols(j, o)))
    else:
        b_spec = pl.BlockSpec((None, tk, GRAD_PASS * tn), lambda j, k, o: (b_slot, k, chip_cols(j, o)))
    outs = pl.pallas_call(
        body, name=name,
        grid_spec=pltpu.PrefetchScalarGridSpec(
            num_scalar_prefetch=1, grid=(npass, nk),
            in_specs=[at_spec, b_spec] + (ex.specs if ex else []),
            out_specs=[ANY] + (ex.specs if ex else []),
            scratch_shapes=[pltpu.VMEM((m, GRAD_PASS * tn), F32), pltpu.VMEM((2, GRAD_PASS) + oshape, BF16),
                            pltpu.SemaphoreType.DMA((N_DEV - 1,)), pltpu.SemaphoreType.DMA((N_DEV - 1,)),
                            pltpu.SemaphoreType.DMA((1,))] + (ex.scratch if ex else [])),
        out_shape=[jax.ShapeDtypeStruct((N_DEV,) + oshape, BF16)] + (ex.out_shape if ex else []),
        compiler_params=_params(dimension_semantics=("arbitrary", "arbitrary")),
    )(order, at, b, *(ex.arrays if ex else []))
    return (outs[0], outs[1:]) if ex else outs[0]


def _my_place():
    x, y, c = lax.axis_index("x"), lax.axis_index("y"), lax.axis_index("c")
    return x, y, c


def _peer(place, k):
    x, y, c = place
    return (1 - x if k & 4 else x, 1 - y if k & 2 else y, 1 - c if k & 1 else c)


def _linear(place):
    x, y, c = place
    return 4 * x + 2 * y + c


class _Exchange:
    passes_on = False

    def __init__(self, arrays, gather):
        self.arrays, self.gather, self.nw = list(arrays), list(gather), len(arrays)
        self.out_shape = []
        for a, g in zip(arrays, gather):
            block = a.shape if g else a.shape[1:]
            self.out_shape.append(jax.ShapeDtypeStruct((N_DEV,) + tuple(block), a.dtype))
        self.specs = [ANY] * self.nw
        self.scratch = [pltpu.SemaphoreType.DMA((self.nw, N_DEV - 1)), pltpu.SemaphoreType.DMA((self.nw, N_DEV - 1)),
                        pltpu.SemaphoreType.DMA((self.nw,))]

    def _copies(self, ins, outs, sems):
        send_sems, recv_sems, local_sems = sems
        me = _my_place()
        mine = _linear(me)
        copies = []
        for w in range(self.nw):
            src = ins[w] if self.gather[w] else ins[w].at[mine]
            copies.append(pltpu.make_async_copy(src, outs[w].at[mine], local_sems.at[w]))
        for k in range(1, N_DEV):
            peer = _peer(me, k)
            for w in range(self.nw):
                src = ins[w] if self.gather[w] else ins[w].at[_linear(peer)]
                copies.append(pltpu.make_async_remote_copy(
                    src_ref=src, dst_ref=outs[w].at[mine],
                    send_sem=send_sems.at[w, k - 1], recv_sem=recv_sems.at[w, k - 1],
                    device_id=peer, device_id_type=pl.DeviceIdType.MESH))
        return copies

    def start(self, ins, outs, sems):
        for cp in self._copies(ins, outs, sems):
            cp.start()

    def wait(self, ins, outs, sems):
        copies = self._copies(ins, outs, sems)
        for cp in copies[self.nw:]:
            cp.wait_recv()
        for cp in copies[self.nw:]:
            cp.wait_send()
        for cp in copies[:self.nw]:
            cp.wait()


class _Gather:
    passes_on = True

    def __init__(self, arrays):
        self.arrays, self.nw = list(arrays), len(arrays)
        self.out_shape = [jax.ShapeDtypeStruct((N_DEV,) + tuple(a.shape), a.dtype) for a in arrays]
        self.specs = [ANY] * self.nw
        self.scratch = [pltpu.SemaphoreType.DMA((self.nw, N_DEV - 1)), pltpu.SemaphoreType.DMA((self.nw, N_DEV - 1)),
                        pltpu.SemaphoreType.DMA((self.nw,))]

    @staticmethod
    def _places():
        x, y, c = _my_place()
        return (x, y, c), (x, y, 1 - c), [(1 - x, y), (x, 1 - y), (1 - x, 1 - y)]

    @staticmethod
    def _copy(outs, sems, w, k, block, to, src=None):
        rows = outs[w].at[_linear(block)]
        return pltpu.make_async_remote_copy(
            src_ref=rows if src is None else src, dst_ref=rows, send_sem=sems[0].at[w, k], recv_sem=sems[1].at[w, k],
            device_id=to, device_id_type=pl.DeviceIdType.MESH)

    def _first(self, ins, outs, sems, w):
        me, sibling, chips = self._places()
        return ([self._copy(outs, sems, w, 0, me, sibling, src=ins[w])]
                + [self._copy(outs, sems, w, 1 + j, me, (*chip, me[2]), src=ins[w]) for j, chip in enumerate(chips)])

    def _passed(self, outs, sems, w):
        me, sibling, chips = self._places()
        return [self._copy(outs, sems, w, 4 + j, (*chip, me[2]), sibling) for j, chip in enumerate(chips)]

    def _local(self, ins, outs, sems, w):
        return pltpu.make_async_copy(ins[w], outs[w].at[_linear(self._places()[0])], sems[2].at[w])

    def start(self, ins, outs, sems):
        for w in range(self.nw):
            self._local(ins, outs, sems, w).start()
            for cp in self._first(ins, outs, sems, w):
                cp.start()

    def pass_on(self, ins, outs, sems):
        me, sibling, chips = self._places()
        for j, chip in enumerate(chips):
            for w in range(self.nw):
                self._copy(outs, sems, w, 1 + j, (*chip, me[2]), me).wait_recv()
                self._passed(outs, sems, w)[j].start()

    def wait(self, ins, outs, sems):
        me, sibling, chips = self._places()
        for w in range(self.nw):
            self._copy(outs, sems, w, 0, sibling, me).wait_recv()
            for j, chip in enumerate(chips):
                self._copy(outs, sems, w, 4 + j, (*chip, sibling[2]), me).wait_recv()
            for cp in self._first(ins, outs, sems, w) + self._passed(outs, sems, w):
                cp.wait_send()
            self._local(ins, outs, sems, w).wait()


def _sum_parts(p_ref):
    g = p_ref[0].astype(F32)
    for j in range(1, N_DEV):
        g = g + p_ref[j].astype(F32)
    return g


def _adam_update(g, w, m, v):
    nm = ADAM_B1 * m + (1.0 - ADAM_B1) * g
    nv = ADAM_B2 * v + (1.0 - ADAM_B2) * (g * g)
    m_hat = nm / (1.0 - ADAM_B1 ** ADAM_STEP)
    v_hat = nv / (1.0 - ADAM_B2 ** ADAM_STEP)
    return -ADAM_LR * (m_hat / (jnp.sqrt(v_hat) + ADAM_EPS) + ADAM_WD * w), nm, nv


def _adamw(parts, w, m, v, name, tr):
    rows, cols = w.shape
    tr = min(tr, rows)
    while rows % tr:
        tr //= 2
    assert tr % 8 == 0, (rows, tr)

    def body(p_ref, w_ref, m_ref, v_ref, g_ref, d_ref, nm_ref, nv_ref):
        g = _sum_parts(p_ref)
        g_ref[...] = g
        d_ref[...], nm_ref[...], nv_ref[...] = _adam_update(g, w_ref[...], m_ref[...], v_ref[...])

    blk = pl.BlockSpec((tr, cols), lambda i: (i, 0))
    return pl.pallas_call(
        body, name=name, grid=(rows // tr,),
        in_specs=[pl.BlockSpec((N_DEV, tr, cols), lambda i: (0, i, 0)), blk, blk, blk],
        out_specs=[blk] * 4,
        out_shape=[jax.ShapeDtypeStruct((rows, cols), F32)] * 4,
        compiler_params=_params(dimension_semantics=("arbitrary",)),
    )(parts, w, m, v)


def _adamw_vectors(got_stats, got_g1, got_scale, params):
    def body(st_ref, g1_ref, sc_ref, *refs):
        ins, outs = refs[:12], refs[12:]
        stats = _sum_parts(st_ref)
        outs[0][...] = stats[0:1, 0:1]
        grads = (_sum_parts(g1_ref), _sum_parts(sc_ref), stats[1:2, :], stats[2:3, :])
        for k, g in enumerate(grads):
            w_ref, m_ref, v_ref = ins[3 * k:3 * k + 3]
            g_ref, d_ref, nm_ref, nv_ref = outs[1 + 4 * k:5 + 4 * k]
            g_ref[...] = g
            d_ref[...], nm_ref[...], nv_ref[...] = _adam_update(g, w_ref[...], m_ref[...], v_ref[...])

    flat = [a for p in params for a in p]
    out_shape = [jax.ShapeDtypeStruct((1, 1), F32)]
    for w, _, _ in params:
        out_shape += [jax.ShapeDtypeStruct(w.shape, F32)] * 4
    res = pl.pallas_call(body, name="adamw_vectors", out_shape=out_shape)(got_stats, got_g1, got_scale, *flat)
    return res[0], [res[1 + 4 * k:5 + 4 * k] for k in range(len(params))]


def kernel(x, norm_mix_g, w_in, w_att_out, w_pool_grp, pool_scale, w_pool_out, w_out, norm_mlp_g, w_mlp_in, w_mlp_out, norm_final_g, loss_target, m_norm_mix_g, m_w_in, m_w_att_out, m_w_pool_grp, m_pool_scale, m_w_pool_out, m_w_out, m_norm_mlp_g, m_w_mlp_in, m_w_mlp_out, m_norm_final_g, v_norm_mix_g, v_w_in, v_w_att_out, v_w_pool_grp, v_pool_scale, v_w_pool_out, v_w_out, v_norm_mlp_g, v_w_mlp_in, v_w_mlp_out, v_norm_final_g):
    x, tgt = x[0], loss_target[0]
    s = x.shape[0]
    g1, g2, g3 = norm_mix_g, norm_mlp_g, norm_final_g.reshape(1, D_MODEL)
    shards = [w_in[0], w_att_out[0], w_pool_out[0], w_out[0], w_mlp_in[0], w_mlp_out[0]]
    wire = [a.astype(BF16) for a in shards]
    cols = lambda a: jnp.transpose(a, (1, 0, 2)).reshape(a.shape[1], N_DEV * a.shape[2])
    rows = lambda a: a.reshape(N_DEV * a.shape[1], a.shape[2])
    wbd =_window_weights(w_pool_grp[0]).astype(BF16)

    (u, ut), (f_in,) = _rms_u(x, g1, 1024, _Gather(wire[:1]))
    (qkv0, qkv1, qkv2, zr), later = _in_proj_fwd(u, f_in, 1024, _Gather(wire[1:]))
    f_ao, f_po, f_out, f_mi, f_mo = cols(later[0]), cols(later[1]), rows(later[2]), later[3], rows(later[4])
    qkv_dil = (qkv0, qkv1, qkv2)
    flat = lambda a: a.reshape(s, a.shape[-1])
    shaped = lambda a, g: a if g == 0 else a.reshape(ATT_GROUPS[g][1], s // ATT_GROUPS[g][1], a.shape[-1])
    ol_dil = [shaped(_attn_fwd(flat(qkv_dil[g]), g), g) for g in range(N_GROUPS)]
    h1 = _mix_fwd(x, zr, ol_dil, wbd, pool_scale, f_ao, f_po, f_out, ts=512)

    dh1, mdt, hd, stats = _mlp_fwd_bwd(h1, tgt, jnp.concatenate([g2, g3], axis=0), f_mi, f_mo, ts=256)
    got = {"w_mlp_in": _weight_grad(mdt, hd, "grad_w_mlp_in", transpose_out=False, at_slot=0, b_slot=1),
           "w_mlp_out": _weight_grad(mdt, hd, "grad_w_mlp_out", transpose_out=True, at_slot=1, b_slot=0)}
    dzr, dc_dil, (g_out, g_ao, g_po, g_bd, g_scale) = _mix_bwd(
        dh1, zr, ol_dil, wbd, pool_scale, f_ao, f_po, f_out, ts=256)
    g_grp = jnp.stack([g_bd[g, off:off + POOL_GW, off:off + POOL_GW] for g, (_, off) in enumerate(GROUP_WIN)])
    early = [_Exchange([g_out], [False]), _Exchange([g_ao, g_po], [False, False]),
             _Exchange([g_grp.reshape(4 * POOL_GW, POOL_GW)], [True])]
    dqkv_dil, arrived = [], []
    for g in range(N_GROUPS):
        dqkv, rode = _attn_bwd(flat(qkv_dil[g]), flat(dc_dil[g]), g, early[g])
        dqkv_dil.append(shaped(dqkv, g))
        arrived += list(rode)
    got["w_out"], got["w_att_out"], got["w_pool_out"], got_grp = arrived
    dx, dz, dg1 = _in_proj_bwd(x, dh1, dzr, dqkv_dil, g1, f_in, ts=512)
    got["w_in"], got_vectors = _weight_grad(ut, dz, "grad_w_in", transpose_out=False,
                                            ex=_Exchange([stats, dg1, g_scale], [True] * 3))

    names = ["w_in", "w_att_out", "w_pool_out", "w_out", "w_mlp_in", "w_mlp_out"]
    ms = [m_w_in, m_w_att_out, m_w_pool_out, m_w_out, m_w_mlp_in, m_w_mlp_out]
    vs = [v_w_in, v_w_att_out, v_w_pool_out, v_w_out, v_w_mlp_in, v_w_mlp_out]
    upd = {}
    for k, name in enumerate(names):
        res = _adamw(got[name], shards[k], ms[k][0], vs[k][0], "adamw_" + name, tr=512)
        upd[name] = [a[None] for a in res]

    as_rows = lambda a: a.reshape(4 * POOL_GW, POOL_GW)
    res = _adamw(got_grp, as_rows(w_pool_grp), as_rows(m_w_pool_grp), as_rows(v_w_pool_grp), "adamw_w_pool_grp", tr=2048)
    upd["w_pool_grp"] = [a.reshape(w_pool_grp.shape) for a in res]
    vectors = {"norm_mix_g": (norm_mix_g, m_norm_mix_g, v_norm_mix_g), "pool_scale": (pool_scale, m_pool_scale, v_pool_scale),
               "norm_mlp_g": (norm_mlp_g, m_norm_mlp_g, v_norm_mlp_g),
               "norm_final_g": (norm_final_g, m_norm_final_g, v_norm_final_g)}
    loss, vector_res = _adamw_vectors(*got_vectors, [tuple(a.reshape(1, -1) for a in p) for p in vectors.values()])
    for (name, (w, _, _)), res in zip(vectors.items(), vector_res):
        upd[name] = [a.reshape(w.shape) for a in res]

    order = ["norm_mix_g", "w_in", "w_att_out", "w_pool_grp", "pool_scale", "w_pool_out", "w_out", "norm_mlp_g",
             "w_mlp_in", "w_mlp_out", "norm_final_g"]
    out = [loss.reshape(()), dx[None]]
    for q in range(4):
        out += [upd[name][q] for name in order]
    return tuple(out)
```

```python
import jax
import jax.numpy as jnp
import numpy as np
from jax import lax
from jax.experimental import pallas as pl
from jax.experimental.pallas import tpu as pltpu

F32 = jnp.float32
BF16 = jnp.bfloat16

D_MODEL = 1024
HEAD_DIM = 64
GROUP_W = 256
HEADS = GROUP_W // HEAD_DIM
ATT_GROUPS = ((128, 1), (512, 4), (2048, 16))
N_GROUPS = len(ATT_GROUPS)
BLK = 128
ATT_W = N_GROUPS * GROUP_W
POOL_WINDOWS = (2, 4, 8, 16)
POOL_GW = 192
POOL_W = len(POOL_WINDOWS) * POOL_GW
D_FF = 4096
N_IN = 5120
REST_W = N_IN - 3 * ATT_W
NORM_EPS = 1e-6
ALIBI_MAX_BIAS = 8.0
N_DEV = 8
HALO = 32

ADAM_LR = 0.001
ADAM_B1 = 0.9
ADAM_B2 = 0.999
ADAM_EPS = 1e-08
ADAM_WD = 0.01
ADAM_STEP = 10

VMEM_LIMIT = 56 * 1024 * 1024
ANY = pl.BlockSpec(memory_space=pl.ANY)


def _params(**kw):
    return pltpu.CompilerParams(vmem_limit_bytes=VMEM_LIMIT, **kw)


def _dot(a, b):
    return jnp.dot(a, b, preferred_element_type=F32)


def _dot_nt(a, b):
    return lax.dot_general(a, b, (((1,), (1,)), ((), ())), preferred_element_type=F32)


def _dot_tn(a, b):
    return lax.dot_general(a, b, (((0,), (0,)), ((), ())), preferred_element_type=F32)


def _slope(group, h):
    return 2.0 ** (-ALIBI_MAX_BIAS * (HEADS * group + h + 1.0) / (N_GROUPS * HEADS))


def _load_resident(step, pairs, sem):
    @pl.when(step == 0)
    def _():
        copies = [pltpu.make_async_copy(src, dst, sem.at[n]) for n, (src, dst) in enumerate(pairs)]
        for cp in copies:
            cp.start()
        for cp in copies:
            cp.wait()


LANES = 128


FAST_STRIDE = 4


def _to_class_order(value, dil, buf, mid, ref, col0):
    ts, w = value.shape
    if dil == 1:
        ref[:, col0:col0 + w] = value.astype(ref.dtype)
        return
    outer = dil // FAST_STRIDE if dil > FAST_STRIDE else 1
    for c in range(w // LANES):
        cols = slice(col0 + c * LANES, col0 + (c + 1) * LANES)
        buf[c] = value[:, c * LANES:(c + 1) * LANES]
        if outer == 1:
            for r in range(dil):
                ref[r, :, cols] = buf[c, pl.ds(r, ts // dil, stride=dil), :].astype(ref.dtype)
            continue
        for r0 in range(FAST_STRIDE):
            mid[r0, c] = buf[c, pl.ds(r0, ts // FAST_STRIDE, stride=FAST_STRIDE), :]
            for r1 in range(outer):
                ref[FAST_STRIDE * r1 + r0, :, cols] = mid[r0, c, pl.ds(r1, ts // dil, stride=outer), :].astype(ref.dtype)


def _to_token_order(ref, dil, buf, mid, ts):
    if dil == 1:
        return ref[...].astype(F32)
    w = ref.shape[-1]
    outer = dil // FAST_STRIDE if dil > FAST_STRIDE else 1
    for c in range(w // LANES):
        cols = slice(c * LANES, (c + 1) * LANES)
        if outer == 1:
            for r in range(dil):
                buf[c, pl.ds(r, ts // dil, stride=dil), :] = ref[r, :, cols].astype(F32)
            continue
        for r0 in range(FAST_STRIDE):
            for r1 in range(outer):
                mid[r0, c, pl.ds(r1, ts // dil, stride=outer), :] = ref[FAST_STRIDE * r1 + r0, :, cols].astype(F32)
            buf[c, pl.ds(r0, ts // FAST_STRIDE, stride=FAST_STRIDE), :] = mid[r0, c]
    return jnp.concatenate([buf[c] for c in range(w // LANES)], axis=1)


def _order_scratch(ts, w):
    return [pltpu.VMEM((w // LANES, ts, LANES), F32), pltpu.VMEM((FAST_STRIDE, w // LANES, ts // FAST_STRIDE, LANES), F32)]


def _dil_shapes(s, width, dtype):
    return [jax.ShapeDtypeStruct((s, width) if d == 1 else (d, s // d, width), dtype) for _, d in ATT_GROUPS]


def _dil_specs(ts, width, idx):
    return [pl.BlockSpec((ts, width), lambda i: (idx(i), 0)) if d == 1 else
            pl.BlockSpec((d, ts // d, width), lambda i: (0, idx(i), 0)) for _, d in ATT_GROUPS]


def _column_blocks(stacked, dst):
    tn = stacked.shape[2]
    return [(stacked.at[j], dst.at[:, pl.ds(j * tn, tn)]) for j in range(N_DEV)]


def _sigmoid(x):
    return 0.5 * jnp.tanh(0.5 * x) + 0.5


def _rms(x):
    return lax.rsqrt(jnp.mean(x * x, axis=-1, keepdims=True) + NORM_EPS)


def _rms_bwd(dn, n, r):
    return r * (dn - n * jnp.mean(dn * n, axis=-1, keepdims=True))


def _split_refs(refs, counts):
    out, at = [], 0
    for c in counts:
        out.append(refs[at:at + c])
        at += c
    return out


def _carry_start(ex, step, ins, outs, sems):
    @pl.when(step == 0)
    def _():
        ex.start(ins, outs, sems)


def _carry_wait(ex, step, last, ins, outs, sems, pass_at=None):
    if ex.passes_on:
        @pl.when(step == (last if pass_at is None else pass_at))
        def _():
            ex.pass_on(ins, outs, sems)

    @pl.when(step == last)
    def _():
        ex.wait(ins, outs, sems)


def _rms_u(x, g1, ts, ex):
    s = x.shape[0]
    n = s // ts

    def body(*refs):
        (x_ref, g_ref), ex_ins, (u_ref, ut_ref), ex_outs, ex_sems = _split_refs(refs, (2, ex.nw, 2, ex.nw, 3))
        i = pl.program_id(0)
        _carry_start(ex, i, ex_ins, ex_outs, ex_sems)
        x = x_ref[...]
        u = x * _rms(x) * g_ref[...]
        u_ref[...] = u.astype(BF16)
        ut_ref[...] = u.T.astype(BF16)
        _carry_wait(ex, i, n - 1, ex_ins, ex_outs, ex_sems)

    outs = pl.pallas_call(
        body, name="rms_u", grid=(n,),
        in_specs=[pl.BlockSpec((ts, D_MODEL), lambda i: (i, 0)), pl.BlockSpec((1, D_MODEL), lambda i: (0, 0))]
                 + ex.specs,
        out_specs=[pl.BlockSpec((ts, D_MODEL), lambda i: (i, 0)), pl.BlockSpec((D_MODEL, ts), lambda i: (0, i))]
                  + ex.specs,
        out_shape=[jax.ShapeDtypeStruct((s, D_MODEL), BF16), jax.ShapeDtypeStruct((D_MODEL, s), BF16)] + ex.out_shape,
        scratch_shapes=ex.scratch,
        compiler_params=_params(dimension_semantics=("arbitrary",)),
    )(x, g1, *ex.arrays)
    return outs[:2], outs[2:]


def _in_proj_fwd(u, w_in, ts, ex):
    s = u.shape[0]
    n = s // ts
    dils = [d for _, d in ATT_GROUPS]

    def body(*refs):
        (u_ref, w_hbm), ex_ins, (q0_ref, q1_ref, q2_ref, zr_ref), ex_outs, (w_ref, zbuf, zmid, sem), ex_sems = (
            _split_refs(refs, (2, ex.nw, 4, ex.nw, 4, 3)))
        i = pl.program_id(0)
        _carry_start(ex, i, ex_ins, ex_outs, ex_sems)
        _load_resident(i, _column_blocks(w_hbm, w_ref), sem)
        ub = u_ref[...]
        outs = (q0_ref, q1_ref, q2_ref)
        for sec in range(3):
            for g in range(N_GROUPS):
                c0 = sec * ATT_W + g * GROUP_W
                zc = _dot(ub, w_ref[:, c0:c0 + GROUP_W])
                _to_class_order(zc, dils[g], zbuf, zmid, outs[g], sec * GROUP_W)
        for c0 in range(0, REST_W, 256):
            zr_ref[:, c0:c0 + 256] = _dot(ub, w_ref[:, 3 * ATT_W + c0:3 * ATT_W + c0 + 256]).astype(BF16)
        _carry_wait(ex, i, n - 1, ex_ins, ex_outs, ex_sems, pass_at=max(n - 3, 0))

    outs = pl.pallas_call(
        body, name="in_proj_fwd", grid=(n,),
        in_specs=[pl.BlockSpec((ts, D_MODEL), lambda i: (i, 0)), ANY] + ex.specs,
        out_specs=_dil_specs(ts, ATT_W, lambda i: i) + [pl.BlockSpec((ts, REST_W), lambda i: (i, 0))] + ex.specs,
        out_shape=_dil_shapes(s, ATT_W, BF16) + [jax.ShapeDtypeStruct((s, REST_W), BF16)] + ex.out_shape,
        scratch_shapes=[pltpu.VMEM((D_MODEL, N_IN), BF16)] + _order_scratch(ts, GROUP_W)
                       + [pltpu.SemaphoreType.DMA((N_DEV,))] + ex.scratch,
        compiler_params=_params(dimension_semantics=("arbitrary",)),
    )(u, w_in, *ex.arrays)
    return outs[:4], outs[4:]


def _blocks_per_step(nbc):
    return 16 if nbc % 16 == 0 else 8


STACK = HEADS * BLK


SCORE_SCALE = HEAD_DIM ** -0.5


def _band_consts(group, dil):
    row = lax.broadcasted_iota(jnp.int32, (STACK, 2 * BLK), 0)
    kj = lax.broadcasted_iota(jnp.int32, (STACK, 2 * BLK), 1)
    head = row // BLK
    steps = BLK + (row % BLK) - kj
    slope = jnp.full((STACK, 2 * BLK), _slope(group, HEADS - 1), F32)
    for h in range(HEADS - 1):
        slope = jnp.where(head == h, _slope(group, h), slope)
    in_band = (steps >= 0) & (steps <= BLK)
    return jnp.where(in_band, slope * (steps.astype(F32) * float(dil)), jnp.inf), kj


def _first_key(block, nbc, nb, b):
    if nbc % nb == 0 and b != 0:
        return None
    return jnp.where((block % nbc) != 0, 0, BLK)


def _head_of_col():
    return lax.broadcasted_iota(jnp.int32, (BLK, GROUP_W), 1) // HEAD_DIM


def _stack_heads(xb):
    head_of = _head_of_col()
    return jnp.concatenate([jnp.where(head_of == h, xb, jnp.zeros_like(xb)) for h in range(HEADS)], axis=0)


def _per_head_cols(stacked, rhs, scale_rows=None):
    lane = lax.broadcasted_iota(jnp.int32, (BLK, LANES), 1)
    halves = []
    for pair in range(HEADS // 2):
        tile = rhs[:, pair * LANES:(pair + 1) * LANES]
        parts = []
        for h in (2 * pair, 2 * pair + 1):
            part = _dot(stacked[h * BLK:(h + 1) * BLK], tile)
            parts.append(part if scale_rows is None else part * scale_rows[h * BLK:(h + 1) * BLK])
        halves.append(jnp.where(lane < HEAD_DIM, parts[0], parts[1]))
    return jnp.concatenate(halves, axis=1)


def _per_head_rows(col):
    lane = lax.broadcasted_iota(jnp.int32, (BLK, LANES), 1)
    return jnp.concatenate(
        [jnp.where(lane < HEAD_DIM, col[(2 * pair) * BLK:(2 * pair + 1) * BLK], col[(2 * pair + 1) * BLK:(2 * pair + 2) * BLK])
         for pair in range(HEADS // 2)], axis=1)


def _band_softmax(qs, kb, penalty, kj, first_key):
    sc = _dot_nt(qs, kb) - penalty
    if first_key is not None:
        sc = jnp.where(kj >= first_key, sc, -jnp.inf)
    mx = jnp.max(sc, axis=1, keepdims=True)
    e = jnp.exp(sc - mx)
    return e, mx, jnp.sum(e, axis=1, keepdims=True)


def _attn_fwd(qkv, group):
    s = qkv.shape[0]
    dil = ATT_GROUPS[group][1]
    nbc = s // (BLK * dil)
    nb = _blocks_per_step(nbc)
    tile = nb * BLK

    def body(q_ref, kc_ref, kp_ref, vc_ref, vp_ref, ol_ref, kbuf, vbuf):
        i = pl.program_id(0)
        kbuf[0:BLK] = kp_ref[...]
        kbuf[BLK:BLK + tile] = kc_ref[...]
        vbuf[0:BLK] = vp_ref[...]
        vbuf[BLK:BLK + tile] = vc_ref[...]
        penalty, kj = _band_consts(group, dil)
        blocks = range(nb)
        qss = [_stack_heads(q_ref[b * BLK:(b + 1) * BLK, :] * SCORE_SCALE) for b in blocks]
        soft = [_band_softmax(qss[b], kbuf[b * BLK:b * BLK + 2 * BLK, :], penalty, kj,
                              _first_key(i * nb + b, nbc, nb, b)) for b in blocks]
        for b in blocks:
            e, mx, den = soft[b]
            ol_ref[b * BLK:(b + 1) * BLK, 0:GROUP_W] = _per_head_cols(
                e.astype(BF16), vbuf[b * BLK:b * BLK + 2 * BLK, :], 1.0 / den)
        for b in blocks:
            e, mx, den = soft[b]
            ol_ref[b * BLK:(b + 1) * BLK, GROUP_W:] = _per_head_rows(mx + jnp.log(den))

    n = s // tile
    cur = lambda c: pl.BlockSpec((tile, GROUP_W), lambda i: (i, c))
    prev = lambda c: pl.BlockSpec((BLK, GROUP_W), lambda i: (jnp.maximum(i * nb - 1, 0), c))
    return pl.pallas_call(
        body, name=f"attn_fwd_g{group}", grid=(n,),
        in_specs=[cur(0), cur(1), prev(1), cur(2), prev(2)],
        out_specs=pl.BlockSpec((tile, 2 * GROUP_W), lambda i: (i, 0)),
        out_shape=jax.ShapeDtypeStruct((s, 2 * GROUP_W), F32),
        scratch_shapes=[pltpu.VMEM((BLK + tile, GROUP_W), BF16), pltpu.VMEM((BLK + tile, GROUP_W), BF16)],
        compiler_params=_params(dimension_semantics=("arbitrary",)),
    )(qkv, qkv, qkv, qkv, qkv)


def _attn_bwd(qkv, dc, group, ex=None):
    s = qkv.shape[0]
    dil = ATT_GROUPS[group][1]
    nbc = s // (BLK * dil)
    nb = _blocks_per_step(nbc)
    tile = nb * BLK
    n = s // tile
    nex = ex.nw if ex else 0

    def body(*refs):
        ((q_ref, kc_ref, kp_ref, vc_ref, vp_ref, do_ref, c_ref), ex_ins, (out_ref,), ex_outs,
         (kbuf, vbuf, dqpend, dkpend, dvpend), ex_sems) = _split_refs(refs, (7, nex, 1, nex, 5, 3 if ex else 0))
        i = pl.program_id(0)
        if ex:
            _carry_start(ex, i, ex_ins, ex_outs, ex_sems)

        @pl.when(i == 0)
        def _():
            dqpend[...] = jnp.zeros_like(dqpend)
            dkpend[...] = jnp.zeros_like(dkpend)
            dvpend[...] = jnp.zeros_like(dvpend)

        out_ref[:, 0:GROUP_W] = dqpend[...]
        body_rows = slice(0, tile - BLK)
        tail = slice(tile - BLK, tile)
        pends = ((dkpend, GROUP_W), (dvpend, 2 * GROUP_W))
        for pend, c0 in pends:
            out_ref[body_rows, c0:c0 + GROUP_W] = pend[body_rows, :].astype(BF16)

        @pl.when(i < n)
        def _():
            kbuf[0:BLK] = kp_ref[...]
            kbuf[BLK:BLK + tile] = kc_ref[...]
            vbuf[0:BLK] = vp_ref[...]
            vbuf[BLK:BLK + tile] = vc_ref[...]
            penalty, kj = _band_consts(group, dil)
            head_of = _head_of_col()
            blocks = range(nb)
            rows = [slice(b * BLK, (b + 1) * BLK) for b in blocks]
            kbs = [kbuf[b * BLK:b * BLK + 2 * BLK, :] for b in blocks]
            vbs = [vbuf[b * BLK:b * BLK + 2 * BLK, :] for b in blocks]
            qss = [_stack_heads(q_ref[rows[b], :] * SCORE_SCALE) for b in blocks]
            doss = [_stack_heads(do_ref[rows[b], :].astype(BF16)) for b in blocks]
            cors = []
            for b in blocks:
                cb = c_ref[rows[b], :]
                cors.append(jnp.concatenate(
                    [jnp.max(jnp.where(head_of == h, cb, -jnp.inf), axis=1, keepdims=True) for h in range(HEADS)],
                    axis=0))
            soft = [_band_softmax(qss[b], kbs[b], penalty, kj, _first_key(i * nb + b, nbc, nb, b)) for b in blocks]
            dps = [_dot_nt(doss[b], vbs[b]) for b in blocks]
            ps = [soft[b][0] * (1.0 / soft[b][2]) for b in blocks]
            dss = [(ps[b] * (dps[b] + cors[b])).astype(BF16) for b in blocks]
            for b in blocks:
                dqpend[rows[b], :] = _per_head_cols(dss[b], kbs[b] * SCORE_SCALE).astype(BF16)
            bands = [(_dot_tn(dss[b], qss[b]), _dot_tn(ps[b].astype(BF16), doss[b])) for b in blocks]
            for which, (pend, c0) in enumerate(pends):
                out_ref[tail, c0:c0 + GROUP_W] = (pend[tail, :] + bands[0][which][0:BLK]).astype(BF16)
                for b in range(nb):
                    own = bands[b][which][BLK:2 * BLK]
                    pend[b * BLK:(b + 1) * BLK, :] = own + bands[b + 1][which][0:BLK] if b + 1 < nb else own

        @pl.when(i == n)
        def _():
            for pend, c0 in pends:
                out_ref[tail, c0:c0 + GROUP_W] = pend[tail, :].astype(BF16)

        if ex:
            _carry_wait(ex, i, n, ex_ins, ex_outs, ex_sems)

    last = n - 1
    cur = lambda c: pl.BlockSpec((tile, GROUP_W), lambda i: (jnp.minimum(i, last), c))
    prev = lambda c: pl.BlockSpec(
        (BLK, GROUP_W), lambda i: (jnp.maximum(jnp.minimum(i, last) * nb - 1, 0), c))
    outs = pl.pallas_call(
        body, name=f"attn_bwd_g{group}", grid=(n + 1,),
        in_specs=[cur(0), cur(1), prev(1), cur(2), prev(2), cur(0), cur(1)] + (ex.specs if ex else []),
        out_specs=[pl.BlockSpec((tile, ATT_W), lambda i: (jnp.maximum(i - 1, 0), 0))] + (ex.specs if ex else []),
        out_shape=[jax.ShapeDtypeStruct((s, ATT_W), BF16)] + (ex.out_shape if ex else []),
        scratch_shapes=[pltpu.VMEM((BLK + tile, GROUP_W), BF16), pltpu.VMEM((BLK + tile, GROUP_W), BF16),
                        pltpu.VMEM((tile, GROUP_W), BF16),
                        pltpu.VMEM((tile, GROUP_W), F32), pltpu.VMEM((tile, GROUP_W), F32)]
                       + (ex.scratch if ex else []),
        compiler_params=_params(dimension_semantics=("arbitrary",)),
    )(qkv, qkv, qkv, qkv, qkv, dc, dc, *(ex.arrays if ex else []))
    return (outs[0], outs[1:]) if ex else outs[0]


def _gather_rows(refs, buf, mid, ts):
    return [_to_token_order(refs[g], ATT_GROUPS[g][1], buf, mid, ts) for g in range(N_GROUPS)]


def _inverse_counts(ts):
    win = np.repeat(np.asarray(POOL_WINDOWS), POOL_GW)
    first = np.minimum(np.arange(ts)[:, None] + 1, win[None, :])
    counts = np.stack([first, np.broadcast_to(win[None, :], (ts, POOL_W))]).astype(np.float32)
    return jnp.asarray(np.float32(1.0) / counts)


def _pool_fwd(ebuf, s2, s4, s8, inv_count, ts):
    n = ts + HALO
    s2[8:n] = ebuf[8:n] + ebuf[7:n - 1]
    s4[16:n] = s2[16:n] + s2[14:n - 2]
    s8[24:n] = s4[24:n] + s4[20:n - 4]
    s16 = s8[32:n] + s8[24:n - 8]
    col = lax.broadcasted_iota(jnp.int32, (ts, POOL_W), 1)
    psum = jnp.where(col < POOL_GW, s2[32:n],
                     jnp.where(col < 2 * POOL_GW, s4[32:n], jnp.where(col < 3 * POOL_GW, s8[32:n], s16)))
    return psum * inv_count - ebuf[32:n]


GROUP_WIN = ((0, 0), (128, 64), (384, 0), (512, 64))
WIN = 256


def _window_weights(w_grp):
    return jnp.stack([jnp.pad(w_grp[g], ((off, WIN - off - POOL_GW), (off, WIN - off - POOL_GW)))
                      for g, (_, off) in enumerate(GROUP_WIN)])


def _group_matmul(xb, wwin_ref, transposed=False):
    outs = []
    for g, (start, _) in enumerate(GROUP_WIN):
        xw = xb[:, start:start + WIN]
        outs.append(_dot_nt(xw, wwin_ref[g]) if transposed else _dot(xw, wwin_ref[g]))
    half = WIN // 2
    return jnp.concatenate([outs[0][:, :half], outs[0][:, half:] + outs[1][:, :half], outs[1][:, half:],
                            outs[2][:, :half], outs[2][:, half:] + outs[3][:, :half], outs[3][:, half:]], axis=1)


def _mix_core(zr, pooled, outs, lses, wwin_ref, scale, wao, wpo):
    mixed = _group_matmul(pooled.astype(BF16), wwin_ref)
    p = mixed * scale
    l0, l1, l2 = lses
    mx = jnp.maximum(jnp.maximum(l0, l1), l2)
    e0, e1, e2 = jnp.exp(l0 - mx), jnp.exp(l1 - mx), jnp.exp(l2 - mx)
    inv = 1.0 / (e0 + e1 + e2)
    wts = (e0 * inv, e1 * inv, e2 * inv)
    a = wts[0] * outs[0] + wts[1] * outs[1] + wts[2] * outs[2]
    att = _dot(a.astype(BF16), wao)
    pol = _dot(p.astype(BF16), wpo)
    sga = _sigmoid(zr[:, POOL_W:POOL_W + D_MODEL].astype(F32))
    sgp = _sigmoid(zr[:, POOL_W + D_MODEL:].astype(F32))
    mg = sga * att + sgp * pol
    return dict(mixed=mixed, p=p, wts=wts, a=a, att=att, pol=pol, sga=sga, sgp=sgp, mg=mg)


def _fill_pool_input(ebuf, zr_ref, halo_ref, t0):
    ts = zr_ref.shape[0]
    halo = halo_ref[...].astype(F32)
    t = t0 - HALO + lax.broadcasted_iota(jnp.int32, (HALO, POOL_W), 0)
    ebuf[0:HALO] = jnp.where(t >= 0, halo, 0.0)
    ebuf[HALO:HALO + ts] = zr_ref[:, 0:POOL_W].astype(F32)


def _mix_fwd(x, zr, ol_dil, wbd, scale, wao, wpo, wout, ts):
    s = x.shape[0]
    n = s // ts

    def body(x_ref, zr_ref, halo_ref, ic_ref, ol0, ol1, ol2, wbd_ref, sc_ref, wao_ref, wpo_ref, wout_ref,
             h1_ref, ebuf, s2, s4, s8, rbuf, rmid):
        i = pl.program_id(0)
        _fill_pool_input(ebuf, zr_ref, halo_ref, i * ts)
        pooled = _pool_fwd(ebuf, s2, s4, s8, ic_ref[...], ts)
        ols = _gather_rows((ol0, ol1, ol2), rbuf, rmid, ts)
        outs, lses = [a[:, :GROUP_W] for a in ols], [a[:, GROUP_W:] for a in ols]
        f = _mix_core(zr_ref[...], pooled, outs, lses, wbd_ref, sc_ref[...], wao_ref[...], wpo_ref[...])
        h1_ref[...] = x_ref[...] + _dot(f["mg"].astype(BF16), wout_ref[...])

    whole = lambda a: pl.BlockSpec(a.shape, lambda i: (0,) * a.ndim)
    idx = lambda i: i
    return pl.pallas_call(
        body, name="mix_fwd", grid=(n,),
        in_specs=[pl.BlockSpec((ts, D_MODEL), lambda i: (i, 0)),
                  pl.BlockSpec((ts, REST_W), lambda i: (i, 0)),
                  pl.BlockSpec((HALO, POOL_W), lambda i: (jnp.maximum(i * (ts // HALO) - 1, 0), 0)),
                  pl.BlockSpec((None, ts, POOL_W), lambda i: (jnp.minimum(i, 1), 0, 0))]
                 + _dil_specs(ts, 2 * GROUP_W, idx)
                 + [whole(wbd), whole(scale), whole(wao), whole(wpo), whole(wout)],
        out_specs=pl.BlockSpec((ts, D_MODEL), lambda i: (i, 0)),
        out_shape=jax.ShapeDtypeStruct((s, D_MODEL), F32),
        scratch_shapes=[pltpu.VMEM((ts + HALO, POOL_W), F32)] * 4 + _order_scratch(ts, 2 * GROUP_W),
        compiler_params=_params(dimension_semantics=("arbitrary",)),
    )(x, zr, zr, _inverse_counts(ts), *ol_dil, wbd, scale, wao, wpo, wout)


def _mix_bwd(dh1, zr, ol_dil, wbd, scale, wao, wpo, wout, ts):
    s = dh1.shape[0]
    n = s // ts

    def body(dh_ref, zr_ref, halo_ref, ic_ref, ol0, ol1, ol2, sc_ref, wbd_hbm, wao_hbm, wpo_hbm, wout_hbm,
             dzr_ref, dc0, dc1, dc2, gsc_ref, gwout_hbm, gwao_hbm, gwpo_hbm, gwbd_hbm,
             ebuf, s2, s4, s8, gbuf, t2, t4, t8, rbuf, rmid,
             wbd_ref, wao_ref, wpo_ref, wout_ref, gwout_ref, gwao_ref, gwpo_ref, gwbd_ref, sem):
        j = pl.program_id(0)
        i = n - 1 - j
        _load_resident(j, [(wbd_hbm, wbd_ref), (wao_hbm, wao_ref), (wpo_hbm, wpo_ref), (wout_hbm, wout_ref)], sem)

        @pl.when(j == 0)
        def _():
            gwout_ref[...] = jnp.zeros_like(gwout_ref)
            gwao_ref[...] = jnp.zeros_like(gwao_ref)
            gwpo_ref[...] = jnp.zeros_like(gwpo_ref)
            gwbd_ref[...] = jnp.zeros_like(gwbd_ref)
            gsc_ref[...] = jnp.zeros_like(gsc_ref)
            gbuf[ts:ts + HALO] = jnp.zeros((HALO, POOL_W), F32)

        _fill_pool_input(ebuf, zr_ref, halo_ref, i * ts)
        inv_count = ic_ref[...]
        pooled = _pool_fwd(ebuf, s2, s4, s8, inv_count, ts)
        ols = _gather_rows((ol0, ol1, ol2), rbuf, rmid, ts)
        outs, lses = [a[:, :GROUP_W] for a in ols], [a[:, GROUP_W:] for a in ols]
        zr = zr_ref[...]
        wao, wpo, wout = wao_ref[...], wpo_ref[...], wout_ref[...]
        scale = sc_ref[...]
        f = _mix_core(zr, pooled, outs, lses, wbd_ref, scale, wao, wpo)

        dhb = dh_ref[...].astype(BF16)
        gwout_ref[...] += _dot(f["mg"].T.astype(BF16), dhb)
        dmg = _dot_nt(dhb, wout)
        sga, sgp, att, pol = f["sga"], f["sgp"], f["att"], f["pol"]
        datt = dmg * sga
        dpol = dmg * sgp
        dzr_ref[:, POOL_W:POOL_W + D_MODEL] = (dmg * att * sga * (1.0 - sga)).astype(BF16)
        dzr_ref[:, POOL_W + D_MODEL:] = (dmg * pol * sgp * (1.0 - sgp)).astype(BF16)
        dattb = datt.astype(BF16)
        dpolb = dpol.astype(BF16)
        gwao_ref[...] += _dot(f["a"].T.astype(BF16), dattb)
        gwpo_ref[...] += _dot(f["p"].T.astype(BF16), dpolb)
        da = _dot_nt(dattb, wao)
        dp = _dot_nt(dpolb, wpo)

        gsc_ref[...] += jnp.sum(f["mixed"] * dp, axis=0, keepdims=True)
        dmixed = (dp * scale).astype(BF16)
        pooled_t = pooled.T.astype(BF16)
        for g, (start, _) in enumerate(GROUP_WIN):
            gwbd_ref[g] += _dot(pooled_t[start:start + WIN, :], dmixed[:, start:start + WIN])
        dpooled = _group_matmul(dmixed, wbd_ref, transposed=True)
        gbuf[0:ts] = dpooled * inv_count
        m = ts + HALO
        t2[0:m - 8] = gbuf[0:m - 8] + gbuf[1:m - 7]
        t4[0:m - 16] = t2[0:m - 16] + t2[2:m - 14]
        t8[0:m - 24] = t4[0:m - 24] + t4[4:m - 20]
        t16 = t8[0:ts] + t8[8:ts + 8]
        col = lax.broadcasted_iota(jnp.int32, (ts, POOL_W), 1)
        back = jnp.where(col < POOL_GW, t2[0:ts],
                         jnp.where(col < 2 * POOL_GW, t4[0:ts], jnp.where(col < 3 * POOL_GW, t8[0:ts], t16)))
        dzr_ref[:, 0:POOL_W] = (back - dpooled).astype(BF16)
        gbuf[ts:ts + HALO] = gbuf[0:HALO]

        head_of = lax.broadcasted_iota(jnp.int32, (ts, GROUP_W), 1) // HEAD_DIM
        prod = da * f["a"]
        inner = jnp.zeros((ts, GROUP_W), F32)
        for h in range(4):
            hm = head_of == h
            tot = jnp.sum(jnp.where(hm, prod, 0.0), axis=1, keepdims=True)
            inner = jnp.where(hm, tot, inner)
        for g, dc_ref in enumerate((dc0, dc1, dc2)):
            both = jnp.concatenate([f["wts"][g] * da, -f["wts"][g] * inner], axis=1)
            _to_class_order(both, ATT_GROUPS[g][1], rbuf, rmid, dc_ref, 0)

        @pl.when(j == n - 1)
        def _():
            wout_ref[...] = gwout_ref[...].astype(BF16)
            wao_ref[...] = gwao_ref[...].astype(BF16)
            wpo_ref[...] = gwpo_ref[...].astype(BF16)
            tn = D_MODEL // N_DEV
            pairs = [(wout_ref, gwout_hbm), (gwbd_ref, gwbd_hbm)]
            for staged, dst in ((wao_ref, gwao_hbm), (wpo_ref, gwpo_hbm)):
                pairs += [(staged.at[:, pl.ds(k * tn, tn)], dst.at[k]) for k in range(N_DEV)]
            copies = [pltpu.make_async_copy(src, dst, sem.at[k]) for k, (src, dst) in enumerate(pairs)]
            for cp in copies:
                cp.start()
            for cp in copies:
                cp.wait()

    idx = lambda j: n - 1 - j
    dc_shapes = _dil_shapes(s, 2 * GROUP_W, F32)
    weights = (wbd, wao, wpo, wout)
    grad_shapes = [(D_MODEL, D_MODEL), (GROUP_W, D_MODEL), (POOL_W, D_MODEL), (len(GROUP_WIN), WIN, WIN)]
    tile_buf = pltpu.VMEM((ts + HALO, POOL_W), F32)
    outs = pl.pallas_call(
        body, name="mix_bwd", grid=(n,),
        in_specs=[pl.BlockSpec((ts, D_MODEL), lambda j: (idx(j), 0)),
                  pl.BlockSpec((ts, REST_W), lambda j: (idx(j), 0)),
                  pl.BlockSpec((HALO, POOL_W), lambda j: (jnp.maximum(idx(j) * (ts // HALO) - 1, 0), 0)),
                  pl.BlockSpec((None, ts, POOL_W), lambda j: (jnp.minimum(idx(j), 1), 0, 0))]
                 + _dil_specs(ts, 2 * GROUP_W, idx)
                 + [pl.BlockSpec((1, POOL_W), lambda j: (0, 0))] + [ANY] * 4,
        out_specs=[pl.BlockSpec((ts, REST_W), lambda j: (idx(j), 0))]
                  + _dil_specs(ts, 2 * GROUP_W, idx)
                  + [pl.BlockSpec((1, POOL_W), lambda j: (0, 0))] + [ANY] * 4,
        out_shape=[jax.ShapeDtypeStruct((s, REST_W), BF16)] + dc_shapes
                  + [jax.ShapeDtypeStruct((1, POOL_W), F32),
                     jax.ShapeDtypeStruct((D_MODEL, D_MODEL), BF16),
                     jax.ShapeDtypeStruct((N_DEV, GROUP_W, D_MODEL // N_DEV), BF16),
                     jax.ShapeDtypeStruct((N_DEV, POOL_W, D_MODEL // N_DEV), BF16),
                     jax.ShapeDtypeStruct(grad_shapes[3], F32)],
        scratch_shapes=[tile_buf] * 8 + _order_scratch(ts, 2 * GROUP_W)
                       + [pltpu.VMEM(w.shape, BF16) for w in weights]
                       + [pltpu.VMEM(shape, F32) for shape in grad_shapes]
                       + [pltpu.SemaphoreType.DMA((2 + 2 * N_DEV,))],
        compiler_params=_params(dimension_semantics=("arbitrary",)),
    )(dh1, zr, zr, _inverse_counts(ts), *ol_dil, scale, wbd, wao, wpo, wout)
    dzr, dc_dil, g_scale = outs[0], outs[1:4], outs[4]
    g_out, g_ao, g_po, g_bd = outs[5:]
    return dzr, dc_dil, (g_out.reshape(N_DEV, D_MODEL // N_DEV, D_MODEL), g_ao, g_po, g_bd, g_scale)


FF_CHUNK = 1024


STAT_ROWS = 8


def _mlp_fwd_bwd(h1, tgt, gains, wmi, wmo, ts):
    s = h1.shape[0]
    n = s // ts
    nchunk = D_FF // FF_CHUNK

    def body(h1_ref, t_ref, g_ref, wmi_hbm, wmo_hbm, dh1_ref, mdt_ref, hd_ref, stat_ref, wmi, wmo, relu_buf, sem):
        i = pl.program_id(0)
        _load_resident(i, _column_blocks(wmi_hbm, wmi) + [(wmo_hbm, wmo)], sem)

        @pl.when(i == 0)
        def _():
            stat_ref[...] = jnp.zeros_like(stat_ref)

        h1 = h1_ref[...]
        g2 = g_ref[0:1, :]
        g3 = g_ref[1:2, :]
        r2 = _rms(h1)
        n2 = h1 * r2
        m = n2 * g2
        mb = m.astype(BF16)
        mdt_ref[0] = m.T.astype(BF16)
        h2 = h1
        for c in range(nchunk):
            cols = slice(c * FF_CHUNK, (c + 1) * FF_CHUNK)
            rl = jnp.maximum(_dot(mb, wmi[:, cols]), 0.0)
            relu_buf[:, cols] = rl
            hb = (rl * rl).astype(BF16)
            hd_ref[0, :, cols] = hb
            h2 = h2 + _dot(hb, wmo[cols, :])
        r3 = _rms(h2)
        n3 = h2 * r3
        diff = n3 * g3 - t_ref[...]
        loss = jnp.sum(0.5 * jnp.sum(diff * diff, axis=1, keepdims=True) / D_MODEL, axis=0, keepdims=True)
        dy = diff * (1.0 / D_MODEL)
        dg3 = jnp.sum(dy * n3, axis=0, keepdims=True)
        dh2 = _rms_bwd(dy * g3, n3, r3)
        dh2b = dh2.astype(BF16)
        mdt_ref[1] = dh2.T.astype(BF16)
        dm = jnp.zeros((ts, D_MODEL), F32)
        for c in range(nchunk):
            cols = slice(c * FF_CHUNK, (c + 1) * FF_CHUNK)
            dfb = (_dot_nt(dh2b, wmo[cols, :]) * (2.0 * relu_buf[:, cols])).astype(BF16)
            hd_ref[1, :, cols] = dfb
            dm = dm + _dot_nt(dfb, wmi[:, cols])
        dg2 = jnp.sum(dm * n2, axis=0, keepdims=True)
        dh1_ref[...] = dh2 + _rms_bwd(dm * g2, n2, r2)
        row = lax.broadcasted_iota(jnp.int32, (STAT_ROWS, D_MODEL), 0)
        stat_ref[...] += jnp.where(row == 0, loss, jnp.where(row == 1, dg2, jnp.where(row == 2, dg3, 0.0)))

    row = lambda w: pl.BlockSpec((ts, w), lambda i: (i, 0))
    return pl.pallas_call(
        body, name="mlp_fwd_bwd", grid=(n,),
        in_specs=[row(D_MODEL), row(D_MODEL), pl.BlockSpec((2, D_MODEL), lambda i: (0, 0)), ANY, ANY],
        out_specs=[row(D_MODEL), pl.BlockSpec((2, D_MODEL, ts), lambda i: (0, 0, i)),
                   pl.BlockSpec((2, ts, D_FF), lambda i: (0, i, 0)),
                   pl.BlockSpec((STAT_ROWS, D_MODEL), lambda i: (0, 0))],
        out_shape=[jax.ShapeDtypeStruct((s, D_MODEL), F32), jax.ShapeDtypeStruct((2, D_MODEL, s), BF16),
                   jax.ShapeDtypeStruct((2, s, D_FF), BF16), jax.ShapeDtypeStruct((STAT_ROWS, D_MODEL), F32)],
        scratch_shapes=[pltpu.VMEM((D_MODEL, D_FF), BF16), pltpu.VMEM((D_FF, D_MODEL), BF16),
                        pltpu.VMEM((ts, D_FF), F32), pltpu.SemaphoreType.DMA((N_DEV + 1,))],
        compiler_params=_params(dimension_semantics=("arbitrary",)),
    )(h1, tgt, gains, wmi, wmo)


def _in_proj_bwd(x, dh1, dzr, dqkv_dil, g1, w_in, ts):
    s = x.shape[0]
    n = s // ts

    def body(x_ref, dh_ref, dzr_ref, q0, q1, q2, g_ref, w_hbm, dx_ref, dz_ref, dg_ref, w_ref, qbuf, qmid, sem):
        i = pl.program_id(0)
        _load_resident(i, _column_blocks(w_hbm, w_ref), sem)

        @pl.when(i == 0)
        def _():
            dg_ref[...] = jnp.zeros_like(dg_ref)

        for g, dqkv in enumerate(_gather_rows((q0, q1, q2), qbuf, qmid, ts)):
            for sec in range(3):
                c0 = sec * ATT_W + g * GROUP_W
                dz_ref[:, c0:c0 + GROUP_W] = dqkv[:, sec * GROUP_W:(sec + 1) * GROUP_W].astype(BF16)
        dz_ref[:, 3 * ATT_W:] = dzr_ref[...]
        du = _dot_nt(dz_ref[...], w_ref[...])
        x = x_ref[...]
        r1 = _rms(x)
        n1 = x * r1
        g1 = g_ref[...]
        dg_ref[...] += jnp.sum(du * n1, axis=0, keepdims=True)
        dx_ref[...] = dh_ref[...] + _rms_bwd(du * g1, n1, r1)

    row = lambda w: pl.BlockSpec((ts, w), lambda i: (i, 0))
    vec = pl.BlockSpec((1, D_MODEL), lambda i: (0, 0))
    return pl.pallas_call(
        body, name="in_proj_bwd", grid=(n,),
        in_specs=[row(D_MODEL), row(D_MODEL), row(REST_W)] + _dil_specs(ts, ATT_W, lambda i: i) + [vec, ANY],
        out_specs=[row(D_MODEL), row(N_IN), vec],
        out_shape=[jax.ShapeDtypeStruct((s, D_MODEL), F32), jax.ShapeDtypeStruct((s, N_IN), BF16),
                   jax.ShapeDtypeStruct((1, D_MODEL), F32)],
        scratch_shapes=[pltpu.VMEM((D_MODEL, N_IN), BF16)] + _order_scratch(ts, ATT_W)
                       + [pltpu.SemaphoreType.DMA((N_DEV,))],
        compiler_params=_params(dimension_semantics=("arbitrary",)),
    )(x, dh1, dzr, *dqkv_dil, g1, w_in)


GRAD_PASS = 2


def _weight_grad(at, b, name, transpose_out, tk=4096, at_slot=None, b_slot=None, ex=None):
    m, s = at.shape[-2:]
    nn = b.shape[-1]
    tn = nn // N_DEV
    tk = min(tk, s)
    nk = s // tk
    npass = N_DEV // GRAD_PASS
    oshape = (tn, m) if transpose_out else (m, tn)
    owner = lambda jj: N_DEV - 1 - jj
    order = jnp.stack([_linear(_peer(_my_place(), owner(jj))) for jj in range(N_DEV)]).astype(jnp.int32)
    sent_in = lambda p: [jj for jj in range(N_DEV - 1) if jj // GRAD_PASS == p]

    nex = ex.nw if ex else 0

    def body(order_ref, at_ref, *refs):
        (b_ref,), ex_ins, (got_ref,), ex_outs, (acc, res, send_sems, recv_sems, local_sem), ex_sems = _split_refs(
            refs, (1, nex, 1, nex, 5, 3 if ex else 0))
        j, k = pl.program_id(0), pl.program_id(1)
        me = _my_place()
        mine = _linear(me)
        if ex:
            _carry_start(ex, j * nk + k, ex_ins, ex_outs, ex_sems)

        def send(jj):
            peer = _peer(me, owner(jj))
            return pltpu.make_async_remote_copy(
                src_ref=res.at[(jj // GRAD_PASS) % 2, peer[2]], dst_ref=got_ref.at[mine],
                send_sem=send_sems.at[jj], recv_sem=recv_sems.at[jj],
                device_id=peer, device_id_type=pl.DeviceIdType.MESH)

        @pl.when(k == 0)
        def _():
            acc[...] = jnp.zeros_like(acc)

        acc[...] += _dot(at_ref[...], b_ref[...])

        @pl.when(k == nk - 1)
        def _():
            for p in range(2, npass):
                @pl.when(j == p)
                def _():
                    for jj in sent_in(p - 2):
                        send(jj).wait_send()

            for core in range(GRAD_PASS):
                r = acc[:, core * tn:(core + 1) * tn]
                res[j % 2, core] = (r.T if transpose_out else r).astype(BF16)
            for p in range(npass):
                @pl.when(j == p)
                def _():
                    for jj in sent_in(p):
                        send(jj).start()

            @pl.when(j == npass - 1)
            def _():
                own = pltpu.make_async_copy(res.at[(npass - 1) % 2, me[2]], got_ref.at[mine], local_sem.at[0])
                own.start()
                for p in range(max(npass - 2, 0), npass):
                    for jj in sent_in(p):
                        send(jj).wait_send()
                for jj in range(N_DEV - 1):
                    send(jj).wait_recv()
                own.wait()

        if ex:
            _carry_wait(ex, j * nk + k, npass * nk - 1, ex_ins, ex_outs, ex_sems)

    if at_slot is None:
        at_spec = pl.BlockSpec((m, tk), lambda j, k, o: (0, k))
    else:
        at_spec = pl.BlockSpec((None, m, tk), lambda j, k, o: (at_slot, 0, k))
    chip_cols = lambda j, o: o[GRAD_PASS * j] // GRAD_PASS
    if b_slot is None:
        b_spec = pl.BlockSpec((tk, GRAD_PASS * tn), lambda j, k, o: (k, chip_cols(j, o)))
    else:
        b_spec = pl.BlockSpec((None, tk, GRAD_PASS * tn), lambda j, k, o: (b_slot, k, chip_cols(j, o)))
    outs = pl.pallas_call(
        body, name=name,
        grid_spec=pltpu.PrefetchScalarGridSpec(
            num_scalar_prefetch=1, grid=(npass, nk),
            in_specs=[at_spec, b_spec] + (ex.specs if ex else []),
            out_specs=[ANY] + (ex.specs if ex else []),
            scratch_shapes=[pltpu.VMEM((m, GRAD_PASS * tn), F32), pltpu.VMEM((2, GRAD_PASS) + oshape, BF16),
                            pltpu.SemaphoreType.DMA((N_DEV - 1,)), pltpu.SemaphoreType.DMA((N_DEV - 1,)),
                            pltpu.SemaphoreType.DMA((1,))] + (ex.scratch if ex else [])),
        out_shape=[jax.ShapeDtypeStruct((N_DEV,) + oshape, BF16)] + (ex.out_shape if ex else []),
        compiler_params=_params(dimension_semantics=("arbitrary", "arbitrary")),
    )(order, at, b, *(ex.arrays if ex else []))
    return (outs[0], outs[1:]) if ex else outs[0]


def _my_place():
    x, y, c = lax.axis_index("x"), lax.axis_index("y"), lax.axis_index("c")
    return x, y, c


def _peer(place, k):
    x, y, c = place
    return (1 - x if k & 4 else x, 1 - y if k & 2 else y, 1 - c if k & 1 else c)


def _linear(place):
    x, y, c = place
    return 4 * x + 2 * y + c


class _Exchange:
    passes_on = False

    def __init__(self, arrays, gather):
        self.arrays, self.gather, self.nw = list(arrays), list(gather), len(arrays)
        self.out_shape = []
        for a, g in zip(arrays, gather):
            block = a.shape if g else a.shape[1:]
            self.out_shape.append(jax.ShapeDtypeStruct((N_DEV,) + tuple(block), a.dtype))
        self.specs = [ANY] * self.nw
        self.scratch = [pltpu.SemaphoreType.DMA((self.nw, N_DEV - 1)), pltpu.SemaphoreType.DMA((self.nw, N_DEV - 1)),
                        pltpu.SemaphoreType.DMA((self.nw,))]

    def _copies(self, ins, outs, sems):
        send_sems, recv_sems, local_sems = sems
        me = _my_place()
        mine = _linear(me)
        copies = []
        for w in range(self.nw):
            src = ins[w] if self.gather[w] else ins[w].at[mine]
            copies.append(pltpu.make_async_copy(src, outs[w].at[mine], local_sems.at[w]))
        for k in range(1, N_DEV):
            peer = _peer(me, k)
            for w in range(self.nw):
                src = ins[w] if self.gather[w] else ins[w].at[_linear(peer)]
                copies.append(pltpu.make_async_remote_copy(
                    src_ref=src, dst_ref=outs[w].at[mine],
                    send_sem=send_sems.at[w, k - 1], recv_sem=recv_sems.at[w, k - 1],
                    device_id=peer, device_id_type=pl.DeviceIdType.MESH))
        return copies

    def start(self, ins, outs, sems):
        for cp in self._copies(ins, outs, sems):
            cp.start()

    def wait(self, ins, outs, sems):
        copies = self._copies(ins, outs, sems)
        for cp in copies[self.nw:]:
            cp.wait_recv()
        for cp in copies[self.nw:]:
            cp.wait_send()
        for cp in copies[:self.nw]:
            cp.wait()


class _Gather:
    passes_on = True

    def __init__(self, arrays):
        self.arrays, self.nw = list(arrays), len(arrays)
        self.out_shape = [jax.ShapeDtypeStruct((N_DEV,) + tuple(a.shape), a.dtype) for a in arrays]
        self.specs = [ANY] * self.nw
        self.scratch = [pltpu.SemaphoreType.DMA((self.nw, N_DEV - 1)), pltpu.SemaphoreType.DMA((self.nw, N_DEV - 1)),
                        pltpu.SemaphoreType.DMA((self.nw,))]

    @staticmethod
    def _places():
        x, y, c = _my_place()
        return (x, y, c), (x, y, 1 - c), [(1 - x, y), (x, 1 - y), (1 - x, 1 - y)]

    @staticmethod
    def _copy(outs, sems, w, k, block, to, src=None):
        rows = outs[w].at[_linear(block)]
        return pltpu.make_async_remote_copy(
            src_ref=rows if src is None else src, dst_ref=rows, send_sem=sems[0].at[w, k], recv_sem=sems[1].at[w, k],
            device_id=to, device_id_type=pl.DeviceIdType.MESH)

    def _first(self, ins, outs, sems, w):
        me, sibling, chips = self._places()
        return ([self._copy(outs, sems, w, 0, me, sibling, src=ins[w])]
                + [self._copy(outs, sems, w, 1 + j, me, (*chip, me[2]), src=ins[w]) for j, chip in enumerate(chips)])

    def _passed(self, outs, sems, w):
        me, sibling, chips = self._places()
        return [self._copy(outs, sems, w, 4 + j, (*chip, me[2]), sibling) for j, chip in enumerate(chips)]

    def _local(self, ins, outs, sems, w):
        return pltpu.make_async_copy(ins[w], outs[w].at[_linear(self._places()[0])], sems[2].at[w])

    def start(self, ins, outs, sems):
        for w in range(self.nw):
            self._local(ins, outs, sems, w).start()
            for cp in self._first(ins, outs, sems, w):
                cp.start()

    def pass_on(self, ins, outs, sems):
        me, sibling, chips = self._places()
        for j, chip in enumerate(chips):
            for w in range(self.nw):
                self._copy(outs, sems, w, 1 + j, (*chip, me[2]), me).wait_recv()
                self._passed(outs, sems, w)[j].start()

    def wait(self, ins, outs, sems):
        me, sibling, chips = self._places()
        for w in range(self.nw):
            self._copy(outs, sems, w, 0, sibling, me).wait_recv()
            for j, chip in enumerate(chips):
                self._copy(outs, sems, w, 4 + j, (*chip, sibling[2]), me).wait_recv()
            for cp in self._first(ins, outs, sems, w) + self._passed(outs, sems, w):
                cp.wait_send()
            self._local(ins, outs, sems, w).wait()


def _sum_parts(p_ref):
    g = p_ref[0].astype(F32)
    for j in range(1, N_DEV):
        g = g + p_ref[j].astype(F32)
    return g


def _adam_update(g, w, m, v):
    nm = ADAM_B1 * m + (1.0 - ADAM_B1) * g
    nv = ADAM_B2 * v + (1.0 - ADAM_B2) * (g * g)
    m_hat = nm / (1.0 - ADAM_B1 ** ADAM_STEP)
    v_hat = nv / (1.0 - ADAM_B2 ** ADAM_STEP)
    return -ADAM_LR * (m_hat / (jnp.sqrt(v_hat) + ADAM_EPS) + ADAM_WD * w), nm, nv


def _adamw(parts, w, m, v, name, tr):
    rows, cols = w.shape
    tr = min(tr, rows)
    while rows % tr:
        tr //= 2
    assert tr % 8 == 0, (rows, tr)

    def body(p_ref, w_ref, m_ref, v_ref, g_ref, d_ref, nm_ref, nv_ref):
        g = _sum_parts(p_ref)
        g_ref[...] = g
        d_ref[...], nm_ref[...], nv_ref[...] = _adam_update(g, w_ref[...], m_ref[...], v_ref[...])

    blk = pl.BlockSpec((tr, cols), lambda i: (i, 0))
    return pl.pallas_call(
        body, name=name, grid=(rows // tr,),
        in_specs=[pl.BlockSpec((N_DEV, tr, cols), lambda i: (0, i, 0)), blk, blk, blk],
        out_specs=[blk] * 4,
        out_shape=[jax.ShapeDtypeStruct((rows, cols), F32)] * 4,
        compiler_params=_params(dimension_semantics=("arbitrary",)),
    )(parts, w, m, v)


def _adamw_vectors(got_stats, got_g1, got_scale, params):
    def body(st_ref, g1_ref, sc_ref, *refs):
        ins, outs = refs[:12], refs[12:]
        stats = _sum_parts(st_ref)
        outs[0][...] = stats[0:1, 0:1]
        grads = (_sum_parts(g1_ref), _sum_parts(sc_ref), stats[1:2, :], stats[2:3, :])
        for k, g in enumerate(grads):
            w_ref, m_ref, v_ref = ins[3 * k:3 * k + 3]
            g_ref, d_ref, nm_ref, nv_ref = outs[1 + 4 * k:5 + 4 * k]
            g_ref[...] = g
            d_ref[...], nm_ref[...], nv_ref[...] = _adam_update(g, w_ref[...], m_ref[...], v_ref[...])

    flat = [a for p in params for a in p]
    out_shape = [jax.ShapeDtypeStruct((1, 1), F32)]
    for w, _, _ in params:
        out_shape += [jax.ShapeDtypeStruct(w.shape, F32)] * 4
    res = pl.pallas_call(body, name="adamw_vectors", out_shape=out_shape)(got_stats, got_g1, got_scale, *flat)
    return res[0], [res[1 + 4 * k:5 + 4 * k] for k in range(len(params))]


def kernel(x, norm_mix_g, w_in, w_att_out, w_pool_grp, pool_scale, w_pool_out, w_out, norm_mlp_g, w_mlp_in, w_mlp_out, norm_final_g, loss_target, m_norm_mix_g, m_w_in, m_w_att_out, m_w_pool_grp, m_pool_scale, m_w_pool_out, m_w_out, m_norm_mlp_g, m_w_mlp_in, m_w_mlp_out, m_norm_final_g, v_norm_mix_g, v_w_in, v_w_att_out, v_w_pool_grp, v_pool_scale, v_w_pool_out, v_w_out, v_norm_mlp_g, v_w_mlp_in, v_w_mlp_out, v_norm_final_g):
    x, tgt = x[0], loss_target[0]
    s = x.shape[0]
    g1, g2, g3 = norm_mix_g, norm_mlp_g, norm_final_g.reshape(1, D_MODEL)
    shards = [w_in[0], w_att_out[0], w_pool_out[0], w_out[0], w_mlp_in[0], w_mlp_out[0]]
    wire = [a.astype(BF16) for a in shards]
    cols = lambda a: jnp.transpose(a, (1, 0, 2)).reshape(a.shape[1], N_DEV * a.shape[2])
    rows = lambda a: a.reshape(N_DEV * a.shape[1], a.shape[2])
    wbd =_window_weights(w_pool_grp[0]).astype(BF16)

    (u, ut), (f_in,) = _rms_u(x, g1, 1024, _Gather(wire[:1]))
    (qkv0, qkv1, qkv2, zr), later = _in_proj_fwd(u, f_in, 1024, _Gather(wire[1:]))
    f_ao, f_po, f_out, f_mi, f_mo = cols(later[0]), cols(later[1]), rows(later[2]), later[3], rows(later[4])
    qkv_dil = (qkv0, qkv1, qkv2)
    flat = lambda a: a.reshape(s, a.shape[-1])
    shaped = lambda a, g: a if g == 0 else a.reshape(ATT_GROUPS[g][1], s // ATT_GROUPS[g][1], a.shape[-1])
    ol_dil = [shaped(_attn_fwd(flat(qkv_dil[g]), g), g) for g in range(N_GROUPS)]
    h1 = _mix_fwd(x, zr, ol_dil, wbd, pool_scale, f_ao, f_po, f_out, ts=512)

    dh1, mdt, hd, stats = _mlp_fwd_bwd(h1, tgt, jnp.concatenate([g2, g3], axis=0), f_mi, f_mo, ts=256)
    got = {"w_mlp_in": _weight_grad(mdt, hd, "grad_w_mlp_in", transpose_out=False, at_slot=0, b_slot=1),
           "w_mlp_out": _weight_grad(mdt, hd, "grad_w_mlp_out", transpose_out=True, at_slot=1, b_slot=0)}
    dzr, dc_dil, (g_out, g_ao, g_po, g_bd, g_scale) = _mix_bwd(
        dh1, zr, ol_dil, wbd, pool_scale, f_ao, f_po, f_out, ts=256)
    g_grp = jnp.stack([g_bd[g, off:off + POOL_GW, off:off + POOL_GW] for g, (_, off) in enumerate(GROUP_WIN)])
    early = [_Exchange([g_out], [False]), _Exchange([g_ao, g_po], [False, False]),
             _Exchange([g_grp.reshape(4 * POOL_GW, POOL_GW)], [True])]
    dqkv_dil, arrived = [], []
    for g in range(N_GROUPS):
        dqkv, rode = _attn_bwd(flat(qkv_dil[g]), flat(dc_dil[g]), g, early[g])
        dqkv_dil.append(shaped(dqkv, g))
        arrived += list(rode)
    got["w_out"], got["w_att_out"], got["w_pool_out"], got_grp = arrived
    dx, dz, dg1 = _in_proj_bwd(x, dh1, dzr, dqkv_dil, g1, f_in, ts=512)
    got["w_in"], got_vectors = _weight_grad(ut, dz, "grad_w_in", transpose_out=False,
                                            ex=_Exchange([stats, dg1, g_scale], [True] * 3))

    names = ["w_in", "w_att_out", "w_pool_out", "w_out", "w_mlp_in", "w_mlp_out"]
    ms = [m_w_in, m_w_att_out, m_w_pool_out, m_w_out, m_w_mlp_in, m_w_mlp_out]
    vs = [v_w_in, v_w_att_out, v_w_pool_out, v_w_out, v_w_mlp_in, v_w_mlp_out]
    upd = {}
    for k, name in enumerate(names):
        res = _adamw(got[name], shards[k], ms[k][0], vs[k][0], "adamw_" + name, tr=512)
        upd[name] = [a[None] for a in res]

    as_rows = lambda a: a.reshape(4 * POOL_GW, POOL_GW)
    res = _adamw(got_grp, as_rows(w_pool_grp), as_rows(m_w_pool_grp), as_rows(v_w_pool_grp), "adamw_w_pool_grp", tr=2048)
    upd["w_pool_grp"] = [a.reshape(w_pool_grp.shape) for a in res]
    vectors = {"norm_mix_g": (norm_mix_g, m_norm_mix_g, v_norm_mix_g), "pool_scale": (pool_scale, m_pool_scale, v_pool_scale),
               "norm_mlp_g": (norm_mlp_g, m_norm_mlp_g, v_norm_mlp_g),
               "norm_final_g": (norm_final_g, m_norm_final_g, v_norm_final_g)}
    loss, vector_res = _adamw_vectors(*got_vectors, [tuple(a.reshape(1, -1) for a in p) for p in vectors.values()])
    for (name, (w, _, _)), res in zip(vectors.items(), vector_res):
        upd[name] = [a.reshape(w.shape) for a in res]

    order = ["norm_mix_g", "w_in", "w_att_out", "w_pool_grp", "pool_scale", "w_pool_out", "w_out", "norm_mlp_g",
             "w_mlp_in", "w_mlp_out", "norm_final_g"]
    out = [loss.reshape(()), dx[None]]
    for q in range(4):
        out += [upd[name][q] for name in order]
    return tuple(out)
```

```python
import jax
import jax.numpy as jnp
import numpy as np
from jax import lax
from jax.experimental import pallas as pl
from jax.experimental.pallas import tpu as pltpu

F32 = jnp.float32
BF16 = jnp.bfloat16

D_MODEL = 1024
HEAD_DIM = 64
GROUP_W = 256
HEADS = GROUP_W // HEAD_DIM
ATT_GROUPS = ((128, 1), (512, 4), (2048, 16))
N_GROUPS = len(ATT_GROUPS)
BLK = 128
ATT_W = N_GROUPS * GROUP_W
POOL_WINDOWS = (2, 4, 8, 16)
POOL_GW = 192
POOL_W = len(POOL_WINDOWS) * POOL_GW
D_FF = 4096
N_IN = 5120
REST_W = N_IN - 3 * ATT_W
NORM_EPS = 1e-6
ALIBI_MAX_BIAS = 8.0
N_DEV = 8
HALO = 32

ADAM_LR = 0.001
ADAM_B1 = 0.9
ADAM_B2 = 0.999
ADAM_EPS = 1e-08
ADAM_WD = 0.01
ADAM_STEP = 10

VMEM_LIMIT = 56 * 1024 * 1024
AHEAD = 2
AHEAD_VMEM_LIMIT = 62 * 1024 * 1024
ANY = pl.BlockSpec(memory_space=pl.ANY)


def _params(vmem_limit_bytes=VMEM_LIMIT, **kw):
    return pltpu.CompilerParams(vmem_limit_bytes=vmem_limit_bytes, **kw)


def _dot(a, b):
    return jnp.dot(a, b, preferred_element_type=F32)


def _dot_nt(a, b):
    return lax.dot_general(a, b, (((1,), (1,)), ((), ())), preferred_element_type=F32)


def _dot_tn(a, b):
    return lax.dot_general(a, b, (((0,), (0,)), ((), ())), preferred_element_type=F32)


def _slope(group, h):
    return 2.0 ** (-ALIBI_MAX_BIAS * (HEADS * group + h + 1.0) / (N_GROUPS * HEADS))


def _load_resident(step, pairs, sem):
    @pl.when(step == 0)
    def _():
        copies = [pltpu.make_async_copy(src, dst, sem.at[n]) for n, (src, dst) in enumerate(pairs)]
        for cp in copies:
            cp.start()
        for cp in copies:
            cp.wait()


LANES = 128


FAST_STRIDE = 4


def _to_class_order(value, dil, buf, mid, ref, col0):
    ts, w = value.shape
    if dil == 1:
        ref[:, col0:col0 + w] = value.astype(ref.dtype)
        return
    outer = dil // FAST_STRIDE if dil > FAST_STRIDE else 1
    for c in range(w // LANES):
        cols = slice(col0 + c * LANES, col0 + (c + 1) * LANES)
        buf[c] = value[:, c * LANES:(c + 1) * LANES]
        if outer == 1:
            for r in range(dil):
                ref[r, :, cols] = buf[c, pl.ds(r, ts // dil, stride=dil), :].astype(ref.dtype)
            continue
        for r0 in range(FAST_STRIDE):
            mid[r0, c] = buf[c, pl.ds(r0, ts // FAST_STRIDE, stride=FAST_STRIDE), :]
            for r1 in range(outer):
                ref[FAST_STRIDE * r1 + r0, :, cols] = mid[r0, c, pl.ds(r1, ts // dil, stride=outer), :].astype(ref.dtype)


def _to_token_order(ref, dil, buf, mid, ts):
    if dil == 1:
        return ref[...].astype(F32)
    w = ref.shape[-1]
    outer = dil // FAST_STRIDE if dil > FAST_STRIDE else 1
    for c in range(w // LANES):
        cols = slice(c * LANES, (c + 1) * LANES)
        if outer == 1:
            for r in range(dil):
                buf[c, pl.ds(r, ts // dil, stride=dil), :] = ref[r, :, cols].astype(F32)
            continue
        for r0 in range(FAST_STRIDE):
            for r1 in range(outer):
                mid[r0, c, pl.ds(r1, ts // dil, stride=outer), :] = ref[FAST_STRIDE * r1 + r0, :, cols].astype(F32)
            buf[c, pl.ds(r0, ts // FAST_STRIDE, stride=FAST_STRIDE), :] = mid[r0, c]
    return jnp.concatenate([buf[c] for c in range(w // LANES)], axis=1)


def _order_scratch(ts, w):
    return [pltpu.VMEM((w // LANES, ts, LANES), F32), pltpu.VMEM((FAST_STRIDE, w // LANES, ts // FAST_STRIDE, LANES), F32)]


def _dil_shapes(s, width, dtype):
    return [jax.ShapeDtypeStruct((s, width) if d == 1 else (d, s // d, width), dtype) for _, d in ATT_GROUPS]


def _dil_specs(ts, width, idx):
    return [pl.BlockSpec((ts, width), lambda i: (idx(i), 0)) if d == 1 else
            pl.BlockSpec((d, ts // d, width), lambda i: (0, idx(i), 0)) for _, d in ATT_GROUPS]


def _column_blocks(stacked, dst):
    tn = stacked.shape[2]
    return [(stacked.at[j], dst.at[:, pl.ds(j * tn, tn)]) for j in range(N_DEV)]


def _sigmoid(x):
    return 0.5 * jnp.tanh(0.5 * x) + 0.5


def _rms(x):
    return lax.rsqrt(jnp.mean(x * x, axis=-1, keepdims=True) + NORM_EPS)


def _rms_bwd(dn, n, r):
    return r * (dn - n * jnp.mean(dn * n, axis=-1, keepdims=True))


def _split_refs(refs, counts):
    out, at = [], 0
    for c in counts:
        out.append(refs[at:at + c])
        at += c
    return out


def _carry_start(ex, step, ins, outs, sems):
    @pl.when(step == 0)
    def _():
        ex.start(ins, outs, sems)


def _carry_wait(ex, step, last, ins, outs, sems, pass_at=None):
    if ex.passes_on:
        @pl.when(step == (last if pass_at is None else pass_at))
        def _():
            ex.pass_on(ins, outs, sems)

    @pl.when(step == last)
    def _():
        ex.wait(ins, outs, sems)


def _rms_u(x, g1, ts, ex):
    s = x.shape[0]
    n = s // ts

    def body(*refs):
        (x_ref, g_ref), ex_ins, (u_ref, ut_ref), ex_outs, ex_sems = _split_refs(refs, (2, ex.nw, 2, ex.nw, 3))
        i = pl.program_id(0)
        _carry_start(ex, i, ex_ins, ex_outs, ex_sems)
        x = x_ref[...]
        u = x * _rms(x) * g_ref[...]
        u_ref[...] = u.astype(BF16)
        ut_ref[...] = u.T.astype(BF16)
        _carry_wait(ex, i, n - 1, ex_ins, ex_outs, ex_sems)

    outs = pl.pallas_call(
        body, name="rms_u", grid=(n,),
        in_specs=[pl.BlockSpec((ts, D_MODEL), lambda i: (i, 0)), pl.BlockSpec((1, D_MODEL), lambda i: (0, 0))]
                 + ex.specs,
        out_specs=[pl.BlockSpec((ts, D_MODEL), lambda i: (i, 0)), pl.BlockSpec((D_MODEL, ts), lambda i: (0, i))]
                  + ex.specs,
        out_shape=[jax.ShapeDtypeStruct((s, D_MODEL), BF16), jax.ShapeDtypeStruct((D_MODEL, s), BF16)] + ex.out_shape,
        scratch_shapes=ex.scratch,
        compiler_params=_params(dimension_semantics=("arbitrary",)),
    )(x, g1, *ex.arrays)
    return outs[:2], outs[2:]


def _in_proj_fwd(u, w_in, ts, ex):
    s = u.shape[0]
    n = s // ts
    dils = [d for _, d in ATT_GROUPS]

    def body(*refs):
        (u_ref, w_hbm), ex_ins, (q0_ref, q1_ref, q2_ref, zr_ref), ex_outs, (w_ref, zbuf, zmid, sem), ex_sems = (
            _split_refs(refs, (2, ex.nw, 4, ex.nw, 4, 3)))
        i = pl.program_id(0)
        _carry_start(ex, i, ex_ins, ex_outs, ex_sems)
        _load_resident(i, _column_blocks(w_hbm, w_ref), sem)
        ub = u_ref[...]
        outs = (q0_ref, q1_ref, q2_ref)
        for sec in range(3):
            for g in range(N_GROUPS):
                c0 = sec * ATT_W + g * GROUP_W
                zc = _dot(ub, w_ref[:, c0:c0 + GROUP_W])
                _to_class_order(zc, dils[g], zbuf, zmid, outs[g], sec * GROUP_W)
        for c0 in range(0, REST_W, 256):
            zr_ref[:, c0:c0 + 256] = _dot(ub, w_ref[:, 3 * ATT_W + c0:3 * ATT_W + c0 + 256]).astype(BF16)
        _carry_wait(ex, i, n - 1, ex_ins, ex_outs, ex_sems, pass_at=max(n - 3, 0))

    outs = pl.pallas_call(
        body, name="in_proj_fwd", grid=(n,),
        in_specs=[pl.BlockSpec((ts, D_MODEL), lambda i: (i, 0)), ANY] + ex.specs,
        out_specs=_dil_specs(ts, ATT_W, lambda i: i) + [pl.BlockSpec((ts, REST_W), lambda i: (i, 0))] + ex.specs,
        out_shape=_dil_shapes(s, ATT_W, BF16) + [jax.ShapeDtypeStruct((s, REST_W), BF16)] + ex.out_shape,
        scratch_shapes=[pltpu.VMEM((D_MODEL, N_IN), BF16)] + _order_scratch(ts, GROUP_W)
                       + [pltpu.SemaphoreType.DMA((N_DEV,))] + ex.scratch,
        compiler_params=_params(dimension_semantics=("arbitrary",)),
    )(u, w_in, *ex.arrays)
    return outs[:4], outs[4:]


def _blocks_per_step(nbc):
    return 16 if nbc % 16 == 0 else 8


STACK = HEADS * BLK


SCORE_SCALE = HEAD_DIM ** -0.5


def _band_consts(group, dil):
    row = lax.broadcasted_iota(jnp.int32, (STACK, 2 * BLK), 0)
    kj = lax.broadcasted_iota(jnp.int32, (STACK, 2 * BLK), 1)
    head = row // BLK
    steps = BLK + (row % BLK) - kj
    slope = jnp.full((STACK, 2 * BLK), _slope(group, HEADS - 1), F32)
    for h in range(HEADS - 1):
        slope = jnp.where(head == h, _slope(group, h), slope)
    in_band = (steps >= 0) & (steps <= BLK)
    return jnp.where(in_band, slope * (steps.astype(F32) * float(dil)), jnp.inf), kj


def _first_key(block, nbc, nb, b):
    if nbc % nb == 0 and b != 0:
        return None
    return jnp.where((block % nbc) != 0, 0, BLK)


def _head_of_col():
    return lax.broadcasted_iota(jnp.int32, (BLK, GROUP_W), 1) // HEAD_DIM


def _stack_heads(xb):
    head_of = _head_of_col()
    return jnp.concatenate([jnp.where(head_of == h, xb, jnp.zeros_like(xb)) for h in range(HEADS)], axis=0)


def _per_head_cols(stacked, rhs, scale_rows=None):
    lane = lax.broadcasted_iota(jnp.int32, (BLK, LANES), 1)
    halves = []
    for pair in range(HEADS // 2):
        tile = rhs[:, pair * LANES:(pair + 1) * LANES]
        parts = []
        for h in (2 * pair, 2 * pair + 1):
            part = _dot(stacked[h * BLK:(h + 1) * BLK], tile)
            parts.append(part if scale_rows is None else part * scale_rows[h * BLK:(h + 1) * BLK])
        halves.append(jnp.where(lane < HEAD_DIM, parts[0], parts[1]))
    return jnp.concatenate(halves, axis=1)


def _per_head_rows(col):
    lane = lax.broadcasted_iota(jnp.int32, (BLK, LANES), 1)
    return jnp.concatenate(
        [jnp.where(lane < HEAD_DIM, col[(2 * pair) * BLK:(2 * pair + 1) * BLK], col[(2 * pair + 1) * BLK:(2 * pair + 2) * BLK])
         for pair in range(HEADS // 2)], axis=1)


def _band_softmax(qs, kb, penalty, kj, first_key):
    sc = _dot_nt(qs, kb) - penalty
    if first_key is not None:
        sc = jnp.where(kj >= first_key, sc, -jnp.inf)
    mx = jnp.max(sc, axis=1, keepdims=True)
    e = jnp.exp(sc - mx)
    return e, mx, jnp.sum(e, axis=1, keepdims=True)


def _attn_fwd(qkv, group):
    s = qkv.shape[0]
    dil = ATT_GROUPS[group][1]
    nbc = s // (BLK * dil)
    nb = _blocks_per_step(nbc)
    tile = nb * BLK

    def body(q_ref, kc_ref, kp_ref, vc_ref, vp_ref, ol_ref, kbuf, vbuf):
        i = pl.program_id(0)
        kbuf[0:BLK] = kp_ref[...]
        kbuf[BLK:BLK + tile] = kc_ref[...]
        vbuf[0:BLK] = vp_ref[...]
        vbuf[BLK:BLK + tile] = vc_ref[...]
        penalty, kj = _band_consts(group, dil)
        blocks = range(nb)
        qss = [_stack_heads(q_ref[b * BLK:(b + 1) * BLK, :] * SCORE_SCALE) for b in blocks]
        soft = [_band_softmax(qss[b], kbuf[b * BLK:b * BLK + 2 * BLK, :], penalty, kj,
                              _first_key(i * nb + b, nbc, nb, b)) for b in blocks]
        for b in blocks:
            e, mx, den = soft[b]
            ol_ref[b * BLK:(b + 1) * BLK, 0:GROUP_W] = _per_head_cols(
                e.astype(BF16), vbuf[b * BLK:b * BLK + 2 * BLK, :], 1.0 / den)
        for b in blocks:
            e, mx, den = soft[b]
            ol_ref[b * BLK:(b + 1) * BLK, GROUP_W:] = _per_head_rows(mx + jnp.log(den))

    n = s // tile
    cur = lambda c: pl.BlockSpec((tile, GROUP_W), lambda i: (i, c))
    prev = lambda c: pl.BlockSpec((BLK, GROUP_W), lambda i: (jnp.maximum(i * nb - 1, 0), c))
    return pl.pallas_call(
        body, name=f"attn_fwd_g{group}", grid=(n,),
        in_specs=[cur(0), cur(1), prev(1), cur(2), prev(2)],
        out_specs=pl.BlockSpec((tile, 2 * GROUP_W), lambda i: (i, 0)),
        out_shape=jax.ShapeDtypeStruct((s, 2 * GROUP_W), F32),
        scratch_shapes=[pltpu.VMEM((BLK + tile, GROUP_W), BF16), pltpu.VMEM((BLK + tile, GROUP_W), BF16)],
        compiler_params=_params(dimension_semantics=("arbitrary",)),
    )(qkv, qkv, qkv, qkv, qkv)


def _attn_bwd(qkv, dc, group, ex=None):
    s = qkv.shape[0]
    dil = ATT_GROUPS[group][1]
    nbc = s // (BLK * dil)
    nb = _blocks_per_step(nbc)
    tile = nb * BLK
    n = s // tile
    nex = ex.nw if ex else 0

    def body(*refs):
        ((q_ref, kc_ref, kp_ref, vc_ref, vp_ref, do_ref, c_ref), ex_ins, (out_ref,), ex_outs,
         (kbuf, vbuf, dqpend, dkpend, dvpend), ex_sems) = _split_refs(refs, (7, nex, 1, nex, 5, 3 if ex else 0))
        i = pl.program_id(0)
        if ex:
            _carry_start(ex, i, ex_ins, ex_outs, ex_sems)

        @pl.when(i == 0)
        def _():
            dqpend[...] = jnp.zeros_like(dqpend)
            dkpend[...] = jnp.zeros_like(dkpend)
            dvpend[...] = jnp.zeros_like(dvpend)

        out_ref[:, 0:GROUP_W] = dqpend[...]
        body_rows = slice(0, tile - BLK)
        tail = slice(tile - BLK, tile)
        pends = ((dkpend, GROUP_W), (dvpend, 2 * GROUP_W))
        for pend, c0 in pends:
            out_ref[body_rows, c0:c0 + GROUP_W] = pend[body_rows, :].astype(BF16)

        @pl.when(i < n)
        def _():
            kbuf[0:BLK] = kp_ref[...]
            kbuf[BLK:BLK + tile] = kc_ref[...]
            vbuf[0:BLK] = vp_ref[...]
            vbuf[BLK:BLK + tile] = vc_ref[...]
            penalty, kj = _band_consts(group, dil)
            head_of = _head_of_col()
            blocks = range(nb)
            rows = [slice(b * BLK, (b + 1) * BLK) for b in blocks]
            kbs = [kbuf[b * BLK:b * BLK + 2 * BLK, :] for b in blocks]
            vbs = [vbuf[b * BLK:b * BLK + 2 * BLK, :] for b in blocks]
            qss = [_stack_heads(q_ref[rows[b], :] * SCORE_SCALE) for b in blocks]
            doss = [_stack_heads(do_ref[rows[b], :].astype(BF16)) for b in blocks]
            cors = []
            for b in blocks:
                cb = c_ref[rows[b], :]
                cors.append(jnp.concatenate(
                    [jnp.max(jnp.where(head_of == h, cb, -jnp.inf), axis=1, keepdims=True) for h in range(HEADS)],
                    axis=0))
            soft = [_band_softmax(qss[b], kbs[b], penalty, kj, _first_key(i * nb + b, nbc, nb, b)) for b in blocks]
            dps = [_dot_nt(doss[b], vbs[b]) for b in blocks]
            ps = [soft[b][0] * (1.0 / soft[b][2]) for b in blocks]
            dss = [(ps[b] * (dps[b] + cors[b])).astype(BF16) for b in blocks]
            for b in blocks:
                dqpend[rows[b], :] = _per_head_cols(dss[b], kbs[b] * SCORE_SCALE).astype(BF16)
            bands = [(_dot_tn(dss[b], qss[b]), _dot_tn(ps[b].astype(BF16), doss[b])) for b in blocks]
            for which, (pend, c0) in enumerate(pends):
                out_ref[tail, c0:c0 + GROUP_W] = (pend[tail, :] + bands[0][which][0:BLK]).astype(BF16)
                for b in range(nb):
                    own = bands[b][which][BLK:2 * BLK]
                    pend[b * BLK:(b + 1) * BLK, :] = own + bands[b + 1][which][0:BLK] if b + 1 < nb else own

        @pl.when(i == n)
        def _():
            for pend, c0 in pends:
                out_ref[tail, c0:c0 + GROUP_W] = pend[tail, :].astype(BF16)

        if ex:
            _carry_wait(ex, i, n, ex_ins, ex_outs, ex_sems)

    last = n - 1
    cur = lambda c: pl.BlockSpec((tile, GROUP_W), lambda i: (jnp.minimum(i, last), c))
    prev = lambda c: pl.BlockSpec(
        (BLK, GROUP_W), lambda i: (jnp.maximum(jnp.minimum(i, last) * nb - 1, 0), c))
    outs = pl.pallas_call(
        body, name=f"attn_bwd_g{group}", grid=(n + 1,),
        in_specs=[cur(0), cur(1), prev(1), cur(2), prev(2), cur(0), cur(1)] + (ex.specs if ex else []),
        out_specs=[pl.BlockSpec((tile, ATT_W), lambda i: (jnp.maximum(i - 1, 0), 0))] + (ex.specs if ex else []),
        out_shape=[jax.ShapeDtypeStruct((s, ATT_W), BF16)] + (ex.out_shape if ex else []),
        scratch_shapes=[pltpu.VMEM((BLK + tile, GROUP_W), BF16), pltpu.VMEM((BLK + tile, GROUP_W), BF16),
                        pltpu.VMEM((tile, GROUP_W), BF16),
                        pltpu.VMEM((tile, GROUP_W), F32), pltpu.VMEM((tile, GROUP_W), F32)]
                       + (ex.scratch if ex else []),
        compiler_params=_params(dimension_semantics=("arbitrary",)),
    )(qkv, qkv, qkv, qkv, qkv, dc, dc, *(ex.arrays if ex else []))
    return (outs[0], outs[1:]) if ex else outs[0]


def _gather_rows(refs, buf, mid, ts):
    return [_to_token_order(refs[g], ATT_GROUPS[g][1], buf, mid, ts) for g in range(N_GROUPS)]


def _inverse_counts(ts):
    win = np.repeat(np.asarray(POOL_WINDOWS), POOL_GW)
    first = np.minimum(np.arange(ts)[:, None] + 1, win[None, :])
    counts = np.stack([first, np.broadcast_to(win[None, :], (ts, POOL_W))]).astype(np.float32)
    return jnp.asarray(np.float32(1.0) / counts)


def _pool_fwd(ebuf, s2, s4, s8, inv_count, ts):
    n = ts + HALO
    s2[8:n] = ebuf[8:n] + ebuf[7:n - 1]
    s4[16:n] = s2[16:n] + s2[14:n - 2]
    s8[24:n] = s4[24:n] + s4[20:n - 4]
    s16 = s8[32:n] + s8[24:n - 8]
    col = lax.broadcasted_iota(jnp.int32, (ts, POOL_W), 1)
    psum = jnp.where(col < POOL_GW, s2[32:n],
                     jnp.where(col < 2 * POOL_GW, s4[32:n], jnp.where(col < 3 * POOL_GW, s8[32:n], s16)))
    return psum * inv_count - ebuf[32:n]


GROUP_WIN = ((0, 0), (128, 64), (384, 0), (512, 64))
WIN = 256


def _window_weights(w_grp):
    return jnp.stack([jnp.pad(w_grp[g], ((off, WIN - off - POOL_GW), (off, WIN - off - POOL_GW)))
                      for g, (_, off) in enumerate(GROUP_WIN)])


def _group_matmul(xb, wwin_ref, transposed=False):
    outs = []
    for g, (start, _) in enumerate(GROUP_WIN):
        xw = xb[:, start:start + WIN]
        outs.append(_dot_nt(xw, wwin_ref[g]) if transposed else _dot(xw, wwin_ref[g]))
    half = WIN // 2
    return jnp.concatenate([outs[0][:, :half], outs[0][:, half:] + outs[1][:, :half], outs[1][:, half:],
                            outs[2][:, :half], outs[2][:, half:] + outs[3][:, :half], outs[3][:, half:]], axis=1)


def _mix_core(zr, pooled, outs, lses, wwin_ref, scale, wao, wpo):
    mixed = _group_matmul(pooled.astype(BF16), wwin_ref)
    p = mixed * scale
    l0, l1, l2 = lses
    mx = jnp.maximum(jnp.maximum(l0, l1), l2)
    e0, e1, e2 = jnp.exp(l0 - mx), jnp.exp(l1 - mx), jnp.exp(l2 - mx)
    inv = 1.0 / (e0 + e1 + e2)
    wts = (e0 * inv, e1 * inv, e2 * inv)
    a = wts[0] * outs[0] + wts[1] * outs[1] + wts[2] * outs[2]
    att = _dot(a.astype(BF16), wao)
    pol = _dot(p.astype(BF16), wpo)
    sga = _sigmoid(zr[:, POOL_W:POOL_W + D_MODEL].astype(F32))
    sgp = _sigmoid(zr[:, POOL_W + D_MODEL:].astype(F32))
    mg = sga * att + sgp * pol
    return dict(mixed=mixed, p=p, wts=wts, a=a, att=att, pol=pol, sga=sga, sgp=sgp, mg=mg)


def _fill_pool_input(ebuf, zr_ref, halo_ref, t0):
    ts = zr_ref.shape[0]
    halo = halo_ref[...].astype(F32)
    t = t0 - HALO + lax.broadcasted_iota(jnp.int32, (HALO, POOL_W), 0)
    ebuf[0:HALO] = jnp.where(t >= 0, halo, 0.0)
    ebuf[HALO:HALO + ts] = zr_ref[:, 0:POOL_W].astype(F32)


def _mix_fwd(x, zr, ol_dil, wbd, scale, wao, wpo, wout, ts):
    s = x.shape[0]
    n = s // ts

    def body(x_ref, zr_ref, halo_ref, ic_ref, ol0, ol1, ol2, wbd_ref, sc_ref, wao_ref, wpo_ref, wout_ref,
             h1_ref, ebuf, s2, s4, s8, rbuf, rmid):
        i = pl.program_id(0)
        _fill_pool_input(ebuf, zr_ref, halo_ref, i * ts)
        pooled = _pool_fwd(ebuf, s2, s4, s8, ic_ref[...], ts)
        ols = _gather_rows((ol0, ol1, ol2), rbuf, rmid, ts)
        outs, lses = [a[:, :GROUP_W] for a in ols], [a[:, GROUP_W:] for a in ols]
        f = _mix_core(zr_ref[...], pooled, outs, lses, wbd_ref, sc_ref[...], wao_ref[...], wpo_ref[...])
        h1_ref[...] = x_ref[...] + _dot(f["mg"].astype(BF16), wout_ref[...])

    whole = lambda a: pl.BlockSpec(a.shape, lambda i: (0,) * a.ndim)
    idx = lambda i: i
    return pl.pallas_call(
        body, name="mix_fwd", grid=(n,),
        in_specs=[pl.BlockSpec((ts, D_MODEL), lambda i: (i, 0)),
                  pl.BlockSpec((ts, REST_W), lambda i: (i, 0)),
                  pl.BlockSpec((HALO, POOL_W), lambda i: (jnp.maximum(i * (ts // HALO) - 1, 0), 0)),
                  pl.BlockSpec((None, ts, POOL_W), lambda i: (jnp.minimum(i, 1), 0, 0))]
                 + _dil_specs(ts, 2 * GROUP_W, idx)
                 + [whole(wbd), whole(scale), whole(wao), whole(wpo), whole(wout)],
        out_specs=pl.BlockSpec((ts, D_MODEL), lambda i: (i, 0)),
        out_shape=jax.ShapeDtypeStruct((s, D_MODEL), F32),
        scratch_shapes=[pltpu.VMEM((ts + HALO, POOL_W), F32)] * 4 + _order_scratch(ts, 2 * GROUP_W),
        compiler_params=_params(dimension_semantics=("arbitrary",)),
    )(x, zr, zr, _inverse_counts(ts), *ol_dil, wbd, scale, wao, wpo, wout)


def _mix_bwd(dh1, zr, ol_dil, wbd, scale, wao, wpo, wout, ts):
    s = dh1.shape[0]
    n = s // ts

    def body(dh_ref, zr_ref, halo_ref, ic_ref, ol0, ol1, ol2, sc_ref, wbd_hbm, wao_hbm, wpo_hbm, wout_hbm,
             dzr_ref, dc0, dc1, dc2, gsc_ref, gwout_hbm, gwao_hbm, gwpo_hbm, gwbd_hbm,
             ebuf, s2, s4, s8, gbuf, t2, t4, t8, rbuf, rmid,
             wbd_ref, wao_ref, wpo_ref, wout_ref, gwout_ref, gwao_ref, gwpo_ref, gwbd_ref, sem):
        j = pl.program_id(0)
        i = n - 1 - j
        _load_resident(j, [(wbd_hbm, wbd_ref), (wao_hbm, wao_ref), (wpo_hbm, wpo_ref), (wout_hbm, wout_ref)], sem)

        @pl.when(j == 0)
        def _():
            gwout_ref[...] = jnp.zeros_like(gwout_ref)
            gwao_ref[...] = jnp.zeros_like(gwao_ref)
            gwpo_ref[...] = jnp.zeros_like(gwpo_ref)
            gwbd_ref[...] = jnp.zeros_like(gwbd_ref)
            gsc_ref[...] = jnp.zeros_like(gsc_ref)
            gbuf[ts:ts + HALO] = jnp.zeros((HALO, POOL_W), F32)

        _fill_pool_input(ebuf, zr_ref, halo_ref, i * ts)
        inv_count = ic_ref[...]
        pooled = _pool_fwd(ebuf, s2, s4, s8, inv_count, ts)
        ols = _gather_rows((ol0, ol1, ol2), rbuf, rmid, ts)
        outs, lses = [a[:, :GROUP_W] for a in ols], [a[:, GROUP_W:] for a in ols]
        zr = zr_ref[...]
        wao, wpo, wout = wao_ref[...], wpo_ref[...], wout_ref[...]
        scale = sc_ref[...]
        f = _mix_core(zr, pooled, outs, lses, wbd_ref, scale, wao, wpo)

        dhb = dh_ref[...].astype(BF16)
        gwout_ref[...] += _dot(f["mg"].T.astype(BF16), dhb)
        dmg = _dot_nt(dhb, wout)
        sga, sgp, att, pol = f["sga"], f["sgp"], f["att"], f["pol"]
        datt = dmg * sga
        dpol = dmg * sgp
        dzr_ref[:, POOL_W:POOL_W + D_MODEL] = (dmg * att * sga * (1.0 - sga)).astype(BF16)
        dzr_ref[:, POOL_W + D_MODEL:] = (dmg * pol * sgp * (1.0 - sgp)).astype(BF16)
        dattb = datt.astype(BF16)
        dpolb = dpol.astype(BF16)
        gwao_ref[...] += _dot(f["a"].T.astype(BF16), dattb)
        gwpo_ref[...] += _dot(f["p"].T.astype(BF16), dpolb)
        da = _dot_nt(dattb, wao)
        dp = _dot_nt(dpolb, wpo)

        gsc_ref[...] += jnp.sum(f["mixed"] * dp, axis=0, keepdims=True)
        dmixed = (dp * scale).astype(BF16)
        pooled_t = pooled.T.astype(BF16)
        for g, (start, _) in enumerate(GROUP_WIN):
            gwbd_ref[g] += _dot(pooled_t[start:start + WIN, :], dmixed[:, start:start + WIN])
        dpooled = _group_matmul(dmixed, wbd_ref, transposed=True)
        gbuf[0:ts] = dpooled * inv_count
        m = ts + HALO
        t2[0:m - 8] = gbuf[0:m - 8] + gbuf[1:m - 7]
        t4[0:m - 16] = t2[0:m - 16] + t2[2:m - 14]
        t8[0:m - 24] = t4[0:m - 24] + t4[4:m - 20]
        t16 = t8[0:ts] + t8[8:ts + 8]
        col = lax.broadcasted_iota(jnp.int32, (ts, POOL_W), 1)
        back = jnp.where(col < POOL_GW, t2[0:ts],
                         jnp.where(col < 2 * POOL_GW, t4[0:ts], jnp.where(col < 3 * POOL_GW, t8[0:ts], t16)))
        dzr_ref[:, 0:POOL_W] = (back - dpooled).astype(BF16)
        gbuf[ts:ts + HALO] = gbuf[0:HALO]

        head_of = lax.broadcasted_iota(jnp.int32, (ts, GROUP_W), 1) // HEAD_DIM
        prod = da * f["a"]
        inner = jnp.zeros((ts, GROUP_W), F32)
        for h in range(4):
            hm = head_of == h
            tot = jnp.sum(jnp.where(hm, prod, 0.0), axis=1, keepdims=True)
            inner = jnp.where(hm, tot, inner)
        for g, dc_ref in enumerate((dc0, dc1, dc2)):
            both = jnp.concatenate([f["wts"][g] * da, -f["wts"][g] * inner], axis=1)
            _to_class_order(both, ATT_GROUPS[g][1], rbuf, rmid, dc_ref, 0)

        @pl.when(j == n - 1)
        def _():
            wout_ref[...] = gwout_ref[...].astype(BF16)
            wao_ref[...] = gwao_ref[...].astype(BF16)
            wpo_ref[...] = gwpo_ref[...].astype(BF16)
            tn = D_MODEL // N_DEV
            pairs = [(wout_ref, gwout_hbm), (gwbd_ref, gwbd_hbm)]
            for staged, dst in ((wao_ref, gwao_hbm), (wpo_ref, gwpo_hbm)):
                pairs += [(staged.at[:, pl.ds(k * tn, tn)], dst.at[k]) for k in range(N_DEV)]
            copies = [pltpu.make_async_copy(src, dst, sem.at[k]) for k, (src, dst) in enumerate(pairs)]
            for cp in copies:
                cp.start()
            for cp in copies:
                cp.wait()

    idx = lambda j: n - 1 - j
    dc_shapes = _dil_shapes(s, 2 * GROUP_W, F32)
    weights = (wbd, wao, wpo, wout)
    grad_shapes = [(D_MODEL, D_MODEL), (GROUP_W, D_MODEL), (POOL_W, D_MODEL), (len(GROUP_WIN), WIN, WIN)]
    tile_buf = pltpu.VMEM((ts + HALO, POOL_W), F32)
    outs = pl.pallas_call(
        body, name="mix_bwd", grid=(n,),
        in_specs=[pl.BlockSpec((ts, D_MODEL), lambda j: (idx(j), 0)),
                  pl.BlockSpec((ts, REST_W), lambda j: (idx(j), 0)),
                  pl.BlockSpec((HALO, POOL_W), lambda j: (jnp.maximum(idx(j) * (ts // HALO) - 1, 0), 0)),
                  pl.BlockSpec((None, ts, POOL_W), lambda j: (jnp.minimum(idx(j), 1), 0, 0))]
                 + _dil_specs(ts, 2 * GROUP_W, idx)
                 + [pl.BlockSpec((1, POOL_W), lambda j: (0, 0))] + [ANY] * 4,
        out_specs=[pl.BlockSpec((ts, REST_W), lambda j: (idx(j), 0))]
                  + _dil_specs(ts, 2 * GROUP_W, idx)
                  + [pl.BlockSpec((1, POOL_W), lambda j: (0, 0))] + [ANY] * 4,
        out_shape=[jax.ShapeDtypeStruct((s, REST_W), BF16)] + dc_shapes
                  + [jax.ShapeDtypeStruct((1, POOL_W), F32),
                     jax.ShapeDtypeStruct((D_MODEL, D_MODEL), BF16),
                     jax.ShapeDtypeStruct((N_DEV, GROUP_W, D_MODEL // N_DEV), BF16),
                     jax.ShapeDtypeStruct((N_DEV, POOL_W, D_MODEL // N_DEV), BF16),
                     jax.ShapeDtypeStruct(grad_shapes[3], F32)],
        scratch_shapes=[tile_buf] * 8 + _order_scratch(ts, 2 * GROUP_W)
                       + [pltpu.VMEM(w.shape, BF16) for w in weights]
                       + [pltpu.VMEM(shape, F32) for shape in grad_shapes]
                       + [pltpu.SemaphoreType.DMA((2 + 2 * N_DEV,))],
        compiler_params=_params(dimension_semantics=("arbitrary",)),
    )(dh1, zr, zr, _inverse_counts(ts), *ol_dil, scale, wbd, wao, wpo, wout)
    dzr, dc_dil, g_scale = outs[0], outs[1:4], outs[4]
    g_out, g_ao, g_po, g_bd = outs[5:]
    return dzr, dc_dil, (g_out.reshape(N_DEV, D_MODEL // N_DEV, D_MODEL), g_ao, g_po, g_bd, g_scale)


FF_CHUNK = 1024


STAT_ROWS = 8


def _mlp_fwd_bwd(h1, tgt, gains, wmi, wmo, ts):
    s = h1.shape[0]
    n = s // ts
    nchunk = D_FF // FF_CHUNK

    def body(h1_ref, t_ref, g_ref, wmi_hbm, wmo_hbm, dh1_ref, mdt_ref, hd_ref, stat_ref, wmi, wmo, relu_buf, sem):
        i = pl.program_id(0)
        _load_resident(i, _column_blocks(wmi_hbm, wmi) + [(wmo_hbm, wmo)], sem)

        @pl.when(i == 0)
        def _():
            stat_ref[...] = jnp.zeros_like(stat_ref)

        h1 = h1_ref[...]
        g2 = g_ref[0:1, :]
        g3 = g_ref[1:2, :]
        r2 = _rms(h1)
        n2 = h1 * r2
        m = n2 * g2
        mb = m.astype(BF16)
        mdt_ref[0] = m.T.astype(BF16)
        h2 = h1
        for c in range(nchunk):
            cols = slice(c * FF_CHUNK, (c + 1) * FF_CHUNK)
            rl = jnp.maximum(_dot(mb, wmi[:, cols]), 0.0)
            relu_buf[:, cols] = rl
            hb = (rl * rl).astype(BF16)
            hd_ref[0, :, cols] = hb
            h2 = h2 + _dot(hb, wmo[cols, :])
        r3 = _rms(h2)
        n3 = h2 * r3
        diff = n3 * g3 - t_ref[...]
        loss = jnp.sum(0.5 * jnp.sum(diff * diff, axis=1, keepdims=True) / D_MODEL, axis=0, keepdims=True)
        dy = diff * (1.0 / D_MODEL)
        dg3 = jnp.sum(dy * n3, axis=0, keepdims=True)
        dh2 = _rms_bwd(dy * g3, n3, r3)
        dh2b = dh2.astype(BF16)
        mdt_ref[1] = dh2.T.astype(BF16)
        dm = jnp.zeros((ts, D_MODEL), F32)
        for c in range(nchunk):
            cols = slice(c * FF_CHUNK, (c + 1) * FF_CHUNK)
            dfb = (_dot_nt(dh2b, wmo[cols, :]) * (2.0 * relu_buf[:, cols])).astype(BF16)
            hd_ref[1, :, cols] = dfb
            dm = dm + _dot_nt(dfb, wmi[:, cols])
        dg2 = jnp.sum(dm * n2, axis=0, keepdims=True)
        dh1_ref[...] = dh2 + _rms_bwd(dm * g2, n2, r2)
        row = lax.broadcasted_iota(jnp.int32, (STAT_ROWS, D_MODEL), 0)
        stat_ref[...] += jnp.where(row == 0, loss, jnp.where(row == 1, dg2, jnp.where(row == 2, dg3, 0.0)))

    row = lambda w: pl.BlockSpec((ts, w), lambda i: (i, 0))
    return pl.pallas_call(
        body, name="mlp_fwd_bwd", grid=(n,),
        in_specs=[row(D_MODEL), row(D_MODEL), pl.BlockSpec((2, D_MODEL), lambda i: (0, 0)), ANY, ANY],
        out_specs=[row(D_MODEL), pl.BlockSpec((2, D_MODEL, ts), lambda i: (0, 0, i)),
                   pl.BlockSpec((2, ts, D_FF), lambda i: (0, i, 0)),
                   pl.BlockSpec((STAT_ROWS, D_MODEL), lambda i: (0, 0))],
        out_shape=[jax.ShapeDtypeStruct((s, D_MODEL), F32), jax.ShapeDtypeStruct((2, D_MODEL, s), BF16),
                   jax.ShapeDtypeStruct((2, s, D_FF), BF16), jax.ShapeDtypeStruct((STAT_ROWS, D_MODEL), F32)],
        scratch_shapes=[pltpu.VMEM((D_MODEL, D_FF), BF16), pltpu.VMEM((D_FF, D_MODEL), BF16),
                        pltpu.VMEM((ts, D_FF), F32), pltpu.SemaphoreType.DMA((N_DEV + 1,))],
        compiler_params=_params(dimension_semantics=("arbitrary",)),
    )(h1, tgt, gains, wmi, wmo)


def _in_proj_bwd(x, dh1, dzr, dqkv_dil, g1, w_in, ts):
    s = x.shape[0]
    n = s // ts

    assert n > AHEAD

    def body(x_hbm, dh_hbm, dzr_hbm, q0, q1, q2, g_ref, w_hbm, dx_ref, dz_ref, dg_ref,
             w_ref, qbuf, qmid, sem, x_ring, dh_ring, dzr_ring, ring_sem):
        i = pl.program_id(0)

        def fetch(step):
            slot = step % (AHEAD + 1)
            rows = pl.ds(step * ts, ts)
            return [pltpu.make_async_copy(src.at[rows], ring.at[slot], ring_sem.at[k, slot])
                    for k, (src, ring) in enumerate(((x_hbm, x_ring), (dh_hbm, dh_ring), (dzr_hbm, dzr_ring)))]

        @pl.when(i == 0)
        def _():
            for step in range(AHEAD):
                for cp in fetch(step):
                    cp.start()

        @pl.when(i + AHEAD < n)
        def _():
            for cp in fetch(i + AHEAD):
                cp.start()

        _load_resident(i, _column_blocks(w_hbm, w_ref), sem)

        @pl.when(i == 0)
        def _():
            dg_ref[...] = jnp.zeros_like(dg_ref)

        slot = i % (AHEAD + 1)
        for cp in fetch(i):
            cp.wait()
        for g, dqkv in enumerate(_gather_rows((q0, q1, q2), qbuf, qmid, ts)):
            for sec in range(3):
                c0 = sec * ATT_W + g * GROUP_W
                dz_ref[:, c0:c0 + GROUP_W] = dqkv[:, sec * GROUP_W:(sec + 1) * GROUP_W].astype(BF16)
        dz_ref[:, 3 * ATT_W:] = dzr_ring[slot]
        du = _dot_nt(dz_ref[...], w_ref[...])
        x = x_ring[slot]
        r1 = _rms(x)
        n1 = x * r1
        g1 = g_ref[...]
        dg_ref[...] += jnp.sum(du * n1, axis=0, keepdims=True)
        dx_ref[...] = dh_ring[slot] + _rms_bwd(du * g1, n1, r1)

    row = lambda w: pl.BlockSpec((ts, w), lambda i: (i, 0))
    vec = pl.BlockSpec((1, D_MODEL), lambda i: (0, 0))
    ring = lambda w, dtype: pltpu.VMEM((AHEAD + 1, ts, w), dtype)
    return pl.pallas_call(
        body, name="in_proj_bwd", grid=(n,),
        in_specs=[ANY, ANY, ANY] + _dil_specs(ts, ATT_W, lambda i: i) + [vec, ANY],
        out_specs=[row(D_MODEL), row(N_IN), vec],
        out_shape=[jax.ShapeDtypeStruct((s, D_MODEL), F32), jax.ShapeDtypeStruct((s, N_IN), BF16),
                   jax.ShapeDtypeStruct((1, D_MODEL), F32)],
        scratch_shapes=[pltpu.VMEM((D_MODEL, N_IN), BF16)] + _order_scratch(ts, ATT_W)
                       + [pltpu.SemaphoreType.DMA((N_DEV,)), ring(D_MODEL, F32), ring(D_MODEL, F32),
                          ring(REST_W, BF16), pltpu.SemaphoreType.DMA((3, AHEAD + 1))],
        compiler_params=_params(vmem_limit_bytes=AHEAD_VMEM_LIMIT, dimension_semantics=("arbitrary",)),
    )(x, dh1, dzr, *dqkv_dil, g1, w_in)


GRAD_PASS = 2


def _weight_grad(at, b, name, transpose_out, tk=2048, at_slot=None, b_slot=None, ex=None):
    m, s = at.shape[-2:]
    nn = b.shape[-1]
    tn = nn // N_DEV
    tk = min(tk, s)
    nk = s // tk
    npass = N_DEV // GRAD_PASS
    oshape = (tn, m) if transpose_out else (m, tn)
    owner = lambda jj: N_DEV - 1 - jj
    order = jnp.stack([_linear(_peer(_my_place(), owner(jj))) for jj in range(N_DEV)]).astype(jnp.int32)
    sent_in = lambda p: [jj for jj in range(N_DEV - 1) if jj // GRAD_PASS == p]

    nex = ex.nw if ex else 0

    def body(order_ref, at_ref, *refs):
        (b_ref,), ex_ins, (got_ref,), ex_outs, (acc, res, send_sems, recv_sems, local_sem), ex_sems = _split_refs(
            refs, (1, nex, 1, nex, 5, 3 if ex else 0))
        j, k = pl.program_id(0), pl.program_id(1)
        me = _my_place()
        mine = _linear(me)
        if ex:
            _carry_start(ex, j * nk + k, ex_ins, ex_outs, ex_sems)

        def send(jj):
            peer = _peer(me, owner(jj))
            return pltpu.make_async_remote_copy(
                src_ref=res.at[(jj // GRAD_PASS) % 2, peer[2]], dst_ref=got_ref.at[mine],
                send_sem=send_sems.at[jj], recv_sem=recv_sems.at[jj],
                device_id=peer, device_id_type=pl.DeviceIdType.MESH)

        @pl.when(k == 0)
        def _():
            acc[...] = jnp.zeros_like(acc)

        acc[...] += _dot(at_ref[...], b_ref[...])

        @pl.when(k == nk - 1)
        def _():
            for p in range(2, npass):
                @pl.when(j == p)
                def _():
                    for jj in sent_in(p - 2):
                        send(jj).wait_send()

            for core in range(GRAD_PASS):
                r = acc[:, core * tn:(core + 1) * tn]
                res[j % 2, core] = (r.T if transpose_out else r).astype(BF16)
            for p in range(npass):
                @pl.when(j == p)
                def _():
                    for jj in sent_in(p):
                        send(jj).start()

            @pl.when(j == npass - 1)
            def _():
                own = pltpu.make_async_copy(res.at[(npass - 1) % 2, me[2]], got_ref.at[mine], local_sem.at[0])
                own.start()
                for p in range(max(npass - 2, 0), npass):
                    for jj in sent_in(p):
                        send(jj).wait_send()
                for jj in range(N_DEV - 1):
                    send(jj).wait_recv()
                own.wait()

        if ex:
            _carry_wait(ex, j * nk + k, npass * nk - 1, ex_ins, ex_outs, ex_sems)

    if at_slot is None:
        at_spec = pl.BlockSpec((m, tk), lambda j, k, o: (0, k))
    else:
        at_spec = pl.BlockSpec((None, m, tk), lambda j, k, o: (at_slot, 0, k))
    chip_cols = lambda j, o: o[GRAD_PASS * j] // GRAD_PASS
    if b_slot is None:
        b_spec = pl.BlockSpec((tk, GRAD_PASS * tn), lambda j, k, o: (k, chip_cols(j, o)))
    else:
        b_spec = pl.BlockSpec((None, tk, GRAD_PASS * tn), lambda j, k, o: (b_slot, k, chip_cols(j, o)))
    outs = pl.pallas_call(
        body, name=name,
        grid_spec=pltpu.PrefetchScalarGridSpec(
            num_scalar_prefetch=1, grid=(npass, nk),
            in_specs=[at_spec, b_spec] + (ex.specs if ex else []),
            out_specs=[ANY] + (ex.specs if ex else []),
            scratch_shapes=[pltpu.VMEM((m, GRAD_PASS * tn), F32), pltpu.VMEM((2, GRAD_PASS) + oshape, BF16),
                            pltpu.SemaphoreType.DMA((N_DEV - 1,)), pltpu.SemaphoreType.DMA((N_DEV - 1,)),
                            pltpu.SemaphoreType.DMA((1,))] + (ex.scratch if ex else [])),
        out_shape=[jax.ShapeDtypeStruct((N_DEV,) + oshape, BF16)] + (ex.out_shape if ex else []),
        compiler_params=_params(dimension_semantics=("arbitrary", "arbitrary")),
    )(order, at, b, *(ex.arrays if ex else []))
    return (outs[0], outs[1:]) if ex else outs[0]


def _my_place():
    x, y, c = lax.axis_index("x"), lax.axis_index("y"), lax.axis_index("c")
    return x, y, c


def _peer(place, k):
    x, y, c = place
    return (1 - x if k & 4 else x, 1 - y if k & 2 else y, 1 - c if k & 1 else c)


def _linear(place):
    x, y, c = place
    return 4 * x + 2 * y + c


class _Exchange:
    passes_on = False

    def __init__(self, arrays, gather):
        self.arrays, self.gather, self.nw = list(arrays), list(gather), len(arrays)
        self.out_shape = []
        for a, g in zip(arrays, gather):
            block = a.shape if g else a.shape[1:]
            self.out_shape.append(jax.ShapeDtypeStruct((N_DEV,) + tuple(block), a.dtype))
        self.specs = [ANY] * self.nw
        self.scratch = [pltpu.SemaphoreType.DMA((self.nw, N_DEV - 1)), pltpu.SemaphoreType.DMA((self.nw, N_DEV - 1)),
                        pltpu.SemaphoreType.DMA((self.nw,))]

    def _copies(self, ins, outs, sems):
        send_sems, recv_sems, local_sems = sems
        me = _my_place()
        mine = _linear(me)
        copies = []
        for w in range(self.nw):
            src = ins[w] if self.gather[w] else ins[w].at[mine]
            copies.append(pltpu.make_async_copy(src, outs[w].at[mine], local_sems.at[w]))
        for k in range(1, N_DEV):
            peer = _peer(me, k)
            for w in range(self.nw):
                src = ins[w] if self.gather[w] else ins[w].at[_linear(peer)]
                copies.append(pltpu.make_async_remote_copy(
                    src_ref=src, dst_ref=outs[w].at[mine],
                    send_sem=send_sems.at[w, k - 1], recv_sem=recv_sems.at[w, k - 1],
                    device_id=peer, device_id_type=pl.DeviceIdType.MESH))
        return copies

    def start(self, ins, outs, sems):
        for cp in self._copies(ins, outs, sems):
            cp.start()

    def wait(self, ins, outs, sems):
        copies = self._copies(ins, outs, sems)
        for cp in copies[self.nw:]:
            cp.wait_recv()
        for cp in copies[self.nw:]:
            cp.wait_send()
        for cp in copies[:self.nw]:
            cp.wait()


class _Gather:
    passes_on = True

    def __init__(self, arrays):
        self.arrays, self.nw = list(arrays), len(arrays)
        self.out_shape = [jax.ShapeDtypeStruct((N_DEV,) + tuple(a.shape), a.dtype) for a in arrays]
        self.specs = [ANY] * self.nw
        self.scratch = [pltpu.SemaphoreType.DMA((self.nw, N_DEV - 1)), pltpu.SemaphoreType.DMA((self.nw, N_DEV - 1)),
                        pltpu.SemaphoreType.DMA((self.nw,))]

    @staticmethod
    def _places():
        x, y, c = _my_place()
        return (x, y, c), (x, y, 1 - c), [(1 - x, y), (x, 1 - y), (1 - x, 1 - y)]

    @staticmethod
    def _copy(outs, sems, w, k, block, to, src=None):
        rows = outs[w].at[_linear(block)]
        return pltpu.make_async_remote_copy(
            src_ref=rows if src is None else src, dst_ref=rows, send_sem=sems[0].at[w, k], recv_sem=sems[1].at[w, k],
            device_id=to, device_id_type=pl.DeviceIdType.MESH)

    def _first(self, ins, outs, sems, w):
        me, sibling, chips = self._places()
        return ([self._copy(outs, sems, w, 0, me, sibling, src=ins[w])]
                + [self._copy(outs, sems, w, 1 + j, me, (*chip, me[2]), src=ins[w]) for j, chip in enumerate(chips)])

    def _passed(self, outs, sems, w):
        me, sibling, chips = self._places()
        return [self._copy(outs, sems, w, 4 + j, (*chip, me[2]), sibling) for j, chip in enumerate(chips)]

    def _local(self, ins, outs, sems, w):
        return pltpu.make_async_copy(ins[w], outs[w].at[_linear(self._places()[0])], sems[2].at[w])

    def start(self, ins, outs, sems):
        for w in range(self.nw):
            self._local(ins, outs, sems, w).start()
            for cp in self._first(ins, outs, sems, w):
                cp.start()

    def pass_on(self, ins, outs, sems):
        me, sibling, chips = self._places()
        for j, chip in enumerate(chips):
            for w in range(self.nw):
                self._copy(outs, sems, w, 1 + j, (*chip, me[2]), me).wait_recv()
                self._passed(outs, sems, w)[j].start()

    def wait(self, ins, outs, sems):
        me, sibling, chips = self._places()
        for w in range(self.nw):
            self._copy(outs, sems, w, 0, sibling, me).wait_recv()
            for j, chip in enumerate(chips):
                self._copy(outs, sems, w, 4 + j, (*chip, sibling[2]), me).wait_recv()
            for cp in self._first(ins, outs, sems, w) + self._passed(outs, sems, w):
                cp.wait_send()
            self._local(ins, outs, sems, w).wait()


def _sum_parts(p_ref):
    g = p_ref[0].astype(F32)
    for j in range(1, N_DEV):
        g = g + p_ref[j].astype(F32)
    return g


def _adam_update(g, w, m, v):
    nm = ADAM_B1 * m + (1.0 - ADAM_B1) * g
    nv = ADAM_B2 * v + (1.0 - ADAM_B2) * (g * g)
    m_hat = nm / (1.0 - ADAM_B1 ** ADAM_STEP)
    v_hat = nv / (1.0 - ADAM_B2 ** ADAM_STEP)
    return -ADAM_LR * (m_hat / (jnp.sqrt(v_hat) + ADAM_EPS) + ADAM_WD * w), nm, nv


def _adamw(parts, w, m, v, name, tr):
    rows, cols = w.shape
    tr = min(tr, rows)
    while rows % tr:
        tr //= 2
    assert tr % 8 == 0, (rows, tr)

    def body(p_ref, w_ref, m_ref, v_ref, g_ref, d_ref, nm_ref, nv_ref):
        g = _sum_parts(p_ref)
        g_ref[...] = g
        d_ref[...], nm_ref[...], nv_ref[...] = _adam_update(g, w_ref[...], m_ref[...], v_ref[...])

    blk = pl.BlockSpec((tr, cols), lambda i: (i, 0))
    return pl.pallas_call(
        body, name=name, grid=(rows // tr,),
        in_specs=[pl.BlockSpec((N_DEV, tr, cols), lambda i: (0, i, 0)), blk, blk, blk],
        out_specs=[blk] * 4,
        out_shape=[jax.ShapeDtypeStruct((rows, cols), F32)] * 4,
        compiler_params=_params(dimension_semantics=("arbitrary",)),
    )(parts, w, m, v)


def _adamw_vectors(got_stats, got_g1, got_scale, params):
    def body(st_ref, g1_ref, sc_ref, *refs):
        ins, outs = refs[:12], refs[12:]
        stats = _sum_parts(st_ref)
        outs[0][...] = stats[0:1, 0:1]
        grads = (_sum_parts(g1_ref), _sum_parts(sc_ref), stats[1:2, :], stats[2:3, :])
        for k, g in enumerate(grads):
            w_ref, m_ref, v_ref = ins[3 * k:3 * k + 3]
            g_ref, d_ref, nm_ref, nv_ref = outs[1 + 4 * k:5 + 4 * k]
            g_ref[...] = g
            d_ref[...], nm_ref[...], nv_ref[...] = _adam_update(g, w_ref[...], m_ref[...], v_ref[...])

    flat = [a for p in params for a in p]
    out_shape = [jax.ShapeDtypeStruct((1, 1), F32)]
    for w, _, _ in params:
        out_shape += [jax.ShapeDtypeStruct(w.shape, F32)] * 4
    res = pl.pallas_call(body, name="adamw_vectors", out_shape=out_shape)(got_stats, got_g1, got_scale, *flat)
    return res[0], [res[1 + 4 * k:5 + 4 * k] for k in range(len(params))]


def kernel(x, norm_mix_g, w_in, w_att_out, w_pool_grp, pool_scale, w_pool_out, w_out, norm_mlp_g, w_mlp_in, w_mlp_out, norm_final_g, loss_target, m_norm_mix_g, m_w_in, m_w_att_out, m_w_pool_grp, m_pool_scale, m_w_pool_out, m_w_out, m_norm_mlp_g, m_w_mlp_in, m_w_mlp_out, m_norm_final_g, v_norm_mix_g, v_w_in, v_w_att_out, v_w_pool_grp, v_pool_scale, v_w_pool_out, v_w_out, v_norm_mlp_g, v_w_mlp_in, v_w_mlp_out, v_norm_final_g):
    x, tgt = x[0], loss_target[0]
    s = x.shape[0]
    g1, g2, g3 = norm_mix_g, norm_mlp_g, norm_final_g.reshape(1, D_MODEL)
    shards = [w_in[0], w_att_out[0], w_pool_out[0], w_out[0], w_mlp_in[0], w_mlp_out[0]]
    wire = [a.astype(BF16) for a in shards]
    cols = lambda a: jnp.transpose(a, (1, 0, 2)).reshape(a.shape[1], N_DEV * a.shape[2])
    rows = lambda a: a.reshape(N_DEV * a.shape[1], a.shape[2])
    wbd =_window_weights(w_pool_grp[0]).astype(BF16)

    (u, ut), (f_in,) = _rms_u(x, g1, 1024, _Gather(wire[:1]))
    (qkv0, qkv1, qkv2, zr), later = _in_proj_fwd(u, f_in, 1024, _Gather(wire[1:]))
    f_ao, f_po, f_out, f_mi, f_mo = cols(later[0]), cols(later[1]), rows(later[2]), later[3], rows(later[4])
    qkv_dil = (qkv0, qkv1, qkv2)
    flat = lambda a: a.reshape(s, a.shape[-1])
    shaped = lambda a, g: a if g == 0 else a.reshape(ATT_GROUPS[g][1], s // ATT_GROUPS[g][1], a.shape[-1])
    ol_dil = [shaped(_attn_fwd(flat(qkv_dil[g]), g), g) for g in range(N_GROUPS)]
    h1 = _mix_fwd(x, zr, ol_dil, wbd, pool_scale, f_ao, f_po, f_out, ts=512)

    dh1, mdt, hd, stats = _mlp_fwd_bwd(h1, tgt, jnp.concatenate([g2, g3], axis=0), f_mi, f_mo, ts=256)
    got = {"w_mlp_in": _weight_grad(mdt, hd, "grad_w_mlp_in", transpose_out=False, at_slot=0, b_slot=1),
           "w_mlp_out": _weight_grad(mdt, hd, "grad_w_mlp_out", transpose_out=True, at_slot=1, b_slot=0)}
    dzr, dc_dil, (g_out, g_ao, g_po, g_bd, g_scale) = _mix_bwd(
        dh1, zr, ol_dil, wbd, pool_scale, f_ao, f_po, f_out, ts=256)
    g_grp = jnp.stack([g_bd[g, off:off + POOL_GW, off:off + POOL_GW] for g, (_, off) in enumerate(GROUP_WIN)])
    early = [_Exchange([g_out], [False]), _Exchange([g_ao, g_po], [False, False]),
             _Exchange([g_grp.reshape(4 * POOL_GW, POOL_GW)], [True])]
    dqkv_dil, arrived = [], []
    for g in range(N_GROUPS):
        dqkv, rode = _attn_bwd(flat(qkv_dil[g]), flat(dc_dil[g]), g, early[g])
        dqkv_dil.append(shaped(dqkv, g))
        arrived += list(rode)
    got["w_out"], got["w_att_out"], got["w_pool_out"], got_grp = arrived
    dx, dz, dg1 = _in_proj_bwd(x, dh1, dzr, dqkv_dil, g1, f_in, ts=512)
    got["w_in"], got_vectors = _weight_grad(ut, dz, "grad_w_in", transpose_out=False,
                                            ex=_Exchange([stats, dg1, g_scale], [True] * 3))

    names = ["w_in", "w_att_out", "w_pool_out", "w_out", "w_mlp_in", "w_mlp_out"]
    ms = [m_w_in, m_w_att_out, m_w_pool_out, m_w_out, m_w_mlp_in, m_w_mlp_out]
    vs = [v_w_in, v_w_att_out, v_w_pool_out, v_w_out, v_w_mlp_in, v_w_mlp_out]
    upd = {}
    for k, name in enumerate(names):
        res = _adamw(got[name], shards[k], ms[k][0], vs[k][0], "adamw_" + name, tr=512)
        upd[name] = [a[None] for a in res]

    as_rows = lambda a: a.reshape(4 * POOL_GW, POOL_GW)
    res = _adamw(got_grp, as_rows(w_pool_grp), as_rows(m_w_pool_grp), as_rows(v_w_pool_grp), "adamw_w_pool_grp", tr=2048)
    upd["w_pool_grp"] = [a.reshape(w_pool_grp.shape) for a in res]
    vectors = {"norm_mix_g": (norm_mix_g, m_norm_mix_g, v_norm_mix_g), "pool_scale": (pool_scale, m_pool_scale, v_pool_scale),
               "norm_mlp_g": (norm_mlp_g, m_norm_mlp_g, v_norm_mlp_g),
               "norm_final_g": (norm_final_g, m_norm_final_g, v_norm_final_g)}
    loss, vector_res = _adamw_vectors(*got_vectors, [tuple(a.reshape(1, -1) for a in p) for p in vectors.values()])
    for (name, (w, _, _)), res in zip(vectors.items(), vector_res):
        upd[name] = [a.reshape(w.shape) for a in res]

    order = ["norm_mix_g", "w_in", "w_att_out", "w_pool_grp", "pool_scale", "w_pool_out", "w_out", "norm_mlp_g",
             "w_mlp_in", "w_mlp_out", "norm_final_g"]
    out = [loss.reshape(()), dx[None]]
    for q in range(4):
        out += [upd[name][q] for name in order]
    return tuple(out)
```
